```python
import math
import jax, jax.numpy as jnp
from jax import lax
import numpy as np

D_MODEL = 1024
BATCH = 8
SEQ = 8192
DEPTH = 2

MEM_LEN = 256
EPS = 1e-6
MLA_HEADS = 8
Q_LORA = 384
KV_LORA = 256
D_NOPE = 64
D_ROPE = 32
D_QK = D_NOPE + D_ROPE
D_V = 64
MLA_WIDTH = MLA_HEADS * D_V
ROPE_THETA = 10000.0
Q_BLOCK = 128
SSM_GROUPS = 32
SSM_GROUP_CH = 16
SSM_WIDTH = SSM_GROUPS * SSM_GROUP_CH
SSM_STATE = 64
DT_MIN = 1e-3
DT_MAX = 1e-1
X_HEADS = 4
X_HEAD_DIM = 128
X_WIDTH = X_HEADS * X_HEAD_DIM
N_BRANCH = 3
D_FF = 2816
CONV_WIDTH = 3
IN_WIDTH = Q_LORA + KV_LORA + D_ROPE + SSM_WIDTH + X_WIDTH + N_BRANCH * D_MODEL

kernel_name = "hybrid_mla_s5_memxattn_convffn"


def rmsnorm(x, g):
    xf = x.astype(jnp.float32)
    y = xf * lax.rsqrt(jnp.mean(xf * xf, axis=-1, keepdims=True) + EPS)
    return (y * g.astype(jnp.float32)).astype(x.dtype)


def rope_tables(positions):
    inv_freq = ROPE_THETA ** (-jnp.arange(0, D_ROPE, 2, dtype=jnp.float32) / D_ROPE)
    ang = positions.astype(jnp.float32)[..., None] * inv_freq
    return jnp.cos(ang), jnp.sin(ang)


def apply_rope(x, cos, sin):
    xf = x.astype(jnp.float32)
    x1, x2 = jnp.split(xf, 2, axis=-1)
    return jnp.concatenate([x1 * cos - x2 * sin, x1 * sin + x2 * cos], axis=-1).astype(x.dtype)


def split_combined(proj):
    sizes = (Q_LORA, KV_LORA, D_ROPE, SSM_WIDTH, X_WIDTH)
    idx = [int(v) for v in np.cumsum(sizes)]
    return jnp.split(proj, idx, axis=-1)


def causal_block_attention(q, k, v):
    b, l, h, dq = q.shape
    nb = l // Q_BLOCK
    scale = dq ** -0.5
    qb = jnp.moveaxis(q.reshape(b, nb, Q_BLOCK, h, dq), 1, 0)
    kpos = jnp.arange(l)

    def one_block(args):
        i, qi = args
        s = jnp.einsum('bqhd,bkhd->bhqk', qi, k, preferred_element_type=jnp.float32) * scale
        qpos = i * Q_BLOCK + jnp.arange(Q_BLOCK)
        s = jnp.where(kpos[None, :] <= qpos[:, None], s, -jnp.inf)
        p = jax.nn.softmax(s, axis=-1)
        return jnp.einsum('bhqk,bkhd->bqhd', p.astype(v.dtype), v)

    out = lax.map(one_block, (jnp.arange(nb), qb))
    return jnp.moveaxis(out, 0, 1).reshape(b, l, h * v.shape[-1])


def mla_branch(c_q, c_kv, k_r, cos, sin, q_a_norm_g, w_q_b, kv_a_norm_g, w_kv_b, q_norm_g, k_norm_g):
    b, l, _ = c_q.shape
    q = (rmsnorm(c_q, q_a_norm_g) @ w_q_b).reshape(b, l, MLA_HEADS, D_QK)
    kv = (rmsnorm(c_kv, kv_a_norm_g) @ w_kv_b).reshape(b, l, MLA_HEADS, D_NOPE + D_V)
    k_nope, v = kv[..., :D_NOPE], kv[..., D_NOPE:]
    k_rope = jnp.broadcast_to(k_r[:, :, None, :], (b, l, MLA_HEADS, D_ROPE))
    k = jnp.concatenate([k_nope, k_rope], axis=-1)
    q = rmsnorm(q, q_norm_g)
    k = rmsnorm(k, k_norm_g)
    c4, s4 = cos[:, :, None, :], sin[:, :, None, :]
    q = jnp.concatenate([q[..., :D_NOPE], apply_rope(q[..., D_NOPE:], c4, s4)], axis=-1)
    k = jnp.concatenate([k[..., :D_NOPE], apply_rope(k[..., D_NOPE:], c4, s4)], axis=-1)
    return causal_block_attention(q, k, v)


def _ssm_combine(left, right):
    a1r, a1i, b1r, b1i = left
    a2r, a2i, b2r, b2i = right
    return (a2r * a1r - a2i * a1i,
            a2r * a1i + a2i * a1r,
            a2r * b1r - a2i * b1i + b2r,
            a2r * b1i + a2i * b1r + b2i)


def s5_branch(u, lam_re, lam_im, log_dt, b_re, b_im, c_re, c_im, d_skip, w_glu, b_glu):
    f32 = jnp.float32
    bsz, l, _ = u.shape
    uf = u.astype(f32).reshape(bsz, l, SSM_GROUPS, SSM_GROUP_CH)
    dt = jnp.exp(log_dt.astype(f32))[:, None]
    lr, li = lam_re.astype(f32), lam_im.astype(f32)
    mag = jnp.exp(lr * dt)
    a_re, a_im = mag * jnp.cos(li * dt), mag * jnp.sin(li * dt)
    den = lr * lr + li * li
    e_re, e_im = a_re - 1.0, a_im
    f_re = ((e_re * lr + e_im * li) / den)[..., None]
    f_im = ((e_im * lr - e_re * li) / den)[..., None]
    br, bi = b_re.astype(f32), b_im.astype(f32)
    bb_re = f_re * br - f_im * bi
    bb_im = f_re * bi + f_im * br
    bu_re = jnp.einsum('blgc,gnc->blgn', uf, bb_re)
    bu_im = jnp.einsum('blgc,gnc->blgn', uf, bb_im)
    a_re_t = jnp.broadcast_to(a_re, bu_re.shape)
    a_im_t = jnp.broadcast_to(a_im, bu_im.shape)
    _, _, s_re, s_im = lax.associative_scan(_ssm_combine, (a_re_t, a_im_t, bu_re, bu_im), axis=1)
    y = (jnp.einsum('blgn,gcn->blgc', s_re, c_re.astype(f32))
         - jnp.einsum('blgn,gcn->blgc', s_im, c_im.astype(f32))
         + d_skip.astype(f32) * uf)
    y = jax.nn.gelu(y.reshape(bsz, l, SSM_WIDTH)).astype(u.dtype)
    return y * jax.nn.sigmoid(y @ w_glu + b_glu)


def cross_branch(x_q, mem, mem_norm_g, w_mem_kv, xq_norm_g, xk_norm_g):
    b, l, _ = x_q.shape
    kv = rmsnorm(mem, mem_norm_g) @ w_mem_kv
    k = kv[..., :X_WIDTH].reshape(b, MEM_LEN, X_HEADS, X_HEAD_DIM)
    v = kv[..., X_WIDTH:].reshape(b, MEM_LEN, X_HEADS, X_HEAD_DIM)
    q = rmsnorm(x_q.reshape(b, l, X_HEADS, X_HEAD_DIM), xq_norm_g)
    k = rmsnorm(k, xk_norm_g)
    s = jnp.einsum('blhd,bmhd->bhlm', q, k, preferred_element_type=jnp.float32) * (X_HEAD_DIM ** -0.5)
    p = jax.nn.softmax(s, axis=-1)
    return jnp.einsum('bhlm,bmhd->blhd', p.astype(v.dtype), v).reshape(b, l, X_WIDTH)


def causal_dwconv(x, w, bias):
    c = x.shape[-1]
    y = lax.conv_general_dilated(x, w[:, None, :].astype(x.dtype), window_strides=(1,),
                                 padding=((CONV_WIDTH - 1, 0),),
                                 dimension_numbers=('NWC', 'WIO', 'NWC'),
                                 feature_group_count=c)
    return y + bias


def _fwd_setup_inputs(seed: int = 0) -> dict:
    key = jax.random.key(seed)
    ks = iter(jax.random.split(key, 48))
    f32 = jnp.float32
    L = DEPTH

    def nrm(shape, fan_in):
        return jax.random.normal(next(ks), shape, f32) * (fan_in ** -0.5)

    def gain(shape):
        return 1.0 + 0.02 * jax.random.normal(next(ks), shape, f32)

    def small(shape):
        return 0.01 * jax.random.normal(next(ks), shape, f32)

    x = jax.random.normal(next(ks), (BATCH, SEQ, D_MODEL), f32)
    mem = jax.random.normal(next(ks), (BATCH, MEM_LEN, D_MODEL), f32)
    offset = jax.random.randint(next(ks), (BATCH, 1), 0, 1024, dtype=jnp.int32)
    positions = offset + jnp.arange(SEQ, dtype=jnp.int32)[None, :]
    n_idx = jnp.arange(SSM_STATE, dtype=f32)
    lam_re = -0.5 + small((L, SSM_GROUPS, SSM_STATE))
    lam_im = math.pi * n_idx + small((L, SSM_GROUPS, SSM_STATE))
    log_dt = jax.random.uniform(next(ks), (L, SSM_GROUPS), f32, math.log(DT_MIN), math.log(DT_MAX))
    return {
        "x": x,
        "mem": mem,
        "positions": positions,
        "norm_mix_g": gain((L, D_MODEL)),
        "w_in": nrm((L, D_MODEL, IN_WIDTH), D_MODEL),
        "q_a_norm_g": gain((L, Q_LORA)),
        "w_q_b": nrm((L, Q_LORA, MLA_HEADS * D_QK), Q_LORA),
        "kv_a_norm_g": gain((L, KV_LORA)),
        "w_kv_b": nrm((L, KV_LORA, MLA_HEADS * (D_NOPE + D_V)), KV_LORA),
        "q_norm_g": gain((L, D_QK)),
        "k_norm_g": gain((L, D_QK)),
        "w_o_mla": nrm((L, MLA_WIDTH, D_MODEL), MLA_WIDTH),
        "ssm_lambda_re": lam_re,
        "ssm_lambda_im": lam_im,
        "ssm_log_dt": log_dt,
        "ssm_b_re": nrm((L, SSM_GROUPS, SSM_STATE, SSM_GROUP_CH), 2 * SSM_GROUP_CH),
        "ssm_b_im": nrm((L, SSM_GROUPS, SSM_STATE, SSM_GROUP_CH), 2 * SSM_GROUP_CH),
        "ssm_c_re": nrm((L, SSM_GROUPS, SSM_GROUP_CH, SSM_STATE), SSM_STATE),
        "ssm_c_im": nrm((L, SSM_GROUPS, SSM_GROUP_CH, SSM_STATE), SSM_STATE),
        "ssm_d": jax.random.normal(next(ks), (L, SSM_GROUPS, SSM_GROUP_CH), f32),
        "w_glu": nrm((L, SSM_WIDTH, SSM_WIDTH), SSM_WIDTH),
        "b_glu": small((L, SSM_WIDTH)),
        "w_o_ssm": nrm((L, SSM_WIDTH, D_MODEL), SSM_WIDTH),
        "mem_norm_g": gain((L, D_MODEL)),
        "w_mem_kv": nrm((L, D_MODEL, 2 * X_WIDTH), D_MODEL),
        "xq_norm_g": gain((L, X_HEAD_DIM)),
        "xk_norm_g": gain((L, X_HEAD_DIM)),
        "w_o_cross": nrm((L, X_WIDTH, D_MODEL), X_WIDTH),
        "b_gate": small((L, N_BRANCH * D_MODEL)),
        "w_out": nrm((L, D_MODEL, D_MODEL), D_MODEL),
        "norm_ffn_g": gain((L, D_MODEL)),
        "w_up": nrm((L, D_MODEL, 2 * D_FF), D_MODEL),
        "conv_w": nrm((L, CONV_WIDTH, 2 * D_FF), CONV_WIDTH),
        "conv_b": small((L, 2 * D_FF)),
        "w_down": nrm((L, D_FF, D_MODEL), D_FF),
    }


def _fwd_reference(x, mem, positions, norm_mix_g, w_in, q_a_norm_g, w_q_b, kv_a_norm_g, w_kv_b,
              q_norm_g, k_norm_g, w_o_mla, ssm_lambda_re, ssm_lambda_im, ssm_log_dt,
              ssm_b_re, ssm_b_im, ssm_c_re, ssm_c_im, ssm_d, w_glu, b_glu, w_o_ssm,
              mem_norm_g, w_mem_kv, xq_norm_g, xk_norm_g, w_o_cross, b_gate, w_out,
              norm_ffn_g, w_up, conv_w, conv_b, w_down):
    bsz, l, _ = x.shape
    cos, sin = rope_tables(positions)
    for i in range(DEPTH):
        h = rmsnorm(x, norm_mix_g[i])
        c_q, c_kv, k_r, u_ssm, x_q, gate_logits = split_combined(h @ w_in[i])
        y_a = mla_branch(c_q, c_kv, k_r, cos, sin, q_a_norm_g[i], w_q_b[i], kv_a_norm_g[i],
                         w_kv_b[i], q_norm_g[i], k_norm_g[i]) @ w_o_mla[i]
        y_b = s5_branch(u_ssm, ssm_lambda_re[i], ssm_lambda_im[i], ssm_log_dt[i], ssm_b_re[i],
                        ssm_b_im[i], ssm_c_re[i], ssm_c_im[i], ssm_d[i], w_glu[i], b_glu[i]) @ w_o_ssm[i]
        y_c = cross_branch(x_q, mem, mem_norm_g[i], w_mem_kv[i], xq_norm_g[i], xk_norm_g[i]) @ w_o_cross[i]
        gates = jax.nn.sigmoid(gate_logits + b_gate[i]).reshape(bsz, l, N_BRANCH, D_MODEL)
        merged = gates[:, :, 0] * y_a + gates[:, :, 1] * y_b + gates[:, :, 2] * y_c
        x = x + merged @ w_out[i]
        h2 = rmsnorm(x, norm_ffn_g[i])
        up = causal_dwconv(h2 @ w_up[i], conv_w[i], conv_b[i])
        g_ff, v_ff = up[..., :D_FF], up[..., D_FF:]
        x = x + (jax.nn.silu(g_ff) * v_ff) @ w_down[i]
    return x


import jax as _jax
import jax.numpy as _jnp

TWIN_FORMAT = 'train_step'
FWD_PARAMS = ['x', 'mem', 'positions', 'norm_mix_g', 'w_in', 'q_a_norm_g', 'w_q_b', 'kv_a_norm_g', 'w_kv_b', 'q_norm_g', 'k_norm_g', 'w_o_mla', 'ssm_lambda_re', 'ssm_lambda_im', 'ssm_log_dt', 'ssm_b_re', 'ssm_b_im', 'ssm_c_re', 'ssm_c_im', 'ssm_d', 'w_glu', 'b_glu', 'w_o_ssm', 'mem_norm_g', 'w_mem_kv', 'xq_norm_g', 'xk_norm_g', 'w_o_cross', 'b_gate', 'w_out', 'norm_ffn_g', 'w_up', 'conv_w', 'conv_b', 'w_down']
TWIN_WEIGHTS = ['norm_mix_g', 'w_in', 'q_a_norm_g', 'w_q_b', 'kv_a_norm_g', 'w_kv_b', 'q_norm_g', 'k_norm_g', 'w_o_mla', 'ssm_lambda_re', 'ssm_lambda_im', 'ssm_log_dt', 'ssm_b_re', 'ssm_b_im', 'ssm_c_re', 'ssm_c_im', 'ssm_d', 'w_glu', 'b_glu', 'w_o_ssm', 'mem_norm_g', 'w_mem_kv', 'xq_norm_g', 'xk_norm_g', 'w_o_cross', 'b_gate', 'w_out', 'norm_ffn_g', 'w_up', 'conv_w', 'conv_b', 'w_down']
TWIN_DIFF_INPUT = 'x'
TWIN_INPUTS = ['x', 'mem', 'positions', 'norm_mix_g', 'w_in', 'q_a_norm_g', 'w_q_b', 'kv_a_norm_g', 'w_kv_b', 'q_norm_g', 'k_norm_g', 'w_o_mla', 'ssm_lambda_re', 'ssm_lambda_im', 'ssm_log_dt', 'ssm_b_re', 'ssm_b_im', 'ssm_c_re', 'ssm_c_im', 'ssm_d', 'w_glu', 'b_glu', 'w_o_ssm', 'mem_norm_g', 'w_mem_kv', 'xq_norm_g', 'xk_norm_g', 'w_o_cross', 'b_gate', 'w_out', 'norm_ffn_g', 'w_up', 'conv_w', 'conv_b', 'w_down', 'loss_target', 'm_norm_mix_g', 'm_w_in', 'm_q_a_norm_g', 'm_w_q_b', 'm_kv_a_norm_g', 'm_w_kv_b', 'm_q_norm_g', 'm_k_norm_g', 'm_w_o_mla', 'm_ssm_lambda_re', 'm_ssm_lambda_im', 'm_ssm_log_dt', 'm_ssm_b_re', 'm_ssm_b_im', 'm_ssm_c_re', 'm_ssm_c_im', 'm_ssm_d', 'm_w_glu', 'm_b_glu', 'm_w_o_ssm', 'm_mem_norm_g', 'm_w_mem_kv', 'm_xq_norm_g', 'm_xk_norm_g', 'm_w_o_cross', 'm_b_gate', 'm_w_out', 'm_norm_ffn_g', 'm_w_up', 'm_conv_w', 'm_conv_b', 'm_w_down', 'v_norm_mix_g', 'v_w_in', 'v_q_a_norm_g', 'v_w_q_b', 'v_kv_a_norm_g', 'v_w_kv_b', 'v_q_norm_g', 'v_k_norm_g', 'v_w_o_mla', 'v_ssm_lambda_re', 'v_ssm_lambda_im', 'v_ssm_log_dt', 'v_ssm_b_re', 'v_ssm_b_im', 'v_ssm_c_re', 'v_ssm_c_im', 'v_ssm_d', 'v_w_glu', 'v_b_glu', 'v_w_o_ssm', 'v_mem_norm_g', 'v_w_mem_kv', 'v_xq_norm_g', 'v_xk_norm_g', 'v_w_o_cross', 'v_b_gate', 'v_w_out', 'v_norm_ffn_g', 'v_w_up', 'v_conv_w', 'v_conv_b', 'v_w_down']
TWIN_OUTPUTS = ['loss', 'grad_x', 'grad_norm_mix_g', 'grad_w_in', 'grad_q_a_norm_g', 'grad_w_q_b', 'grad_kv_a_norm_g', 'grad_w_kv_b', 'grad_q_norm_g', 'grad_k_norm_g', 'grad_w_o_mla', 'grad_ssm_lambda_re', 'grad_ssm_lambda_im', 'grad_ssm_log_dt', 'grad_ssm_b_re', 'grad_ssm_b_im', 'grad_ssm_c_re', 'grad_ssm_c_im', 'grad_ssm_d', 'grad_w_glu', 'grad_b_glu', 'grad_w_o_ssm', 'grad_mem_norm_g', 'grad_w_mem_kv', 'grad_xq_norm_g', 'grad_xk_norm_g', 'grad_w_o_cross', 'grad_b_gate', 'grad_w_out', 'grad_norm_ffn_g', 'grad_w_up', 'grad_conv_w', 'grad_conv_b', 'grad_w_down', 'delta_norm_mix_g', 'delta_w_in', 'delta_q_a_norm_g', 'delta_w_q_b', 'delta_kv_a_norm_g', 'delta_w_kv_b', 'delta_q_norm_g', 'delta_k_norm_g', 'delta_w_o_mla', 'delta_ssm_lambda_re', 'delta_ssm_lambda_im', 'delta_ssm_log_dt', 'delta_ssm_b_re', 'delta_ssm_b_im', 'delta_ssm_c_re', 'delta_ssm_c_im', 'delta_ssm_d', 'delta_w_glu', 'delta_b_glu', 'delta_w_o_ssm', 'delta_mem_norm_g', 'delta_w_mem_kv', 'delta_xq_norm_g', 'delta_xk_norm_g', 'delta_w_o_cross', 'delta_b_gate', 'delta_w_out', 'delta_norm_ffn_g', 'delta_w_up', 'delta_conv_w', 'delta_conv_b', 'delta_w_down', 'new_m_norm_mix_g', 'new_m_w_in', 'new_m_q_a_norm_g', 'new_m_w_q_b', 'new_m_kv_a_norm_g', 'new_m_w_kv_b', 'new_m_q_norm_g', 'new_m_k_norm_g', 'new_m_w_o_mla', 'new_m_ssm_lambda_re', 'new_m_ssm_lambda_im', 'new_m_ssm_log_dt', 'new_m_ssm_b_re', 'new_m_ssm_b_im', 'new_m_ssm_c_re', 'new_m_ssm_c_im', 'new_m_ssm_d', 'new_m_w_glu', 'new_m_b_glu', 'new_m_w_o_ssm', 'new_m_mem_norm_g', 'new_m_w_mem_kv', 'new_m_xq_norm_g', 'new_m_xk_norm_g', 'new_m_w_o_cross', 'new_m_b_gate', 'new_m_w_out', 'new_m_norm_ffn_g', 'new_m_w_up', 'new_m_conv_w', 'new_m_conv_b', 'new_m_w_down', 'new_v_norm_mix_g', 'new_v_w_in', 'new_v_q_a_norm_g', 'new_v_w_q_b', 'new_v_kv_a_norm_g', 'new_v_w_kv_b', 'new_v_q_norm_g', 'new_v_k_norm_g', 'new_v_w_o_mla', 'new_v_ssm_lambda_re', 'new_v_ssm_lambda_im', 'new_v_ssm_log_dt', 'new_v_ssm_b_re', 'new_v_ssm_b_im', 'new_v_ssm_c_re', 'new_v_ssm_c_im', 'new_v_ssm_d', 'new_v_w_glu', 'new_v_b_glu', 'new_v_w_o_ssm', 'new_v_mem_norm_g', 'new_v_w_mem_kv', 'new_v_xq_norm_g', 'new_v_xk_norm_g', 'new_v_w_o_cross', 'new_v_b_gate', 'new_v_w_out', 'new_v_norm_ffn_g', 'new_v_w_up', 'new_v_conv_w', 'new_v_conv_b', 'new_v_w_down']
TWIN_LEAF_KINDS = {'loss': 'loss', 'grad_x': 'grad_x', 'grad_norm_mix_g': 'grad_w', 'grad_w_in': 'grad_w', 'grad_q_a_norm_g': 'grad_w', 'grad_w_q_b': 'grad_w', 'grad_kv_a_norm_g': 'grad_w', 'grad_w_kv_b': 'grad_w', 'grad_q_norm_g': 'grad_w', 'grad_k_norm_g': 'grad_w', 'grad_w_o_mla': 'grad_w', 'grad_ssm_lambda_re': 'grad_w', 'grad_ssm_lambda_im': 'grad_w', 'grad_ssm_log_dt': 'grad_w', 'grad_ssm_b_re': 'grad_w', 'grad_ssm_b_im': 'grad_w', 'grad_ssm_c_re': 'grad_w', 'grad_ssm_c_im': 'grad_w', 'grad_ssm_d': 'grad_w', 'grad_w_glu': 'grad_w', 'grad_b_glu': 'grad_w', 'grad_w_o_ssm': 'grad_w', 'grad_mem_norm_g': 'grad_w', 'grad_w_mem_kv': 'grad_w', 'grad_xq_norm_g': 'grad_w', 'grad_xk_norm_g': 'grad_w', 'grad_w_o_cross': 'grad_w', 'grad_b_gate': 'grad_w', 'grad_w_out': 'grad_w', 'grad_norm_ffn_g': 'grad_w', 'grad_w_up': 'grad_w', 'grad_conv_w': 'grad_w', 'grad_conv_b': 'grad_w', 'grad_w_down': 'grad_w', 'delta_norm_mix_g': 'delta_w', 'delta_w_in': 'delta_w', 'delta_q_a_norm_g': 'delta_w', 'delta_w_q_b': 'delta_w', 'delta_kv_a_norm_g': 'delta_w', 'delta_w_kv_b': 'delta_w', 'delta_q_norm_g': 'delta_w', 'delta_k_norm_g': 'delta_w', 'delta_w_o_mla': 'delta_w', 'delta_ssm_lambda_re': 'delta_w', 'delta_ssm_lambda_im': 'delta_w', 'delta_ssm_log_dt': 'delta_w', 'delta_ssm_b_re': 'delta_w', 'delta_ssm_b_im': 'delta_w', 'delta_ssm_c_re': 'delta_w', 'delta_ssm_c_im': 'delta_w', 'delta_ssm_d': 'delta_w', 'delta_w_glu': 'delta_w', 'delta_b_glu': 'delta_w', 'delta_w_o_ssm': 'delta_w', 'delta_mem_norm_g': 'delta_w', 'delta_w_mem_kv': 'delta_w', 'delta_xq_norm_g': 'delta_w', 'delta_xk_norm_g': 'delta_w', 'delta_w_o_cross': 'delta_w', 'delta_b_gate': 'delta_w', 'delta_w_out': 'delta_w', 'delta_norm_ffn_g': 'delta_w', 'delta_w_up': 'delta_w', 'delta_conv_w': 'delta_w', 'delta_conv_b': 'delta_w', 'delta_w_down': 'delta_w', 'new_m_norm_mix_g': 'new_m', 'new_m_w_in': 'new_m', 'new_m_q_a_norm_g': 'new_m', 'new_m_w_q_b': 'new_m', 'new_m_kv_a_norm_g': 'new_m', 'new_m_w_kv_b': 'new_m', 'new_m_q_norm_g': 'new_m', 'new_m_k_norm_g': 'new_m', 'new_m_w_o_mla': 'new_m', 'new_m_ssm_lambda_re': 'new_m', 'new_m_ssm_lambda_im': 'new_m', 'new_m_ssm_log_dt': 'new_m', 'new_m_ssm_b_re': 'new_m', 'new_m_ssm_b_im': 'new_m', 'new_m_ssm_c_re': 'new_m', 'new_m_ssm_c_im': 'new_m', 'new_m_ssm_d': 'new_m', 'new_m_w_glu': 'new_m', 'new_m_b_glu': 'new_m', 'new_m_w_o_ssm': 'new_m', 'new_m_mem_norm_g': 'new_m', 'new_m_w_mem_kv': 'new_m', 'new_m_xq_norm_g': 'new_m', 'new_m_xk_norm_g': 'new_m', 'new_m_w_o_cross': 'new_m', 'new_m_b_gate': 'new_m', 'new_m_w_out': 'new_m', 'new_m_norm_ffn_g': 'new_m', 'new_m_w_up': 'new_m', 'new_m_conv_w': 'new_m', 'new_m_conv_b': 'new_m', 'new_m_w_down': 'new_m', 'new_v_norm_mix_g': 'new_v', 'new_v_w_in': 'new_v', 'new_v_q_a_norm_g': 'new_v', 'new_v_w_q_b': 'new_v', 'new_v_kv_a_norm_g': 'new_v', 'new_v_w_kv_b': 'new_v', 'new_v_q_norm_g': 'new_v', 'new_v_k_norm_g': 'new_v', 'new_v_w_o_mla': 'new_v', 'new_v_ssm_lambda_re': 'new_v', 'new_v_ssm_lambda_im': 'new_v', 'new_v_ssm_log_dt': 'new_v', 'new_v_ssm_b_re': 'new_v', 'new_v_ssm_b_im': 'new_v', 'new_v_ssm_c_re': 'new_v', 'new_v_ssm_c_im': 'new_v', 'new_v_ssm_d': 'new_v', 'new_v_w_glu': 'new_v', 'new_v_b_glu': 'new_v', 'new_v_w_o_ssm': 'new_v', 'new_v_mem_norm_g': 'new_v', 'new_v_w_mem_kv': 'new_v', 'new_v_xq_norm_g': 'new_v', 'new_v_xk_norm_g': 'new_v', 'new_v_w_o_cross': 'new_v', 'new_v_b_gate': 'new_v', 'new_v_w_out': 'new_v', 'new_v_norm_ffn_g': 'new_v', 'new_v_w_up': 'new_v', 'new_v_conv_w': 'new_v', 'new_v_conv_b': 'new_v', 'new_v_w_down': 'new_v'}


def _forward(args):
    return _fwd_reference(*[args[k] for k in FWD_PARAMS])


def _output_shape():
    def fwd():
        inp = _fwd_setup_inputs(0)
        return _fwd_reference(*[inp[k] for k in FWD_PARAMS])
    out = _jax.eval_shape(fwd)
    return out.shape, out.dtype

N_MICROBATCH = 1
ADAM_LR = 0.001
ADAM_B1 = 0.9
ADAM_B2 = 0.999
ADAM_EPS = 1e-08
ADAM_WD = 0.01
ADAM_STEP = 10
PER_EXAMPLE_BATCH_AXIS = {'x': 0, 'mem': 0, 'positions': 0, 'loss_target': 0}
SHARED_INPUTS = []
_WEIGHT_DTYPES = {'norm_mix_g': _jnp.float32, 'w_in': _jnp.float32, 'q_a_norm_g': _jnp.float32, 'w_q_b': _jnp.float32, 'kv_a_norm_g': _jnp.float32, 'w_kv_b': _jnp.float32, 'q_norm_g': _jnp.float32, 'k_norm_g': _jnp.float32, 'w_o_mla': _jnp.float32, 'ssm_lambda_re': _jnp.float32, 'ssm_lambda_im': _jnp.float32, 'ssm_log_dt': _jnp.float32, 'ssm_b_re': _jnp.float32, 'ssm_b_im': _jnp.float32, 'ssm_c_re': _jnp.float32, 'ssm_c_im': _jnp.float32, 'ssm_d': _jnp.float32, 'w_glu': _jnp.float32, 'b_glu': _jnp.float32, 'w_o_ssm': _jnp.float32, 'mem_norm_g': _jnp.float32, 'w_mem_kv': _jnp.float32, 'xq_norm_g': _jnp.float32, 'xk_norm_g': _jnp.float32, 'w_o_cross': _jnp.float32, 'b_gate': _jnp.float32, 'w_out': _jnp.float32, 'norm_ffn_g': _jnp.float32, 'w_up': _jnp.float32, 'conv_w': _jnp.float32, 'conv_b': _jnp.float32, 'w_down': _jnp.float32}
MOMENT_SCALE = {'norm_mix_g': 2.624274e+00, 'w_in': 1.988901e-01, 'q_a_norm_g': 1.385353e-01, 'w_q_b': 9.533761e-02, 'kv_a_norm_g': 8.555604e-01, 'w_kv_b': 2.644195e-01, 'q_norm_g': 1.008650e+00, 'k_norm_g': 1.012443e+00, 'w_o_mla': 2.814830e-01, 'ssm_lambda_re': 4.879993e-02, 'ssm_lambda_im': 6.095728e-02, 'ssm_log_dt': 1.204136e+01, 'ssm_b_re': 3.108281e-02, 'ssm_b_im': 2.432357e-02, 'ssm_c_re': 4.682979e-02, 'ssm_c_im': 3.588971e-02, 'ssm_d': 6.254453e+00, 'w_glu': 1.319021e+00, 'b_glu': 3.738410e+00, 'w_o_ssm': 1.997831e+00, 'mem_norm_g': 2.813056e-01, 'w_mem_kv': 2.295358e-01, 'xq_norm_g': 1.367366e+00, 'xk_norm_g': 1.368426e+00, 'w_o_cross': 2.334656e-01, 'b_gate': 6.731707e-01, 'w_out': 1.957891e+00, 'norm_ffn_g': 5.185700e+01, 'w_up': 7.278815e-01, 'conv_w': 7.218749e+00, 'conv_b': 6.600159e+00, 'w_down': 6.971967e-01}


def _to_microbatches(a, axis):
    t = _jnp.moveaxis(a, axis, 0)
    t = t.reshape((N_MICROBATCH, t.shape[0] // N_MICROBATCH) + t.shape[1:])
    return _jnp.moveaxis(t, 1, axis + 1)


def setup_inputs(seed: int = 0) -> dict:
    inp = _fwd_setup_inputs(seed)
    key = _jax.random.fold_in(_jax.random.key(seed), 7919)
    shape, _ = _output_shape()
    out = dict(inp)
    out["loss_target"] = _jax.random.normal(_jax.random.fold_in(key, 0), shape, _jnp.float32)
    for i, name in enumerate(TWIN_WEIGHTS):
        w = inp[name].astype(_jnp.float32)
        if MOMENT_SCALE is None:
            s = _jnp.sqrt(_jnp.mean(_jnp.square(w)) + 1e-30)
        else:
            s = MOMENT_SCALE[name]
        km, kv = _jax.random.split(_jax.random.fold_in(key, i + 1))
        out[name] = w
        out["m_" + name] = s * _jax.random.normal(km, w.shape, _jnp.float32)
        out["v_" + name] = (s * s) * _jax.random.uniform(kv, w.shape, _jnp.float32, 0.5, 1.5)
    if N_MICROBATCH > 1:
        for name, axis in PER_EXAMPLE_BATCH_AXIS.items():
            out[name] = _to_microbatches(out[name], axis)
    return {'x': out['x'], 'mem': out['mem'], 'positions': out['positions'], 'norm_mix_g': out['norm_mix_g'], 'w_in': out['w_in'], 'q_a_norm_g': out['q_a_norm_g'], 'w_q_b': out['w_q_b'], 'kv_a_norm_g': out['kv_a_norm_g'], 'w_kv_b': out['w_kv_b'], 'q_norm_g': out['q_norm_g'], 'k_norm_g': out['k_norm_g'], 'w_o_mla': out['w_o_mla'], 'ssm_lambda_re': out['ssm_lambda_re'], 'ssm_lambda_im': out['ssm_lambda_im'], 'ssm_log_dt': out['ssm_log_dt'], 'ssm_b_re': out['ssm_b_re'], 'ssm_b_im': out['ssm_b_im'], 'ssm_c_re': out['ssm_c_re'], 'ssm_c_im': out['ssm_c_im'], 'ssm_d': out['ssm_d'], 'w_glu': out['w_glu'], 'b_glu': out['b_glu'], 'w_o_ssm': out['w_o_ssm'], 'mem_norm_g': out['mem_norm_g'], 'w_mem_kv': out['w_mem_kv'], 'xq_norm_g': out['xq_norm_g'], 'xk_norm_g': out['xk_norm_g'], 'w_o_cross': out['w_o_cross'], 'b_gate': out['b_gate'], 'w_out': out['w_out'], 'norm_ffn_g': out['norm_ffn_g'], 'w_up': out['w_up'], 'conv_w': out['conv_w'], 'conv_b': out['conv_b'], 'w_down': out['w_down'], 'loss_target': out['loss_target'], 'm_norm_mix_g': out['m_norm_mix_g'], 'm_w_in': out['m_w_in'], 'm_q_a_norm_g': out['m_q_a_norm_g'], 'm_w_q_b': out['m_w_q_b'], 'm_kv_a_norm_g': out['m_kv_a_norm_g'], 'm_w_kv_b': out['m_w_kv_b'], 'm_q_norm_g': out['m_q_norm_g'], 'm_k_norm_g': out['m_k_norm_g'], 'm_w_o_mla': out['m_w_o_mla'], 'm_ssm_lambda_re': out['m_ssm_lambda_re'], 'm_ssm_lambda_im': out['m_ssm_lambda_im'], 'm_ssm_log_dt': out['m_ssm_log_dt'], 'm_ssm_b_re': out['m_ssm_b_re'], 'm_ssm_b_im': out['m_ssm_b_im'], 'm_ssm_c_re': out['m_ssm_c_re'], 'm_ssm_c_im': out['m_ssm_c_im'], 'm_ssm_d': out['m_ssm_d'], 'm_w_glu': out['m_w_glu'], 'm_b_glu': out['m_b_glu'], 'm_w_o_ssm': out['m_w_o_ssm'], 'm_mem_norm_g': out['m_mem_norm_g'], 'm_w_mem_kv': out['m_w_mem_kv'], 'm_xq_norm_g': out['m_xq_norm_g'], 'm_xk_norm_g': out['m_xk_norm_g'], 'm_w_o_cross': out['m_w_o_cross'], 'm_b_gate': out['m_b_gate'], 'm_w_out': out['m_w_out'], 'm_norm_ffn_g': out['m_norm_ffn_g'], 'm_w_up': out['m_w_up'], 'm_conv_w': out['m_conv_w'], 'm_conv_b': out['m_conv_b'], 'm_w_down': out['m_w_down'], 'v_norm_mix_g': out['v_norm_mix_g'], 'v_w_in': out['v_w_in'], 'v_q_a_norm_g': out['v_q_a_norm_g'], 'v_w_q_b': out['v_w_q_b'], 'v_kv_a_norm_g': out['v_kv_a_norm_g'], 'v_w_kv_b': out['v_w_kv_b'], 'v_q_norm_g': out['v_q_norm_g'], 'v_k_norm_g': out['v_k_norm_g'], 'v_w_o_mla': out['v_w_o_mla'], 'v_ssm_lambda_re': out['v_ssm_lambda_re'], 'v_ssm_lambda_im': out['v_ssm_lambda_im'], 'v_ssm_log_dt': out['v_ssm_log_dt'], 'v_ssm_b_re': out['v_ssm_b_re'], 'v_ssm_b_im': out['v_ssm_b_im'], 'v_ssm_c_re': out['v_ssm_c_re'], 'v_ssm_c_im': out['v_ssm_c_im'], 'v_ssm_d': out['v_ssm_d'], 'v_w_glu': out['v_w_glu'], 'v_b_glu': out['v_b_glu'], 'v_w_o_ssm': out['v_w_o_ssm'], 'v_mem_norm_g': out['v_mem_norm_g'], 'v_w_mem_kv': out['v_w_mem_kv'], 'v_xq_norm_g': out['v_xq_norm_g'], 'v_xk_norm_g': out['v_xk_norm_g'], 'v_w_o_cross': out['v_w_o_cross'], 'v_b_gate': out['v_b_gate'], 'v_w_out': out['v_w_out'], 'v_norm_ffn_g': out['v_norm_ffn_g'], 'v_w_up': out['v_w_up'], 'v_conv_w': out['v_conv_w'], 'v_conv_b': out['v_conv_b'], 'v_w_down': out['v_w_down']}


def _loss(weights, diff, rest, loss_target):
    with _jax.named_scope("forward"):
        args = {**rest, TWIN_DIFF_INPUT: diff, **{k: w.astype(_WEIGHT_DTYPES[k]) for k, w in weights.items()}}
        y = _forward(args)
    with _jax.named_scope("loss_head"):
        err = _jnp.square(y.astype(_jnp.float32) - loss_target)
        return 0.5 * _jnp.sum(_jnp.mean(err, axis=-1)) if err.ndim else 0.5 * err


def _adamw(w, g, m, v):
    m = ADAM_B1 * m + (1.0 - ADAM_B1) * g
    v = ADAM_B2 * v + (1.0 - ADAM_B2) * _jnp.square(g)
    m_hat = m / (1.0 - ADAM_B1 ** ADAM_STEP)
    v_hat = v / (1.0 - ADAM_B2 ** ADAM_STEP)
    delta = -ADAM_LR * (m_hat / (_jnp.sqrt(v_hat) + ADAM_EPS) + ADAM_WD * w)
    return delta, m, v


def reference(x, mem, positions, norm_mix_g, w_in, q_a_norm_g, w_q_b, kv_a_norm_g, w_kv_b, q_norm_g, k_norm_g, w_o_mla, ssm_lambda_re, ssm_lambda_im, ssm_log_dt, ssm_b_re, ssm_b_im, ssm_c_re, ssm_c_im, ssm_d, w_glu, b_glu, w_o_ssm, mem_norm_g, w_mem_kv, xq_norm_g, xk_norm_g, w_o_cross, b_gate, w_out, norm_ffn_g, w_up, conv_w, conv_b, w_down, loss_target, m_norm_mix_g, m_w_in, m_q_a_norm_g, m_w_q_b, m_kv_a_norm_g, m_w_kv_b, m_q_norm_g, m_k_norm_g, m_w_o_mla, m_ssm_lambda_re, m_ssm_lambda_im, m_ssm_log_dt, m_ssm_b_re, m_ssm_b_im, m_ssm_c_re, m_ssm_c_im, m_ssm_d, m_w_glu, m_b_glu, m_w_o_ssm, m_mem_norm_g, m_w_mem_kv, m_xq_norm_g, m_xk_norm_g, m_w_o_cross, m_b_gate, m_w_out, m_norm_ffn_g, m_w_up, m_conv_w, m_conv_b, m_w_down, v_norm_mix_g, v_w_in, v_q_a_norm_g, v_w_q_b, v_kv_a_norm_g, v_w_kv_b, v_q_norm_g, v_k_norm_g, v_w_o_mla, v_ssm_lambda_re, v_ssm_lambda_im, v_ssm_log_dt, v_ssm_b_re, v_ssm_b_im, v_ssm_c_re, v_ssm_c_im, v_ssm_d, v_w_glu, v_b_glu, v_w_o_ssm, v_mem_norm_g, v_w_mem_kv, v_xq_norm_g, v_xk_norm_g, v_w_o_cross, v_b_gate, v_w_out, v_norm_ffn_g, v_w_up, v_conv_w, v_conv_b, v_w_down):
    given = dict(x=x, mem=mem, positions=positions, norm_mix_g=norm_mix_g, w_in=w_in, q_a_norm_g=q_a_norm_g, w_q_b=w_q_b, kv_a_norm_g=kv_a_norm_g, w_kv_b=w_kv_b, q_norm_g=q_norm_g, k_norm_g=k_norm_g, w_o_mla=w_o_mla, ssm_lambda_re=ssm_lambda_re, ssm_lambda_im=ssm_lambda_im, ssm_log_dt=ssm_log_dt, ssm_b_re=ssm_b_re, ssm_b_im=ssm_b_im, ssm_c_re=ssm_c_re, ssm_c_im=ssm_c_im, ssm_d=ssm_d, w_glu=w_glu, b_glu=b_glu, w_o_ssm=w_o_ssm, mem_norm_g=mem_norm_g, w_mem_kv=w_mem_kv, xq_norm_g=xq_norm_g, xk_norm_g=xk_norm_g, w_o_cross=w_o_cross, b_gate=b_gate, w_out=w_out, norm_ffn_g=norm_ffn_g, w_up=w_up, conv_w=conv_w, conv_b=conv_b, w_down=w_down, loss_target=loss_target, m_norm_mix_g=m_norm_mix_g, m_w_in=m_w_in, m_q_a_norm_g=m_q_a_norm_g, m_w_q_b=m_w_q_b, m_kv_a_norm_g=m_kv_a_norm_g, m_w_kv_b=m_w_kv_b, m_q_norm_g=m_q_norm_g, m_k_norm_g=m_k_norm_g, m_w_o_mla=m_w_o_mla, m_ssm_lambda_re=m_ssm_lambda_re, m_ssm_lambda_im=m_ssm_lambda_im, m_ssm_log_dt=m_ssm_log_dt, m_ssm_b_re=m_ssm_b_re, m_ssm_b_im=m_ssm_b_im, m_ssm_c_re=m_ssm_c_re, m_ssm_c_im=m_ssm_c_im, m_ssm_d=m_ssm_d, m_w_glu=m_w_glu, m_b_glu=m_b_glu, m_w_o_ssm=m_w_o_ssm, m_mem_norm_g=m_mem_norm_g, m_w_mem_kv=m_w_mem_kv, m_xq_norm_g=m_xq_norm_g, m_xk_norm_g=m_xk_norm_g, m_w_o_cross=m_w_o_cross, m_b_gate=m_b_gate, m_w_out=m_w_out, m_norm_ffn_g=m_norm_ffn_g, m_w_up=m_w_up, m_conv_w=m_conv_w, m_conv_b=m_conv_b, m_w_down=m_w_down, v_norm_mix_g=v_norm_mix_g, v_w_in=v_w_in, v_q_a_norm_g=v_q_a_norm_g, v_w_q_b=v_w_q_b, v_kv_a_norm_g=v_kv_a_norm_g, v_w_kv_b=v_w_kv_b, v_q_norm_g=v_q_norm_g, v_k_norm_g=v_k_norm_g, v_w_o_mla=v_w_o_mla, v_ssm_lambda_re=v_ssm_lambda_re, v_ssm_lambda_im=v_ssm_lambda_im, v_ssm_log_dt=v_ssm_log_dt, v_ssm_b_re=v_ssm_b_re, v_ssm_b_im=v_ssm_b_im, v_ssm_c_re=v_ssm_c_re, v_ssm_c_im=v_ssm_c_im, v_ssm_d=v_ssm_d, v_w_glu=v_w_glu, v_b_glu=v_b_glu, v_w_o_ssm=v_w_o_ssm, v_mem_norm_g=v_mem_norm_g, v_w_mem_kv=v_w_mem_kv, v_xq_norm_g=v_xq_norm_g, v_xk_norm_g=v_xk_norm_g, v_w_o_cross=v_w_o_cross, v_b_gate=v_b_gate, v_w_out=v_w_out, v_norm_ffn_g=v_norm_ffn_g, v_w_up=v_w_up, v_conv_w=v_conv_w, v_conv_b=v_conv_b, v_w_down=v_w_down)
    weights = {n: given[n] for n in TWIN_WEIGHTS}
    shared = {n: given[n] for n in SHARED_INPUTS}
    per_example = {n: given[n] for n in ['x', 'mem', 'positions']}
    grad_fn = _jax.value_and_grad(_loss, argnums=(0, 1))

    def one_microbatch(ex, loss_target):
        ex = dict(ex)
        diff = ex.pop(TWIN_DIFF_INPUT)
        return grad_fn(weights, diff, {**shared, **ex}, loss_target)

    if N_MICROBATCH == 1:
        loss, (grad_w, grad_x) = one_microbatch(per_example, given["loss_target"])
    else:
        def body(carry, xs):
            loss_sum, grad_sum = carry
            l_k, (gw_k, gx_k) = one_microbatch(xs[0], xs[1])
            with _jax.named_scope("update"):
                return (loss_sum + l_k, _jax.tree.map(_jnp.add, grad_sum, gw_k)), gx_k

        init = (_jnp.zeros((), _jnp.float32), _jax.tree.map(_jnp.zeros_like, weights))
        (loss, grad_w), grad_x = _jax.lax.scan(body, init, (per_example, given["loss_target"]))
    with _jax.named_scope("update"):
        delta_w, new_m, new_v = {}, {}, {}
        for n in TWIN_WEIGHTS:
            delta_w[n], new_m[n], new_v[n] = _adamw(weights[n], grad_w[n], given["m_" + n], given["v_" + n])
    return (loss, grad_x, *[grad_w[n] for n in TWIN_WEIGHTS], *[delta_w[n] for n in TWIN_WEIGHTS],
            *[new_m[n] for n in TWIN_WEIGHTS], *[new_v[n] for n in TWIN_WEIGHTS])
```

```python
import functools
import math

import numpy as np
import jax
import jax.numpy as jnp
from jax import lax
from jax.experimental import pallas as pl
from jax.experimental.pallas import tpu as pltpu

F32 = jnp.float32
BF16 = jnp.bfloat16
MESH = pl.DeviceIdType.MESH

DEPTH = 2
D_MODEL = 1024
EPS = 1e-6
MLA_HEADS = 8
Q_LORA = 384
KV_LORA = 256
D_NOPE = 64
D_ROPE = 32
D_QK = D_NOPE + D_ROPE
D_V = 64
HEAD_PAD = 128
MLA_PAD = MLA_HEADS * HEAD_PAD
ROPE_THETA = 10000.0
SSM_GROUPS = 32
SSM_GROUP_CH = 16
SSM_WIDTH = 512
SSM_STATE = 64
SSM_LANES = SSM_GROUPS * SSM_STATE
SSM_JB = 4
X_HEADS = 4
X_HEAD_DIM = 128
X_WIDTH = 512
D_FF = 2816
SMALL_W = Q_LORA + KV_LORA + HEAD_PAD
SCAN_SEGS = 32
LANE = 128
NEG = -1e30

ADAM_LR = 0.001
ADAM_B1 = 0.9
ADAM_B2 = 0.999
ADAM_EPS = 1e-08
ADAM_WD = 0.01
ADAM_STEP = 10

WEIGHTS = ['norm_mix_g', 'w_in', 'q_a_norm_g', 'w_q_b', 'kv_a_norm_g', 'w_kv_b', 'q_norm_g', 'k_norm_g', 'w_o_mla',
           'ssm_lambda_re', 'ssm_lambda_im', 'ssm_log_dt', 'ssm_b_re', 'ssm_b_im', 'ssm_c_re', 'ssm_c_im', 'ssm_d',
           'w_glu', 'b_glu', 'w_o_ssm', 'mem_norm_g', 'w_mem_kv', 'xq_norm_g', 'xk_norm_g', 'w_o_cross', 'b_gate',
           'w_out', 'norm_ffn_g', 'w_up', 'conv_w', 'conv_b', 'w_down']
SHARD_AXIS = {'w_in': 2, 'w_q_b': 2, 'w_kv_b': 2, 'w_o_mla': 2, 'w_glu': 1, 'w_o_ssm': 2, 'w_mem_kv': 1,
              'w_o_cross': 2, 'w_out': 1, 'w_up': 2, 'conv_w': 2, 'w_down': 1}
SHARDED = [n for n in WEIGHTS if n in SHARD_AXIS]
GATHER_BF16 = [n for n in SHARDED if n != 'conv_w']
SMALL = [n for n in WEIGHTS if n not in SHARD_AXIS]


def _bf(v):
    return v.astype(BF16)


def _mm(a, b):
    return jnp.dot(_bf(a), _bf(b), preferred_element_type=F32)


def _mm_nt(a, b):
    return lax.dot_general(_bf(a), _bf(b), (((1,), (1,)), ((), ())), preferred_element_type=F32)


def _mm_tn(a, b):
    return lax.dot_general(_bf(a), _bf(b), (((0,), (0,)), ((), ())), preferred_element_type=F32)


def _rms(v, g, n):
    ms = jnp.sum(v * v, axis=-1, keepdims=True) * (1.0 / n)
    return (v * lax.rsqrt(ms + EPS)) * g


def _head_rms(v, g, heads, n):
    return jnp.concatenate([_rms(v[:, h * LANE:(h + 1) * LANE], g, n) for h in range(heads)], axis=-1)


def _rope(v, c, s1, s2):
    return v * c + pltpu.roll(v, LANE - 16, 1) * s1 + pltpu.roll(v, 16, 1) * s2


def _rope_t(g, c, s1, s2):
    return g * c + pltpu.roll(g * s1, 16, 1) + pltpu.roll(g * s2, LANE - 16, 1)


def _heads(fn, v, heads, *tabs):
    return jnp.concatenate([fn(v[:, h * LANE:(h + 1) * LANE], *tabs) for h in range(heads)], axis=-1)


def _gelu(y):
    return y * (0.5 * (1.0 + jnp.tanh(math.sqrt(2.0 / math.pi) * (y + 0.044715 * (y * y * y)))))


def _silu(g):
    return g * jax.nn.sigmoid(g)


def _colsum(v):
    return jnp.sum(v, axis=0, keepdims=True)


def _row_select(rows, n):
    rid = lax.broadcasted_iota(jnp.int32, (n, rows[0].shape[-1]), 0)
    out = jnp.zeros((n, rows[0].shape[-1]), F32)
    for k, r in enumerate(rows):
        out = jnp.where(rid == k, jnp.broadcast_to(r, out.shape), out)
    return out


def _params(sem, vmem_mb):
    return pltpu.CompilerParams(dimension_semantics=sem, vmem_limit_bytes=vmem_mb * 1024 * 1024)


def _rt(tm, w, cb=0):
    return pl.BlockSpec((tm, w), lambda i: (i, cb))


def _full(shape):
    nd = len(shape)
    return pl.BlockSpec(tuple(shape), lambda i: (0,) * nd)


def _rows(fn, *, name, n, ins, in_specs, outs, out_specs, n_acc=0, vmem=48):
    n_in = len(ins)
    n_out = len(outs)

    def body(*refs):
        i = pl.program_id(0)
        res = fn(i, *[r[...] for r in refs[:n_in]])
        if not isinstance(res, (tuple, list)):
            res = (res,)
        assert len(res) == n_out, (name, len(res), n_out)
        for k, (r, v) in enumerate(zip(refs[n_in:], res)):
            if k < n_out - n_acc:
                r[...] = v.astype(r.dtype)
            else:
                @pl.when(i == 0)
                def _():
                    r[...] = v

                @pl.when(i > 0)
                def _():
                    r[...] += v

    return pl.pallas_call(
        body, name=name, grid=(n,), in_specs=list(in_specs), out_specs=tuple(out_specs), out_shape=tuple(outs),
        compiler_params=_params(("arbitrary",), vmem))(*ins)


def _sds(shape, dtype=F32):
    return jax.ShapeDtypeStruct(tuple(shape), dtype)


def _tile_n(n, cap=1024):
    best = None
    for t in range(LANE, min(n, cap) + 1, LANE):
        if n % t == 0:
            best = t
    if best is None or n <= 1408:
        return n
    return best


def _matmul(name, pairs, m, n, *, nt=False, rms_gain=None, resid=None, out_dtype=F32, tm=512, vmem=56):
    tm = min(tm, m)
    tn = _tile_n(n)
    ks = [a.shape[1] for a, _ in pairs]
    np_ = len(pairs)

    def body(*refs):
        a_refs = refs[:np_]
        b_refs = refs[np_:2 * np_]
        k = 2 * np_
        g_ref = None
        r_ref = None
        if rms_gain is not None:
            g_ref = refs[k]
            k += 1
        if resid is not None:
            r_ref = refs[k]
            k += 1
        o_ref = refs[k]
        scr = refs[k + 1:]
        j = pl.program_id(1)

        @pl.when(j == 0)
        def _():
            for p in range(np_):
                a = a_refs[p][...]
                if p == 0 and g_ref is not None:
                    a = _rms(a.astype(F32), g_ref[...], ks[0])
                scr[p][...] = a.astype(BF16)

        acc = None
        for p in range(np_):
            b = b_refs[p][...].astype(BF16)
            if nt:
                t = lax.dot_general(scr[p][...], b, (((1,), (1,)), ((), ())), preferred_element_type=F32)
            else:
                t = jnp.dot(scr[p][...], b, preferred_element_type=F32)
            acc = t if acc is None else acc + t
        if r_ref is not None:
            acc = acc + r_ref[...]
        o_ref[...] = acc.astype(o_ref.dtype)

    in_specs = [pl.BlockSpec((tm, kk), lambda i, j: (i, 0)) for kk in ks]
    if nt:
        in_specs += [pl.BlockSpec((tn, kk), lambda i, j: (j, 0)) for kk in ks]
    else:
        in_specs += [pl.BlockSpec((kk, tn), lambda i, j: (0, j)) for kk in ks]
    ins = [a for a, _ in pairs] + [b for _, b in pairs]
    if rms_gain is not None:
        in_specs.append(pl.BlockSpec((1, ks[0]), lambda i, j: (0, 0)))
        ins.append(rms_gain)
    if resid is not None:
        in_specs.append(pl.BlockSpec((tm, tn), lambda i, j: (i, j)))
        ins.append(resid)
    return pl.pallas_call(
        body, name=name, grid=(m // tm, n // tn), in_specs=in_specs,
        out_specs=pl.BlockSpec((tm, tn), lambda i, j: (i, j)), out_shape=_sds((m, n), out_dtype),
        scratch_shapes=[pltpu.VMEM((tm, kk), BF16) for kk in ks],
        compiler_params=_params(("arbitrary", "arbitrary"), vmem))(*ins)


def _matmul_tn(name, a, b, *, rms_gain=None, tl=512, vmem=56):
    l, ka = a.shape
    n = b.shape[1]
    tl = min(tl, l)
    tn = _tile_n(n, 512)

    def body(*refs):
        if rms_gain is not None:
            a_ref, b_ref, g_ref, o_ref = refs
        else:
            a_ref, b_ref, o_ref = refs
        t = pl.program_id(1)
        av = a_ref[...]
        if rms_gain is not None:
            av = _rms(av.astype(F32), g_ref[...], ka)
        v = _mm_tn(av, b_ref[...])

        @pl.when(t == 0)
        def _():
            o_ref[...] = v

        @pl.when(t > 0)
        def _():
            o_ref[...] += v

    in_specs = [pl.BlockSpec((tl, ka), lambda j, t: (t, 0)), pl.BlockSpec((tl, tn), lambda j, t: (t, j))]
    ins = [a, b]
    if rms_gain is not None:
        in_specs.append(pl.BlockSpec((1, ka), lambda j, t: (0, 0)))
        ins.append(rms_gain)
    return pl.pallas_call(
        body, name=name, grid=(n // tn, l // tl), in_specs=in_specs,
        out_specs=pl.BlockSpec((ka, tn), lambda j, t: (0, j)), out_shape=_sds((ka, n)),
        compiler_params=_params(("arbitrary", "arbitrary"), vmem))(*ins)


def _scores(q, k, qi, ki, tq, scale):
    s = _mm_nt(q, k) * scale
    rows = qi * tq + lax.broadcasted_iota(jnp.int32, s.shape, 0)
    cols = ki * tq + lax.broadcasted_iota(jnp.int32, s.shape, 1)
    return jnp.where(cols <= rows, s, NEG)


def _flash_fwd(name, q, k, v, scale):
    l = q.shape[0]
    tq = min(512, l)
    nq = l // tq

    def body(q_ref, k_ref, v_ref, o_ref, lse_ref, m_s, l_s, acc_s):
        qi = pl.program_id(1)
        ki = pl.program_id(2)

        @pl.when(ki == 0)
        def _():
            m_s[...] = jnp.full(m_s.shape, NEG, F32)
            l_s[...] = jnp.zeros(l_s.shape, F32)
            acc_s[...] = jnp.zeros(acc_s.shape, F32)

        @pl.when(ki <= qi)
        def _():
            s = _scores(q_ref[...], k_ref[...], qi, ki, tq, scale)
            m_old = m_s[...]
            m_new = jnp.maximum(m_old, jnp.max(s, axis=-1, keepdims=True))
            alpha = jnp.exp(m_old - m_new)
            p = jnp.exp(s - m_new)
            l_s[...] = alpha * l_s[...] + jnp.sum(p, axis=-1, keepdims=True)
            acc_s[...] = alpha * acc_s[...] + _mm(p, v_ref[...])
            m_s[...] = m_new

        @pl.when(ki == nq - 1)
        def _():
            o_ref[...] = acc_s[...] / l_s[...]
            lse_ref[...] = jnp.broadcast_to(m_s[...] + jnp.log(l_s[...]), lse_ref.shape)

    qspec = pl.BlockSpec((tq, LANE), lambda h, qi, ki: (qi, h))
    kspec = pl.BlockSpec((tq, LANE), lambda h, qi, ki: (jnp.minimum(ki, qi), h))
    return pl.pallas_call(
        body, name=name, grid=(MLA_HEADS, nq, nq), in_specs=[qspec, kspec, kspec],
        out_specs=(qspec, qspec), out_shape=(_sds((l, MLA_PAD)), _sds((l, MLA_PAD))),
        scratch_shapes=[pltpu.VMEM((tq, 1), F32), pltpu.VMEM((tq, 1), F32), pltpu.VMEM((tq, LANE), F32)],
        compiler_params=_params(("arbitrary", "arbitrary", "arbitrary"), 48))(q, k, v)


def _flash_bwd(name, q, k, v, o, lse, do, scale):
    l = q.shape[0]
    tq = min(512, l)
    nq = l // tq

    def probs(q_ref, k_ref, v_ref, o_ref, lse_ref, do_ref, qi, ki):
        s = _scores(q_ref[...], k_ref[...], qi, ki, tq, scale)
        p = jnp.exp(s - lse_ref[...][:, :1])
        dov = do_ref[...]
        dp = _mm_nt(dov, v_ref[...])
        delta = jnp.sum(dov * o_ref[...], axis=-1, keepdims=True)
        ds = p * (dp - delta) * scale
        return p, ds

    def dq_body(q_ref, k_ref, v_ref, o_ref, lse_ref, do_ref, dq_ref):
        qi = pl.program_id(1)
        ki = pl.program_id(2)

        @pl.when(ki == 0)
        def _():
            dq_ref[...] = jnp.zeros(dq_ref.shape, F32)

        @pl.when(ki <= qi)
        def _():
            _, ds = probs(q_ref, k_ref, v_ref, o_ref, lse_ref, do_ref, qi, ki)
            dq_ref[...] += _mm(ds, k_ref[...])

    qspec = pl.BlockSpec((tq, LANE), lambda h, qi, ki: (qi, h))
    kspec = pl.BlockSpec((tq, LANE), lambda h, qi, ki: (jnp.minimum(ki, qi), h))
    dq = pl.pallas_call(
        dq_body, name=name + "_dq", grid=(MLA_HEADS, nq, nq),
        in_specs=[qspec, kspec, kspec, qspec, qspec, qspec], out_specs=qspec, out_shape=_sds((l, MLA_PAD)),
        compiler_params=_params(("arbitrary", "arbitrary", "arbitrary"), 48))(q, k, v, o, lse, do)

    def dkv_body(q_ref, k_ref, v_ref, o_ref, lse_ref, do_ref, dk_ref, dv_ref):
        ki = pl.program_id(1)
        qi = pl.program_id(2)

        @pl.when(qi == 0)
        def _():
            dk_ref[...] = jnp.zeros(dk_ref.shape, F32)
            dv_ref[...] = jnp.zeros(dv_ref.shape, F32)

        @pl.when(qi >= ki)
        def _():
            p, ds = probs(q_ref, k_ref, v_ref, o_ref, lse_ref, do_ref, qi, ki)
            dv_ref[...] += _mm_tn(p, do_ref[...])
            dk_ref[...] += _mm_tn(ds, q_ref[...])

    qspec2 = pl.BlockSpec((tq, LANE), lambda h, ki, qi: (jnp.maximum(qi, ki), h))
    kspec2 = pl.BlockSpec((tq, LANE), lambda h, ki, qi: (ki, h))
    dk, dv = pl.pallas_call(
        dkv_body, name=name + "_dkv", grid=(MLA_HEADS, nq, nq),
        in_specs=[qspec2, kspec2, kspec2, qspec2, qspec2, qspec2], out_specs=(kspec2, kspec2),
        out_shape=(_sds((l, MLA_PAD)), _sds((l, MLA_PAD))),
        compiler_params=_params(("arbitrary", "arbitrary", "arbitrary"), 48))(q, k, v, o, lse, do)
    return dq, dk, dv


def _cmul(ar, ai, br, bi):
    return ar * br - ai * bi, ar * bi + ai * br


def _scan(name, x_re, x_im, a_re, a_im, reverse):
    l, lanes = x_re.shape
    ns = SCAN_SEGS
    tl = l // ns
    steps = int(math.log2(tl))
    assert 2 ** steps == tl and tl * ns == l

    def body(xr_ref, xi_ref, ar_ref, ai_ref, sr_ref, si_ref):
        a_r1 = ar_ref[...]
        a_i1 = ai_ref[...]
        a_r = jnp.broadcast_to(a_r1, (ns, LANE))
        a_i = jnp.broadcast_to(a_i1, (ns, LANE))

        def rows(t):
            t = (tl - 1 - t) if reverse else t
            return pl.ds(pl.multiple_of(t * ns, ns), ns)

        def local(t, carry):
            cr, ci = carry
            r = rows(t)
            pr, pi = _cmul(a_r, a_i, cr, ci)
            return pr + xr_ref[r, :], pi + xi_ref[r, :]

        zero = jnp.zeros((ns, LANE), F32)
        e_r, e_i = lax.fori_loop(0, tl, local, (zero, zero))
        p_r, p_i = a_r1, a_i1
        for _ in range(steps):
            p_r, p_i = _cmul(p_r, p_i, p_r, p_i)
        rid = lax.broadcasted_iota(jnp.int32, (ns, LANE), 0)
        c_r = jnp.zeros((1, LANE), F32)
        c_i = jnp.zeros((1, LANE), F32)
        in_r, in_i = zero, zero
        order = range(ns - 2, -1, -1) if reverse else range(1, ns)
        for kk in order:
            src = kk + 1 if reverse else kk - 1
            ek_r = jnp.sum(jnp.where(rid == src, e_r, 0.0), axis=0, keepdims=True)
            ek_i = jnp.sum(jnp.where(rid == src, e_i, 0.0), axis=0, keepdims=True)
            q_r, q_i = _cmul(p_r, p_i, c_r, c_i)
            c_r, c_i = q_r + ek_r, q_i + ek_i
            in_r = jnp.where(rid == kk, jnp.broadcast_to(c_r, (ns, LANE)), in_r)
            in_i = jnp.where(rid == kk, jnp.broadcast_to(c_i, (ns, LANE)), in_i)

        def final(t, carry):
            cr, ci = carry
            r = rows(t)
            pr, pi = _cmul(a_r, a_i, cr, ci)
            nr, ni = pr + xr_ref[r, :], pi + xi_ref[r, :]
            sr_ref[r, :] = nr
            si_ref[r, :] = ni
            return nr, ni

        lax.fori_loop(0, tl, final, (in_r, in_i))

    xs = pl.BlockSpec((l, LANE), lambda j: (0, j))
    as_ = pl.BlockSpec((1, LANE), lambda j: (0, j))
    return pl.pallas_call(
        body, name=name, grid=(lanes // LANE,), in_specs=[xs, xs, as_, as_], out_specs=(xs, xs),
        out_shape=(_sds((l, lanes)), _sds((l, lanes))),
        compiler_params=_params(("arbitrary",), 48))(x_re, x_im, a_re, a_im)


ANY = pl.BlockSpec(memory_space=pl.ANY)


def _chip_exchange(name, src, per_peer):
    r, c = src.shape[-2:]

    def body(x_ref, y_ref, send_sems, recv_sems, local_sem):
        mx, my, mc = lax.axis_index("x"), lax.axis_index("y"), lax.axis_index("c")
        me = 2 * mx + my
        peers = [(1 - mx, my), (mx, 1 - my), (1 - mx, 1 - my)]

        def part(k):
            return x_ref.at[k] if per_peer else x_ref

        own = pltpu.make_async_copy(part(me), y_ref.at[me], local_sem)
        own.start()
        copies = []
        for j, (px, py) in enumerate(peers):
            cp = pltpu.make_async_remote_copy(
                src_ref=part(2 * px + py), dst_ref=y_ref.at[me], send_sem=send_sems.at[j],
                recv_sem=recv_sems.at[j], device_id=(px, py, mc), device_id_type=MESH)
            cp.start()
            copies.append(cp)
        for cp in copies:
            cp.wait_recv()
        for cp in copies:
            cp.wait_send()
        own.wait()

    return pl.pallas_call(
        body, name=name, in_specs=[ANY], out_specs=ANY, out_shape=_sds((4, r, c), src.dtype),
        scratch_shapes=[pltpu.SemaphoreType.DMA((3,)), pltpu.SemaphoreType.DMA((3,)), pltpu.SemaphoreType.DMA])(src)


def _core_exchange(name, src):
    def body(x_ref, y_ref, send_sem, recv_sem, local_sem):
        mx, my, mc = lax.axis_index("x"), lax.axis_index("y"), lax.axis_index("c")
        own = pltpu.make_async_copy(x_ref, y_ref.at[mc], local_sem)
        own.start()
        cp = pltpu.make_async_remote_copy(
            src_ref=x_ref, dst_ref=y_ref.at[mc], send_sem=send_sem, recv_sem=recv_sem,
            device_id=(mx, my, 1 - mc), device_id_type=MESH)
        cp.start()
        cp.wait_recv()
        cp.wait_send()
        own.wait()

    return pl.pallas_call(
        body, name=name, in_specs=[ANY], out_specs=ANY, out_shape=_sds((2,) + src.shape, src.dtype),
        scratch_shapes=[pltpu.SemaphoreType.DMA, pltpu.SemaphoreType.DMA, pltpu.SemaphoreType.DMA])(src)


def _all_exchange(name, src):
    def body(x_ref, y_ref, send_sems, recv_sems, local_sem):
        mx, my, mc = lax.axis_index("x"), lax.axis_index("y"), lax.axis_index("c")
        me = 4 * mx + 2 * my + mc
        own = pltpu.make_async_copy(x_ref, y_ref.at[me], local_sem)
        own.start()
        copies = []
        for j in range(1, 8):
            px = (1 - mx) if (j & 4) else mx
            py = (1 - my) if (j & 2) else my
            pc = (1 - mc) if (j & 1) else mc
            cp = pltpu.make_async_remote_copy(
                src_ref=x_ref, dst_ref=y_ref.at[me], send_sem=send_sems.at[j - 1], recv_sem=recv_sems.at[j - 1],
                device_id=(px, py, pc), device_id_type=MESH)
            cp.start()
            copies.append(cp)
        for cp in copies:
            cp.wait_recv()
        for cp in copies:
            cp.wait_send()
        own.wait()

    return pl.pallas_call(
        body, name=name, in_specs=[ANY], out_specs=ANY, out_shape=_sds((8,) + src.shape, src.dtype),
        scratch_shapes=[pltpu.SemaphoreType.DMA((7,)), pltpu.SemaphoreType.DMA((7,)), pltpu.SemaphoreType.DMA])(src)


PACK_W = 1024


def _pack(arrs, rows_multiple, dtype):
    flat = jnp.concatenate([a.reshape(-1).astype(dtype) for a in arrs])
    n = flat.shape[0]
    unit = PACK_W * rows_multiple
    tot = -(-n // unit) * unit
    flat = jnp.pad(flat, (0, tot - n))
    return flat.reshape(tot // PACK_W, PACK_W)


def _unpack(flat, shapes):
    flat = flat.reshape(-1)
    out = []
    off = 0
    for s in shapes:
        n = int(np.prod(s))
        out.append(flat[off:off + n].reshape(s))
        off += n
    return out


def _rope_tables(pos):
    l = pos.shape[0]
    tm = min(512, l)
    inv = (np.float32(ROPE_THETA) ** (-np.arange(0, D_ROPE, 2, dtype=np.float32) / np.float32(D_ROPE))).astype(np.float32)
    lane_f = np.zeros((1, LANE), np.float32)
    lane_f[0, D_NOPE:D_NOPE + 16] = inv
    lane_f[0, D_NOPE + 16:D_NOPE + 32] = inv

    def fn(i, p, f):
        ang = p * f
        lane = lax.broadcasted_iota(jnp.int32, ang.shape, 1)
        co = jnp.cos(ang)
        si = jnp.sin(ang)
        c = jnp.where(lane < D_NOPE, 1.0, jnp.where(lane < D_QK, co, 0.0))
        s1 = jnp.where((lane >= D_NOPE) & (lane < D_NOPE + 16), -si, 0.0)
        s2 = jnp.where((lane >= D_NOPE + 16) & (lane < D_QK), si, 0.0)
        return c, s1, s2

    return _rows(fn, name="rope_tables", n=l // tm, ins=[pos, jnp.asarray(lane_f)],
                 in_specs=[_rt(tm, 1), _full((1, LANE))], outs=[_sds((l, LANE))] * 3, out_specs=[_rt(tm, LANE)] * 3)


def _ssm_param_fn(lr, li, log_dt, br, bi):
    dt = jnp.exp(log_dt)
    mag = jnp.exp(lr * dt)
    a_re = mag * jnp.cos(li * dt)
    a_im = mag * jnp.sin(li * dt)
    den = lr * lr + li * li
    e_re = a_re - 1.0
    e_im = a_im
    f_re = (e_re * lr + e_im * li) / den
    f_im = (e_im * lr - e_re * li) / den
    bb_re = f_re[None] * br - f_im[None] * bi
    bb_im = f_re[None] * bi + f_im[None] * br
    return a_re, a_im, bb_re, bb_im


def _ssm_params(name, lr, li, log_dt, br, bi):
    g, n = lr.shape
    c = br.shape[0]
    return _rows(lambda i, *v: _ssm_param_fn(*v), name=name, n=1, ins=[lr, li, log_dt, br, bi],
                 in_specs=[_full((g, n)), _full((g, n)), _full((g, 1)), _full((c, g, n)), _full((c, g, n))],
                 outs=[_sds((g, n)), _sds((g, n)), _sds((c, g, n)), _sds((c, g, n))],
                 out_specs=[_full((g, n)), _full((g, n)), _full((c, g, n)), _full((c, g, n))])


def _ssm_params_bwd(name, lr, li, log_dt, br, bi, d_are, d_aim, d_bbre, d_bbim):
    g, n = lr.shape
    c = br.shape[0]

    def fn(i, lr, li, log_dt, br, bi, g0, g1, g2, g3):
        _, vjp = jax.vjp(_ssm_param_fn, lr, li, log_dt, br, bi)
        return vjp((g0, g1, g2, g3))

    sp = [_full((g, n)), _full((g, n)), _full((g, 1)), _full((c, g, n)), _full((c, g, n))]
    return _rows(fn, name=name, n=1, ins=[lr, li, log_dt, br, bi, d_are, d_aim, d_bbre, d_bbim],
                 in_specs=sp + [_full((g, n)), _full((g, n)), _full((c, g, n)), _full((c, g, n))],
                 outs=[_sds((g, n)), _sds((g, n)), _sds((g, 1)), _sds((c, g, n)), _sds((c, g, n))], out_specs=sp)


_EYE8 = np.eye(8, dtype=np.float32)


def _blockdiag(v):
    j, g, p, q = v.shape
    m = v[:, :, :, None, :] * jnp.asarray(_EYE8)[None, :, None, :, None]
    return m.reshape(j, g * p, g * q)


def _blockdiag_t(m, p, q):
    j = m.shape[0]
    m = m.reshape(j, 8, p, 8, q)
    return jnp.sum(m * jnp.asarray(_EYE8)[None, :, None, :, None], axis=3)


def _to_perm(v, l):
    ns = SCAN_SEGS
    return v.reshape(ns, l // ns, v.shape[-1]).transpose(1, 0, 2).reshape(l, v.shape[-1])


def _from_perm(v, l):
    ns = SCAN_SEGS
    return v.reshape(l // ns, ns, v.shape[-1]).transpose(1, 0, 2).reshape(l, v.shape[-1])


def _prep_layer(w, i):
    p = {}
    w_in = w['w_in'][i]
    z = lambda n: jnp.zeros((D_MODEL, n), w_in.dtype)
    o = Q_LORA + KV_LORA
    p['w_s'] = jnp.concatenate([w_in[:, :o], z(D_NOPE), w_in[:, o:o + D_ROPE], z(HEAD_PAD - D_QK)], axis=1)
    o += D_ROPE
    p['w_u'] = w_in[:, o:o + SSM_WIDTH]
    o += SSM_WIDTH
    p['w_xq'] = w_in[:, o:o + X_WIDTH]
    o += X_WIDTH
    p['w_g'] = w_in[:, o:]
    wq = w['w_q_b'][i].reshape(Q_LORA, MLA_HEADS, D_QK)
    p['w_qb'] = jnp.pad(wq, ((0, 0), (0, 0), (0, HEAD_PAD - D_QK))).reshape(Q_LORA, MLA_PAD)
    wkv = w['w_kv_b'][i].reshape(KV_LORA, MLA_HEADS, D_NOPE + D_V)
    p['w_k'] = jnp.pad(wkv[:, :, :D_NOPE], ((0, 0), (0, 0), (0, HEAD_PAD - D_NOPE))).reshape(KV_LORA, MLA_PAD)
    p['w_v'] = jnp.pad(wkv[:, :, D_NOPE:], ((0, 0), (0, 0), (0, HEAD_PAD - D_V))).reshape(KV_LORA, MLA_PAD)
    wo = w['w_o_mla'][i].reshape(MLA_HEADS, D_V, D_MODEL)
    p['w_oa'] = jnp.pad(wo, ((0, 0), (0, HEAD_PAD - D_V), (0, 0))).reshape(MLA_PAD, D_MODEL)
    for n in ('w_glu', 'w_o_ssm', 'w_mem_kv', 'w_o_cross', 'w_out', 'w_up', 'w_down'):
        p[n] = w[n][i]
    p['conv_w'] = w['conv_w'][i]
    for n in ('norm_mix_g', 'q_a_norm_g', 'kv_a_norm_g', 'b_glu', 'mem_norm_g', 'xq_norm_g', 'xk_norm_g', 'b_gate',
              'norm_ffn_g', 'conv_b'):
        p[n] = w[n][i].reshape(1, -1)
    p['q_norm_g'] = jnp.pad(w['q_norm_g'][i], (0, HEAD_PAD - D_QK)).reshape(1, HEAD_PAD)
    p['k_norm_g'] = jnp.pad(w['k_norm_g'][i], (0, HEAD_PAD - D_QK)).reshape(1, HEAD_PAD)
    p['ssm_d'] = w['ssm_d'][i].reshape(1, SSM_WIDTH)
    p['lr'] = w['ssm_lambda_re'][i]
    p['li'] = w['ssm_lambda_im'][i]
    p['log_dt'] = w['ssm_log_dt'][i].reshape(SSM_GROUPS, 1)
    p['br'] = w['ssm_b_re'][i].transpose(2, 0, 1)
    p['bi'] = w['ssm_b_im'][i].transpose(2, 0, 1)
    cr = w['ssm_c_re'][i].reshape(SSM_JB, 8, SSM_GROUP_CH, SSM_STATE).transpose(0, 1, 3, 2)
    ci = w['ssm_c_im'][i].reshape(SSM_JB, 8, SSM_GROUP_CH, SSM_STATE).transpose(0, 1, 3, 2)
    p['c_mat'] = jnp.concatenate([_blockdiag(cr), -_blockdiag(ci)], axis=1).astype(BF16)
    return p


def _b_mat(bb_re, bb_im):
    r = bb_re.transpose(1, 0, 2).reshape(SSM_JB, 8, SSM_GROUP_CH, SSM_STATE)
    i = bb_im.transpose(1, 0, 2).reshape(SSM_JB, 8, SSM_GROUP_CH, SSM_STATE)
    return jnp.concatenate([_blockdiag(r), _blockdiag(i)], axis=2).astype(BF16)


def _qkv_fn(ps, c, s1, s2, qag, wqb, kvag, wk, wv, qng, kng):
    c_q = ps[:, :Q_LORA]
    c_kv = ps[:, Q_LORA:Q_LORA + KV_LORA]
    kr = ps[:, Q_LORA + KV_LORA:]
    cqn = _rms(c_q, qag, Q_LORA)
    ckvn = _rms(c_kv, kvag, KV_LORA)
    q_raw = _mm(cqn, wqb)
    k_raw = _mm(ckvn, wk) + jnp.concatenate([kr] * MLA_HEADS, axis=-1)
    v = _mm(ckvn, wv)
    q = _heads(_rope, _head_rms(q_raw, qng, MLA_HEADS, D_QK), MLA_HEADS, c, s1, s2)
    k = _heads(_rope, _head_rms(k_raw, kng, MLA_HEADS, D_QK), MLA_HEADS, c, s1, s2)
    return q, k, v


def _layer_fwd(name, x, tabs, mem, p):
    l = x.shape[0]
    tm = min(512, l)
    nt = l // tm
    sv = {'x0': x}
    sv['p_g'] = _matmul(name + "_in_g", [(x, p['w_g'])], l, 3 * D_MODEL, rms_gain=p['norm_mix_g'])
    sv['p_u'] = _matmul(name + "_in_u", [(x, p['w_u'])], l, SSM_WIDTH, rms_gain=p['norm_mix_g'])
    sv['p_xq'] = _matmul(name + "_in_xq", [(x, p['w_xq'])], l, X_WIDTH, rms_gain=p['norm_mix_g'])
    sv['p_s'] = _matmul(name + "_in_s", [(x, p['w_s'])], l, SMALL_W, rms_gain=p['norm_mix_g'])

    qkv_consts = [p['q_a_norm_g'], p['w_qb'], p['kv_a_norm_g'], p['w_k'], p['w_v'], p['q_norm_g'], p['k_norm_g']]
    qkv_cspecs = [_full(a.shape) for a in qkv_consts]
    q, k, v = _rows(lambda i, *a: _qkv_fn(*a), name=name + "_qkv", n=nt, ins=[sv['p_s'], *tabs, *qkv_consts],
                    in_specs=[_rt(tm, SMALL_W)] + [_rt(tm, LANE)] * 3 + qkv_cspecs,
                    outs=[_sds((l, MLA_PAD), BF16)] * 3, out_specs=[_rt(tm, MLA_PAD)] * 3)
    sv['q'], sv['k'], sv['v'] = q, k, v
    sv['o_a'], sv['lse'] = _flash_fwd(name + "_attn", q, k, v, D_QK ** -0.5)

    a_re, a_im, bb_re, bb_im = _ssm_params(name + "_ssm_par", p['lr'], p['li'], p['log_dt'], p['br'], p['bi'])
    sv['a_re'], sv['a_im'] = a_re.reshape(1, SSM_LANES), a_im.reshape(1, SSM_LANES)
    sv['b_mat'] = _b_mat(bb_re, bb_im)
    u_p = _to_perm(sv['p_u'], l)
    sv['u_p'] = u_p

    def bu_fn(i, u, bm):
        res = [_mm(u[:, j * LANE:(j + 1) * LANE], bm[j]) for j in range(SSM_JB)]
        return (jnp.concatenate([r[:, :512] for r in res], axis=-1), jnp.concatenate([r[:, 512:] for r in res], axis=-1))

    ts = min(256, l)
    bu_re, bu_im = _rows(bu_fn, name=name + "_ssm_bu", n=l // ts, ins=[u_p, sv['b_mat']],
                         in_specs=[_rt(ts, SSM_WIDTH), _full(sv['b_mat'].shape)],
                         outs=[_sds((l, SSM_LANES))] * 2, out_specs=[_rt(ts, SSM_LANES)] * 2)
    s_re, s_im = _scan(name + "_ssm_scan", bu_re, bu_im, sv['a_re'], sv['a_im'], reverse=False)
    sv['s_re'], sv['s_im'] = s_re, s_im

    def glu_fn(i, sr, si, u, cm, dsk, wg, bg):
        y = jnp.concatenate([_mm(jnp.concatenate([sr[:, j * 512:(j + 1) * 512], si[:, j * 512:(j + 1) * 512]], axis=-1),
                                 cm[j]) for j in range(SSM_JB)], axis=-1) + dsk * u
        zz = _gelu(y)
        return zz * jax.nn.sigmoid(_mm(zz, wg) + bg)

    glu_consts = [p['c_mat'], p['ssm_d'], p['w_glu'], p['b_glu']]
    zo_p = _rows(glu_fn, name=name + "_ssm_glu", n=l // ts, ins=[s_re, s_im, u_p, *glu_consts],
                 in_specs=[_rt(ts, SSM_LANES), _rt(ts, SSM_LANES), _rt(ts, SSM_WIDTH)] + [_full(a.shape) for a in glu_consts],
                 outs=[_sds((l, SSM_WIDTH), BF16)], out_specs=[_rt(ts, SSM_WIDTH)])[0]
    sv['zo'] = _from_perm(zo_p, l)

    m_len = mem.shape[0]

    def memkv_fn(i, mm_, mg, wmk, xkg):
        kv = _mm(_rms(mm_, mg, D_MODEL), wmk)
        return _head_rms(kv[:, :X_WIDTH], xkg, X_HEADS, X_HEAD_DIM), kv[:, X_WIDTH:]

    mem_consts = [p['mem_norm_g'], p['w_mem_kv'], p['xk_norm_g']]
    k_c, v_c = _rows(memkv_fn, name=name + "_memkv", n=1, ins=[mem, *mem_consts],
                     in_specs=[_full(mem.shape)] + [_full(a.shape) for a in mem_consts],
                     outs=[_sds((m_len, X_WIDTH))] * 2, out_specs=[_full((m_len, X_WIDTH))] * 2)
    sv['k_c'], sv['v_c'] = k_c, v_c

    def cross_fn(i, xq, kc, vc, xqg):
        outs = []
        for h in range(X_HEADS):
            sl = slice(h * LANE, (h + 1) * LANE)
            qh = _rms(xq[:, sl], xqg, X_HEAD_DIM)
            s = _mm_nt(qh, kc[:, sl]) * (X_HEAD_DIM ** -0.5)
            s = s - jnp.max(s, axis=-1, keepdims=True)
            e = jnp.exp(s)
            pr = e / jnp.sum(e, axis=-1, keepdims=True)
            outs.append(_mm(pr, vc[:, sl]))
        return jnp.concatenate(outs, axis=-1)

    sv['o_c'] = _rows(cross_fn, name=name + "_cross", n=nt, ins=[sv['p_xq'], k_c, v_c, p['xq_norm_g']],
                      in_specs=[_rt(tm, X_WIDTH), _full(k_c.shape), _full(v_c.shape), _full((1, LANE))],
                      outs=[_sds((l, X_WIDTH), BF16)], out_specs=[_rt(tm, X_WIDTH)])[0]

    def merge_fn(i, oa, zo, oc, pg, x0, woa, wos, woc, bg, wout):
        gates = jax.nn.sigmoid(pg + bg)
        merged = (gates[:, :D_MODEL] * _mm(oa, woa) + gates[:, D_MODEL:2 * D_MODEL] * _mm(zo, wos)
                  + gates[:, 2 * D_MODEL:] * _mm(oc, woc))
        return x0 + _mm(merged, wout), merged

    merge_consts = [p['w_oa'], p['w_o_ssm'], p['w_o_cross'], p['b_gate'], p['w_out']]
    tg = min(256, l)
    x1, merged = _rows(merge_fn, name=name + "_merge", n=l // tg, ins=[sv['o_a'], sv['zo'], sv['o_c'], sv['p_g'], x, *merge_consts],
                       in_specs=[_rt(tg, MLA_PAD), _rt(tg, SSM_WIDTH), _rt(tg, X_WIDTH), _rt(tg, 3 * D_MODEL), _rt(tg, D_MODEL)]
                       + [_full(a.shape) for a in merge_consts],
                       outs=[_sds((l, D_MODEL)), _sds((l, D_MODEL), BF16)], out_specs=[_rt(tg, D_MODEL)] * 2)
    sv['x1'], sv['merged'] = x1, merged

    up = _matmul(name + "_up", [(x1, p['w_up'])], l, 2 * D_FF, rms_gain=p['norm_ffn_g'])
    sv['up'] = up
    tc = min(128, l)

    def conv_fn(i, upt, halo, cw, cb):
        upc = _conv(i, upt, halo, cw) + cb
        return _silu(upc[:, :D_FF]) * upc[:, D_FF:]

    act = _rows(conv_fn, name=name + "_conv", n=l // tc, ins=[up, up, p['conv_w'], p['conv_b']],
                in_specs=[_rt(tc, 2 * D_FF), _halo_prev(tc, 2 * D_FF), _full((3, 2 * D_FF)), _full((1, 2 * D_FF))],
                outs=[_sds((l, D_FF), BF16)], out_specs=[_rt(tc, D_FF)])[0]
    sv['act'] = act
    x2 = _matmul(name + "_down", [(act, p['w_down'])], l, D_MODEL, resid=x1)
    return x2, sv


def _halo_prev(tm, w):
    return pl.BlockSpec((8, w), lambda i: (jnp.maximum(i * (tm // 8) - 1, 0), 0))


def _halo_next(tm, w, n_tiles):
    last = n_tiles * (tm // 8) - 1
    return pl.BlockSpec((8, w), lambda i: (jnp.minimum((i + 1) * (tm // 8), last), 0))


def _conv(i, tile, halo, cw):
    halo = jnp.where(i > 0, halo, 0.0)
    ext = jnp.concatenate([halo, tile], axis=0)
    n = ext.shape[0]
    x1 = pltpu.roll(ext, 1, 0)[8:]
    x2 = pltpu.roll(ext, 2, 0)[8:]
    del n
    return cw[0:1] * x2 + cw[1:2] * x1 + cw[2:3] * tile


def _layer_bwd(name, dx2, sv, tabs, mem, p):
    l = dx2.shape[0]
    tm = min(512, l)
    nt = l // tm
    g = {}
    x1 = sv['x1']
    dact = _matmul(name + "_b_down", [(dx2, p['w_down'])], l, D_FF, nt=True)
    g['w_down'] = _matmul_tn(name + "_gw_down", sv['act'], dx2)
    tc = min(128, l)
    ntc = l // tc

    def conv_b1(i, upt, halo, da, cw, cb):
        halo = jnp.where(i > 0, halo, 0.0)
        ext = jnp.concatenate([halo, upt], axis=0)
        xm1 = pltpu.roll(ext, 1, 0)[8:]
        xm2 = pltpu.roll(ext, 2, 0)[8:]
        upc = cw[0:1] * xm2 + cw[1:2] * xm1 + cw[2:3] * upt + cb
        gf, vf = upc[:, :D_FF], upc[:, D_FF:]
        _, vjp = jax.vjp(lambda a, b: _silu(a) * b, gf, vf)
        dg, dv = vjp(da)
        dupc = jnp.concatenate([dg, dv], axis=-1)
        dcw = _row_select([_colsum(dupc * xm2), _colsum(dupc * xm1), _colsum(dupc * upt)], 8)
        return dupc, dcw, _colsum(dupc)

    dupc, g_cw, g_cb = _rows(conv_b1, name=name + "_b_conv1", n=ntc, ins=[sv['up'], sv['up'], dact, p['conv_w'], p['conv_b']],
                             in_specs=[_rt(tc, 2 * D_FF), _halo_prev(tc, 2 * D_FF), _rt(tc, D_FF), _full((3, 2 * D_FF)),
                                       _full((1, 2 * D_FF))],
                             outs=[_sds((l, 2 * D_FF)), _sds((8, 2 * D_FF)), _sds((1, 2 * D_FF))],
                             out_specs=[_rt(tc, 2 * D_FF), _full((8, 2 * D_FF)), _full((1, 2 * D_FF))], n_acc=2)
    g['conv_w'] = g_cw[:3]
    g['conv_b'] = g_cb

    def conv_b2(i, dt, halo, cw):
        halo = jnp.where(i < ntc - 1, halo, 0.0)
        ext = jnp.concatenate([dt, halo], axis=0)
        n = ext.shape[0]
        dp1 = pltpu.roll(ext, n - 1, 0)[:tc]
        dp2 = pltpu.roll(ext, n - 2, 0)[:tc]
        return cw[2:3] * dt + cw[1:2] * dp1 + cw[0:1] * dp2

    dup = _rows(conv_b2, name=name + "_b_conv2", n=ntc, ins=[dupc, dupc, p['conv_w']],
                in_specs=[_rt(tc, 2 * D_FF), _halo_next(tc, 2 * D_FF, ntc), _full((3, 2 * D_FF))],
                outs=[_sds((l, 2 * D_FF))], out_specs=[_rt(tc, 2 * D_FF)])[0]
    dh2 = _matmul(name + "_b_up", [(dup, p['w_up'])], l, D_MODEL, nt=True, tm=256)
    g['w_up'] = _matmul_tn(name + "_gw_up", x1, dup, rms_gain=p['norm_ffn_g'])

    def norm_b(i, xv, dh, dres, gn):
        _, vjp = jax.vjp(lambda a, b: _rms(a, b, D_MODEL), xv, gn)
        dxv, dgn = vjp(dh)
        return dres + dxv, dgn

    dx1, g['norm_ffn_g'] = _rows(norm_b, name=name + "_b_norm2", n=nt, ins=[x1, dh2, dx2, p['norm_ffn_g']],
                                 in_specs=[_rt(tm, D_MODEL)] * 3 + [_full((1, D_MODEL))],
                                 outs=[_sds((l, D_MODEL)), _sds((1, D_MODEL))], out_specs=[_rt(tm, D_MODEL), _full((1, D_MODEL))],
                                 n_acc=1)

    tg = min(256, l)

    def merge_b(i, dx, oa, zo, oc, pg, woa, wos, woc, bg, wout):
        dm = _mm_nt(dx, wout)
        gates = jax.nn.sigmoid(pg + bg)
        ys = [_mm(oa, woa), _mm(zo, wos), _mm(oc, woc)]
        dys, dpg = [], []
        for b in range(3):
            gb = gates[:, b * D_MODEL:(b + 1) * D_MODEL]
            dys.append(dm * gb)
            dpg.append(dm * ys[b] * gb * (1.0 - gb))
        dpg = jnp.concatenate(dpg, axis=-1)
        return (_mm_nt(dys[0], woa), _mm_nt(dys[1], wos), _mm_nt(dys[2], woc), dpg, dys[0], dys[1], dys[2], _colsum(dpg))

    merge_consts = [p['w_oa'], p['w_o_ssm'], p['w_o_cross'], p['b_gate'], p['w_out']]
    (do_a, dzo, do_c, dp_g, dy_a, dy_b, dy_c, g['b_gate']) = _rows(
        merge_b, name=name + "_b_merge", n=l // tg, ins=[dx1, sv['o_a'], sv['zo'], sv['o_c'], sv['p_g'], *merge_consts],
        in_specs=[_rt(tg, D_MODEL), _rt(tg, MLA_PAD), _rt(tg, SSM_WIDTH), _rt(tg, X_WIDTH), _rt(tg, 3 * D_MODEL)]
        + [_full(a.shape) for a in merge_consts],
        outs=[_sds((l, MLA_PAD)), _sds((l, SSM_WIDTH)), _sds((l, X_WIDTH)), _sds((l, 3 * D_MODEL)),
              _sds((l, D_MODEL), BF16), _sds((l, D_MODEL), BF16), _sds((l, D_MODEL), BF16), _sds((1, 3 * D_MODEL))],
        out_specs=[_rt(tg, MLA_PAD), _rt(tg, SSM_WIDTH), _rt(tg, X_WIDTH), _rt(tg, 3 * D_MODEL),
                   _rt(tg, D_MODEL), _rt(tg, D_MODEL), _rt(tg, D_MODEL), _full((1, 3 * D_MODEL))], n_acc=1, vmem=56)
    g['w_out'] = _matmul_tn(name + "_gw_out", sv['merged'], dx1)
    g['w_oa'] = _matmul_tn(name + "_gw_oa", sv['o_a'], dy_a)
    g['w_o_ssm'] = _matmul_tn(name + "_gw_os", sv['zo'], dy_b)
    g['w_o_cross'] = _matmul_tn(name + "_gw_oc", sv['o_c'], dy_c)

    k_c, v_c = sv['k_c'], sv['v_c']
    m_len = k_c.shape[0]

    def cross_b(i, xq, do, kc, vc, xqg):
        dxq, dk, dv = [], [], []
        dg = jnp.zeros((1, LANE), F32)
        for h in range(X_HEADS):
            sl = slice(h * LANE, (h + 1) * LANE)
            qh, vjp = jax.vjp(lambda a, b: _rms(a, b, X_HEAD_DIM), xq[:, sl], xqg)
            sc = X_HEAD_DIM ** -0.5
            s = _mm_nt(qh, kc[:, sl]) * sc
            s = s - jnp.max(s, axis=-1, keepdims=True)
            e = jnp.exp(s)
            pr = e / jnp.sum(e, axis=-1, keepdims=True)
            doh = do[:, sl]
            dv.append(_mm_tn(pr, doh))
            dp = _mm_nt(doh, vc[:, sl])
            ds = pr * (dp - jnp.sum(dp * pr, axis=-1, keepdims=True)) * sc
            dk.append(_mm_tn(ds, qh))
            dxh, dgh = vjp(_mm(ds, kc[:, sl]))
            dxq.append(dxh)
            dg = dg + dgh
        return jnp.concatenate(dxq, axis=-1), jnp.concatenate(dk, axis=-1), jnp.concatenate(dv, axis=-1), dg

    dp_xq, dk_c, dv_c, g['xq_norm_g'] = _rows(
        cross_b, name=name + "_b_cross", n=nt, ins=[sv['p_xq'], do_c, k_c, v_c, p['xq_norm_g']],
        in_specs=[_rt(tm, X_WIDTH), _rt(tm, X_WIDTH), _full(k_c.shape), _full(v_c.shape), _full((1, LANE))],
        outs=[_sds((l, X_WIDTH)), _sds((m_len, X_WIDTH)), _sds((m_len, X_WIDTH)), _sds((1, LANE))],
        out_specs=[_rt(tm, X_WIDTH), _full((m_len, X_WIDTH)), _full((m_len, X_WIDTH)), _full((1, LANE))], n_acc=3)

    def memkv_b(i, mm_, dk, dv, mg, wmk, xkg):
        memn, vjp_n = jax.vjp(lambda a, b: _rms(a, b, D_MODEL), mm_, mg)
        kv = _mm(memn, wmk)
        _, vjp_k = jax.vjp(lambda a, b: _head_rms(a, b, X_HEADS, X_HEAD_DIM), kv[:, :X_WIDTH], xkg)
        dkr, dxkg = vjp_k(dk)
        dkv = jnp.concatenate([dkr, dv], axis=-1)
        _, dmg = vjp_n(_mm_nt(dkv, wmk))
        return _mm_tn(memn, dkv), dmg, dxkg

    mem_consts = [p['mem_norm_g'], p['w_mem_kv'], p['xk_norm_g']]
    g['w_mem_kv'], g['mem_norm_g'], g['xk_norm_g'] = _rows(
        memkv_b, name=name + "_b_memkv", n=1, ins=[mem, dk_c, dv_c, *mem_consts],
        in_specs=[_full(mem.shape), _full(dk_c.shape), _full(dv_c.shape)] + [_full(a.shape) for a in mem_consts],
        outs=[_sds((D_MODEL, 2 * X_WIDTH)), _sds((1, D_MODEL)), _sds((1, LANE))],
        out_specs=[_full((D_MODEL, 2 * X_WIDTH)), _full((1, D_MODEL)), _full((1, LANE))])

    u_p = sv['u_p']
    dzo_p = _to_perm(dzo, l)
    s_re, s_im = sv['s_re'], sv['s_im']

    def glu_b(i, sr, si, u, dz, cm, dsk, wg, bg):
        cats = [jnp.concatenate([sr[:, j * 512:(j + 1) * 512], si[:, j * 512:(j + 1) * 512]], axis=-1) for j in range(SSM_JB)]
        y = jnp.concatenate([_mm(cats[j], cm[j]) for j in range(SSM_JB)], axis=-1) + dsk * u
        zz, vjp_g = jax.vjp(_gelu, y)
        t = _mm(zz, wg) + bg
        sg = jax.nn.sigmoid(t)
        dt = dz * zz * sg * (1.0 - sg)
        dzz = dz * sg + _mm_nt(dt, wg)
        dy = vjp_g(dzz)[0]
        dss = [_mm_nt(dy[:, j * LANE:(j + 1) * LANE], cm[j]) for j in range(SSM_JB)]
        dsr = jnp.concatenate([d[:, :512] for d in dss], axis=-1)
        dsi = jnp.concatenate([d[:, 512:] for d in dss], axis=-1)
        dcm = jnp.stack([_mm_tn(cats[j], dy[:, j * LANE:(j + 1) * LANE]) for j in range(SSM_JB)], axis=0)
        return dsr, dsi, dy * dsk, dcm, _colsum(dy * u), _mm_tn(zz, dt), _colsum(dt)

    glu_consts = [p['c_mat'], p['ssm_d'], p['w_glu'], p['b_glu']]
    ts = min(256, l)
    nts = l // ts
    ds_re, ds_im, du_dir, g['c_mat'], g['ssm_d'], g['w_glu'], g['b_glu'] = _rows(
        glu_b, name=name + "_b_glu", n=nts, ins=[s_re, s_im, u_p, dzo_p, *glu_consts],
        in_specs=[_rt(ts, SSM_LANES), _rt(ts, SSM_LANES), _rt(ts, SSM_WIDTH), _rt(ts, SSM_WIDTH)] + [_full(a.shape) for a in glu_consts],
        outs=[_sds((l, SSM_LANES)), _sds((l, SSM_LANES)), _sds((l, SSM_WIDTH)), _sds((SSM_JB, 1024, LANE)), _sds((1, SSM_WIDTH)),
              _sds((SSM_WIDTH, SSM_WIDTH)), _sds((1, SSM_WIDTH))],
        out_specs=[_rt(ts, SSM_LANES), _rt(ts, SSM_LANES), _rt(ts, SSM_WIDTH), _full((SSM_JB, 1024, LANE)), _full((1, SSM_WIDTH)),
                   _full((SSM_WIDTH, SSM_WIDTH)), _full((1, SSM_WIDTH))], n_acc=4)
    gb_re, gb_im = _scan(name + "_b_scan", ds_re, ds_im, sv['a_re'], -sv['a_im'], reverse=True)
    ns = SCAN_SEGS
    last_blk = l // ns - 1

    def da_fn(i, gr, gi, sr, si, hr, hi, lr_, li_):
        rid = lax.broadcasted_iota(jnp.int32, lr_.shape, 0)
        fr = jnp.where(rid == 0, 0.0, pltpu.roll(lr_, 1, 0))
        fi = jnp.where(rid == 0, 0.0, pltpu.roll(li_, 1, 0))
        hr = jnp.where(i == 0, fr, hr)
        hi = jnp.where(i == 0, fi, hi)
        if ts > ns:
            pr = jnp.concatenate([hr, sr[:ts - ns]], axis=0)
            pi = jnp.concatenate([hi, si[:ts - ns]], axis=0)
        else:
            pr, pi = hr, hi
        return _colsum(gr * pr + gi * pi), _colsum(gi * pr - gr * pi)

    hprev = pl.BlockSpec((ns, SSM_LANES), lambda i: (jnp.maximum(i * (ts // ns) - 1, 0), 0))
    hlast = pl.BlockSpec((ns, SSM_LANES), lambda i: (last_blk, 0))
    da_re, da_im = _rows(da_fn, name=name + "_b_da", n=nts, ins=[gb_re, gb_im, s_re, s_im, s_re, s_im, s_re, s_im],
                         in_specs=[_rt(ts, SSM_LANES)] * 4 + [hprev, hprev, hlast, hlast],
                         outs=[_sds((1, SSM_LANES))] * 2, out_specs=[_full((1, SSM_LANES))] * 2, n_acc=2)

    def bu_b(i, dbr, dbi, u, dud, bm):
        dus, dbm = [], []
        for j in range(SSM_JB):
            cat = jnp.concatenate([dbr[:, j * 512:(j + 1) * 512], dbi[:, j * 512:(j + 1) * 512]], axis=-1)
            dus.append(_mm_nt(cat, bm[j]))
            dbm.append(_mm_tn(u[:, j * LANE:(j + 1) * LANE], cat))
        return dud + jnp.concatenate(dus, axis=-1), jnp.stack(dbm, axis=0)

    du_p, d_bmat = _rows(bu_b, name=name + "_b_bu", n=nts, ins=[gb_re, gb_im, u_p, du_dir, sv['b_mat']],
                         in_specs=[_rt(ts, SSM_LANES), _rt(ts, SSM_LANES), _rt(ts, SSM_WIDTH), _rt(ts, SSM_WIDTH),
                                   _full(sv['b_mat'].shape)],
                         outs=[_sds((l, SSM_WIDTH)), _sds((SSM_JB, LANE, 1024))],
                         out_specs=[_rt(ts, SSM_WIDTH), _full((SSM_JB, LANE, 1024))], n_acc=1)
    dp_u = _from_perm(du_p, l)
    dbb_re = _blockdiag_t(d_bmat[:, :, :512], SSM_GROUP_CH, SSM_STATE).reshape(SSM_GROUPS, SSM_GROUP_CH, SSM_STATE).transpose(1, 0, 2)
    dbb_im = _blockdiag_t(d_bmat[:, :, 512:], SSM_GROUP_CH, SSM_STATE).reshape(SSM_GROUPS, SSM_GROUP_CH, SSM_STATE).transpose(1, 0, 2)
    g['lr'], g['li'], g['log_dt'], g['br'], g['bi'] = _ssm_params_bwd(
        name + "_b_ssm_par", p['lr'], p['li'], p['log_dt'], p['br'], p['bi'],
        da_re.reshape(SSM_GROUPS, SSM_STATE), da_im.reshape(SSM_GROUPS, SSM_STATE), dbb_re, dbb_im)

    dq, dk, dv = _flash_bwd(name + "_b_attn", sv['q'], sv['k'], sv['v'], sv['o_a'], sv['lse'], do_a, D_QK ** -0.5)

    def qkv_b(i, ps, c, s1, s2, dq_, dk_, dv_, qag, wqb, kvag, wk, wv, qng, kng):
        c_q = ps[:, :Q_LORA]
        c_kv = ps[:, Q_LORA:Q_LORA + KV_LORA]
        kr = ps[:, Q_LORA + KV_LORA:]
        cqn, vjp_cq = jax.vjp(lambda a, b: _rms(a, b, Q_LORA), c_q, qag)
        ckvn, vjp_ckv = jax.vjp(lambda a, b: _rms(a, b, KV_LORA), c_kv, kvag)
        q_raw = _mm(cqn, wqb)
        k_raw = _mm(ckvn, wk) + jnp.concatenate([kr] * MLA_HEADS, axis=-1)
        _, vjp_qn = jax.vjp(lambda a, b: _head_rms(a, b, MLA_HEADS, D_QK), q_raw, qng)
        _, vjp_kn = jax.vjp(lambda a, b: _head_rms(a, b, MLA_HEADS, D_QK), k_raw, kng)
        dq_raw, dqng = vjp_qn(_heads(_rope_t, dq_, MLA_HEADS, c, s1, s2))
        dk_raw, dkng = vjp_kn(_heads(_rope_t, dk_, MLA_HEADS, c, s1, s2))
        dkr = dk_raw[:, :LANE]
        for h in range(1, MLA_HEADS):
            dkr = dkr + dk_raw[:, h * LANE:(h + 1) * LANE]
        dcq, dqag = vjp_cq(_mm_nt(dq_raw, wqb))
        dckv, dkvag = vjp_ckv(_mm_nt(dk_raw, wk) + _mm_nt(dv_, wv))
        dps = jnp.concatenate([dcq, dckv, dkr], axis=-1)
        return (dps, _mm_tn(cqn, dq_raw), _mm_tn(ckvn, dk_raw), _mm_tn(ckvn, dv_), dqag, dkvag, dqng, dkng)

    qkv_consts = [p['q_a_norm_g'], p['w_qb'], p['kv_a_norm_g'], p['w_k'], p['w_v'], p['q_norm_g'], p['k_norm_g']]
    (dp_s, g['w_qb'], g['w_k'], g['w_v'], g['q_a_norm_g'], g['kv_a_norm_g'], g['q_norm_g'], g['k_norm_g']) = _rows(
        qkv_b, name=name + "_b_qkv", n=nt, ins=[sv['p_s'], *tabs, dq, dk, dv, *qkv_consts],
        in_specs=[_rt(tm, SMALL_W)] + [_rt(tm, LANE)] * 3 + [_rt(tm, MLA_PAD)] * 3 + [_full(a.shape) for a in qkv_consts],
        outs=[_sds((l, SMALL_W)), _sds((Q_LORA, MLA_PAD)), _sds((KV_LORA, MLA_PAD)), _sds((KV_LORA, MLA_PAD)),
              _sds((1, Q_LORA)), _sds((1, KV_LORA)), _sds((1, LANE)), _sds((1, LANE))],
        out_specs=[_rt(tm, SMALL_W), _full((Q_LORA, MLA_PAD)), _full((KV_LORA, MLA_PAD)), _full((KV_LORA, MLA_PAD)),
                   _full((1, Q_LORA)), _full((1, KV_LORA)), _full((1, LANE)), _full((1, LANE))], n_acc=7)

    x0 = sv['x0']
    dh = _matmul(name + "_b_in", [(dp_g, p['w_g']), (dp_u, p['w_u']), (dp_xq, p['w_xq']), (dp_s, p['w_s'])], l, D_MODEL, nt=True,
                 tm=256)
    gm = p['norm_mix_g']
    g['w_g'] = _matmul_tn(name + "_gw_g", x0, dp_g, rms_gain=gm)
    g['w_u'] = _matmul_tn(name + "_gw_u", x0, dp_u, rms_gain=gm)
    g['w_xq'] = _matmul_tn(name + "_gw_xq", x0, dp_xq, rms_gain=gm)
    g['w_s'] = _matmul_tn(name + "_gw_s", x0, dp_s, rms_gain=gm)
    dx0, g['norm_mix_g'] = _rows(norm_b, name=name + "_b_norm1", n=nt, ins=[x0, dh, dx1, gm],
                                 in_specs=[_rt(tm, D_MODEL)] * 3 + [_full((1, D_MODEL))],
                                 outs=[_sds((l, D_MODEL)), _sds((1, D_MODEL))], out_specs=[_rt(tm, D_MODEL), _full((1, D_MODEL))],
                                 n_acc=1)
    return dx0, g


def _unprep_grads(g):
    o = {}
    ws = g['w_s']
    o['w_in'] = jnp.concatenate([ws[:, :Q_LORA + KV_LORA], ws[:, Q_LORA + KV_LORA + D_NOPE:Q_LORA + KV_LORA + D_QK],
                                 g['w_u'], g['w_xq'], g['w_g']], axis=1)
    o['w_q_b'] = g['w_qb'].reshape(Q_LORA, MLA_HEADS, HEAD_PAD)[:, :, :D_QK].reshape(Q_LORA, MLA_HEADS * D_QK)
    gk = g['w_k'].reshape(KV_LORA, MLA_HEADS, HEAD_PAD)[:, :, :D_NOPE]
    gv = g['w_v'].reshape(KV_LORA, MLA_HEADS, HEAD_PAD)[:, :, :D_V]
    o['w_kv_b'] = jnp.concatenate([gk, gv], axis=2).reshape(KV_LORA, MLA_HEADS * (D_NOPE + D_V))
    o['w_o_mla'] = g['w_oa'].reshape(MLA_HEADS, HEAD_PAD, D_MODEL)[:, :D_V].reshape(MLA_HEADS * D_V, D_MODEL)
    for n in ('w_glu', 'w_o_ssm', 'w_mem_kv', 'w_o_cross', 'w_out', 'w_up', 'w_down', 'conv_w'):
        o[n] = g[n]
    for n in ('norm_mix_g', 'q_a_norm_g', 'kv_a_norm_g', 'b_glu', 'mem_norm_g', 'xq_norm_g', 'xk_norm_g', 'b_gate',
              'norm_ffn_g', 'conv_b'):
        o[n] = g[n].reshape(-1)
    o['q_norm_g'] = g['q_norm_g'].reshape(-1)[:D_QK]
    o['k_norm_g'] = g['k_norm_g'].reshape(-1)[:D_QK]
    o['ssm_d'] = g['ssm_d'].reshape(SSM_GROUPS, SSM_GROUP_CH)
    o['ssm_lambda_re'] = g['lr']
    o['ssm_lambda_im'] = g['li']
    o['ssm_log_dt'] = g['log_dt'].reshape(SSM_GROUPS)
    o['ssm_b_re'] = g['br'].transpose(1, 2, 0)
    o['ssm_b_im'] = g['bi'].transpose(1, 2, 0)
    dc = g['c_mat']
    o['ssm_c_re'] = _blockdiag_t(dc[:, :512], SSM_STATE, SSM_GROUP_CH).transpose(0, 1, 3, 2).reshape(SSM_GROUPS, SSM_GROUP_CH, SSM_STATE)
    o['ssm_c_im'] = -_blockdiag_t(dc[:, 512:], SSM_STATE, SSM_GROUP_CH).transpose(0, 1, 3, 2).reshape(SSM_GROUPS, SSM_GROUP_CH, SSM_STATE)
    return o


def _local_step(x, mem, pos, target, w):
    l = x.shape[0]
    tm = min(512, l)
    tabs = _rope_tables(pos.astype(F32).reshape(l, 1))
    ps = [_prep_layer(w, i) for i in range(DEPTH)]
    saved = []
    h = x
    for i in range(DEPTH):
        h, sv = _layer_fwd("l%d" % i, h, tabs, mem, ps[i])
        saved.append(sv)

    def loss_fn(i, y, t):
        e = y - t
        per_tok = jnp.sum(e * e, axis=-1, keepdims=True) * (1.0 / D_MODEL)
        tot = 0.5 * jnp.sum(per_tok, axis=0, keepdims=True)
        return e * (1.0 / D_MODEL), jnp.broadcast_to(tot, (1, LANE))

    dy, loss = _rows(loss_fn, name="loss", n=l // tm, ins=[h, target], in_specs=[_rt(tm, D_MODEL)] * 2,
                     outs=[_sds((l, D_MODEL)), _sds((1, LANE))], out_specs=[_rt(tm, D_MODEL), _full((1, LANE))], n_acc=1)
    grads = []
    d = dy
    for i in reversed(range(DEPTH)):
        d, g = _layer_bwd("l%d" % i, d, saved[i], tabs, mem, ps[i])
        grads.append(_unprep_grads(g))
    grads = grads[::-1]
    full = {n: jnp.stack([grads[i][n] for i in range(DEPTH)], axis=0) for n in WEIGHTS}
    return loss[0, 0], d, full


def _sum_slots(name, y, n_slots):
    r = y.shape[1]
    tr = _row_tile(r)

    def fn(i, v):
        acc = v[0].astype(F32)
        for k in range(1, n_slots):
            acc = acc + v[k].astype(F32)
        return acc

    return _rows(fn, name=name, n=r // tr, ins=[y], in_specs=[pl.BlockSpec((n_slots, tr, PACK_W), lambda i: (0, i, 0))],
                 outs=[_sds((r, PACK_W))], out_specs=[_rt(tr, PACK_W)])[0]


def _row_tile(r):
    for t in (256, 128, 64, 32, 16, 8):
        if r % t == 0:
            return t
    return r


def _adamw(name, parts, n_slots, w, m, v):
    r = w.shape[0]
    tr = _row_tile(r)

    def fn(i, pv, wv, mv, vv):
        g = pv[0]
        for k in range(1, n_slots):
            g = g + pv[k]
        mn = ADAM_B1 * mv + (1.0 - ADAM_B1) * g
        vn = ADAM_B2 * vv + (1.0 - ADAM_B2) * (g * g)
        m_hat = mn / (1.0 - ADAM_B1 ** ADAM_STEP)
        v_hat = vn / (1.0 - ADAM_B2 ** ADAM_STEP)
        delta = -ADAM_LR * (m_hat / (jnp.sqrt(v_hat) + ADAM_EPS) + ADAM_WD * wv)
        return g, delta, mn, vn

    return _rows(fn, name=name, n=r // tr, ins=[parts, w, m, v],
                 in_specs=[pl.BlockSpec((n_slots, tr, PACK_W), lambda i: (0, i, 0))] + [_rt(tr, PACK_W)] * 3,
                 outs=[_sds((r, PACK_W))] * 4, out_specs=[_rt(tr, PACK_W)] * 4)


def _shard_of(a, axis, k):
    n = a.shape[axis] // 4
    return lax.slice_in_dim(a, k * n, (k + 1) * n, axis=axis)


def _step(a):
    mc = lax.axis_index("c")
    x = a['x'][0]
    mem = a['mem'][0]
    pos = a['positions'][0]
    target = a['loss_target'][0]

    shard_shapes = [a[n].shape for n in GATHER_BF16]
    mine = _pack([a[n] for n in GATHER_BF16], 512, BF16)
    rh = mine.shape[0] // 2
    half = lax.dynamic_slice_in_dim(mine, mc * rh, rh, axis=0)
    halves = _chip_exchange("comm_gather_ici", half, per_peer=False)
    both = _core_exchange("comm_gather_d2d", halves)
    w = {}
    per_chip = [_unpack(jnp.concatenate([both[0, k], both[1, k]], axis=0), shard_shapes) for k in range(4)]
    for j, n in enumerate(GATHER_BF16):
        w[n] = jnp.concatenate([per_chip[k][j] for k in range(4)], axis=SHARD_AXIS[n])
    cw = _chip_exchange("comm_gather_conv", _pack([a['conv_w']], 8, F32), per_peer=False)
    w['conv_w'] = jnp.concatenate([_unpack(cw[k], [a['conv_w'].shape])[0] for k in range(4)], axis=2)
    for n in SMALL:
        w[n] = a[n]

    loss, grad_x, full = _local_step(x, mem, pos, target, w)
    loss = lax.psum(loss, ("x", "y", "c"))

    gsh = jnp.stack([_pack([_shard_of(full[n], SHARD_AXIS[n], k) for n in SHARDED], 256, BF16) for k in range(4)], axis=0)
    got = _chip_exchange("comm_reduce_ici", gsh, per_peer=True)
    part = _sum_slots("reduce_chips", got, 4)
    pair = _core_exchange("comm_reduce_d2d", part)
    sh_shapes = [a[n].shape for n in SHARDED]
    res_sh = _adamw("adamw_sharded", pair, 2, *[_pack([a[pre + n] for n in SHARDED], 256, F32) for pre in ('', 'm_', 'v_')])
    res_sh = [_unpack(r, sh_shapes) for r in res_sh]

    sm_shapes = [a[n].shape for n in SMALL]
    gsm = _pack([full[n] for n in SMALL], 8, F32)
    alls = _all_exchange("comm_reduce_small", gsm)
    res_sm = _adamw("adamw_small", alls, 8, *[_pack([a[pre + n] for n in SMALL], 8, F32) for pre in ('', 'm_', 'v_')])
    res_sm = [_unpack(r, sm_shapes) for r in res_sm]

    outs = [loss, grad_x[None]]
    for kind in range(4):
        byname = dict(zip(SHARDED, res_sh[kind]))
        byname.update(zip(SMALL, res_sm[kind]))
        outs += [byname[n] for n in WEIGHTS]
    return tuple(outs)


def kernel(x, mem, positions, norm_mix_g, w_in, q_a_norm_g, w_q_b, kv_a_norm_g, w_kv_b, q_norm_g, k_norm_g, w_o_mla, ssm_lambda_re, ssm_lambda_im, ssm_log_dt, ssm_b_re, ssm_b_im, ssm_c_re, ssm_c_im, ssm_d, w_glu, b_glu, w_o_ssm, mem_norm_g, w_mem_kv, xq_norm_g, xk_norm_g, w_o_cross, b_gate, w_out, norm_ffn_g, w_up, conv_w, conv_b, w_down, loss_target, m_norm_mix_g, m_w_in, m_q_a_norm_g, m_w_q_b, m_kv_a_norm_g, m_w_kv_b, m_q_norm_g, m_k_norm_g, m_w_o_mla, m_ssm_lambda_re, m_ssm_lambda_im, m_ssm_log_dt, m_ssm_b_re, m_ssm_b_im, m_ssm_c_re, m_ssm_c_im, m_ssm_d, m_w_glu, m_b_glu, m_w_o_ssm, m_mem_norm_g, m_w_mem_kv, m_xq_norm_g, m_xk_norm_g, m_w_o_cross, m_b_gate, m_w_out, m_norm_ffn_g, m_w_up, m_conv_w, m_conv_b, m_w_down, v_norm_mix_g, v_w_in, v_q_a_norm_g, v_w_q_b, v_kv_a_norm_g, v_w_kv_b, v_q_norm_g, v_k_norm_g, v_w_o_mla, v_ssm_lambda_re, v_ssm_lambda_im, v_ssm_log_dt, v_ssm_b_re, v_ssm_b_im, v_ssm_c_re, v_ssm_c_im, v_ssm_d, v_w_glu, v_b_glu, v_w_o_ssm, v_mem_norm_g, v_w_mem_kv, v_xq_norm_g, v_xk_norm_g, v_w_o_cross, v_b_gate, v_w_out, v_norm_ffn_g, v_w_up, v_conv_w, v_conv_b, v_w_down):
    return _step(dict(locals()))
```

```python
import functools
import math

import numpy as np
import jax
import jax.numpy as jnp
from jax import lax
from jax.experimental import pallas as pl
from jax.experimental.pallas import tpu as pltpu

F32 = jnp.float32
BF16 = jnp.bfloat16
MESH = pl.DeviceIdType.MESH

DEPTH = 2
D_MODEL = 1024
EPS = 1e-6
MLA_HEADS = 8
Q_LORA = 384
KV_LORA = 256
D_NOPE = 64
D_ROPE = 32
D_QK = D_NOPE + D_ROPE
D_V = 64
HEAD_PAD = 128
MLA_PAD = MLA_HEADS * HEAD_PAD
ROPE_THETA = 10000.0
SSM_GROUPS = 32
SSM_GROUP_CH = 16
SSM_WIDTH = 512
SSM_STATE = 64
SSM_LANES = SSM_GROUPS * SSM_STATE
SSM_JB = 4
X_HEADS = 4
X_HEAD_DIM = 128
X_WIDTH = 512
D_FF = 2816
SMALL_W = Q_LORA + KV_LORA + HEAD_PAD
SCAN_SEGS = 32
LANE = 128
NEG = -1e30

ADAM_LR = 0.001
ADAM_B1 = 0.9
ADAM_B2 = 0.999
ADAM_EPS = 1e-08
ADAM_WD = 0.01
ADAM_STEP = 10

WEIGHTS = ['norm_mix_g', 'w_in', 'q_a_norm_g', 'w_q_b', 'kv_a_norm_g', 'w_kv_b', 'q_norm_g', 'k_norm_g', 'w_o_mla',
           'ssm_lambda_re', 'ssm_lambda_im', 'ssm_log_dt', 'ssm_b_re', 'ssm_b_im', 'ssm_c_re', 'ssm_c_im', 'ssm_d',
           'w_glu', 'b_glu', 'w_o_ssm', 'mem_norm_g', 'w_mem_kv', 'xq_norm_g', 'xk_norm_g', 'w_o_cross', 'b_gate',
           'w_out', 'norm_ffn_g', 'w_up', 'conv_w', 'conv_b', 'w_down']
SHARD_AXIS = {'w_in': 2, 'w_q_b': 2, 'w_kv_b': 2, 'w_o_mla': 2, 'w_glu': 1, 'w_o_ssm': 2, 'w_mem_kv': 1,
              'w_o_cross': 2, 'w_out': 1, 'w_up': 2, 'conv_w': 2, 'w_down': 1}
SHARDED = [n for n in WEIGHTS if n in SHARD_AXIS]
GATHER_BF16 = [n for n in SHARDED if n != 'conv_w']
SMALL = [n for n in WEIGHTS if n not in SHARD_AXIS]


def _bf(v):
    return v.astype(BF16)


def _mm(a, b):
    return jnp.dot(_bf(a), _bf(b), preferred_element_type=F32)


def _mm_nt(a, b):
    return lax.dot_general(_bf(a), _bf(b), (((1,), (1,)), ((), ())), preferred_element_type=F32)


def _mm_tn(a, b):
    return lax.dot_general(_bf(a), _bf(b), (((0,), (0,)), ((), ())), preferred_element_type=F32)


def _rms(v, g, n):
    ms = jnp.sum(v * v, axis=-1, keepdims=True) * (1.0 / n)
    return (v * lax.rsqrt(ms + EPS)) * g


def _head_rms(v, g, heads, n):
    return jnp.concatenate([_rms(v[:, h * LANE:(h + 1) * LANE], g, n) for h in range(heads)], axis=-1)


def _rope(v, c, s1, s2):
    return v * c + pltpu.roll(v, LANE - 16, 1) * s1 + pltpu.roll(v, 16, 1) * s2


def _rope_t(g, c, s1, s2):
    return g * c + pltpu.roll(g * s1, 16, 1) + pltpu.roll(g * s2, LANE - 16, 1)


def _heads(fn, v, heads, *tabs):
    return jnp.concatenate([fn(v[:, h * LANE:(h + 1) * LANE], *tabs) for h in range(heads)], axis=-1)


def _gelu(y):
    return y * (0.5 * (1.0 + jnp.tanh(math.sqrt(2.0 / math.pi) * (y + 0.044715 * (y * y * y)))))


def _silu(g):
    return g * jax.nn.sigmoid(g)


def _colsum(v):
    return jnp.sum(v, axis=0, keepdims=True)


def _row_select(rows, n):
    rid = lax.broadcasted_iota(jnp.int32, (n, rows[0].shape[-1]), 0)
    out = jnp.zeros((n, rows[0].shape[-1]), F32)
    for k, r in enumerate(rows):
        out = jnp.where(rid == k, jnp.broadcast_to(r, out.shape), out)
    return out


def _params(sem, vmem_mb):
    return pltpu.CompilerParams(dimension_semantics=sem, vmem_limit_bytes=vmem_mb * 1024 * 1024)


def _rt(tm, w, cb=0):
    return pl.BlockSpec((tm, w), lambda i: (i, cb))


def _full(shape):
    nd = len(shape)
    return pl.BlockSpec(tuple(shape), lambda i: (0,) * nd)


def _rows(fn, *, name, n, ins, in_specs, outs, out_specs, n_acc=0, vmem=48):
    n_in = len(ins)
    n_out = len(outs)

    def body(*refs):
        i = pl.program_id(0)
        res = fn(i, *[r[...] for r in refs[:n_in]])
        if not isinstance(res, (tuple, list)):
            res = (res,)
        assert len(res) == n_out, (name, len(res), n_out)
        for k, (r, v) in enumerate(zip(refs[n_in:], res)):
            if k < n_out - n_acc:
                r[...] = v.astype(r.dtype)
            else:
                @pl.when(i == 0)
                def _():
                    r[...] = v

                @pl.when(i > 0)
                def _():
                    r[...] += v

    return pl.pallas_call(
        body, name=name, grid=(n,), in_specs=list(in_specs), out_specs=tuple(out_specs), out_shape=tuple(outs),
        compiler_params=_params(("arbitrary",), vmem))(*ins)


def _sds(shape, dtype=F32):
    return jax.ShapeDtypeStruct(tuple(shape), dtype)


def _tile_n(n, cap=1024):
    best = None
    for t in range(LANE, min(n, cap) + 1, LANE):
        if n % t == 0:
            best = t
    if best is None or n <= 1408:
        return n
    return best


def _matmul(name, pairs, m, n, *, nt=False, rms_gain=None, resid=None, out_dtype=F32, tm=512, vmem=56):
    tm = min(tm, m)
    tn = _tile_n(n)
    ks = [a.shape[1] for a, _ in pairs]
    np_ = len(pairs)

    def body(*refs):
        a_refs = refs[:np_]
        b_refs = refs[np_:2 * np_]
        k = 2 * np_
        g_ref = None
        r_ref = None
        if rms_gain is not None:
            g_ref = refs[k]
            k += 1
        if resid is not None:
            r_ref = refs[k]
            k += 1
        o_ref = refs[k]
        scr = refs[k + 1:]
        j = pl.program_id(1)

        @pl.when(j == 0)
        def _():
            for p in range(np_):
                a = a_refs[p][...]
                if p == 0 and g_ref is not None:
                    a = _rms(a.astype(F32), g_ref[...], ks[0])
                scr[p][...] = a.astype(BF16)

        acc = None
        for p in range(np_):
            b = b_refs[p][...].astype(BF16)
            if nt:
                t = lax.dot_general(scr[p][...], b, (((1,), (1,)), ((), ())), preferred_element_type=F32)
            else:
                t = jnp.dot(scr[p][...], b, preferred_element_type=F32)
            acc = t if acc is None else acc + t
        if r_ref is not None:
            acc = acc + r_ref[...]
        o_ref[...] = acc.astype(o_ref.dtype)

    in_specs = [pl.BlockSpec((tm, kk), lambda i, j: (i, 0)) for kk in ks]
    if nt:
        in_specs += [pl.BlockSpec((tn, kk), lambda i, j: (j, 0)) for kk in ks]
    else:
        in_specs += [pl.BlockSpec((kk, tn), lambda i, j: (0, j)) for kk in ks]
    ins = [a for a, _ in pairs] + [b for _, b in pairs]
    if rms_gain is not None:
        in_specs.append(pl.BlockSpec((1, ks[0]), lambda i, j: (0, 0)))
        ins.append(rms_gain)
    if resid is not None:
        in_specs.append(pl.BlockSpec((tm, tn), lambda i, j: (i, j)))
        ins.append(resid)
    return pl.pallas_call(
        body, name=name, grid=(m // tm, n // tn), in_specs=in_specs,
        out_specs=pl.BlockSpec((tm, tn), lambda i, j: (i, j)), out_shape=_sds((m, n), out_dtype),
        scratch_shapes=[pltpu.VMEM((tm, kk), BF16) for kk in ks],
        compiler_params=_params(("arbitrary", "arbitrary"), vmem))(*ins)


def _matmul_tn(name, a, b, *, rms_gain=None, tl=512, vmem=56):
    l, ka = a.shape
    n = b.shape[1]
    tl = min(tl, l)
    tn = _tile_n(n, 512)

    def body(*refs):
        if rms_gain is not None:
            a_ref, b_ref, g_ref, o_ref = refs
        else:
            a_ref, b_ref, o_ref = refs
        t = pl.program_id(1)
        av = a_ref[...]
        if rms_gain is not None:
            av = _rms(av.astype(F32), g_ref[...], ka)
        v = _mm_tn(av, b_ref[...])

        @pl.when(t == 0)
        def _():
            o_ref[...] = v

        @pl.when(t > 0)
        def _():
            o_ref[...] += v

    in_specs = [pl.BlockSpec((tl, ka), lambda j, t: (t, 0)), pl.BlockSpec((tl, tn), lambda j, t: (t, j))]
    ins = [a, b]
    if rms_gain is not None:
        in_specs.append(pl.BlockSpec((1, ka), lambda j, t: (0, 0)))
        ins.append(rms_gain)
    return pl.pallas_call(
        body, name=name, grid=(n // tn, l // tl), in_specs=in_specs,
        out_specs=pl.BlockSpec((ka, tn), lambda j, t: (0, j)), out_shape=_sds((ka, n)),
        compiler_params=_params(("arbitrary", "arbitrary"), vmem))(*ins)


ATT_HEADS_PER_STEP = 2
ATT_W = ATT_HEADS_PER_STEP * LANE
ATT_GROUPS = MLA_HEADS // ATT_HEADS_PER_STEP
LOG2E = math.log2(math.e)
ATT_SCALE = D_QK ** -0.5
ATT_QSCALE = ATT_SCALE * LOG2E


def _tri_tables(nq, by_k):
    qs, ks = [], []
    if by_k:
        for ki in range(nq):
            for qi in range(ki, nq):
                qs.append(qi)
                ks.append(ki)
    else:
        for qi in range(nq):
            for ki in range(qi + 1):
                qs.append(qi)
                ks.append(ki)
    return jnp.asarray(np.array(qs, np.int32)), jnp.asarray(np.array(ks, np.int32))


def _causal_keep(shape, transposed):
    r = lax.broadcasted_iota(jnp.int32, shape, 0)
    c = lax.broadcasted_iota(jnp.int32, shape, 1)
    return (r <= c) if transposed else (c <= r)


def _nt16(a, b):
    return lax.dot_general(a, b, (((1,), (1,)), ((), ())), preferred_element_type=F32)


def _row_form(col):
    return jnp.transpose(jnp.broadcast_to(col, (col.shape[0], LANE)))[:8]


def _att_call(body, name, l, tq, tabs, ins, in_specs, outs, out_specs, scratch=()):
    grid_spec = pltpu.PrefetchScalarGridSpec(
        num_scalar_prefetch=2, grid=(ATT_GROUPS, tabs[0].shape[0]), in_specs=in_specs, out_specs=out_specs,
        scratch_shapes=list(scratch))
    return pl.pallas_call(body, name=name, grid_spec=grid_spec, out_shape=outs,
                          compiler_params=_params(("arbitrary", "arbitrary"), 48))(*tabs, *ins)


def _flash_fwd(name, q, k, v):
    l = q.shape[0]
    tq = min(512, l)
    nq = l // tq
    tabs = _tri_tables(nq, by_k=False)

    def body(qt, kt, q_ref, k_ref, v_ref, o_ref, lse_ref, lset_ref, m_s, l_s, acc_s):
        t = pl.program_id(1)
        qi = qt[t]
        ki = kt[t]

        @pl.when(ki == 0)
        def _():
            m_s[...] = jnp.full(m_s.shape, NEG, F32)
            l_s[...] = jnp.zeros(l_s.shape, F32)
            acc_s[...] = jnp.zeros(acc_s.shape, F32)

        def step(masked):
            for h in range(ATT_HEADS_PER_STEP):
                sl = slice(h * LANE, (h + 1) * LANE)
                s = _nt16(q_ref[:, sl], k_ref[:, sl])
                if masked:
                    s = jnp.where(_causal_keep(s.shape, False), s, NEG)
                m_old = m_s[h]
                m_new = jnp.maximum(m_old, jnp.max(s, axis=-1, keepdims=True))
                alpha = jnp.exp2(m_old - m_new)
                p = jnp.exp2(s - m_new)
                l_s[h] = alpha * l_s[h] + jnp.sum(p, axis=-1, keepdims=True)
                acc_s[:, sl] = alpha * acc_s[:, sl] + jnp.dot(p.astype(BF16), v_ref[:, sl], preferred_element_type=F32)
                m_s[h] = m_new

        @pl.when(ki < qi)
        def _():
            step(False)

        @pl.when(ki == qi)
        def _():
            step(True)
            for h in range(ATT_HEADS_PER_STEP):
                sl = slice(h * LANE, (h + 1) * LANE)
                o_ref[:, sl] = acc_s[:, sl] / l_s[h]
                lse = m_s[h] + jnp.log2(l_s[h])
                lse_ref[:, sl] = jnp.broadcast_to(lse, (tq, LANE))
                lset_ref[h * 8:(h + 1) * 8, :] = _row_form(lse)

    qspec = pl.BlockSpec((tq, ATT_W), lambda g, t, qt, kt: (qt[t], g))
    kspec = pl.BlockSpec((tq, ATT_W), lambda g, t, qt, kt: (kt[t], g))
    rspec = pl.BlockSpec((8 * ATT_HEADS_PER_STEP, tq), lambda g, t, qt, kt: (g, qt[t]))
    return _att_call(
        body, name, l, tq, tabs, [q, k, v], [qspec, kspec, kspec],
        (_sds((l, MLA_PAD)), _sds((l, MLA_PAD)), _sds((8 * MLA_HEADS, l))), (qspec, qspec, rspec),
        scratch=[pltpu.VMEM((ATT_HEADS_PER_STEP, tq, 1), F32), pltpu.VMEM((ATT_HEADS_PER_STEP, tq, 1), F32),
                 pltpu.VMEM((tq, ATT_W), F32)])


def _flash_bwd(name, q, k, v, o, lse, lse_t, do):
    l = q.shape[0]
    tq = min(512, l)
    nq = l // tq

    def delta_fn(i, dov, ov):
        cols, rows = [], []
        for h in range(MLA_HEADS):
            sl = slice(h * LANE, (h + 1) * LANE)
            d = jnp.sum(dov[:, sl] * ov[:, sl], axis=-1, keepdims=True)
            cols.append(jnp.broadcast_to(d, (tq, LANE)))
            rows.append(_row_form(d))
        return jnp.concatenate(cols, axis=-1), jnp.concatenate(rows, axis=0), dov

    delta, delta_t, do16 = _rows(
        delta_fn, name=name + "_delta", n=nq, ins=[do, o], in_specs=[_rt(tq, MLA_PAD)] * 2,
        outs=[_sds((l, MLA_PAD)), _sds((8 * MLA_HEADS, l)), _sds((l, MLA_PAD), BF16)],
        out_specs=[_rt(tq, MLA_PAD), pl.BlockSpec((8 * MLA_HEADS, tq), lambda i: (0, i)), _rt(tq, MLA_PAD)])

    def dq_body(qt, kt, q_ref, k_ref, v_ref, do_ref, lse_ref, dl_ref, dq_ref):
        t = pl.program_id(1)
        qi = qt[t]
        ki = kt[t]

        @pl.when(ki == 0)
        def _():
            dq_ref[...] = jnp.zeros(dq_ref.shape, F32)

        def step(masked):
            for h in range(ATT_HEADS_PER_STEP):
                sl = slice(h * LANE, (h + 1) * LANE)
                s = _nt16(q_ref[:, sl], k_ref[:, sl])
                if masked:
                    s = jnp.where(_causal_keep(s.shape, False), s, NEG)
                p = jnp.exp2(s - lse_ref[:, sl][:, :1])
                dp = _nt16(do_ref[:, sl], v_ref[:, sl])
                ds = p * (dp - dl_ref[:, sl][:, :1])
                dq_ref[:, sl] += jnp.dot(ds.astype(BF16), k_ref[:, sl], preferred_element_type=F32)

        @pl.when(ki < qi)
        def _():
            step(False)

        @pl.when(ki == qi)
        def _():
            step(True)
            dq_ref[...] = dq_ref[...] * ATT_SCALE

    tabs = _tri_tables(nq, by_k=False)
    qspec = pl.BlockSpec((tq, ATT_W), lambda g, t, qt, kt: (qt[t], g))
    kspec = pl.BlockSpec((tq, ATT_W), lambda g, t, qt, kt: (kt[t], g))
    dq = _att_call(dq_body, name + "_dq", l, tq, tabs, [q, k, v, do16, lse, delta],
                   [qspec, kspec, kspec, qspec, qspec, qspec], _sds((l, MLA_PAD)), qspec)

    def dkv_body(qt, kt, q_ref, k_ref, v_ref, do_ref, lset_ref, dlt_ref, dk_ref, dv_ref):
        t = pl.program_id(1)
        qi = qt[t]
        ki = kt[t]

        def step(masked):
            for h in range(ATT_HEADS_PER_STEP):
                sl = slice(h * LANE, (h + 1) * LANE)
                st = _nt16(k_ref[:, sl], q_ref[:, sl])
                if masked:
                    st = jnp.where(_causal_keep(st.shape, True), st, NEG)
                pt = jnp.exp2(st - lset_ref[h * 8:(h + 1) * 8, :][:1])
                dpt = _nt16(v_ref[:, sl], do_ref[:, sl])
                dst = pt * (dpt - dlt_ref[h * 8:(h + 1) * 8, :][:1])
                dv_ref[:, sl] += jnp.dot(pt.astype(BF16), do_ref[:, sl], preferred_element_type=F32)
                dk_ref[:, sl] += jnp.dot(dst.astype(BF16), q_ref[:, sl], preferred_element_type=F32)

        @pl.when(qi == ki)
        def _():
            dk_ref[...] = jnp.zeros(dk_ref.shape, F32)
            dv_ref[...] = jnp.zeros(dv_ref.shape, F32)
            step(True)

        @pl.when(qi > ki)
        def _():
            step(False)

        @pl.when(qi == nq - 1)
        def _():
            dk_ref[...] = dk_ref[...] * (1.0 / LOG2E)

    tabs_k = _tri_tables(nq, by_k=True)
    rspec = pl.BlockSpec((8 * ATT_HEADS_PER_STEP, tq), lambda g, t, qt, kt: (g, qt[t]))
    dk, dv = _att_call(dkv_body, name + "_dkv", l, tq, tabs_k, [q, k, v, do16, lse_t, delta_t],
                       [qspec, kspec, kspec, qspec, rspec, rspec], (_sds((l, MLA_PAD)), _sds((l, MLA_PAD))), (kspec, kspec))
    return dq, dk, dv


def _cmul(ar, ai, br, bi):
    return ar * br - ai * bi, ar * bi + ai * br


def _scan(name, x_re, x_im, a_re, a_im, reverse):
    l, lanes = x_re.shape
    ns = SCAN_SEGS
    tl = l // ns
    steps = int(math.log2(tl))
    assert 2 ** steps == tl and tl * ns == l

    def body(xr_ref, xi_ref, ar_ref, ai_ref, sr_ref, si_ref):
        a_r1 = ar_ref[...]
        a_i1 = ai_ref[...]
        a_r = jnp.broadcast_to(a_r1, (ns, LANE))
        a_i = jnp.broadcast_to(a_i1, (ns, LANE))

        def rows(t):
            t = (tl - 1 - t) if reverse else t
            return pl.ds(pl.multiple_of(t * ns, ns), ns)

        def local(t, carry):
            cr, ci = carry
            r = rows(t)
            pr, pi = _cmul(a_r, a_i, cr, ci)
            return pr + xr_ref[r, :], pi + xi_ref[r, :]

        zero = jnp.zeros((ns, LANE), F32)
        e_r, e_i = lax.fori_loop(0, tl, local, (zero, zero))
        p_r, p_i = a_r1, a_i1
        for _ in range(steps):
            p_r, p_i = _cmul(p_r, p_i, p_r, p_i)
        rid = lax.broadcasted_iota(jnp.int32, (ns, LANE), 0)
        c_r = jnp.zeros((1, LANE), F32)
        c_i = jnp.zeros((1, LANE), F32)
        in_r, in_i = zero, zero
        order = range(ns - 2, -1, -1) if reverse else range(1, ns)
        for kk in order:
            src = kk + 1 if reverse else kk - 1
            ek_r = jnp.sum(jnp.where(rid == src, e_r, 0.0), axis=0, keepdims=True)
            ek_i = jnp.sum(jnp.where(rid == src, e_i, 0.0), axis=0, keepdims=True)
            q_r, q_i = _cmul(p_r, p_i, c_r, c_i)
            c_r, c_i = q_r + ek_r, q_i + ek_i
            in_r = jnp.where(rid == kk, jnp.broadcast_to(c_r, (ns, LANE)), in_r)
            in_i = jnp.where(rid == kk, jnp.broadcast_to(c_i, (ns, LANE)), in_i)

        def final(t, carry):
            cr, ci = carry
            r = rows(t)
            pr, pi = _cmul(a_r, a_i, cr, ci)
            nr, ni = pr + xr_ref[r, :], pi + xi_ref[r, :]
            sr_ref[r, :] = nr
            si_ref[r, :] = ni
            return nr, ni

        lax.fori_loop(0, tl, final, (in_r, in_i))

    xs = pl.BlockSpec((l, LANE), lambda j: (0, j))
    as_ = pl.BlockSpec((1, LANE), lambda j: (0, j))
    return pl.pallas_call(
        body, name=name, grid=(lanes // LANE,), in_specs=[xs, xs, as_, as_], out_specs=(xs, xs),
        out_shape=(_sds((l, lanes)), _sds((l, lanes))),
        compiler_params=_params(("arbitrary",), 48))(x_re, x_im, a_re, a_im)


ANY = pl.BlockSpec(memory_space=pl.ANY)


def _chip_exchange(name, src, per_peer):
    r, c = src.shape[-2:]

    def body(x_ref, y_ref, send_sems, recv_sems, local_sem):
        mx, my, mc = lax.axis_index("x"), lax.axis_index("y"), lax.axis_index("c")
        me = 2 * mx + my
        peers = [(1 - mx, my), (mx, 1 - my), (1 - mx, 1 - my)]

        def part(k):
            return x_ref.at[k] if per_peer else x_ref

        own = pltpu.make_async_copy(part(me), y_ref.at[me], local_sem)
        own.start()
        copies = []
        for j, (px, py) in enumerate(peers):
            cp = pltpu.make_async_remote_copy(
                src_ref=part(2 * px + py), dst_ref=y_ref.at[me], send_sem=send_sems.at[j],
                recv_sem=recv_sems.at[j], device_id=(px, py, mc), device_id_type=MESH)
            cp.start()
            copies.append(cp)
        for cp in copies:
            cp.wait_recv()
        for cp in copies:
            cp.wait_send()
        own.wait()

    return pl.pallas_call(
        body, name=name, in_specs=[ANY], out_specs=ANY, out_shape=_sds((4, r, c), src.dtype),
        scratch_shapes=[pltpu.SemaphoreType.DMA((3,)), pltpu.SemaphoreType.DMA((3,)), pltpu.SemaphoreType.DMA])(src)


D2D_CHUNKS = 16


def _core_exchange(name, src, keep_own):
    n = src.shape[0]

    def body(x_ref, y_ref, send_sems, recv_sems, local_sems):
        mx, my, mc = lax.axis_index("x"), lax.axis_index("y"), lax.axis_index("c")
        copies, owns = [], []
        for j in range(n):
            dst = y_ref.at[mc, j] if keep_own else y_ref.at[j]
            cp = pltpu.make_async_remote_copy(
                src_ref=x_ref.at[j], dst_ref=dst, send_sem=send_sems.at[j], recv_sem=recv_sems.at[j],
                device_id=(mx, my, 1 - mc), device_id_type=MESH)
            cp.start()
            copies.append(cp)
            if keep_own:
                own = pltpu.make_async_copy(x_ref.at[j], y_ref.at[mc, j], local_sems.at[j])
                own.start()
                owns.append(own)
        for cp in copies:
            cp.wait_recv()
        for cp in copies:
            cp.wait_send()
        for own in owns:
            own.wait()

    out_shape = _sds(((2,) if keep_own else ()) + src.shape, src.dtype)
    return pl.pallas_call(
        body, name=name, in_specs=[ANY], out_specs=ANY, out_shape=out_shape,
        scratch_shapes=[pltpu.SemaphoreType.DMA((n,)), pltpu.SemaphoreType.DMA((n,)), pltpu.SemaphoreType.DMA((n,))])(src)


def _all_exchange(name, src):
    def body(x_ref, y_ref, send_sems, recv_sems, local_sem):
        mx, my, mc = lax.axis_index("x"), lax.axis_index("y"), lax.axis_index("c")
        me = 4 * mx + 2 * my + mc
        own = pltpu.make_async_copy(x_ref, y_ref.at[me], local_sem)
        own.start()
        copies = []
        for j in range(1, 8):
            px = (1 - mx) if (j & 4) else mx
            py = (1 - my) if (j & 2) else my
            pc = (1 - mc) if (j & 1) else mc
            cp = pltpu.make_async_remote_copy(
                src_ref=x_ref, dst_ref=y_ref.at[me], send_sem=send_sems.at[j - 1], recv_sem=recv_sems.at[j - 1],
                device_id=(px, py, pc), device_id_type=MESH)
            cp.start()
            copies.append(cp)
        for cp in copies:
            cp.wait_recv()
        for cp in copies:
            cp.wait_send()
        own.wait()

    return pl.pallas_call(
        body, name=name, in_specs=[ANY], out_specs=ANY, out_shape=_sds((8,) + src.shape, src.dtype),
        scratch_shapes=[pltpu.SemaphoreType.DMA((7,)), pltpu.SemaphoreType.DMA((7,)), pltpu.SemaphoreType.DMA])(src)


PACK_W = 1024


def _pack(arrs, rows_multiple, dtype):
    flat = jnp.concatenate([a.reshape(-1).astype(dtype) for a in arrs])
    n = flat.shape[0]
    unit = PACK_W * rows_multiple
    tot = -(-n // unit) * unit
    flat = jnp.pad(flat, (0, tot - n))
    return flat.reshape(tot // PACK_W, PACK_W)


def _unpack(flat, shapes):
    flat = flat.reshape(-1)
    out = []
    off = 0
    for s in shapes:
        n = int(np.prod(s))
        out.append(flat[off:off + n].reshape(s))
        off += n
    return out


def _rope_tables(pos):
    l = pos.shape[0]
    tm = min(512, l)
    inv = (np.float32(ROPE_THETA) ** (-np.arange(0, D_ROPE, 2, dtype=np.float32) / np.float32(D_ROPE))).astype(np.float32)
    lane_f = np.zeros((1, LANE), np.float32)
    lane_f[0, D_NOPE:D_NOPE + 16] = inv
    lane_f[0, D_NOPE + 16:D_NOPE + 32] = inv

    def fn(i, p, f):
        ang = p * f
        lane = lax.broadcasted_iota(jnp.int32, ang.shape, 1)
        co = jnp.cos(ang)
        si = jnp.sin(ang)
        c = jnp.where(lane < D_NOPE, 1.0, jnp.where(lane < D_QK, co, 0.0))
        s1 = jnp.where((lane >= D_NOPE) & (lane < D_NOPE + 16), -si, 0.0)
        s2 = jnp.where((lane >= D_NOPE + 16) & (lane < D_QK), si, 0.0)
        return c, s1, s2

    return _rows(fn, name="rope_tables", n=l // tm, ins=[pos, jnp.asarray(lane_f)],
                 in_specs=[_rt(tm, 1), _full((1, LANE))], outs=[_sds((l, LANE))] * 3, out_specs=[_rt(tm, LANE)] * 3)


def _ssm_param_fn(lr, li, log_dt, br, bi):
    dt = jnp.exp(log_dt)
    mag = jnp.exp(lr * dt)
    a_re = mag * jnp.cos(li * dt)
    a_im = mag * jnp.sin(li * dt)
    den = lr * lr + li * li
    e_re = a_re - 1.0
    e_im = a_im
    f_re = (e_re * lr + e_im * li) / den
    f_im = (e_im * lr - e_re * li) / den
    bb_re = f_re[None] * br - f_im[None] * bi
    bb_im = f_re[None] * bi + f_im[None] * br
    return a_re, a_im, bb_re, bb_im


def _ssm_params(name, lr, li, log_dt, br, bi):
    g, n = lr.shape
    c = br.shape[0]
    return _rows(lambda i, *v: _ssm_param_fn(*v), name=name, n=1, ins=[lr, li, log_dt, br, bi],
                 in_specs=[_full((g, n)), _full((g, n)), _full((g, 1)), _full((c, g, n)), _full((c, g, n))],
                 outs=[_sds((g, n)), _sds((g, n)), _sds((c, g, n)), _sds((c, g, n))],
                 out_specs=[_full((g, n)), _full((g, n)), _full((c, g, n)), _full((c, g, n))])


def _ssm_params_bwd(name, lr, li, log_dt, br, bi, d_are, d_aim, d_bbre, d_bbim):
    g, n = lr.shape
    c = br.shape[0]

    def fn(i, lr, li, log_dt, br, bi, g0, g1, g2, g3):
        _, vjp = jax.vjp(_ssm_param_fn, lr, li, log_dt, br, bi)
        return vjp((g0, g1, g2, g3))

    sp = [_full((g, n)), _full((g, n)), _full((g, 1)), _full((c, g, n)), _full((c, g, n))]
    return _rows(fn, name=name, n=1, ins=[lr, li, log_dt, br, bi, d_are, d_aim, d_bbre, d_bbim],
                 in_specs=sp + [_full((g, n)), _full((g, n)), _full((c, g, n)), _full((c, g, n))],
                 outs=[_sds((g, n)), _sds((g, n)), _sds((g, 1)), _sds((c, g, n)), _sds((c, g, n))], out_specs=sp)


_EYE8 = np.eye(8, dtype=np.float32)


def _blockdiag(v):
    j, g, p, q = v.shape
    m = v[:, :, :, None, :] * jnp.asarray(_EYE8)[None, :, None, :, None]
    return m.reshape(j, g * p, g * q)


def _blockdiag_t(m, p, q):
    j = m.shape[0]
    m = m.reshape(j, 8, p, 8, q)
    return jnp.sum(m * jnp.asarray(_EYE8)[None, :, None, :, None], axis=3)


def _to_perm(v, l):
    ns = SCAN_SEGS
    return v.reshape(ns, l // ns, v.shape[-1]).transpose(1, 0, 2).reshape(l, v.shape[-1])


def _from_perm(v, l):
    ns = SCAN_SEGS
    return v.reshape(l // ns, ns, v.shape[-1]).transpose(1, 0, 2).reshape(l, v.shape[-1])


def _prep_layer(w, i):
    p = {}
    w_in = w['w_in'][i]
    z = lambda n: jnp.zeros((D_MODEL, n), w_in.dtype)
    o = Q_LORA + KV_LORA
    p['w_s'] = jnp.concatenate([w_in[:, :o], z(D_NOPE), w_in[:, o:o + D_ROPE], z(HEAD_PAD - D_QK)], axis=1)
    o += D_ROPE
    p['w_u'] = w_in[:, o:o + SSM_WIDTH]
    o += SSM_WIDTH
    p['w_xq'] = w_in[:, o:o + X_WIDTH]
    o += X_WIDTH
    p['w_g'] = w_in[:, o:]
    wq = w['w_q_b'][i].reshape(Q_LORA, MLA_HEADS, D_QK)
    p['w_qb'] = jnp.pad(wq, ((0, 0), (0, 0), (0, HEAD_PAD - D_QK))).reshape(Q_LORA, MLA_PAD)
    wkv = w['w_kv_b'][i].reshape(KV_LORA, MLA_HEADS, D_NOPE + D_V)
    p['w_k'] = jnp.pad(wkv[:, :, :D_NOPE], ((0, 0), (0, 0), (0, HEAD_PAD - D_NOPE))).reshape(KV_LORA, MLA_PAD)
    p['w_v'] = jnp.pad(wkv[:, :, D_NOPE:], ((0, 0), (0, 0), (0, HEAD_PAD - D_V))).reshape(KV_LORA, MLA_PAD)
    wo = w['w_o_mla'][i].reshape(MLA_HEADS, D_V, D_MODEL)
    p['w_oa'] = jnp.pad(wo, ((0, 0), (0, HEAD_PAD - D_V), (0, 0))).reshape(MLA_PAD, D_MODEL)
    for n in ('w_glu', 'w_o_ssm', 'w_mem_kv', 'w_o_cross', 'w_out', 'w_up', 'w_down'):
        p[n] = w[n][i]
    p['conv_w'] = w['conv_w'][i]
    for n in ('norm_mix_g', 'q_a_norm_g', 'kv_a_norm_g', 'b_glu', 'mem_norm_g', 'xq_norm_g', 'xk_norm_g', 'b_gate',
              'norm_ffn_g', 'conv_b'):
        p[n] = w[n][i].reshape(1, -1)
    p['q_norm_g'] = jnp.pad(w['q_norm_g'][i], (0, HEAD_PAD - D_QK)).reshape(1, HEAD_PAD)
    p['k_norm_g'] = jnp.pad(w['k_norm_g'][i], (0, HEAD_PAD - D_QK)).reshape(1, HEAD_PAD)
    p['ssm_d'] = w['ssm_d'][i].reshape(1, SSM_WIDTH)
    p['lr'] = w['ssm_lambda_re'][i]
    p['li'] = w['ssm_lambda_im'][i]
    p['log_dt'] = w['ssm_log_dt'][i].reshape(SSM_GROUPS, 1)
    p['br'] = w['ssm_b_re'][i].transpose(2, 0, 1)
    p['bi'] = w['ssm_b_im'][i].transpose(2, 0, 1)
    cr = w['ssm_c_re'][i].reshape(SSM_JB, 8, SSM_GROUP_CH, SSM_STATE).transpose(0, 1, 3, 2)
    ci = w['ssm_c_im'][i].reshape(SSM_JB, 8, SSM_GROUP_CH, SSM_STATE).transpose(0, 1, 3, 2)
    p['c_mat'] = jnp.concatenate([_blockdiag(cr), -_blockdiag(ci)], axis=1).astype(BF16)
    return p


def _b_mat(bb_re, bb_im):
    r = bb_re.transpose(1, 0, 2).reshape(SSM_JB, 8, SSM_GROUP_CH, SSM_STATE)
    i = bb_im.transpose(1, 0, 2).reshape(SSM_JB, 8, SSM_GROUP_CH, SSM_STATE)
    return jnp.concatenate([_blockdiag(r), _blockdiag(i)], axis=2).astype(BF16)


def _qkv_fn(ps, c, s1, s2, qag, wqb, kvag, wk, wv, qng, kng):
    c_q = ps[:, :Q_LORA]
    c_kv = ps[:, Q_LORA:Q_LORA + KV_LORA]
    kr = ps[:, Q_LORA + KV_LORA:]
    cqn = _rms(c_q, qag, Q_LORA)
    ckvn = _rms(c_kv, kvag, KV_LORA)
    q_raw = _mm(cqn, wqb)
    k_raw = _mm(ckvn, wk) + jnp.concatenate([kr] * MLA_HEADS, axis=-1)
    v = _mm(ckvn, wv)
    q = _heads(_rope, _head_rms(q_raw, qng, MLA_HEADS, D_QK), MLA_HEADS, c, s1, s2)
    k = _heads(_rope, _head_rms(k_raw, kng, MLA_HEADS, D_QK), MLA_HEADS, c, s1, s2)
    return q * ATT_QSCALE, k, v


def _layer_fwd(name, x, tabs, mem, p):
    l = x.shape[0]
    tm = min(512, l)
    nt = l // tm
    sv = {'x0': x}
    sv['p_g'] = _matmul(name + "_in_g", [(x, p['w_g'])], l, 3 * D_MODEL, rms_gain=p['norm_mix_g'])
    sv['p_u'] = _matmul(name + "_in_u", [(x, p['w_u'])], l, SSM_WIDTH, rms_gain=p['norm_mix_g'])
    sv['p_xq'] = _matmul(name + "_in_xq", [(x, p['w_xq'])], l, X_WIDTH, rms_gain=p['norm_mix_g'])
    sv['p_s'] = _matmul(name + "_in_s", [(x, p['w_s'])], l, SMALL_W, rms_gain=p['norm_mix_g'])

    qkv_consts = [p['q_a_norm_g'], p['w_qb'], p['kv_a_norm_g'], p['w_k'], p['w_v'], p['q_norm_g'], p['k_norm_g']]
    qkv_cspecs = [_full(a.shape) for a in qkv_consts]
    q, k, v = _rows(lambda i, *a: _qkv_fn(*a), name=name + "_qkv", n=nt, ins=[sv['p_s'], *tabs, *qkv_consts],
                    in_specs=[_rt(tm, SMALL_W)] + [_rt(tm, LANE)] * 3 + qkv_cspecs,
                    outs=[_sds((l, MLA_PAD), BF16)] * 3, out_specs=[_rt(tm, MLA_PAD)] * 3)
    sv['q'], sv['k'], sv['v'] = q, k, v
    sv['o_a'], sv['lse'], sv['lse_t'] = _flash_fwd(name + "_attn", q, k, v)

    a_re, a_im, bb_re, bb_im = _ssm_params(name + "_ssm_par", p['lr'], p['li'], p['log_dt'], p['br'], p['bi'])
    sv['a_re'], sv['a_im'] = a_re.reshape(1, SSM_LANES), a_im.reshape(1, SSM_LANES)
    sv['b_mat'] = _b_mat(bb_re, bb_im)
    u_p = _to_perm(sv['p_u'], l)
    sv['u_p'] = u_p

    def bu_fn(i, u, bm):
        res = [_mm(u[:, j * LANE:(j + 1) * LANE], bm[j]) for j in range(SSM_JB)]
        return (jnp.concatenate([r[:, :512] for r in res], axis=-1), jnp.concatenate([r[:, 512:] for r in res], axis=-1))

    ts = min(256, l)
    bu_re, bu_im = _rows(bu_fn, name=name + "_ssm_bu", n=l // ts, ins=[u_p, sv['b_mat']],
                         in_specs=[_rt(ts, SSM_WIDTH), _full(sv['b_mat'].shape)],
                         outs=[_sds((l, SSM_LANES))] * 2, out_specs=[_rt(ts, SSM_LANES)] * 2)
    s_re, s_im = _scan(name + "_ssm_scan", bu_re, bu_im, sv['a_re'], sv['a_im'], reverse=False)
    sv['s_re'], sv['s_im'] = s_re, s_im

    def glu_fn(i, sr, si, u, cm, dsk, wg, bg):
        y = jnp.concatenate([_mm(jnp.concatenate([sr[:, j * 512:(j + 1) * 512], si[:, j * 512:(j + 1) * 512]], axis=-1),
                                 cm[j]) for j in range(SSM_JB)], axis=-1) + dsk * u
        zz = _gelu(y)
        return zz * jax.nn.sigmoid(_mm(zz, wg) + bg)

    glu_consts = [p['c_mat'], p['ssm_d'], p['w_glu'], p['b_glu']]
    zo_p = _rows(glu_fn, name=name + "_ssm_glu", n=l // ts, ins=[s_re, s_im, u_p, *glu_consts],
                 in_specs=[_rt(ts, SSM_LANES), _rt(ts, SSM_LANES), _rt(ts, SSM_WIDTH)] + [_full(a.shape) for a in glu_consts],
                 outs=[_sds((l, SSM_WIDTH), BF16)], out_specs=[_rt(ts, SSM_WIDTH)])[0]
    sv['zo'] = _from_perm(zo_p, l)

    m_len = mem.shape[0]

    def memkv_fn(i, mm_, mg, wmk, xkg):
        kv = _mm(_rms(mm_, mg, D_MODEL), wmk)
        return _head_rms(kv[:, :X_WIDTH], xkg, X_HEADS, X_HEAD_DIM), kv[:, X_WIDTH:]

    mem_consts = [p['mem_norm_g'], p['w_mem_kv'], p['xk_norm_g']]
    k_c, v_c = _rows(memkv_fn, name=name + "_memkv", n=1, ins=[mem, *mem_consts],
                     in_specs=[_full(mem.shape)] + [_full(a.shape) for a in mem_consts],
                     outs=[_sds((m_len, X_WIDTH))] * 2, out_specs=[_full((m_len, X_WIDTH))] * 2)
    sv['k_c'], sv['v_c'] = k_c, v_c

    def cross_fn(i, xq, kc, vc, xqg):
        outs = []
        for h in range(X_HEADS):
            sl = slice(h * LANE, (h + 1) * LANE)
            qh = _rms(xq[:, sl], xqg, X_HEAD_DIM)
            s = _mm_nt(qh, kc[:, sl]) * (X_HEAD_DIM ** -0.5)
            s = s - jnp.max(s, axis=-1, keepdims=True)
            e = jnp.exp(s)
            pr = e / jnp.sum(e, axis=-1, keepdims=True)
            outs.append(_mm(pr, vc[:, sl]))
        return jnp.concatenate(outs, axis=-1)

    sv['o_c'] = _rows(cross_fn, name=name + "_cross", n=nt, ins=[sv['p_xq'], k_c, v_c, p['xq_norm_g']],
                      in_specs=[_rt(tm, X_WIDTH), _full(k_c.shape), _full(v_c.shape), _full((1, LANE))],
                      outs=[_sds((l, X_WIDTH), BF16)], out_specs=[_rt(tm, X_WIDTH)])[0]

    def merge_fn(i, oa, zo, oc, pg, x0, woa, wos, woc, bg, wout):
        gates = jax.nn.sigmoid(pg + bg)
        merged = (gates[:, :D_MODEL] * _mm(oa, woa) + gates[:, D_MODEL:2 * D_MODEL] * _mm(zo, wos)
                  + gates[:, 2 * D_MODEL:] * _mm(oc, woc))
        return x0 + _mm(merged, wout), merged

    merge_consts = [p['w_oa'], p['w_o_ssm'], p['w_o_cross'], p['b_gate'], p['w_out']]
    tg = min(256, l)
    x1, merged = _rows(merge_fn, name=name + "_merge", n=l // tg, ins=[sv['o_a'], sv['zo'], sv['o_c'], sv['p_g'], x, *merge_consts],
                       in_specs=[_rt(tg, MLA_PAD), _rt(tg, SSM_WIDTH), _rt(tg, X_WIDTH), _rt(tg, 3 * D_MODEL), _rt(tg, D_MODEL)]
                       + [_full(a.shape) for a in merge_consts],
                       outs=[_sds((l, D_MODEL)), _sds((l, D_MODEL), BF16)], out_specs=[_rt(tg, D_MODEL)] * 2)
    sv['x1'], sv['merged'] = x1, merged

    up = _matmul(name + "_up", [(x1, p['w_up'])], l, 2 * D_FF, rms_gain=p['norm_ffn_g'])
    sv['up'] = up
    tc = min(128, l)

    def conv_fn(i, upt, halo, cw, cb):
        upc = _conv(i, upt, halo, cw) + cb
        return _silu(upc[:, :D_FF]) * upc[:, D_FF:]

    act = _rows(conv_fn, name=name + "_conv", n=l // tc, ins=[up, up, p['conv_w'], p['conv_b']],
                in_specs=[_rt(tc, 2 * D_FF), _halo_prev(tc, 2 * D_FF), _full((3, 2 * D_FF)), _full((1, 2 * D_FF))],
                outs=[_sds((l, D_FF), BF16)], out_specs=[_rt(tc, D_FF)])[0]
    sv['act'] = act
    x2 = _matmul(name + "_down", [(act, p['w_down'])], l, D_MODEL, resid=x1)
    return x2, sv


def _halo_prev(tm, w):
    return pl.BlockSpec((8, w), lambda i: (jnp.maximum(i * (tm // 8) - 1, 0), 0))


def _halo_next(tm, w, n_tiles):
    last = n_tiles * (tm // 8) - 1
    return pl.BlockSpec((8, w), lambda i: (jnp.minimum((i + 1) * (tm // 8), last), 0))


def _conv(i, tile, halo, cw):
    halo = jnp.where(i > 0, halo, 0.0)
    ext = jnp.concatenate([halo, tile], axis=0)
    n = ext.shape[0]
    x1 = pltpu.roll(ext, 1, 0)[8:]
    x2 = pltpu.roll(ext, 2, 0)[8:]
    del n
    return cw[0:1] * x2 + cw[1:2] * x1 + cw[2:3] * tile


def _layer_bwd(name, dx2, sv, tabs, mem, p):
    l = dx2.shape[0]
    tm = min(512, l)
    nt = l // tm
    g = {}
    x1 = sv['x1']
    dact = _matmul(name + "_b_down", [(dx2, p['w_down'])], l, D_FF, nt=True)
    g['w_down'] = _matmul_tn(name + "_gw_down", sv['act'], dx2)
    tc = min(128, l)
    ntc = l // tc

    def conv_b1(i, upt, halo, da, cw, cb):
        halo = jnp.where(i > 0, halo, 0.0)
        ext = jnp.concatenate([halo, upt], axis=0)
        xm1 = pltpu.roll(ext, 1, 0)[8:]
        xm2 = pltpu.roll(ext, 2, 0)[8:]
        upc = cw[0:1] * xm2 + cw[1:2] * xm1 + cw[2:3] * upt + cb
        gf, vf = upc[:, :D_FF], upc[:, D_FF:]
        _, vjp = jax.vjp(lambda a, b: _silu(a) * b, gf, vf)
        dg, dv = vjp(da)
        dupc = jnp.concatenate([dg, dv], axis=-1)
        dcw = _row_select([_colsum(dupc * xm2), _colsum(dupc * xm1), _colsum(dupc * upt)], 8)
        return dupc, dcw, _colsum(dupc)

    dupc, g_cw, g_cb = _rows(conv_b1, name=name + "_b_conv1", n=ntc, ins=[sv['up'], sv['up'], dact, p['conv_w'], p['conv_b']],
                             in_specs=[_rt(tc, 2 * D_FF), _halo_prev(tc, 2 * D_FF), _rt(tc, D_FF), _full((3, 2 * D_FF)),
                                       _full((1, 2 * D_FF))],
                             outs=[_sds((l, 2 * D_FF)), _sds((8, 2 * D_FF)), _sds((1, 2 * D_FF))],
                             out_specs=[_rt(tc, 2 * D_FF), _full((8, 2 * D_FF)), _full((1, 2 * D_FF))], n_acc=2)
    g['conv_w'] = g_cw[:3]
    g['conv_b'] = g_cb

    def conv_b2(i, dt, halo, cw):
        halo = jnp.where(i < ntc - 1, halo, 0.0)
        ext = jnp.concatenate([dt, halo], axis=0)
        n = ext.shape[0]
        dp1 = pltpu.roll(ext, n - 1, 0)[:tc]
        dp2 = pltpu.roll(ext, n - 2, 0)[:tc]
        return cw[2:3] * dt + cw[1:2] * dp1 + cw[0:1] * dp2

    dup = _rows(conv_b2, name=name + "_b_conv2", n=ntc, ins=[dupc, dupc, p['conv_w']],
                in_specs=[_rt(tc, 2 * D_FF), _halo_next(tc, 2 * D_FF, ntc), _full((3, 2 * D_FF))],
                outs=[_sds((l, 2 * D_FF))], out_specs=[_rt(tc, 2 * D_FF)])[0]
    dh2 = _matmul(name + "_b_up", [(dup, p['w_up'])], l, D_MODEL, nt=True, tm=256)
    g['w_up'] = _matmul_tn(name + "_gw_up", x1, dup, rms_gain=p['norm_ffn_g'])

    def norm_b(i, xv, dh, dres, gn):
        _, vjp = jax.vjp(lambda a, b: _rms(a, b, D_MODEL), xv, gn)
        dxv, dgn = vjp(dh)
        return dres + dxv, dgn

    dx1, g['norm_ffn_g'] = _rows(norm_b, name=name + "_b_norm2", n=nt, ins=[x1, dh2, dx2, p['norm_ffn_g']],
                                 in_specs=[_rt(tm, D_MODEL)] * 3 + [_full((1, D_MODEL))],
                                 outs=[_sds((l, D_MODEL)), _sds((1, D_MODEL))], out_specs=[_rt(tm, D_MODEL), _full((1, D_MODEL))],
                                 n_acc=1)

    tg = min(256, l)

    def merge_b(i, dx, oa, zo, oc, pg, woa, wos, woc, bg, wout):
        dm = _mm_nt(dx, wout)
        gates = jax.nn.sigmoid(pg + bg)
        ys = [_mm(oa, woa), _mm(zo, wos), _mm(oc, woc)]
        dys, dpg = [], []
        for b in range(3):
            gb = gates[:, b * D_MODEL:(b + 1) * D_MODEL]
            dys.append(dm * gb)
            dpg.append(dm * ys[b] * gb * (1.0 - gb))
        dpg = jnp.concatenate(dpg, axis=-1)
        return (_mm_nt(dys[0], woa), _mm_nt(dys[1], wos), _mm_nt(dys[2], woc), dpg, dys[0], dys[1], dys[2], _colsum(dpg))

    merge_consts = [p['w_oa'], p['w_o_ssm'], p['w_o_cross'], p['b_gate'], p['w_out']]
    (do_a, dzo, do_c, dp_g, dy_a, dy_b, dy_c, g['b_gate']) = _rows(
        merge_b, name=name + "_b_merge", n=l // tg, ins=[dx1, sv['o_a'], sv['zo'], sv['o_c'], sv['p_g'], *merge_consts],
        in_specs=[_rt(tg, D_MODEL), _rt(tg, MLA_PAD), _rt(tg, SSM_WIDTH), _rt(tg, X_WIDTH), _rt(tg, 3 * D_MODEL)]
        + [_full(a.shape) for a in merge_consts],
        outs=[_sds((l, MLA_PAD)), _sds((l, SSM_WIDTH)), _sds((l, X_WIDTH)), _sds((l, 3 * D_MODEL)),
              _sds((l, D_MODEL), BF16), _sds((l, D_MODEL), BF16), _sds((l, D_MODEL), BF16), _sds((1, 3 * D_MODEL))],
        out_specs=[_rt(tg, MLA_PAD), _rt(tg, SSM_WIDTH), _rt(tg, X_WIDTH), _rt(tg, 3 * D_MODEL),
                   _rt(tg, D_MODEL), _rt(tg, D_MODEL), _rt(tg, D_MODEL), _full((1, 3 * D_MODEL))], n_acc=1, vmem=56)
    g['w_out'] = _matmul_tn(name + "_gw_out", sv['merged'], dx1)
    g['w_oa'] = _matmul_tn(name + "_gw_oa", sv['o_a'], dy_a)
    g['w_o_ssm'] = _matmul_tn(name + "_gw_os", sv['zo'], dy_b)
    g['w_o_cross'] = _matmul_tn(name + "_gw_oc", sv['o_c'], dy_c)

    k_c, v_c = sv['k_c'], sv['v_c']
    m_len = k_c.shape[0]

    def cross_b(i, xq, do, kc, vc, xqg):
        dxq, dk, dv = [], [], []
        dg = jnp.zeros((1, LANE), F32)
        for h in range(X_HEADS):
            sl = slice(h * LANE, (h + 1) * LANE)
            qh, vjp = jax.vjp(lambda a, b: _rms(a, b, X_HEAD_DIM), xq[:, sl], xqg)
            sc = X_HEAD_DIM ** -0.5
            s = _mm_nt(qh, kc[:, sl]) * sc
            s = s - jnp.max(s, axis=-1, keepdims=True)
            e = jnp.exp(s)
            pr = e / jnp.sum(e, axis=-1, keepdims=True)
            doh = do[:, sl]
            dv.append(_mm_tn(pr, doh))
            dp = _mm_nt(doh, vc[:, sl])
            ds = pr * (dp - jnp.sum(dp * pr, axis=-1, keepdims=True)) * sc
            dk.append(_mm_tn(ds, qh))
            dxh, dgh = vjp(_mm(ds, kc[:, sl]))
            dxq.append(dxh)
            dg = dg + dgh
        return jnp.concatenate(dxq, axis=-1), jnp.concatenate(dk, axis=-1), jnp.concatenate(dv, axis=-1), dg

    dp_xq, dk_c, dv_c, g['xq_norm_g'] = _rows(
        cross_b, name=name + "_b_cross", n=nt, ins=[sv['p_xq'], do_c, k_c, v_c, p['xq_norm_g']],
        in_specs=[_rt(tm, X_WIDTH), _rt(tm, X_WIDTH), _full(k_c.shape), _full(v_c.shape), _full((1, LANE))],
        outs=[_sds((l, X_WIDTH)), _sds((m_len, X_WIDTH)), _sds((m_len, X_WIDTH)), _sds((1, LANE))],
        out_specs=[_rt(tm, X_WIDTH), _full((m_len, X_WIDTH)), _full((m_len, X_WIDTH)), _full((1, LANE))], n_acc=3)

    def memkv_b(i, mm_, dk, dv, mg, wmk, xkg):
        memn, vjp_n = jax.vjp(lambda a, b: _rms(a, b, D_MODEL), mm_, mg)
        kv = _mm(memn, wmk)
        _, vjp_k = jax.vjp(lambda a, b: _head_rms(a, b, X_HEADS, X_HEAD_DIM), kv[:, :X_WIDTH], xkg)
        dkr, dxkg = vjp_k(dk)
        dkv = jnp.concatenate([dkr, dv], axis=-1)
        _, dmg = vjp_n(_mm_nt(dkv, wmk))
        return _mm_tn(memn, dkv), dmg, dxkg

    mem_consts = [p['mem_norm_g'], p['w_mem_kv'], p['xk_norm_g']]
    g['w_mem_kv'], g['mem_norm_g'], g['xk_norm_g'] = _rows(
        memkv_b, name=name + "_b_memkv", n=1, ins=[mem, dk_c, dv_c, *mem_consts],
        in_specs=[_full(mem.shape), _full(dk_c.shape), _full(dv_c.shape)] + [_full(a.shape) for a in mem_consts],
        outs=[_sds((D_MODEL, 2 * X_WIDTH)), _sds((1, D_MODEL)), _sds((1, LANE))],
        out_specs=[_full((D_MODEL, 2 * X_WIDTH)), _full((1, D_MODEL)), _full((1, LANE))])

    u_p = sv['u_p']
    dzo_p = _to_perm(dzo, l)
    s_re, s_im = sv['s_re'], sv['s_im']

    def glu_b(i, sr, si, u, dz, cm, dsk, wg, bg):
        cats = [jnp.concatenate([sr[:, j * 512:(j + 1) * 512], si[:, j * 512:(j + 1) * 512]], axis=-1) for j in range(SSM_JB)]
        y = jnp.concatenate([_mm(cats[j], cm[j]) for j in range(SSM_JB)], axis=-1) + dsk * u
        zz, vjp_g = jax.vjp(_gelu, y)
        t = _mm(zz, wg) + bg
        sg = jax.nn.sigmoid(t)
        dt = dz * zz * sg * (1.0 - sg)
        dzz = dz * sg + _mm_nt(dt, wg)
        dy = vjp_g(dzz)[0]
        dss = [_mm_nt(dy[:, j * LANE:(j + 1) * LANE], cm[j]) for j in range(SSM_JB)]
        dsr = jnp.concatenate([d[:, :512] for d in dss], axis=-1)
        dsi = jnp.concatenate([d[:, 512:] for d in dss], axis=-1)
        dcm = jnp.stack([_mm_tn(cats[j], dy[:, j * LANE:(j + 1) * LANE]) for j in range(SSM_JB)], axis=0)
        return dsr, dsi, dy * dsk, dcm, _colsum(dy * u), _mm_tn(zz, dt), _colsum(dt)

    glu_consts = [p['c_mat'], p['ssm_d'], p['w_glu'], p['b_glu']]
    ts = min(256, l)
    nts = l // ts
    ds_re, ds_im, du_dir, g['c_mat'], g['ssm_d'], g['w_glu'], g['b_glu'] = _rows(
        glu_b, name=name + "_b_glu", n=nts, ins=[s_re, s_im, u_p, dzo_p, *glu_consts],
        in_specs=[_rt(ts, SSM_LANES), _rt(ts, SSM_LANES), _rt(ts, SSM_WIDTH), _rt(ts, SSM_WIDTH)] + [_full(a.shape) for a in glu_consts],
        outs=[_sds((l, SSM_LANES)), _sds((l, SSM_LANES)), _sds((l, SSM_WIDTH)), _sds((SSM_JB, 1024, LANE)), _sds((1, SSM_WIDTH)),
              _sds((SSM_WIDTH, SSM_WIDTH)), _sds((1, SSM_WIDTH))],
        out_specs=[_rt(ts, SSM_LANES), _rt(ts, SSM_LANES), _rt(ts, SSM_WIDTH), _full((SSM_JB, 1024, LANE)), _full((1, SSM_WIDTH)),
                   _full((SSM_WIDTH, SSM_WIDTH)), _full((1, SSM_WIDTH))], n_acc=4)
    gb_re, gb_im = _scan(name + "_b_scan", ds_re, ds_im, sv['a_re'], -sv['a_im'], reverse=True)
    ns = SCAN_SEGS
    last_blk = l // ns - 1

    def da_fn(i, gr, gi, sr, si, hr, hi, lr_, li_):
        rid = lax.broadcasted_iota(jnp.int32, lr_.shape, 0)
        fr = jnp.where(rid == 0, 0.0, pltpu.roll(lr_, 1, 0))
        fi = jnp.where(rid == 0, 0.0, pltpu.roll(li_, 1, 0))
        hr = jnp.where(i == 0, fr, hr)
        hi = jnp.where(i == 0, fi, hi)
        if ts > ns:
            pr = jnp.concatenate([hr, sr[:ts - ns]], axis=0)
            pi = jnp.concatenate([hi, si[:ts - ns]], axis=0)
        else:
            pr, pi = hr, hi
        return _colsum(gr * pr + gi * pi), _colsum(gi * pr - gr * pi)

    hprev = pl.BlockSpec((ns, SSM_LANES), lambda i: (jnp.maximum(i * (ts // ns) - 1, 0), 0))
    hlast = pl.BlockSpec((ns, SSM_LANES), lambda i: (last_blk, 0))
    da_re, da_im = _rows(da_fn, name=name + "_b_da", n=nts, ins=[gb_re, gb_im, s_re, s_im, s_re, s_im, s_re, s_im],
                         in_specs=[_rt(ts, SSM_LANES)] * 4 + [hprev, hprev, hlast, hlast],
                         outs=[_sds((1, SSM_LANES))] * 2, out_specs=[_full((1, SSM_LANES))] * 2, n_acc=2)

    def bu_b(i, dbr, dbi, u, dud, bm):
        dus, dbm = [], []
        for j in range(SSM_JB):
            cat = jnp.concatenate([dbr[:, j * 512:(j + 1) * 512], dbi[:, j * 512:(j + 1) * 512]], axis=-1)
            dus.append(_mm_nt(cat, bm[j]))
            dbm.append(_mm_tn(u[:, j * LANE:(j + 1) * LANE], cat))
        return dud + jnp.concatenate(dus, axis=-1), jnp.stack(dbm, axis=0)

    du_p, d_bmat = _rows(bu_b, name=name + "_b_bu", n=nts, ins=[gb_re, gb_im, u_p, du_dir, sv['b_mat']],
                         in_specs=[_rt(ts, SSM_LANES), _rt(ts, SSM_LANES), _rt(ts, SSM_WIDTH), _rt(ts, SSM_WIDTH),
                                   _full(sv['b_mat'].shape)],
                         outs=[_sds((l, SSM_WIDTH)), _sds((SSM_JB, LANE, 1024))],
                         out_specs=[_rt(ts, SSM_WIDTH), _full((SSM_JB, LANE, 1024))], n_acc=1)
    dp_u = _from_perm(du_p, l)
    dbb_re = _blockdiag_t(d_bmat[:, :, :512], SSM_GROUP_CH, SSM_STATE).reshape(SSM_GROUPS, SSM_GROUP_CH, SSM_STATE).transpose(1, 0, 2)
    dbb_im = _blockdiag_t(d_bmat[:, :, 512:], SSM_GROUP_CH, SSM_STATE).reshape(SSM_GROUPS, SSM_GROUP_CH, SSM_STATE).transpose(1, 0, 2)
    g['lr'], g['li'], g['log_dt'], g['br'], g['bi'] = _ssm_params_bwd(
        name + "_b_ssm_par", p['lr'], p['li'], p['log_dt'], p['br'], p['bi'],
        da_re.reshape(SSM_GROUPS, SSM_STATE), da_im.reshape(SSM_GROUPS, SSM_STATE), dbb_re, dbb_im)

    dq, dk, dv = _flash_bwd(name + "_b_attn", sv['q'], sv['k'], sv['v'], sv['o_a'], sv['lse'], sv['lse_t'], do_a)

    def qkv_b(i, ps, c, s1, s2, dq_, dk_, dv_, qag, wqb, kvag, wk, wv, qng, kng):
        c_q = ps[:, :Q_LORA]
        c_kv = ps[:, Q_LORA:Q_LORA + KV_LORA]
        kr = ps[:, Q_LORA + KV_LORA:]
        cqn, vjp_cq = jax.vjp(lambda a, b: _rms(a, b, Q_LORA), c_q, qag)
        ckvn, vjp_ckv = jax.vjp(lambda a, b: _rms(a, b, KV_LORA), c_kv, kvag)
        q_raw = _mm(cqn, wqb)
        k_raw = _mm(ckvn, wk) + jnp.concatenate([kr] * MLA_HEADS, axis=-1)
        _, vjp_qn = jax.vjp(lambda a, b: _head_rms(a, b, MLA_HEADS, D_QK), q_raw, qng)
        _, vjp_kn = jax.vjp(lambda a, b: _head_rms(a, b, MLA_HEADS, D_QK), k_raw, kng)
        dq_raw, dqng = vjp_qn(_heads(_rope_t, dq_, MLA_HEADS, c, s1, s2))
        dk_raw, dkng = vjp_kn(_heads(_rope_t, dk_, MLA_HEADS, c, s1, s2))
        dkr = dk_raw[:, :LANE]
        for h in range(1, MLA_HEADS):
            dkr = dkr + dk_raw[:, h * LANE:(h + 1) * LANE]
        dcq, dqag = vjp_cq(_mm_nt(dq_raw, wqb))
        dckv, dkvag = vjp_ckv(_mm_nt(dk_raw, wk) + _mm_nt(dv_, wv))
        dps = jnp.concatenate([dcq, dckv, dkr], axis=-1)
        return (dps, _mm_tn(cqn, dq_raw), _mm_tn(ckvn, dk_raw), _mm_tn(ckvn, dv_), dqag, dkvag, dqng, dkng)

    qkv_consts = [p['q_a_norm_g'], p['w_qb'], p['kv_a_norm_g'], p['w_k'], p['w_v'], p['q_norm_g'], p['k_norm_g']]
    (dp_s, g['w_qb'], g['w_k'], g['w_v'], g['q_a_norm_g'], g['kv_a_norm_g'], g['q_norm_g'], g['k_norm_g']) = _rows(
        qkv_b, name=name + "_b_qkv", n=nt, ins=[sv['p_s'], *tabs, dq, dk, dv, *qkv_consts],
        in_specs=[_rt(tm, SMALL_W)] + [_rt(tm, LANE)] * 3 + [_rt(tm, MLA_PAD)] * 3 + [_full(a.shape) for a in qkv_consts],
        outs=[_sds((l, SMALL_W)), _sds((Q_LORA, MLA_PAD)), _sds((KV_LORA, MLA_PAD)), _sds((KV_LORA, MLA_PAD)),
              _sds((1, Q_LORA)), _sds((1, KV_LORA)), _sds((1, LANE)), _sds((1, LANE))],
        out_specs=[_rt(tm, SMALL_W), _full((Q_LORA, MLA_PAD)), _full((KV_LORA, MLA_PAD)), _full((KV_LORA, MLA_PAD)),
                   _full((1, Q_LORA)), _full((1, KV_LORA)), _full((1, LANE)), _full((1, LANE))], n_acc=7)

    x0 = sv['x0']
    dh = _matmul(name + "_b_in", [(dp_g, p['w_g']), (dp_u, p['w_u']), (dp_xq, p['w_xq']), (dp_s, p['w_s'])], l, D_MODEL, nt=True,
                 tm=256)
    gm = p['norm_mix_g']
    g['w_g'] = _matmul_tn(name + "_gw_g", x0, dp_g, rms_gain=gm)
    g['w_u'] = _matmul_tn(name + "_gw_u", x0, dp_u, rms_gain=gm)
    g['w_xq'] = _matmul_tn(name + "_gw_xq", x0, dp_xq, rms_gain=gm)
    g['w_s'] = _matmul_tn(name + "_gw_s", x0, dp_s, rms_gain=gm)
    dx0, g['norm_mix_g'] = _rows(norm_b, name=name + "_b_norm1", n=nt, ins=[x0, dh, dx1, gm],
                                 in_specs=[_rt(tm, D_MODEL)] * 3 + [_full((1, D_MODEL))],
                                 outs=[_sds((l, D_MODEL)), _sds((1, D_MODEL))], out_specs=[_rt(tm, D_MODEL), _full((1, D_MODEL))],
                                 n_acc=1)
    return dx0, g


def _unprep_grads(g):
    o = {}
    ws = g['w_s']
    o['w_in'] = jnp.concatenate([ws[:, :Q_LORA + KV_LORA], ws[:, Q_LORA + KV_LORA + D_NOPE:Q_LORA + KV_LORA + D_QK],
                                 g['w_u'], g['w_xq'], g['w_g']], axis=1)
    o['w_q_b'] = g['w_qb'].reshape(Q_LORA, MLA_HEADS, HEAD_PAD)[:, :, :D_QK].reshape(Q_LORA, MLA_HEADS * D_QK)
    gk = g['w_k'].reshape(KV_LORA, MLA_HEADS, HEAD_PAD)[:, :, :D_NOPE]
    gv = g['w_v'].reshape(KV_LORA, MLA_HEADS, HEAD_PAD)[:, :, :D_V]
    o['w_kv_b'] = jnp.concatenate([gk, gv], axis=2).reshape(KV_LORA, MLA_HEADS * (D_NOPE + D_V))
    o['w_o_mla'] = g['w_oa'].reshape(MLA_HEADS, HEAD_PAD, D_MODEL)[:, :D_V].reshape(MLA_HEADS * D_V, D_MODEL)
    for n in ('w_glu', 'w_o_ssm', 'w_mem_kv', 'w_o_cross', 'w_out', 'w_up', 'w_down', 'conv_w'):
        o[n] = g[n]
    for n in ('norm_mix_g', 'q_a_norm_g', 'kv_a_norm_g', 'b_glu', 'mem_norm_g', 'xq_norm_g', 'xk_norm_g', 'b_gate',
              'norm_ffn_g', 'conv_b'):
        o[n] = g[n].reshape(-1)
    o['q_norm_g'] = g['q_norm_g'].reshape(-1)[:D_QK]
    o['k_norm_g'] = g['k_norm_g'].reshape(-1)[:D_QK]
    o['ssm_d'] = g['ssm_d'].reshape(SSM_GROUPS, SSM_GROUP_CH)
    o['ssm_lambda_re'] = g['lr']
    o['ssm_lambda_im'] = g['li']
    o['ssm_log_dt'] = g['log_dt'].reshape(SSM_GROUPS)
    o['ssm_b_re'] = g['br'].transpose(1, 2, 0)
    o['ssm_b_im'] = g['bi'].transpose(1, 2, 0)
    dc = g['c_mat']
    o['ssm_c_re'] = _blockdiag_t(dc[:, :512], SSM_STATE, SSM_GROUP_CH).transpose(0, 1, 3, 2).reshape(SSM_GROUPS, SSM_GROUP_CH, SSM_STATE)
    o['ssm_c_im'] = -_blockdiag_t(dc[:, 512:], SSM_STATE, SSM_GROUP_CH).transpose(0, 1, 3, 2).reshape(SSM_GROUPS, SSM_GROUP_CH, SSM_STATE)
    return o


def _local_step(x, mem, pos, target, w):
    l = x.shape[0]
    tm = min(512, l)
    tabs = _rope_tables(pos.astype(F32).reshape(l, 1))
    ps = [_prep_layer(w, i) for i in range(DEPTH)]
    saved = []
    h = x
    for i in range(DEPTH):
        h, sv = _layer_fwd("l%d" % i, h, tabs, mem, ps[i])
        saved.append(sv)

    def loss_fn(i, y, t):
        e = y - t
        per_tok = jnp.sum(e * e, axis=-1, keepdims=True) * (1.0 / D_MODEL)
        tot = 0.5 * jnp.sum(per_tok, axis=0, keepdims=True)
        return e * (1.0 / D_MODEL), jnp.broadcast_to(tot, (1, LANE))

    dy, loss = _rows(loss_fn, name="loss", n=l // tm, ins=[h, target], in_specs=[_rt(tm, D_MODEL)] * 2,
                     outs=[_sds((l, D_MODEL)), _sds((1, LANE))], out_specs=[_rt(tm, D_MODEL), _full((1, LANE))], n_acc=1)
    grads = []
    d = dy
    for i in reversed(range(DEPTH)):
        d, g = _layer_bwd("l%d" % i, d, saved[i], tabs, mem, ps[i])
        grads.append(_unprep_grads(g))
    grads = grads[::-1]
    full = {n: jnp.stack([grads[i][n] for i in range(DEPTH)], axis=0) for n in WEIGHTS}
    return loss[0, 0], d, full


def _sum_slots(name, y, n_slots):
    r = y.shape[1]
    tr = _row_tile(r)

    def fn(i, v):
        acc = v[0].astype(F32)
        for k in range(1, n_slots):
            acc = acc + v[k].astype(F32)
        return acc

    return _rows(fn, name=name, n=r // tr, ins=[y], in_specs=[pl.BlockSpec((n_slots, tr, PACK_W), lambda i: (0, i, 0))],
                 outs=[_sds((r, PACK_W))], out_specs=[_rt(tr, PACK_W)])[0]


def _row_tile(r):
    for t in (256, 128, 64, 32, 16, 8):
        if r % t == 0:
            return t
    return r


def _adamw(name, parts, w, m, v):
    r = w.shape[0]
    tr = _row_tile(r)
    np_ = len(parts)

    def fn(i, *vals):
        wv, mv, vv = vals[np_:]
        terms = []
        for pv in vals[:np_]:
            terms += [pv] if pv.ndim == 2 else [pv[k] for k in range(pv.shape[0])]
        g = terms[0]
        for t in terms[1:]:
            g = g + t
        mn = ADAM_B1 * mv + (1.0 - ADAM_B1) * g
        vn = ADAM_B2 * vv + (1.0 - ADAM_B2) * (g * g)
        m_hat = mn / (1.0 - ADAM_B1 ** ADAM_STEP)
        v_hat = vn / (1.0 - ADAM_B2 ** ADAM_STEP)
        delta = -ADAM_LR * (m_hat / (jnp.sqrt(v_hat) + ADAM_EPS) + ADAM_WD * wv)
        return g, delta, mn, vn

    pspecs = [_rt(tr, PACK_W) if p.ndim == 2 else pl.BlockSpec((p.shape[0], tr, PACK_W), lambda i: (0, i, 0)) for p in parts]
    return _rows(fn, name=name, n=r // tr, ins=[*parts, w, m, v], in_specs=pspecs + [_rt(tr, PACK_W)] * 3,
                 outs=[_sds((r, PACK_W))] * 4, out_specs=[_rt(tr, PACK_W)] * 4)


def _shard_of(a, axis, k):
    n = a.shape[axis] // 4
    return lax.slice_in_dim(a, k * n, (k + 1) * n, axis=axis)


def _step(a):
    mc = lax.axis_index("c")
    x = a['x'][0]
    mem = a['mem'][0]
    pos = a['positions'][0]
    target = a['loss_target'][0]

    shard_shapes = [a[n].shape for n in GATHER_BF16]
    mine = _pack([a[n] for n in GATHER_BF16], 512, BF16)
    rh = mine.shape[0] // 2
    half = lax.dynamic_slice_in_dim(mine, mc * rh, rh, axis=0)
    halves = _chip_exchange("comm_gather_ici", half, per_peer=False)
    both = _core_exchange("comm_gather_d2d", halves.reshape(D2D_CHUNKS, -1, PACK_W), keep_own=True)
    both = both.reshape(2, 4, rh, PACK_W)
    w = {}
    per_chip = [_unpack(jnp.concatenate([both[0, k], both[1, k]], axis=0), shard_shapes) for k in range(4)]
    for j, n in enumerate(GATHER_BF16):
        w[n] = jnp.concatenate([per_chip[k][j] for k in range(4)], axis=SHARD_AXIS[n])
    cw = _chip_exchange("comm_gather_conv", _pack([a['conv_w']], 8, F32), per_peer=False)
    w['conv_w'] = jnp.concatenate([_unpack(cw[k], [a['conv_w'].shape])[0] for k in range(4)], axis=2)
    for n in SMALL:
        w[n] = a[n]

    loss, grad_x, full = _local_step(x, mem, pos, target, w)

    gsh = jnp.stack([_pack([_shard_of(full[n], SHARD_AXIS[n], k) for n in SHARDED], 256, BF16) for k in range(4)], axis=0)
    got = _chip_exchange("comm_reduce_ici", gsh, per_peer=True)
    part = _sum_slots("reduce_chips", got, 4)
    other = _core_exchange("comm_reduce_d2d", part.reshape(D2D_CHUNKS, -1, PACK_W), keep_own=False).reshape(part.shape)
    sh_shapes = [a[n].shape for n in SHARDED]
    res_sh = _adamw("adamw_sharded", [part, other], *[_pack([a[pre + n] for n in SHARDED], 256, F32) for pre in ('', 'm_', 'v_')])
    res_sh = [_unpack(r, sh_shapes) for r in res_sh]

    sm_shapes = [a[n].shape for n in SMALL] + [(1,)]
    gsm = _pack([full[n] for n in SMALL] + [loss.reshape(1)], 8, F32)
    alls = _all_exchange("comm_reduce_small", gsm)
    zero1 = jnp.zeros((1,), F32)
    res_sm = _adamw("adamw_small", [alls], *[_pack([a[pre + n] for n in SMALL] + [zero1], 8, F32) for pre in ('', 'm_', 'v_')])
    res_sm = [_unpack(r, sm_shapes) for r in res_sm]
    loss = res_sm[0][-1][0]

    outs = [loss, grad_x[None]]
    for kind in range(4):
        byname = dict(zip(SHARDED, res_sh[kind]))
        byname.update(zip(SMALL, res_sm[kind]))
        outs += [byname[n] for n in WEIGHTS]
    return tuple(outs)


def kernel(x, mem, positions, norm_mix_g, w_in, q_a_norm_g, w_q_b, kv_a_norm_g, w_kv_b, q_norm_g, k_norm_g, w_o_mla, ssm_lambda_re, ssm_lambda_im, ssm_log_dt, ssm_b_re, ssm_b_im, ssm_c_re, ssm_c_im, ssm_d, w_glu, b_glu, w_o_ssm, mem_norm_g, w_mem_kv, xq_norm_g, xk_norm_g, w_o_cross, b_gate, w_out, norm_ffn_g, w_up, conv_w, conv_b, w_down, loss_target, m_norm_mix_g, m_w_in, m_q_a_norm_g, m_w_q_b, m_kv_a_norm_g, m_w_kv_b, m_q_norm_g, m_k_norm_g, m_w_o_mla, m_ssm_lambda_re, m_ssm_lambda_im, m_ssm_log_dt, m_ssm_b_re, m_ssm_b_im, m_ssm_c_re, m_ssm_c_im, m_ssm_d, m_w_glu, m_b_glu, m_w_o_ssm, m_mem_norm_g, m_w_mem_kv, m_xq_norm_g, m_xk_norm_g, m_w_o_cross, m_b_gate, m_w_out, m_norm_ffn_g, m_w_up, m_conv_w, m_conv_b, m_w_down, v_norm_mix_g, v_w_in, v_q_a_norm_g, v_w_q_b, v_kv_a_norm_g, v_w_kv_b, v_q_norm_g, v_k_norm_g, v_w_o_mla, v_ssm_lambda_re, v_ssm_lambda_im, v_ssm_log_dt, v_ssm_b_re, v_ssm_b_im, v_ssm_c_re, v_ssm_c_im, v_ssm_d, v_w_glu, v_b_glu, v_w_o_ssm, v_mem_norm_g, v_w_mem_kv, v_xq_norm_g, v_xk_norm_g, v_w_o_cross, v_b_gate, v_w_out, v_norm_ffn_g, v_w_up, v_conv_w, v_conv_b, v_w_down):
    return _step(dict(locals()))
```

```python
import functools
import math

import numpy as np
import jax
import jax.numpy as jnp
from jax import lax
from jax.experimental import pallas as pl
from jax.experimental.pallas import tpu as pltpu

F32 = jnp.float32
BF16 = jnp.bfloat16
MESH = pl.DeviceIdType.MESH

DEPTH = 2
D_MODEL = 1024
EPS = 1e-6
MLA_HEADS = 8
Q_LORA = 384
KV_LORA = 256
D_NOPE = 64
D_ROPE = 32
D_QK = D_NOPE + D_ROPE
D_V = 64
HEAD_PAD = 128
MLA_PAD = MLA_HEADS * HEAD_PAD
ROPE_THETA = 10000.0
SSM_GROUPS = 32
SSM_GROUP_CH = 16
SSM_WIDTH = 512
SSM_STATE = 64
SSM_LANES = SSM_GROUPS * SSM_STATE
SSM_JB = 4
X_HEADS = 4
X_HEAD_DIM = 128
X_WIDTH = 512
D_FF = 2816
SMALL_W = Q_LORA + KV_LORA + HEAD_PAD
SCAN_SEGS = 32
LANE = 128
NEG = -1e30

ADAM_LR = 0.001
ADAM_B1 = 0.9
ADAM_B2 = 0.999
ADAM_EPS = 1e-08
ADAM_WD = 0.01
ADAM_STEP = 10

WEIGHTS = ['norm_mix_g', 'w_in', 'q_a_norm_g', 'w_q_b', 'kv_a_norm_g', 'w_kv_b', 'q_norm_g', 'k_norm_g', 'w_o_mla',
           'ssm_lambda_re', 'ssm_lambda_im', 'ssm_log_dt', 'ssm_b_re', 'ssm_b_im', 'ssm_c_re', 'ssm_c_im', 'ssm_d',
           'w_glu', 'b_glu', 'w_o_ssm', 'mem_norm_g', 'w_mem_kv', 'xq_norm_g', 'xk_norm_g', 'w_o_cross', 'b_gate',
           'w_out', 'norm_ffn_g', 'w_up', 'conv_w', 'conv_b', 'w_down']
SHARD_AXIS = {'w_in': 2, 'w_q_b': 2, 'w_kv_b': 2, 'w_o_mla': 2, 'w_glu': 1, 'w_o_ssm': 2, 'w_mem_kv': 1,
              'w_o_cross': 2, 'w_out': 1, 'w_up': 2, 'conv_w': 2, 'w_down': 1}
SHARDED = [n for n in WEIGHTS if n in SHARD_AXIS]
GATHER_BF16 = [n for n in SHARDED if n != 'conv_w']
SMALL = [n for n in WEIGHTS if n not in SHARD_AXIS]


def _bf(v):
    return v.astype(BF16)


def _mm(a, b):
    return jnp.dot(_bf(a), _bf(b), preferred_element_type=F32)


def _mm_nt(a, b):
    return lax.dot_general(_bf(a), _bf(b), (((1,), (1,)), ((), ())), preferred_element_type=F32)


def _mm_tn(a, b):
    return lax.dot_general(_bf(a), _bf(b), (((0,), (0,)), ((), ())), preferred_element_type=F32)


def _rms(v, g, n):
    ms = jnp.sum(v * v, axis=-1, keepdims=True) * (1.0 / n)
    return (v * lax.rsqrt(ms + EPS)) * g


def _head_rms(v, g, heads, n):
    return jnp.concatenate([_rms(v[:, h * LANE:(h + 1) * LANE], g, n) for h in range(heads)], axis=-1)


def _rope(v, c, s1, s2):
    return v * c + pltpu.roll(v, LANE - 16, 1) * s1 + pltpu.roll(v, 16, 1) * s2


def _rope_t(g, c, s1, s2):
    return g * c + pltpu.roll(g * s1, 16, 1) + pltpu.roll(g * s2, LANE - 16, 1)


def _heads(fn, v, heads, *tabs):
    return jnp.concatenate([fn(v[:, h * LANE:(h + 1) * LANE], *tabs) for h in range(heads)], axis=-1)


def _gelu(y):
    return y * (0.5 * (1.0 + jnp.tanh(math.sqrt(2.0 / math.pi) * (y + 0.044715 * (y * y * y)))))


def _silu(g):
    return g * jax.nn.sigmoid(g)


def _colsum(v):
    return jnp.sum(v, axis=0, keepdims=True)


def _row_select(rows, n):
    rid = lax.broadcasted_iota(jnp.int32, (n, rows[0].shape[-1]), 0)
    out = jnp.zeros((n, rows[0].shape[-1]), F32)
    for k, r in enumerate(rows):
        out = jnp.where(rid == k, jnp.broadcast_to(r, out.shape), out)
    return out


def _params(sem, vmem_mb):
    return pltpu.CompilerParams(dimension_semantics=sem, vmem_limit_bytes=vmem_mb * 1024 * 1024)


def _rt(tm, w, cb=0):
    return pl.BlockSpec((tm, w), lambda i: (i, cb))


def _full(shape):
    nd = len(shape)
    return pl.BlockSpec(tuple(shape), lambda i: (0,) * nd)


def _rows(fn, *, name, n, ins, in_specs, outs, out_specs, n_acc=0, vmem=48):
    n_in = len(ins)
    n_out = len(outs)

    def body(*refs):
        i = pl.program_id(0)
        res = fn(i, *[r[...] for r in refs[:n_in]])
        if not isinstance(res, (tuple, list)):
            res = (res,)
        assert len(res) == n_out, (name, len(res), n_out)
        for k, (r, v) in enumerate(zip(refs[n_in:], res)):
            if k < n_out - n_acc:
                r[...] = v.astype(r.dtype)
            else:
                @pl.when(i == 0)
                def _():
                    r[...] = v

                @pl.when(i > 0)
                def _():
                    r[...] += v

    return pl.pallas_call(
        body, name=name, grid=(n,), in_specs=list(in_specs), out_specs=tuple(out_specs), out_shape=tuple(outs),
        compiler_params=_params(("arbitrary",), vmem))(*ins)


def _sds(shape, dtype=F32):
    return jax.ShapeDtypeStruct(tuple(shape), dtype)


def _tile_n(n, cap=1024):
    best = None
    for t in range(LANE, min(n, cap) + 1, LANE):
        if n % t == 0:
            best = t
    if best is None or n <= 1408:
        return n
    return best


def _matmul(name, pairs, m, n, *, nt=False, rms_gain=None, resid=None, out_dtype=F32, tm=512, vmem=56):
    tm = min(tm, m)
    tn = _tile_n(n)
    ks = [a.shape[1] for a, _ in pairs]
    np_ = len(pairs)

    def body(*refs):
        a_refs = refs[:np_]
        b_refs = refs[np_:2 * np_]
        k = 2 * np_
        g_ref = None
        r_ref = None
        if rms_gain is not None:
            g_ref = refs[k]
            k += 1
        if resid is not None:
            r_ref = refs[k]
            k += 1
        o_ref = refs[k]
        scr = refs[k + 1:]
        j = pl.program_id(1)

        @pl.when(j == 0)
        def _():
            for p in range(np_):
                a = a_refs[p][...]
                if p == 0 and g_ref is not None:
                    a = _rms(a.astype(F32), g_ref[...], ks[0])
                scr[p][...] = a.astype(BF16)

        acc = None
        for p in range(np_):
            b = b_refs[p][...].astype(BF16)
            if nt:
                t = lax.dot_general(scr[p][...], b, (((1,), (1,)), ((), ())), preferred_element_type=F32)
            else:
                t = jnp.dot(scr[p][...], b, preferred_element_type=F32)
            acc = t if acc is None else acc + t
        if r_ref is not None:
            acc = acc + r_ref[...]
        o_ref[...] = acc.astype(o_ref.dtype)

    in_specs = [pl.BlockSpec((tm, kk), lambda i, j: (i, 0)) for kk in ks]
    if nt:
        in_specs += [pl.BlockSpec((tn, kk), lambda i, j: (j, 0)) for kk in ks]
    else:
        in_specs += [pl.BlockSpec((kk, tn), lambda i, j: (0, j)) for kk in ks]
    ins = [a for a, _ in pairs] + [b for _, b in pairs]
    if rms_gain is not None:
        in_specs.append(pl.BlockSpec((1, ks[0]), lambda i, j: (0, 0)))
        ins.append(rms_gain)
    if resid is not None:
        in_specs.append(pl.BlockSpec((tm, tn), lambda i, j: (i, j)))
        ins.append(resid)
    return pl.pallas_call(
        body, name=name, grid=(m // tm, n // tn), in_specs=in_specs,
        out_specs=pl.BlockSpec((tm, tn), lambda i, j: (i, j)), out_shape=_sds((m, n), out_dtype),
        scratch_shapes=[pltpu.VMEM((tm, kk), BF16) for kk in ks],
        compiler_params=_params(("arbitrary", "arbitrary"), vmem))(*ins)


def _matmul_tn(name, a, b, *, rms_gain=None, tl=512, vmem=56):
    l, ka = a.shape
    n = b.shape[1]
    tl = min(tl, l)
    tn = _tile_n(n, 512)

    def body(*refs):
        if rms_gain is not None:
            a_ref, b_ref, g_ref, o_ref = refs
        else:
            a_ref, b_ref, o_ref = refs
        t = pl.program_id(1)
        av = a_ref[...]
        if rms_gain is not None:
            av = _rms(av.astype(F32), g_ref[...], ka)
        v = _mm_tn(av, b_ref[...])

        @pl.when(t == 0)
        def _():
            o_ref[...] = v

        @pl.when(t > 0)
        def _():
            o_ref[...] += v

    in_specs = [pl.BlockSpec((tl, ka), lambda j, t: (t, 0)), pl.BlockSpec((tl, tn), lambda j, t: (t, j))]
    ins = [a, b]
    if rms_gain is not None:
        in_specs.append(pl.BlockSpec((1, ka), lambda j, t: (0, 0)))
        ins.append(rms_gain)
    return pl.pallas_call(
        body, name=name, grid=(n // tn, l // tl), in_specs=in_specs,
        out_specs=pl.BlockSpec((ka, tn), lambda j, t: (0, j)), out_shape=_sds((ka, n)),
        compiler_params=_params(("arbitrary", "arbitrary"), vmem))(*ins)


ATT_HEADS_PER_STEP = 2
ATT_W = ATT_HEADS_PER_STEP * LANE
ATT_GROUPS = MLA_HEADS // ATT_HEADS_PER_STEP
LOG2E = math.log2(math.e)
ATT_SCALE = D_QK ** -0.5
ATT_QSCALE = ATT_SCALE * LOG2E


def _tri_tables(nq, by_k):
    qs, ks = [], []
    if by_k:
        for ki in range(nq):
            for qi in range(ki, nq):
                qs.append(qi)
                ks.append(ki)
    else:
        for qi in range(nq):
            for ki in range(qi + 1):
                qs.append(qi)
                ks.append(ki)
    return jnp.asarray(np.array(qs, np.int32)), jnp.asarray(np.array(ks, np.int32))


def _causal_keep(shape, transposed):
    r = lax.broadcasted_iota(jnp.int32, shape, 0)
    c = lax.broadcasted_iota(jnp.int32, shape, 1)
    return (r <= c) if transposed else (c <= r)


def _nt16(a, b):
    return lax.dot_general(a, b, (((1,), (1,)), ((), ())), preferred_element_type=F32)


def _row_form(col):
    return jnp.transpose(jnp.broadcast_to(col, (col.shape[0], LANE)))[:8]


def _att_call(body, name, l, tq, tabs, ins, in_specs, outs, out_specs, scratch=()):
    grid_spec = pltpu.PrefetchScalarGridSpec(
        num_scalar_prefetch=2, grid=(ATT_GROUPS, tabs[0].shape[0]), in_specs=in_specs, out_specs=out_specs,
        scratch_shapes=list(scratch))
    return pl.pallas_call(body, name=name, grid_spec=grid_spec, out_shape=outs,
                          compiler_params=_params(("arbitrary", "arbitrary"), 48))(*tabs, *ins)


def _flash_fwd(name, q, k, v):
    l = q.shape[0]
    tq = min(512, l)
    nq = l // tq
    tabs = _tri_tables(nq, by_k=False)

    def body(qt, kt, q_ref, k_ref, v_ref, o_ref, lse_ref, lset_ref, m_s, acc_s):
        t = pl.program_id(1)
        qi = qt[t]
        ki = kt[t]

        @pl.when(ki == 0)
        def _():
            m_s[...] = jnp.full(m_s.shape, NEG, F32)
            acc_s[...] = jnp.zeros(acc_s.shape, F32)

        def step(masked):
            sls =[slice(h * LANE, (h + 1) * LANE) for h in range(ATT_HEADS_PER_STEP)]
            ss = [_nt16(q_ref[:, sl], k_ref[:, sl]) for sl in sls]
            for h, sl in enumerate(sls):
                s = ss[h]
                if masked:
                    s = jnp.where(_causal_keep(s.shape, False), s, NEG)
                m_old = m_s[h]
                m_new = jnp.maximum(m_old, jnp.max(s, axis=-1, keepdims=True))
                alpha = jnp.exp2(m_old - m_new)
                p = jnp.exp2(s - m_new)
                acc_s[:, sl] = alpha * acc_s[:, sl] + jnp.dot(p.astype(BF16), v_ref[:, sl], preferred_element_type=F32)
                m_s[h] = m_new

        @pl.when(ki < qi)
        def _():
            step(False)

        @pl.when(ki == qi)
        def _():
            step(True)
            lane = lax.broadcasted_iota(jnp.int32, (tq, LANE), 1)
            for h in range(ATT_HEADS_PER_STEP):
                sl = slice(h * LANE, (h + 1) * LANE)
                acc = acc_s[:, sl]
                lsum = acc[:, D_V:D_V + 1]
                o_ref[:, sl] = jnp.where(lane < D_V, acc / lsum, 0.0)
                lse = m_s[h] + jnp.log2(lsum)
                lse_ref[:, sl] = jnp.broadcast_to(lse, (tq, LANE))
                lset_ref[h * 8:(h + 1) * 8, :] = _row_form(lse)

    qspec = pl.BlockSpec((tq, ATT_W), lambda g, t, qt, kt: (qt[t], g))
    kspec = pl.BlockSpec((tq, ATT_W), lambda g, t, qt, kt: (kt[t], g))
    rspec = pl.BlockSpec((8 * ATT_HEADS_PER_STEP, tq), lambda g, t, qt, kt: (g, qt[t]))
    return _att_call(
        body, name, l, tq, tabs, [q, k, v], [qspec, kspec, kspec],
        (_sds((l, MLA_PAD)), _sds((l, MLA_PAD)), _sds((8 * MLA_HEADS, l))), (qspec, qspec, rspec),
        scratch=[pltpu.VMEM((ATT_HEADS_PER_STEP, tq, 1), F32), pltpu.VMEM((tq, ATT_W), F32)])


def _flash_bwd(name, q, k, v, o, lse, lse_t, do):
    l = q.shape[0]
    tq = min(512, l)
    nq = l // tq

    def delta_fn(i, dov, ov):
        cols, rows = [], []
        for h in range(MLA_HEADS):
            sl = slice(h * LANE, (h + 1) * LANE)
            d = jnp.sum(dov[:, sl] * ov[:, sl], axis=-1, keepdims=True)
            cols.append(jnp.broadcast_to(d, (tq, LANE)))
            rows.append(_row_form(d))
        return jnp.concatenate(cols, axis=-1), jnp.concatenate(rows, axis=0), dov

    delta, delta_t, do16 = _rows(
        delta_fn, name=name + "_delta", n=nq, ins=[do, o], in_specs=[_rt(tq, MLA_PAD)] * 2,
        outs=[_sds((l, MLA_PAD)), _sds((8 * MLA_HEADS, l)), _sds((l, MLA_PAD), BF16)],
        out_specs=[_rt(tq, MLA_PAD), pl.BlockSpec((8 * MLA_HEADS, tq), lambda i: (0, i)), _rt(tq, MLA_PAD)])

    def dq_body(qt, kt, q_ref, k_ref, v_ref, do_ref, lse_ref, dl_ref, dq_ref):
        t = pl.program_id(1)
        qi = qt[t]
        ki = kt[t]

        @pl.when(ki == 0)
        def _():
            dq_ref[...] = jnp.zeros(dq_ref.shape, F32)

        def step(masked):
            sls = [slice(h * LANE, (h + 1) * LANE) for h in range(ATT_HEADS_PER_STEP)]
            ss = [_nt16(q_ref[:, sl], k_ref[:, sl]) for sl in sls]
            dps = [_nt16(do_ref[:, sl], v_ref[:, sl]) for sl in sls]
            for h, sl in enumerate(sls):
                s = ss[h]
                if masked:
                    s = jnp.where(_causal_keep(s.shape, False), s, NEG)
                p = jnp.exp2(s - lse_ref[:, sl][:, :1])
                ds = p * (dps[h] - dl_ref[:, sl][:, :1])
                dq_ref[:, sl] += jnp.dot(ds.astype(BF16), k_ref[:, sl], preferred_element_type=F32)

        @pl.when(ki < qi)
        def _():
            step(False)

        @pl.when(ki == qi)
        def _():
            step(True)
            dq_ref[...] = dq_ref[...] * ATT_SCALE

    tabs = _tri_tables(nq, by_k=False)
    qspec = pl.BlockSpec((tq, ATT_W), lambda g, t, qt, kt: (qt[t], g))
    kspec = pl.BlockSpec((tq, ATT_W), lambda g, t, qt, kt: (kt[t], g))
    dq = _att_call(dq_body, name + "_dq", l, tq, tabs, [q, k, v, do16, lse, delta],
                   [qspec, kspec, kspec, qspec, qspec, qspec], _sds((l, MLA_PAD)), qspec)

    def dkv_body(qt, kt, q_ref, k_ref, v_ref, do_ref, lset_ref, dlt_ref, dk_ref, dv_ref):
        t = pl.program_id(1)
        qi = qt[t]
        ki = kt[t]

        def step(masked):
            sls = [slice(h * LANE, (h + 1) * LANE) for h in range(ATT_HEADS_PER_STEP)]
            sts = [_nt16(k_ref[:, sl], q_ref[:, sl]) for sl in sls]
            dpts = [_nt16(v_ref[:, sl], do_ref[:, sl]) for sl in sls]
            for h, sl in enumerate(sls):
                st = sts[h]
                if masked:
                    st = jnp.where(_causal_keep(st.shape, True), st, NEG)
                pt = jnp.exp2(st - lset_ref[h * 8:(h + 1) * 8, :][:1])
                dst = pt * (dpts[h] - dlt_ref[h * 8:(h + 1) * 8, :][:1])
                dv_ref[:, sl] += jnp.dot(pt.astype(BF16), do_ref[:, sl], preferred_element_type=F32)
                dk_ref[:, sl] += jnp.dot(dst.astype(BF16), q_ref[:, sl], preferred_element_type=F32)

        @pl.when(qi == ki)
        def _():
            dk_ref[...] = jnp.zeros(dk_ref.shape, F32)
            dv_ref[...] = jnp.zeros(dv_ref.shape, F32)
            step(True)

        @pl.when(qi > ki)
        def _():
            step(False)

        @pl.when(qi == nq - 1)
        def _():
            dk_ref[...] = dk_ref[...] * (1.0 / LOG2E)

    tabs_k = _tri_tables(nq, by_k=True)
    rspec = pl.BlockSpec((8 * ATT_HEADS_PER_STEP, tq), lambda g, t, qt, kt: (g, qt[t]))
    dk, dv = _att_call(dkv_body, name + "_dkv", l, tq, tabs_k, [q, k, v, do16, lse_t, delta_t],
                       [qspec, kspec, kspec, qspec, rspec, rspec], (_sds((l, MLA_PAD)), _sds((l, MLA_PAD))), (kspec, kspec))
    return dq, dk, dv


def _cmul(ar, ai, br, bi):
    return ar * br - ai * bi, ar * bi + ai * br


def _scan(name, x_re, x_im, a_re, a_im, reverse):
    l, lanes = x_re.shape
    ns = SCAN_SEGS
    tl = l // ns
    steps = int(math.log2(tl))
    assert 2 ** steps == tl and tl * ns == l

    def body(xr_ref, xi_ref, ar_ref, ai_ref, sr_ref, si_ref):
        a_r1 = ar_ref[...]
        a_i1 = ai_ref[...]
        a_r = jnp.broadcast_to(a_r1, (ns, LANE))
        a_i = jnp.broadcast_to(a_i1, (ns, LANE))

        def rows(t):
            t = (tl - 1 - t) if reverse else t
            return pl.ds(pl.multiple_of(t * ns, ns), ns)

        def local(t, carry):
            cr, ci = carry
            r = rows(t)
            pr, pi = _cmul(a_r, a_i, cr, ci)
            return pr + xr_ref[r, :], pi + xi_ref[r, :]

        zero = jnp.zeros((ns, LANE), F32)
        e_r, e_i = lax.fori_loop(0, tl, local, (zero, zero))
        p_r, p_i = a_r1, a_i1
        for _ in range(steps):
            p_r, p_i = _cmul(p_r, p_i, p_r, p_i)
        rid = lax.broadcasted_iota(jnp.int32, (ns, LANE), 0)
        c_r = jnp.zeros((1, LANE), F32)
        c_i = jnp.zeros((1, LANE), F32)
        in_r, in_i = zero, zero
        order = range(ns - 2, -1, -1) if reverse else range(1, ns)
        for kk in order:
            src = kk + 1 if reverse else kk - 1
            ek_r = jnp.sum(jnp.where(rid == src, e_r, 0.0), axis=0, keepdims=True)
            ek_i = jnp.sum(jnp.where(rid == src, e_i, 0.0), axis=0, keepdims=True)
            q_r, q_i = _cmul(p_r, p_i, c_r, c_i)
            c_r, c_i = q_r + ek_r, q_i + ek_i
            in_r = jnp.where(rid == kk, jnp.broadcast_to(c_r, (ns, LANE)), in_r)
            in_i = jnp.where(rid == kk, jnp.broadcast_to(c_i, (ns, LANE)), in_i)

        def final(t, carry):
            cr, ci = carry
            r = rows(t)
            pr, pi = _cmul(a_r, a_i, cr, ci)
            nr, ni = pr + xr_ref[r, :], pi + xi_ref[r, :]
            sr_ref[r, :] = nr
            si_ref[r, :] = ni
            return nr, ni

        lax.fori_loop(0, tl, final, (in_r, in_i))

    xs = pl.BlockSpec((l, LANE), lambda j: (0, j))
    as_ = pl.BlockSpec((1, LANE), lambda j: (0, j))
    return pl.pallas_call(
        body, name=name, grid=(lanes // LANE,), in_specs=[xs, xs, as_, as_], out_specs=(xs, xs),
        out_shape=(_sds((l, lanes)), _sds((l, lanes))),
        compiler_params=_params(("arbitrary",), 48))(x_re, x_im, a_re, a_im)


ANY = pl.BlockSpec(memory_space=pl.ANY)


def _place():
    mx, my, mc = lax.axis_index("x"), lax.axis_index("y"), lax.axis_index("c")
    return mx, my, mc, [(1 - mx, my), (mx, 1 - my), (1 - mx, 1 - my)]


def _run_copies(copies):
    for cp in copies:
        cp.start()
    for cp in copies:
        cp.wait_recv()
    for cp in copies:
        cp.wait_send()


def _remote(src, dst, sems, k, dev):
    return pltpu.make_async_remote_copy(src_ref=src, dst_ref=dst, send_sem=sems[0].at[k], recv_sem=sems[1].at[k],
                                        device_id=dev, device_id_type=MESH)


def _copy_call(body, name, ins, outs, n_copies, aliases=None):
    return pl.pallas_call(
        body, name=name, in_specs=[ANY] * len(ins), out_specs=[ANY] * len(outs), out_shape=list(outs),
        input_output_aliases=aliases or {},
        scratch_shapes=[pltpu.SemaphoreType.DMA((n_copies,)), pltpu.SemaphoreType.DMA((n_copies,))])(*ins)


def _gather_ici(xs):
    n = len(xs)

    def body(*refs):
        x_refs, y_refs, sems = refs[:n], refs[n:2 * n], refs[2 * n:]
        mx, my, mc, peers = _place()
        me = 2 * mx + my
        _run_copies([_remote(x_refs[i].at[mc], y_refs[i].at[me, mc], sems, 3 * i + j, (px, py, mc))
                     for i in range(n) for j, (px, py) in enumerate(peers)])

    return _copy_call(body, "comm_gather_ici", xs, [_sds((4,) + x.shape, x.dtype) for x in xs], 3 * n)


def _gather_d2d(ys):
    n = len(ys)

    def body(*refs):
        y_in, y_out, sems = refs[:n], refs[n:2 * n], refs[2 * n:]
        mx, my, mc, peers = _place()
        _run_copies([_remote(y_in[i].at[2 * px + py, mc], y_out[i].at[2 * px + py, mc], sems, 3 * i + j, (mx, my, 1 - mc))
                     for i in range(n) for j, (px, py) in enumerate(peers)])

    return _copy_call(body, "comm_gather_d2d", ys, [_sds(y.shape, y.dtype) for y in ys], 3 * n,
                      aliases={i: i for i in range(n)})


def _reduce_ici(gs):
    n = len(gs)

    def body(*refs):
        g_refs, y_refs, sems = refs[:n], refs[n:2 * n], refs[2 * n:]
        mx, my, mc, peers = _place()
        _run_copies([_remote(g_refs[i].at[2 * px + py], y_refs[i].at[j], sems, 3 * i + j, (px, py, mc))
                     for i in range(n) for j, (px, py) in enumerate(peers)])

    return _copy_call(body, "comm_reduce_ici", gs, [_sds((3,) + g.shape[1:], g.dtype) for g in gs], 3 * n)


def _swap_d2d(ps):
    n = len(ps)
    depth = ps[0].shape[0]

    def body(*refs):
        p_refs, o_refs, sems = refs[:n], refs[n:2 * n], refs[2 * n:]
        mx, my, mc, _ = _place()
        _run_copies([_remote(p_refs[i].at[l], o_refs[i].at[l], sems, depth * i + l, (mx, my, 1 - mc))
                     for i in range(n) for l in range(depth)])

    return _copy_call(body, "comm_reduce_d2d", ps, [_sds(p.shape, p.dtype) for p in ps], depth * n)


def _all_exchange(name, src):
    def body(x_ref, y_ref, send_sems, recv_sems, local_sem):
        mx, my, mc = lax.axis_index("x"), lax.axis_index("y"), lax.axis_index("c")
        me = 4 * mx + 2 * my + mc
        own = pltpu.make_async_copy(x_ref, y_ref.at[me], local_sem)
        own.start()
        copies = []
        for j in range(1, 8):
            px = (1 - mx) if (j & 4) else mx
            py = (1 - my) if (j & 2) else my
            pc = (1 - mc) if (j & 1) else mc
            cp = pltpu.make_async_remote_copy(
                src_ref=x_ref, dst_ref=y_ref.at[me], send_sem=send_sems.at[j - 1], recv_sem=recv_sems.at[j - 1],
                device_id=(px, py, pc), device_id_type=MESH)
            cp.start()
            copies.append(cp)
        for cp in copies:
            cp.wait_recv()
        for cp in copies:
            cp.wait_send()
        own.wait()

    return pl.pallas_call(
        body, name=name, in_specs=[ANY], out_specs=ANY, out_shape=_sds((8,) + src.shape, src.dtype),
        scratch_shapes=[pltpu.SemaphoreType.DMA((7,)), pltpu.SemaphoreType.DMA((7,)), pltpu.SemaphoreType.DMA])(src)


PACK_W = 1024


def _pack(arrs, rows_multiple, dtype):
    flat = jnp.concatenate([a.reshape(-1).astype(dtype) for a in arrs])
    n = flat.shape[0]
    unit = PACK_W * rows_multiple
    tot = -(-n // unit) * unit
    flat = jnp.pad(flat, (0, tot - n))
    return flat.reshape(tot // PACK_W, PACK_W)


def _unpack(flat, shapes):
    flat = flat.reshape(-1)
    out = []
    off = 0
    for s in shapes:
        n = int(np.prod(s))
        out.append(flat[off:off + n].reshape(s))
        off += n
    return out


def _rope_tables(pos):
    l = pos.shape[0]
    tm = min(512, l)
    inv = (np.float32(ROPE_THETA) ** (-np.arange(0, D_ROPE, 2, dtype=np.float32) / np.float32(D_ROPE))).astype(np.float32)
    lane_f = np.zeros((1, LANE), np.float32)
    lane_f[0, D_NOPE:D_NOPE + 16] = inv
    lane_f[0, D_NOPE + 16:D_NOPE + 32] = inv

    def fn(i, p, f):
        ang = p * f
        lane = lax.broadcasted_iota(jnp.int32, ang.shape, 1)
        co = jnp.cos(ang)
        si = jnp.sin(ang)
        c = jnp.where(lane < D_NOPE, 1.0, jnp.where(lane < D_QK, co, 0.0))
        s1 = jnp.where((lane >= D_NOPE) & (lane < D_NOPE + 16), -si, 0.0)
        s2 = jnp.where((lane >= D_NOPE + 16) & (lane < D_QK), si, 0.0)
        return c, s1, s2

    return _rows(fn, name="rope_tables", n=l // tm, ins=[pos, jnp.asarray(lane_f)],
                 in_specs=[_rt(tm, 1), _full((1, LANE))], outs=[_sds((l, LANE))] * 3, out_specs=[_rt(tm, LANE)] * 3)


def _ssm_param_fn(lr, li, log_dt, br, bi):
    dt = jnp.exp(log_dt)
    mag = jnp.exp(lr * dt)
    a_re = mag * jnp.cos(li * dt)
    a_im = mag * jnp.sin(li * dt)
    den = lr * lr + li * li
    e_re = a_re - 1.0
    e_im = a_im
    f_re = (e_re * lr + e_im * li) / den
    f_im = (e_im * lr - e_re * li) / den
    bb_re = f_re[None] * br - f_im[None] * bi
    bb_im = f_re[None] * bi + f_im[None] * br
    return a_re, a_im, bb_re, bb_im


def _ssm_params(name, lr, li, log_dt, br, bi):
    g, n = lr.shape
    c = br.shape[0]
    return _rows(lambda i, *v: _ssm_param_fn(*v), name=name, n=1, ins=[lr, li, log_dt, br, bi],
                 in_specs=[_full((g, n)), _full((g, n)), _full((g, 1)), _full((c, g, n)), _full((c, g, n))],
                 outs=[_sds((g, n)), _sds((g, n)), _sds((c, g, n)), _sds((c, g, n))],
                 out_specs=[_full((g, n)), _full((g, n)), _full((c, g, n)), _full((c, g, n))])


def _ssm_params_bwd(name, lr, li, log_dt, br, bi, d_are, d_aim, d_bbre, d_bbim):
    g, n = lr.shape
    c = br.shape[0]

    def fn(i, lr, li, log_dt, br, bi, g0, g1, g2, g3):
        _, vjp = jax.vjp(_ssm_param_fn, lr, li, log_dt, br, bi)
        return vjp((g0, g1, g2, g3))

    sp = [_full((g, n)), _full((g, n)), _full((g, 1)), _full((c, g, n)), _full((c, g, n))]
    return _rows(fn, name=name, n=1, ins=[lr, li, log_dt, br, bi, d_are, d_aim, d_bbre, d_bbim],
                 in_specs=sp + [_full((g, n)), _full((g, n)), _full((c, g, n)), _full((c, g, n))],
                 outs=[_sds((g, n)), _sds((g, n)), _sds((g, 1)), _sds((c, g, n)), _sds((c, g, n))], out_specs=sp)


_EYE8 = np.eye(8, dtype=np.float32)


def _blockdiag(v):
    j, g, p, q = v.shape
    m = v[:, :, :, None, :] * jnp.asarray(_EYE8)[None, :, None, :, None]
    return m.reshape(j, g * p, g * q)


def _blockdiag_t(m, p, q):
    j = m.shape[0]
    m = m.reshape(j, 8, p, 8, q)
    return jnp.sum(m * jnp.asarray(_EYE8)[None, :, None, :, None], axis=3)


def _to_perm(v, l):
    ns = SCAN_SEGS
    return v.reshape(ns, l // ns, v.shape[-1]).transpose(1, 0, 2).reshape(l, v.shape[-1])


def _from_perm(v, l):
    ns = SCAN_SEGS
    return v.reshape(l // ns, ns, v.shape[-1]).transpose(1, 0, 2).reshape(l, v.shape[-1])


def _prep_layer(w, i):
    p = {}
    w_in = w['w_in'][i]
    z = lambda n: jnp.zeros((D_MODEL, n), w_in.dtype)
    o = Q_LORA + KV_LORA
    p['w_s'] = jnp.concatenate([w_in[:, :o], z(D_NOPE), w_in[:, o:o + D_ROPE], z(HEAD_PAD - D_QK)], axis=1)
    o += D_ROPE
    p['w_u'] = w_in[:, o:o + SSM_WIDTH]
    o += SSM_WIDTH
    p['w_xq'] = w_in[:, o:o + X_WIDTH]
    o += X_WIDTH
    p['w_g'] = w_in[:, o:]
    wq = w['w_q_b'][i].reshape(Q_LORA, MLA_HEADS, D_QK)
    p['w_qb'] = jnp.pad(wq, ((0, 0), (0, 0), (0, HEAD_PAD - D_QK))).reshape(Q_LORA, MLA_PAD)
    wkv = w['w_kv_b'][i].reshape(KV_LORA, MLA_HEADS, D_NOPE + D_V)
    p['w_k'] = jnp.pad(wkv[:, :, :D_NOPE], ((0, 0), (0, 0), (0, HEAD_PAD - D_NOPE))).reshape(KV_LORA, MLA_PAD)
    p['w_v'] = jnp.pad(wkv[:, :, D_NOPE:], ((0, 0), (0, 0), (0, HEAD_PAD - D_V))).reshape(KV_LORA, MLA_PAD)
    wo = w['w_o_mla'][i].reshape(MLA_HEADS, D_V, D_MODEL)
    p['w_oa'] = jnp.pad(wo, ((0, 0), (0, HEAD_PAD - D_V), (0, 0))).reshape(MLA_PAD, D_MODEL)
    for n in ('w_glu', 'w_o_ssm', 'w_mem_kv', 'w_o_cross', 'w_out', 'w_up', 'w_down'):
        p[n] = w[n][i]
    p['conv_w'] = w['conv_w'][i]
    for n in ('norm_mix_g', 'q_a_norm_g', 'kv_a_norm_g', 'b_glu', 'mem_norm_g', 'xq_norm_g', 'xk_norm_g', 'b_gate',
              'norm_ffn_g', 'conv_b'):
        p[n] = w[n][i].reshape(1, -1)
    p['q_norm_g'] = jnp.pad(w['q_norm_g'][i], (0, HEAD_PAD - D_QK)).reshape(1, HEAD_PAD)
    p['k_norm_g'] = jnp.pad(w['k_norm_g'][i], (0, HEAD_PAD - D_QK)).reshape(1, HEAD_PAD)
    p['ssm_d'] = w['ssm_d'][i].reshape(1, SSM_WIDTH)
    p['lr'] = w['ssm_lambda_re'][i]
    p['li'] = w['ssm_lambda_im'][i]
    p['log_dt'] = w['ssm_log_dt'][i].reshape(SSM_GROUPS, 1)
    p['br'] = w['ssm_b_re'][i].transpose(2, 0, 1)
    p['bi'] = w['ssm_b_im'][i].transpose(2, 0, 1)
    cr = w['ssm_c_re'][i].reshape(SSM_JB, 8, SSM_GROUP_CH, SSM_STATE).transpose(0, 1, 3, 2)
    ci = w['ssm_c_im'][i].reshape(SSM_JB, 8, SSM_GROUP_CH, SSM_STATE).transpose(0, 1, 3, 2)
    p['c_mat'] = jnp.concatenate([_blockdiag(cr), -_blockdiag(ci)], axis=1).astype(BF16)
    return p


def _b_mat(bb_re, bb_im):
    r = bb_re.transpose(1, 0, 2).reshape(SSM_JB, 8, SSM_GROUP_CH, SSM_STATE)
    i = bb_im.transpose(1, 0, 2).reshape(SSM_JB, 8, SSM_GROUP_CH, SSM_STATE)
    return jnp.concatenate([_blockdiag(r), _blockdiag(i)], axis=2).astype(BF16)


def _qkv_fn(ps, c, s1, s2, qag, wqb, kvag, wk, wv, qng, kng):
    c_q = ps[:, :Q_LORA]
    c_kv = ps[:, Q_LORA:Q_LORA + KV_LORA]
    kr = ps[:, Q_LORA + KV_LORA:]
    cqn = _rms(c_q, qag, Q_LORA)
    ckvn = _rms(c_kv, kvag, KV_LORA)
    q_raw = _mm(cqn, wqb)
    k_raw = _mm(ckvn, wk) + jnp.concatenate([kr] * MLA_HEADS, axis=-1)
    v = _mm(ckvn, wv)
    q = _heads(_rope, _head_rms(q_raw, qng, MLA_HEADS, D_QK), MLA_HEADS, c, s1, s2)
    k = _heads(_rope, _head_rms(k_raw, kng, MLA_HEADS, D_QK), MLA_HEADS, c, s1, s2)
    lane = lax.broadcasted_iota(jnp.int32, v.shape, 1)
    v = jnp.where((lane & (LANE - 1)) == D_V, 1.0, v)
    return q * ATT_QSCALE, k, v


def _layer_fwd(name, x, tabs, mem, p):
    l = x.shape[0]
    tm = min(512, l)
    nt = l // tm
    sv = {'x0': x}
    sv['p_g'] = _matmul(name + "_in_g", [(x, p['w_g'])], l, 3 * D_MODEL, rms_gain=p['norm_mix_g'])
    sv['p_u'] = _matmul(name + "_in_u", [(x, p['w_u'])], l, SSM_WIDTH, rms_gain=p['norm_mix_g'])
    sv['p_xq'] = _matmul(name + "_in_xq", [(x, p['w_xq'])], l, X_WIDTH, rms_gain=p['norm_mix_g'])
    sv['p_s'] = _matmul(name + "_in_s", [(x, p['w_s'])], l, SMALL_W, rms_gain=p['norm_mix_g'])

    qkv_consts = [p['q_a_norm_g'], p['w_qb'], p['kv_a_norm_g'], p['w_k'], p['w_v'], p['q_norm_g'], p['k_norm_g']]
    qkv_cspecs = [_full(a.shape) for a in qkv_consts]
    q, k, v = _rows(lambda i, *a: _qkv_fn(*a), name=name + "_qkv", n=nt, ins=[sv['p_s'], *tabs, *qkv_consts],
                    in_specs=[_rt(tm, SMALL_W)] + [_rt(tm, LANE)] * 3 + qkv_cspecs,
                    outs=[_sds((l, MLA_PAD), BF16)] * 3, out_specs=[_rt(tm, MLA_PAD)] * 3)
    sv['q'], sv['k'], sv['v'] = q, k, v
    sv['o_a'], sv['lse'], sv['lse_t'] = _flash_fwd(name + "_attn", q, k, v)

    a_re, a_im, bb_re, bb_im = _ssm_params(name + "_ssm_par", p['lr'], p['li'], p['log_dt'], p['br'], p['bi'])
    sv['a_re'], sv['a_im'] = a_re.reshape(1, SSM_LANES), a_im.reshape(1, SSM_LANES)
    sv['b_mat'] = _b_mat(bb_re, bb_im)
    u_p = _to_perm(sv['p_u'], l)
    sv['u_p'] = u_p

    def bu_fn(i, u, bm):
        res = [_mm(u[:, j * LANE:(j + 1) * LANE], bm[j]) for j in range(SSM_JB)]
        return (jnp.concatenate([r[:, :512] for r in res], axis=-1), jnp.concatenate([r[:, 512:] for r in res], axis=-1))

    ts = min(256, l)
    bu_re, bu_im = _rows(bu_fn, name=name + "_ssm_bu", n=l // ts, ins=[u_p, sv['b_mat']],
                         in_specs=[_rt(ts, SSM_WIDTH), _full(sv['b_mat'].shape)],
                         outs=[_sds((l, SSM_LANES))] * 2, out_specs=[_rt(ts, SSM_LANES)] * 2)
    s_re, s_im = _scan(name + "_ssm_scan", bu_re, bu_im, sv['a_re'], sv['a_im'], reverse=False)
    sv['s_re'], sv['s_im'] = s_re, s_im

    def glu_fn(i, sr, si, u, cm, dsk, wg, bg):
        y = jnp.concatenate([_mm(jnp.concatenate([sr[:, j * 512:(j + 1) * 512], si[:, j * 512:(j + 1) * 512]], axis=-1),
                                 cm[j]) for j in range(SSM_JB)], axis=-1) + dsk * u
        zz = _gelu(y)
        return zz * jax.nn.sigmoid(_mm(zz, wg) + bg)

    glu_consts = [p['c_mat'], p['ssm_d'], p['w_glu'], p['b_glu']]
    zo_p = _rows(glu_fn, name=name + "_ssm_glu", n=l // ts, ins=[s_re, s_im, u_p, *glu_consts],
                 in_specs=[_rt(ts, SSM_LANES), _rt(ts, SSM_LANES), _rt(ts, SSM_WIDTH)] + [_full(a.shape) for a in glu_consts],
                 outs=[_sds((l, SSM_WIDTH), BF16)], out_specs=[_rt(ts, SSM_WIDTH)])[0]
    sv['zo'] = _from_perm(zo_p, l)

    m_len = mem.shape[0]

    def memkv_fn(i, mm_, mg, wmk, xkg):
        kv = _mm(_rms(mm_, mg, D_MODEL), wmk)
        return _head_rms(kv[:, :X_WIDTH], xkg, X_HEADS, X_HEAD_DIM), kv[:, X_WIDTH:]

    mem_consts = [p['mem_norm_g'], p['w_mem_kv'], p['xk_norm_g']]
    k_c, v_c = _rows(memkv_fn, name=name + "_memkv", n=1, ins=[mem, *mem_consts],
                     in_specs=[_full(mem.shape)] + [_full(a.shape) for a in mem_consts],
                     outs=[_sds((m_len, X_WIDTH))] * 2, out_specs=[_full((m_len, X_WIDTH))] * 2)
    sv['k_c'], sv['v_c'] = k_c, v_c

    def cross_fn(i, xq, kc, vc, xqg):
        outs = []
        for h in range(X_HEADS):
            sl = slice(h * LANE, (h + 1) * LANE)
            qh = _rms(xq[:, sl], xqg, X_HEAD_DIM)
            s = _mm_nt(qh, kc[:, sl]) * (X_HEAD_DIM ** -0.5)
            s = s - jnp.max(s, axis=-1, keepdims=True)
            e = jnp.exp(s)
            pr = e / jnp.sum(e, axis=-1, keepdims=True)
            outs.append(_mm(pr, vc[:, sl]))
        return jnp.concatenate(outs, axis=-1)

    sv['o_c'] = _rows(cross_fn, name=name + "_cross", n=nt, ins=[sv['p_xq'], k_c, v_c, p['xq_norm_g']],
                      in_specs=[_rt(tm, X_WIDTH), _full(k_c.shape), _full(v_c.shape), _full((1, LANE))],
                      outs=[_sds((l, X_WIDTH), BF16)], out_specs=[_rt(tm, X_WIDTH)])[0]

    def merge_fn(i, oa, zo, oc, pg, x0, woa, wos, woc, bg, wout):
        gates = jax.nn.sigmoid(pg + bg)
        merged = (gates[:, :D_MODEL] * _mm(oa, woa) + gates[:, D_MODEL:2 * D_MODEL] * _mm(zo, wos)
                  + gates[:, 2 * D_MODEL:] * _mm(oc, woc))
        return x0 + _mm(merged, wout), merged

    merge_consts = [p['w_oa'], p['w_o_ssm'], p['w_o_cross'], p['b_gate'], p['w_out']]
    tg = min(256, l)
    x1, merged = _rows(merge_fn, name=name + "_merge", n=l // tg, ins=[sv['o_a'], sv['zo'], sv['o_c'], sv['p_g'], x, *merge_consts],
                       in_specs=[_rt(tg, MLA_PAD), _rt(tg, SSM_WIDTH), _rt(tg, X_WIDTH), _rt(tg, 3 * D_MODEL), _rt(tg, D_MODEL)]
                       + [_full(a.shape) for a in merge_consts],
                       outs=[_sds((l, D_MODEL)), _sds((l, D_MODEL), BF16)], out_specs=[_rt(tg, D_MODEL)] * 2)
    sv['x1'], sv['merged'] = x1, merged

    up = _matmul(name + "_up", [(x1, p['w_up'])], l, 2 * D_FF, rms_gain=p['norm_ffn_g'])
    sv['up'] = up
    tc = min(128, l)

    def conv_fn(i, upt, halo, cw, cb):
        upc = _conv(i, upt, halo, cw) + cb
        return _silu(upc[:, :D_FF]) * upc[:, D_FF:]

    act = _rows(conv_fn, name=name + "_conv", n=l // tc, ins=[up, up, p['conv_w'], p['conv_b']],
                in_specs=[_rt(tc, 2 * D_FF), _halo_prev(tc, 2 * D_FF), _full((3, 2 * D_FF)), _full((1, 2 * D_FF))],
                outs=[_sds((l, D_FF), BF16)], out_specs=[_rt(tc, D_FF)])[0]
    sv['act'] = act
    x2 = _matmul(name + "_down", [(act, p['w_down'])], l, D_MODEL, resid=x1)
    return x2, sv


def _halo_prev(tm, w):
    return pl.BlockSpec((8, w), lambda i: (jnp.maximum(i * (tm // 8) - 1, 0), 0))


def _halo_next(tm, w, n_tiles):
    last = n_tiles * (tm // 8) - 1
    return pl.BlockSpec((8, w), lambda i: (jnp.minimum((i + 1) * (tm // 8), last), 0))


def _conv(i, tile, halo, cw):
    halo = jnp.where(i > 0, halo, 0.0)
    ext = jnp.concatenate([halo, tile], axis=0)
    n = ext.shape[0]
    x1 = pltpu.roll(ext, 1, 0)[8:]
    x2 = pltpu.roll(ext, 2, 0)[8:]
    del n
    return cw[0:1] * x2 + cw[1:2] * x1 + cw[2:3] * tile


def _layer_bwd(name, dx2, sv, tabs, mem, p):
    l = dx2.shape[0]
    tm = min(512, l)
    nt = l // tm
    g = {}
    x1 = sv['x1']
    dact = _matmul(name + "_b_down", [(dx2, p['w_down'])], l, D_FF, nt=True)
    g['w_down'] = _matmul_tn(name + "_gw_down", sv['act'], dx2)
    tc = min(128, l)
    ntc = l // tc

    def conv_b1(i, upt, halo, da, cw, cb):
        halo = jnp.where(i > 0, halo, 0.0)
        ext = jnp.concatenate([halo, upt], axis=0)
        xm1 = pltpu.roll(ext, 1, 0)[8:]
        xm2 = pltpu.roll(ext, 2, 0)[8:]
        upc = cw[0:1] * xm2 + cw[1:2] * xm1 + cw[2:3] * upt + cb
        gf, vf = upc[:, :D_FF], upc[:, D_FF:]
        _, vjp = jax.vjp(lambda a, b: _silu(a) * b, gf, vf)
        dg, dv = vjp(da)
        dupc = jnp.concatenate([dg, dv], axis=-1)
        dcw = _row_select([_colsum(dupc * xm2), _colsum(dupc * xm1), _colsum(dupc * upt)], 8)
        return dupc, dcw, _colsum(dupc)

    dupc, g_cw, g_cb = _rows(conv_b1, name=name + "_b_conv1", n=ntc, ins=[sv['up'], sv['up'], dact, p['conv_w'], p['conv_b']],
                             in_specs=[_rt(tc, 2 * D_FF), _halo_prev(tc, 2 * D_FF), _rt(tc, D_FF), _full((3, 2 * D_FF)),
                                       _full((1, 2 * D_FF))],
                             outs=[_sds((l, 2 * D_FF)), _sds((8, 2 * D_FF)), _sds((1, 2 * D_FF))],
                             out_specs=[_rt(tc, 2 * D_FF), _full((8, 2 * D_FF)), _full((1, 2 * D_FF))], n_acc=2)
    g['conv_w'] = g_cw[:3]
    g['conv_b'] = g_cb

    def conv_b2(i, dt, halo, cw):
        halo = jnp.where(i < ntc - 1, halo, 0.0)
        ext = jnp.concatenate([dt, halo], axis=0)
        n = ext.shape[0]
        dp1 = pltpu.roll(ext, n - 1, 0)[:tc]
        dp2 = pltpu.roll(ext, n - 2, 0)[:tc]
        return cw[2:3] * dt + cw[1:2] * dp1 + cw[0:1] * dp2

    dup = _rows(conv_b2, name=name + "_b_conv2", n=ntc, ins=[dupc, dupc, p['conv_w']],
                in_specs=[_rt(tc, 2 * D_FF), _halo_next(tc, 2 * D_FF, ntc), _full((3, 2 * D_FF))],
                outs=[_sds((l, 2 * D_FF))], out_specs=[_rt(tc, 2 * D_FF)])[0]
    dh2 = _matmul(name + "_b_up", [(dup, p['w_up'])], l, D_MODEL, nt=True, tm=256)
    g['w_up'] = _matmul_tn(name + "_gw_up", x1, dup, rms_gain=p['norm_ffn_g'])

    def norm_b(i, xv, dh, dres, gn):
        _, vjp = jax.vjp(lambda a, b: _rms(a, b, D_MODEL), xv, gn)
        dxv, dgn = vjp(dh)
        return dres + dxv, dgn

    dx1, g['norm_ffn_g'] = _rows(norm_b, name=name + "_b_norm2", n=nt, ins=[x1, dh2, dx2, p['norm_ffn_g']],
                                 in_specs=[_rt(tm, D_MODEL)] * 3 + [_full((1, D_MODEL))],
                                 outs=[_sds((l, D_MODEL)), _sds((1, D_MODEL))], out_specs=[_rt(tm, D_MODEL), _full((1, D_MODEL))],
                                 n_acc=1)

    tg = min(256, l)

    def merge_b(i, dx, oa, zo, oc, pg, woa, wos, woc, bg, wout):
        dm = _mm_nt(dx, wout)
        gates = jax.nn.sigmoid(pg + bg)
        ys = [_mm(oa, woa), _mm(zo, wos), _mm(oc, woc)]
        dys, dpg = [], []
        for b in range(3):
            gb = gates[:, b * D_MODEL:(b + 1) * D_MODEL]
            dys.append(dm * gb)
            dpg.append(dm * ys[b] * gb * (1.0 - gb))
        dpg = jnp.concatenate(dpg, axis=-1)
        return (_mm_nt(dys[0], woa), _mm_nt(dys[1], wos), _mm_nt(dys[2], woc), dpg, dys[0], dys[1], dys[2], _colsum(dpg))

    merge_consts = [p['w_oa'], p['w_o_ssm'], p['w_o_cross'], p['b_gate'], p['w_out']]
    (do_a, dzo, do_c, dp_g, dy_a, dy_b, dy_c, g['b_gate']) = _rows(
        merge_b, name=name + "_b_merge", n=l // tg, ins=[dx1, sv['o_a'], sv['zo'], sv['o_c'], sv['p_g'], *merge_consts],
        in_specs=[_rt(tg, D_MODEL), _rt(tg, MLA_PAD), _rt(tg, SSM_WIDTH), _rt(tg, X_WIDTH), _rt(tg, 3 * D_MODEL)]
        + [_full(a.shape) for a in merge_consts],
        outs=[_sds((l, MLA_PAD)), _sds((l, SSM_WIDTH)), _sds((l, X_WIDTH)), _sds((l, 3 * D_MODEL)),
              _sds((l, D_MODEL), BF16), _sds((l, D_MODEL), BF16), _sds((l, D_MODEL), BF16), _sds((1, 3 * D_MODEL))],
        out_specs=[_rt(tg, MLA_PAD), _rt(tg, SSM_WIDTH), _rt(tg, X_WIDTH), _rt(tg, 3 * D_MODEL),
                   _rt(tg, D_MODEL), _rt(tg, D_MODEL), _rt(tg, D_MODEL), _full((1, 3 * D_MODEL))], n_acc=1, vmem=56)
    g['w_out'] = _matmul_tn(name + "_gw_out", sv['merged'], dx1)
    g['w_oa'] = _matmul_tn(name + "_gw_oa", sv['o_a'], dy_a)
    g['w_o_ssm'] = _matmul_tn(name + "_gw_os", sv['zo'], dy_b)
    g['w_o_cross'] = _matmul_tn(name + "_gw_oc", sv['o_c'], dy_c)

    k_c, v_c = sv['k_c'], sv['v_c']
    m_len = k_c.shape[0]

    def cross_b(i, xq, do, kc, vc, xqg):
        dxq, dk, dv = [], [], []
        dg = jnp.zeros((1, LANE), F32)
        for h in range(X_HEADS):
            sl = slice(h * LANE, (h + 1) * LANE)
            qh, vjp = jax.vjp(lambda a, b: _rms(a, b, X_HEAD_DIM), xq[:, sl], xqg)
            sc = X_HEAD_DIM ** -0.5
            s = _mm_nt(qh, kc[:, sl]) * sc
            s = s - jnp.max(s, axis=-1, keepdims=True)
            e = jnp.exp(s)
            pr = e / jnp.sum(e, axis=-1, keepdims=True)
            doh = do[:, sl]
            dv.append(_mm_tn(pr, doh))
            dp = _mm_nt(doh, vc[:, sl])
            ds = pr * (dp - jnp.sum(dp * pr, axis=-1, keepdims=True)) * sc
            dk.append(_mm_tn(ds, qh))
            dxh, dgh = vjp(_mm(ds, kc[:, sl]))
            dxq.append(dxh)
            dg = dg + dgh
        return jnp.concatenate(dxq, axis=-1), jnp.concatenate(dk, axis=-1), jnp.concatenate(dv, axis=-1), dg

    dp_xq, dk_c, dv_c, g['xq_norm_g'] = _rows(
        cross_b, name=name + "_b_cross", n=nt, ins=[sv['p_xq'], do_c, k_c, v_c, p['xq_norm_g']],
        in_specs=[_rt(tm, X_WIDTH), _rt(tm, X_WIDTH), _full(k_c.shape), _full(v_c.shape), _full((1, LANE))],
        outs=[_sds((l, X_WIDTH)), _sds((m_len, X_WIDTH)), _sds((m_len, X_WIDTH)), _sds((1, LANE))],
        out_specs=[_rt(tm, X_WIDTH), _full((m_len, X_WIDTH)), _full((m_len, X_WIDTH)), _full((1, LANE))], n_acc=3)

    def memkv_b(i, mm_, dk, dv, mg, wmk, xkg):
        memn, vjp_n = jax.vjp(lambda a, b: _rms(a, b, D_MODEL), mm_, mg)
        kv = _mm(memn, wmk)
        _, vjp_k = jax.vjp(lambda a, b: _head_rms(a, b, X_HEADS, X_HEAD_DIM), kv[:, :X_WIDTH], xkg)
        dkr, dxkg = vjp_k(dk)
        dkv = jnp.concatenate([dkr, dv], axis=-1)
        _, dmg = vjp_n(_mm_nt(dkv, wmk))
        return _mm_tn(memn, dkv), dmg, dxkg

    mem_consts = [p['mem_norm_g'], p['w_mem_kv'], p['xk_norm_g']]
    g['w_mem_kv'], g['mem_norm_g'], g['xk_norm_g'] = _rows(
        memkv_b, name=name + "_b_memkv", n=1, ins=[mem, dk_c, dv_c, *mem_consts],
        in_specs=[_full(mem.shape), _full(dk_c.shape), _full(dv_c.shape)] + [_full(a.shape) for a in mem_consts],
        outs=[_sds((D_MODEL, 2 * X_WIDTH)), _sds((1, D_MODEL)), _sds((1, LANE))],
        out_specs=[_full((D_MODEL, 2 * X_WIDTH)), _full((1, D_MODEL)), _full((1, LANE))])

    u_p = sv['u_p']
    dzo_p = _to_perm(dzo, l)
    s_re, s_im = sv['s_re'], sv['s_im']

    def glu_b(i, sr, si, u, dz, cm, dsk, wg, bg):
        cats = [jnp.concatenate([sr[:, j * 512:(j + 1) * 512], si[:, j * 512:(j + 1) * 512]], axis=-1) for j in range(SSM_JB)]
        y = jnp.concatenate([_mm(cats[j], cm[j]) for j in range(SSM_JB)], axis=-1) + dsk * u
        zz, vjp_g = jax.vjp(_gelu, y)
        t = _mm(zz, wg) + bg
        sg = jax.nn.sigmoid(t)
        dt = dz * zz * sg * (1.0 - sg)
        dzz = dz * sg + _mm_nt(dt, wg)
        dy = vjp_g(dzz)[0]
        dss = [_mm_nt(dy[:, j * LANE:(j + 1) * LANE], cm[j]) for j in range(SSM_JB)]
        dsr = jnp.concatenate([d[:, :512] for d in dss], axis=-1)
        dsi = jnp.concatenate([d[:, 512:] for d in dss], axis=-1)
        dcm = jnp.stack([_mm_tn(cats[j], dy[:, j * LANE:(j + 1) * LANE]) for j in range(SSM_JB)], axis=0)
        return dsr, dsi, dy * dsk, dcm, _colsum(dy * u), _mm_tn(zz, dt), _colsum(dt)

    glu_consts = [p['c_mat'], p['ssm_d'], p['w_glu'], p['b_glu']]
    ts = min(256, l)
    nts = l // ts
    ds_re, ds_im, du_dir, g['c_mat'], g['ssm_d'], g['w_glu'], g['b_glu'] = _rows(
        glu_b, name=name + "_b_glu", n=nts, ins=[s_re, s_im, u_p, dzo_p, *glu_consts],
        in_specs=[_rt(ts, SSM_LANES), _rt(ts, SSM_LANES), _rt(ts, SSM_WIDTH), _rt(ts, SSM_WIDTH)] + [_full(a.shape) for a in glu_consts],
        outs=[_sds((l, SSM_LANES)), _sds((l, SSM_LANES)), _sds((l, SSM_WIDTH)), _sds((SSM_JB, 1024, LANE)), _sds((1, SSM_WIDTH)),
              _sds((SSM_WIDTH, SSM_WIDTH)), _sds((1, SSM_WIDTH))],
        out_specs=[_rt(ts, SSM_LANES), _rt(ts, SSM_LANES), _rt(ts, SSM_WIDTH), _full((SSM_JB, 1024, LANE)), _full((1, SSM_WIDTH)),
                   _full((SSM_WIDTH, SSM_WIDTH)), _full((1, SSM_WIDTH))], n_acc=4)
    gb_re, gb_im = _scan(name + "_b_scan", ds_re, ds_im, sv['a_re'], -sv['a_im'], reverse=True)
    ns = SCAN_SEGS
    last_blk = l // ns - 1

    def da_fn(i, gr, gi, sr, si, hr, hi, lr_, li_):
        rid = lax.broadcasted_iota(jnp.int32, lr_.shape, 0)
        fr = jnp.where(rid == 0, 0.0, pltpu.roll(lr_, 1, 0))
        fi = jnp.where(rid == 0, 0.0, pltpu.roll(li_, 1, 0))
        hr = jnp.where(i == 0, fr, hr)
        hi = jnp.where(i == 0, fi, hi)
        if ts > ns:
            pr = jnp.concatenate([hr, sr[:ts - ns]], axis=0)
            pi = jnp.concatenate([hi, si[:ts - ns]], axis=0)
        else:
            pr, pi = hr, hi
        return _colsum(gr * pr + gi * pi), _colsum(gi * pr - gr * pi)

    hprev = pl.BlockSpec((ns, SSM_LANES), lambda i: (jnp.maximum(i * (ts // ns) - 1, 0), 0))
    hlast = pl.BlockSpec((ns, SSM_LANES), lambda i: (last_blk, 0))
    da_re, da_im = _rows(da_fn, name=name + "_b_da", n=nts, ins=[gb_re, gb_im, s_re, s_im, s_re, s_im, s_re, s_im],
                         in_specs=[_rt(ts, SSM_LANES)] * 4 + [hprev, hprev, hlast, hlast],
                         outs=[_sds((1, SSM_LANES))] * 2, out_specs=[_full((1, SSM_LANES))] * 2, n_acc=2)

    def bu_b(i, dbr, dbi, u, dud, bm):
        dus, dbm = [], []
        for j in range(SSM_JB):
            cat = jnp.concatenate([dbr[:, j * 512:(j + 1) * 512], dbi[:, j * 512:(j + 1) * 512]], axis=-1)
            dus.append(_mm_nt(cat, bm[j]))
            dbm.append(_mm_tn(u[:, j * LANE:(j + 1) * LANE], cat))
        return dud + jnp.concatenate(dus, axis=-1), jnp.stack(dbm, axis=0)

    du_p, d_bmat = _rows(bu_b, name=name + "_b_bu", n=nts, ins=[gb_re, gb_im, u_p, du_dir, sv['b_mat']],
                         in_specs=[_rt(ts, SSM_LANES), _rt(ts, SSM_LANES), _rt(ts, SSM_WIDTH), _rt(ts, SSM_WIDTH),
                                   _full(sv['b_mat'].shape)],
                         outs=[_sds((l, SSM_WIDTH)), _sds((SSM_JB, LANE, 1024))],
                         out_specs=[_rt(ts, SSM_WIDTH), _full((SSM_JB, LANE, 1024))], n_acc=1)
    dp_u = _from_perm(du_p, l)
    dbb_re = _blockdiag_t(d_bmat[:, :, :512], SSM_GROUP_CH, SSM_STATE).reshape(SSM_GROUPS, SSM_GROUP_CH, SSM_STATE).transpose(1, 0, 2)
    dbb_im = _blockdiag_t(d_bmat[:, :, 512:], SSM_GROUP_CH, SSM_STATE).reshape(SSM_GROUPS, SSM_GROUP_CH, SSM_STATE).transpose(1, 0, 2)
    g['lr'], g['li'], g['log_dt'], g['br'], g['bi'] = _ssm_params_bwd(
        name + "_b_ssm_par", p['lr'], p['li'], p['log_dt'], p['br'], p['bi'],
        da_re.reshape(SSM_GROUPS, SSM_STATE), da_im.reshape(SSM_GROUPS, SSM_STATE), dbb_re, dbb_im)

    dq, dk, dv = _flash_bwd(name + "_b_attn", sv['q'], sv['k'], sv['v'], sv['o_a'], sv['lse'], sv['lse_t'], do_a)

    def qkv_b(i, ps, c, s1, s2, dq_, dk_, dv_, qag, wqb, kvag, wk, wv, qng, kng):
        c_q = ps[:, :Q_LORA]
        c_kv = ps[:, Q_LORA:Q_LORA + KV_LORA]
        kr = ps[:, Q_LORA + KV_LORA:]
        cqn, vjp_cq = jax.vjp(lambda a, b: _rms(a, b, Q_LORA), c_q, qag)
        ckvn, vjp_ckv = jax.vjp(lambda a, b: _rms(a, b, KV_LORA), c_kv, kvag)
        q_raw = _mm(cqn, wqb)
        k_raw = _mm(ckvn, wk) + jnp.concatenate([kr] * MLA_HEADS, axis=-1)
        _, vjp_qn = jax.vjp(lambda a, b: _head_rms(a, b, MLA_HEADS, D_QK), q_raw, qng)
        _, vjp_kn = jax.vjp(lambda a, b: _head_rms(a, b, MLA_HEADS, D_QK), k_raw, kng)
        dq_raw, dqng = vjp_qn(_heads(_rope_t, dq_, MLA_HEADS, c, s1, s2))
        dk_raw, dkng = vjp_kn(_heads(_rope_t, dk_, MLA_HEADS, c, s1, s2))
        dkr = dk_raw[:, :LANE]
        for h in range(1, MLA_HEADS):
            dkr = dkr + dk_raw[:, h * LANE:(h + 1) * LANE]
        dcq, dqag = vjp_cq(_mm_nt(dq_raw, wqb))
        dckv, dkvag = vjp_ckv(_mm_nt(dk_raw, wk) + _mm_nt(dv_, wv))
        dps = jnp.concatenate([dcq, dckv, dkr], axis=-1)
        return (dps, _mm_tn(cqn, dq_raw), _mm_tn(ckvn, dk_raw), _mm_tn(ckvn, dv_), dqag, dkvag, dqng, dkng)

    qkv_consts = [p['q_a_norm_g'], p['w_qb'], p['kv_a_norm_g'], p['w_k'], p['w_v'], p['q_norm_g'], p['k_norm_g']]
    (dp_s, g['w_qb'], g['w_k'], g['w_v'], g['q_a_norm_g'], g['kv_a_norm_g'], g['q_norm_g'], g['k_norm_g']) = _rows(
        qkv_b, name=name + "_b_qkv", n=nt, ins=[sv['p_s'], *tabs, dq, dk, dv, *qkv_consts],
        in_specs=[_rt(tm, SMALL_W)] + [_rt(tm, LANE)] * 3 + [_rt(tm, MLA_PAD)] * 3 + [_full(a.shape) for a in qkv_consts],
        outs=[_sds((l, SMALL_W)), _sds((Q_LORA, MLA_PAD)), _sds((KV_LORA, MLA_PAD)), _sds((KV_LORA, MLA_PAD)),
              _sds((1, Q_LORA)), _sds((1, KV_LORA)), _sds((1, LANE)), _sds((1, LANE))],
        out_specs=[_rt(tm, SMALL_W), _full((Q_LORA, MLA_PAD)), _full((KV_LORA, MLA_PAD)), _full((KV_LORA, MLA_PAD)),
                   _full((1, Q_LORA)), _full((1, KV_LORA)), _full((1, LANE)), _full((1, LANE))], n_acc=7)

    x0 = sv['x0']
    dh = _matmul(name + "_b_in", [(dp_g, p['w_g']), (dp_u, p['w_u']), (dp_xq, p['w_xq']), (dp_s, p['w_s'])], l, D_MODEL, nt=True,
                 tm=256)
    gm = p['norm_mix_g']
    g['w_g'] = _matmul_tn(name + "_gw_g", x0, dp_g, rms_gain=gm)
    g['w_u'] = _matmul_tn(name + "_gw_u", x0, dp_u, rms_gain=gm)
    g['w_xq'] = _matmul_tn(name + "_gw_xq", x0, dp_xq, rms_gain=gm)
    g['w_s'] = _matmul_tn(name + "_gw_s", x0, dp_s, rms_gain=gm)
    dx0, g['norm_mix_g'] = _rows(norm_b, name=name + "_b_norm1", n=nt, ins=[x0, dh, dx1, gm],
                                 in_specs=[_rt(tm, D_MODEL)] * 3 + [_full((1, D_MODEL))],
                                 outs=[_sds((l, D_MODEL)), _sds((1, D_MODEL))], out_specs=[_rt(tm, D_MODEL), _full((1, D_MODEL))],
                                 n_acc=1)
    return dx0, g


def _unprep_grads(g):
    o = {}
    ws = g['w_s']
    o['w_in'] = jnp.concatenate([ws[:, :Q_LORA + KV_LORA], ws[:, Q_LORA + KV_LORA + D_NOPE:Q_LORA + KV_LORA + D_QK],
                                 g['w_u'], g['w_xq'], g['w_g']], axis=1)
    o['w_q_b'] = g['w_qb'].reshape(Q_LORA, MLA_HEADS, HEAD_PAD)[:, :, :D_QK].reshape(Q_LORA, MLA_HEADS * D_QK)
    gk = g['w_k'].reshape(KV_LORA, MLA_HEADS, HEAD_PAD)[:, :, :D_NOPE]
    gv = g['w_v'].reshape(KV_LORA, MLA_HEADS, HEAD_PAD)[:, :, :D_V]
    o['w_kv_b'] = jnp.concatenate([gk, gv], axis=2).reshape(KV_LORA, MLA_HEADS * (D_NOPE + D_V))
    o['w_o_mla'] = g['w_oa'].reshape(MLA_HEADS, HEAD_PAD, D_MODEL)[:, :D_V].reshape(MLA_HEADS * D_V, D_MODEL)
    for n in ('w_glu', 'w_o_ssm', 'w_mem_kv', 'w_o_cross', 'w_out', 'w_up', 'w_down', 'conv_w'):
        o[n] = g[n]
    for n in ('norm_mix_g', 'q_a_norm_g', 'kv_a_norm_g', 'b_glu', 'mem_norm_g', 'xq_norm_g', 'xk_norm_g', 'b_gate',
              'norm_ffn_g', 'conv_b'):
        o[n] = g[n].reshape(-1)
    o['q_norm_g'] = g['q_norm_g'].reshape(-1)[:D_QK]
    o['k_norm_g'] = g['k_norm_g'].reshape(-1)[:D_QK]
    o['ssm_d'] = g['ssm_d'].reshape(SSM_GROUPS, SSM_GROUP_CH)
    o['ssm_lambda_re'] = g['lr']
    o['ssm_lambda_im'] = g['li']
    o['ssm_log_dt'] = g['log_dt'].reshape(SSM_GROUPS)
    o['ssm_b_re'] = g['br'].transpose(1, 2, 0)
    o['ssm_b_im'] = g['bi'].transpose(1, 2, 0)
    dc = g['c_mat']
    o['ssm_c_re'] = _blockdiag_t(dc[:, :512], SSM_STATE, SSM_GROUP_CH).transpose(0, 1, 3, 2).reshape(SSM_GROUPS, SSM_GROUP_CH, SSM_STATE)
    o['ssm_c_im'] = -_blockdiag_t(dc[:, 512:], SSM_STATE, SSM_GROUP_CH).transpose(0, 1, 3, 2).reshape(SSM_GROUPS, SSM_GROUP_CH, SSM_STATE)
    return o


def _local_step(x, mem, pos, target, w):
    l = x.shape[0]
    tm = min(512, l)
    tabs = _rope_tables(pos.astype(F32).reshape(l, 1))
    ps = [_prep_layer(w, i) for i in range(DEPTH)]
    saved = []
    h = x
    for i in range(DEPTH):
        h, sv = _layer_fwd("l%d" % i, h, tabs, mem, ps[i])
        saved.append(sv)

    def loss_fn(i, y, t):
        e = y - t
        per_tok = jnp.sum(e * e, axis=-1, keepdims=True) * (1.0 / D_MODEL)
        tot = 0.5 * jnp.sum(per_tok, axis=0, keepdims=True)
        return e * (1.0 / D_MODEL), jnp.broadcast_to(tot, (1, LANE))

    dy, loss = _rows(loss_fn, name="loss", n=l // tm, ins=[h, target], in_specs=[_rt(tm, D_MODEL)] * 2,
                     outs=[_sds((l, D_MODEL)), _sds((1, LANE))], out_specs=[_rt(tm, D_MODEL), _full((1, LANE))], n_acc=1)
    grads = []
    d = dy
    for i in reversed(range(DEPTH)):
        d, g = _layer_bwd("l%d" % i, d, saved[i], tabs, mem, ps[i])
        grads.append(_unprep_grads(g))
    return loss[0, 0], d, grads[::-1]


def _sum4(name, own, got):
    r, c = own.shape
    tr = _row_tile(r)

    def fn(i, o, v):
        return ((o.astype(F32) + v[0].astype(F32)) + v[1].astype(F32)) + v[2].astype(F32)

    return _rows(fn, name=name, n=r // tr, ins=[own, got],
                 in_specs=[_rt(tr, c), pl.BlockSpec((3, tr, c), lambda i: (0, i, 0))],
                 outs=[_sds((r, c))], out_specs=[_rt(tr, c)])[0]


def _row_tile(r):
    for t in (256, 128, 64, 32, 16, 8):
        if r % t == 0:
            return t
    return r


def _adamw(name, parts, w, m, v):
    r, cw = w.shape
    tr = _row_tile(r)
    np_ = len(parts)

    def fn(i, *vals):
        wv, mv, vv = vals[np_:]
        terms = []
        for pv in vals[:np_]:
            terms += [pv] if pv.ndim == 2 else [pv[k] for k in range(pv.shape[0])]
        g = terms[0]
        for t in terms[1:]:
            g = g + t
        mn = ADAM_B1 * mv + (1.0 - ADAM_B1) * g
        vn = ADAM_B2 * vv + (1.0 - ADAM_B2) * (g * g)
        m_hat = mn / (1.0 - ADAM_B1 ** ADAM_STEP)
        v_hat = vn / (1.0 - ADAM_B2 ** ADAM_STEP)
        delta = -ADAM_LR * (m_hat / (jnp.sqrt(v_hat) + ADAM_EPS) + ADAM_WD * wv)
        return g, delta, mn, vn

    pspecs = [_rt(tr, cw) if p.ndim == 2 else pl.BlockSpec((p.shape[0], tr, cw), lambda i: (0, i, 0)) for p in parts]
    return _rows(fn, name=name, n=r // tr, ins=[*parts, w, m, v], in_specs=pspecs + [_rt(tr, cw)] * 3,
                 outs=[_sds((r, cw))] * 4, out_specs=[_rt(tr, cw)] * 4)


def _shard_of(a, axis, k):
    n = a.shape[axis] // 4
    return lax.slice_in_dim(a, k * n, (k + 1) * n, axis=axis)


def _step(a):
    x = a['x'][0]
    mem = a['mem'][0]
    pos = a['positions'][0]
    target = a['loss_target'][0]

    me = 2 * lax.axis_index("x") + lax.axis_index("y")

    mine = [a[n] if n == 'conv_w' else a[n].astype(BF16) for n in SHARDED]
    got = _gather_d2d(_gather_ici(mine))
    w = {}
    for n, own, y in zip(SHARDED, mine, got):
        ax = SHARD_AXIS[n] - 1
        w[n] = [jnp.concatenate([jnp.where(me == k, own[i], y[k, i]) for k in range(4)], axis=ax) for i in range(DEPTH)]
    for n in SMALL:
        w[n] = a[n]

    loss, grad_x, grads = _local_step(x, mem, pos, target, w)

    gsh = []
    for n in SHARDED:
        ax = SHARD_AXIS[n] - 1
        gsh.append(jnp.stack([jnp.stack([_shard_of(grads[i][n], ax, k) for i in range(DEPTH)], axis=0)
                              for k in range(4)], axis=0).astype(BF16))
    got = _reduce_ici(gsh)
    parts = []
    for n, g4, g3 in zip(SHARDED, gsh, got):
        own = lax.dynamic_index_in_dim(g4, me, axis=0, keepdims=False)
        cols = own.shape[-1]
        parts.append(_sum4("sum4_" + n, own.reshape(-1, cols), g3.reshape(3, -1, cols)).reshape(own.shape))
    others = _swap_d2d(parts)
    res_sh = []
    for n, part, other in zip(SHARDED, parts, others):
        cols = part.shape[-1]
        res = _adamw("adamw_" + n, [part.reshape(-1, cols), other.reshape(-1, cols)],
                     *[a[pre + n].reshape(-1, cols) for pre in ('', 'm_', 'v_')])
        res_sh.append([r.reshape(a[n].shape) for r in res])
    res_sh = [[res_sh[j][kind] for j in range(len(SHARDED))] for kind in range(4)]

    sm_shapes = [a[n].shape for n in SMALL] + [(1,)]
    gsm = _pack([jnp.stack([grads[i][n] for i in range(DEPTH)], axis=0) for n in SMALL] + [loss.reshape(1)], 8, F32)
    alls = _all_exchange("comm_reduce_small", gsm)
    zero1 = jnp.zeros((1,), F32)
    res_sm = _adamw("adamw_small", [alls], *[_pack([a[pre + n] for n in SMALL] + [zero1], 8, F32) for pre in ('', 'm_', 'v_')])
    res_sm = [_unpack(r, sm_shapes) for r in res_sm]
    loss = res_sm[0][-1][0]

    outs = [loss, grad_x[None]]
    for kind in range(4):
        byname = dict(zip(SHARDED, res_sh[kind]))
        byname.update(zip(SMALL, res_sm[kind]))
        outs += [byname[n] for n in WEIGHTS]
    return tuple(outs)


def kernel(x, mem, positions, norm_mix_g, w_in, q_a_norm_g, w_q_b, kv_a_norm_g, w_kv_b, q_norm_g, k_norm_g, w_o_mla, ssm_lambda_re, ssm_lambda_im, ssm_log_dt, ssm_b_re, ssm_b_im, ssm_c_re, ssm_c_im, ssm_d, w_glu, b_glu, w_o_ssm, mem_norm_g, w_mem_kv, xq_norm_g, xk_norm_g, w_o_cross, b_gate, w_out, norm_ffn_g, w_up, conv_w, conv_b, w_down, loss_target, m_norm_mix_g, m_w_in, m_q_a_norm_g, m_w_q_b, m_kv_a_norm_g, m_w_kv_b, m_q_norm_g, m_k_norm_g, m_w_o_mla, m_ssm_lambda_re, m_ssm_lambda_im, m_ssm_log_dt, m_ssm_b_re, m_ssm_b_im, m_ssm_c_re, m_ssm_c_im, m_ssm_d, m_w_glu, m_b_glu, m_w_o_ssm, m_mem_norm_g, m_w_mem_kv, m_xq_norm_g, m_xk_norm_g, m_w_o_cross, m_b_gate, m_w_out, m_norm_ffn_g, m_w_up, m_conv_w, m_conv_b, m_w_down, v_norm_mix_g, v_w_in, v_q_a_norm_g, v_w_q_b, v_kv_a_norm_g, v_w_kv_b, v_q_norm_g, v_k_norm_g, v_w_o_mla, v_ssm_lambda_re, v_ssm_lambda_im, v_ssm_log_dt, v_ssm_b_re, v_ssm_b_im, v_ssm_c_re, v_ssm_c_im, v_ssm_d, v_w_glu, v_b_glu, v_w_o_ssm, v_mem_norm_g, v_w_mem_kv, v_xq_norm_g, v_xk_norm_g, v_w_o_cross, v_b_gate, v_w_out, v_norm_ffn_g, v_w_up, v_conv_w, v_conv_b, v_w_down):
    return _step(dict(locals()))
```

```python
import functools
import math

import numpy as np
import jax
import jax.numpy as jnp
from jax import lax
from jax.experimental import pallas as pl
from jax.experimental.pallas import tpu as pltpu

F32 = jnp.float32
BF16 = jnp.bfloat16
MESH = pl.DeviceIdType.MESH

DEPTH = 2
D_MODEL = 1024
EPS = 1e-6
MLA_HEADS = 8
Q_LORA = 384
KV_LORA = 256
D_NOPE = 64
D_ROPE = 32
D_QK = D_NOPE + D_ROPE
D_V = 64
HEAD_PAD = 128
MLA_PAD = MLA_HEADS * HEAD_PAD
ROPE_THETA = 10000.0
SSM_GROUPS = 32
SSM_GROUP_CH = 16
SSM_WIDTH = 512
SSM_STATE = 64
SSM_LANES = SSM_GROUPS * SSM_STATE
SSM_JB = 4
X_HEADS = 4
X_HEAD_DIM = 128
X_WIDTH = 512
D_FF = 2816
SMALL_W = Q_LORA + KV_LORA + HEAD_PAD
SCAN_SEGS = 32
LANE = 128
NEG = -1e30

ADAM_LR = 0.001
ADAM_B1 = 0.9
ADAM_B2 = 0.999
ADAM_EPS = 1e-08
ADAM_WD = 0.01
ADAM_STEP = 10

WEIGHTS = ['norm_mix_g', 'w_in', 'q_a_norm_g', 'w_q_b', 'kv_a_norm_g', 'w_kv_b', 'q_norm_g', 'k_norm_g', 'w_o_mla',
           'ssm_lambda_re', 'ssm_lambda_im', 'ssm_log_dt', 'ssm_b_re', 'ssm_b_im', 'ssm_c_re', 'ssm_c_im', 'ssm_d',
           'w_glu', 'b_glu', 'w_o_ssm', 'mem_norm_g', 'w_mem_kv', 'xq_norm_g', 'xk_norm_g', 'w_o_cross', 'b_gate',
           'w_out', 'norm_ffn_g', 'w_up', 'conv_w', 'conv_b', 'w_down']
SHARD_AXIS = {'w_in': 2, 'w_q_b': 2, 'w_kv_b': 2, 'w_o_mla': 2, 'w_glu': 1, 'w_o_ssm': 2, 'w_mem_kv': 1,
              'w_o_cross': 2, 'w_out': 1, 'w_up': 2, 'conv_w': 2, 'w_down': 1}
SHARDED = [n for n in WEIGHTS if n in SHARD_AXIS]
GATHER_BF16 = [n for n in SHARDED if n != 'conv_w']
SMALL = [n for n in WEIGHTS if n not in SHARD_AXIS]


def _bf(v):
    return v.astype(BF16)


def _mm(a, b):
    return jnp.dot(_bf(a), _bf(b), preferred_element_type=F32)


def _mm_nt(a, b):
    return lax.dot_general(_bf(a), _bf(b), (((1,), (1,)), ((), ())), preferred_element_type=F32)


def _mm_tn(a, b):
    return lax.dot_general(_bf(a), _bf(b), (((0,), (0,)), ((), ())), preferred_element_type=F32)


def _rms(v, g, n):
    ms = jnp.sum(v * v, axis=-1, keepdims=True) * (1.0 / n)
    return (v * lax.rsqrt(ms + EPS)) * g


def _head_rms(v, g, heads, n):
    return jnp.concatenate([_rms(v[:, h * LANE:(h + 1) * LANE], g, n) for h in range(heads)], axis=-1)


def _rope(v, c, s1, s2):
    return v * c + pltpu.roll(v, LANE - 16, 1) * s1 + pltpu.roll(v, 16, 1) * s2


def _rope_t(g, c, s1, s2):
    return g * c + pltpu.roll(g * s1, 16, 1) + pltpu.roll(g * s2, LANE - 16, 1)


def _heads(fn, v, heads, *tabs):
    return jnp.concatenate([fn(v[:, h * LANE:(h + 1) * LANE], *tabs) for h in range(heads)], axis=-1)


def _gelu(y):
    return y * (0.5 * (1.0 + jnp.tanh(math.sqrt(2.0 / math.pi) * (y + 0.044715 * (y * y * y)))))


def _silu(g):
    return g * jax.nn.sigmoid(g)


def _colsum(v):
    return jnp.sum(v, axis=0, keepdims=True)


def _row_select(rows, n):
    rid = lax.broadcasted_iota(jnp.int32, (n, rows[0].shape[-1]), 0)
    out = jnp.zeros((n, rows[0].shape[-1]), F32)
    for k, r in enumerate(rows):
        out = jnp.where(rid == k, jnp.broadcast_to(r, out.shape), out)
    return out


def _params(sem, vmem_mb):
    return pltpu.CompilerParams(dimension_semantics=sem, vmem_limit_bytes=vmem_mb * 1024 * 1024)


def _rt(tm, w, cb=0):
    return pl.BlockSpec((tm, w), lambda i: (i, cb))


def _full(shape):
    nd = len(shape)
    return pl.BlockSpec(tuple(shape), lambda i: (0,) * nd)


def _rows(fn, *, name, n, ins, in_specs, outs, out_specs, n_acc=0, vmem=48):
    n_in = len(ins)
    n_out = len(outs)

    def body(*refs):
        i = pl.program_id(0)
        res = fn(i, *[r[...] for r in refs[:n_in]])
        if not isinstance(res, (tuple, list)):
            res = (res,)
        assert len(res) == n_out, (name, len(res), n_out)
        for k, (r, v) in enumerate(zip(refs[n_in:], res)):
            if k < n_out - n_acc:
                r[...] = v.astype(r.dtype)
            else:
                @pl.when(i == 0)
                def _():
                    r[...] = v

                @pl.when(i > 0)
                def _():
                    r[...] += v

    return pl.pallas_call(
        body, name=name, grid=(n,), in_specs=list(in_specs), out_specs=tuple(out_specs), out_shape=tuple(outs),
        compiler_params=_params(("arbitrary",), vmem))(*ins)


def _sds(shape, dtype=F32):
    return jax.ShapeDtypeStruct(tuple(shape), dtype)


def _tile_n(n, cap=1024):
    best = None
    for t in range(LANE, min(n, cap) + 1, LANE):
        if n % t == 0:
            best = t
    if best is None or n <= 1408:
        return n
    return best


def _matmul(name, pairs, m, n, *, nt=False, rms_gain=None, resid=None, out_dtype=F32, tm=512, vmem=56):
    tm = min(tm, m)
    tn = _tile_n(n)
    ks = [a.shape[1] for a, _ in pairs]
    np_ = len(pairs)

    def body(*refs):
        a_refs = refs[:np_]
        b_refs = refs[np_:2 * np_]
        k = 2 * np_
        g_ref = None
        r_ref = None
        if rms_gain is not None:
            g_ref = refs[k]
            k += 1
        if resid is not None:
            r_ref = refs[k]
            k += 1
        o_ref = refs[k]
        scr = refs[k + 1:]
        j = pl.program_id(1)

        @pl.when(j == 0)
        def _():
            for p in range(np_):
                a = a_refs[p][...]
                if p == 0 and g_ref is not None:
                    a = _rms(a.astype(F32), g_ref[...], ks[0])
                scr[p][...] = a.astype(BF16)

        acc = None
        for p in range(np_):
            b = b_refs[p][...].astype(BF16)
            if nt:
                t = lax.dot_general(scr[p][...], b, (((1,), (1,)), ((), ())), preferred_element_type=F32)
            else:
                t = jnp.dot(scr[p][...], b, preferred_element_type=F32)
            acc = t if acc is None else acc + t
        if r_ref is not None:
            acc = acc + r_ref[...]
        o_ref[...] = acc.astype(o_ref.dtype)

    in_specs = [pl.BlockSpec((tm, kk), lambda i, j: (i, 0)) for kk in ks]
    if nt:
        in_specs += [pl.BlockSpec((tn, kk), lambda i, j: (j, 0)) for kk in ks]
    else:
        in_specs += [pl.BlockSpec((kk, tn), lambda i, j: (0, j)) for kk in ks]
    ins = [a for a, _ in pairs] + [b for _, b in pairs]
    if rms_gain is not None:
        in_specs.append(pl.BlockSpec((1, ks[0]), lambda i, j: (0, 0)))
        ins.append(rms_gain)
    if resid is not None:
        in_specs.append(pl.BlockSpec((tm, tn), lambda i, j: (i, j)))
        ins.append(resid)
    return pl.pallas_call(
        body, name=name, grid=(m // tm, n // tn), in_specs=in_specs,
        out_specs=pl.BlockSpec((tm, tn), lambda i, j: (i, j)), out_shape=_sds((m, n), out_dtype),
        scratch_shapes=[pltpu.VMEM((tm, kk), BF16) for kk in ks],
        compiler_params=_params(("arbitrary", "arbitrary"), vmem))(*ins)


def _matmul_tn(name, a, b, *, rms_gain=None, tl=512, vmem=56):
    l, ka = a.shape
    n = b.shape[1]
    tl = min(tl, l)
    tn = _tile_n(n, 1536)

    def body(*refs):
        if rms_gain is not None:
            a_ref, b_ref, g_ref, o_ref = refs
        else:
            a_ref, b_ref, o_ref = refs
        t = pl.program_id(1)
        av = a_ref[...]
        if rms_gain is not None:
            av = _rms(av.astype(F32), g_ref[...], ka)
        v = _mm_tn(av, b_ref[...])

        @pl.when(t == 0)
        def _():
            o_ref[...] = v

        @pl.when(t > 0)
        def _():
            o_ref[...] += v

    in_specs = [pl.BlockSpec((tl, ka), lambda j, t: (t, 0)), pl.BlockSpec((tl, tn), lambda j, t: (t, j))]
    ins = [a, b]
    if rms_gain is not None:
        in_specs.append(pl.BlockSpec((1, ka), lambda j, t: (0, 0)))
        ins.append(rms_gain)
    return pl.pallas_call(
        body, name=name, grid=(n // tn, l // tl), in_specs=in_specs,
        out_specs=pl.BlockSpec((ka, tn), lambda j, t: (0, j)), out_shape=_sds((ka, n)),
        compiler_params=_params(("arbitrary", "arbitrary"), vmem))(*ins)


ATT_HEADS_PER_STEP = 2
ATT_W = ATT_HEADS_PER_STEP * LANE
ATT_GROUPS = MLA_HEADS // ATT_HEADS_PER_STEP
LOG2E = math.log2(math.e)
ATT_SCALE = D_QK ** -0.5
ATT_QSCALE = ATT_SCALE * LOG2E


def _tri_tables(nq, by_k):
    qs, ks = [], []
    if by_k:
        for ki in range(nq):
            for qi in range(ki, nq):
                qs.append(qi)
                ks.append(ki)
    else:
        for qi in range(nq):
            for ki in range(qi + 1):
                qs.append(qi)
                ks.append(ki)
    return jnp.asarray(np.array(qs, np.int32)), jnp.asarray(np.array(ks, np.int32))


def _causal_keep(shape, transposed):
    r = lax.broadcasted_iota(jnp.int32, shape, 0)
    c = lax.broadcasted_iota(jnp.int32, shape, 1)
    return (r <= c) if transposed else (c <= r)


def _nt16(a, b):
    return lax.dot_general(a, b, (((1,), (1,)), ((), ())), preferred_element_type=F32)


def _row_form(col):
    return jnp.transpose(jnp.broadcast_to(col, (col.shape[0], LANE)))[:8]


def _att_call(body, name, l, tq, tabs, ins, in_specs, outs, out_specs, scratch=()):
    grid_spec = pltpu.PrefetchScalarGridSpec(
        num_scalar_prefetch=2, grid=(ATT_GROUPS, tabs[0].shape[0]), in_specs=in_specs, out_specs=out_specs,
        scratch_shapes=list(scratch))
    return pl.pallas_call(body, name=name, grid_spec=grid_spec, out_shape=outs,
                          compiler_params=_params(("arbitrary", "arbitrary"), 48))(*tabs, *ins)


def _flash_fwd(name, q, k, v):
    l = q.shape[0]
    tq = min(512, l)
    nq = l // tq
    tabs = _tri_tables(nq, by_k=False)

    def body(qt, kt, q_ref, k_ref, v_ref, o_ref, lset_ref, m_s, acc_s):
        t = pl.program_id(1)
        qi = qt[t]
        ki = kt[t]

        @pl.when(ki == 0)
        def _():
            m_s[...] = jnp.full(m_s.shape, NEG, F32)
            acc_s[...] = jnp.zeros(acc_s.shape, F32)

        def step(masked):
            sls =[slice(h * LANE, (h + 1) * LANE) for h in range(ATT_HEADS_PER_STEP)]
            ss = [_nt16(q_ref[:, sl], k_ref[:, sl]) for sl in sls]
            for h, sl in enumerate(sls):
                s = ss[h]
                if masked:
                    s = jnp.where(_causal_keep(s.shape, False), s, NEG)
                m_old = m_s[h]
                m_new = jnp.maximum(m_old, jnp.max(s, axis=-1, keepdims=True))
                alpha = jnp.exp2(m_old - m_new)
                p = jnp.exp2(s - m_new)
                acc_s[:, sl] = alpha * acc_s[:, sl] + jnp.dot(p.astype(BF16), v_ref[:, sl], preferred_element_type=F32)
                m_s[h] = m_new

        @pl.when(ki < qi)
        def _():
            step(False)

        @pl.when(ki == qi)
        def _():
            step(True)
            lane = lax.broadcasted_iota(jnp.int32, (tq, LANE), 1)
            for h in range(ATT_HEADS_PER_STEP):
                sl = slice(h * LANE, (h + 1) * LANE)
                acc = acc_s[:, sl]
                lsum = acc[:, D_V:D_V + 1]
                o_ref[:, sl] = jnp.where(lane < D_V, acc / lsum, 0.0)
                lse = m_s[h] + jnp.log2(lsum)
                lset_ref[h * 8:(h + 1) * 8, :] = _row_form(lse)

    qspec = pl.BlockSpec((tq, ATT_W), lambda g, t, qt, kt: (qt[t], g))
    kspec = pl.BlockSpec((tq, ATT_W), lambda g, t, qt, kt: (kt[t], g))
    rspec = pl.BlockSpec((8 * ATT_HEADS_PER_STEP, tq), lambda g, t, qt, kt: (g, qt[t]))
    return _att_call(
        body, name, l, tq, tabs, [q, k, v], [qspec, kspec, kspec],
        (_sds((l, MLA_PAD)), _sds((8 * MLA_HEADS, l))), (qspec, rspec),
        scratch=[pltpu.VMEM((ATT_HEADS_PER_STEP, tq, 1), F32), pltpu.VMEM((tq, ATT_W), F32)])


def _flash_bwd(name, q, k, v, o, lse_t, do):
    l = q.shape[0]
    tq = min(512, l)
    nq = l // tq

    def delta_fn(i, dov, ov):
        rows = []
        for h in range(MLA_HEADS):
            sl = slice(h * LANE, (h + 1) * LANE)
            rows.append(_row_form(jnp.sum(dov[:, sl] * ov[:, sl], axis=-1, keepdims=True)))
        return jnp.concatenate(rows, axis=0), dov

    delta_t, do16 = _rows(
        delta_fn, name=name + "_delta", n=nq, ins=[do, o], in_specs=[_rt(tq, MLA_PAD)] * 2,
        outs=[_sds((8 * MLA_HEADS, l)), _sds((l, MLA_PAD), BF16)],
        out_specs=[pl.BlockSpec((8 * MLA_HEADS, tq), lambda i: (0, i)), _rt(tq, MLA_PAD)])

    def body(qt, kt, q_ref, k_ref, v_ref, do_ref, lset_ref, dlt_ref, dk_ref, dv_ref, dqt_ref, kt_s):
        t = pl.program_id(1)
        qi = qt[t]
        ki = kt[t]
        sls = [slice(h * LANE, (h + 1) * LANE) for h in range(ATT_HEADS_PER_STEP)]

        @pl.when(ki == 0)
        def _():
            dqt_ref[qi] = jnp.zeros((ATT_W, tq), F32)

        def step(masked):
            sts = [_nt16(k_ref[:, sl], q_ref[:, sl]) for sl in sls]
            dpts = [_nt16(v_ref[:, sl], do_ref[:, sl]) for sl in sls]
            for h, sl in enumerate(sls):
                st = sts[h]
                if masked:
                    st = jnp.where(_causal_keep(st.shape, True), st, NEG)
                pt = jnp.exp2(st - lset_ref[h * 8:(h + 1) * 8, :][:1])
                dst = (pt * (dpts[h] - dlt_ref[h * 8:(h + 1) * 8, :][:1])).astype(BF16)
                dv_ref[:, sl] += jnp.dot(pt.astype(BF16), do_ref[:, sl], preferred_element_type=F32)
                dk_ref[:, sl] += jnp.dot(dst, q_ref[:, sl], preferred_element_type=F32)
                dqt_ref[qi, sl, :] += jnp.dot(kt_s[sl, :], dst, preferred_element_type=F32)

        @pl.when(qi == ki)
        def _():
            dk_ref[...] = jnp.zeros(dk_ref.shape, F32)
            dv_ref[...] = jnp.zeros(dv_ref.shape, F32)
            for sl in sls:
                kt_s[sl, :] = jnp.transpose(k_ref[:, sl].astype(F32)).astype(BF16)
            step(True)
            dqt_ref[qi] = dqt_ref[qi] * ATT_SCALE

        @pl.when(qi > ki)
        def _():
            step(False)

        @pl.when(qi == nq - 1)
        def _():
            dk_ref[...] = dk_ref[...] * (1.0 / LOG2E)

    tabs_k = _tri_tables(nq, by_k=True)
    qspec = pl.BlockSpec((tq, ATT_W), lambda g, t, qt, kt: (qt[t], g))
    kspec = pl.BlockSpec((tq, ATT_W), lambda g, t, qt, kt: (kt[t], g))
    rspec = pl.BlockSpec((8 * ATT_HEADS_PER_STEP, tq), lambda g, t, qt, kt: (g, qt[t]))
    dqspec = pl.BlockSpec((nq, ATT_W, tq), lambda g, t, qt, kt: (0, g, 0))
    dk, dv, dq_t = _att_call(body, name + "_dqkv", l, tq, tabs_k, [q, k, v, do16, lse_t, delta_t],
                             [qspec, kspec, kspec, qspec, rspec, rspec],
                             (_sds((l, MLA_PAD)), _sds((l, MLA_PAD)), _sds((nq, MLA_PAD, tq))), (kspec, kspec, dqspec),
                             scratch=[pltpu.VMEM((ATT_W, tq), BF16)])
    return dq_t, dk, dv


def _cmul(ar, ai, br, bi):
    return ar * br - ai * bi, ar * bi + ai * br


def _scan(name, x_re, x_im, a_re, a_im, reverse):
    l, lanes = x_re.shape
    ns = SCAN_SEGS
    tl = l // ns
    steps = int(math.log2(tl))
    assert 2 ** steps == tl and tl * ns == l

    def body(xr_ref, xi_ref, ar_ref, ai_ref, sr_ref, si_ref):
        a_r1 = ar_ref[...]
        a_i1 = ai_ref[...]
        a_r = jnp.broadcast_to(a_r1, (ns, LANE))
        a_i = jnp.broadcast_to(a_i1, (ns, LANE))

        def rows(t):
            t = (tl - 1 - t) if reverse else t
            return pl.ds(pl.multiple_of(t * ns, ns), ns)

        def local(t, carry):
            cr, ci = carry
            r = rows(t)
            pr, pi = _cmul(a_r, a_i, cr, ci)
            return pr + xr_ref[r, :], pi + xi_ref[r, :]

        zero = jnp.zeros((ns, LANE), F32)
        e_r, e_i = lax.fori_loop(0, tl, local, (zero, zero))
        p_r, p_i = a_r1, a_i1
        for _ in range(steps):
            p_r, p_i = _cmul(p_r, p_i, p_r, p_i)
        rid = lax.broadcasted_iota(jnp.int32, (ns, LANE), 0)
        c_r = jnp.zeros((1, LANE), F32)
        c_i = jnp.zeros((1, LANE), F32)
        in_r, in_i = zero, zero
        order = range(ns - 2, -1, -1) if reverse else range(1, ns)
        for kk in order:
            src = kk + 1 if reverse else kk - 1
            ek_r = jnp.sum(jnp.where(rid == src, e_r, 0.0), axis=0, keepdims=True)
            ek_i = jnp.sum(jnp.where(rid == src, e_i, 0.0), axis=0, keepdims=True)
            q_r, q_i = _cmul(p_r, p_i, c_r, c_i)
            c_r, c_i = q_r + ek_r, q_i + ek_i
            in_r = jnp.where(rid == kk, jnp.broadcast_to(c_r, (ns, LANE)), in_r)
            in_i = jnp.where(rid == kk, jnp.broadcast_to(c_i, (ns, LANE)), in_i)

        def final(t, carry):
            cr, ci = carry
            r = rows(t)
            pr, pi = _cmul(a_r, a_i, cr, ci)
            nr, ni = pr + xr_ref[r, :], pi + xi_ref[r, :]
            sr_ref[r, :] = nr
            si_ref[r, :] = ni
            return nr, ni

        lax.fori_loop(0, tl, final, (in_r, in_i))

    xs = pl.BlockSpec((l, LANE), lambda j: (0, j))
    as_ = pl.BlockSpec((1, LANE), lambda j: (0, j))
    return pl.pallas_call(
        body, name=name, grid=(lanes // LANE,), in_specs=[xs, xs, as_, as_], out_specs=(xs, xs),
        out_shape=(_sds((l, lanes)), _sds((l, lanes))),
        compiler_params=_params(("arbitrary",), 48))(x_re, x_im, a_re, a_im)


ANY = pl.BlockSpec(memory_space=pl.ANY)


def _place():
    mx, my, mc = lax.axis_index("x"), lax.axis_index("y"), lax.axis_index("c")
    return mx, my, mc, [(1 - mx, my), (mx, 1 - my), (1 - mx, 1 - my)]


def _run_copies(copies):
    for cp in copies:
        cp.start()
    for cp in copies:
        cp.wait_recv()
    for cp in copies:
        cp.wait_send()


def _remote(src, dst, sems, k, dev):
    return pltpu.make_async_remote_copy(src_ref=src, dst_ref=dst, send_sem=sems[0].at[k], recv_sem=sems[1].at[k],
                                        device_id=dev, device_id_type=MESH)


def _copy_call(body, name, ins, outs, n_copies, aliases=None):
    return pl.pallas_call(
        body, name=name, in_specs=[ANY] * len(ins), out_specs=[ANY] * len(outs), out_shape=list(outs),
        input_output_aliases=aliases or {},
        scratch_shapes=[pltpu.SemaphoreType.DMA((n_copies,)), pltpu.SemaphoreType.DMA((n_copies,))])(*ins)


def _gather_ici(xs):
    n = len(xs)

    def body(*refs):
        x_refs, y_refs, sems = refs[:n], refs[n:2 * n], refs[2 * n:]
        mx, my, mc, peers = _place()
        me = 2 * mx + my
        _run_copies([_remote(x_refs[i].at[mc], y_refs[i].at[me, mc], sems, 3 * i + j, (px, py, mc))
                     for i in range(n) for j, (px, py) in enumerate(peers)])

    return _copy_call(body, "comm_gather_ici", xs, [_sds((4,) + x.shape, x.dtype) for x in xs], 3 * n)


def _gather_d2d(ys):
    n = len(ys)

    def body(*refs):
        y_in, y_out, sems = refs[:n], refs[n:2 * n], refs[2 * n:]
        mx, my, mc, peers = _place()
        _run_copies([_remote(y_in[i].at[2 * px + py, mc], y_out[i].at[2 * px + py, mc], sems, 3 * i + j, (mx, my, 1 - mc))
                     for i in range(n) for j, (px, py) in enumerate(peers)])

    return _copy_call(body, "comm_gather_d2d", ys, [_sds(y.shape, y.dtype) for y in ys], 3 * n,
                      aliases={i: i for i in range(n)})


def _reduce_ici(gs):
    n = len(gs)

    def body(*refs):
        g_refs, y_refs, sems = refs[:n], refs[n:2 * n], refs[2 * n:]
        mx, my, mc, peers = _place()
        _run_copies([_remote(g_refs[i].at[2 * px + py], y_refs[i].at[j], sems, 3 * i + j, (px, py, mc))
                     for i in range(n) for j, (px, py) in enumerate(peers)])

    return _copy_call(body, "comm_reduce_ici", gs, [_sds((3,) + g.shape[1:], g.dtype) for g in gs], 3 * n)


def _swap_d2d(ps):
    n = len(ps)
    depth = ps[0].shape[0]

    def body(*refs):
        p_refs, o_refs, sems = refs[:n], refs[n:2 * n], refs[2 * n:]
        mx, my, mc, _ = _place()
        _run_copies([_remote(p_refs[i].at[l], o_refs[i].at[l], sems, depth * i + l, (mx, my, 1 - mc))
                     for i in range(n) for l in range(depth)])

    return _copy_call(body, "comm_reduce_d2d", ps, [_sds(p.shape, p.dtype) for p in ps], depth * n)


def _all_exchange(name, src):
    def body(x_ref, y_ref, send_sems, recv_sems, local_sem):
        mx, my, mc = lax.axis_index("x"), lax.axis_index("y"), lax.axis_index("c")
        me = 4 * mx + 2 * my + mc
        own = pltpu.make_async_copy(x_ref, y_ref.at[me], local_sem)
        own.start()
        copies = []
        for j in range(1, 8):
            px = (1 - mx) if (j & 4) else mx
            py = (1 - my) if (j & 2) else my
            pc = (1 - mc) if (j & 1) else mc
            cp = pltpu.make_async_remote_copy(
                src_ref=x_ref, dst_ref=y_ref.at[me], send_sem=send_sems.at[j - 1], recv_sem=recv_sems.at[j - 1],
                device_id=(px, py, pc), device_id_type=MESH)
            cp.start()
            copies.append(cp)
        for cp in copies:
            cp.wait_recv()
        for cp in copies:
            cp.wait_send()
        own.wait()

    return pl.pallas_call(
        body, name=name, in_specs=[ANY], out_specs=ANY, out_shape=_sds((8,) + src.shape, src.dtype),
        scratch_shapes=[pltpu.SemaphoreType.DMA((7,)), pltpu.SemaphoreType.DMA((7,)), pltpu.SemaphoreType.DMA])(src)


PACK_W = 1024


def _pack(arrs, rows_multiple, dtype):
    flat = jnp.concatenate([a.reshape(-1).astype(dtype) for a in arrs])
    n = flat.shape[0]
    unit = PACK_W * rows_multiple
    tot = -(-n // unit) * unit
    flat = jnp.pad(flat, (0, tot - n))
    return flat.reshape(tot // PACK_W, PACK_W)


def _unpack(flat, shapes):
    flat = flat.reshape(-1)
    out = []
    off = 0
    for s in shapes:
        n = int(np.prod(s))
        out.append(flat[off:off + n].reshape(s))
        off += n
    return out


def _rope_tables(pos):
    l = pos.shape[0]
    tm = min(512, l)
    inv = (np.float32(ROPE_THETA) ** (-np.arange(0, D_ROPE, 2, dtype=np.float32) / np.float32(D_ROPE))).astype(np.float32)
    lane_f = np.zeros((1, LANE), np.float32)
    lane_f[0, D_NOPE:D_NOPE + 16] = inv
    lane_f[0, D_NOPE + 16:D_NOPE + 32] = inv

    def fn(i, p, f):
        ang = p * f
        lane = lax.broadcasted_iota(jnp.int32, ang.shape, 1)
        co = jnp.cos(ang)
        si = jnp.sin(ang)
        c = jnp.where(lane < D_NOPE, 1.0, jnp.where(lane < D_QK, co, 0.0))
        s1 = jnp.where((lane >= D_NOPE) & (lane < D_NOPE + 16), -si, 0.0)
        s2 = jnp.where((lane >= D_NOPE + 16) & (lane < D_QK), si, 0.0)
        return c, s1, s2

    return _rows(fn, name="rope_tables", n=l // tm, ins=[pos, jnp.asarray(lane_f)],
                 in_specs=[_rt(tm, 1), _full((1, LANE))], outs=[_sds((l, LANE))] * 3, out_specs=[_rt(tm, LANE)] * 3)


def _ssm_param_fn(lr, li, log_dt, br, bi):
    dt = jnp.exp(log_dt)
    mag = jnp.exp(lr * dt)
    a_re = mag * jnp.cos(li * dt)
    a_im = mag * jnp.sin(li * dt)
    den = lr * lr + li * li
    e_re = a_re - 1.0
    e_im = a_im
    f_re = (e_re * lr + e_im * li) / den
    f_im = (e_im * lr - e_re * li) / den
    bb_re = f_re[None] * br - f_im[None] * bi
    bb_im = f_re[None] * bi + f_im[None] * br
    return a_re, a_im, bb_re, bb_im


def _ssm_params(name, lr, li, log_dt, br, bi):
    g, n = lr.shape
    c = br.shape[0]
    return _rows(lambda i, *v: _ssm_param_fn(*v), name=name, n=1, ins=[lr, li, log_dt, br, bi],
                 in_specs=[_full((g, n)), _full((g, n)), _full((g, 1)), _full((c, g, n)), _full((c, g, n))],
                 outs=[_sds((g, n)), _sds((g, n)), _sds((c, g, n)), _sds((c, g, n))],
                 out_specs=[_full((g, n)), _full((g, n)), _full((c, g, n)), _full((c, g, n))])


def _ssm_params_bwd(name, lr, li, log_dt, br, bi, d_are, d_aim, d_bbre, d_bbim):
    g, n = lr.shape
    c = br.shape[0]

    def fn(i, lr, li, log_dt, br, bi, g0, g1, g2, g3):
        _, vjp = jax.vjp(_ssm_param_fn, lr, li, log_dt, br, bi)
        return vjp((g0, g1, g2, g3))

    sp = [_full((g, n)), _full((g, n)), _full((g, 1)), _full((c, g, n)), _full((c, g, n))]
    return _rows(fn, name=name, n=1, ins=[lr, li, log_dt, br, bi, d_are, d_aim, d_bbre, d_bbim],
                 in_specs=sp + [_full((g, n)), _full((g, n)), _full((c, g, n)), _full((c, g, n))],
                 outs=[_sds((g, n)), _sds((g, n)), _sds((g, 1)), _sds((c, g, n)), _sds((c, g, n))], out_specs=sp)


_EYE8 = np.eye(8, dtype=np.float32)


def _blockdiag(v):
    j, g, p, q = v.shape
    m = v[:, :, :, None, :] * jnp.asarray(_EYE8)[None, :, None, :, None]
    return m.reshape(j, g * p, g * q)


def _blockdiag_t(m, p, q):
    j = m.shape[0]
    m = m.reshape(j, 8, p, 8, q)
    return jnp.sum(m * jnp.asarray(_EYE8)[None, :, None, :, None], axis=3)


def _to_perm(v, l):
    ns = SCAN_SEGS
    return v.reshape(ns, l // ns, v.shape[-1]).transpose(1, 0, 2).reshape(l, v.shape[-1])


def _from_perm(v, l):
    ns = SCAN_SEGS
    return v.reshape(l // ns, ns, v.shape[-1]).transpose(1, 0, 2).reshape(l, v.shape[-1])


def _prep_layer(w, i):
    p = {}
    w_in = w['w_in'][i]
    z = lambda n: jnp.zeros((D_MODEL, n), w_in.dtype)
    o = Q_LORA + KV_LORA
    p['w_s'] = jnp.concatenate([w_in[:, :o], z(D_NOPE), w_in[:, o:o + D_ROPE], z(HEAD_PAD - D_QK)], axis=1)
    o += D_ROPE
    p['w_u'] = w_in[:, o:o + SSM_WIDTH]
    o += SSM_WIDTH
    p['w_xq'] = w_in[:, o:o + X_WIDTH]
    o += X_WIDTH
    p['w_g'] = w_in[:, o:]
    wq = w['w_q_b'][i].reshape(Q_LORA, MLA_HEADS, D_QK)
    p['w_qb'] = jnp.pad(wq, ((0, 0), (0, 0), (0, HEAD_PAD - D_QK))).reshape(Q_LORA, MLA_PAD)
    wkv = w['w_kv_b'][i].reshape(KV_LORA, MLA_HEADS, D_NOPE + D_V)
    p['w_k'] = jnp.pad(wkv[:, :, :D_NOPE], ((0, 0), (0, 0), (0, HEAD_PAD - D_NOPE))).reshape(KV_LORA, MLA_PAD)
    p['w_v'] = jnp.pad(wkv[:, :, D_NOPE:], ((0, 0), (0, 0), (0, HEAD_PAD - D_V))).reshape(KV_LORA, MLA_PAD)
    wo = w['w_o_mla'][i].reshape(MLA_HEADS, D_V, D_MODEL)
    p['w_oa'] = jnp.pad(wo, ((0, 0), (0, HEAD_PAD - D_V), (0, 0))).reshape(MLA_PAD, D_MODEL)
    for n in ('w_glu', 'w_o_ssm', 'w_mem_kv', 'w_o_cross', 'w_out', 'w_up', 'w_down'):
        p[n] = w[n][i]
    p['conv_w'] = w['conv_w'][i]
    for n in ('norm_mix_g', 'q_a_norm_g', 'kv_a_norm_g', 'b_glu', 'mem_norm_g', 'xq_norm_g', 'xk_norm_g', 'b_gate',
              'norm_ffn_g', 'conv_b'):
        p[n] = w[n][i].reshape(1, -1)
    p['q_norm_g'] = jnp.pad(w['q_norm_g'][i], (0, HEAD_PAD - D_QK)).reshape(1, HEAD_PAD)
    p['k_norm_g'] = jnp.pad(w['k_norm_g'][i], (0, HEAD_PAD - D_QK)).reshape(1, HEAD_PAD)
    p['ssm_d'] = w['ssm_d'][i].reshape(1, SSM_WIDTH)
    p['lr'] = w['ssm_lambda_re'][i]
    p['li'] = w['ssm_lambda_im'][i]
    p['log_dt'] = w['ssm_log_dt'][i].reshape(SSM_GROUPS, 1)
    p['br'] = w['ssm_b_re'][i].transpose(2, 0, 1)
    p['bi'] = w['ssm_b_im'][i].transpose(2, 0, 1)
    cr = w['ssm_c_re'][i].reshape(SSM_JB, 8, SSM_GROUP_CH, SSM_STATE).transpose(0, 1, 3, 2)
    ci = w['ssm_c_im'][i].reshape(SSM_JB, 8, SSM_GROUP_CH, SSM_STATE).transpose(0, 1, 3, 2)
    p['c_mat'] = jnp.concatenate([_blockdiag(cr), -_blockdiag(ci)], axis=1).astype(BF16)
    return p


def _b_mat(bb_re, bb_im):
    r = bb_re.transpose(1, 0, 2).reshape(SSM_JB, 8, SSM_GROUP_CH, SSM_STATE)
    i = bb_im.transpose(1, 0, 2).reshape(SSM_JB, 8, SSM_GROUP_CH, SSM_STATE)
    return jnp.concatenate([_blockdiag(r), _blockdiag(i)], axis=2).astype(BF16)


def _qkv_fn(ps, c, s1, s2, qag, wqb, kvag, wk, wv, qng, kng):
    c_q = ps[:, :Q_LORA]
    c_kv = ps[:, Q_LORA:Q_LORA + KV_LORA]
    kr = ps[:, Q_LORA + KV_LORA:]
    cqn = _rms(c_q, qag, Q_LORA)
    ckvn = _rms(c_kv, kvag, KV_LORA)
    q_raw = _mm(cqn, wqb)
    k_raw = _mm(ckvn, wk) + jnp.concatenate([kr] * MLA_HEADS, axis=-1)
    v = _mm(ckvn, wv)
    q = _heads(_rope, _head_rms(q_raw, qng, MLA_HEADS, D_QK), MLA_HEADS, c, s1, s2)
    k = _heads(_rope, _head_rms(k_raw, kng, MLA_HEADS, D_QK), MLA_HEADS, c, s1, s2)
    lane = lax.broadcasted_iota(jnp.int32, v.shape, 1)
    v = jnp.where((lane & (LANE - 1)) == D_V, 1.0, v)
    return q * ATT_QSCALE, k, v


def _layer_fwd(name, x, tabs, mem, p):
    l = x.shape[0]
    tm = min(512, l)
    nt = l // tm
    sv = {'x0': x}
    sv['p_g'] = _matmul(name + "_in_g", [(x, p['w_g'])], l, 3 * D_MODEL, rms_gain=p['norm_mix_g'])
    sv['p_u'] = _matmul(name + "_in_u", [(x, p['w_u'])], l, SSM_WIDTH, rms_gain=p['norm_mix_g'])
    sv['p_xq'] = _matmul(name + "_in_xq", [(x, p['w_xq'])], l, X_WIDTH, rms_gain=p['norm_mix_g'])
    sv['p_s'] = _matmul(name + "_in_s", [(x, p['w_s'])], l, SMALL_W, rms_gain=p['norm_mix_g'])

    qkv_consts = [p['q_a_norm_g'], p['w_qb'], p['kv_a_norm_g'], p['w_k'], p['w_v'], p['q_norm_g'], p['k_norm_g']]
    qkv_cspecs = [_full(a.shape) for a in qkv_consts]
    q, k, v = _rows(lambda i, *a: _qkv_fn(*a), name=name + "_qkv", n=nt, ins=[sv['p_s'], *tabs, *qkv_consts],
                    in_specs=[_rt(tm, SMALL_W)] + [_rt(tm, LANE)] * 3 + qkv_cspecs,
                    outs=[_sds((l, MLA_PAD), BF16)] * 3, out_specs=[_rt(tm, MLA_PAD)] * 3)
    sv['q'], sv['k'], sv['v'] = q, k, v
    sv['o_a'], sv['lse_t'] = _flash_fwd(name + "_attn", q, k, v)

    a_re, a_im, bb_re, bb_im = _ssm_params(name + "_ssm_par", p['lr'], p['li'], p['log_dt'], p['br'], p['bi'])
    sv['a_re'], sv['a_im'] = a_re.reshape(1, SSM_LANES), a_im.reshape(1, SSM_LANES)
    sv['b_mat'] = _b_mat(bb_re, bb_im)
    u_p = _to_perm(sv['p_u'], l)
    sv['u_p'] = u_p

    def bu_fn(i, u, bm):
        res = [_mm(u[:, j * LANE:(j + 1) * LANE], bm[j]) for j in range(SSM_JB)]
        return (jnp.concatenate([r[:, :512] for r in res], axis=-1), jnp.concatenate([r[:, 512:] for r in res], axis=-1))

    ts = min(256, l)
    bu_re, bu_im = _rows(bu_fn, name=name + "_ssm_bu", n=l // ts, ins=[u_p, sv['b_mat']],
                         in_specs=[_rt(ts, SSM_WIDTH), _full(sv['b_mat'].shape)],
                         outs=[_sds((l, SSM_LANES))] * 2, out_specs=[_rt(ts, SSM_LANES)] * 2)
    s_re, s_im = _scan(name + "_ssm_scan", bu_re, bu_im, sv['a_re'], sv['a_im'], reverse=False)
    sv['s_re'], sv['s_im'] = s_re, s_im

    def glu_fn(i, sr, si, u, cm, dsk, wg, bg):
        y = jnp.concatenate([_mm(jnp.concatenate([sr[:, j * 512:(j + 1) * 512], si[:, j * 512:(j + 1) * 512]], axis=-1),
                                 cm[j]) for j in range(SSM_JB)], axis=-1) + dsk * u
        zz = _gelu(y)
        return zz * jax.nn.sigmoid(_mm(zz, wg) + bg)

    glu_consts = [p['c_mat'], p['ssm_d'], p['w_glu'], p['b_glu']]
    zo_p = _rows(glu_fn, name=name + "_ssm_glu", n=l // ts, ins=[s_re, s_im, u_p, *glu_consts],
                 in_specs=[_rt(ts, SSM_LANES), _rt(ts, SSM_LANES), _rt(ts, SSM_WIDTH)] + [_full(a.shape) for a in glu_consts],
                 outs=[_sds((l, SSM_WIDTH), BF16)], out_specs=[_rt(ts, SSM_WIDTH)])[0]
    sv['zo'] = _from_perm(zo_p, l)

    m_len = mem.shape[0]

    def memkv_fn(i, mm_, mg, wmk, xkg):
        kv = _mm(_rms(mm_, mg, D_MODEL), wmk)
        return _head_rms(kv[:, :X_WIDTH], xkg, X_HEADS, X_HEAD_DIM), kv[:, X_WIDTH:]

    mem_consts = [p['mem_norm_g'], p['w_mem_kv'], p['xk_norm_g']]
    k_c, v_c = _rows(memkv_fn, name=name + "_memkv", n=1, ins=[mem, *mem_consts],
                     in_specs=[_full(mem.shape)] + [_full(a.shape) for a in mem_consts],
                     outs=[_sds((m_len, X_WIDTH))] * 2, out_specs=[_full((m_len, X_WIDTH))] * 2)
    sv['k_c'], sv['v_c'] = k_c, v_c

    def cross_fn(i, xq, kc, vc, xqg):
        outs = []
        for h in range(X_HEADS):
            sl = slice(h * LANE, (h + 1) * LANE)
            qh = _rms(xq[:, sl], xqg, X_HEAD_DIM)
            s = _mm_nt(qh, kc[:, sl]) * (X_HEAD_DIM ** -0.5)
            s = s - jnp.max(s, axis=-1, keepdims=True)
            e = jnp.exp(s)
            pr = e / jnp.sum(e, axis=-1, keepdims=True)
            outs.append(_mm(pr, vc[:, sl]))
        return jnp.concatenate(outs, axis=-1)

    sv['o_c'] = _rows(cross_fn, name=name + "_cross", n=nt, ins=[sv['p_xq'], k_c, v_c, p['xq_norm_g']],
                      in_specs=[_rt(tm, X_WIDTH), _full(k_c.shape), _full(v_c.shape), _full((1, LANE))],
                      outs=[_sds((l, X_WIDTH), BF16)], out_specs=[_rt(tm, X_WIDTH)])[0]

    def merge_fn(i, oa, zo, oc, pg, x0, woa, wos, woc, bg, wout):
        gates = jax.nn.sigmoid(pg + bg)
        merged = (gates[:, :D_MODEL] * _mm(oa, woa) + gates[:, D_MODEL:2 * D_MODEL] * _mm(zo, wos)
                  + gates[:, 2 * D_MODEL:] * _mm(oc, woc))
        return x0 + _mm(merged, wout), merged

    merge_consts = [p['w_oa'], p['w_o_ssm'], p['w_o_cross'], p['b_gate'], p['w_out']]
    tg = min(256, l)
    x1, merged = _rows(merge_fn, name=name + "_merge", n=l // tg, ins=[sv['o_a'], sv['zo'], sv['o_c'], sv['p_g'], x, *merge_consts],
                       in_specs=[_rt(tg, MLA_PAD), _rt(tg, SSM_WIDTH), _rt(tg, X_WIDTH), _rt(tg, 3 * D_MODEL), _rt(tg, D_MODEL)]
                       + [_full(a.shape) for a in merge_consts],
                       outs=[_sds((l, D_MODEL)), _sds((l, D_MODEL), BF16)], out_specs=[_rt(tg, D_MODEL)] * 2)
    sv['x1'], sv['merged'] = x1, merged

    up = _matmul(name + "_up", [(x1, p['w_up'])], l, 2 * D_FF, rms_gain=p['norm_ffn_g'])
    sv['up'] = up
    tc = min(128, l)

    def conv_fn(i, upt, halo, cw, cb):
        upc = _conv(i, upt, halo, cw) + cb
        return _silu(upc[:, :D_FF]) * upc[:, D_FF:]

    act = _rows(conv_fn, name=name + "_conv", n=l // tc, ins=[up, up, p['conv_w'], p['conv_b']],
                in_specs=[_rt(tc, 2 * D_FF), _halo_prev(tc, 2 * D_FF), _full((3, 2 * D_FF)), _full((1, 2 * D_FF))],
                outs=[_sds((l, D_FF), BF16)], out_specs=[_rt(tc, D_FF)])[0]
    sv['act'] = act
    x2 = _matmul(name + "_down", [(act, p['w_down'])], l, D_MODEL, resid=x1)
    return x2, sv


def _halo_prev(tm, w):
    return pl.BlockSpec((8, w), lambda i: (jnp.maximum(i * (tm // 8) - 1, 0), 0))


def _halo_next(tm, w, n_tiles):
    last = n_tiles * (tm // 8) - 1
    return pl.BlockSpec((8, w), lambda i: (jnp.minimum((i + 1) * (tm // 8), last), 0))


def _conv(i, tile, halo, cw):
    halo = jnp.where(i > 0, halo, 0.0)
    ext = jnp.concatenate([halo, tile], axis=0)
    n = ext.shape[0]
    x1 = pltpu.roll(ext, 1, 0)[8:]
    x2 = pltpu.roll(ext, 2, 0)[8:]
    del n
    return cw[0:1] * x2 + cw[1:2] * x1 + cw[2:3] * tile


def _layer_bwd(name, dx2, sv, tabs, mem, p):
    l = dx2.shape[0]
    tm = min(512, l)
    nt = l // tm
    g = {}
    x1 = sv['x1']
    dact = _matmul(name + "_b_down", [(dx2, p['w_down'])], l, D_FF, nt=True)
    g['w_down'] = _matmul_tn(name + "_gw_down", sv['act'], dx2)
    tc = min(128, l)
    ntc = l // tc

    def conv_b(i, upt, up_prev, up_next, da, da_next, cw, cb):
        up_prev = jnp.where(i > 0, up_prev, 0.0)
        da_next = jnp.where(i < ntc - 1, da_next, 0.0)
        ext = jnp.concatenate([up_prev, upt, up_next], axis=0)
        x0 = ext[8:]
        xm1 = pltpu.roll(ext, 1, 0)[8:]
        xm2 = pltpu.roll(ext, 2, 0)[8:]
        upc = cw[0:1] * xm2 + cw[1:2] * xm1 + cw[2:3] * x0 + cb
        _, vjp = jax.vjp(lambda a, b: _silu(a) * b, upc[:, :D_FF], upc[:, D_FF:])
        dg, dv = vjp(jnp.concatenate([da, da_next], axis=0))
        dupc = jnp.concatenate([dg, dv], axis=-1)
        n = dupc.shape[0]
        dup = cw[2:3] * dupc[:tc] + cw[1:2] * pltpu.roll(dupc, n - 1, 0)[:tc] + cw[0:1] * pltpu.roll(dupc, n - 2, 0)[:tc]
        dt = dupc[:tc]
        dcw = _row_select([_colsum(dt * xm2[:tc]), _colsum(dt * xm1[:tc]), _colsum(dt * upt)], 8)
        return dup, dcw, _colsum(dt)

    dup, g_cw, g_cb = _rows(
        conv_b, name=name + "_b_conv", n=ntc, ins=[sv['up'], sv['up'], sv['up'], dact, dact, p['conv_w'], p['conv_b']],
        in_specs=[_rt(tc, 2 * D_FF), _halo_prev(tc, 2 * D_FF), _halo_next(tc, 2 * D_FF, ntc), _rt(tc, D_FF),
                  _halo_next(tc, D_FF, ntc), _full((3, 2 * D_FF)), _full((1, 2 * D_FF))],
        outs=[_sds((l, 2 * D_FF)), _sds((8, 2 * D_FF)), _sds((1, 2 * D_FF))],
        out_specs=[_rt(tc, 2 * D_FF), _full((8, 2 * D_FF)), _full((1, 2 * D_FF))], n_acc=2, vmem=56)
    g['conv_w'] = g_cw[:3]
    g['conv_b'] = g_cb
    dh2 = _matmul(name + "_b_up", [(dup, p['w_up'])], l, D_MODEL, nt=True, tm=256)
    g['w_up'] = _matmul_tn(name + "_gw_up", x1, dup, rms_gain=p['norm_ffn_g'])

    def norm_b(i, xv, dh, dres, gn):
        _, vjp = jax.vjp(lambda a, b: _rms(a, b, D_MODEL), xv, gn)
        dxv, dgn = vjp(dh)
        return dres + dxv, dgn

    dx1, g['norm_ffn_g'] = _rows(norm_b, name=name + "_b_norm2", n=nt, ins=[x1, dh2, dx2, p['norm_ffn_g']],
                                 in_specs=[_rt(tm, D_MODEL)] * 3 + [_full((1, D_MODEL))],
                                 outs=[_sds((l, D_MODEL)), _sds((1, D_MODEL))], out_specs=[_rt(tm, D_MODEL), _full((1, D_MODEL))],
                                 n_acc=1)

    tg = min(256, l)

    def merge_b(i, dx, oa, zo, oc, pg, woa, wos, woc, bg, wout):
        dm = _mm_nt(dx, wout)
        gates = jax.nn.sigmoid(pg + bg)
        ys = [_mm(oa, woa), _mm(zo, wos), _mm(oc, woc)]
        dys, dpg = [], []
        for b in range(3):
            gb = gates[:, b * D_MODEL:(b + 1) * D_MODEL]
            dys.append(dm * gb)
            dpg.append(dm * ys[b] * gb * (1.0 - gb))
        dpg = jnp.concatenate(dpg, axis=-1)
        return (_mm_nt(dys[0], woa), _mm_nt(dys[1], wos), _mm_nt(dys[2], woc), dpg, dys[0], dys[1], dys[2], _colsum(dpg))

    merge_consts = [p['w_oa'], p['w_o_ssm'], p['w_o_cross'], p['b_gate'], p['w_out']]
    (do_a, dzo, do_c, dp_g, dy_a, dy_b, dy_c, g['b_gate']) = _rows(
        merge_b, name=name + "_b_merge", n=l // tg, ins=[dx1, sv['o_a'], sv['zo'], sv['o_c'], sv['p_g'], *merge_consts],
        in_specs=[_rt(tg, D_MODEL), _rt(tg, MLA_PAD), _rt(tg, SSM_WIDTH), _rt(tg, X_WIDTH), _rt(tg, 3 * D_MODEL)]
        + [_full(a.shape) for a in merge_consts],
        outs=[_sds((l, MLA_PAD)), _sds((l, SSM_WIDTH)), _sds((l, X_WIDTH)), _sds((l, 3 * D_MODEL)),
              _sds((l, D_MODEL), BF16), _sds((l, D_MODEL), BF16), _sds((l, D_MODEL), BF16), _sds((1, 3 * D_MODEL))],
        out_specs=[_rt(tg, MLA_PAD), _rt(tg, SSM_WIDTH), _rt(tg, X_WIDTH), _rt(tg, 3 * D_MODEL),
                   _rt(tg, D_MODEL), _rt(tg, D_MODEL), _rt(tg, D_MODEL), _full((1, 3 * D_MODEL))], n_acc=1, vmem=56)
    g['w_out'] = _matmul_tn(name + "_gw_out", sv['merged'], dx1)
    g['w_oa'] = _matmul_tn(name + "_gw_oa", sv['o_a'], dy_a)
    g['w_o_ssm'] = _matmul_tn(name + "_gw_os", sv['zo'], dy_b)
    g['w_o_cross'] = _matmul_tn(name + "_gw_oc", sv['o_c'], dy_c)

    k_c, v_c = sv['k_c'], sv['v_c']
    m_len = k_c.shape[0]

    def cross_b(i, xq, do, kc, vc, xqg):
        dxq, dk, dv = [], [], []
        dg = jnp.zeros((1, LANE), F32)
        for h in range(X_HEADS):
            sl = slice(h * LANE, (h + 1) * LANE)
            qh, vjp = jax.vjp(lambda a, b: _rms(a, b, X_HEAD_DIM), xq[:, sl], xqg)
            sc = X_HEAD_DIM ** -0.5
            s = _mm_nt(qh, kc[:, sl]) * sc
            s = s - jnp.max(s, axis=-1, keepdims=True)
            e = jnp.exp(s)
            pr = e / jnp.sum(e, axis=-1, keepdims=True)
            doh = do[:, sl]
            dv.append(_mm_tn(pr, doh))
            dp = _mm_nt(doh, vc[:, sl])
            ds = pr * (dp - jnp.sum(dp * pr, axis=-1, keepdims=True)) * sc
            dk.append(_mm_tn(ds, qh))
            dxh, dgh = vjp(_mm(ds, kc[:, sl]))
            dxq.append(dxh)
            dg = dg + dgh
        return jnp.concatenate(dxq, axis=-1), jnp.concatenate(dk, axis=-1), jnp.concatenate(dv, axis=-1), dg

    dp_xq, dk_c, dv_c, g['xq_norm_g'] = _rows(
        cross_b, name=name + "_b_cross", n=nt, ins=[sv['p_xq'], do_c, k_c, v_c, p['xq_norm_g']],
        in_specs=[_rt(tm, X_WIDTH), _rt(tm, X_WIDTH), _full(k_c.shape), _full(v_c.shape), _full((1, LANE))],
        outs=[_sds((l, X_WIDTH)), _sds((m_len, X_WIDTH)), _sds((m_len, X_WIDTH)), _sds((1, LANE))],
        out_specs=[_rt(tm, X_WIDTH), _full((m_len, X_WIDTH)), _full((m_len, X_WIDTH)), _full((1, LANE))], n_acc=3)

    def memkv_b(i, mm_, dk, dv, mg, wmk, xkg):
        memn, vjp_n = jax.vjp(lambda a, b: _rms(a, b, D_MODEL), mm_, mg)
        kv = _mm(memn, wmk)
        _, vjp_k = jax.vjp(lambda a, b: _head_rms(a, b, X_HEADS, X_HEAD_DIM), kv[:, :X_WIDTH], xkg)
        dkr, dxkg = vjp_k(dk)
        dkv = jnp.concatenate([dkr, dv], axis=-1)
        _, dmg = vjp_n(_mm_nt(dkv, wmk))
        return _mm_tn(memn, dkv), dmg, dxkg

    mem_consts = [p['mem_norm_g'], p['w_mem_kv'], p['xk_norm_g']]
    g['w_mem_kv'], g['mem_norm_g'], g['xk_norm_g'] = _rows(
        memkv_b, name=name + "_b_memkv", n=1, ins=[mem, dk_c, dv_c, *mem_consts],
        in_specs=[_full(mem.shape), _full(dk_c.shape), _full(dv_c.shape)] + [_full(a.shape) for a in mem_consts],
        outs=[_sds((D_MODEL, 2 * X_WIDTH)), _sds((1, D_MODEL)), _sds((1, LANE))],
        out_specs=[_full((D_MODEL, 2 * X_WIDTH)), _full((1, D_MODEL)), _full((1, LANE))])

    u_p = sv['u_p']
    dzo_p = _to_perm(dzo, l)
    s_re, s_im = sv['s_re'], sv['s_im']

    def glu_b(i, sr, si, u, dz, cm, dsk, wg, bg):
        cats = [jnp.concatenate([sr[:, j * 512:(j + 1) * 512], si[:, j * 512:(j + 1) * 512]], axis=-1) for j in range(SSM_JB)]
        y = jnp.concatenate([_mm(cats[j], cm[j]) for j in range(SSM_JB)], axis=-1) + dsk * u
        zz, vjp_g = jax.vjp(_gelu, y)
        t = _mm(zz, wg) + bg
        sg = jax.nn.sigmoid(t)
        dt = dz * zz * sg * (1.0 - sg)
        dzz = dz * sg + _mm_nt(dt, wg)
        dy = vjp_g(dzz)[0]
        dss = [_mm_nt(dy[:, j * LANE:(j + 1) * LANE], cm[j]) for j in range(SSM_JB)]
        dsr = jnp.concatenate([d[:, :512] for d in dss], axis=-1)
        dsi = jnp.concatenate([d[:, 512:] for d in dss], axis=-1)
        dcm = jnp.stack([_mm_tn(cats[j], dy[:, j * LANE:(j + 1) * LANE]) for j in range(SSM_JB)], axis=0)
        return dsr, dsi, dy * dsk, dcm, _colsum(dy * u), _mm_tn(zz, dt), _colsum(dt)

    glu_consts = [p['c_mat'], p['ssm_d'], p['w_glu'], p['b_glu']]
    ts = min(256, l)
    nts = l // ts
    ds_re, ds_im, du_dir, g['c_mat'], g['ssm_d'], g['w_glu'], g['b_glu'] = _rows(
        glu_b, name=name + "_b_glu", n=nts, ins=[s_re, s_im, u_p, dzo_p, *glu_consts],
        in_specs=[_rt(ts, SSM_LANES), _rt(ts, SSM_LANES), _rt(ts, SSM_WIDTH), _rt(ts, SSM_WIDTH)] + [_full(a.shape) for a in glu_consts],
        outs=[_sds((l, SSM_LANES)), _sds((l, SSM_LANES)), _sds((l, SSM_WIDTH)), _sds((SSM_JB, 1024, LANE)), _sds((1, SSM_WIDTH)),
              _sds((SSM_WIDTH, SSM_WIDTH)), _sds((1, SSM_WIDTH))],
        out_specs=[_rt(ts, SSM_LANES), _rt(ts, SSM_LANES), _rt(ts, SSM_WIDTH), _full((SSM_JB, 1024, LANE)), _full((1, SSM_WIDTH)),
                   _full((SSM_WIDTH, SSM_WIDTH)), _full((1, SSM_WIDTH))], n_acc=4)
    gb_re, gb_im = _scan(name + "_b_scan", ds_re, ds_im, sv['a_re'], -sv['a_im'], reverse=True)
    ns = SCAN_SEGS
    last_blk = l // ns - 1

    def da_fn(i, gr, gi, sr, si, hr, hi, lr_, li_):
        rid = lax.broadcasted_iota(jnp.int32, lr_.shape, 0)
        fr = jnp.where(rid == 0, 0.0, pltpu.roll(lr_, 1, 0))
        fi = jnp.where(rid == 0, 0.0, pltpu.roll(li_, 1, 0))
        hr = jnp.where(i == 0, fr, hr)
        hi = jnp.where(i == 0, fi, hi)
        if ts > ns:
            pr = jnp.concatenate([hr, sr[:ts - ns]], axis=0)
            pi = jnp.concatenate([hi, si[:ts - ns]], axis=0)
        else:
            pr, pi = hr, hi
        return _colsum(gr * pr + gi * pi), _colsum(gi * pr - gr * pi)

    hprev = pl.BlockSpec((ns, SSM_LANES), lambda i: (jnp.maximum(i * (ts // ns) - 1, 0), 0))
    hlast = pl.BlockSpec((ns, SSM_LANES), lambda i: (last_blk, 0))
    da_re, da_im = _rows(da_fn, name=name + "_b_da", n=nts, ins=[gb_re, gb_im, s_re, s_im, s_re, s_im, s_re, s_im],
                         in_specs=[_rt(ts, SSM_LANES)] * 4 + [hprev, hprev, hlast, hlast],
                         outs=[_sds((1, SSM_LANES))] * 2, out_specs=[_full((1, SSM_LANES))] * 2, n_acc=2)

    def bu_b(i, dbr, dbi, u, dud, bm):
        dus, dbm = [], []
        for j in range(SSM_JB):
            cat = jnp.concatenate([dbr[:, j * 512:(j + 1) * 512], dbi[:, j * 512:(j + 1) * 512]], axis=-1)
            dus.append(_mm_nt(cat, bm[j]))
            dbm.append(_mm_tn(u[:, j * LANE:(j + 1) * LANE], cat))
        return dud + jnp.concatenate(dus, axis=-1), jnp.stack(dbm, axis=0)

    du_p, d_bmat = _rows(bu_b, name=name + "_b_bu", n=nts, ins=[gb_re, gb_im, u_p, du_dir, sv['b_mat']],
                         in_specs=[_rt(ts, SSM_LANES), _rt(ts, SSM_LANES), _rt(ts, SSM_WIDTH), _rt(ts, SSM_WIDTH),
                                   _full(sv['b_mat'].shape)],
                         outs=[_sds((l, SSM_WIDTH)), _sds((SSM_JB, LANE, 1024))],
                         out_specs=[_rt(ts, SSM_WIDTH), _full((SSM_JB, LANE, 1024))], n_acc=1)
    dp_u = _from_perm(du_p, l)
    dbb_re = _blockdiag_t(d_bmat[:, :, :512], SSM_GROUP_CH, SSM_STATE).reshape(SSM_GROUPS, SSM_GROUP_CH, SSM_STATE).transpose(1, 0, 2)
    dbb_im = _blockdiag_t(d_bmat[:, :, 512:], SSM_GROUP_CH, SSM_STATE).reshape(SSM_GROUPS, SSM_GROUP_CH, SSM_STATE).transpose(1, 0, 2)
    g['lr'], g['li'], g['log_dt'], g['br'], g['bi'] = _ssm_params_bwd(
        name + "_b_ssm_par", p['lr'], p['li'], p['log_dt'], p['br'], p['bi'],
        da_re.reshape(SSM_GROUPS, SSM_STATE), da_im.reshape(SSM_GROUPS, SSM_STATE), dbb_re, dbb_im)

    dq_t, dk, dv = _flash_bwd(name + "_b_attn", sv['q'], sv['k'], sv['v'], sv['o_a'], sv['lse_t'], do_a)

    def qkv_b(i, ps, c, s1, s2, dq_, dk_, dv_, qag, wqb, kvag, wk, wv, qng, kng):
        c_q = ps[:, :Q_LORA]
        c_kv = ps[:, Q_LORA:Q_LORA + KV_LORA]
        kr = ps[:, Q_LORA + KV_LORA:]
        cqn, vjp_cq = jax.vjp(lambda a, b: _rms(a, b, Q_LORA), c_q, qag)
        ckvn, vjp_ckv = jax.vjp(lambda a, b: _rms(a, b, KV_LORA), c_kv, kvag)
        q_raw = _mm(cqn, wqb)
        k_raw = _mm(ckvn, wk) + jnp.concatenate([kr] * MLA_HEADS, axis=-1)
        _, vjp_qn = jax.vjp(lambda a, b: _head_rms(a, b, MLA_HEADS, D_QK), q_raw, qng)
        _, vjp_kn = jax.vjp(lambda a, b: _head_rms(a, b, MLA_HEADS, D_QK), k_raw, kng)
        dq_raw, dqng = vjp_qn(_heads(_rope_t, jnp.transpose(dq_[0]), MLA_HEADS, c, s1, s2))
        dk_raw, dkng = vjp_kn(_heads(_rope_t, dk_, MLA_HEADS, c, s1, s2))
        dkr = dk_raw[:, :LANE]
        for h in range(1, MLA_HEADS):
            dkr = dkr + dk_raw[:, h * LANE:(h + 1) * LANE]
        dcq, dqag = vjp_cq(_mm_nt(dq_raw, wqb))
        dckv, dkvag = vjp_ckv(_mm_nt(dk_raw, wk) + _mm_nt(dv_, wv))
        dps = jnp.concatenate([dcq, dckv, dkr], axis=-1)
        return (dps, _mm_tn(cqn, dq_raw), _mm_tn(ckvn, dk_raw), _mm_tn(ckvn, dv_), dqag, dkvag, dqng, dkng)

    qkv_consts = [p['q_a_norm_g'], p['w_qb'], p['kv_a_norm_g'], p['w_k'], p['w_v'], p['q_norm_g'], p['k_norm_g']]
    (dp_s, g['w_qb'], g['w_k'], g['w_v'], g['q_a_norm_g'], g['kv_a_norm_g'], g['q_norm_g'], g['k_norm_g']) = _rows(
        qkv_b, name=name + "_b_qkv", n=nt, ins=[sv['p_s'], *tabs, dq_t, dk, dv, *qkv_consts],
        in_specs=[_rt(tm, SMALL_W)] + [_rt(tm, LANE)] * 3 + [pl.BlockSpec((1, MLA_PAD, tm), lambda i: (i, 0, 0))]
        + [_rt(tm, MLA_PAD)] * 2 + [_full(a.shape) for a in qkv_consts],
        outs=[_sds((l, SMALL_W)), _sds((Q_LORA, MLA_PAD)), _sds((KV_LORA, MLA_PAD)), _sds((KV_LORA, MLA_PAD)),
              _sds((1, Q_LORA)), _sds((1, KV_LORA)), _sds((1, LANE)), _sds((1, LANE))],
        out_specs=[_rt(tm, SMALL_W), _full((Q_LORA, MLA_PAD)), _full((KV_LORA, MLA_PAD)), _full((KV_LORA, MLA_PAD)),
                   _full((1, Q_LORA)), _full((1, KV_LORA)), _full((1, LANE)), _full((1, LANE))], n_acc=7)

    x0 = sv['x0']
    dh = _matmul(name + "_b_in", [(dp_g, p['w_g']), (dp_u, p['w_u']), (dp_xq, p['w_xq']), (dp_s, p['w_s'])], l, D_MODEL, nt=True,
                 tm=256)
    gm = p['norm_mix_g']
    g['w_g'] = _matmul_tn(name + "_gw_g", x0, dp_g, rms_gain=gm)
    g['w_u'] = _matmul_tn(name + "_gw_u", x0, dp_u, rms_gain=gm)
    g['w_xq'] = _matmul_tn(name + "_gw_xq", x0, dp_xq, rms_gain=gm)
    g['w_s'] = _matmul_tn(name + "_gw_s", x0, dp_s, rms_gain=gm)
    dx0, g['norm_mix_g'] = _rows(norm_b, name=name + "_b_norm1", n=nt, ins=[x0, dh, dx1, gm],
                                 in_specs=[_rt(tm, D_MODEL)] * 3 + [_full((1, D_MODEL))],
                                 outs=[_sds((l, D_MODEL)), _sds((1, D_MODEL))], out_specs=[_rt(tm, D_MODEL), _full((1, D_MODEL))],
                                 n_acc=1)
    return dx0, g


def _unprep_grads(g):
    o = {}
    ws = g['w_s']
    o['w_in'] = jnp.concatenate([ws[:, :Q_LORA + KV_LORA], ws[:, Q_LORA + KV_LORA + D_NOPE:Q_LORA + KV_LORA + D_QK],
                                 g['w_u'], g['w_xq'], g['w_g']], axis=1)
    o['w_q_b'] = g['w_qb'].reshape(Q_LORA, MLA_HEADS, HEAD_PAD)[:, :, :D_QK].reshape(Q_LORA, MLA_HEADS * D_QK)
    gk = g['w_k'].reshape(KV_LORA, MLA_HEADS, HEAD_PAD)[:, :, :D_NOPE]
    gv = g['w_v'].reshape(KV_LORA, MLA_HEADS, HEAD_PAD)[:, :, :D_V]
    o['w_kv_b'] = jnp.concatenate([gk, gv], axis=2).reshape(KV_LORA, MLA_HEADS * (D_NOPE + D_V))
    o['w_o_mla'] = g['w_oa'].reshape(MLA_HEADS, HEAD_PAD, D_MODEL)[:, :D_V].reshape(MLA_HEADS * D_V, D_MODEL)
    for n in ('w_glu', 'w_o_ssm', 'w_mem_kv', 'w_o_cross', 'w_out', 'w_up', 'w_down', 'conv_w'):
        o[n] = g[n]
    for n in ('norm_mix_g', 'q_a_norm_g', 'kv_a_norm_g', 'b_glu', 'mem_norm_g', 'xq_norm_g', 'xk_norm_g', 'b_gate',
              'norm_ffn_g', 'conv_b'):
        o[n] = g[n].reshape(-1)
    o['q_norm_g'] = g['q_norm_g'].reshape(-1)[:D_QK]
    o['k_norm_g'] = g['k_norm_g'].reshape(-1)[:D_QK]
    o['ssm_d'] = g['ssm_d'].reshape(SSM_GROUPS, SSM_GROUP_CH)
    o['ssm_lambda_re'] = g['lr']
    o['ssm_lambda_im'] = g['li']
    o['ssm_log_dt'] = g['log_dt'].reshape(SSM_GROUPS)
    o['ssm_b_re'] = g['br'].transpose(1, 2, 0)
    o['ssm_b_im'] = g['bi'].transpose(1, 2, 0)
    dc = g['c_mat']
    o['ssm_c_re'] = _blockdiag_t(dc[:, :512], SSM_STATE, SSM_GROUP_CH).transpose(0, 1, 3, 2).reshape(SSM_GROUPS, SSM_GROUP_CH, SSM_STATE)
    o['ssm_c_im'] = -_blockdiag_t(dc[:, 512:], SSM_STATE, SSM_GROUP_CH).transpose(0, 1, 3, 2).reshape(SSM_GROUPS, SSM_GROUP_CH, SSM_STATE)
    return o


def _local_step(x, mem, pos, target, w):
    l = x.shape[0]
    tm = min(512, l)
    tabs = _rope_tables(pos.astype(F32).reshape(l, 1))
    ps = [_prep_layer(w, i) for i in range(DEPTH)]
    saved = []
    h = x
    for i in range(DEPTH):
        h, sv = _layer_fwd("l%d" % i, h, tabs, mem, ps[i])
        saved.append(sv)

    def loss_fn(i, y, t):
        e = y - t
        per_tok = jnp.sum(e * e, axis=-1, keepdims=True) * (1.0 / D_MODEL)
        tot = 0.5 * jnp.sum(per_tok, axis=0, keepdims=True)
        return e * (1.0 / D_MODEL), jnp.broadcast_to(tot, (1, LANE))

    dy, loss = _rows(loss_fn, name="loss", n=l // tm, ins=[h, target], in_specs=[_rt(tm, D_MODEL)] * 2,
                     outs=[_sds((l, D_MODEL)), _sds((1, LANE))], out_specs=[_rt(tm, D_MODEL), _full((1, LANE))], n_acc=1)
    grads = []
    d = dy
    for i in reversed(range(DEPTH)):
        d, g = _layer_bwd("l%d" % i, d, saved[i], tabs, mem, ps[i])
        grads.append(_unprep_grads(g))
    return loss[0, 0], d, grads[::-1]


def _sum4(name, own, got):
    r, c = own.shape
    tr = _row_tile(r)

    def fn(i, o, v):
        return ((o.astype(F32) + v[0].astype(F32)) + v[1].astype(F32)) + v[2].astype(F32)

    return _rows(fn, name=name, n=r // tr, ins=[own, got],
                 in_specs=[_rt(tr, c), pl.BlockSpec((3, tr, c), lambda i: (0, i, 0))],
                 outs=[_sds((r, c))], out_specs=[_rt(tr, c)])[0]


def _row_tile(r):
    for t in (256, 128, 64, 32, 16, 8):
        if r % t == 0:
            return t
    return r


def _adamw(name, parts, w, m, v):
    r, cw = w.shape
    tr = _row_tile(r)
    np_ = len(parts)

    def fn(i, *vals):
        wv, mv, vv = vals[np_:]
        terms = []
        for pv in vals[:np_]:
            terms += [pv] if pv.ndim == 2 else [pv[k] for k in range(pv.shape[0])]
        g = terms[0]
        for t in terms[1:]:
            g = g + t
        mn = ADAM_B1 * mv + (1.0 - ADAM_B1) * g
        vn = ADAM_B2 * vv + (1.0 - ADAM_B2) * (g * g)
        m_hat = mn / (1.0 - ADAM_B1 ** ADAM_STEP)
        v_hat = vn / (1.0 - ADAM_B2 ** ADAM_STEP)
        delta = -ADAM_LR * (m_hat / (jnp.sqrt(v_hat) + ADAM_EPS) + ADAM_WD * wv)
        return g, delta, mn, vn

    pspecs = [_rt(tr, cw) if p.ndim == 2 else pl.BlockSpec((p.shape[0], tr, cw), lambda i: (0, i, 0)) for p in parts]
    return _rows(fn, name=name, n=r // tr, ins=[*parts, w, m, v], in_specs=pspecs + [_rt(tr, cw)] * 3,
                 outs=[_sds((r, cw))] * 4, out_specs=[_rt(tr, cw)] * 4)


def _shard_of(a, axis, k):
    n = a.shape[axis] // 4
    return lax.slice_in_dim(a, k * n, (k + 1) * n, axis=axis)


def _step(a):
    x = a['x'][0]
    mem = a['mem'][0]
    pos = a['positions'][0]
    target = a['loss_target'][0]

    me = 2 * lax.axis_index("x") + lax.axis_index("y")

    mine = [a[n] if n == 'conv_w' else a[n].astype(BF16) for n in SHARDED]
    got = _gather_d2d(_gather_ici(mine))
    w = {}
    for n, own, y in zip(SHARDED, mine, got):
        ax = SHARD_AXIS[n] - 1
        w[n] = [jnp.concatenate([jnp.where(me == k, own[i], y[k, i]) for k in range(4)], axis=ax) for i in range(DEPTH)]
    for n in SMALL:
        w[n] = a[n]

    loss, grad_x, grads = _local_step(x, mem, pos, target, w)

    gsh = []
    for n in SHARDED:
        ax = SHARD_AXIS[n] - 1
        gsh.append(jnp.stack([jnp.stack([_shard_of(grads[i][n], ax, k) for i in range(DEPTH)], axis=0)
                              for k in range(4)], axis=0).astype(BF16))
    got = _reduce_ici(gsh)
    parts = []
    for n, g4, g3 in zip(SHARDED, gsh, got):
        own = lax.dynamic_index_in_dim(g4, me, axis=0, keepdims=False)
        cols = own.shape[-1]
        parts.append(_sum4("sum4_" + n, own.reshape(-1, cols), g3.reshape(3, -1, cols)).reshape(own.shape))
    others = _swap_d2d(parts)
    res_sh = []
    for n, part, other in zip(SHARDED, parts, others):
        cols = part.shape[-1]
        res = _adamw("adamw_" + n, [part.reshape(-1, cols), other.reshape(-1, cols)],
                     *[a[pre + n].reshape(-1, cols) for pre in ('', 'm_', 'v_')])
        res_sh.append([r.reshape(a[n].shape) for r in res])
    res_sh = [[res_sh[j][kind] for j in range(len(SHARDED))] for kind in range(4)]

    sm_shapes = [a[n].shape for n in SMALL] + [(1,)]
    gsm = _pack([jnp.stack([grads[i][n] for i in range(DEPTH)], axis=0) for n in SMALL] + [loss.reshape(1)], 8, F32)
    alls = _all_exchange("comm_reduce_small", gsm)
    zero1 = jnp.zeros((1,), F32)
    res_sm = _adamw("adamw_small", [alls], *[_pack([a[pre + n] for n in SMALL] + [zero1], 8, F32) for pre in ('', 'm_', 'v_')])
    res_sm = [_unpack(r, sm_shapes) for r in res_sm]
    loss = res_sm[0][-1][0]

    outs = [loss, grad_x[None]]
    for kind in range(4):
        byname = dict(zip(SHARDED, res_sh[kind]))
        byname.update(zip(SMALL, res_sm[kind]))
        outs += [byname[n] for n in WEIGHTS]
    return tuple(outs)


def kernel(x, mem, positions, norm_mix_g, w_in, q_a_norm_g, w_q_b, kv_a_norm_g, w_kv_b, q_norm_g, k_norm_g, w_o_mla, ssm_lambda_re, ssm_lambda_im, ssm_log_dt, ssm_b_re, ssm_b_im, ssm_c_re, ssm_c_im, ssm_d, w_glu, b_glu, w_o_ssm, mem_norm_g, w_mem_kv, xq_norm_g, xk_norm_g, w_o_cross, b_gate, w_out, norm_ffn_g, w_up, conv_w, conv_b, w_down, loss_target, m_norm_mix_g, m_w_in, m_q_a_norm_g, m_w_q_b, m_kv_a_norm_g, m_w_kv_b, m_q_norm_g, m_k_norm_g, m_w_o_mla, m_ssm_lambda_re, m_ssm_lambda_im, m_ssm_log_dt, m_ssm_b_re, m_ssm_b_im, m_ssm_c_re, m_ssm_c_im, m_ssm_d, m_w_glu, m_b_glu, m_w_o_ssm, m_mem_norm_g, m_w_mem_kv, m_xq_norm_g, m_xk_norm_g, m_w_o_cross, m_b_gate, m_w_out, m_norm_ffn_g, m_w_up, m_conv_w, m_conv_b, m_w_down, v_norm_mix_g, v_w_in, v_q_a_norm_g, v_w_q_b, v_kv_a_norm_g, v_w_kv_b, v_q_norm_g, v_k_norm_g, v_w_o_mla, v_ssm_lambda_re, v_ssm_lambda_im, v_ssm_log_dt, v_ssm_b_re, v_ssm_b_im, v_ssm_c_re, v_ssm_c_im, v_ssm_d, v_w_glu, v_b_glu, v_w_o_ssm, v_mem_norm_g, v_w_mem_kv, v_xq_norm_g, v_xk_norm_g, v_w_o_cross, v_b_gate, v_w_out, v_norm_ffn_g, v_w_up, v_conv_w, v_conv_b, v_w_down):
    return _step(dict(locals()))
```

```python
import functools
import math

import numpy as np
import jax
import jax.numpy as jnp
from jax import lax
from jax.experimental import pallas as pl
from jax.experimental.pallas import tpu as pltpu

F32 = jnp.float32
BF16 = jnp.bfloat16
MESH = pl.DeviceIdType.MESH

DEPTH = 2
D_MODEL = 1024
EPS = 1e-6
MLA_HEADS = 8
Q_LORA = 384
KV_LORA = 256
D_NOPE = 64
D_ROPE = 32
D_QK = D_NOPE + D_ROPE
D_V = 64
HEAD_PAD = 128
MLA_PAD = MLA_HEADS * HEAD_PAD
ROPE_THETA = 10000.0
SSM_GROUPS = 32
SSM_GROUP_CH = 16
SSM_WIDTH = 512
SSM_STATE = 64
SSM_LANES = SSM_GROUPS * SSM_STATE
SSM_JB = 4
X_HEADS = 4
X_HEAD_DIM = 128
X_WIDTH = 512
D_FF = 2816
SMALL_W = Q_LORA + KV_LORA + HEAD_PAD
SCAN_SEGS = 32
LANE = 128
NEG = -1e30

ADAM_LR = 0.001
ADAM_B1 = 0.9
ADAM_B2 = 0.999
ADAM_EPS = 1e-08
ADAM_WD = 0.01
ADAM_STEP = 10

WEIGHTS = ['norm_mix_g', 'w_in', 'q_a_norm_g', 'w_q_b', 'kv_a_norm_g', 'w_kv_b', 'q_norm_g', 'k_norm_g', 'w_o_mla',
           'ssm_lambda_re', 'ssm_lambda_im', 'ssm_log_dt', 'ssm_b_re', 'ssm_b_im', 'ssm_c_re', 'ssm_c_im', 'ssm_d',
           'w_glu', 'b_glu', 'w_o_ssm', 'mem_norm_g', 'w_mem_kv', 'xq_norm_g', 'xk_norm_g', 'w_o_cross', 'b_gate',
           'w_out', 'norm_ffn_g', 'w_up', 'conv_w', 'conv_b', 'w_down']
SHARD_AXIS = {'w_in': 2, 'w_q_b': 2, 'w_kv_b': 2, 'w_o_mla': 2, 'w_glu': 1, 'w_o_ssm': 2, 'w_mem_kv': 1,
              'w_o_cross': 2, 'w_out': 1, 'w_up': 2, 'conv_w': 2, 'w_down': 1}
SHARDED = [n for n in WEIGHTS if n in SHARD_AXIS]
GATHER_BF16 = [n for n in SHARDED if n != 'conv_w']
SMALL = [n for n in WEIGHTS if n not in SHARD_AXIS]


def _bf(v):
    return v.astype(BF16)


def _mm(a, b):
    return jnp.dot(_bf(a), _bf(b), preferred_element_type=F32)


def _mm_nt(a, b):
    return lax.dot_general(_bf(a), _bf(b), (((1,), (1,)), ((), ())), preferred_element_type=F32)


def _mm_tn(a, b):
    return lax.dot_general(_bf(a), _bf(b), (((0,), (0,)), ((), ())), preferred_element_type=F32)


def _rms(v, g, n):
    ms = jnp.sum(v * v, axis=-1, keepdims=True) * (1.0 / n)
    return (v * lax.rsqrt(ms + EPS)) * g


def _head_rms(v, g, heads, n):
    return jnp.concatenate([_rms(v[:, h * LANE:(h + 1) * LANE], g, n) for h in range(heads)], axis=-1)


def _rope(v, c, s1, s2):
    return v * c + pltpu.roll(v, LANE - 16, 1) * s1 + pltpu.roll(v, 16, 1) * s2


def _rope_t(g, c, s1, s2):
    return g * c + pltpu.roll(g * s1, 16, 1) + pltpu.roll(g * s2, LANE - 16, 1)


def _heads(fn, v, heads, *tabs):
    return jnp.concatenate([fn(v[:, h * LANE:(h + 1) * LANE], *tabs) for h in range(heads)], axis=-1)


def _gelu(y):
    return y * (0.5 * (1.0 + jnp.tanh(math.sqrt(2.0 / math.pi) * (y + 0.044715 * (y * y * y)))))


def _silu(g):
    return g * jax.nn.sigmoid(g)


def _colsum(v):
    return jnp.sum(v, axis=0, keepdims=True)


def _row_select(rows, n):
    rid = lax.broadcasted_iota(jnp.int32, (n, rows[0].shape[-1]), 0)
    out = jnp.zeros((n, rows[0].shape[-1]), F32)
    for k, r in enumerate(rows):
        out = jnp.where(rid == k, jnp.broadcast_to(r, out.shape), out)
    return out


def _params(sem, vmem_mb):
    return pltpu.CompilerParams(dimension_semantics=sem, vmem_limit_bytes=vmem_mb * 1024 * 1024)


def _rt(tm, w, cb=0):
    return pl.BlockSpec((tm, w), lambda i: (i, cb))


def _full(shape):
    nd = len(shape)
    return pl.BlockSpec(tuple(shape), lambda i: (0,) * nd)


def _rows(fn, *, name, n, ins, in_specs, outs, out_specs, n_acc=0, vmem=48):
    n_in = len(ins)
    n_out = len(outs)

    def body(*refs):
        i = pl.program_id(0)
        res = fn(i, *[r[...] for r in refs[:n_in]])
        if not isinstance(res, (tuple, list)):
            res = (res,)
        assert len(res) == n_out, (name, len(res), n_out)
        for k, (r, v) in enumerate(zip(refs[n_in:], res)):
            if k < n_out - n_acc:
                r[...] = v.astype(r.dtype)
            else:
                @pl.when(i == 0)
                def _():
                    r[...] = v

                @pl.when(i > 0)
                def _():
                    r[...] += v

    return pl.pallas_call(
        body, name=name, grid=(n,), in_specs=list(in_specs), out_specs=tuple(out_specs), out_shape=tuple(outs),
        compiler_params=_params(("arbitrary",), vmem))(*ins)


def _sds(shape, dtype=F32):
    return jax.ShapeDtypeStruct(tuple(shape), dtype)


def _tile_n(n, cap=1536):
    best = None
    for t in range(LANE, min(n, cap) + 1, LANE):
        if n % t == 0:
            best = t
    if best is None or n <= 1408:
        return n
    return best


def _matmul(name, pairs, m, n, *, nt=False, rms_gain=None, resid=None, out_dtype=F32, tm=512, vmem=56):
    tm = min(tm, m)
    tn = _tile_n(n)
    ks = [a.shape[1] for a, _ in pairs]
    np_ = len(pairs)

    def body(*refs):
        a_refs = refs[:np_]
        b_refs = refs[np_:2 * np_]
        k = 2 * np_
        g_ref = None
        r_ref = None
        if rms_gain is not None:
            g_ref = refs[k]
            k += 1
        if resid is not None:
            r_ref = refs[k]
            k += 1
        o_ref = refs[k]
        scr = refs[k + 1:]
        j = pl.program_id(1)

        @pl.when(j == 0)
        def _():
            for p in range(np_):
                a = a_refs[p][...]
                if p == 0 and g_ref is not None:
                    a = _rms(a.astype(F32), g_ref[...], ks[0])
                scr[p][...] = a.astype(BF16)

        acc = None
        for p in range(np_):
            b = b_refs[p][...].astype(BF16)
            if nt:
                t = lax.dot_general(scr[p][...], b, (((1,), (1,)), ((), ())), preferred_element_type=F32)
            else:
                t = jnp.dot(scr[p][...], b, preferred_element_type=F32)
            acc = t if acc is None else acc + t
        if r_ref is not None:
            acc = acc + r_ref[...]
        o_ref[...] = acc.astype(o_ref.dtype)

    in_specs = [pl.BlockSpec((tm, kk), lambda i, j: (i, 0)) for kk in ks]
    if nt:
        in_specs += [pl.BlockSpec((tn, kk), lambda i, j: (j, 0)) for kk in ks]
    else:
        in_specs += [pl.BlockSpec((kk, tn), lambda i, j: (0, j)) for kk in ks]
    ins = [a for a, _ in pairs] + [b for _, b in pairs]
    if rms_gain is not None:
        in_specs.append(pl.BlockSpec((1, ks[0]), lambda i, j: (0, 0)))
        ins.append(rms_gain)
    if resid is not None:
        in_specs.append(pl.BlockSpec((tm, tn), lambda i, j: (i, j)))
        ins.append(resid)
    return pl.pallas_call(
        body, name=name, grid=(m // tm, n // tn), in_specs=in_specs,
        out_specs=pl.BlockSpec((tm, tn), lambda i, j: (i, j)), out_shape=_sds((m, n), out_dtype),
        scratch_shapes=[pltpu.VMEM((tm, kk), BF16) for kk in ks],
        compiler_params=_params(("arbitrary", "arbitrary"), vmem))(*ins)


def _matmul_tn(name, a, b, *, rms_gain=None, tl=512, vmem=56):
    l, ka = a.shape
    n = b.shape[1]
    tl = min(tl, l)
    tn = _tile_n(n, 1536)

    def body(*refs):
        if rms_gain is not None:
            a_ref, b_ref, g_ref, o_ref = refs
        else:
            a_ref, b_ref, o_ref = refs
        t = pl.program_id(1)
        av = a_ref[...]
        if rms_gain is not None:
            av = _rms(av.astype(F32), g_ref[...], ka)
        v = _mm_tn(av, b_ref[...])

        @pl.when(t == 0)
        def _():
            o_ref[...] = v

        @pl.when(t > 0)
        def _():
            o_ref[...] += v

    in_specs = [pl.BlockSpec((tl, ka), lambda j, t: (t, 0)), pl.BlockSpec((tl, tn), lambda j, t: (t, j))]
    ins = [a, b]
    if rms_gain is not None:
        in_specs.append(pl.BlockSpec((1, ka), lambda j, t: (0, 0)))
        ins.append(rms_gain)
    return pl.pallas_call(
        body, name=name, grid=(n // tn, l // tl), in_specs=in_specs,
        out_specs=pl.BlockSpec((ka, tn), lambda j, t: (0, j)), out_shape=_sds((ka, n)),
        compiler_params=_params(("arbitrary", "arbitrary"), vmem))(*ins)


ATT_HEADS_PER_STEP = 2
ATT_W = ATT_HEADS_PER_STEP * LANE
ATT_GROUPS = MLA_HEADS // ATT_HEADS_PER_STEP
LOG2E = math.log2(math.e)
ATT_SCALE = D_QK ** -0.5
ATT_QSCALE = ATT_SCALE * LOG2E


def _tri_tables(nq, by_k):
    qs, ks = [], []
    if by_k:
        for ki in range(nq):
            for qi in range(ki, nq):
                qs.append(qi)
                ks.append(ki)
    else:
        for qi in range(nq):
            for ki in range(qi + 1):
                qs.append(qi)
                ks.append(ki)
    return jnp.asarray(np.array(qs, np.int32)), jnp.asarray(np.array(ks, np.int32))


def _causal_keep(shape, transposed):
    r = lax.broadcasted_iota(jnp.int32, shape, 0)
    c = lax.broadcasted_iota(jnp.int32, shape, 1)
    return (r <= c) if transposed else (c <= r)


def _nt16(a, b):
    return lax.dot_general(a, b, (((1,), (1,)), ((), ())), preferred_element_type=F32)


def _row_form(col):
    return jnp.transpose(jnp.broadcast_to(col, (col.shape[0], LANE)))[:8]


def _att_call(body, name, l, tq, tabs, ins, in_specs, outs, out_specs, scratch=()):
    grid_spec = pltpu.PrefetchScalarGridSpec(
        num_scalar_prefetch=2, grid=(ATT_GROUPS, tabs[0].shape[0]), in_specs=in_specs, out_specs=out_specs,
        scratch_shapes=list(scratch))
    return pl.pallas_call(body, name=name, grid_spec=grid_spec, out_shape=outs,
                          compiler_params=_params(("arbitrary", "arbitrary"), 48))(*tabs, *ins)


def _flash_fwd(name, q, k, v_t):
    l = q.shape[0]
    tq = min(512, l)
    nq = l // tq
    tabs = _tri_tables(nq, by_k=False)

    def body(qt, kt, q_ref, k_ref, vt_ref, o_ref, lset_ref, m_s, acc_s):
        t = pl.program_id(1)
        qi = qt[t]
        ki = kt[t]
        sls = [slice(h * LANE, (h + 1) * LANE) for h in range(ATT_HEADS_PER_STEP)]

        @pl.when(ki == 0)
        def _():
            m_s[...] = jnp.full(m_s.shape, NEG, F32)
            acc_s[...] = jnp.zeros(acc_s.shape, F32)

        def step(masked):
            sts = [_nt16(k_ref[:, sl], q_ref[:, sl]) for sl in sls]
            for h, sl in enumerate(sls):
                st = sts[h]
                if masked:
                    st = jnp.where(_causal_keep(st.shape, True), st, NEG)
                m_old = m_s[h][:1]
                m_new = jnp.maximum(m_old, jnp.max(st, axis=0, keepdims=True))
                alpha = jnp.exp2(m_old - m_new)
                pt = jnp.exp2(st - m_new).astype(BF16)
                acc_s[sl, :] = alpha * acc_s[sl, :] + jnp.dot(vt_ref[sl, :], pt, preferred_element_type=F32)
                m_s[h] = jnp.broadcast_to(m_new, (8, tq))

        @pl.when(ki < qi)
        def _():
            step(False)

        @pl.when(ki == qi)
        def _():
            step(True)
            row = lax.broadcasted_iota(jnp.int32, (LANE, tq), 0)
            for h, sl in enumerate(sls):
                acc = acc_s[sl, :]
                lsum = acc[D_V:D_V + 1, :]
                o_ref[:, sl] = jnp.transpose(jnp.where(row < D_V, acc / lsum, 0.0))
                lset_ref[h * 8:(h + 1) * 8, :] = m_s[h] + jnp.log2(lsum)

    qspec = pl.BlockSpec((tq, ATT_W), lambda g, t, qt, kt: (qt[t], g))
    kspec = pl.BlockSpec((tq, ATT_W), lambda g, t, qt, kt: (kt[t], g))
    vspec = pl.BlockSpec((ATT_W, tq), lambda g, t, qt, kt: (g, kt[t]))
    rspec = pl.BlockSpec((8 * ATT_HEADS_PER_STEP, tq), lambda g, t, qt, kt: (g, qt[t]))
    return _att_call(
        body, name, l, tq, tabs, [q, k, v_t], [qspec, kspec, vspec],
        (_sds((l, MLA_PAD)), _sds((8 * MLA_HEADS, l))), (qspec, rspec),
        scratch=[pltpu.VMEM((ATT_HEADS_PER_STEP, 8, tq), F32), pltpu.VMEM((ATT_W, tq), F32)])


def _flash_bwd(name, q, k, v, o, lse_t, do):
    l = q.shape[0]
    tq = min(512, l)
    nq = l // tq

    def delta_fn(i, dov, ov):
        rows = []
        for h in range(MLA_HEADS):
            sl = slice(h * LANE, (h + 1) * LANE)
            rows.append(_row_form(jnp.sum(dov[:, sl] * ov[:, sl], axis=-1, keepdims=True)))
        return jnp.concatenate(rows, axis=0), dov

    delta_t, do16 = _rows(
        delta_fn, name=name + "_delta", n=nq, ins=[do, o], in_specs=[_rt(tq, MLA_PAD)] * 2,
        outs=[_sds((8 * MLA_HEADS, l)), _sds((l, MLA_PAD), BF16)],
        out_specs=[pl.BlockSpec((8 * MLA_HEADS, tq), lambda i: (0, i)), _rt(tq, MLA_PAD)])

    def body(qt, kt, q_ref, k_ref, v_ref, do_ref, lset_ref, dlt_ref, dk_ref, dv_ref, dqt_ref, kt_s):
        t = pl.program_id(1)
        qi = qt[t]
        ki = kt[t]
        sls = [slice(h * LANE, (h + 1) * LANE) for h in range(ATT_HEADS_PER_STEP)]

        @pl.when(ki == 0)
        def _():
            dqt_ref[qi] = jnp.zeros((ATT_W, tq), F32)

        def step(masked):
            sts = [_nt16(k_ref[:, sl], q_ref[:, sl]) for sl in sls]
            dpts = [_nt16(v_ref[:, sl], do_ref[:, sl]) for sl in sls]
            for h, sl in enumerate(sls):
                st = sts[h]
                if masked:
                    st = jnp.where(_causal_keep(st.shape, True), st, NEG)
                pt = jnp.exp2(st - lset_ref[h * 8:(h + 1) * 8, :][:1])
                dst = (pt * (dpts[h] - dlt_ref[h * 8:(h + 1) * 8, :][:1])).astype(BF16)
                dv_ref[:, sl] += jnp.dot(pt.astype(BF16), do_ref[:, sl], preferred_element_type=F32)
                dk_ref[:, sl] += jnp.dot(dst, q_ref[:, sl], preferred_element_type=F32)
                dqt_ref[qi, sl, :] += jnp.dot(kt_s[sl, :], dst, preferred_element_type=F32)

        @pl.when(qi == ki)
        def _():
            dk_ref[...] = jnp.zeros(dk_ref.shape, F32)
            dv_ref[...] = jnp.zeros(dv_ref.shape, F32)
            for sl in sls:
                kt_s[sl, :] = jnp.transpose(k_ref[:, sl].astype(F32)).astype(BF16)
            step(True)
            dqt_ref[qi] = dqt_ref[qi] * ATT_SCALE

        @pl.when(qi > ki)
        def _():
            step(False)

        @pl.when(qi == nq - 1)
        def _():
            dk_ref[...] = dk_ref[...] * (1.0 / LOG2E)

    tabs_k = _tri_tables(nq, by_k=True)
    qspec = pl.BlockSpec((tq, ATT_W), lambda g, t, qt, kt: (qt[t], g))
    kspec = pl.BlockSpec((tq, ATT_W), lambda g, t, qt, kt: (kt[t], g))
    rspec = pl.BlockSpec((8 * ATT_HEADS_PER_STEP, tq), lambda g, t, qt, kt: (g, qt[t]))
    dqspec = pl.BlockSpec((nq, ATT_W, tq), lambda g, t, qt, kt: (0, g, 0))
    dk, dv, dq_t = _att_call(body, name + "_dqkv", l, tq, tabs_k, [q, k, v, do16, lse_t, delta_t],
                             [qspec, kspec, kspec, qspec, rspec, rspec],
                             (_sds((l, MLA_PAD)), _sds((l, MLA_PAD)), _sds((nq, MLA_PAD, tq))), (kspec, kspec, dqspec),
                             scratch=[pltpu.VMEM((ATT_W, tq), BF16)])
    return dq_t, dk, dv


def _cmul(ar, ai, br, bi):
    return ar * br - ai * bi, ar * bi + ai * br


def _scan(name, x_re, x_im, a_re, a_im, reverse):
    l, lanes = x_re.shape
    ns = SCAN_SEGS
    tl = l // ns
    steps = int(math.log2(tl))
    assert 2 ** steps == tl and tl * ns == l

    def body(xr_ref, xi_ref, ar_ref, ai_ref, sr_ref, si_ref):
        a_r1 = ar_ref[...]
        a_i1 = ai_ref[...]
        a_r = jnp.broadcast_to(a_r1, (ns, LANE))
        a_i = jnp.broadcast_to(a_i1, (ns, LANE))

        def rows(t):
            t = (tl - 1 - t) if reverse else t
            return pl.ds(pl.multiple_of(t * ns, ns), ns)

        def local(t, carry):
            cr, ci = carry
            r = rows(t)
            pr, pi = _cmul(a_r, a_i, cr, ci)
            return pr + xr_ref[r, :], pi + xi_ref[r, :]

        zero = jnp.zeros((ns, LANE), F32)
        e_r, e_i = lax.fori_loop(0, tl, local, (zero, zero))
        p_r, p_i = a_r1, a_i1
        for _ in range(steps):
            p_r, p_i = _cmul(p_r, p_i, p_r, p_i)
        rid = lax.broadcasted_iota(jnp.int32, (ns, LANE), 0)
        c_r = jnp.zeros((1, LANE), F32)
        c_i = jnp.zeros((1, LANE), F32)
        in_r, in_i = zero, zero
        order = range(ns - 2, -1, -1) if reverse else range(1, ns)
        for kk in order:
            src = kk + 1 if reverse else kk - 1
            ek_r = jnp.sum(jnp.where(rid == src, e_r, 0.0), axis=0, keepdims=True)
            ek_i = jnp.sum(jnp.where(rid == src, e_i, 0.0), axis=0, keepdims=True)
            q_r, q_i = _cmul(p_r, p_i, c_r, c_i)
            c_r, c_i = q_r + ek_r, q_i + ek_i
            in_r = jnp.where(rid == kk, jnp.broadcast_to(c_r, (ns, LANE)), in_r)
            in_i = jnp.where(rid == kk, jnp.broadcast_to(c_i, (ns, LANE)), in_i)

        def final(t, carry):
            cr, ci = carry
            r = rows(t)
            pr, pi = _cmul(a_r, a_i, cr, ci)
            nr, ni = pr + xr_ref[r, :], pi + xi_ref[r, :]
            sr_ref[r, :] = nr
            si_ref[r, :] = ni
            return nr, ni

        lax.fori_loop(0, tl, final, (in_r, in_i))

    xs = pl.BlockSpec((l, LANE), lambda j: (0, j))
    as_ = pl.BlockSpec((1, LANE), lambda j: (0, j))
    return pl.pallas_call(
        body, name=name, grid=(lanes // LANE,), in_specs=[xs, xs, as_, as_], out_specs=(xs, xs),
        out_shape=(_sds((l, lanes)), _sds((l, lanes))),
        compiler_params=_params(("arbitrary",), 48))(x_re, x_im, a_re, a_im)


ANY = pl.BlockSpec(memory_space=pl.ANY)


def _place():
    mx, my, mc = lax.axis_index("x"), lax.axis_index("y"), lax.axis_index("c")
    return mx, my, mc, [(1 - mx, my), (mx, 1 - my), (1 - mx, 1 - my)]


def _run_copies(copies):
    for cp in copies:
        cp.start()
    for cp in copies:
        cp.wait_recv()
    for cp in copies:
        cp.wait_send()


def _remote(src, dst, sems, k, dev):
    return pltpu.make_async_remote_copy(src_ref=src, dst_ref=dst, send_sem=sems[0].at[k], recv_sem=sems[1].at[k],
                                        device_id=dev, device_id_type=MESH)


def _copy_call(body, name, ins, outs, n_copies, aliases=None):
    return pl.pallas_call(
        body, name=name, in_specs=[ANY] * len(ins), out_specs=[ANY] * len(outs), out_shape=list(outs),
        input_output_aliases=aliases or {},
        scratch_shapes=[pltpu.SemaphoreType.DMA((n_copies,)), pltpu.SemaphoreType.DMA((n_copies,))])(*ins)


def _gather_ici(xs):
    n = len(xs)

    def body(*refs):
        x_refs, y_refs, sems = refs[:n], refs[n:2 * n], refs[2 * n:]
        mx, my, mc, peers = _place()
        me = 2 * mx + my
        _run_copies([_remote(x_refs[i].at[mc], y_refs[i].at[me, mc], sems, 3 * i + j, (px, py, mc))
                     for i in range(n) for j, (px, py) in enumerate(peers)])

    return _copy_call(body, "comm_gather_ici", xs, [_sds((4,) + x.shape, x.dtype) for x in xs], 3 * n)


def _gather_d2d(ys):
    n = len(ys)

    def body(*refs):
        y_in, y_out, sems = refs[:n], refs[n:2 * n], refs[2 * n:]
        mx, my, mc, peers = _place()
        _run_copies([_remote(y_in[i].at[2 * px + py, mc], y_out[i].at[2 * px + py, mc], sems, 3 * i + j, (mx, my, 1 - mc))
                     for i in range(n) for j, (px, py) in enumerate(peers)])

    return _copy_call(body, "comm_gather_d2d", ys, [_sds(y.shape, y.dtype) for y in ys], 3 * n,
                      aliases={i: i for i in range(n)})


def _reduce_ici(gs):
    n = len(gs)

    def body(*refs):
        g_refs, y_refs, sems = refs[:n], refs[n:2 * n], refs[2 * n:]
        mx, my, mc, peers = _place()
        _run_copies([_remote(g_refs[i].at[2 * px + py], y_refs[i].at[j], sems, 3 * i + j, (px, py, mc))
                     for i in range(n) for j, (px, py) in enumerate(peers)])

    return _copy_call(body, "comm_reduce_ici", gs, [_sds((3,) + g.shape[1:], g.dtype) for g in gs], 3 * n)


def _swap_d2d(ps):
    n = len(ps)
    depth = ps[0].shape[0]

    def body(*refs):
        p_refs, o_refs, sems = refs[:n], refs[n:2 * n], refs[2 * n:]
        mx, my, mc, _ = _place()
        _run_copies([_remote(p_refs[i].at[l], o_refs[i].at[l], sems, depth * i + l, (mx, my, 1 - mc))
                     for i in range(n) for l in range(depth)])

    return _copy_call(body, "comm_reduce_d2d", ps, [_sds(p.shape, p.dtype) for p in ps], depth * n)


def _all_exchange(name, src):
    def body(x_ref, y_ref, send_sems, recv_sems, local_sem):
        mx, my, mc = lax.axis_index("x"), lax.axis_index("y"), lax.axis_index("c")
        me = 4 * mx + 2 * my + mc
        own = pltpu.make_async_copy(x_ref, y_ref.at[me], local_sem)
        own.start()
        copies = []
        for j in range(1, 8):
            px = (1 - mx) if (j & 4) else mx
            py = (1 - my) if (j & 2) else my
            pc = (1 - mc) if (j & 1) else mc
            cp = pltpu.make_async_remote_copy(
                src_ref=x_ref, dst_ref=y_ref.at[me], send_sem=send_sems.at[j - 1], recv_sem=recv_sems.at[j - 1],
                device_id=(px, py, pc), device_id_type=MESH)
            cp.start()
            copies.append(cp)
        for cp in copies:
            cp.wait_recv()
        for cp in copies:
            cp.wait_send()
        own.wait()

    return pl.pallas_call(
        body, name=name, in_specs=[ANY], out_specs=ANY, out_shape=_sds((8,) + src.shape, src.dtype),
        scratch_shapes=[pltpu.SemaphoreType.DMA((7,)), pltpu.SemaphoreType.DMA((7,)), pltpu.SemaphoreType.DMA])(src)


PACK_W = 1024


def _pack(arrs, rows_multiple, dtype):
    flat = jnp.concatenate([a.reshape(-1).astype(dtype) for a in arrs])
    n = flat.shape[0]
    unit = PACK_W * rows_multiple
    tot = -(-n // unit) * unit
    flat = jnp.pad(flat, (0, tot - n))
    return flat.reshape(tot // PACK_W, PACK_W)


def _unpack(flat, shapes):
    flat = flat.reshape(-1)
    out = []
    off = 0
    for s in shapes:
        n = int(np.prod(s))
        out.append(flat[off:off + n].reshape(s))
        off += n
    return out


def _rope_tables(pos):
    l = pos.shape[0]
    tm = min(512, l)
    inv = (np.float32(ROPE_THETA) ** (-np.arange(0, D_ROPE, 2, dtype=np.float32) / np.float32(D_ROPE))).astype(np.float32)
    lane_f = np.zeros((1, LANE), np.float32)
    lane_f[0, D_NOPE:D_NOPE + 16] = inv
    lane_f[0, D_NOPE + 16:D_NOPE + 32] = inv

    def fn(i, p, f):
        ang = p * f
        lane = lax.broadcasted_iota(jnp.int32, ang.shape, 1)
        co = jnp.cos(ang)
        si = jnp.sin(ang)
        c = jnp.where(lane < D_NOPE, 1.0, jnp.where(lane < D_QK, co, 0.0))
        s1 = jnp.where((lane >= D_NOPE) & (lane < D_NOPE + 16), -si, 0.0)
        s2 = jnp.where((lane >= D_NOPE + 16) & (lane < D_QK), si, 0.0)
        return c, s1, s2

    return _rows(fn, name="rope_tables", n=l // tm, ins=[pos, jnp.asarray(lane_f)],
                 in_specs=[_rt(tm, 1), _full((1, LANE))], outs=[_sds((l, LANE))] * 3, out_specs=[_rt(tm, LANE)] * 3)


def _ssm_param_fn(lr, li, log_dt, br, bi):
    dt = jnp.exp(log_dt)
    mag = jnp.exp(lr * dt)
    a_re = mag * jnp.cos(li * dt)
    a_im = mag * jnp.sin(li * dt)
    den = lr * lr + li * li
    e_re = a_re - 1.0
    e_im = a_im
    f_re = (e_re * lr + e_im * li) / den
    f_im = (e_im * lr - e_re * li) / den
    bb_re = f_re[None] * br - f_im[None] * bi
    bb_im = f_re[None] * bi + f_im[None] * br
    return a_re, a_im, bb_re, bb_im


def _ssm_params(name, lr, li, log_dt, br, bi):
    g, n = lr.shape
    c = br.shape[0]
    return _rows(lambda i, *v: _ssm_param_fn(*v), name=name, n=1, ins=[lr, li, log_dt, br, bi],
                 in_specs=[_full((g, n)), _full((g, n)), _full((g, 1)), _full((c, g, n)), _full((c, g, n))],
                 outs=[_sds((g, n)), _sds((g, n)), _sds((c, g, n)), _sds((c, g, n))],
                 out_specs=[_full((g, n)), _full((g, n)), _full((c, g, n)), _full((c, g, n))])


def _ssm_params_bwd(name, lr, li, log_dt, br, bi, d_are, d_aim, d_bbre, d_bbim):
    g, n = lr.shape
    c = br.shape[0]

    def fn(i, lr, li, log_dt, br, bi, g0, g1, g2, g3):
        _, vjp = jax.vjp(_ssm_param_fn, lr, li, log_dt, br, bi)
        return vjp((g0, g1, g2, g3))

    sp = [_full((g, n)), _full((g, n)), _full((g, 1)), _full((c, g, n)), _full((c, g, n))]
    return _rows(fn, name=name, n=1, ins=[lr, li, log_dt, br, bi, d_are, d_aim, d_bbre, d_bbim],
                 in_specs=sp + [_full((g, n)), _full((g, n)), _full((c, g, n)), _full((c, g, n))],
                 outs=[_sds((g, n)), _sds((g, n)), _sds((g, 1)), _sds((c, g, n)), _sds((c, g, n))], out_specs=sp)


_EYE8 = np.eye(8, dtype=np.float32)


def _blockdiag(v):
    j, g, p, q = v.shape
    m = v[:, :, :, None, :] * jnp.asarray(_EYE8)[None, :, None, :, None]
    return m.reshape(j, g * p, g * q)


def _blockdiag_t(m, p, q):
    j = m.shape[0]
    m = m.reshape(j, 8, p, 8, q)
    return jnp.sum(m * jnp.asarray(_EYE8)[None, :, None, :, None], axis=3)


def _to_perm(v, l):
    ns = SCAN_SEGS
    return v.reshape(ns, l // ns, v.shape[-1]).transpose(1, 0, 2).reshape(l, v.shape[-1])


def _from_perm(v, l):
    ns = SCAN_SEGS
    return v.reshape(l // ns, ns, v.shape[-1]).transpose(1, 0, 2).reshape(l, v.shape[-1])


def _prep_layer(w, i):
    p = {}
    w_in = w['w_in'][i]
    z = lambda n: jnp.zeros((D_MODEL, n), w_in.dtype)
    o = Q_LORA + KV_LORA
    p['w_s'] = jnp.concatenate([w_in[:, :o], z(D_NOPE), w_in[:, o:o + D_ROPE], z(HEAD_PAD - D_QK)], axis=1)
    o += D_ROPE
    p['w_u'] = w_in[:, o:o + SSM_WIDTH]
    o += SSM_WIDTH
    p['w_xq'] = w_in[:, o:o + X_WIDTH]
    o += X_WIDTH
    p['w_g'] = w_in[:, o:]
    wq = w['w_q_b'][i].reshape(Q_LORA, MLA_HEADS, D_QK)
    p['w_qb'] = jnp.pad(wq, ((0, 0), (0, 0), (0, HEAD_PAD - D_QK))).reshape(Q_LORA, MLA_PAD)
    wkv = w['w_kv_b'][i].reshape(KV_LORA, MLA_HEADS, D_NOPE + D_V)
    p['w_k'] = jnp.pad(wkv[:, :, :D_NOPE], ((0, 0), (0, 0), (0, HEAD_PAD - D_NOPE))).reshape(KV_LORA, MLA_PAD)
    p['w_v'] = jnp.pad(wkv[:, :, D_NOPE:], ((0, 0), (0, 0), (0, HEAD_PAD - D_V))).reshape(KV_LORA, MLA_PAD)
    wo = w['w_o_mla'][i].reshape(MLA_HEADS, D_V, D_MODEL)
    p['w_oa'] = jnp.pad(wo, ((0, 0), (0, HEAD_PAD - D_V), (0, 0))).reshape(MLA_PAD, D_MODEL)
    for n in ('w_glu', 'w_o_ssm', 'w_mem_kv', 'w_o_cross', 'w_out', 'w_up', 'w_down'):
        p[n] = w[n][i]
    p['conv_w'] = w['conv_w'][i]
    for n in ('norm_mix_g', 'q_a_norm_g', 'kv_a_norm_g', 'b_glu', 'mem_norm_g', 'xq_norm_g', 'xk_norm_g', 'b_gate',
              'norm_ffn_g', 'conv_b'):
        p[n] = w[n][i].reshape(1, -1)
    p['q_norm_g'] = jnp.pad(w['q_norm_g'][i], (0, HEAD_PAD - D_QK)).reshape(1, HEAD_PAD)
    p['k_norm_g'] = jnp.pad(w['k_norm_g'][i], (0, HEAD_PAD - D_QK)).reshape(1, HEAD_PAD)
    p['ssm_d'] = w['ssm_d'][i].reshape(1, SSM_WIDTH)
    p['lr'] = w['ssm_lambda_re'][i]
    p['li'] = w['ssm_lambda_im'][i]
    p['log_dt'] = w['ssm_log_dt'][i].reshape(SSM_GROUPS, 1)
    p['br'] = w['ssm_b_re'][i].transpose(2, 0, 1)
    p['bi'] = w['ssm_b_im'][i].transpose(2, 0, 1)
    cr = w['ssm_c_re'][i].reshape(SSM_JB, 8, SSM_GROUP_CH, SSM_STATE).transpose(0, 1, 3, 2)
    ci = w['ssm_c_im'][i].reshape(SSM_JB, 8, SSM_GROUP_CH, SSM_STATE).transpose(0, 1, 3, 2)
    p['c_mat'] = jnp.concatenate([_blockdiag(cr), -_blockdiag(ci)], axis=1).astype(BF16)
    return p


def _b_mat(bb_re, bb_im):
    r = bb_re.transpose(1, 0, 2).reshape(SSM_JB, 8, SSM_GROUP_CH, SSM_STATE)
    i = bb_im.transpose(1, 0, 2).reshape(SSM_JB, 8, SSM_GROUP_CH, SSM_STATE)
    return jnp.concatenate([_blockdiag(r), _blockdiag(i)], axis=2).astype(BF16)


def _qkv_fn(ps, c, s1, s2, qag, wqb, kvag, wk, wv, qng, kng):
    c_q = ps[:, :Q_LORA]
    c_kv = ps[:, Q_LORA:Q_LORA + KV_LORA]
    kr = ps[:, Q_LORA + KV_LORA:]
    cqn = _rms(c_q, qag, Q_LORA)
    ckvn = _rms(c_kv, kvag, KV_LORA)
    q_raw = _mm(cqn, wqb)
    k_raw = _mm(ckvn, wk) + jnp.concatenate([kr] * MLA_HEADS, axis=-1)
    v = _mm(ckvn, wv)
    q = _heads(_rope, _head_rms(q_raw, qng, MLA_HEADS, D_QK), MLA_HEADS, c, s1, s2)
    k = _heads(_rope, _head_rms(k_raw, kng, MLA_HEADS, D_QK), MLA_HEADS, c, s1, s2)
    lane = lax.broadcasted_iota(jnp.int32, v.shape, 1)
    v = jnp.where((lane & (LANE - 1)) == D_V, 1.0, v)
    return q * ATT_QSCALE, k, v


def _layer_fwd(name, x, tabs, mem, p):
    l = x.shape[0]
    tm = min(512, l)
    nt = l // tm
    sv = {'x0': x}
    sv['p_g'] = _matmul(name + "_in_g", [(x, p['w_g'])], l, 3 * D_MODEL, rms_gain=p['norm_mix_g'])
    sv['p_u'] = _matmul(name + "_in_u", [(x, p['w_u'])], l, SSM_WIDTH, rms_gain=p['norm_mix_g'])
    sv['p_xq'] = _matmul(name + "_in_xq", [(x, p['w_xq'])], l, X_WIDTH, rms_gain=p['norm_mix_g'])
    sv['p_s'] = _matmul(name + "_in_s", [(x, p['w_s'])], l, SMALL_W, rms_gain=p['norm_mix_g'])

    qkv_consts = [p['q_a_norm_g'], p['w_qb'], p['kv_a_norm_g'], p['w_k'], p['w_v'], p['q_norm_g'], p['k_norm_g']]
    qkv_cspecs = [_full(a.shape) for a in qkv_consts]
    def qkv_fwd(i, *a):
        qv, kv, vv = _qkv_fn(*a)
        return qv, kv, vv, jnp.transpose(vv)

    q, k, v, v_t = _rows(qkv_fwd, name=name + "_qkv", n=nt, ins=[sv['p_s'], *tabs, *qkv_consts],
                         in_specs=[_rt(tm, SMALL_W)] + [_rt(tm, LANE)] * 3 + qkv_cspecs,
                         outs=[_sds((l, MLA_PAD), BF16)] * 3 + [_sds((MLA_PAD, l), BF16)],
                         out_specs=[_rt(tm, MLA_PAD)] * 3 + [pl.BlockSpec((MLA_PAD, tm), lambda i: (0, i))])
    sv['q'], sv['k'], sv['v'] = q, k, v
    sv['o_a'], sv['lse_t'] = _flash_fwd(name + "_attn", q, k, v_t)

    a_re, a_im, bb_re, bb_im = _ssm_params(name + "_ssm_par", p['lr'], p['li'], p['log_dt'], p['br'], p['bi'])
    sv['a_re'], sv['a_im'] = a_re.reshape(1, SSM_LANES), a_im.reshape(1, SSM_LANES)
    sv['b_mat'] = _b_mat(bb_re, bb_im)
    u_p = _to_perm(sv['p_u'], l)
    sv['u_p'] = u_p

    def bu_fn(i, u, bm):
        res = [_mm(u[:, j * LANE:(j + 1) * LANE], bm[j]) for j in range(SSM_JB)]
        return (jnp.concatenate([r[:, :512] for r in res], axis=-1), jnp.concatenate([r[:, 512:] for r in res], axis=-1))

    ts = min(256, l)
    bu_re, bu_im = _rows(bu_fn, name=name + "_ssm_bu", n=l // ts, ins=[u_p, sv['b_mat']],
                         in_specs=[_rt(ts, SSM_WIDTH), _full(sv['b_mat'].shape)],
                         outs=[_sds((l, SSM_LANES))] * 2, out_specs=[_rt(ts, SSM_LANES)] * 2)
    s_re, s_im = _scan(name + "_ssm_scan", bu_re, bu_im, sv['a_re'], sv['a_im'], reverse=False)
    sv['s_re'], sv['s_im'] = s_re, s_im

    def glu_fn(i, sr, si, u, cm, dsk, wg, bg):
        y = jnp.concatenate([_mm(jnp.concatenate([sr[:, j * 512:(j + 1) * 512], si[:, j * 512:(j + 1) * 512]], axis=-1),
                                 cm[j]) for j in range(SSM_JB)], axis=-1) + dsk * u
        zz = _gelu(y)
        return zz * jax.nn.sigmoid(_mm(zz, wg) + bg)

    glu_consts = [p['c_mat'], p['ssm_d'], p['w_glu'], p['b_glu']]
    zo_p = _rows(glu_fn, name=name + "_ssm_glu", n=l // ts, ins=[s_re, s_im, u_p, *glu_consts],
                 in_specs=[_rt(ts, SSM_LANES), _rt(ts, SSM_LANES), _rt(ts, SSM_WIDTH)] + [_full(a.shape) for a in glu_consts],
                 outs=[_sds((l, SSM_WIDTH), BF16)], out_specs=[_rt(ts, SSM_WIDTH)])[0]
    sv['zo'] = _from_perm(zo_p, l)

    m_len = mem.shape[0]

    def memkv_fn(i, mm_, mg, wmk, xkg):
        kv = _mm(_rms(mm_, mg, D_MODEL), wmk)
        return _head_rms(kv[:, :X_WIDTH], xkg, X_HEADS, X_HEAD_DIM), kv[:, X_WIDTH:]

    mem_consts = [p['mem_norm_g'], p['w_mem_kv'], p['xk_norm_g']]
    k_c, v_c = _rows(memkv_fn, name=name + "_memkv", n=1, ins=[mem, *mem_consts],
                     in_specs=[_full(mem.shape)] + [_full(a.shape) for a in mem_consts],
                     outs=[_sds((m_len, X_WIDTH))] * 2, out_specs=[_full((m_len, X_WIDTH))] * 2)
    sv['k_c'], sv['v_c'] = k_c, v_c

    def cross_fn(i, xq, kc, vc, xqg):
        outs = []
        for h in range(X_HEADS):
            sl = slice(h * LANE, (h + 1) * LANE)
            qh = _rms(xq[:, sl], xqg, X_HEAD_DIM)
            s = _mm_nt(qh, kc[:, sl]) * (X_HEAD_DIM ** -0.5)
            s = s - jnp.max(s, axis=-1, keepdims=True)
            e = jnp.exp(s)
            pr = e / jnp.sum(e, axis=-1, keepdims=True)
            outs.append(_mm(pr, vc[:, sl]))
        return jnp.concatenate(outs, axis=-1)

    sv['o_c'] = _rows(cross_fn, name=name + "_cross", n=nt, ins=[sv['p_xq'], k_c, v_c, p['xq_norm_g']],
                      in_specs=[_rt(tm, X_WIDTH), _full(k_c.shape), _full(v_c.shape), _full((1, LANE))],
                      outs=[_sds((l, X_WIDTH), BF16)], out_specs=[_rt(tm, X_WIDTH)])[0]

    def merge_fn(i, oa, zo, oc, pg, x0, woa, wos, woc, bg, wout):
        gates = jax.nn.sigmoid(pg + bg)
        merged = (gates[:, :D_MODEL] * _mm(oa, woa) + gates[:, D_MODEL:2 * D_MODEL] * _mm(zo, wos)
                  + gates[:, 2 * D_MODEL:] * _mm(oc, woc))
        return x0 + _mm(merged, wout), merged

    merge_consts = [p['w_oa'], p['w_o_ssm'], p['w_o_cross'], p['b_gate'], p['w_out']]
    tg = min(256, l)
    x1, merged = _rows(merge_fn, name=name + "_merge", n=l // tg, ins=[sv['o_a'], sv['zo'], sv['o_c'], sv['p_g'], x, *merge_consts],
                       in_specs=[_rt(tg, MLA_PAD), _rt(tg, SSM_WIDTH), _rt(tg, X_WIDTH), _rt(tg, 3 * D_MODEL), _rt(tg, D_MODEL)]
                       + [_full(a.shape) for a in merge_consts],
                       outs=[_sds((l, D_MODEL)), _sds((l, D_MODEL), BF16)], out_specs=[_rt(tg, D_MODEL)] * 2)
    sv['x1'], sv['merged'] = x1, merged

    up = _matmul(name + "_up", [(x1, p['w_up'])], l, 2 * D_FF, rms_gain=p['norm_ffn_g'])
    sv['up'] = up
    tc = min(128, l)

    def conv_fn(i, upt, halo, cw, cb):
        upc = _conv(i, upt, halo, cw) + cb
        return _silu(upc[:, :D_FF]) * upc[:, D_FF:]

    act = _rows(conv_fn, name=name + "_conv", n=l // tc, ins=[up, up, p['conv_w'], p['conv_b']],
                in_specs=[_rt(tc, 2 * D_FF), _halo_prev(tc, 2 * D_FF), _full((3, 2 * D_FF)), _full((1, 2 * D_FF))],
                outs=[_sds((l, D_FF), BF16)], out_specs=[_rt(tc, D_FF)])[0]
    sv['act'] = act
    x2 = _matmul(name + "_down", [(act, p['w_down'])], l, D_MODEL, resid=x1)
    return x2, sv


def _halo_prev(tm, w):
    return pl.BlockSpec((8, w), lambda i: (jnp.maximum(i * (tm // 8) - 1, 0), 0))


def _halo_next(tm, w, n_tiles):
    last = n_tiles * (tm // 8) - 1
    return pl.BlockSpec((8, w), lambda i: (jnp.minimum((i + 1) * (tm // 8), last), 0))


def _conv(i, tile, halo, cw):
    halo = jnp.where(i > 0, halo, 0.0)
    ext = jnp.concatenate([halo, tile], axis=0)
    n = ext.shape[0]
    x1 = pltpu.roll(ext, 1, 0)[8:]
    x2 = pltpu.roll(ext, 2, 0)[8:]
    del n
    return cw[0:1] * x2 + cw[1:2] * x1 + cw[2:3] * tile


def _layer_bwd(name, dx2, sv, tabs, mem, p):
    l = dx2.shape[0]
    tm = min(512, l)
    nt = l // tm
    g = {}
    x1 = sv['x1']
    dact = _matmul(name + "_b_down", [(dx2, p['w_down'])], l, D_FF, nt=True)
    g['w_down'] = _matmul_tn(name + "_gw_down", sv['act'], dx2)
    tc = min(128, l)
    ntc = l // tc

    def conv_b(i, upt, up_prev, up_next, da, da_next, cw, cb):
        up_prev = jnp.where(i > 0, up_prev, 0.0)
        da_next = jnp.where(i < ntc - 1, da_next, 0.0)
        ext = jnp.concatenate([up_prev, upt, up_next], axis=0)
        x0 = ext[8:]
        xm1 = pltpu.roll(ext, 1, 0)[8:]
        xm2 = pltpu.roll(ext, 2, 0)[8:]
        upc = cw[0:1] * xm2 + cw[1:2] * xm1 + cw[2:3] * x0 + cb
        _, vjp = jax.vjp(lambda a, b: _silu(a) * b, upc[:, :D_FF], upc[:, D_FF:])
        dg, dv = vjp(jnp.concatenate([da, da_next], axis=0))
        dupc = jnp.concatenate([dg, dv], axis=-1)
        n = dupc.shape[0]
        dup = cw[2:3] * dupc[:tc] + cw[1:2] * pltpu.roll(dupc, n - 1, 0)[:tc] + cw[0:1] * pltpu.roll(dupc, n - 2, 0)[:tc]
        dt = dupc[:tc]
        dcw = _row_select([_colsum(dt * xm2[:tc]), _colsum(dt * xm1[:tc]), _colsum(dt * upt)], 8)
        return dup, dcw, _colsum(dt)

    dup, g_cw, g_cb = _rows(
        conv_b, name=name + "_b_conv", n=ntc, ins=[sv['up'], sv['up'], sv['up'], dact, dact, p['conv_w'], p['conv_b']],
        in_specs=[_rt(tc, 2 * D_FF), _halo_prev(tc, 2 * D_FF), _halo_next(tc, 2 * D_FF, ntc), _rt(tc, D_FF),
                  _halo_next(tc, D_FF, ntc), _full((3, 2 * D_FF)), _full((1, 2 * D_FF))],
        outs=[_sds((l, 2 * D_FF)), _sds((8, 2 * D_FF)), _sds((1, 2 * D_FF))],
        out_specs=[_rt(tc, 2 * D_FF), _full((8, 2 * D_FF)), _full((1, 2 * D_FF))], n_acc=2, vmem=56)
    g['conv_w'] = g_cw[:3]
    g['conv_b'] = g_cb
    dh2 = _matmul(name + "_b_up", [(dup, p['w_up'])], l, D_MODEL, nt=True, tm=256)
    g['w_up'] = _matmul_tn(name + "_gw_up", x1, dup, rms_gain=p['norm_ffn_g'])

    def norm_b(i, xv, dh, dres, gn):
        _, vjp = jax.vjp(lambda a, b: _rms(a, b, D_MODEL), xv, gn)
        dxv, dgn = vjp(dh)
        return dres + dxv, dgn

    dx1, g['norm_ffn_g'] = _rows(norm_b, name=name + "_b_norm2", n=nt, ins=[x1, dh2, dx2, p['norm_ffn_g']],
                                 in_specs=[_rt(tm, D_MODEL)] * 3 + [_full((1, D_MODEL))],
                                 outs=[_sds((l, D_MODEL)), _sds((1, D_MODEL))], out_specs=[_rt(tm, D_MODEL), _full((1, D_MODEL))],
                                 n_acc=1)

    tg = min(256, l)

    def merge_b(i, dx, oa, zo, oc, pg, woa, wos, woc, bg, wout):
        dm = _mm_nt(dx, wout)
        gates = jax.nn.sigmoid(pg + bg)
        ys = [_mm(oa, woa), _mm(zo, wos), _mm(oc, woc)]
        dys, dpg = [], []
        for b in range(3):
            gb = gates[:, b * D_MODEL:(b + 1) * D_MODEL]
            dys.append(dm * gb)
            dpg.append(dm * ys[b] * gb * (1.0 - gb))
        dpg = jnp.concatenate(dpg, axis=-1)
        return (_mm_nt(dys[0], woa), _mm_nt(dys[1], wos), _mm_nt(dys[2], woc), dpg, dys[0], dys[1], dys[2], _colsum(dpg))

    merge_consts = [p['w_oa'], p['w_o_ssm'], p['w_o_cross'], p['b_gate'], p['w_out']]
    (do_a, dzo, do_c, dp_g, dy_a, dy_b, dy_c, g['b_gate']) = _rows(
        merge_b, name=name + "_b_merge", n=l // tg, ins=[dx1, sv['o_a'], sv['zo'], sv['o_c'], sv['p_g'], *merge_consts],
        in_specs=[_rt(tg, D_MODEL), _rt(tg, MLA_PAD), _rt(tg, SSM_WIDTH), _rt(tg, X_WIDTH), _rt(tg, 3 * D_MODEL)]
        + [_full(a.shape) for a in merge_consts],
        outs=[_sds((l, MLA_PAD)), _sds((l, SSM_WIDTH)), _sds((l, X_WIDTH)), _sds((l, 3 * D_MODEL)),
              _sds((l, D_MODEL), BF16), _sds((l, D_MODEL), BF16), _sds((l, D_MODEL), BF16), _sds((1, 3 * D_MODEL))],
        out_specs=[_rt(tg, MLA_PAD), _rt(tg, SSM_WIDTH), _rt(tg, X_WIDTH), _rt(tg, 3 * D_MODEL),
                   _rt(tg, D_MODEL), _rt(tg, D_MODEL), _rt(tg, D_MODEL), _full((1, 3 * D_MODEL))], n_acc=1, vmem=56)
    g['w_out'] = _matmul_tn(name + "_gw_out", sv['merged'], dx1)
    g['w_oa'] = _matmul_tn(name + "_gw_oa", sv['o_a'], dy_a)
    g['w_o_ssm'] = _matmul_tn(name + "_gw_os", sv['zo'], dy_b)
    g['w_o_cross'] = _matmul_tn(name + "_gw_oc", sv['o_c'], dy_c)

    k_c, v_c = sv['k_c'], sv['v_c']
    m_len = k_c.shape[0]

    def cross_b(i, xq, do, kc, vc, xqg):
        dxq, dk, dv = [], [], []
        dg = jnp.zeros((1, LANE), F32)
        for h in range(X_HEADS):
            sl = slice(h * LANE, (h + 1) * LANE)
            qh, vjp = jax.vjp(lambda a, b: _rms(a, b, X_HEAD_DIM), xq[:, sl], xqg)
            sc = X_HEAD_DIM ** -0.5
            s = _mm_nt(qh, kc[:, sl]) * sc
            s = s - jnp.max(s, axis=-1, keepdims=True)
            e = jnp.exp(s)
            pr = e / jnp.sum(e, axis=-1, keepdims=True)
            doh = do[:, sl]
            dv.append(_mm_tn(pr, doh))
            dp = _mm_nt(doh, vc[:, sl])
            ds = pr * (dp - jnp.sum(dp * pr, axis=-1, keepdims=True)) * sc
            dk.append(_mm_tn(ds, qh))
            dxh, dgh = vjp(_mm(ds, kc[:, sl]))
            dxq.append(dxh)
            dg = dg + dgh
        return jnp.concatenate(dxq, axis=-1), jnp.concatenate(dk, axis=-1), jnp.concatenate(dv, axis=-1), dg

    dp_xq, dk_c, dv_c, g['xq_norm_g'] = _rows(
        cross_b, name=name + "_b_cross", n=nt, ins=[sv['p_xq'], do_c, k_c, v_c, p['xq_norm_g']],
        in_specs=[_rt(tm, X_WIDTH), _rt(tm, X_WIDTH), _full(k_c.shape), _full(v_c.shape), _full((1, LANE))],
        outs=[_sds((l, X_WIDTH)), _sds((m_len, X_WIDTH)), _sds((m_len, X_WIDTH)), _sds((1, LANE))],
        out_specs=[_rt(tm, X_WIDTH), _full((m_len, X_WIDTH)), _full((m_len, X_WIDTH)), _full((1, LANE))], n_acc=3)

    def memkv_b(i, mm_, dk, dv, mg, wmk, xkg):
        memn, vjp_n = jax.vjp(lambda a, b: _rms(a, b, D_MODEL), mm_, mg)
        kv = _mm(memn, wmk)
        _, vjp_k = jax.vjp(lambda a, b: _head_rms(a, b, X_HEADS, X_HEAD_DIM), kv[:, :X_WIDTH], xkg)
        dkr, dxkg = vjp_k(dk)
        dkv = jnp.concatenate([dkr, dv], axis=-1)
        _, dmg = vjp_n(_mm_nt(dkv, wmk))
        return _mm_tn(memn, dkv), dmg, dxkg

    mem_consts = [p['mem_norm_g'], p['w_mem_kv'], p['xk_norm_g']]
    g['w_mem_kv'], g['mem_norm_g'], g['xk_norm_g'] = _rows(
        memkv_b, name=name + "_b_memkv", n=1, ins=[mem, dk_c, dv_c, *mem_consts],
        in_specs=[_full(mem.shape), _full(dk_c.shape), _full(dv_c.shape)] + [_full(a.shape) for a in mem_consts],
        outs=[_sds((D_MODEL, 2 * X_WIDTH)), _sds((1, D_MODEL)), _sds((1, LANE))],
        out_specs=[_full((D_MODEL, 2 * X_WIDTH)), _full((1, D_MODEL)), _full((1, LANE))])

    u_p = sv['u_p']
    dzo_p = _to_perm(dzo, l)
    s_re, s_im = sv['s_re'], sv['s_im']

    def glu_b(i, sr, si, u, dz, cm, dsk, wg, bg):
        cats = [jnp.concatenate([sr[:, j * 512:(j + 1) * 512], si[:, j * 512:(j + 1) * 512]], axis=-1) for j in range(SSM_JB)]
        y = jnp.concatenate([_mm(cats[j], cm[j]) for j in range(SSM_JB)], axis=-1) + dsk * u
        zz, vjp_g = jax.vjp(_gelu, y)
        t = _mm(zz, wg) + bg
        sg = jax.nn.sigmoid(t)
        dt = dz * zz * sg * (1.0 - sg)
        dzz = dz * sg + _mm_nt(dt, wg)
        dy = vjp_g(dzz)[0]
        dss = [_mm_nt(dy[:, j * LANE:(j + 1) * LANE], cm[j]) for j in range(SSM_JB)]
        dsr = jnp.concatenate([d[:, :512] for d in dss], axis=-1)
        dsi = jnp.concatenate([d[:, 512:] for d in dss], axis=-1)
        dcm = jnp.stack([_mm_tn(cats[j], dy[:, j * LANE:(j + 1) * LANE]) for j in range(SSM_JB)], axis=0)
        return dsr, dsi, dy * dsk, dcm, _colsum(dy * u), _mm_tn(zz, dt), _colsum(dt)

    glu_consts = [p['c_mat'], p['ssm_d'], p['w_glu'], p['b_glu']]
    ts = min(256, l)
    nts = l // ts
    ds_re, ds_im, du_dir, g['c_mat'], g['ssm_d'], g['w_glu'], g['b_glu'] = _rows(
        glu_b, name=name + "_b_glu", n=nts, ins=[s_re, s_im, u_p, dzo_p, *glu_consts],
        in_specs=[_rt(ts, SSM_LANES), _rt(ts, SSM_LANES), _rt(ts, SSM_WIDTH), _rt(ts, SSM_WIDTH)] + [_full(a.shape) for a in glu_consts],
        outs=[_sds((l, SSM_LANES)), _sds((l, SSM_LANES)), _sds((l, SSM_WIDTH)), _sds((SSM_JB, 1024, LANE)), _sds((1, SSM_WIDTH)),
              _sds((SSM_WIDTH, SSM_WIDTH)), _sds((1, SSM_WIDTH))],
        out_specs=[_rt(ts, SSM_LANES), _rt(ts, SSM_LANES), _rt(ts, SSM_WIDTH), _full((SSM_JB, 1024, LANE)), _full((1, SSM_WIDTH)),
                   _full((SSM_WIDTH, SSM_WIDTH)), _full((1, SSM_WIDTH))], n_acc=4)
    gb_re, gb_im = _scan(name + "_b_scan", ds_re, ds_im, sv['a_re'], -sv['a_im'], reverse=True)
    ns = SCAN_SEGS
    last_blk = l // ns - 1

    def da_fn(i, gr, gi, sr, si, hr, hi, lr_, li_):
        rid = lax.broadcasted_iota(jnp.int32, lr_.shape, 0)
        fr = jnp.where(rid == 0, 0.0, pltpu.roll(lr_, 1, 0))
        fi = jnp.where(rid == 0, 0.0, pltpu.roll(li_, 1, 0))
        hr = jnp.where(i == 0, fr, hr)
        hi = jnp.where(i == 0, fi, hi)
        if ts > ns:
            pr = jnp.concatenate([hr, sr[:ts - ns]], axis=0)
            pi = jnp.concatenate([hi, si[:ts - ns]], axis=0)
        else:
            pr, pi = hr, hi
        return _colsum(gr * pr + gi * pi), _colsum(gi * pr - gr * pi)

    hprev = pl.BlockSpec((ns, SSM_LANES), lambda i: (jnp.maximum(i * (ts // ns) - 1, 0), 0))
    hlast = pl.BlockSpec((ns, SSM_LANES), lambda i: (last_blk, 0))
    da_re, da_im = _rows(da_fn, name=name + "_b_da", n=nts, ins=[gb_re, gb_im, s_re, s_im, s_re, s_im, s_re, s_im],
                         in_specs=[_rt(ts, SSM_LANES)] * 4 + [hprev, hprev, hlast, hlast],
                         outs=[_sds((1, SSM_LANES))] * 2, out_specs=[_full((1, SSM_LANES))] * 2, n_acc=2)

    def bu_b(i, dbr, dbi, u, dud, bm):
        dus, dbm = [], []
        for j in range(SSM_JB):
            cat = jnp.concatenate([dbr[:, j * 512:(j + 1) * 512], dbi[:, j * 512:(j + 1) * 512]], axis=-1)
            dus.append(_mm_nt(cat, bm[j]))
            dbm.append(_mm_tn(u[:, j * LANE:(j + 1) * LANE], cat))
        return dud + jnp.concatenate(dus, axis=-1), jnp.stack(dbm, axis=0)

    du_p, d_bmat = _rows(bu_b, name=name + "_b_bu", n=nts, ins=[gb_re, gb_im, u_p, du_dir, sv['b_mat']],
                         in_specs=[_rt(ts, SSM_LANES), _rt(ts, SSM_LANES), _rt(ts, SSM_WIDTH), _rt(ts, SSM_WIDTH),
                                   _full(sv['b_mat'].shape)],
                         outs=[_sds((l, SSM_WIDTH)), _sds((SSM_JB, LANE, 1024))],
                         out_specs=[_rt(ts, SSM_WIDTH), _full((SSM_JB, LANE, 1024))], n_acc=1)
    dp_u = _from_perm(du_p, l)
    dbb_re = _blockdiag_t(d_bmat[:, :, :512], SSM_GROUP_CH, SSM_STATE).reshape(SSM_GROUPS, SSM_GROUP_CH, SSM_STATE).transpose(1, 0, 2)
    dbb_im = _blockdiag_t(d_bmat[:, :, 512:], SSM_GROUP_CH, SSM_STATE).reshape(SSM_GROUPS, SSM_GROUP_CH, SSM_STATE).transpose(1, 0, 2)
    g['lr'], g['li'], g['log_dt'], g['br'], g['bi'] = _ssm_params_bwd(
        name + "_b_ssm_par", p['lr'], p['li'], p['log_dt'], p['br'], p['bi'],
        da_re.reshape(SSM_GROUPS, SSM_STATE), da_im.reshape(SSM_GROUPS, SSM_STATE), dbb_re, dbb_im)

    dq_t, dk, dv = _flash_bwd(name + "_b_attn", sv['q'], sv['k'], sv['v'], sv['o_a'], sv['lse_t'], do_a)

    def qkv_b(i, ps, c, s1, s2, dq_, dk_, dv_, qag, wqb, kvag, wk, wv, qng, kng):
        c_q = ps[:, :Q_LORA]
        c_kv = ps[:, Q_LORA:Q_LORA + KV_LORA]
        kr = ps[:, Q_LORA + KV_LORA:]
        cqn, vjp_cq = jax.vjp(lambda a, b: _rms(a, b, Q_LORA), c_q, qag)
        ckvn, vjp_ckv = jax.vjp(lambda a, b: _rms(a, b, KV_LORA), c_kv, kvag)
        q_raw = _mm(cqn, wqb)
        k_raw = _mm(ckvn, wk) + jnp.concatenate([kr] * MLA_HEADS, axis=-1)
        _, vjp_qn = jax.vjp(lambda a, b: _head_rms(a, b, MLA_HEADS, D_QK), q_raw, qng)
        _, vjp_kn = jax.vjp(lambda a, b: _head_rms(a, b, MLA_HEADS, D_QK), k_raw, kng)
        dq_raw, dqng = vjp_qn(_heads(_rope_t, jnp.transpose(dq_[0]), MLA_HEADS, c, s1, s2))
        dk_raw, dkng = vjp_kn(_heads(_rope_t, dk_, MLA_HEADS, c, s1, s2))
        dkr = dk_raw[:, :LANE]
        for h in range(1, MLA_HEADS):
            dkr = dkr + dk_raw[:, h * LANE:(h + 1) * LANE]
        dcq, dqag = vjp_cq(_mm_nt(dq_raw, wqb))
        dckv, dkvag = vjp_ckv(_mm_nt(dk_raw, wk) + _mm_nt(dv_, wv))
        dps = jnp.concatenate([dcq, dckv, dkr], axis=-1)
        return (dps, _mm_tn(cqn, dq_raw), _mm_tn(ckvn, dk_raw), _mm_tn(ckvn, dv_), dqag, dkvag, dqng, dkng)

    qkv_consts = [p['q_a_norm_g'], p['w_qb'], p['kv_a_norm_g'], p['w_k'], p['w_v'], p['q_norm_g'], p['k_norm_g']]
    (dp_s, g['w_qb'], g['w_k'], g['w_v'], g['q_a_norm_g'], g['kv_a_norm_g'], g['q_norm_g'], g['k_norm_g']) = _rows(
        qkv_b, name=name + "_b_qkv", n=nt, ins=[sv['p_s'], *tabs, dq_t, dk, dv, *qkv_consts],
        in_specs=[_rt(tm, SMALL_W)] + [_rt(tm, LANE)] * 3 + [pl.BlockSpec((1, MLA_PAD, tm), lambda i: (i, 0, 0))]
        + [_rt(tm, MLA_PAD)] * 2 + [_full(a.shape) for a in qkv_consts],
        outs=[_sds((l, SMALL_W)), _sds((Q_LORA, MLA_PAD)), _sds((KV_LORA, MLA_PAD)), _sds((KV_LORA, MLA_PAD)),
              _sds((1, Q_LORA)), _sds((1, KV_LORA)), _sds((1, LANE)), _sds((1, LANE))],
        out_specs=[_rt(tm, SMALL_W), _full((Q_LORA, MLA_PAD)), _full((KV_LORA, MLA_PAD)), _full((KV_LORA, MLA_PAD)),
                   _full((1, Q_LORA)), _full((1, KV_LORA)), _full((1, LANE)), _full((1, LANE))], n_acc=7)

    x0 = sv['x0']
    dh = _matmul(name + "_b_in", [(dp_g, p['w_g']), (dp_u, p['w_u']), (dp_xq, p['w_xq']), (dp_s, p['w_s'])], l, D_MODEL, nt=True,
                 tm=256)
    gm = p['norm_mix_g']
    g['w_g'] = _matmul_tn(name + "_gw_g", x0, dp_g, rms_gain=gm)
    g['w_u'] = _matmul_tn(name + "_gw_u", x0, dp_u, rms_gain=gm)
    g['w_xq'] = _matmul_tn(name + "_gw_xq", x0, dp_xq, rms_gain=gm)
    g['w_s'] = _matmul_tn(name + "_gw_s", x0, dp_s, rms_gain=gm)
    dx0, g['norm_mix_g'] = _rows(norm_b, name=name + "_b_norm1", n=nt, ins=[x0, dh, dx1, gm],
                                 in_specs=[_rt(tm, D_MODEL)] * 3 + [_full((1, D_MODEL))],
                                 outs=[_sds((l, D_MODEL)), _sds((1, D_MODEL))], out_specs=[_rt(tm, D_MODEL), _full((1, D_MODEL))],
                                 n_acc=1)
    return dx0, g


def _unprep_grads(g):
    o = {}
    ws = g['w_s']
    o['w_in'] = jnp.concatenate([ws[:, :Q_LORA + KV_LORA], ws[:, Q_LORA + KV_LORA + D_NOPE:Q_LORA + KV_LORA + D_QK],
                                 g['w_u'], g['w_xq'], g['w_g']], axis=1)
    o['w_q_b'] = g['w_qb'].reshape(Q_LORA, MLA_HEADS, HEAD_PAD)[:, :, :D_QK].reshape(Q_LORA, MLA_HEADS * D_QK)
    gk = g['w_k'].reshape(KV_LORA, MLA_HEADS, HEAD_PAD)[:, :, :D_NOPE]
    gv = g['w_v'].reshape(KV_LORA, MLA_HEADS, HEAD_PAD)[:, :, :D_V]
    o['w_kv_b'] = jnp.concatenate([gk, gv], axis=2).reshape(KV_LORA, MLA_HEADS * (D_NOPE + D_V))
    o['w_o_mla'] = g['w_oa'].reshape(MLA_HEADS, HEAD_PAD, D_MODEL)[:, :D_V].reshape(MLA_HEADS * D_V, D_MODEL)
    for n in ('w_glu', 'w_o_ssm', 'w_mem_kv', 'w_o_cross', 'w_out', 'w_up', 'w_down', 'conv_w'):
        o[n] = g[n]
    for n in ('norm_mix_g', 'q_a_norm_g', 'kv_a_norm_g', 'b_glu', 'mem_norm_g', 'xq_norm_g', 'xk_norm_g', 'b_gate',
              'norm_ffn_g', 'conv_b'):
        o[n] = g[n].reshape(-1)
    o['q_norm_g'] = g['q_norm_g'].reshape(-1)[:D_QK]
    o['k_norm_g'] = g['k_norm_g'].reshape(-1)[:D_QK]
    o['ssm_d'] = g['ssm_d'].reshape(SSM_GROUPS, SSM_GROUP_CH)
    o['ssm_lambda_re'] = g['lr']
    o['ssm_lambda_im'] = g['li']
    o['ssm_log_dt'] = g['log_dt'].reshape(SSM_GROUPS)
    o['ssm_b_re'] = g['br'].transpose(1, 2, 0)
    o['ssm_b_im'] = g['bi'].transpose(1, 2, 0)
    dc = g['c_mat']
    o['ssm_c_re'] = _blockdiag_t(dc[:, :512], SSM_STATE, SSM_GROUP_CH).transpose(0, 1, 3, 2).reshape(SSM_GROUPS, SSM_GROUP_CH, SSM_STATE)
    o['ssm_c_im'] = -_blockdiag_t(dc[:, 512:], SSM_STATE, SSM_GROUP_CH).transpose(0, 1, 3, 2).reshape(SSM_GROUPS, SSM_GROUP_CH, SSM_STATE)
    return o


def _local_step(x, mem, pos, target, w):
    l = x.shape[0]
    tm = min(512, l)
    tabs = _rope_tables(pos.astype(F32).reshape(l, 1))
    ps = [_prep_layer(w, i) for i in range(DEPTH)]
    saved = []
    h = x
    for i in range(DEPTH):
        h, sv = _layer_fwd("l%d" % i, h, tabs, mem, ps[i])
        saved.append(sv)

    def loss_fn(i, y, t):
        e = y - t
        per_tok = jnp.sum(e * e, axis=-1, keepdims=True) * (1.0 / D_MODEL)
        tot = 0.5 * jnp.sum(per_tok, axis=0, keepdims=True)
        return e * (1.0 / D_MODEL), jnp.broadcast_to(tot, (1, LANE))

    dy, loss = _rows(loss_fn, name="loss", n=l // tm, ins=[h, target], in_specs=[_rt(tm, D_MODEL)] * 2,
                     outs=[_sds((l, D_MODEL)), _sds((1, LANE))], out_specs=[_rt(tm, D_MODEL), _full((1, LANE))], n_acc=1)
    grads = []
    d = dy
    for i in reversed(range(DEPTH)):
        d, g = _layer_bwd("l%d" % i, d, saved[i], tabs, mem, ps[i])
        grads.append(_unprep_grads(g))
    return loss[0, 0], d, grads[::-1]


def _sum4(name, own, got):
    r, c = own.shape
    tr = _row_tile(r)

    def fn(i, o, v):
        return ((o.astype(F32) + v[0].astype(F32)) + v[1].astype(F32)) + v[2].astype(F32)

    return _rows(fn, name=name, n=r // tr, ins=[own, got],
                 in_specs=[_rt(tr, c), pl.BlockSpec((3, tr, c), lambda i: (0, i, 0))],
                 outs=[_sds((r, c))], out_specs=[_rt(tr, c)])[0]


def _row_tile(r):
    for t in (256, 128, 64, 32, 16, 8):
        if r % t == 0:
            return t
    return r


def _adamw(name, parts, w, m, v):
    r, cw = w.shape
    tr = _row_tile(r)
    np_ = len(parts)

    def fn(i, *vals):
        wv, mv, vv = vals[np_:]
        terms = []
        for pv in vals[:np_]:
            terms += [pv] if pv.ndim == 2 else [pv[k] for k in range(pv.shape[0])]
        g = terms[0]
        for t in terms[1:]:
            g = g + t
        mn = ADAM_B1 * mv + (1.0 - ADAM_B1) * g
        vn = ADAM_B2 * vv + (1.0 - ADAM_B2) * (g * g)
        m_hat = mn / (1.0 - ADAM_B1 ** ADAM_STEP)
        v_hat = vn / (1.0 - ADAM_B2 ** ADAM_STEP)
        delta = -ADAM_LR * (m_hat / (jnp.sqrt(v_hat) + ADAM_EPS) + ADAM_WD * wv)
        return g, delta, mn, vn

    pspecs = [_rt(tr, cw) if p.ndim == 2 else pl.BlockSpec((p.shape[0], tr, cw), lambda i: (0, i, 0)) for p in parts]
    return _rows(fn, name=name, n=r // tr, ins=[*parts, w, m, v], in_specs=pspecs + [_rt(tr, cw)] * 3,
                 outs=[_sds((r, cw))] * 4, out_specs=[_rt(tr, cw)] * 4)


def _shard_of(a, axis, k):
    n = a.shape[axis] // 4
    return lax.slice_in_dim(a, k * n, (k + 1) * n, axis=axis)


def _step(a):
    x = a['x'][0]
    mem = a['mem'][0]
    pos = a['positions'][0]
    target = a['loss_target'][0]

    me = 2 * lax.axis_index("x") + lax.axis_index("y")

    mine = [a[n] if n == 'conv_w' else a[n].astype(BF16) for n in SHARDED]
    got = _gather_d2d(_gather_ici(mine))
    w = {}
    for n, own, y in zip(SHARDED, mine, got):
        ax = SHARD_AXIS[n] - 1
        w[n] = [jnp.concatenate([jnp.where(me == k, own[i], y[k, i]) for k in range(4)], axis=ax) for i in range(DEPTH)]
    for n in SMALL:
        w[n] = a[n]

    loss, grad_x, grads = _local_step(x, mem, pos, target, w)

    gsh = []
    for n in SHARDED:
        ax = SHARD_AXIS[n] - 1
        gsh.append(jnp.stack([jnp.stack([_shard_of(grads[i][n], ax, k) for i in range(DEPTH)], axis=0)
                              for k in range(4)], axis=0).astype(BF16))
    got = _reduce_ici(gsh)
    parts = []
    for n, g4, g3 in zip(SHARDED, gsh, got):
        own = lax.dynamic_index_in_dim(g4, me, axis=0, keepdims=False)
        cols = own.shape[-1]
        parts.append(_sum4("sum4_" + n, own.reshape(-1, cols), g3.reshape(3, -1, cols)).reshape(own.shape))
    others = _swap_d2d(parts)
    res_sh = []
    for n, part, other in zip(SHARDED, parts, others):
        cols = part.shape[-1]
        res = _adamw("adamw_" + n, [part.reshape(-1, cols), other.reshape(-1, cols)],
                     *[a[pre + n].reshape(-1, cols) for pre in ('', 'm_', 'v_')])
        res_sh.append([r.reshape(a[n].shape) for r in res])
    res_sh = [[res_sh[j][kind] for j in range(len(SHARDED))] for kind in range(4)]

    sm_shapes = [a[n].shape for n in SMALL] + [(1,)]
    gsm = _pack([jnp.stack([grads[i][n] for i in range(DEPTH)], axis=0) for n in SMALL] + [loss.reshape(1)], 8, F32)
    alls = _all_exchange("comm_reduce_small", gsm)
    zero1 = jnp.zeros((1,), F32)
    res_sm = _adamw("adamw_small", [alls], *[_pack([a[pre + n] for n in SMALL] + [zero1], 8, F32) for pre in ('', 'm_', 'v_')])
    res_sm = [_unpack(r, sm_shapes) for r in res_sm]
    loss = res_sm[0][-1][0]

    outs = [loss, grad_x[None]]
    for kind in range(4):
        byname = dict(zip(SHARDED, res_sh[kind]))
        byname.update(zip(SMALL, res_sm[kind]))
        outs += [byname[n] for n in WEIGHTS]
    return tuple(outs)


def kernel(x, mem, positions, norm_mix_g, w_in, q_a_norm_g, w_q_b, kv_a_norm_g, w_kv_b, q_norm_g, k_norm_g, w_o_mla, ssm_lambda_re, ssm_lambda_im, ssm_log_dt, ssm_b_re, ssm_b_im, ssm_c_re, ssm_c_im, ssm_d, w_glu, b_glu, w_o_ssm, mem_norm_g, w_mem_kv, xq_norm_g, xk_norm_g, w_o_cross, b_gate, w_out, norm_ffn_g, w_up, conv_w, conv_b, w_down, loss_target, m_norm_mix_g, m_w_in, m_q_a_norm_g, m_w_q_b, m_kv_a_norm_g, m_w_kv_b, m_q_norm_g, m_k_norm_g, m_w_o_mla, m_ssm_lambda_re, m_ssm_lambda_im, m_ssm_log_dt, m_ssm_b_re, m_ssm_b_im, m_ssm_c_re, m_ssm_c_im, m_ssm_d, m_w_glu, m_b_glu, m_w_o_ssm, m_mem_norm_g, m_w_mem_kv, m_xq_norm_g, m_xk_norm_g, m_w_o_cross, m_b_gate, m_w_out, m_norm_ffn_g, m_w_up, m_conv_w, m_conv_b, m_w_down, v_norm_mix_g, v_w_in, v_q_a_norm_g, v_w_q_b, v_kv_a_norm_g, v_w_kv_b, v_q_norm_g, v_k_norm_g, v_w_o_mla, v_ssm_lambda_re, v_ssm_lambda_im, v_ssm_log_dt, v_ssm_b_re, v_ssm_b_im, v_ssm_c_re, v_ssm_c_im, v_ssm_d, v_w_glu, v_b_glu, v_w_o_ssm, v_mem_norm_g, v_w_mem_kv, v_xq_norm_g, v_xk_norm_g, v_w_o_cross, v_b_gate, v_w_out, v_norm_ffn_g, v_w_up, v_conv_w, v_conv_b, v_w_down):
    return _step(dict(locals()))
```

```python
import functools
import math

import numpy as np
import jax
import jax.numpy as jnp
from jax import lax
from jax.experimental import pallas as pl
from jax.experimental.pallas import tpu as pltpu

F32 = jnp.float32
BF16 = jnp.bfloat16
MESH = pl.DeviceIdType.MESH

DEPTH = 2
D_MODEL = 1024
EPS = 1e-6
MLA_HEADS = 8
Q_LORA = 384
KV_LORA = 256
D_NOPE = 64
D_ROPE = 32
D_QK = D_NOPE + D_ROPE
D_V = 64
HEAD_PAD = 128
MLA_PAD = MLA_HEADS * HEAD_PAD
ROPE_THETA = 10000.0
SSM_GROUPS = 32
SSM_GROUP_CH = 16
SSM_WIDTH = 512
SSM_STATE = 64
SSM_LANES = SSM_GROUPS * SSM_STATE
SSM_JB = 4
X_HEADS = 4
X_HEAD_DIM = 128
X_WIDTH = 512
D_FF = 2816
SMALL_W = Q_LORA + KV_LORA + HEAD_PAD
SCAN_SEGS = 32
LANE = 128
NEG = -1e30

ADAM_LR = 0.001
ADAM_B1 = 0.9
ADAM_B2 = 0.999
ADAM_EPS = 1e-08
ADAM_WD = 0.01
ADAM_STEP = 10

WEIGHTS = ['norm_mix_g', 'w_in', 'q_a_norm_g', 'w_q_b', 'kv_a_norm_g', 'w_kv_b', 'q_norm_g', 'k_norm_g', 'w_o_mla',
           'ssm_lambda_re', 'ssm_lambda_im', 'ssm_log_dt', 'ssm_b_re', 'ssm_b_im', 'ssm_c_re', 'ssm_c_im', 'ssm_d',
           'w_glu', 'b_glu', 'w_o_ssm', 'mem_norm_g', 'w_mem_kv', 'xq_norm_g', 'xk_norm_g', 'w_o_cross', 'b_gate',
           'w_out', 'norm_ffn_g', 'w_up', 'conv_w', 'conv_b', 'w_down']
SHARD_AXIS = {'w_in': 2, 'w_q_b': 2, 'w_kv_b': 2, 'w_o_mla': 2, 'w_glu': 1, 'w_o_ssm': 2, 'w_mem_kv': 1,
              'w_o_cross': 2, 'w_out': 1, 'w_up': 2, 'conv_w': 2, 'w_down': 1}
SHARDED = [n for n in WEIGHTS if n in SHARD_AXIS]
GATHER_BF16 = [n for n in SHARDED if n != 'conv_w']
SMALL = [n for n in WEIGHTS if n not in SHARD_AXIS]


def _bf(v):
    return v.astype(BF16)


def _mm(a, b):
    return jnp.dot(_bf(a), _bf(b), preferred_element_type=F32)


def _mm_nt(a, b):
    return lax.dot_general(_bf(a), _bf(b), (((1,), (1,)), ((), ())), preferred_element_type=F32)


def _mm_tn(a, b):
    return lax.dot_general(_bf(a), _bf(b), (((0,), (0,)), ((), ())), preferred_element_type=F32)


def _rms(v, g, n):
    ms = jnp.sum(v * v, axis=-1, keepdims=True) * (1.0 / n)
    return (v * lax.rsqrt(ms + EPS)) * g


def _head_rms(v, g, heads, n):
    return jnp.concatenate([_rms(v[:, h * LANE:(h + 1) * LANE], g, n) for h in range(heads)], axis=-1)


def _rope(v, c, s1, s2):
    return v * c + pltpu.roll(v, LANE - 16, 1) * s1 + pltpu.roll(v, 16, 1) * s2


def _rope_t(g, c, s1, s2):
    return g * c + pltpu.roll(g * s1, 16, 1) + pltpu.roll(g * s2, LANE - 16, 1)


def _heads(fn, v, heads, *tabs):
    return jnp.concatenate([fn(v[:, h * LANE:(h + 1) * LANE], *tabs) for h in range(heads)], axis=-1)


def _gelu(y):
    return y * (0.5 * (1.0 + jnp.tanh(math.sqrt(2.0 / math.pi) * (y + 0.044715 * (y * y * y)))))


def _silu(g):
    return g * jax.nn.sigmoid(g)


def _colsum(v):
    return jnp.sum(v, axis=0, keepdims=True)


def _row_select(rows, n):
    rid = lax.broadcasted_iota(jnp.int32, (n, rows[0].shape[-1]), 0)
    out = jnp.zeros((n, rows[0].shape[-1]), F32)
    for k, r in enumerate(rows):
        out = jnp.where(rid == k, jnp.broadcast_to(r, out.shape), out)
    return out


def _params(sem, vmem_mb):
    return pltpu.CompilerParams(dimension_semantics=sem, vmem_limit_bytes=vmem_mb * 1024 * 1024)


def _rt(tm, w, cb=0):
    return pl.BlockSpec((tm, w), lambda i: (i, cb))


def _full(shape):
    nd = len(shape)
    return pl.BlockSpec(tuple(shape), lambda i: (0,) * nd)


def _rows(fn, *, name, n, ins, in_specs, outs, out_specs, n_acc=0, vmem=48):
    n_in = len(ins)
    n_out = len(outs)

    def body(*refs):
        i = pl.program_id(0)
        res = fn(i, *[r[...] for r in refs[:n_in]])
        if not isinstance(res, (tuple, list)):
            res = (res,)
        assert len(res) == n_out, (name, len(res), n_out)
        for k, (r, v) in enumerate(zip(refs[n_in:], res)):
            if k < n_out - n_acc:
                r[...] = v.astype(r.dtype)
            else:
                @pl.when(i == 0)
                def _():
                    r[...] = v

                @pl.when(i > 0)
                def _():
                    r[...] += v

    return pl.pallas_call(
        body, name=name, grid=(n,), in_specs=list(in_specs), out_specs=tuple(out_specs), out_shape=tuple(outs),
        compiler_params=_params(("arbitrary",), vmem))(*ins)


def _sds(shape, dtype=F32):
    return jax.ShapeDtypeStruct(tuple(shape), dtype)


def _tile_n(n, cap=1536):
    best = None
    for t in range(LANE, min(n, cap) + 1, LANE):
        if n % t == 0:
            best = t
    if best is None or n <= 1408:
        return n
    return best


def _matmul(name, pairs, m, n, *, nt=False, rms_gain=None, resid=None, out_dtype=F32, tm=512, vmem=56):
    tm = min(tm, m)
    tn = _tile_n(n)
    ks = [a.shape[1] for a, _ in pairs]
    np_ = len(pairs)

    def body(*refs):
        a_refs = refs[:np_]
        b_refs = refs[np_:2 * np_]
        k = 2 * np_
        g_ref = None
        r_ref = None
        if rms_gain is not None:
            g_ref = refs[k]
            k += 1
        if resid is not None:
            r_ref = refs[k]
            k += 1
        o_ref = refs[k]
        scr = refs[k + 1:]
        j = pl.program_id(1)

        @pl.when(j == 0)
        def _():
            for p in range(np_):
                a = a_refs[p][...]
                if p == 0 and g_ref is not None:
                    a = _rms(a.astype(F32), g_ref[...], ks[0])
                scr[p][...] = a.astype(BF16)

        acc = None
        for p in range(np_):
            b = b_refs[p][...].astype(BF16)
            if nt:
                t = lax.dot_general(scr[p][...], b, (((1,), (1,)), ((), ())), preferred_element_type=F32)
            else:
                t = jnp.dot(scr[p][...], b, preferred_element_type=F32)
            acc = t if acc is None else acc + t
        if r_ref is not None:
            acc = acc + r_ref[...]
        o_ref[...] = acc.astype(o_ref.dtype)

    in_specs = [pl.BlockSpec((tm, kk), lambda i, j: (i, 0)) for kk in ks]
    if nt:
        in_specs += [pl.BlockSpec((tn, kk), lambda i, j: (j, 0)) for kk in ks]
    else:
        in_specs += [pl.BlockSpec((kk, tn), lambda i, j: (0, j)) for kk in ks]
    ins = [a for a, _ in pairs] + [b for _, b in pairs]
    if rms_gain is not None:
        in_specs.append(pl.BlockSpec((1, ks[0]), lambda i, j: (0, 0)))
        ins.append(rms_gain)
    if resid is not None:
        in_specs.append(pl.BlockSpec((tm, tn), lambda i, j: (i, j)))
        ins.append(resid)
    return pl.pallas_call(
        body, name=name, grid=(m // tm, n // tn), in_specs=in_specs,
        out_specs=pl.BlockSpec((tm, tn), lambda i, j: (i, j)), out_shape=_sds((m, n), out_dtype),
        scratch_shapes=[pltpu.VMEM((tm, kk), BF16) for kk in ks],
        compiler_params=_params(("arbitrary", "arbitrary"), vmem))(*ins)


def _matmul_tn(name, a, b, *, rms_gain=None, tl=512, vmem=56):
    l, ka = a.shape
    n = b.shape[1]
    tl = min(tl, l)
    tn = _tile_n(n, 1536)

    def body(*refs):
        if rms_gain is not None:
            a_ref, b_ref, g_ref, o_ref = refs
        else:
            a_ref, b_ref, o_ref = refs
        t = pl.program_id(1)
        av = a_ref[...]
        if rms_gain is not None:
            av = _rms(av.astype(F32), g_ref[...], ka)
        v = _mm_tn(av, b_ref[...])

        @pl.when(t == 0)
        def _():
            o_ref[...] = v

        @pl.when(t > 0)
        def _():
            o_ref[...] += v

    in_specs = [pl.BlockSpec((tl, ka), lambda j, t: (t, 0)), pl.BlockSpec((tl, tn), lambda j, t: (t, j))]
    ins = [a, b]
    if rms_gain is not None:
        in_specs.append(pl.BlockSpec((1, ka), lambda j, t: (0, 0)))
        ins.append(rms_gain)
    return pl.pallas_call(
        body, name=name, grid=(n // tn, l // tl), in_specs=in_specs,
        out_specs=pl.BlockSpec((ka, tn), lambda j, t: (0, j)), out_shape=_sds((ka, n)),
        compiler_params=_params(("arbitrary", "arbitrary"), vmem))(*ins)


ATT_HEADS_PER_STEP = 2
ATT_W = ATT_HEADS_PER_STEP * LANE
ATT_GROUPS = MLA_HEADS // ATT_HEADS_PER_STEP
LOG2E = math.log2(math.e)
ATT_SCALE = D_QK ** -0.5
ATT_QSCALE = ATT_SCALE * LOG2E


def _tri_tables(nq, by_k):
    qs, ks = [], []
    if by_k:
        for ki in range(nq):
            for qi in range(ki, nq):
                qs.append(qi)
                ks.append(ki)
    else:
        for qi in range(nq):
            for ki in range(qi + 1):
                qs.append(qi)
                ks.append(ki)
    return jnp.asarray(np.array(qs, np.int32)), jnp.asarray(np.array(ks, np.int32))


def _causal_keep(shape, transposed):
    r = lax.broadcasted_iota(jnp.int32, shape, 0)
    c = lax.broadcasted_iota(jnp.int32, shape, 1)
    return (r <= c) if transposed else (c <= r)


def _nt16(a, b):
    return lax.dot_general(a, b, (((1,), (1,)), ((), ())), preferred_element_type=F32)


def _row_form(col):
    return jnp.transpose(jnp.broadcast_to(col, (col.shape[0], LANE)))[:8]


def _att_call(body, name, l, tq, tabs, ins, in_specs, outs, out_specs, scratch=()):
    grid_spec = pltpu.PrefetchScalarGridSpec(
        num_scalar_prefetch=2, grid=(ATT_GROUPS, tabs[0].shape[0]), in_specs=in_specs, out_specs=out_specs,
        scratch_shapes=list(scratch))
    return pl.pallas_call(body, name=name, grid_spec=grid_spec, out_shape=outs,
                          compiler_params=_params(("arbitrary", "arbitrary"), 48))(*tabs, *ins)


def _flash_fwd(name, q, k, v_t):
    l = q.shape[0]
    tq = min(512, l)
    nq = l // tq
    tabs = _tri_tables(nq, by_k=False)

    def body(qt, kt, q_ref, k_ref, vt_ref, o_ref, lset_ref, m_s, acc_s):
        t = pl.program_id(1)
        qi = qt[t]
        ki = kt[t]
        sls = [slice(h * LANE, (h + 1) * LANE) for h in range(ATT_HEADS_PER_STEP)]

        @pl.when(ki == 0)
        def _():
            m_s[...] = jnp.full(m_s.shape, NEG, F32)
            acc_s[...] = jnp.zeros(acc_s.shape, F32)

        def step(masked):
            sts = [_nt16(k_ref[:, sl], q_ref[:, sl]) for sl in sls]
            for h, sl in enumerate(sls):
                st = sts[h]
                if masked:
                    st = jnp.where(_causal_keep(st.shape, True), st, NEG)
                m_old = m_s[h][:1]
                m_new = jnp.maximum(m_old, jnp.max(st, axis=0, keepdims=True))
                alpha = jnp.exp2(m_old - m_new)
                pt = jnp.exp2(st - m_new).astype(BF16)
                acc_s[sl, :] = alpha * acc_s[sl, :] + jnp.dot(vt_ref[sl, :], pt, preferred_element_type=F32)
                m_s[h] = jnp.broadcast_to(m_new, (8, tq))

        @pl.when(ki < qi)
        def _():
            step(False)

        @pl.when(ki == qi)
        def _():
            step(True)
            row = lax.broadcasted_iota(jnp.int32, (LANE, tq), 0)
            for h, sl in enumerate(sls):
                acc = acc_s[sl, :]
                lsum = acc[D_V:D_V + 1, :]
                o_ref[:, sl] = jnp.transpose(jnp.where(row < D_V, acc / lsum, 0.0))
                lset_ref[h * 8:(h + 1) * 8, :] = m_s[h] + jnp.log2(lsum)

    qspec = pl.BlockSpec((tq, ATT_W), lambda g, t, qt, kt: (qt[t], g))
    kspec = pl.BlockSpec((tq, ATT_W), lambda g, t, qt, kt: (kt[t], g))
    vspec = pl.BlockSpec((ATT_W, tq), lambda g, t, qt, kt: (g, kt[t]))
    rspec = pl.BlockSpec((8 * ATT_HEADS_PER_STEP, tq), lambda g, t, qt, kt: (g, qt[t]))
    return _att_call(
        body, name, l, tq, tabs, [q, k, v_t], [qspec, kspec, vspec],
        (_sds((l, MLA_PAD)), _sds((8 * MLA_HEADS, l))), (qspec, rspec),
        scratch=[pltpu.VMEM((ATT_HEADS_PER_STEP, 8, tq), F32), pltpu.VMEM((ATT_W, tq), F32)])


def _flash_bwd(name, q, k, v, q_t, k_t, o, lse_t, do):
    l = q.shape[0]
    tq = min(512, l)
    nq = l // tq

    def delta_fn(i, dov, ov):
        rows = []
        for h in range(MLA_HEADS):
            sl = slice(h * LANE, (h + 1) * LANE)
            rows.append(_row_form(jnp.sum(dov[:, sl] * ov[:, sl], axis=-1, keepdims=True)))
        return jnp.concatenate(rows, axis=0), dov, jnp.transpose(dov)

    delta_t, do16, do_t = _rows(
        delta_fn, name=name + "_delta", n=nq, ins=[do, o], in_specs=[_rt(tq, MLA_PAD)] * 2,
        outs=[_sds((8 * MLA_HEADS, l)), _sds((l, MLA_PAD), BF16), _sds((MLA_PAD, l), BF16)],
        out_specs=[pl.BlockSpec((8 * MLA_HEADS, tq), lambda i: (0, i)), _rt(tq, MLA_PAD),
                   pl.BlockSpec((MLA_PAD, tq), lambda i: (0, i))])

    def body(qt, kt, q_ref, k_ref, v_ref, do_ref, qt_ref, kt_ref, dot_ref, lset_ref, dlt_ref,
             dk_ref, dv_ref, dqt_ref, dkt_s, dvt_s):
        t = pl.program_id(1)
        qi = qt[t]
        ki = kt[t]
        sls = [slice(h * LANE, (h + 1) * LANE) for h in range(ATT_HEADS_PER_STEP)]

        @pl.when(ki == 0)
        def _():
            dqt_ref[qi] = jnp.zeros((ATT_W, tq), F32)

        def step(masked):
            sts = [_nt16(k_ref[:, sl], q_ref[:, sl]) for sl in sls]
            dpts = [_nt16(v_ref[:, sl], do_ref[:, sl]) for sl in sls]
            for h, sl in enumerate(sls):
                st = sts[h]
                if masked:
                    st = jnp.where(_causal_keep(st.shape, True), st, NEG)
                pt = jnp.exp2(st - lset_ref[h * 8:(h + 1) * 8, :][:1])
                dst = (pt * (dpts[h] - dlt_ref[h * 8:(h + 1) * 8, :][:1])).astype(BF16)
                dvt_s[sl, :] += _nt16(dot_ref[sl, :], pt.astype(BF16))
                dkt_s[sl, :] += _nt16(qt_ref[sl, :], dst)
                dqt_ref[qi, sl, :] += jnp.dot(kt_ref[sl, :], dst, preferred_element_type=F32)

        @pl.when(qi == ki)
        def _():
            dkt_s[...] = jnp.zeros(dkt_s.shape, F32)
            dvt_s[...] = jnp.zeros(dvt_s.shape, F32)
            step(True)
            dqt_ref[qi] = dqt_ref[qi] * ATT_SCALE

        @pl.when(qi > ki)
        def _():
            step(False)

        @pl.when(qi == nq - 1)
        def _():
            for sl in sls:
                dk_ref[:, sl] = jnp.transpose(dkt_s[sl, :]) * (1.0 / LOG2E)
                dv_ref[:, sl] = jnp.transpose(dvt_s[sl, :])

    tabs_k = _tri_tables(nq, by_k=True)
    qspec = pl.BlockSpec((tq, ATT_W), lambda g, t, qt, kt: (qt[t], g))
    kspec = pl.BlockSpec((tq, ATT_W), lambda g, t, qt, kt: (kt[t], g))
    qtspec = pl.BlockSpec((ATT_W, tq), lambda g, t, qt, kt: (g, qt[t]))
    ktspec = pl.BlockSpec((ATT_W, tq), lambda g, t, qt, kt: (g, kt[t]))
    rspec = pl.BlockSpec((8 * ATT_HEADS_PER_STEP, tq), lambda g, t, qt, kt: (g, qt[t]))
    dqspec = pl.BlockSpec((nq, ATT_W, tq), lambda g, t, qt, kt: (0, g, 0))
    dk, dv, dq_t = _att_call(body, name + "_dqkv", l, tq, tabs_k, [q, k, v, do16, q_t, k_t, do_t, lse_t, delta_t],
                             [qspec, kspec, kspec, qspec, qtspec, ktspec, qtspec, rspec, rspec],
                             (_sds((l, MLA_PAD)), _sds((l, MLA_PAD)), _sds((nq, MLA_PAD, tq))), (kspec, kspec, dqspec),
                             scratch=[pltpu.VMEM((ATT_W, tq), F32), pltpu.VMEM((ATT_W, tq), F32)])
    return dq_t, dk, dv


def _cmul(ar, ai, br, bi):
    return ar * br - ai * bi, ar * bi + ai * br


def _scan(name, x_re, x_im, a_re, a_im, reverse):
    l, lanes = x_re.shape
    ns = SCAN_SEGS
    tl = l // ns
    steps = int(math.log2(tl))
    assert 2 ** steps == tl and tl * ns == l

    def body(xr_ref, xi_ref, ar_ref, ai_ref, sr_ref, si_ref):
        a_r1 = ar_ref[...]
        a_i1 = ai_ref[...]
        a_r = jnp.broadcast_to(a_r1, (ns, LANE))
        a_i = jnp.broadcast_to(a_i1, (ns, LANE))

        def rows(t):
            t = (tl - 1 - t) if reverse else t
            return pl.ds(pl.multiple_of(t * ns, ns), ns)

        def local(t, carry):
            cr, ci = carry
            r = rows(t)
            pr, pi = _cmul(a_r, a_i, cr, ci)
            return pr + xr_ref[r, :], pi + xi_ref[r, :]

        zero = jnp.zeros((ns, LANE), F32)
        e_r, e_i = lax.fori_loop(0, tl, local, (zero, zero), unroll=min(8, tl))
        p_r, p_i = a_r1, a_i1
        for _ in range(steps):
            p_r, p_i = _cmul(p_r, p_i, p_r, p_i)
        rid = lax.broadcasted_iota(jnp.int32, (ns, LANE), 0)
        c_r = jnp.zeros((1, LANE), F32)
        c_i = jnp.zeros((1, LANE), F32)
        in_r, in_i = zero, zero
        order = range(ns - 2, -1, -1) if reverse else range(1, ns)
        for kk in order:
            src = kk + 1 if reverse else kk - 1
            ek_r = jnp.sum(jnp.where(rid == src, e_r, 0.0), axis=0, keepdims=True)
            ek_i = jnp.sum(jnp.where(rid == src, e_i, 0.0), axis=0, keepdims=True)
            q_r, q_i = _cmul(p_r, p_i, c_r, c_i)
            c_r, c_i = q_r + ek_r, q_i + ek_i
            in_r = jnp.where(rid == kk, jnp.broadcast_to(c_r, (ns, LANE)), in_r)
            in_i = jnp.where(rid == kk, jnp.broadcast_to(c_i, (ns, LANE)), in_i)

        def final(t, carry):
            cr, ci = carry
            r = rows(t)
            pr, pi = _cmul(a_r, a_i, cr, ci)
            nr, ni = pr + xr_ref[r, :], pi + xi_ref[r, :]
            sr_ref[r, :] = nr
            si_ref[r, :] = ni
            return nr, ni

        lax.fori_loop(0, tl, final, (in_r, in_i), unroll=min(8, tl))

    xs = pl.BlockSpec((l, LANE), lambda j: (0, j))
    as_ = pl.BlockSpec((1, LANE), lambda j: (0, j))
    return pl.pallas_call(
        body, name=name, grid=(lanes // LANE,), in_specs=[xs, xs, as_, as_], out_specs=(xs, xs),
        out_shape=(_sds((l, lanes)), _sds((l, lanes))),
        compiler_params=_params(("arbitrary",), 48))(x_re, x_im, a_re, a_im)


ANY = pl.BlockSpec(memory_space=pl.ANY)


def _place():
    mx, my, mc = lax.axis_index("x"), lax.axis_index("y"), lax.axis_index("c")
    return mx, my, mc, [(1 - mx, my), (mx, 1 - my), (1 - mx, 1 - my)]


def _run_copies(copies):
    for cp in copies:
        cp.start()
    for cp in copies:
        cp.wait_recv()
    for cp in copies:
        cp.wait_send()


def _remote(src, dst, sems, k, dev):
    return pltpu.make_async_remote_copy(src_ref=src, dst_ref=dst, send_sem=sems[0].at[k], recv_sem=sems[1].at[k],
                                        device_id=dev, device_id_type=MESH)


def _copy_call(body, name, ins, outs, n_copies, aliases=None):
    return pl.pallas_call(
        body, name=name, in_specs=[ANY] * len(ins), out_specs=[ANY] * len(outs), out_shape=list(outs),
        input_output_aliases=aliases or {},
        scratch_shapes=[pltpu.SemaphoreType.DMA((n_copies,)), pltpu.SemaphoreType.DMA((n_copies,))])(*ins)


def _gather_ici(xs):
    n = len(xs)

    def body(*refs):
        x_refs, y_refs, sems = refs[:n], refs[n:2 * n], refs[2 * n:]
        mx, my, mc, peers = _place()
        me = 2 * mx + my
        _run_copies([_remote(x_refs[i].at[mc], y_refs[i].at[me, mc], sems, 3 * i + j, (px, py, mc))
                     for i in range(n) for j, (px, py) in enumerate(peers)])

    return _copy_call(body, "comm_gather_ici", xs, [_sds((4,) + x.shape, x.dtype) for x in xs], 3 * n)


def _gather_d2d(ys):
    n = len(ys)

    def body(*refs):
        y_in, y_out, sems = refs[:n], refs[n:2 * n], refs[2 * n:]
        mx, my, mc, peers = _place()
        _run_copies([_remote(y_in[i].at[2 * px + py, mc], y_out[i].at[2 * px + py, mc], sems, 3 * i + j, (mx, my, 1 - mc))
                     for i in range(n) for j, (px, py) in enumerate(peers)])

    return _copy_call(body, "comm_gather_d2d", ys, [_sds(y.shape, y.dtype) for y in ys], 3 * n,
                      aliases={i: i for i in range(n)})


def _reduce_ici(gs):
    n = len(gs)

    def body(*refs):
        g_refs, y_refs, sems = refs[:n], refs[n:2 * n], refs[2 * n:]
        mx, my, mc, peers = _place()
        _run_copies([_remote(g_refs[i].at[2 * px + py], y_refs[i].at[j], sems, 3 * i + j, (px, py, mc))
                     for i in range(n) for j, (px, py) in enumerate(peers)])

    return _copy_call(body, "comm_reduce_ici", gs, [_sds((3,) + g.shape[1:], g.dtype) for g in gs], 3 * n)


def _swap_d2d(name, ps, other_layer):
    n = len(ps)

    def body(*refs):
        p_refs, o_refs, sems = refs[:n], refs[n:2 * n], refs[2 * n:]
        mx, my, mc, _ = _place()
        _run_copies([_remote(p_refs[i].at[1 - mc] if other_layer else p_refs[i], o_refs[i], sems, i, (mx, my, 1 - mc))
                     for i in range(n)])

    outs = [_sds(p.shape[1:] if other_layer else p.shape, p.dtype) for p in ps]
    return _copy_call(body, name, ps, outs, n)


def _all_exchange(name, src):
    def body(x_ref, y_ref, send_sems, recv_sems, local_sem):
        mx, my, mc = lax.axis_index("x"), lax.axis_index("y"), lax.axis_index("c")
        me = 4 * mx + 2 * my + mc
        own = pltpu.make_async_copy(x_ref, y_ref.at[me], local_sem)
        own.start()
        copies = []
        for j in range(1, 8):
            px = (1 - mx) if (j & 4) else mx
            py = (1 - my) if (j & 2) else my
            pc = (1 - mc) if (j & 1) else mc
            cp = pltpu.make_async_remote_copy(
                src_ref=x_ref, dst_ref=y_ref.at[me], send_sem=send_sems.at[j - 1], recv_sem=recv_sems.at[j - 1],
                device_id=(px, py, pc), device_id_type=MESH)
            cp.start()
            copies.append(cp)
        for cp in copies:
            cp.wait_recv()
        for cp in copies:
            cp.wait_send()
        own.wait()

    return pl.pallas_call(
        body, name=name, in_specs=[ANY], out_specs=ANY, out_shape=_sds((8,) + src.shape, src.dtype),
        scratch_shapes=[pltpu.SemaphoreType.DMA((7,)), pltpu.SemaphoreType.DMA((7,)), pltpu.SemaphoreType.DMA])(src)


PACK_W = 1024


def _pack(arrs, rows_multiple, dtype):
    flat = jnp.concatenate([a.reshape(-1).astype(dtype) for a in arrs])
    n = flat.shape[0]
    unit = PACK_W * rows_multiple
    tot = -(-n // unit) * unit
    flat = jnp.pad(flat, (0, tot - n))
    return flat.reshape(tot // PACK_W, PACK_W)


def _unpack(flat, shapes):
    flat = flat.reshape(-1)
    out = []
    off = 0
    for s in shapes:
        n = int(np.prod(s))
        out.append(flat[off:off + n].reshape(s))
        off += n
    return out


def _rope_tables(pos):
    l = pos.shape[0]
    tm = min(512, l)
    inv = (np.float32(ROPE_THETA) ** (-np.arange(0, D_ROPE, 2, dtype=np.float32) / np.float32(D_ROPE))).astype(np.float32)
    lane_f = np.zeros((1, LANE), np.float32)
    lane_f[0, D_NOPE:D_NOPE + 16] = inv
    lane_f[0, D_NOPE + 16:D_NOPE + 32] = inv

    def fn(i, p, f):
        ang = p * f
        lane = lax.broadcasted_iota(jnp.int32, ang.shape, 1)
        co = jnp.cos(ang)
        si = jnp.sin(ang)
        c = jnp.where(lane < D_NOPE, 1.0, jnp.where(lane < D_QK, co, 0.0))
        s1 = jnp.where((lane >= D_NOPE) & (lane < D_NOPE + 16), -si, 0.0)
        s2 = jnp.where((lane >= D_NOPE + 16) & (lane < D_QK), si, 0.0)
        return c, s1, s2

    return _rows(fn, name="rope_tables", n=l // tm, ins=[pos, jnp.asarray(lane_f)],
                 in_specs=[_rt(tm, 1), _full((1, LANE))], outs=[_sds((l, LANE))] * 3, out_specs=[_rt(tm, LANE)] * 3)


def _ssm_param_fn(lr, li, log_dt, br, bi):
    dt = jnp.exp(log_dt)
    mag = jnp.exp(lr * dt)
    a_re = mag * jnp.cos(li * dt)
    a_im = mag * jnp.sin(li * dt)
    den = lr * lr + li * li
    e_re = a_re - 1.0
    e_im = a_im
    f_re = (e_re * lr + e_im * li) / den
    f_im = (e_im * lr - e_re * li) / den
    bb_re = f_re[None] * br - f_im[None] * bi
    bb_im = f_re[None] * bi + f_im[None] * br
    return a_re, a_im, bb_re, bb_im


def _ssm_params(name, lr, li, log_dt, br, bi):
    g, n = lr.shape
    c = br.shape[0]
    return _rows(lambda i, *v: _ssm_param_fn(*v), name=name, n=1, ins=[lr, li, log_dt, br, bi],
                 in_specs=[_full((g, n)), _full((g, n)), _full((g, 1)), _full((c, g, n)), _full((c, g, n))],
                 outs=[_sds((g, n)), _sds((g, n)), _sds((c, g, n)), _sds((c, g, n))],
                 out_specs=[_full((g, n)), _full((g, n)), _full((c, g, n)), _full((c, g, n))])


def _ssm_params_bwd(name, lr, li, log_dt, br, bi, d_are, d_aim, d_bbre, d_bbim):
    g, n = lr.shape
    c = br.shape[0]

    def fn(i, lr, li, log_dt, br, bi, g0, g1, g2, g3):
        _, vjp = jax.vjp(_ssm_param_fn, lr, li, log_dt, br, bi)
        return vjp((g0, g1, g2, g3))

    sp = [_full((g, n)), _full((g, n)), _full((g, 1)), _full((c, g, n)), _full((c, g, n))]
    return _rows(fn, name=name, n=1, ins=[lr, li, log_dt, br, bi, d_are, d_aim, d_bbre, d_bbim],
                 in_specs=sp + [_full((g, n)), _full((g, n)), _full((c, g, n)), _full((c, g, n))],
                 outs=[_sds((g, n)), _sds((g, n)), _sds((g, 1)), _sds((c, g, n)), _sds((c, g, n))], out_specs=sp)


_EYE8 = np.eye(8, dtype=np.float32)


def _blockdiag(v):
    j, g, p, q = v.shape
    m = v[:, :, :, None, :] * jnp.asarray(_EYE8)[None, :, None, :, None]
    return m.reshape(j, g * p, g * q)


def _blockdiag_t(m, p, q):
    j = m.shape[0]
    m = m.reshape(j, 8, p, 8, q)
    return jnp.sum(m * jnp.asarray(_EYE8)[None, :, None, :, None], axis=3)


def _to_perm(v, l):
    ns = SCAN_SEGS
    return v.reshape(ns, l // ns, v.shape[-1]).transpose(1, 0, 2).reshape(l, v.shape[-1])


def _from_perm(v, l):
    ns = SCAN_SEGS
    return v.reshape(l // ns, ns, v.shape[-1]).transpose(1, 0, 2).reshape(l, v.shape[-1])


def _prep_layer(w, i):
    p = {}
    w_in = w['w_in'][i]
    z = lambda n: jnp.zeros((D_MODEL, n), w_in.dtype)
    o = Q_LORA + KV_LORA
    p['w_s'] = jnp.concatenate([w_in[:, :o], z(D_NOPE), w_in[:, o:o + D_ROPE], z(HEAD_PAD - D_QK)], axis=1)
    o += D_ROPE
    p['w_u'] = w_in[:, o:o + SSM_WIDTH]
    o += SSM_WIDTH
    p['w_xq'] = w_in[:, o:o + X_WIDTH]
    o += X_WIDTH
    p['w_g'] = w_in[:, o:]
    wq = w['w_q_b'][i].reshape(Q_LORA, MLA_HEADS, D_QK)
    p['w_qb'] = jnp.pad(wq, ((0, 0), (0, 0), (0, HEAD_PAD - D_QK))).reshape(Q_LORA, MLA_PAD)
    wkv = w['w_kv_b'][i].reshape(KV_LORA, MLA_HEADS, D_NOPE + D_V)
    p['w_k'] = jnp.pad(wkv[:, :, :D_NOPE], ((0, 0), (0, 0), (0, HEAD_PAD - D_NOPE))).reshape(KV_LORA, MLA_PAD)
    p['w_v'] = jnp.pad(wkv[:, :, D_NOPE:], ((0, 0), (0, 0), (0, HEAD_PAD - D_V))).reshape(KV_LORA, MLA_PAD)
    wo = w['w_o_mla'][i].reshape(MLA_HEADS, D_V, D_MODEL)
    p['w_oa'] = jnp.pad(wo, ((0, 0), (0, HEAD_PAD - D_V), (0, 0))).reshape(MLA_PAD, D_MODEL)
    for n in ('w_glu', 'w_o_ssm', 'w_mem_kv', 'w_o_cross', 'w_out', 'w_up', 'w_down'):
        p[n] = w[n][i]
    p['conv_w'] = w['conv_w'][i]
    for n in ('norm_mix_g', 'q_a_norm_g', 'kv_a_norm_g', 'b_glu', 'mem_norm_g', 'xq_norm_g', 'xk_norm_g', 'b_gate',
              'norm_ffn_g', 'conv_b'):
        p[n] = w[n][i].reshape(1, -1)
    p['q_norm_g'] = jnp.pad(w['q_norm_g'][i], (0, HEAD_PAD - D_QK)).reshape(1, HEAD_PAD)
    p['k_norm_g'] = jnp.pad(w['k_norm_g'][i], (0, HEAD_PAD - D_QK)).reshape(1, HEAD_PAD)
    p['ssm_d'] = w['ssm_d'][i].reshape(1, SSM_WIDTH)
    p['lr'] = w['ssm_lambda_re'][i]
    p['li'] = w['ssm_lambda_im'][i]
    p['log_dt'] = w['ssm_log_dt'][i].reshape(SSM_GROUPS, 1)
    p['br'] = w['ssm_b_re'][i].transpose(2, 0, 1)
    p['bi'] = w['ssm_b_im'][i].transpose(2, 0, 1)
    cr = w['ssm_c_re'][i].reshape(SSM_JB, 8, SSM_GROUP_CH, SSM_STATE).transpose(0, 1, 3, 2)
    ci = w['ssm_c_im'][i].reshape(SSM_JB, 8, SSM_GROUP_CH, SSM_STATE).transpose(0, 1, 3, 2)
    p['c_mat'] = jnp.concatenate([_blockdiag(cr), -_blockdiag(ci)], axis=1).astype(BF16)
    return p


def _b_mat(bb_re, bb_im):
    r = bb_re.transpose(1, 0, 2).reshape(SSM_JB, 8, SSM_GROUP_CH, SSM_STATE)
    i = bb_im.transpose(1, 0, 2).reshape(SSM_JB, 8, SSM_GROUP_CH, SSM_STATE)
    return jnp.concatenate([_blockdiag(r), _blockdiag(i)], axis=2).astype(BF16)


def _qkv_fn(ps, c, s1, s2, qag, wqb, kvag, wk, wv, qng, kng):
    c_q = ps[:, :Q_LORA]
    c_kv = ps[:, Q_LORA:Q_LORA + KV_LORA]
    kr = ps[:, Q_LORA + KV_LORA:]
    cqn = _rms(c_q, qag, Q_LORA)
    ckvn = _rms(c_kv, kvag, KV_LORA)
    q_raw = _mm(cqn, wqb)
    k_raw = _mm(ckvn, wk) + jnp.concatenate([kr] * MLA_HEADS, axis=-1)
    v = _mm(ckvn, wv)
    q = _heads(_rope, _head_rms(q_raw, qng, MLA_HEADS, D_QK), MLA_HEADS, c, s1, s2)
    k = _heads(_rope, _head_rms(k_raw, kng, MLA_HEADS, D_QK), MLA_HEADS, c, s1, s2)
    lane = lax.broadcasted_iota(jnp.int32, v.shape, 1)
    v = jnp.where((lane & (LANE - 1)) == D_V, 1.0, v)
    return q * ATT_QSCALE, k, v


def _layer_fwd(name, x, tabs, mem, p):
    l = x.shape[0]
    tm = min(512, l)
    nt = l // tm
    sv = {'x0': x}
    sv['p_g'] = _matmul(name + "_in_g", [(x, p['w_g'])], l, 3 * D_MODEL, rms_gain=p['norm_mix_g'])
    sv['p_u'] = _matmul(name + "_in_u", [(x, p['w_u'])], l, SSM_WIDTH, rms_gain=p['norm_mix_g'])
    sv['p_xq'] = _matmul(name + "_in_xq", [(x, p['w_xq'])], l, X_WIDTH, rms_gain=p['norm_mix_g'])
    sv['p_s'] = _matmul(name + "_in_s", [(x, p['w_s'])], l, SMALL_W, rms_gain=p['norm_mix_g'])

    qkv_consts = [p['q_a_norm_g'], p['w_qb'], p['kv_a_norm_g'], p['w_k'], p['w_v'], p['q_norm_g'], p['k_norm_g']]
    qkv_cspecs = [_full(a.shape) for a in qkv_consts]
    def qkv_fwd(i, *a):
        qv, kv, vv = _qkv_fn(*a)
        return qv, kv, vv, jnp.transpose(qv), jnp.transpose(kv), jnp.transpose(vv)

    q, k, v, q_t, k_t, v_t = _rows(qkv_fwd, name=name + "_qkv", n=nt, ins=[sv['p_s'], *tabs, *qkv_consts],
                                   in_specs=[_rt(tm, SMALL_W)] + [_rt(tm, LANE)] * 3 + qkv_cspecs,
                                   outs=[_sds((l, MLA_PAD), BF16)] * 3 + [_sds((MLA_PAD, l), BF16)] * 3,
                                   out_specs=[_rt(tm, MLA_PAD)] * 3 + [pl.BlockSpec((MLA_PAD, tm), lambda i: (0, i))] * 3)
    sv['q'], sv['k'], sv['v'], sv['q_t'], sv['k_t'] = q, k, v, q_t, k_t
    sv['o_a'], sv['lse_t'] = _flash_fwd(name + "_attn", q, k, v_t)

    a_re, a_im, bb_re, bb_im = _ssm_params(name + "_ssm_par", p['lr'], p['li'], p['log_dt'], p['br'], p['bi'])
    sv['a_re'], sv['a_im'] = a_re.reshape(1, SSM_LANES), a_im.reshape(1, SSM_LANES)
    sv['b_mat'] = _b_mat(bb_re, bb_im)
    u_p = _to_perm(sv['p_u'], l)
    sv['u_p'] = u_p

    def bu_fn(i, u, bm):
        res = [_mm(u[:, j * LANE:(j + 1) * LANE], bm[j]) for j in range(SSM_JB)]
        return (jnp.concatenate([r[:, :512] for r in res], axis=-1), jnp.concatenate([r[:, 512:] for r in res], axis=-1))

    ts = min(256, l)
    bu_re, bu_im = _rows(bu_fn, name=name + "_ssm_bu", n=l // ts, ins=[u_p, sv['b_mat']],
                         in_specs=[_rt(ts, SSM_WIDTH), _full(sv['b_mat'].shape)],
                         outs=[_sds((l, SSM_LANES))] * 2, out_specs=[_rt(ts, SSM_LANES)] * 2)
    s_re, s_im = _scan(name + "_ssm_scan", bu_re, bu_im, sv['a_re'], sv['a_im'], reverse=False)
    sv['s_re'], sv['s_im'] = s_re, s_im

    def glu_fn(i, sr, si, u, cm, dsk, wg, bg):
        y = jnp.concatenate([_mm(jnp.concatenate([sr[:, j * 512:(j + 1) * 512], si[:, j * 512:(j + 1) * 512]], axis=-1),
                                 cm[j]) for j in range(SSM_JB)], axis=-1) + dsk * u
        zz = _gelu(y)
        return zz * jax.nn.sigmoid(_mm(zz, wg) + bg)

    glu_consts = [p['c_mat'], p['ssm_d'], p['w_glu'], p['b_glu']]
    zo_p = _rows(glu_fn, name=name + "_ssm_glu", n=l // ts, ins=[s_re, s_im, u_p, *glu_consts],
                 in_specs=[_rt(ts, SSM_LANES), _rt(ts, SSM_LANES), _rt(ts, SSM_WIDTH)] + [_full(a.shape) for a in glu_consts],
                 outs=[_sds((l, SSM_WIDTH), BF16)], out_specs=[_rt(ts, SSM_WIDTH)])[0]
    sv['zo'] = _from_perm(zo_p, l)

    m_len = mem.shape[0]

    def memkv_fn(i, mm_, mg, wmk, xkg):
        kv = _mm(_rms(mm_, mg, D_MODEL), wmk)
        return _head_rms(kv[:, :X_WIDTH], xkg, X_HEADS, X_HEAD_DIM), kv[:, X_WIDTH:]

    mem_consts = [p['mem_norm_g'], p['w_mem_kv'], p['xk_norm_g']]
    k_c, v_c = _rows(memkv_fn, name=name + "_memkv", n=1, ins=[mem, *mem_consts],
                     in_specs=[_full(mem.shape)] + [_full(a.shape) for a in mem_consts],
                     outs=[_sds((m_len, X_WIDTH))] * 2, out_specs=[_full((m_len, X_WIDTH))] * 2)
    sv['k_c'], sv['v_c'] = k_c, v_c

    def cross_fn(i, xq, kc, vc, xqg):
        outs = []
        for h in range(X_HEADS):
            sl = slice(h * LANE, (h + 1) * LANE)
            qh = _rms(xq[:, sl], xqg, X_HEAD_DIM)
            s = _mm_nt(qh, kc[:, sl]) * (X_HEAD_DIM ** -0.5)
            s = s - jnp.max(s, axis=-1, keepdims=True)
            e = jnp.exp(s)
            pr = e / jnp.sum(e, axis=-1, keepdims=True)
            outs.append(_mm(pr, vc[:, sl]))
        return jnp.concatenate(outs, axis=-1)

    sv['o_c'] = _rows(cross_fn, name=name + "_cross", n=nt, ins=[sv['p_xq'], k_c, v_c, p['xq_norm_g']],
                      in_specs=[_rt(tm, X_WIDTH), _full(k_c.shape), _full(v_c.shape), _full((1, LANE))],
                      outs=[_sds((l, X_WIDTH), BF16)], out_specs=[_rt(tm, X_WIDTH)])[0]

    def merge_fn(i, oa, zo, oc, pg, x0, woa, wos, woc, bg, wout):
        gates = jax.nn.sigmoid(pg + bg)
        merged = (gates[:, :D_MODEL] * _mm(oa, woa) + gates[:, D_MODEL:2 * D_MODEL] * _mm(zo, wos)
                  + gates[:, 2 * D_MODEL:] * _mm(oc, woc))
        return x0 + _mm(merged, wout), merged

    merge_consts = [p['w_oa'], p['w_o_ssm'], p['w_o_cross'], p['b_gate'], p['w_out']]
    tg = min(256, l)
    x1, merged = _rows(merge_fn, name=name + "_merge", n=l // tg, ins=[sv['o_a'], sv['zo'], sv['o_c'], sv['p_g'], x, *merge_consts],
                       in_specs=[_rt(tg, MLA_PAD), _rt(tg, SSM_WIDTH), _rt(tg, X_WIDTH), _rt(tg, 3 * D_MODEL), _rt(tg, D_MODEL)]
                       + [_full(a.shape) for a in merge_consts],
                       outs=[_sds((l, D_MODEL)), _sds((l, D_MODEL), BF16)], out_specs=[_rt(tg, D_MODEL)] * 2)
    sv['x1'], sv['merged'] = x1, merged

    up = _matmul(name + "_up", [(x1, p['w_up'])], l, 2 * D_FF, rms_gain=p['norm_ffn_g'])
    sv['up'] = up
    tc = min(128, l)

    def conv_fn(i, upt, halo, cw, cb):
        upc = _conv(i, upt, halo, cw) + cb
        return _silu(upc[:, :D_FF]) * upc[:, D_FF:]

    act = _rows(conv_fn, name=name + "_conv", n=l // tc, ins=[up, up, p['conv_w'], p['conv_b']],
                in_specs=[_rt(tc, 2 * D_FF), _halo_prev(tc, 2 * D_FF), _full((3, 2 * D_FF)), _full((1, 2 * D_FF))],
                outs=[_sds((l, D_FF), BF16)], out_specs=[_rt(tc, D_FF)])[0]
    sv['act'] = act
    x2 = _matmul(name + "_down", [(act, p['w_down'])], l, D_MODEL, resid=x1)
    return x2, sv


def _halo_prev(tm, w):
    return pl.BlockSpec((8, w), lambda i: (jnp.maximum(i * (tm // 8) - 1, 0), 0))


def _halo_next(tm, w, n_tiles):
    last = n_tiles * (tm // 8) - 1
    return pl.BlockSpec((8, w), lambda i: (jnp.minimum((i + 1) * (tm // 8), last), 0))


def _conv(i, tile, halo, cw):
    halo = jnp.where(i > 0, halo, 0.0)
    ext = jnp.concatenate([halo, tile], axis=0)
    n = ext.shape[0]
    x1 = pltpu.roll(ext, 1, 0)[8:]
    x2 = pltpu.roll(ext, 2, 0)[8:]
    del n
    return cw[0:1] * x2 + cw[1:2] * x1 + cw[2:3] * tile


def _layer_bwd(name, dx2, sv, tabs, mem, p):
    l = dx2.shape[0]
    tm = min(512, l)
    nt = l // tm
    g = {}
    x1 = sv['x1']
    dact = _matmul(name + "_b_down", [(dx2, p['w_down'])], l, D_FF, nt=True)
    g['w_down'] = _matmul_tn(name + "_gw_down", sv['act'], dx2)
    tc = min(128, l)
    ntc = l // tc

    def conv_b(i, upt, up_prev, up_next, da, da_next, cw, cb):
        up_prev = jnp.where(i > 0, up_prev, 0.0)
        da_next = jnp.where(i < ntc - 1, da_next, 0.0)
        ext = jnp.concatenate([up_prev, upt, up_next], axis=0)
        x0 = ext[8:]
        xm1 = pltpu.roll(ext, 1, 0)[8:]
        xm2 = pltpu.roll(ext, 2, 0)[8:]
        upc = cw[0:1] * xm2 + cw[1:2] * xm1 + cw[2:3] * x0 + cb
        _, vjp = jax.vjp(lambda a, b: _silu(a) * b, upc[:, :D_FF], upc[:, D_FF:])
        dg, dv = vjp(jnp.concatenate([da, da_next], axis=0))
        dupc = jnp.concatenate([dg, dv], axis=-1)
        n = dupc.shape[0]
        dup = cw[2:3] * dupc[:tc] + cw[1:2] * pltpu.roll(dupc, n - 1, 0)[:tc] + cw[0:1] * pltpu.roll(dupc, n - 2, 0)[:tc]
        dt = dupc[:tc]
        dcw = _row_select([_colsum(dt * xm2[:tc]), _colsum(dt * xm1[:tc]), _colsum(dt * upt)], 8)
        return dup, dcw, _colsum(dt)

    dup, g_cw, g_cb = _rows(
        conv_b, name=name + "_b_conv", n=ntc, ins=[sv['up'], sv['up'], sv['up'], dact, dact, p['conv_w'], p['conv_b']],
        in_specs=[_rt(tc, 2 * D_FF), _halo_prev(tc, 2 * D_FF), _halo_next(tc, 2 * D_FF, ntc), _rt(tc, D_FF),
                  _halo_next(tc, D_FF, ntc), _full((3, 2 * D_FF)), _full((1, 2 * D_FF))],
        outs=[_sds((l, 2 * D_FF)), _sds((8, 2 * D_FF)), _sds((1, 2 * D_FF))],
        out_specs=[_rt(tc, 2 * D_FF), _full((8, 2 * D_FF)), _full((1, 2 * D_FF))], n_acc=2, vmem=56)
    g['conv_w'] = g_cw[:3]
    g['conv_b'] = g_cb
    dh2 = _matmul(name + "_b_up", [(dup, p['w_up'])], l, D_MODEL, nt=True, tm=256)
    g['w_up'] = _matmul_tn(name + "_gw_up", x1, dup, rms_gain=p['norm_ffn_g'])

    def norm_b(i, xv, dh, dres, gn):
        _, vjp = jax.vjp(lambda a, b: _rms(a, b, D_MODEL), xv, gn)
        dxv, dgn = vjp(dh)
        return dres + dxv, dgn

    dx1, g['norm_ffn_g'] = _rows(norm_b, name=name + "_b_norm2", n=nt, ins=[x1, dh2, dx2, p['norm_ffn_g']],
                                 in_specs=[_rt(tm, D_MODEL)] * 3 + [_full((1, D_MODEL))],
                                 outs=[_sds((l, D_MODEL)), _sds((1, D_MODEL))], out_specs=[_rt(tm, D_MODEL), _full((1, D_MODEL))],
                                 n_acc=1)

    tg = min(256, l)

    def merge_b(i, dx, oa, zo, oc, pg, woa, wos, woc, bg, wout):
        dm = _mm_nt(dx, wout)
        gates = jax.nn.sigmoid(pg + bg)
        ys = [_mm(oa, woa), _mm(zo, wos), _mm(oc, woc)]
        dys, dpg = [], []
        for b in range(3):
            gb = gates[:, b * D_MODEL:(b + 1) * D_MODEL]
            dys.append(dm * gb)
            dpg.append(dm * ys[b] * gb * (1.0 - gb))
        dpg = jnp.concatenate(dpg, axis=-1)
        return (_mm_nt(dys[0], woa), _mm_nt(dys[1], wos), _mm_nt(dys[2], woc), dpg, dys[0], dys[1], dys[2], _colsum(dpg))

    merge_consts = [p['w_oa'], p['w_o_ssm'], p['w_o_cross'], p['b_gate'], p['w_out']]
    (do_a, dzo, do_c, dp_g, dy_a, dy_b, dy_c, g['b_gate']) = _rows(
        merge_b, name=name + "_b_merge", n=l // tg, ins=[dx1, sv['o_a'], sv['zo'], sv['o_c'], sv['p_g'], *merge_consts],
        in_specs=[_rt(tg, D_MODEL), _rt(tg, MLA_PAD), _rt(tg, SSM_WIDTH), _rt(tg, X_WIDTH), _rt(tg, 3 * D_MODEL)]
        + [_full(a.shape) for a in merge_consts],
        outs=[_sds((l, MLA_PAD)), _sds((l, SSM_WIDTH)), _sds((l, X_WIDTH)), _sds((l, 3 * D_MODEL)),
              _sds((l, D_MODEL), BF16), _sds((l, D_MODEL), BF16), _sds((l, D_MODEL), BF16), _sds((1, 3 * D_MODEL))],
        out_specs=[_rt(tg, MLA_PAD), _rt(tg, SSM_WIDTH), _rt(tg, X_WIDTH), _rt(tg, 3 * D_MODEL),
                   _rt(tg, D_MODEL), _rt(tg, D_MODEL), _rt(tg, D_MODEL), _full((1, 3 * D_MODEL))], n_acc=1, vmem=56)
    g['w_out'] = _matmul_tn(name + "_gw_out", sv['merged'], dx1)
    g['w_oa'] = _matmul_tn(name + "_gw_oa", sv['o_a'], dy_a)
    g['w_o_ssm'] = _matmul_tn(name + "_gw_os", sv['zo'], dy_b)
    g['w_o_cross'] = _matmul_tn(name + "_gw_oc", sv['o_c'], dy_c)

    k_c, v_c = sv['k_c'], sv['v_c']
    m_len = k_c.shape[0]

    def cross_b(i, xq, do, kc, vc, xqg):
        dxq, dk, dv = [], [], []
        dg = jnp.zeros((1, LANE), F32)
        for h in range(X_HEADS):
            sl = slice(h * LANE, (h + 1) * LANE)
            qh, vjp = jax.vjp(lambda a, b: _rms(a, b, X_HEAD_DIM), xq[:, sl], xqg)
            sc = X_HEAD_DIM ** -0.5
            s = _mm_nt(qh, kc[:, sl]) * sc
            s = s - jnp.max(s, axis=-1, keepdims=True)
            e = jnp.exp(s)
            pr = e / jnp.sum(e, axis=-1, keepdims=True)
            doh = do[:, sl]
            dv.append(_mm_tn(pr, doh))
            dp = _mm_nt(doh, vc[:, sl])
            ds = pr * (dp - jnp.sum(dp * pr, axis=-1, keepdims=True)) * sc
            dk.append(_mm_tn(ds, qh))
            dxh, dgh = vjp(_mm(ds, kc[:, sl]))
            dxq.append(dxh)
            dg = dg + dgh
        return jnp.concatenate(dxq, axis=-1), jnp.concatenate(dk, axis=-1), jnp.concatenate(dv, axis=-1), dg

    dp_xq, dk_c, dv_c, g['xq_norm_g'] = _rows(
        cross_b, name=name + "_b_cross", n=nt, ins=[sv['p_xq'], do_c, k_c, v_c, p['xq_norm_g']],
        in_specs=[_rt(tm, X_WIDTH), _rt(tm, X_WIDTH), _full(k_c.shape), _full(v_c.shape), _full((1, LANE))],
        outs=[_sds((l, X_WIDTH)), _sds((m_len, X_WIDTH)), _sds((m_len, X_WIDTH)), _sds((1, LANE))],
        out_specs=[_rt(tm, X_WIDTH), _full((m_len, X_WIDTH)), _full((m_len, X_WIDTH)), _full((1, LANE))], n_acc=3)

    def memkv_b(i, mm_, dk, dv, mg, wmk, xkg):
        memn, vjp_n = jax.vjp(lambda a, b: _rms(a, b, D_MODEL), mm_, mg)
        kv = _mm(memn, wmk)
        _, vjp_k = jax.vjp(lambda a, b: _head_rms(a, b, X_HEADS, X_HEAD_DIM), kv[:, :X_WIDTH], xkg)
        dkr, dxkg = vjp_k(dk)
        dkv = jnp.concatenate([dkr, dv], axis=-1)
        _, dmg = vjp_n(_mm_nt(dkv, wmk))
        return _mm_tn(memn, dkv), dmg, dxkg

    mem_consts = [p['mem_norm_g'], p['w_mem_kv'], p['xk_norm_g']]
    g['w_mem_kv'], g['mem_norm_g'], g['xk_norm_g'] = _rows(
        memkv_b, name=name + "_b_memkv", n=1, ins=[mem, dk_c, dv_c, *mem_consts],
        in_specs=[_full(mem.shape), _full(dk_c.shape), _full(dv_c.shape)] + [_full(a.shape) for a in mem_consts],
        outs=[_sds((D_MODEL, 2 * X_WIDTH)), _sds((1, D_MODEL)), _sds((1, LANE))],
        out_specs=[_full((D_MODEL, 2 * X_WIDTH)), _full((1, D_MODEL)), _full((1, LANE))])

    u_p = sv['u_p']
    dzo_p = _to_perm(dzo, l)
    s_re, s_im = sv['s_re'], sv['s_im']

    def glu_b(i, sr, si, u, dz, cm, dsk, wg, bg):
        cats = [jnp.concatenate([sr[:, j * 512:(j + 1) * 512], si[:, j * 512:(j + 1) * 512]], axis=-1) for j in range(SSM_JB)]
        y = jnp.concatenate([_mm(cats[j], cm[j]) for j in range(SSM_JB)], axis=-1) + dsk * u
        zz, vjp_g = jax.vjp(_gelu, y)
        t = _mm(zz, wg) + bg
        sg = jax.nn.sigmoid(t)
        dt = dz * zz * sg * (1.0 - sg)
        dzz = dz * sg + _mm_nt(dt, wg)
        dy = vjp_g(dzz)[0]
        dss = [_mm_nt(dy[:, j * LANE:(j + 1) * LANE], cm[j]) for j in range(SSM_JB)]
        dsr = jnp.concatenate([d[:, :512] for d in dss], axis=-1)
        dsi = jnp.concatenate([d[:, 512:] for d in dss], axis=-1)
        dcm = jnp.stack([_mm_tn(cats[j], dy[:, j * LANE:(j + 1) * LANE]) for j in range(SSM_JB)], axis=0)
        return dsr, dsi, dy * dsk, dcm, _colsum(dy * u), _mm_tn(zz, dt), _colsum(dt)

    glu_consts = [p['c_mat'], p['ssm_d'], p['w_glu'], p['b_glu']]
    ts = min(256, l)
    nts = l // ts
    ds_re, ds_im, du_dir, g['c_mat'], g['ssm_d'], g['w_glu'], g['b_glu'] = _rows(
        glu_b, name=name + "_b_glu", n=nts, ins=[s_re, s_im, u_p, dzo_p, *glu_consts],
        in_specs=[_rt(ts, SSM_LANES), _rt(ts, SSM_LANES), _rt(ts, SSM_WIDTH), _rt(ts, SSM_WIDTH)] + [_full(a.shape) for a in glu_consts],
        outs=[_sds((l, SSM_LANES)), _sds((l, SSM_LANES)), _sds((l, SSM_WIDTH)), _sds((SSM_JB, 1024, LANE)), _sds((1, SSM_WIDTH)),
              _sds((SSM_WIDTH, SSM_WIDTH)), _sds((1, SSM_WIDTH))],
        out_specs=[_rt(ts, SSM_LANES), _rt(ts, SSM_LANES), _rt(ts, SSM_WIDTH), _full((SSM_JB, 1024, LANE)), _full((1, SSM_WIDTH)),
                   _full((SSM_WIDTH, SSM_WIDTH)), _full((1, SSM_WIDTH))], n_acc=4)
    gb_re, gb_im = _scan(name + "_b_scan", ds_re, ds_im, sv['a_re'], -sv['a_im'], reverse=True)
    ns = SCAN_SEGS
    last_blk = l // ns - 1

    def da_fn(i, gr, gi, sr, si, hr, hi, lr_, li_):
        rid = lax.broadcasted_iota(jnp.int32, lr_.shape, 0)
        fr = jnp.where(rid == 0, 0.0, pltpu.roll(lr_, 1, 0))
        fi = jnp.where(rid == 0, 0.0, pltpu.roll(li_, 1, 0))
        hr = jnp.where(i == 0, fr, hr)
        hi = jnp.where(i == 0, fi, hi)
        if ts > ns:
            pr = jnp.concatenate([hr, sr[:ts - ns]], axis=0)
            pi = jnp.concatenate([hi, si[:ts - ns]], axis=0)
        else:
            pr, pi = hr, hi
        return _colsum(gr * pr + gi * pi), _colsum(gi * pr - gr * pi)

    hprev = pl.BlockSpec((ns, SSM_LANES), lambda i: (jnp.maximum(i * (ts // ns) - 1, 0), 0))
    hlast = pl.BlockSpec((ns, SSM_LANES), lambda i: (last_blk, 0))
    da_re, da_im = _rows(da_fn, name=name + "_b_da", n=nts, ins=[gb_re, gb_im, s_re, s_im, s_re, s_im, s_re, s_im],
                         in_specs=[_rt(ts, SSM_LANES)] * 4 + [hprev, hprev, hlast, hlast],
                         outs=[_sds((1, SSM_LANES))] * 2, out_specs=[_full((1, SSM_LANES))] * 2, n_acc=2)

    def bu_b(i, dbr, dbi, u, dud, bm):
        dus, dbm = [], []
        for j in range(SSM_JB):
            cat = jnp.concatenate([dbr[:, j * 512:(j + 1) * 512], dbi[:, j * 512:(j + 1) * 512]], axis=-1)
            dus.append(_mm_nt(cat, bm[j]))
            dbm.append(_mm_tn(u[:, j * LANE:(j + 1) * LANE], cat))
        return dud + jnp.concatenate(dus, axis=-1), jnp.stack(dbm, axis=0)

    du_p, d_bmat = _rows(bu_b, name=name + "_b_bu", n=nts, ins=[gb_re, gb_im, u_p, du_dir, sv['b_mat']],
                         in_specs=[_rt(ts, SSM_LANES), _rt(ts, SSM_LANES), _rt(ts, SSM_WIDTH), _rt(ts, SSM_WIDTH),
                                   _full(sv['b_mat'].shape)],
                         outs=[_sds((l, SSM_WIDTH)), _sds((SSM_JB, LANE, 1024))],
                         out_specs=[_rt(ts, SSM_WIDTH), _full((SSM_JB, LANE, 1024))], n_acc=1)
    dp_u = _from_perm(du_p, l)
    dbb_re = _blockdiag_t(d_bmat[:, :, :512], SSM_GROUP_CH, SSM_STATE).reshape(SSM_GROUPS, SSM_GROUP_CH, SSM_STATE).transpose(1, 0, 2)
    dbb_im = _blockdiag_t(d_bmat[:, :, 512:], SSM_GROUP_CH, SSM_STATE).reshape(SSM_GROUPS, SSM_GROUP_CH, SSM_STATE).transpose(1, 0, 2)
    g['lr'], g['li'], g['log_dt'], g['br'], g['bi'] = _ssm_params_bwd(
        name + "_b_ssm_par", p['lr'], p['li'], p['log_dt'], p['br'], p['bi'],
        da_re.reshape(SSM_GROUPS, SSM_STATE), da_im.reshape(SSM_GROUPS, SSM_STATE), dbb_re, dbb_im)

    dq_t, dk, dv = _flash_bwd(name + "_b_attn", sv['q'], sv['k'], sv['v'], sv['q_t'], sv['k_t'], sv['o_a'], sv['lse_t'], do_a)

    def qkv_b(i, ps, c, s1, s2, dq_, dk_, dv_, qag, wqb, kvag, wk, wv, qng, kng):
        c_q = ps[:, :Q_LORA]
        c_kv = ps[:, Q_LORA:Q_LORA + KV_LORA]
        kr = ps[:, Q_LORA + KV_LORA:]
        cqn, vjp_cq = jax.vjp(lambda a, b: _rms(a, b, Q_LORA), c_q, qag)
        ckvn, vjp_ckv = jax.vjp(lambda a, b: _rms(a, b, KV_LORA), c_kv, kvag)
        q_raw = _mm(cqn, wqb)
        k_raw = _mm(ckvn, wk) + jnp.concatenate([kr] * MLA_HEADS, axis=-1)
        _, vjp_qn = jax.vjp(lambda a, b: _head_rms(a, b, MLA_HEADS, D_QK), q_raw, qng)
        _, vjp_kn = jax.vjp(lambda a, b: _head_rms(a, b, MLA_HEADS, D_QK), k_raw, kng)
        dq_raw, dqng = vjp_qn(_heads(_rope_t, jnp.transpose(dq_[0]), MLA_HEADS, c, s1, s2))
        dk_raw, dkng = vjp_kn(_heads(_rope_t, dk_, MLA_HEADS, c, s1, s2))
        dkr = dk_raw[:, :LANE]
        for h in range(1, MLA_HEADS):
            dkr = dkr + dk_raw[:, h * LANE:(h + 1) * LANE]
        dcq, dqag = vjp_cq(_mm_nt(dq_raw, wqb))
        dckv, dkvag = vjp_ckv(_mm_nt(dk_raw, wk) + _mm_nt(dv_, wv))
        dps = jnp.concatenate([dcq, dckv, dkr], axis=-1)
        return (dps, _mm_tn(cqn, dq_raw), _mm_tn(ckvn, dk_raw), _mm_tn(ckvn, dv_), dqag, dkvag, dqng, dkng)

    qkv_consts = [p['q_a_norm_g'], p['w_qb'], p['kv_a_norm_g'], p['w_k'], p['w_v'], p['q_norm_g'], p['k_norm_g']]
    (dp_s, g['w_qb'], g['w_k'], g['w_v'], g['q_a_norm_g'], g['kv_a_norm_g'], g['q_norm_g'], g['k_norm_g']) = _rows(
        qkv_b, name=name + "_b_qkv", n=nt, ins=[sv['p_s'], *tabs, dq_t, dk, dv, *qkv_consts],
        in_specs=[_rt(tm, SMALL_W)] + [_rt(tm, LANE)] * 3 + [pl.BlockSpec((1, MLA_PAD, tm), lambda i: (i, 0, 0))]
        + [_rt(tm, MLA_PAD)] * 2 + [_full(a.shape) for a in qkv_consts],
        outs=[_sds((l, SMALL_W)), _sds((Q_LORA, MLA_PAD)), _sds((KV_LORA, MLA_PAD)), _sds((KV_LORA, MLA_PAD)),
              _sds((1, Q_LORA)), _sds((1, KV_LORA)), _sds((1, LANE)), _sds((1, LANE))],
        out_specs=[_rt(tm, SMALL_W), _full((Q_LORA, MLA_PAD)), _full((KV_LORA, MLA_PAD)), _full((KV_LORA, MLA_PAD)),
                   _full((1, Q_LORA)), _full((1, KV_LORA)), _full((1, LANE)), _full((1, LANE))], n_acc=7)

    x0 = sv['x0']
    dh = _matmul(name + "_b_in", [(dp_g, p['w_g']), (dp_u, p['w_u']), (dp_xq, p['w_xq']), (dp_s, p['w_s'])], l, D_MODEL, nt=True,
                 tm=256)
    gm = p['norm_mix_g']
    g['w_g'] = _matmul_tn(name + "_gw_g", x0, dp_g, rms_gain=gm)
    g['w_u'] = _matmul_tn(name + "_gw_u", x0, dp_u, rms_gain=gm)
    g['w_xq'] = _matmul_tn(name + "_gw_xq", x0, dp_xq, rms_gain=gm)
    g['w_s'] = _matmul_tn(name + "_gw_s", x0, dp_s, rms_gain=gm)
    dx0, g['norm_mix_g'] = _rows(norm_b, name=name + "_b_norm1", n=nt, ins=[x0, dh, dx1, gm],
                                 in_specs=[_rt(tm, D_MODEL)] * 3 + [_full((1, D_MODEL))],
                                 outs=[_sds((l, D_MODEL)), _sds((1, D_MODEL))], out_specs=[_rt(tm, D_MODEL), _full((1, D_MODEL))],
                                 n_acc=1)
    return dx0, g


def _unprep_grads(g):
    o = {}
    ws = g['w_s']
    o['w_in'] = jnp.concatenate([ws[:, :Q_LORA + KV_LORA], ws[:, Q_LORA + KV_LORA + D_NOPE:Q_LORA + KV_LORA + D_QK],
                                 g['w_u'], g['w_xq'], g['w_g']], axis=1)
    o['w_q_b'] = g['w_qb'].reshape(Q_LORA, MLA_HEADS, HEAD_PAD)[:, :, :D_QK].reshape(Q_LORA, MLA_HEADS * D_QK)
    gk = g['w_k'].reshape(KV_LORA, MLA_HEADS, HEAD_PAD)[:, :, :D_NOPE]
    gv = g['w_v'].reshape(KV_LORA, MLA_HEADS, HEAD_PAD)[:, :, :D_V]
    o['w_kv_b'] = jnp.concatenate([gk, gv], axis=2).reshape(KV_LORA, MLA_HEADS * (D_NOPE + D_V))
    o['w_o_mla'] = g['w_oa'].reshape(MLA_HEADS, HEAD_PAD, D_MODEL)[:, :D_V].reshape(MLA_HEADS * D_V, D_MODEL)
    for n in ('w_glu', 'w_o_ssm', 'w_mem_kv', 'w_o_cross', 'w_out', 'w_up', 'w_down', 'conv_w'):
        o[n] = g[n]
    for n in ('norm_mix_g', 'q_a_norm_g', 'kv_a_norm_g', 'b_glu', 'mem_norm_g', 'xq_norm_g', 'xk_norm_g', 'b_gate',
              'norm_ffn_g', 'conv_b'):
        o[n] = g[n].reshape(-1)
    o['q_norm_g'] = g['q_norm_g'].reshape(-1)[:D_QK]
    o['k_norm_g'] = g['k_norm_g'].reshape(-1)[:D_QK]
    o['ssm_d'] = g['ssm_d'].reshape(SSM_GROUPS, SSM_GROUP_CH)
    o['ssm_lambda_re'] = g['lr']
    o['ssm_lambda_im'] = g['li']
    o['ssm_log_dt'] = g['log_dt'].reshape(SSM_GROUPS)
    o['ssm_b_re'] = g['br'].transpose(1, 2, 0)
    o['ssm_b_im'] = g['bi'].transpose(1, 2, 0)
    dc = g['c_mat']
    o['ssm_c_re'] = _blockdiag_t(dc[:, :512], SSM_STATE, SSM_GROUP_CH).transpose(0, 1, 3, 2).reshape(SSM_GROUPS, SSM_GROUP_CH, SSM_STATE)
    o['ssm_c_im'] = -_blockdiag_t(dc[:, 512:], SSM_STATE, SSM_GROUP_CH).transpose(0, 1, 3, 2).reshape(SSM_GROUPS, SSM_GROUP_CH, SSM_STATE)
    return o


def _local_step(x, mem, pos, target, w):
    l = x.shape[0]
    tm = min(512, l)
    tabs = _rope_tables(pos.astype(F32).reshape(l, 1))
    ps = [_prep_layer(w, i) for i in range(DEPTH)]
    saved = []
    h = x
    for i in range(DEPTH):
        h, sv = _layer_fwd("l%d" % i, h, tabs, mem, ps[i])
        saved.append(sv)

    def loss_fn(i, y, t):
        e = y - t
        per_tok = jnp.sum(e * e, axis=-1, keepdims=True) * (1.0 / D_MODEL)
        tot = 0.5 * jnp.sum(per_tok, axis=0, keepdims=True)
        return e * (1.0 / D_MODEL), jnp.broadcast_to(tot, (1, LANE))

    dy, loss = _rows(loss_fn, name="loss", n=l // tm, ins=[h, target], in_specs=[_rt(tm, D_MODEL)] * 2,
                     outs=[_sds((l, D_MODEL)), _sds((1, LANE))], out_specs=[_rt(tm, D_MODEL), _full((1, LANE))], n_acc=1)
    grads = []
    d = dy
    for i in reversed(range(DEPTH)):
        d, g = _layer_bwd("l%d" % i, d, saved[i], tabs, mem, ps[i])
        grads.append(_unprep_grads(g))
    return loss[0, 0], d, grads[::-1]


def _sum_picked(name, slots, pick, extra, out_dtype):
    _, r, c = slots.shape
    e = extra.shape[0]
    tr = _row_tile(r)

    def body(pk, s_ref, x_ref, o_ref):
        acc = s_ref[...].astype(F32)
        for k in range(e):
            acc = acc + x_ref[k].astype(F32)
        o_ref[...] = acc.astype(o_ref.dtype)

    grid_spec = pltpu.PrefetchScalarGridSpec(
        num_scalar_prefetch=1, grid=(r // tr,),
        in_specs=[pl.BlockSpec((None, tr, c), lambda i, pk: (pk[0], i, 0)), pl.BlockSpec((e, tr, c), lambda i, pk: (0, i, 0))],
        out_specs=pl.BlockSpec((tr, c), lambda i, pk: (i, 0)))
    return pl.pallas_call(body, name=name, grid_spec=grid_spec, out_shape=_sds((r, c), out_dtype),
                          compiler_params=_params(("arbitrary",), 48))(pick, slots, extra)


def _row_tile(r):
    for t in (256, 128, 64, 32, 16, 8):
        if r % t == 0:
            return t
    return r


def _adamw(name, parts, w, m, v):
    r, cw = w.shape
    tr = _row_tile(r)
    np_ = len(parts)

    def fn(i, *vals):
        wv, mv, vv = vals[np_:]
        terms = []
        for pv in vals[:np_]:
            terms += [pv] if pv.ndim == 2 else [pv[k] for k in range(pv.shape[0])]
        g = terms[0]
        for t in terms[1:]:
            g = g + t
        mn = ADAM_B1 * mv + (1.0 - ADAM_B1) * g
        vn = ADAM_B2 * vv + (1.0 - ADAM_B2) * (g * g)
        m_hat = mn / (1.0 - ADAM_B1 ** ADAM_STEP)
        v_hat = vn / (1.0 - ADAM_B2 ** ADAM_STEP)
        delta = -ADAM_LR * (m_hat / (jnp.sqrt(v_hat) + ADAM_EPS) + ADAM_WD * wv)
        return g, delta, mn, vn

    pspecs = [_rt(tr, cw) if p.ndim == 2 else pl.BlockSpec((p.shape[0], tr, cw), lambda i: (0, i, 0)) for p in parts]
    return _rows(fn, name=name, n=r // tr, ins=[*parts, w, m, v], in_specs=pspecs + [_rt(tr, cw)] * 3,
                 outs=[_sds((r, cw))] * 4, out_specs=[_rt(tr, cw)] * 4)


def _shard_of(a, axis, k):
    n = a.shape[axis] // 4
    return lax.slice_in_dim(a, k * n, (k + 1) * n, axis=axis)


def _step(a):
    x = a['x'][0]
    mem = a['mem'][0]
    pos = a['positions'][0]
    target = a['loss_target'][0]

    me = 2 * lax.axis_index("x") + lax.axis_index("y")

    mine = [a[n] if n == 'conv_w' else a[n].astype(BF16) for n in SHARDED]
    got = _gather_d2d(_gather_ici(mine))
    w = {}
    for n, own, y in zip(SHARDED, mine, got):
        ax = SHARD_AXIS[n] - 1
        w[n] = [jnp.concatenate([jnp.where(me == k, own[i], y[k, i]) for k in range(4)], axis=ax) for i in range(DEPTH)]
    for n in SMALL:
        w[n] = a[n]

    loss, grad_x, grads = _local_step(x, mem, pos, target, w)

    mc = lax.axis_index("c")
    mc1 = mc.astype(jnp.int32).reshape(1)
    me1 = me.astype(jnp.int32).reshape(1)
    gsh = []
    for n in SHARDED:
        ax = SHARD_AXIS[n] - 1
        gsh.append(jnp.stack([jnp.stack([_shard_of(grads[i][n], ax, k) for k in range(4)], axis=0)
                              for i in range(DEPTH)], axis=0).astype(BF16))
    sib = _swap_d2d("comm_reduce_pair", gsh, other_layer=True)
    pair = []
    for n, g, s in zip(SHARDED, gsh, sib):
        rows, cols = g.shape[-2:]
        pair.append(_sum_picked("sum2_" + n, g.reshape(DEPTH, 4 * rows, cols), mc1, s.reshape(1, 4 * rows, cols), BF16)
                    .reshape(4, rows, cols))
    got = _reduce_ici(pair)
    parts = [_sum_picked("sum4_" + n, p4, me1, g3, F32) for n, p4, g3 in zip(SHARDED, pair, got)]
    others = _swap_d2d("comm_reduce_d2d", parts, other_layer=False)
    res_sh = []
    for n, part, other in zip(SHARDED, parts, others):
        cols = part.shape[-1]
        full = jnp.where(mc == 0, jnp.stack([part, other], axis=0), jnp.stack([other, part], axis=0))
        res = _adamw("adamw_" + n, [full.reshape(-1, cols)], *[a[pre + n].reshape(-1, cols) for pre in ('', 'm_', 'v_')])
        res_sh.append([r.reshape(a[n].shape) for r in res])
    res_sh = [[res_sh[j][kind] for j in range(len(SHARDED))] for kind in range(4)]

    sm_shapes = [a[n].shape for n in SMALL] + [(1,)]
    gsm = _pack([jnp.stack([grads[i][n] for i in range(DEPTH)], axis=0) for n in SMALL] + [loss.reshape(1)], 8, F32)
    alls = _all_exchange("comm_reduce_small", gsm)
    zero1 = jnp.zeros((1,), F32)
    res_sm = _adamw("adamw_small", [alls], *[_pack([a[pre + n] for n in SMALL] + [zero1], 8, F32) for pre in ('', 'm_', 'v_')])
    res_sm = [_unpack(r, sm_shapes) for r in res_sm]
    loss = res_sm[0][-1][0]

    outs = [loss, grad_x[None]]
    for kind in range(4):
        byname = dict(zip(SHARDED, res_sh[kind]))
        byname.update(zip(SMALL, res_sm[kind]))
        outs += [byname[n] for n in WEIGHTS]
    return tuple(outs)


def kernel(x, mem, positions, norm_mix_g, w_in, q_a_norm_g, w_q_b, kv_a_norm_g, w_kv_b, q_norm_g, k_norm_g, w_o_mla, ssm_lambda_re, ssm_lambda_im, ssm_log_dt, ssm_b_re, ssm_b_im, ssm_c_re, ssm_c_im, ssm_d, w_glu, b_glu, w_o_ssm, mem_norm_g, w_mem_kv, xq_norm_g, xk_norm_g, w_o_cross, b_gate, w_out, norm_ffn_g, w_up, conv_w, conv_b, w_down, loss_target, m_norm_mix_g, m_w_in, m_q_a_norm_g, m_w_q_b, m_kv_a_norm_g, m_w_kv_b, m_q_norm_g, m_k_norm_g, m_w_o_mla, m_ssm_lambda_re, m_ssm_lambda_im, m_ssm_log_dt, m_ssm_b_re, m_ssm_b_im, m_ssm_c_re, m_ssm_c_im, m_ssm_d, m_w_glu, m_b_glu, m_w_o_ssm, m_mem_norm_g, m_w_mem_kv, m_xq_norm_g, m_xk_norm_g, m_w_o_cross, m_b_gate, m_w_out, m_norm_ffn_g, m_w_up, m_conv_w, m_conv_b, m_w_down, v_norm_mix_g, v_w_in, v_q_a_norm_g, v_w_q_b, v_kv_a_norm_g, v_w_kv_b, v_q_norm_g, v_k_norm_g, v_w_o_mla, v_ssm_lambda_re, v_ssm_lambda_im, v_ssm_log_dt, v_ssm_b_re, v_ssm_b_im, v_ssm_c_re, v_ssm_c_im, v_ssm_d, v_w_glu, v_b_glu, v_w_o_ssm, v_mem_norm_g, v_w_mem_kv, v_xq_norm_g, v_xk_norm_g, v_w_o_cross, v_b_gate, v_w_out, v_norm_ffn_g, v_w_up, v_conv_w, v_conv_b, v_w_down):
    return _step(dict(locals()))
```

```python
import functools
import math

import numpy as np
import jax
import jax.numpy as jnp
from jax import lax
from jax.experimental import pallas as pl
from jax.experimental.pallas import tpu as pltpu

F32 = jnp.float32
BF16 = jnp.bfloat16
MESH = pl.DeviceIdType.MESH

DEPTH = 2
D_MODEL = 1024
EPS = 1e-6
MLA_HEADS = 8
Q_LORA = 384
KV_LORA = 256
D_NOPE = 64
D_ROPE = 32
D_QK = D_NOPE + D_ROPE
D_V = 64
HEAD_PAD = 128
MLA_PAD = MLA_HEADS * HEAD_PAD
ROPE_THETA = 10000.0
SSM_GROUPS = 32
SSM_GROUP_CH = 16
SSM_WIDTH = 512
SSM_STATE = 64
SSM_LANES = SSM_GROUPS * SSM_STATE
SSM_JB = 4
X_HEADS = 4
X_HEAD_DIM = 128
X_WIDTH = 512
D_FF = 2816
SMALL_W = Q_LORA + KV_LORA + HEAD_PAD
SCAN_SEGS = 32
LANE = 128
NEG = -1e30

ADAM_LR = 0.001
ADAM_B1 = 0.9
ADAM_B2 = 0.999
ADAM_EPS = 1e-08
ADAM_WD = 0.01
ADAM_STEP = 10

WEIGHTS = ['norm_mix_g', 'w_in', 'q_a_norm_g', 'w_q_b', 'kv_a_norm_g', 'w_kv_b', 'q_norm_g', 'k_norm_g', 'w_o_mla',
           'ssm_lambda_re', 'ssm_lambda_im', 'ssm_log_dt', 'ssm_b_re', 'ssm_b_im', 'ssm_c_re', 'ssm_c_im', 'ssm_d',
           'w_glu', 'b_glu', 'w_o_ssm', 'mem_norm_g', 'w_mem_kv', 'xq_norm_g', 'xk_norm_g', 'w_o_cross', 'b_gate',
           'w_out', 'norm_ffn_g', 'w_up', 'conv_w', 'conv_b', 'w_down']
SHARD_AXIS = {'w_in': 2, 'w_q_b': 2, 'w_kv_b': 2, 'w_o_mla': 2, 'w_glu': 1, 'w_o_ssm': 2, 'w_mem_kv': 1,
              'w_o_cross': 2, 'w_out': 1, 'w_up': 2, 'conv_w': 2, 'w_down': 1}
SHARDED = [n for n in WEIGHTS if n in SHARD_AXIS]
GATHER_BF16 = [n for n in SHARDED if n != 'conv_w']
SMALL = [n for n in WEIGHTS if n not in SHARD_AXIS]


def _bf(v):
    return v.astype(BF16)


def _mm(a, b):
    return jnp.dot(_bf(a), _bf(b), preferred_element_type=F32)


def _mm_nt(a, b):
    return lax.dot_general(_bf(a), _bf(b), (((1,), (1,)), ((), ())), preferred_element_type=F32)


def _mm_tn(a, b):
    return lax.dot_general(_bf(a), _bf(b), (((0,), (0,)), ((), ())), preferred_element_type=F32)


def _rms(v, g, n):
    ms = jnp.sum(v * v, axis=-1, keepdims=True) * (1.0 / n)
    return (v * lax.rsqrt(ms + EPS)) * g


def _head_rms(v, g, heads, n):
    return jnp.concatenate([_rms(v[:, h * LANE:(h + 1) * LANE], g, n) for h in range(heads)], axis=-1)


def _rope(v, c, s1, s2):
    return v * c + pltpu.roll(v, LANE - 16, 1) * s1 + pltpu.roll(v, 16, 1) * s2


def _rope_t(g, c, s1, s2):
    return g * c + pltpu.roll(g * s1, 16, 1) + pltpu.roll(g * s2, LANE - 16, 1)


def _heads(fn, v, heads, *tabs):
    return jnp.concatenate([fn(v[:, h * LANE:(h + 1) * LANE], *tabs) for h in range(heads)], axis=-1)


def _gelu(y):
    return y * (0.5 * (1.0 + jnp.tanh(math.sqrt(2.0 / math.pi) * (y + 0.044715 * (y * y * y)))))


def _silu(g):
    return g * jax.nn.sigmoid(g)


def _colsum(v):
    return jnp.sum(v, axis=0, keepdims=True)


def _row_select(rows, n):
    rid = lax.broadcasted_iota(jnp.int32, (n, rows[0].shape[-1]), 0)
    out = jnp.zeros((n, rows[0].shape[-1]), F32)
    for k, r in enumerate(rows):
        out = jnp.where(rid == k, jnp.broadcast_to(r, out.shape), out)
    return out


def _params(sem, vmem_mb):
    return pltpu.CompilerParams(dimension_semantics=sem, vmem_limit_bytes=vmem_mb * 1024 * 1024)


def _rt(tm, w, cb=0):
    return pl.BlockSpec((tm, w), lambda i: (i, cb))


def _full(shape):
    nd = len(shape)
    return pl.BlockSpec(tuple(shape), lambda i: (0,) * nd)


def _rows(fn, *, name, n, ins, in_specs, outs, out_specs, n_acc=0, vmem=48):
    n_in = len(ins)
    n_out = len(outs)

    def body(*refs):
        i = pl.program_id(0)
        res = fn(i, *[r[...] for r in refs[:n_in]])
        if not isinstance(res, (tuple, list)):
            res = (res,)
        assert len(res) == n_out, (name, len(res), n_out)
        for k, (r, v) in enumerate(zip(refs[n_in:], res)):
            if k < n_out - n_acc:
                r[...] = v.astype(r.dtype)
            else:
                @pl.when(i == 0)
                def _():
                    r[...] = v

                @pl.when(i > 0)
                def _():
                    r[...] += v

    return pl.pallas_call(
        body, name=name, grid=(n,), in_specs=list(in_specs), out_specs=tuple(out_specs), out_shape=tuple(outs),
        compiler_params=_params(("arbitrary",), vmem))(*ins)


def _sds(shape, dtype=F32):
    return jax.ShapeDtypeStruct(tuple(shape), dtype)


def _tile_n(n, cap=1536):
    best = None
    for t in range(LANE, min(n, cap) + 1, LANE):
        if n % t == 0:
            best = t
    if best is None or n <= 1408:
        return n
    return best


def _matmul(name, pairs, m, n, *, nt=False, rms_gain=None, resid=None, out_dtype=F32, tm=512, vmem=56):
    tm = min(tm, m)
    tn = _tile_n(n)
    ks = [a.shape[1] for a, _ in pairs]
    np_ = len(pairs)

    def body(*refs):
        a_refs = refs[:np_]
        b_refs = refs[np_:2 * np_]
        k = 2 * np_
        g_ref = None
        r_ref = None
        if rms_gain is not None:
            g_ref = refs[k]
            k += 1
        if resid is not None:
            r_ref = refs[k]
            k += 1
        o_ref = refs[k]
        scr = refs[k + 1:]
        j = pl.program_id(1)

        @pl.when(j == 0)
        def _():
            for p in range(np_):
                a = a_refs[p][...]
                if p == 0 and g_ref is not None:
                    a = _rms(a.astype(F32), g_ref[...], ks[0])
                scr[p][...] = a.astype(BF16)

        acc = None
        for p in range(np_):
            b = b_refs[p][...].astype(BF16)
            if nt:
                t = lax.dot_general(scr[p][...], b, (((1,), (1,)), ((), ())), preferred_element_type=F32)
            else:
                t = jnp.dot(scr[p][...], b, preferred_element_type=F32)
            acc = t if acc is None else acc + t
        if r_ref is not None:
            acc = acc + r_ref[...]
        o_ref[...] = acc.astype(o_ref.dtype)

    in_specs = [pl.BlockSpec((tm, kk), lambda i, j: (i, 0)) for kk in ks]
    if nt:
        in_specs += [pl.BlockSpec((tn, kk), lambda i, j: (j, 0)) for kk in ks]
    else:
        in_specs += [pl.BlockSpec((kk, tn), lambda i, j: (0, j)) for kk in ks]
    ins = [a for a, _ in pairs] + [b for _, b in pairs]
    if rms_gain is not None:
        in_specs.append(pl.BlockSpec((1, ks[0]), lambda i, j: (0, 0)))
        ins.append(rms_gain)
    if resid is not None:
        in_specs.append(pl.BlockSpec((tm, tn), lambda i, j: (i, j)))
        ins.append(resid)
    return pl.pallas_call(
        body, name=name, grid=(m // tm, n // tn), in_specs=in_specs,
        out_specs=pl.BlockSpec((tm, tn), lambda i, j: (i, j)), out_shape=_sds((m, n), out_dtype),
        scratch_shapes=[pltpu.VMEM((tm, kk), BF16) for kk in ks],
        compiler_params=_params(("arbitrary", "arbitrary"), vmem))(*ins)


def _matmul_tn(name, a, b, *, rms_gain=None, tl=512, vmem=56):
    l, ka = a.shape
    n = b.shape[1]
    tl = min(tl, l)
    tn = _tile_n(n, 1536)

    def body(*refs):
        if rms_gain is not None:
            a_ref, b_ref, g_ref, o_ref = refs
        else:
            a_ref, b_ref, o_ref = refs
        t = pl.program_id(1)
        av = a_ref[...]
        if rms_gain is not None:
            av = _rms(av.astype(F32), g_ref[...], ka)
        v = _mm_tn(av, b_ref[...])

        @pl.when(t == 0)
        def _():
            o_ref[...] = v

        @pl.when(t > 0)
        def _():
            o_ref[...] += v

    in_specs = [pl.BlockSpec((tl, ka), lambda j, t: (t, 0)), pl.BlockSpec((tl, tn), lambda j, t: (t, j))]
    ins = [a, b]
    if rms_gain is not None:
        in_specs.append(pl.BlockSpec((1, ka), lambda j, t: (0, 0)))
        ins.append(rms_gain)
    return pl.pallas_call(
        body, name=name, grid=(n // tn, l // tl), in_specs=in_specs,
        out_specs=pl.BlockSpec((ka, tn), lambda j, t: (0, j)), out_shape=_sds((ka, n)),
        compiler_params=_params(("arbitrary", "arbitrary"), vmem))(*ins)


ATT_HEADS_PER_STEP = 2
ATT_W = ATT_HEADS_PER_STEP * LANE
ATT_GROUPS = MLA_HEADS // ATT_HEADS_PER_STEP
LOG2E = math.log2(math.e)
ATT_FWD_TILE = 1024
ATT_SCALE = D_QK ** -0.5
ATT_QSCALE = ATT_SCALE * LOG2E


def _tri_tables(nq, by_k):
    qs, ks = [], []
    if by_k:
        for ki in range(nq):
            for qi in range(ki, nq):
                qs.append(qi)
                ks.append(ki)
    else:
        for qi in range(nq):
            for ki in range(qi + 1):
                qs.append(qi)
                ks.append(ki)
    return jnp.asarray(np.array(qs, np.int32)), jnp.asarray(np.array(ks, np.int32))


def _causal_keep(shape, transposed):
    r = lax.broadcasted_iota(jnp.int32, shape, 0)
    c = lax.broadcasted_iota(jnp.int32, shape, 1)
    return (r <= c) if transposed else (c <= r)


def _nt16(a, b):
    return lax.dot_general(a, b, (((1,), (1,)), ((), ())), preferred_element_type=F32)


def _row_form(col):
    return jnp.transpose(jnp.broadcast_to(col, (col.shape[0], LANE)))[:8]


def _att_call(body, name, l, tq, tabs, ins, in_specs, outs, out_specs, scratch=()):
    grid_spec = pltpu.PrefetchScalarGridSpec(
        num_scalar_prefetch=2, grid=(ATT_GROUPS, tabs[0].shape[0]), in_specs=in_specs, out_specs=out_specs,
        scratch_shapes=list(scratch))
    return pl.pallas_call(body, name=name, grid_spec=grid_spec, out_shape=outs,
                          compiler_params=_params(("arbitrary", "arbitrary"), 48))(*tabs, *ins)


def _flash_fwd(name, q, k, v_t):
    l = q.shape[0]
    tq = min(ATT_FWD_TILE, l)
    nq = l // tq
    tabs = _tri_tables(nq, by_k=False)

    def body(qt, kt, q_ref, k_ref, vt_ref, o_ref, lset_ref, m_s, acc_s):
        t = pl.program_id(1)
        qi = qt[t]
        ki = kt[t]
        sls = [slice(h * LANE, (h + 1) * LANE) for h in range(ATT_HEADS_PER_STEP)]

        @pl.when(ki == 0)
        def _():
            m_s[...] = jnp.full(m_s.shape, NEG, F32)
            acc_s[...] = jnp.zeros(acc_s.shape, F32)

        def step(masked):
            sts = [_nt16(k_ref[:, sl], q_ref[:, sl]) for sl in sls]
            for h, sl in enumerate(sls):
                st = sts[h]
                if masked:
                    st = jnp.where(_causal_keep(st.shape, True), st, NEG)
                m_old = m_s[h][:1]
                m_new = jnp.maximum(m_old, jnp.max(st, axis=0, keepdims=True))
                alpha = jnp.exp2(m_old - m_new)
                pt = jnp.exp2(st - m_new).astype(BF16)
                acc_s[sl, :] = alpha * acc_s[sl, :] + jnp.dot(vt_ref[sl, :], pt, preferred_element_type=F32)
                m_s[h] = jnp.broadcast_to(m_new, (8, tq))

        @pl.when(ki < qi)
        def _():
            step(False)

        @pl.when(ki == qi)
        def _():
            step(True)
            row = lax.broadcasted_iota(jnp.int32, (LANE, tq), 0)
            for h, sl in enumerate(sls):
                acc = acc_s[sl, :]
                lsum = acc[D_V:D_V + 1, :]
                o_ref[:, sl] = jnp.transpose(jnp.where(row < D_V, acc / lsum, 0.0))
                lset_ref[h * 8:(h + 1) * 8, :] = m_s[h] + jnp.log2(lsum)

    qspec = pl.BlockSpec((tq, ATT_W), lambda g, t, qt, kt: (qt[t], g))
    kspec = pl.BlockSpec((tq, ATT_W), lambda g, t, qt, kt: (kt[t], g))
    vspec = pl.BlockSpec((ATT_W, tq), lambda g, t, qt, kt: (g, kt[t]))
    rspec = pl.BlockSpec((8 * ATT_HEADS_PER_STEP, tq), lambda g, t, qt, kt: (g, qt[t]))
    return _att_call(
        body, name, l, tq, tabs, [q, k, v_t], [qspec, kspec, vspec],
        (_sds((l, MLA_PAD)), _sds((8 * MLA_HEADS, l))), (qspec, rspec),
        scratch=[pltpu.VMEM((ATT_HEADS_PER_STEP, 8, tq), F32), pltpu.VMEM((ATT_W, tq), F32)])


def _flash_bwd(name, q, k, v, k_t, o, lse_t, do):
    l = q.shape[0]
    tq = min(512, l)
    nq = l // tq

    def delta_fn(i, dov, ov):
        rows = []
        for h in range(MLA_HEADS):
            sl = slice(h * LANE, (h + 1) * LANE)
            rows.append(_row_form(jnp.sum(dov[:, sl] * ov[:, sl], axis=-1, keepdims=True)))
        return jnp.concatenate(rows, axis=0), dov

    delta_t, do16 = _rows(
        delta_fn, name=name + "_delta", n=nq, ins=[do, o], in_specs=[_rt(tq, MLA_PAD)] * 2,
        outs=[_sds((8 * MLA_HEADS, l)), _sds((l, MLA_PAD), BF16)],
        out_specs=[pl.BlockSpec((8 * MLA_HEADS, tq), lambda i: (0, i)), _rt(tq, MLA_PAD)])

    def body(qt, kt, q_ref, k_ref, v_ref, do_ref, kt_ref, lset_ref, dlt_ref, dk_ref, dv_ref, dqt_ref):
        t = pl.program_id(1)
        qi = qt[t]
        ki = kt[t]
        sls = [slice(h * LANE, (h + 1) * LANE) for h in range(ATT_HEADS_PER_STEP)]

        @pl.when(ki == 0)
        def _():
            dqt_ref[qi] = jnp.zeros((ATT_W, tq), F32)

        def step(masked):
            sts = [_nt16(k_ref[:, sl], q_ref[:, sl]) for sl in sls]
            dpts = [_nt16(v_ref[:, sl], do_ref[:, sl]) for sl in sls]
            for h, sl in enumerate(sls):
                st = sts[h]
                if masked:
                    st = jnp.where(_causal_keep(st.shape, True), st, NEG)
                pt = jnp.exp2(st - lset_ref[h * 8:(h + 1) * 8, :][:1])
                dst = (pt * (dpts[h] - dlt_ref[h * 8:(h + 1) * 8, :][:1])).astype(BF16)
                dv_ref[:, sl] += jnp.dot(pt.astype(BF16), do_ref[:, sl], preferred_element_type=F32)
                dk_ref[:, sl] += jnp.dot(dst, q_ref[:, sl], preferred_element_type=F32)
                dqt_ref[qi, sl, :] += jnp.dot(kt_ref[sl, :], dst, preferred_element_type=F32)

        @pl.when(qi == ki)
        def _():
            dk_ref[...] = jnp.zeros(dk_ref.shape, F32)
            dv_ref[...] = jnp.zeros(dv_ref.shape, F32)
            step(True)
            dqt_ref[qi] = dqt_ref[qi] * ATT_SCALE

        @pl.when(qi > ki)
        def _():
            step(False)

        @pl.when(qi == nq - 1)
        def _():
            dk_ref[...] = dk_ref[...] * (1.0 / LOG2E)

    tabs_k = _tri_tables(nq, by_k=True)
    qspec = pl.BlockSpec((tq, ATT_W), lambda g, t, qt, kt: (qt[t], g))
    kspec = pl.BlockSpec((tq, ATT_W), lambda g, t, qt, kt: (kt[t], g))
    ktspec = pl.BlockSpec((ATT_W, tq), lambda g, t, qt, kt: (g, kt[t]))
    rspec = pl.BlockSpec((8 * ATT_HEADS_PER_STEP, tq), lambda g, t, qt, kt: (g, qt[t]))
    dqspec = pl.BlockSpec((nq, ATT_W, tq), lambda g, t, qt, kt: (0, g, 0))
    dk, dv, dq_t = _att_call(body, name + "_dqkv", l, tq, tabs_k, [q, k, v, do16, k_t, lse_t, delta_t],
                             [qspec, kspec, kspec, qspec, ktspec, rspec, rspec],
                             (_sds((l, MLA_PAD)), _sds((l, MLA_PAD)), _sds((nq, MLA_PAD, tq))), (kspec, kspec, dqspec))
    return dq_t, dk, dv


def _cmul(ar, ai, br, bi):
    return ar * br - ai * bi, ar * bi + ai * br


def _scan(name, x_re, x_im, a_re, a_im, reverse):
    l, lanes = x_re.shape
    ns = SCAN_SEGS
    tl = l // ns
    steps = int(math.log2(tl))
    assert 2 ** steps == tl and tl * ns == l

    def body(xr_ref, xi_ref, ar_ref, ai_ref, sr_ref, si_ref):
        a_r1 = ar_ref[...]
        a_i1 = ai_ref[...]
        a_r = jnp.broadcast_to(a_r1, (ns, LANE))
        a_i = jnp.broadcast_to(a_i1, (ns, LANE))

        def rows(t):
            t = (tl - 1 - t) if reverse else t
            return pl.ds(pl.multiple_of(t * ns, ns), ns)

        def local(t, carry):
            cr, ci = carry
            r = rows(t)
            pr, pi = _cmul(a_r, a_i, cr, ci)
            return pr + xr_ref[r, :], pi + xi_ref[r, :]

        zero = jnp.zeros((ns, LANE), F32)
        e_r, e_i = lax.fori_loop(0, tl, local, (zero, zero), unroll=min(8, tl))
        p_r, p_i = a_r1, a_i1
        for _ in range(steps):
            p_r, p_i = _cmul(p_r, p_i, p_r, p_i)
        rid = lax.broadcasted_iota(jnp.int32, (ns, LANE), 0)
        c_r = jnp.zeros((1, LANE), F32)
        c_i = jnp.zeros((1, LANE), F32)
        in_r, in_i = zero, zero
        order = range(ns - 2, -1, -1) if reverse else range(1, ns)
        for kk in order:
            src = kk + 1 if reverse else kk - 1
            ek_r = jnp.sum(jnp.where(rid == src, e_r, 0.0), axis=0, keepdims=True)
            ek_i = jnp.sum(jnp.where(rid == src, e_i, 0.0), axis=0, keepdims=True)
            q_r, q_i = _cmul(p_r, p_i, c_r, c_i)
            c_r, c_i = q_r + ek_r, q_i + ek_i
            in_r = jnp.where(rid == kk, jnp.broadcast_to(c_r, (ns, LANE)), in_r)
            in_i = jnp.where(rid == kk, jnp.broadcast_to(c_i, (ns, LANE)), in_i)

        def final(t, carry):
            cr, ci = carry
            r = rows(t)
            pr, pi = _cmul(a_r, a_i, cr, ci)
            nr, ni = pr + xr_ref[r, :], pi + xi_ref[r, :]
            sr_ref[r, :] = nr
            si_ref[r, :] = ni
            return nr, ni

        lax.fori_loop(0, tl, final, (in_r, in_i), unroll=min(8, tl))

    xs = pl.BlockSpec((l, LANE), lambda j: (0, j))
    as_ = pl.BlockSpec((1, LANE), lambda j: (0, j))
    return pl.pallas_call(
        body, name=name, grid=(lanes // LANE,), in_specs=[xs, xs, as_, as_], out_specs=(xs, xs),
        out_shape=(_sds((l, lanes)), _sds((l, lanes))),
        compiler_params=_params(("arbitrary",), 48))(x_re, x_im, a_re, a_im)


ANY = pl.BlockSpec(memory_space=pl.ANY)


def _place():
    mx, my, mc = lax.axis_index("x"), lax.axis_index("y"), lax.axis_index("c")
    return mx, my, mc, [(1 - mx, my), (mx, 1 - my), (1 - mx, 1 - my)]


def _run_copies(copies):
    for cp in copies:
        cp.start()
    for cp in copies:
        cp.wait_recv()
    for cp in copies:
        cp.wait_send()


def _remote(src, dst, sems, k, dev):
    return pltpu.make_async_remote_copy(src_ref=src, dst_ref=dst, send_sem=sems[0].at[k], recv_sem=sems[1].at[k],
                                        device_id=dev, device_id_type=MESH)


def _copy_call(body, name, ins, outs, n_copies, aliases=None):
    return pl.pallas_call(
        body, name=name, in_specs=[ANY] * len(ins), out_specs=[ANY] * len(outs), out_shape=list(outs),
        input_output_aliases=aliases or {},
        scratch_shapes=[pltpu.SemaphoreType.DMA((n_copies,)), pltpu.SemaphoreType.DMA((n_copies,))])(*ins)


def _gather_ici(xs):
    n = len(xs)

    def body(*refs):
        x_refs, y_refs, sems = refs[:n], refs[n:2 * n], refs[2 * n:]
        mx, my, mc, peers = _place()
        me = 2 * mx + my
        _run_copies([_remote(x_refs[i].at[mc], y_refs[i].at[me, mc], sems, 3 * i + j, (px, py, mc))
                     for i in range(n) for j, (px, py) in enumerate(peers)])

    return _copy_call(body, "comm_gather_ici", xs, [_sds((4,) + x.shape, x.dtype) for x in xs], 3 * n)


def _gather_d2d(ys):
    n = len(ys)

    def body(*refs):
        y_in, y_out, sems = refs[:n], refs[n:2 * n], refs[2 * n:]
        mx, my, mc, peers = _place()
        _run_copies([_remote(y_in[i].at[2 * px + py, mc], y_out[i].at[2 * px + py, mc], sems, 3 * i + j, (mx, my, 1 - mc))
                     for i in range(n) for j, (px, py) in enumerate(peers)])

    return _copy_call(body, "comm_gather_d2d", ys, [_sds(y.shape, y.dtype) for y in ys], 3 * n,
                      aliases={i: i for i in range(n)})


def _reduce_ici(gs):
    n = len(gs)

    def body(*refs):
        g_refs, y_refs, sems = refs[:n], refs[n:2 * n], refs[2 * n:]
        mx, my, mc, peers = _place()
        _run_copies([_remote(g_refs[i].at[2 * px + py], y_refs[i].at[j], sems, 3 * i + j, (px, py, mc))
                     for i in range(n) for j, (px, py) in enumerate(peers)])

    return _copy_call(body, "comm_reduce_ici", gs, [_sds((3,) + g.shape[1:], g.dtype) for g in gs], 3 * n)


def _swap_d2d(name, ps, other_layer):
    n = len(ps)

    def body(*refs):
        p_refs, o_refs, sems = refs[:n], refs[n:2 * n], refs[2 * n:]
        mx, my, mc, _ = _place()
        _run_copies([_remote(p_refs[i].at[1 - mc] if other_layer else p_refs[i], o_refs[i], sems, i, (mx, my, 1 - mc))
                     for i in range(n)])

    outs = [_sds(p.shape[1:] if other_layer else p.shape, p.dtype) for p in ps]
    return _copy_call(body, name, ps, outs, n)


def _all_exchange(name, src):
    def body(x_ref, y_ref, send_sems, recv_sems, local_sem):
        mx, my, mc = lax.axis_index("x"), lax.axis_index("y"), lax.axis_index("c")
        me = 4 * mx + 2 * my + mc
        own = pltpu.make_async_copy(x_ref, y_ref.at[me], local_sem)
        own.start()
        copies = []
        for j in range(1, 8):
            px = (1 - mx) if (j & 4) else mx
            py = (1 - my) if (j & 2) else my
            pc = (1 - mc) if (j & 1) else mc
            cp = pltpu.make_async_remote_copy(
                src_ref=x_ref, dst_ref=y_ref.at[me], send_sem=send_sems.at[j - 1], recv_sem=recv_sems.at[j - 1],
                device_id=(px, py, pc), device_id_type=MESH)
            cp.start()
            copies.append(cp)
        for cp in copies:
            cp.wait_recv()
        for cp in copies:
            cp.wait_send()
        own.wait()

    return pl.pallas_call(
        body, name=name, in_specs=[ANY], out_specs=ANY, out_shape=_sds((8,) + src.shape, src.dtype),
        scratch_shapes=[pltpu.SemaphoreType.DMA((7,)), pltpu.SemaphoreType.DMA((7,)), pltpu.SemaphoreType.DMA])(src)


PACK_W = 1024


def _pack(arrs, rows_multiple, dtype):
    flat = jnp.concatenate([a.reshape(-1).astype(dtype) for a in arrs])
    n = flat.shape[0]
    unit = PACK_W * rows_multiple
    tot = -(-n // unit) * unit
    flat = jnp.pad(flat, (0, tot - n))
    return flat.reshape(tot // PACK_W, PACK_W)


def _unpack(flat, shapes):
    flat = flat.reshape(-1)
    out = []
    off = 0
    for s in shapes:
        n = int(np.prod(s))
        out.append(flat[off:off + n].reshape(s))
        off += n
    return out


def _rope_tables(pos):
    l = pos.shape[0]
    tm = min(512, l)
    inv = (np.float32(ROPE_THETA) ** (-np.arange(0, D_ROPE, 2, dtype=np.float32) / np.float32(D_ROPE))).astype(np.float32)
    lane_f = np.zeros((1, LANE), np.float32)
    lane_f[0, D_NOPE:D_NOPE + 16] = inv
    lane_f[0, D_NOPE + 16:D_NOPE + 32] = inv

    def fn(i, p, f):
        ang = p * f
        lane = lax.broadcasted_iota(jnp.int32, ang.shape, 1)
        co = jnp.cos(ang)
        si = jnp.sin(ang)
        c = jnp.where(lane < D_NOPE, 1.0, jnp.where(lane < D_QK, co, 0.0))
        s1 = jnp.where((lane >= D_NOPE) & (lane < D_NOPE + 16), -si, 0.0)
        s2 = jnp.where((lane >= D_NOPE + 16) & (lane < D_QK), si, 0.0)
        return c, s1, s2

    return _rows(fn, name="rope_tables", n=l // tm, ins=[pos, jnp.asarray(lane_f)],
                 in_specs=[_rt(tm, 1), _full((1, LANE))], outs=[_sds((l, LANE))] * 3, out_specs=[_rt(tm, LANE)] * 3)


def _ssm_param_fn(lr, li, log_dt, br, bi):
    dt = jnp.exp(log_dt)
    mag = jnp.exp(lr * dt)
    a_re = mag * jnp.cos(li * dt)
    a_im = mag * jnp.sin(li * dt)
    den = lr * lr + li * li
    e_re = a_re - 1.0
    e_im = a_im
    f_re = (e_re * lr + e_im * li) / den
    f_im = (e_im * lr - e_re * li) / den
    bb_re = f_re[None] * br - f_im[None] * bi
    bb_im = f_re[None] * bi + f_im[None] * br
    return a_re, a_im, bb_re, bb_im


def _ssm_params(name, lr, li, log_dt, br, bi):
    g, n = lr.shape
    c = br.shape[0]
    return _rows(lambda i, *v: _ssm_param_fn(*v), name=name, n=1, ins=[lr, li, log_dt, br, bi],
                 in_specs=[_full((g, n)), _full((g, n)), _full((g, 1)), _full((c, g, n)), _full((c, g, n))],
                 outs=[_sds((g, n)), _sds((g, n)), _sds((c, g, n)), _sds((c, g, n))],
                 out_specs=[_full((g, n)), _full((g, n)), _full((c, g, n)), _full((c, g, n))])


def _ssm_params_bwd(name, lr, li, log_dt, br, bi, d_are, d_aim, d_bbre, d_bbim):
    g, n = lr.shape
    c = br.shape[0]

    def fn(i, lr, li, log_dt, br, bi, g0, g1, g2, g3):
        _, vjp = jax.vjp(_ssm_param_fn, lr, li, log_dt, br, bi)
        return vjp((g0, g1, g2, g3))

    sp = [_full((g, n)), _full((g, n)), _full((g, 1)), _full((c, g, n)), _full((c, g, n))]
    return _rows(fn, name=name, n=1, ins=[lr, li, log_dt, br, bi, d_are, d_aim, d_bbre, d_bbim],
                 in_specs=sp + [_full((g, n)), _full((g, n)), _full((c, g, n)), _full((c, g, n))],
                 outs=[_sds((g, n)), _sds((g, n)), _sds((g, 1)), _sds((c, g, n)), _sds((c, g, n))], out_specs=sp)


_EYE8 = np.eye(8, dtype=np.float32)


def _blockdiag(v):
    j, g, p, q = v.shape
    m = v[:, :, :, None, :] * jnp.asarray(_EYE8)[None, :, None, :, None]
    return m.reshape(j, g * p, g * q)


def _blockdiag_t(m, p, q):
    j = m.shape[0]
    m = m.reshape(j, 8, p, 8, q)
    return jnp.sum(m * jnp.asarray(_EYE8)[None, :, None, :, None], axis=3)


def _to_perm(v, l):
    ns = SCAN_SEGS
    return v.reshape(ns, l // ns, v.shape[-1]).transpose(1, 0, 2).reshape(l, v.shape[-1])


def _from_perm(v, l):
    ns = SCAN_SEGS
    return v.reshape(l // ns, ns, v.shape[-1]).transpose(1, 0, 2).reshape(l, v.shape[-1])


def _prep_layer(w, i):
    p = {}
    w_in = w['w_in'][i]
    z = lambda n: jnp.zeros((D_MODEL, n), w_in.dtype)
    o = Q_LORA + KV_LORA
    p['w_s'] = jnp.concatenate([w_in[:, :o], z(D_NOPE), w_in[:, o:o + D_ROPE], z(HEAD_PAD - D_QK)], axis=1)
    o += D_ROPE
    p['w_u'] = w_in[:, o:o + SSM_WIDTH]
    o += SSM_WIDTH
    p['w_xq'] = w_in[:, o:o + X_WIDTH]
    o += X_WIDTH
    p['w_g'] = w_in[:, o:]
    wq = w['w_q_b'][i].reshape(Q_LORA, MLA_HEADS, D_QK)
    p['w_qb'] = jnp.pad(wq, ((0, 0), (0, 0), (0, HEAD_PAD - D_QK))).reshape(Q_LORA, MLA_PAD)
    wkv = w['w_kv_b'][i].reshape(KV_LORA, MLA_HEADS, D_NOPE + D_V)
    p['w_k'] = jnp.pad(wkv[:, :, :D_NOPE], ((0, 0), (0, 0), (0, HEAD_PAD - D_NOPE))).reshape(KV_LORA, MLA_PAD)
    p['w_v'] = jnp.pad(wkv[:, :, D_NOPE:], ((0, 0), (0, 0), (0, HEAD_PAD - D_V))).reshape(KV_LORA, MLA_PAD)
    wo = w['w_o_mla'][i].reshape(MLA_HEADS, D_V, D_MODEL)
    p['w_oa'] = jnp.pad(wo, ((0, 0), (0, HEAD_PAD - D_V), (0, 0))).reshape(MLA_PAD, D_MODEL)
    for n in ('w_glu', 'w_o_ssm', 'w_mem_kv', 'w_o_cross', 'w_out', 'w_up', 'w_down'):
        p[n] = w[n][i]
    p['conv_w'] = w['conv_w'][i]
    for n in ('norm_mix_g', 'q_a_norm_g', 'kv_a_norm_g', 'b_glu', 'mem_norm_g', 'xq_norm_g', 'xk_norm_g', 'b_gate',
              'norm_ffn_g', 'conv_b'):
        p[n] = w[n][i].reshape(1, -1)
    p['q_norm_g'] = jnp.pad(w['q_norm_g'][i], (0, HEAD_PAD - D_QK)).reshape(1, HEAD_PAD)
    p['k_norm_g'] = jnp.pad(w['k_norm_g'][i], (0, HEAD_PAD - D_QK)).reshape(1, HEAD_PAD)
    p['ssm_d'] = w['ssm_d'][i].reshape(1, SSM_WIDTH)
    p['lr'] = w['ssm_lambda_re'][i]
    p['li'] = w['ssm_lambda_im'][i]
    p['log_dt'] = w['ssm_log_dt'][i].reshape(SSM_GROUPS, 1)
    p['br'] = w['ssm_b_re'][i].transpose(2, 0, 1)
    p['bi'] = w['ssm_b_im'][i].transpose(2, 0, 1)
    cr = w['ssm_c_re'][i].reshape(SSM_JB, 8, SSM_GROUP_CH, SSM_STATE).transpose(0, 1, 3, 2)
    ci = w['ssm_c_im'][i].reshape(SSM_JB, 8, SSM_GROUP_CH, SSM_STATE).transpose(0, 1, 3, 2)
    p['c_mat'] = jnp.concatenate([_blockdiag(cr), -_blockdiag(ci)], axis=1).astype(BF16)
    return p


def _b_mat(bb_re, bb_im):
    r = bb_re.transpose(1, 0, 2).reshape(SSM_JB, 8, SSM_GROUP_CH, SSM_STATE)
    i = bb_im.transpose(1, 0, 2).reshape(SSM_JB, 8, SSM_GROUP_CH, SSM_STATE)
    return jnp.concatenate([_blockdiag(r), _blockdiag(i)], axis=2).astype(BF16)


def _qkv_fn(ps, c, s1, s2, qag, wqb, kvag, wk, wv, qng, kng):
    c_q = ps[:, :Q_LORA]
    c_kv = ps[:, Q_LORA:Q_LORA + KV_LORA]
    kr = ps[:, Q_LORA + KV_LORA:]
    cqn = _rms(c_q, qag, Q_LORA)
    ckvn = _rms(c_kv, kvag, KV_LORA)
    q_raw = _mm(cqn, wqb)
    k_raw = _mm(ckvn, wk) + jnp.concatenate([kr] * MLA_HEADS, axis=-1)
    v = _mm(ckvn, wv)
    q = _heads(_rope, _head_rms(q_raw, qng, MLA_HEADS, D_QK), MLA_HEADS, c, s1, s2)
    k = _heads(_rope, _head_rms(k_raw, kng, MLA_HEADS, D_QK), MLA_HEADS, c, s1, s2)
    lane = lax.broadcasted_iota(jnp.int32, v.shape, 1)
    v = jnp.where((lane & (LANE - 1)) == D_V, 1.0, v)
    return q * ATT_QSCALE, k, v


def _layer_fwd(name, x, tabs, mem, p):
    l = x.shape[0]
    tm = min(512, l)
    nt = l // tm
    sv = {'x0': x}
    sv['p_g'] = _matmul(name + "_in_g", [(x, p['w_g'])], l, 3 * D_MODEL, rms_gain=p['norm_mix_g'])
    sv['p_u'] = _matmul(name + "_in_u", [(x, p['w_u'])], l, SSM_WIDTH, rms_gain=p['norm_mix_g'])
    sv['p_xq'] = _matmul(name + "_in_xq", [(x, p['w_xq'])], l, X_WIDTH, rms_gain=p['norm_mix_g'])
    sv['p_s'] = _matmul(name + "_in_s", [(x, p['w_s'])], l, SMALL_W, rms_gain=p['norm_mix_g'])

    qkv_consts = [p['q_a_norm_g'], p['w_qb'], p['kv_a_norm_g'], p['w_k'], p['w_v'], p['q_norm_g'], p['k_norm_g']]
    qkv_cspecs = [_full(a.shape) for a in qkv_consts]
    def qkv_fwd(i, *a):
        qv, kv, vv = _qkv_fn(*a)
        return qv, kv, vv, jnp.transpose(kv), jnp.transpose(vv)

    q, k, v, k_t, v_t = _rows(qkv_fwd, name=name + "_qkv", n=nt, ins=[sv['p_s'], *tabs, *qkv_consts],
                              in_specs=[_rt(tm, SMALL_W)] + [_rt(tm, LANE)] * 3 + qkv_cspecs,
                              outs=[_sds((l, MLA_PAD), BF16)] * 3 + [_sds((MLA_PAD, l), BF16)] * 2,
                              out_specs=[_rt(tm, MLA_PAD)] * 3 + [pl.BlockSpec((MLA_PAD, tm), lambda i: (0, i))] * 2)
    sv['q'], sv['k'], sv['v'], sv['k_t'] = q, k, v, k_t
    sv['o_a'], sv['lse_t'] = _flash_fwd(name + "_attn", q, k, v_t)

    a_re, a_im, bb_re, bb_im = _ssm_params(name + "_ssm_par", p['lr'], p['li'], p['log_dt'], p['br'], p['bi'])
    sv['a_re'], sv['a_im'] = a_re.reshape(1, SSM_LANES), a_im.reshape(1, SSM_LANES)
    sv['b_mat'] = _b_mat(bb_re, bb_im)
    u_p = _to_perm(sv['p_u'], l)
    sv['u_p'] = u_p

    def bu_fn(i, u, bm):
        res = [_mm(u[:, j * LANE:(j + 1) * LANE], bm[j]) for j in range(SSM_JB)]
        return (jnp.concatenate([r[:, :512] for r in res], axis=-1), jnp.concatenate([r[:, 512:] for r in res], axis=-1))

    ts = min(256, l)
    bu_re, bu_im = _rows(bu_fn, name=name + "_ssm_bu", n=l // ts, ins=[u_p, sv['b_mat']],
                         in_specs=[_rt(ts, SSM_WIDTH), _full(sv['b_mat'].shape)],
                         outs=[_sds((l, SSM_LANES))] * 2, out_specs=[_rt(ts, SSM_LANES)] * 2)
    s_re, s_im = _scan(name + "_ssm_scan", bu_re, bu_im, sv['a_re'], sv['a_im'], reverse=False)
    sv['s_re'], sv['s_im'] = s_re, s_im

    def glu_fn(i, sr, si, u, cm, dsk, wg, bg):
        y = jnp.concatenate([_mm(jnp.concatenate([sr[:, j * 512:(j + 1) * 512], si[:, j * 512:(j + 1) * 512]], axis=-1),
                                 cm[j]) for j in range(SSM_JB)], axis=-1) + dsk * u
        zz = _gelu(y)
        return zz * jax.nn.sigmoid(_mm(zz, wg) + bg)

    glu_consts = [p['c_mat'], p['ssm_d'], p['w_glu'], p['b_glu']]
    zo_p = _rows(glu_fn, name=name + "_ssm_glu", n=l // ts, ins=[s_re, s_im, u_p, *glu_consts],
                 in_specs=[_rt(ts, SSM_LANES), _rt(ts, SSM_LANES), _rt(ts, SSM_WIDTH)] + [_full(a.shape) for a in glu_consts],
                 outs=[_sds((l, SSM_WIDTH), BF16)], out_specs=[_rt(ts, SSM_WIDTH)])[0]
    sv['zo'] = _from_perm(zo_p, l)

    m_len = mem.shape[0]

    def memkv_fn(i, mm_, mg, wmk, xkg):
        kv = _mm(_rms(mm_, mg, D_MODEL), wmk)
        return _head_rms(kv[:, :X_WIDTH], xkg, X_HEADS, X_HEAD_DIM), kv[:, X_WIDTH:]

    mem_consts = [p['mem_norm_g'], p['w_mem_kv'], p['xk_norm_g']]
    k_c, v_c = _rows(memkv_fn, name=name + "_memkv", n=1, ins=[mem, *mem_consts],
                     in_specs=[_full(mem.shape)] + [_full(a.shape) for a in mem_consts],
                     outs=[_sds((m_len, X_WIDTH))] * 2, out_specs=[_full((m_len, X_WIDTH))] * 2)
    sv['k_c'], sv['v_c'] = k_c, v_c

    def cross_fn(i, xq, kc, vc, xqg):
        outs = []
        for h in range(X_HEADS):
            sl = slice(h * LANE, (h + 1) * LANE)
            qh = _rms(xq[:, sl], xqg, X_HEAD_DIM)
            s = _mm_nt(qh, kc[:, sl]) * (X_HEAD_DIM ** -0.5)
            s = s - jnp.max(s, axis=-1, keepdims=True)
            e = jnp.exp(s)
            pr = e / jnp.sum(e, axis=-1, keepdims=True)
            outs.append(_mm(pr, vc[:, sl]))
        return jnp.concatenate(outs, axis=-1)

    sv['o_c'] = _rows(cross_fn, name=name + "_cross", n=nt, ins=[sv['p_xq'], k_c, v_c, p['xq_norm_g']],
                      in_specs=[_rt(tm, X_WIDTH), _full(k_c.shape), _full(v_c.shape), _full((1, LANE))],
                      outs=[_sds((l, X_WIDTH), BF16)], out_specs=[_rt(tm, X_WIDTH)])[0]

    def merge_fn(i, oa, zo, oc, pg, x0, woa, wos, woc, bg, wout):
        gates = jax.nn.sigmoid(pg + bg)
        merged = (gates[:, :D_MODEL] * _mm(oa, woa) + gates[:, D_MODEL:2 * D_MODEL] * _mm(zo, wos)
                  + gates[:, 2 * D_MODEL:] * _mm(oc, woc))
        return x0 + _mm(merged, wout), merged

    merge_consts = [p['w_oa'], p['w_o_ssm'], p['w_o_cross'], p['b_gate'], p['w_out']]
    tg = min(256, l)
    x1, merged = _rows(merge_fn, name=name + "_merge", n=l // tg, ins=[sv['o_a'], sv['zo'], sv['o_c'], sv['p_g'], x, *merge_consts],
                       in_specs=[_rt(tg, MLA_PAD), _rt(tg, SSM_WIDTH), _rt(tg, X_WIDTH), _rt(tg, 3 * D_MODEL), _rt(tg, D_MODEL)]
                       + [_full(a.shape) for a in merge_consts],
                       outs=[_sds((l, D_MODEL)), _sds((l, D_MODEL), BF16)], out_specs=[_rt(tg, D_MODEL)] * 2)
    sv['x1'], sv['merged'] = x1, merged

    up = _matmul(name + "_up", [(x1, p['w_up'])], l, 2 * D_FF, rms_gain=p['norm_ffn_g'])
    sv['up'] = up
    tc = min(128, l)

    def conv_fn(i, upt, halo, cw, cb):
        upc = _conv(i, upt, halo, cw) + cb
        return _silu(upc[:, :D_FF]) * upc[:, D_FF:]

    act = _rows(conv_fn, name=name + "_conv", n=l // tc, ins=[up, up, p['conv_w'], p['conv_b']],
                in_specs=[_rt(tc, 2 * D_FF), _halo_prev(tc, 2 * D_FF), _full((3, 2 * D_FF)), _full((1, 2 * D_FF))],
                outs=[_sds((l, D_FF), BF16)], out_specs=[_rt(tc, D_FF)])[0]
    sv['act'] = act
    x2 = _matmul(name + "_down", [(act, p['w_down'])], l, D_MODEL, resid=x1)
    return x2, sv


def _halo_prev(tm, w):
    return pl.BlockSpec((8, w), lambda i: (jnp.maximum(i * (tm // 8) - 1, 0), 0))


def _halo_next(tm, w, n_tiles):
    last = n_tiles * (tm // 8) - 1
    return pl.BlockSpec((8, w), lambda i: (jnp.minimum((i + 1) * (tm // 8), last), 0))


def _conv(i, tile, halo, cw):
    halo = jnp.where(i > 0, halo, 0.0)
    ext = jnp.concatenate([halo, tile], axis=0)
    n = ext.shape[0]
    x1 = pltpu.roll(ext, 1, 0)[8:]
    x2 = pltpu.roll(ext, 2, 0)[8:]
    del n
    return cw[0:1] * x2 + cw[1:2] * x1 + cw[2:3] * tile


def _layer_bwd(name, dx2, sv, tabs, mem, p):
    l = dx2.shape[0]
    tm = min(512, l)
    nt = l // tm
    g = {}
    x1 = sv['x1']
    dact = _matmul(name + "_b_down", [(dx2, p['w_down'])], l, D_FF, nt=True)
    g['w_down'] = _matmul_tn(name + "_gw_down", sv['act'], dx2)
    tc = min(128, l)
    ntc = l // tc

    def conv_b(i, upt, up_prev, up_next, da, da_next, cw, cb):
        up_prev = jnp.where(i > 0, up_prev, 0.0)
        da_next = jnp.where(i < ntc - 1, da_next, 0.0)
        ext = jnp.concatenate([up_prev, upt, up_next], axis=0)
        x0 = ext[8:]
        xm1 = pltpu.roll(ext, 1, 0)[8:]
        xm2 = pltpu.roll(ext, 2, 0)[8:]
        upc = cw[0:1] * xm2 + cw[1:2] * xm1 + cw[2:3] * x0 + cb
        _, vjp = jax.vjp(lambda a, b: _silu(a) * b, upc[:, :D_FF], upc[:, D_FF:])
        dg, dv = vjp(jnp.concatenate([da, da_next], axis=0))
        dupc = jnp.concatenate([dg, dv], axis=-1)
        n = dupc.shape[0]
        dup = cw[2:3] * dupc[:tc] + cw[1:2] * pltpu.roll(dupc, n - 1, 0)[:tc] + cw[0:1] * pltpu.roll(dupc, n - 2, 0)[:tc]
        dt = dupc[:tc]
        dcw = _row_select([_colsum(dt * xm2[:tc]), _colsum(dt * xm1[:tc]), _colsum(dt * upt)], 8)
        return dup, dcw, _colsum(dt)

    dup, g_cw, g_cb = _rows(
        conv_b, name=name + "_b_conv", n=ntc, ins=[sv['up'], sv['up'], sv['up'], dact, dact, p['conv_w'], p['conv_b']],
        in_specs=[_rt(tc, 2 * D_FF), _halo_prev(tc, 2 * D_FF), _halo_next(tc, 2 * D_FF, ntc), _rt(tc, D_FF),
                  _halo_next(tc, D_FF, ntc), _full((3, 2 * D_FF)), _full((1, 2 * D_FF))],
        outs=[_sds((l, 2 * D_FF)), _sds((8, 2 * D_FF)), _sds((1, 2 * D_FF))],
        out_specs=[_rt(tc, 2 * D_FF), _full((8, 2 * D_FF)), _full((1, 2 * D_FF))], n_acc=2, vmem=56)
    g['conv_w'] = g_cw[:3]
    g['conv_b'] = g_cb
    dh2 = _matmul(name + "_b_up", [(dup, p['w_up'])], l, D_MODEL, nt=True, tm=256)
    g['w_up'] = _matmul_tn(name + "_gw_up", x1, dup, rms_gain=p['norm_ffn_g'])

    def norm_b(i, xv, dh, dres, gn):
        _, vjp = jax.vjp(lambda a, b: _rms(a, b, D_MODEL), xv, gn)
        dxv, dgn = vjp(dh)
        return dres + dxv, dgn

    dx1, g['norm_ffn_g'] = _rows(norm_b, name=name + "_b_norm2", n=nt, ins=[x1, dh2, dx2, p['norm_ffn_g']],
                                 in_specs=[_rt(tm, D_MODEL)] * 3 + [_full((1, D_MODEL))],
                                 outs=[_sds((l, D_MODEL)), _sds((1, D_MODEL))], out_specs=[_rt(tm, D_MODEL), _full((1, D_MODEL))],
                                 n_acc=1)

    tg = min(256, l)

    def merge_b(i, dx, oa, zo, oc, pg, woa, wos, woc, bg, wout):
        dm = _mm_nt(dx, wout)
        gates = jax.nn.sigmoid(pg + bg)
        ys = [_mm(oa, woa), _mm(zo, wos), _mm(oc, woc)]
        dys, dpg = [], []
        for b in range(3):
            gb = gates[:, b * D_MODEL:(b + 1) * D_MODEL]
            dys.append(dm * gb)
            dpg.append(dm * ys[b] * gb * (1.0 - gb))
        dpg = jnp.concatenate(dpg, axis=-1)
        return (_mm_nt(dys[0], woa), _mm_nt(dys[1], wos), _mm_nt(dys[2], woc), dpg, dys[0], dys[1], dys[2], _colsum(dpg))

    merge_consts = [p['w_oa'], p['w_o_ssm'], p['w_o_cross'], p['b_gate'], p['w_out']]
    (do_a, dzo, do_c, dp_g, dy_a, dy_b, dy_c, g['b_gate']) = _rows(
        merge_b, name=name + "_b_merge", n=l // tg, ins=[dx1, sv['o_a'], sv['zo'], sv['o_c'], sv['p_g'], *merge_consts],
        in_specs=[_rt(tg, D_MODEL), _rt(tg, MLA_PAD), _rt(tg, SSM_WIDTH), _rt(tg, X_WIDTH), _rt(tg, 3 * D_MODEL)]
        + [_full(a.shape) for a in merge_consts],
        outs=[_sds((l, MLA_PAD)), _sds((l, SSM_WIDTH)), _sds((l, X_WIDTH)), _sds((l, 3 * D_MODEL)),
              _sds((l, D_MODEL), BF16), _sds((l, D_MODEL), BF16), _sds((l, D_MODEL), BF16), _sds((1, 3 * D_MODEL))],
        out_specs=[_rt(tg, MLA_PAD), _rt(tg, SSM_WIDTH), _rt(tg, X_WIDTH), _rt(tg, 3 * D_MODEL),
                   _rt(tg, D_MODEL), _rt(tg, D_MODEL), _rt(tg, D_MODEL), _full((1, 3 * D_MODEL))], n_acc=1, vmem=56)
    g['w_out'] = _matmul_tn(name + "_gw_out", sv['merged'], dx1)
    g['w_oa'] = _matmul_tn(name + "_gw_oa", sv['o_a'], dy_a)
    g['w_o_ssm'] = _matmul_tn(name + "_gw_os", sv['zo'], dy_b)
    g['w_o_cross'] = _matmul_tn(name + "_gw_oc", sv['o_c'], dy_c)

    k_c, v_c = sv['k_c'], sv['v_c']
    m_len = k_c.shape[0]

    def cross_b(i, xq, do, kc, vc, xqg):
        dxq, dk, dv = [], [], []
        dg = jnp.zeros((1, LANE), F32)
        for h in range(X_HEADS):
            sl = slice(h * LANE, (h + 1) * LANE)
            qh, vjp = jax.vjp(lambda a, b: _rms(a, b, X_HEAD_DIM), xq[:, sl], xqg)
            sc = X_HEAD_DIM ** -0.5
            s = _mm_nt(qh, kc[:, sl]) * sc
            s = s - jnp.max(s, axis=-1, keepdims=True)
            e = jnp.exp(s)
            pr = e / jnp.sum(e, axis=-1, keepdims=True)
            doh = do[:, sl]
            dv.append(_mm_tn(pr, doh))
            dp = _mm_nt(doh, vc[:, sl])
            ds = pr * (dp - jnp.sum(dp * pr, axis=-1, keepdims=True)) * sc
            dk.append(_mm_tn(ds, qh))
            dxh, dgh = vjp(_mm(ds, kc[:, sl]))
            dxq.append(dxh)
            dg = dg + dgh
        return jnp.concatenate(dxq, axis=-1), jnp.concatenate(dk, axis=-1), jnp.concatenate(dv, axis=-1), dg

    dp_xq, dk_c, dv_c, g['xq_norm_g'] = _rows(
        cross_b, name=name + "_b_cross", n=nt, ins=[sv['p_xq'], do_c, k_c, v_c, p['xq_norm_g']],
        in_specs=[_rt(tm, X_WIDTH), _rt(tm, X_WIDTH), _full(k_c.shape), _full(v_c.shape), _full((1, LANE))],
        outs=[_sds((l, X_WIDTH)), _sds((m_len, X_WIDTH)), _sds((m_len, X_WIDTH)), _sds((1, LANE))],
        out_specs=[_rt(tm, X_WIDTH), _full((m_len, X_WIDTH)), _full((m_len, X_WIDTH)), _full((1, LANE))], n_acc=3)

    def memkv_b(i, mm_, dk, dv, mg, wmk, xkg):
        memn, vjp_n = jax.vjp(lambda a, b: _rms(a, b, D_MODEL), mm_, mg)
        kv = _mm(memn, wmk)
        _, vjp_k = jax.vjp(lambda a, b: _head_rms(a, b, X_HEADS, X_HEAD_DIM), kv[:, :X_WIDTH], xkg)
        dkr, dxkg = vjp_k(dk)
        dkv = jnp.concatenate([dkr, dv], axis=-1)
        _, dmg = vjp_n(_mm_nt(dkv, wmk))
        return _mm_tn(memn, dkv), dmg, dxkg

    mem_consts = [p['mem_norm_g'], p['w_mem_kv'], p['xk_norm_g']]
    g['w_mem_kv'], g['mem_norm_g'], g['xk_norm_g'] = _rows(
        memkv_b, name=name + "_b_memkv", n=1, ins=[mem, dk_c, dv_c, *mem_consts],
        in_specs=[_full(mem.shape), _full(dk_c.shape), _full(dv_c.shape)] + [_full(a.shape) for a in mem_consts],
        outs=[_sds((D_MODEL, 2 * X_WIDTH)), _sds((1, D_MODEL)), _sds((1, LANE))],
        out_specs=[_full((D_MODEL, 2 * X_WIDTH)), _full((1, D_MODEL)), _full((1, LANE))])

    u_p = sv['u_p']
    dzo_p = _to_perm(dzo, l)
    s_re, s_im = sv['s_re'], sv['s_im']

    def glu_b(i, sr, si, u, dz, cm, dsk, wg, bg):
        cats = [jnp.concatenate([sr[:, j * 512:(j + 1) * 512], si[:, j * 512:(j + 1) * 512]], axis=-1) for j in range(SSM_JB)]
        y = jnp.concatenate([_mm(cats[j], cm[j]) for j in range(SSM_JB)], axis=-1) + dsk * u
        zz, vjp_g = jax.vjp(_gelu, y)
        t = _mm(zz, wg) + bg
        sg = jax.nn.sigmoid(t)
        dt = dz * zz * sg * (1.0 - sg)
        dzz = dz * sg + _mm_nt(dt, wg)
        dy = vjp_g(dzz)[0]
        dss = [_mm_nt(dy[:, j * LANE:(j + 1) * LANE], cm[j]) for j in range(SSM_JB)]
        dsr = jnp.concatenate([d[:, :512] for d in dss], axis=-1)
        dsi = jnp.concatenate([d[:, 512:] for d in dss], axis=-1)
        dcm = jnp.stack([_mm_tn(cats[j], dy[:, j * LANE:(j + 1) * LANE]) for j in range(SSM_JB)], axis=0)
        return dsr, dsi, dy * dsk, dcm, _colsum(dy * u), _mm_tn(zz, dt), _colsum(dt)

    glu_consts = [p['c_mat'], p['ssm_d'], p['w_glu'], p['b_glu']]
    ts = min(256, l)
    nts = l // ts
    ds_re, ds_im, du_dir, g['c_mat'], g['ssm_d'], g['w_glu'], g['b_glu'] = _rows(
        glu_b, name=name + "_b_glu", n=nts, ins=[s_re, s_im, u_p, dzo_p, *glu_consts],
        in_specs=[_rt(ts, SSM_LANES), _rt(ts, SSM_LANES), _rt(ts, SSM_WIDTH), _rt(ts, SSM_WIDTH)] + [_full(a.shape) for a in glu_consts],
        outs=[_sds((l, SSM_LANES)), _sds((l, SSM_LANES)), _sds((l, SSM_WIDTH)), _sds((SSM_JB, 1024, LANE)), _sds((1, SSM_WIDTH)),
              _sds((SSM_WIDTH, SSM_WIDTH)), _sds((1, SSM_WIDTH))],
        out_specs=[_rt(ts, SSM_LANES), _rt(ts, SSM_LANES), _rt(ts, SSM_WIDTH), _full((SSM_JB, 1024, LANE)), _full((1, SSM_WIDTH)),
                   _full((SSM_WIDTH, SSM_WIDTH)), _full((1, SSM_WIDTH))], n_acc=4)
    gb_re, gb_im = _scan(name + "_b_scan", ds_re, ds_im, sv['a_re'], -sv['a_im'], reverse=True)
    ns = SCAN_SEGS
    last_blk = l // ns - 1

    def da_fn(i, gr, gi, sr, si, hr, hi, lr_, li_):
        rid = lax.broadcasted_iota(jnp.int32, lr_.shape, 0)
        fr = jnp.where(rid == 0, 0.0, pltpu.roll(lr_, 1, 0))
        fi = jnp.where(rid == 0, 0.0, pltpu.roll(li_, 1, 0))
        hr = jnp.where(i == 0, fr, hr)
        hi = jnp.where(i == 0, fi, hi)
        if ts > ns:
            pr = jnp.concatenate([hr, sr[:ts - ns]], axis=0)
            pi = jnp.concatenate([hi, si[:ts - ns]], axis=0)
        else:
            pr, pi = hr, hi
        return _colsum(gr * pr + gi * pi), _colsum(gi * pr - gr * pi)

    hprev = pl.BlockSpec((ns, SSM_LANES), lambda i: (jnp.maximum(i * (ts // ns) - 1, 0), 0))
    hlast = pl.BlockSpec((ns, SSM_LANES), lambda i: (last_blk, 0))
    da_re, da_im = _rows(da_fn, name=name + "_b_da", n=nts, ins=[gb_re, gb_im, s_re, s_im, s_re, s_im, s_re, s_im],
                         in_specs=[_rt(ts, SSM_LANES)] * 4 + [hprev, hprev, hlast, hlast],
                         outs=[_sds((1, SSM_LANES))] * 2, out_specs=[_full((1, SSM_LANES))] * 2, n_acc=2)

    def bu_b(i, dbr, dbi, u, dud, bm):
        dus, dbm = [], []
        for j in range(SSM_JB):
            cat = jnp.concatenate([dbr[:, j * 512:(j + 1) * 512], dbi[:, j * 512:(j + 1) * 512]], axis=-1)
            dus.append(_mm_nt(cat, bm[j]))
            dbm.append(_mm_tn(u[:, j * LANE:(j + 1) * LANE], cat))
        return dud + jnp.concatenate(dus, axis=-1), jnp.stack(dbm, axis=0)

    du_p, d_bmat = _rows(bu_b, name=name + "_b_bu", n=nts, ins=[gb_re, gb_im, u_p, du_dir, sv['b_mat']],
                         in_specs=[_rt(ts, SSM_LANES), _rt(ts, SSM_LANES), _rt(ts, SSM_WIDTH), _rt(ts, SSM_WIDTH),
                                   _full(sv['b_mat'].shape)],
                         outs=[_sds((l, SSM_WIDTH)), _sds((SSM_JB, LANE, 1024))],
                         out_specs=[_rt(ts, SSM_WIDTH), _full((SSM_JB, LANE, 1024))], n_acc=1)
    dp_u = _from_perm(du_p, l)
    dbb_re = _blockdiag_t(d_bmat[:, :, :512], SSM_GROUP_CH, SSM_STATE).reshape(SSM_GROUPS, SSM_GROUP_CH, SSM_STATE).transpose(1, 0, 2)
    dbb_im = _blockdiag_t(d_bmat[:, :, 512:], SSM_GROUP_CH, SSM_STATE).reshape(SSM_GROUPS, SSM_GROUP_CH, SSM_STATE).transpose(1, 0, 2)
    g['lr'], g['li'], g['log_dt'], g['br'], g['bi'] = _ssm_params_bwd(
        name + "_b_ssm_par", p['lr'], p['li'], p['log_dt'], p['br'], p['bi'],
        da_re.reshape(SSM_GROUPS, SSM_STATE), da_im.reshape(SSM_GROUPS, SSM_STATE), dbb_re, dbb_im)

    dq_t, dk, dv = _flash_bwd(name + "_b_attn", sv['q'], sv['k'], sv['v'], sv['k_t'], sv['o_a'], sv['lse_t'], do_a)

    def qkv_b(i, ps, c, s1, s2, dq_, dk_, dv_, qag, wqb, kvag, wk, wv, qng, kng):
        c_q = ps[:, :Q_LORA]
        c_kv = ps[:, Q_LORA:Q_LORA + KV_LORA]
        kr = ps[:, Q_LORA + KV_LORA:]
        cqn, vjp_cq = jax.vjp(lambda a, b: _rms(a, b, Q_LORA), c_q, qag)
        ckvn, vjp_ckv = jax.vjp(lambda a, b: _rms(a, b, KV_LORA), c_kv, kvag)
        q_raw = _mm(cqn, wqb)
        k_raw = _mm(ckvn, wk) + jnp.concatenate([kr] * MLA_HEADS, axis=-1)
        _, vjp_qn = jax.vjp(lambda a, b: _head_rms(a, b, MLA_HEADS, D_QK), q_raw, qng)
        _, vjp_kn = jax.vjp(lambda a, b: _head_rms(a, b, MLA_HEADS, D_QK), k_raw, kng)
        dq_raw, dqng = vjp_qn(_heads(_rope_t, jnp.transpose(dq_[0]), MLA_HEADS, c, s1, s2))
        dk_raw, dkng = vjp_kn(_heads(_rope_t, dk_, MLA_HEADS, c, s1, s2))
        dkr = dk_raw[:, :LANE]
        for h in range(1, MLA_HEADS):
            dkr = dkr + dk_raw[:, h * LANE:(h + 1) * LANE]
        dcq, dqag = vjp_cq(_mm_nt(dq_raw, wqb))
        dckv, dkvag = vjp_ckv(_mm_nt(dk_raw, wk) + _mm_nt(dv_, wv))
        dps = jnp.concatenate([dcq, dckv, dkr], axis=-1)
        return (dps, _mm_tn(cqn, dq_raw), _mm_tn(ckvn, dk_raw), _mm_tn(ckvn, dv_), dqag, dkvag, dqng, dkng)

    qkv_consts = [p['q_a_norm_g'], p['w_qb'], p['kv_a_norm_g'], p['w_k'], p['w_v'], p['q_norm_g'], p['k_norm_g']]
    (dp_s, g['w_qb'], g['w_k'], g['w_v'], g['q_a_norm_g'], g['kv_a_norm_g'], g['q_norm_g'], g['k_norm_g']) = _rows(
        qkv_b, name=name + "_b_qkv", n=nt, ins=[sv['p_s'], *tabs, dq_t, dk, dv, *qkv_consts],
        in_specs=[_rt(tm, SMALL_W)] + [_rt(tm, LANE)] * 3 + [pl.BlockSpec((1, MLA_PAD, tm), lambda i: (i, 0, 0))]
        + [_rt(tm, MLA_PAD)] * 2 + [_full(a.shape) for a in qkv_consts],
        outs=[_sds((l, SMALL_W)), _sds((Q_LORA, MLA_PAD)), _sds((KV_LORA, MLA_PAD)), _sds((KV_LORA, MLA_PAD)),
              _sds((1, Q_LORA)), _sds((1, KV_LORA)), _sds((1, LANE)), _sds((1, LANE))],
        out_specs=[_rt(tm, SMALL_W), _full((Q_LORA, MLA_PAD)), _full((KV_LORA, MLA_PAD)), _full((KV_LORA, MLA_PAD)),
                   _full((1, Q_LORA)), _full((1, KV_LORA)), _full((1, LANE)), _full((1, LANE))], n_acc=7)

    x0 = sv['x0']
    dh = _matmul(name + "_b_in", [(dp_g, p['w_g']), (dp_u, p['w_u']), (dp_xq, p['w_xq']), (dp_s, p['w_s'])], l, D_MODEL, nt=True,
                 tm=256)
    gm = p['norm_mix_g']
    g['w_g'] = _matmul_tn(name + "_gw_g", x0, dp_g, rms_gain=gm)
    g['w_u'] = _matmul_tn(name + "_gw_u", x0, dp_u, rms_gain=gm)
    g['w_xq'] = _matmul_tn(name + "_gw_xq", x0, dp_xq, rms_gain=gm)
    g['w_s'] = _matmul_tn(name + "_gw_s", x0, dp_s, rms_gain=gm)
    dx0, g['norm_mix_g'] = _rows(norm_b, name=name + "_b_norm1", n=nt, ins=[x0, dh, dx1, gm],
                                 in_specs=[_rt(tm, D_MODEL)] * 3 + [_full((1, D_MODEL))],
                                 outs=[_sds((l, D_MODEL)), _sds((1, D_MODEL))], out_specs=[_rt(tm, D_MODEL), _full((1, D_MODEL))],
                                 n_acc=1)
    return dx0, g


def _unprep_grads(g):
    o = {}
    ws = g['w_s']
    o['w_in'] = jnp.concatenate([ws[:, :Q_LORA + KV_LORA], ws[:, Q_LORA + KV_LORA + D_NOPE:Q_LORA + KV_LORA + D_QK],
                                 g['w_u'], g['w_xq'], g['w_g']], axis=1)
    o['w_q_b'] = g['w_qb'].reshape(Q_LORA, MLA_HEADS, HEAD_PAD)[:, :, :D_QK].reshape(Q_LORA, MLA_HEADS * D_QK)
    gk = g['w_k'].reshape(KV_LORA, MLA_HEADS, HEAD_PAD)[:, :, :D_NOPE]
    gv = g['w_v'].reshape(KV_LORA, MLA_HEADS, HEAD_PAD)[:, :, :D_V]
    o['w_kv_b'] = jnp.concatenate([gk, gv], axis=2).reshape(KV_LORA, MLA_HEADS * (D_NOPE + D_V))
    o['w_o_mla'] = g['w_oa'].reshape(MLA_HEADS, HEAD_PAD, D_MODEL)[:, :D_V].reshape(MLA_HEADS * D_V, D_MODEL)
    for n in ('w_glu', 'w_o_ssm', 'w_mem_kv', 'w_o_cross', 'w_out', 'w_up', 'w_down', 'conv_w'):
        o[n] = g[n]
    for n in ('norm_mix_g', 'q_a_norm_g', 'kv_a_norm_g', 'b_glu', 'mem_norm_g', 'xq_norm_g', 'xk_norm_g', 'b_gate',
              'norm_ffn_g', 'conv_b'):
        o[n] = g[n].reshape(-1)
    o['q_norm_g'] = g['q_norm_g'].reshape(-1)[:D_QK]
    o['k_norm_g'] = g['k_norm_g'].reshape(-1)[:D_QK]
    o['ssm_d'] = g['ssm_d'].reshape(SSM_GROUPS, SSM_GROUP_CH)
    o['ssm_lambda_re'] = g['lr']
    o['ssm_lambda_im'] = g['li']
    o['ssm_log_dt'] = g['log_dt'].reshape(SSM_GROUPS)
    o['ssm_b_re'] = g['br'].transpose(1, 2, 0)
    o['ssm_b_im'] = g['bi'].transpose(1, 2, 0)
    dc = g['c_mat']
    o['ssm_c_re'] = _blockdiag_t(dc[:, :512], SSM_STATE, SSM_GROUP_CH).transpose(0, 1, 3, 2).reshape(SSM_GROUPS, SSM_GROUP_CH, SSM_STATE)
    o['ssm_c_im'] = -_blockdiag_t(dc[:, 512:], SSM_STATE, SSM_GROUP_CH).transpose(0, 1, 3, 2).reshape(SSM_GROUPS, SSM_GROUP_CH, SSM_STATE)
    return o


def _local_step(x, mem, pos, target, w):
    l = x.shape[0]
    tm = min(512, l)
    tabs = _rope_tables(pos.astype(F32).reshape(l, 1))
    ps = [_prep_layer(w, i) for i in range(DEPTH)]
    saved = []
    h = x
    for i in range(DEPTH):
        h, sv = _layer_fwd("l%d" % i, h, tabs, mem, ps[i])
        saved.append(sv)

    def loss_fn(i, y, t):
        e = y - t
        per_tok = jnp.sum(e * e, axis=-1, keepdims=True) * (1.0 / D_MODEL)
        tot = 0.5 * jnp.sum(per_tok, axis=0, keepdims=True)
        return e * (1.0 / D_MODEL), jnp.broadcast_to(tot, (1, LANE))

    dy, loss = _rows(loss_fn, name="loss", n=l // tm, ins=[h, target], in_specs=[_rt(tm, D_MODEL)] * 2,
                     outs=[_sds((l, D_MODEL)), _sds((1, LANE))], out_specs=[_rt(tm, D_MODEL), _full((1, LANE))], n_acc=1)
    grads = []
    d = dy
    for i in reversed(range(DEPTH)):
        d, g = _layer_bwd("l%d" % i, d, saved[i], tabs, mem, ps[i])
        grads.append(_unprep_grads(g))
    return loss[0, 0], d, grads[::-1]


def _sum_picked(name, slots, pick, extra, out_dtype):
    _, r, c = slots.shape
    e = extra.shape[0]
    tr = _row_tile(r)

    def body(pk, s_ref, x_ref, o_ref):
        acc = s_ref[...].astype(F32)
        for k in range(e):
            acc = acc + x_ref[k].astype(F32)
        o_ref[...] = acc.astype(o_ref.dtype)

    grid_spec = pltpu.PrefetchScalarGridSpec(
        num_scalar_prefetch=1, grid=(r // tr,),
        in_specs=[pl.BlockSpec((None, tr, c), lambda i, pk: (pk[0], i, 0)), pl.BlockSpec((e, tr, c), lambda i, pk: (0, i, 0))],
        out_specs=pl.BlockSpec((tr, c), lambda i, pk: (i, 0)))
    return pl.pallas_call(body, name=name, grid_spec=grid_spec, out_shape=_sds((r, c), out_dtype),
                          compiler_params=_params(("arbitrary",), 48))(pick, slots, extra)


def _row_tile(r):
    for t in (256, 128, 64, 32, 16, 8):
        if r % t == 0:
            return t
    return r


def _adamw(name, parts, w, m, v):
    r, cw = w.shape
    tr = _row_tile(r)
    np_ = len(parts)

    def fn(i, *vals):
        wv, mv, vv = vals[np_:]
        terms = []
        for pv in vals[:np_]:
            terms += [pv] if pv.ndim == 2 else [pv[k] for k in range(pv.shape[0])]
        g = terms[0]
        for t in terms[1:]:
            g = g + t
        mn = ADAM_B1 * mv + (1.0 - ADAM_B1) * g
        vn = ADAM_B2 * vv + (1.0 - ADAM_B2) * (g * g)
        m_hat = mn / (1.0 - ADAM_B1 ** ADAM_STEP)
        v_hat = vn / (1.0 - ADAM_B2 ** ADAM_STEP)
        delta = -ADAM_LR * (m_hat / (jnp.sqrt(v_hat) + ADAM_EPS) + ADAM_WD * wv)
        return g, delta, mn, vn

    pspecs = [_rt(tr, cw) if p.ndim == 2 else pl.BlockSpec((p.shape[0], tr, cw), lambda i: (0, i, 0)) for p in parts]
    return _rows(fn, name=name, n=r // tr, ins=[*parts, w, m, v], in_specs=pspecs + [_rt(tr, cw)] * 3,
                 outs=[_sds((r, cw))] * 4, out_specs=[_rt(tr, cw)] * 4)


def _shard_of(a, axis, k):
    n = a.shape[axis] // 4
    return lax.slice_in_dim(a, k * n, (k + 1) * n, axis=axis)


def _step(a):
    x = a['x'][0]
    mem = a['mem'][0]
    pos = a['positions'][0]
    target = a['loss_target'][0]

    me = 2 * lax.axis_index("x") + lax.axis_index("y")

    mine = [a[n] if n == 'conv_w' else a[n].astype(BF16) for n in SHARDED]
    got = _gather_d2d(_gather_ici(mine))
    w = {}
    for n, own, y in zip(SHARDED, mine, got):
        ax = SHARD_AXIS[n] - 1
        w[n] = [jnp.concatenate([jnp.where(me == k, own[i], y[k, i]) for k in range(4)], axis=ax) for i in range(DEPTH)]
    for n in SMALL:
        w[n] = a[n]

    loss, grad_x, grads = _local_step(x, mem, pos, target, w)

    mc = lax.axis_index("c")
    mc1 = mc.astype(jnp.int32).reshape(1)
    me1 = me.astype(jnp.int32).reshape(1)
    gsh = []
    for n in SHARDED:
        ax = SHARD_AXIS[n] - 1
        gsh.append(jnp.stack([jnp.stack([_shard_of(grads[i][n], ax, k) for k in range(4)], axis=0)
                              for i in range(DEPTH)], axis=0).astype(BF16))
    sib = _swap_d2d("comm_reduce_pair", gsh, other_layer=True)
    pair = []
    for n, g, s in zip(SHARDED, gsh, sib):
        rows, cols = g.shape[-2:]
        pair.append(_sum_picked("sum2_" + n, g.reshape(DEPTH, 4 * rows, cols), mc1, s.reshape(1, 4 * rows, cols), BF16)
                    .reshape(4, rows, cols))
    got = _reduce_ici(pair)
    parts = [_sum_picked("sum4_" + n, p4, me1, g3, F32) for n, p4, g3 in zip(SHARDED, pair, got)]
    others = _swap_d2d("comm_reduce_d2d", parts, other_layer=False)
    res_sh = []
    for n, part, other in zip(SHARDED, parts, others):
        cols = part.shape[-1]
        full = jnp.where(mc == 0, jnp.stack([part, other], axis=0), jnp.stack([other, part], axis=0))
        res = _adamw("adamw_" + n, [full.reshape(-1, cols)], *[a[pre + n].reshape(-1, cols) for pre in ('', 'm_', 'v_')])
        res_sh.append([r.reshape(a[n].shape) for r in res])
    res_sh = [[res_sh[j][kind] for j in range(len(SHARDED))] for kind in range(4)]

    sm_shapes = [a[n].shape for n in SMALL] + [(1,)]
    gsm = _pack([jnp.stack([grads[i][n] for i in range(DEPTH)], axis=0) for n in SMALL] + [loss.reshape(1)], 8, F32)
    alls = _all_exchange("comm_reduce_small", gsm)
    zero1 = jnp.zeros((1,), F32)
    res_sm = _adamw("adamw_small", [alls], *[_pack([a[pre + n] for n in SMALL] + [zero1], 8, F32) for pre in ('', 'm_', 'v_')])
    res_sm = [_unpack(r, sm_shapes) for r in res_sm]
    loss = res_sm[0][-1][0]

    outs = [loss, grad_x[None]]
    for kind in range(4):
        byname = dict(zip(SHARDED, res_sh[kind]))
        byname.update(zip(SMALL, res_sm[kind]))
        outs += [byname[n] for n in WEIGHTS]
    return tuple(outs)


def kernel(x, mem, positions, norm_mix_g, w_in, q_a_norm_g, w_q_b, kv_a_norm_g, w_kv_b, q_norm_g, k_norm_g, w_o_mla, ssm_lambda_re, ssm_lambda_im, ssm_log_dt, ssm_b_re, ssm_b_im, ssm_c_re, ssm_c_im, ssm_d, w_glu, b_glu, w_o_ssm, mem_norm_g, w_mem_kv, xq_norm_g, xk_norm_g, w_o_cross, b_gate, w_out, norm_ffn_g, w_up, conv_w, conv_b, w_down, loss_target, m_norm_mix_g, m_w_in, m_q_a_norm_g, m_w_q_b, m_kv_a_norm_g, m_w_kv_b, m_q_norm_g, m_k_norm_g, m_w_o_mla, m_ssm_lambda_re, m_ssm_lambda_im, m_ssm_log_dt, m_ssm_b_re, m_ssm_b_im, m_ssm_c_re, m_ssm_c_im, m_ssm_d, m_w_glu, m_b_glu, m_w_o_ssm, m_mem_norm_g, m_w_mem_kv, m_xq_norm_g, m_xk_norm_g, m_w_o_cross, m_b_gate, m_w_out, m_norm_ffn_g, m_w_up, m_conv_w, m_conv_b, m_w_down, v_norm_mix_g, v_w_in, v_q_a_norm_g, v_w_q_b, v_kv_a_norm_g, v_w_kv_b, v_q_norm_g, v_k_norm_g, v_w_o_mla, v_ssm_lambda_re, v_ssm_lambda_im, v_ssm_log_dt, v_ssm_b_re, v_ssm_b_im, v_ssm_c_re, v_ssm_c_im, v_ssm_d, v_w_glu, v_b_glu, v_w_o_ssm, v_mem_norm_g, v_w_mem_kv, v_xq_norm_g, v_xk_norm_g, v_w_o_cross, v_b_gate, v_w_out, v_norm_ffn_g, v_w_up, v_conv_w, v_conv_b, v_w_down):
    return _step(dict(locals()))
```

```python
import functools
import math

import numpy as np
import jax
import jax.numpy as jnp
from jax import lax
from jax.experimental import pallas as pl
from jax.experimental.pallas import tpu as pltpu

F32 = jnp.float32
BF16 = jnp.bfloat16
MESH = pl.DeviceIdType.MESH

DEPTH = 2
D_MODEL = 1024
EPS = 1e-6
MLA_HEADS = 8
Q_LORA = 384
KV_LORA = 256
D_NOPE = 64
D_ROPE = 32
D_QK = D_NOPE + D_ROPE
D_V = 64
HEAD_PAD = 128
MLA_PAD = MLA_HEADS * HEAD_PAD
ROPE_THETA = 10000.0
SSM_GROUPS = 32
SSM_GROUP_CH = 16
SSM_WIDTH = 512
SSM_STATE = 64
SSM_LANES = SSM_GROUPS * SSM_STATE
SSM_JB = 4
X_HEADS = 4
X_HEAD_DIM = 128
X_WIDTH = 512
D_FF = 2816
SMALL_W = Q_LORA + KV_LORA + HEAD_PAD
SCAN_SEGS = 32
LANE = 128
NEG = -1e30

ADAM_LR = 0.001
ADAM_B1 = 0.9
ADAM_B2 = 0.999
ADAM_EPS = 1e-08
ADAM_WD = 0.01
ADAM_STEP = 10

WEIGHTS = ['norm_mix_g', 'w_in', 'q_a_norm_g', 'w_q_b', 'kv_a_norm_g', 'w_kv_b', 'q_norm_g', 'k_norm_g', 'w_o_mla',
           'ssm_lambda_re', 'ssm_lambda_im', 'ssm_log_dt', 'ssm_b_re', 'ssm_b_im', 'ssm_c_re', 'ssm_c_im', 'ssm_d',
           'w_glu', 'b_glu', 'w_o_ssm', 'mem_norm_g', 'w_mem_kv', 'xq_norm_g', 'xk_norm_g', 'w_o_cross', 'b_gate',
           'w_out', 'norm_ffn_g', 'w_up', 'conv_w', 'conv_b', 'w_down']
SHARD_AXIS = {'w_in': 2, 'w_q_b': 2, 'w_kv_b': 2, 'w_o_mla': 2, 'w_glu': 1, 'w_o_ssm': 2, 'w_mem_kv': 1,
              'w_o_cross': 2, 'w_out': 1, 'w_up': 2, 'conv_w': 2, 'w_down': 1}
SHARDED = [n for n in WEIGHTS if n in SHARD_AXIS]
GATHER_BF16 = [n for n in SHARDED if n != 'conv_w']
SMALL = [n for n in WEIGHTS if n not in SHARD_AXIS]


def _bf(v):
    return v.astype(BF16)


def _mm(a, b):
    return jnp.dot(_bf(a), _bf(b), preferred_element_type=F32)


def _mm_nt(a, b):
    return lax.dot_general(_bf(a), _bf(b), (((1,), (1,)), ((), ())), preferred_element_type=F32)


def _mm_tn(a, b):
    return lax.dot_general(_bf(a), _bf(b), (((0,), (0,)), ((), ())), preferred_element_type=F32)


def _rms(v, g, n):
    ms = jnp.sum(v * v, axis=-1, keepdims=True) * (1.0 / n)
    return (v * lax.rsqrt(ms + EPS)) * g


def _head_rms(v, g, heads, n):
    return jnp.concatenate([_rms(v[:, h * LANE:(h + 1) * LANE], g, n) for h in range(heads)], axis=-1)


def _rope(v, c, s1, s2):
    return v * c + pltpu.roll(v, LANE - 16, 1) * s1 + pltpu.roll(v, 16, 1) * s2


def _rope_t(g, c, s1, s2):
    return g * c + pltpu.roll(g * s1, 16, 1) + pltpu.roll(g * s2, LANE - 16, 1)


def _heads(fn, v, heads, *tabs):
    return jnp.concatenate([fn(v[:, h * LANE:(h + 1) * LANE], *tabs) for h in range(heads)], axis=-1)


def _gelu(y):
    return y * (0.5 * (1.0 + jnp.tanh(math.sqrt(2.0 / math.pi) * (y + 0.044715 * (y * y * y)))))


def _silu(g):
    return g * jax.nn.sigmoid(g)


def _colsum(v):
    return jnp.sum(v, axis=0, keepdims=True)


def _row_select(rows, n):
    rid = lax.broadcasted_iota(jnp.int32, (n, rows[0].shape[-1]), 0)
    out = jnp.zeros((n, rows[0].shape[-1]), F32)
    for k, r in enumerate(rows):
        out = jnp.where(rid == k, jnp.broadcast_to(r, out.shape), out)
    return out


def _params(sem, vmem_mb):
    return pltpu.CompilerParams(dimension_semantics=sem, vmem_limit_bytes=vmem_mb * 1024 * 1024)


def _rt(tm, w, cb=0):
    return pl.BlockSpec((tm, w), lambda i: (i, cb))


def _full(shape):
    nd = len(shape)
    return pl.BlockSpec(tuple(shape), lambda i: (0,) * nd)


def _rows(fn, *, name, n, ins, in_specs, outs, out_specs, n_acc=0, vmem=48):
    n_in = len(ins)
    n_out = len(outs)

    def body(*refs):
        i = pl.program_id(0)
        res = fn(i, *[r[...] for r in refs[:n_in]])
        if not isinstance(res, (tuple, list)):
            res = (res,)
        assert len(res) == n_out, (name, len(res), n_out)
        for k, (r, v) in enumerate(zip(refs[n_in:], res)):
            if k < n_out - n_acc:
                r[...] = v.astype(r.dtype)
            else:
                @pl.when(i == 0)
                def _():
                    r[...] = v

                @pl.when(i > 0)
                def _():
                    r[...] += v

    return pl.pallas_call(
        body, name=name, grid=(n,), in_specs=list(in_specs), out_specs=tuple(out_specs), out_shape=tuple(outs),
        compiler_params=_params(("arbitrary",), vmem))(*ins)


def _sds(shape, dtype=F32):
    return jax.ShapeDtypeStruct(tuple(shape), dtype)


def _tile_n(n, cap=1536):
    best = None
    for t in range(LANE, min(n, cap) + 1, LANE):
        if n % t == 0:
            best = t
    if best is None or n <= 1408:
        return n
    return best


def _matmul(name, pairs, m, n, *, nt=False, rms_gain=None, resid=None, out_dtype=F32, tm=512, vmem=56):
    tm = min(tm, m)
    tn = _tile_n(n)
    ks = [a.shape[1] for a, _ in pairs]
    np_ = len(pairs)

    def body(*refs):
        a_refs = refs[:np_]
        b_refs = refs[np_:2 * np_]
        k = 2 * np_
        g_ref = None
        r_ref = None
        if rms_gain is not None:
            g_ref = refs[k]
            k += 1
        if resid is not None:
            r_ref = refs[k]
            k += 1
        o_ref = refs[k]
        scr = refs[k + 1:]
        j = pl.program_id(1)

        @pl.when(j == 0)
        def _():
            for p in range(np_):
                a = a_refs[p][...]
                if p == 0 and g_ref is not None:
                    a = _rms(a.astype(F32), g_ref[...], ks[0])
                scr[p][...] = a.astype(BF16)

        acc = None
        for p in range(np_):
            b = b_refs[p][...].astype(BF16)
            if nt:
                t = lax.dot_general(scr[p][...], b, (((1,), (1,)), ((), ())), preferred_element_type=F32)
            else:
                t = jnp.dot(scr[p][...], b, preferred_element_type=F32)
            acc = t if acc is None else acc + t
        if r_ref is not None:
            acc = acc + r_ref[...]
        o_ref[...] = acc.astype(o_ref.dtype)

    in_specs = [pl.BlockSpec((tm, kk), lambda i, j: (i, 0)) for kk in ks]
    if nt:
        in_specs += [pl.BlockSpec((tn, kk), lambda i, j: (j, 0)) for kk in ks]
    else:
        in_specs += [pl.BlockSpec((kk, tn), lambda i, j: (0, j)) for kk in ks]
    ins = [a for a, _ in pairs] + [b for _, b in pairs]
    if rms_gain is not None:
        in_specs.append(pl.BlockSpec((1, ks[0]), lambda i, j: (0, 0)))
        ins.append(rms_gain)
    if resid is not None:
        in_specs.append(pl.BlockSpec((tm, tn), lambda i, j: (i, j)))
        ins.append(resid)
    return pl.pallas_call(
        body, name=name, grid=(m // tm, n // tn), in_specs=in_specs,
        out_specs=pl.BlockSpec((tm, tn), lambda i, j: (i, j)), out_shape=_sds((m, n), out_dtype),
        scratch_shapes=[pltpu.VMEM((tm, kk), BF16) for kk in ks],
        compiler_params=_params(("arbitrary", "arbitrary"), vmem))(*ins)


def _matmul_tn(name, a, b, *, rms_gain=None, tl=512, vmem=56):
    l, ka = a.shape
    n = b.shape[1]
    tl = min(tl, l)
    tn = _tile_n(n, 1536)

    def body(*refs):
        if rms_gain is not None:
            a_ref, b_ref, g_ref, o_ref = refs
        else:
            a_ref, b_ref, o_ref = refs
        t = pl.program_id(1)
        av = a_ref[...]
        if rms_gain is not None:
            av = _rms(av.astype(F32), g_ref[...], ka)
        v = _mm_tn(av, b_ref[...])

        @pl.when(t == 0)
        def _():
            o_ref[...] = v

        @pl.when(t > 0)
        def _():
            o_ref[...] += v

    in_specs = [pl.BlockSpec((tl, ka), lambda j, t: (t, 0)), pl.BlockSpec((tl, tn), lambda j, t: (t, j))]
    ins = [a, b]
    if rms_gain is not None:
        in_specs.append(pl.BlockSpec((1, ka), lambda j, t: (0, 0)))
        ins.append(rms_gain)
    return pl.pallas_call(
        body, name=name, grid=(n // tn, l // tl), in_specs=in_specs,
        out_specs=pl.BlockSpec((ka, tn), lambda j, t: (0, j)), out_shape=_sds((ka, n)),
        compiler_params=_params(("arbitrary", "arbitrary"), vmem))(*ins)


ATT_HEADS_PER_STEP = 2
ATT_W = ATT_HEADS_PER_STEP * LANE
ATT_GROUPS = MLA_HEADS // ATT_HEADS_PER_STEP
LOG2E = math.log2(math.e)
ATT_FWD_TILE = 1024
ATT_BWD_TILE = 1024
ATT_BWD_HEADS = 1
ATT_SCALE = D_QK ** -0.5
ATT_QSCALE = ATT_SCALE * LOG2E


def _tri_tables(nq, by_k):
    qs, ks = [], []
    if by_k:
        for ki in range(nq):
            for qi in range(ki, nq):
                qs.append(qi)
                ks.append(ki)
    else:
        for qi in range(nq):
            for ki in range(qi + 1):
                qs.append(qi)
                ks.append(ki)
    return jnp.asarray(np.array(qs, np.int32)), jnp.asarray(np.array(ks, np.int32))


def _causal_keep(shape, transposed):
    r = lax.broadcasted_iota(jnp.int32, shape, 0)
    c = lax.broadcasted_iota(jnp.int32, shape, 1)
    return (r <= c) if transposed else (c <= r)


def _nt16(a, b):
    return lax.dot_general(a, b, (((1,), (1,)), ((), ())), preferred_element_type=F32)


def _row_form(col):
    return jnp.transpose(jnp.broadcast_to(col, (col.shape[0], LANE)))[:8]


def _att_call(body, name, l, tq, tabs, ins, in_specs, outs, out_specs, scratch=(), groups=ATT_GROUPS, vmem=48):
    grid_spec = pltpu.PrefetchScalarGridSpec(
        num_scalar_prefetch=2, grid=(groups, tabs[0].shape[0]), in_specs=in_specs, out_specs=out_specs,
        scratch_shapes=list(scratch))
    return pl.pallas_call(body, name=name, grid_spec=grid_spec, out_shape=outs,
                          compiler_params=_params(("arbitrary", "arbitrary"), vmem))(*tabs, *ins)


def _flash_fwd(name, q, k, v_t):
    l = q.shape[0]
    tq = min(ATT_FWD_TILE, l)
    nq = l // tq
    tabs = _tri_tables(nq, by_k=False)

    def body(qt, kt, q_ref, k_ref, vt_ref, o_ref, lset_ref, m_s, acc_s):
        t = pl.program_id(1)
        qi = qt[t]
        ki = kt[t]
        sls = [slice(h * LANE, (h + 1) * LANE) for h in range(ATT_HEADS_PER_STEP)]

        @pl.when(ki == 0)
        def _():
            m_s[...] = jnp.full(m_s.shape, NEG, F32)
            acc_s[...] = jnp.zeros(acc_s.shape, F32)

        def step(masked):
            sts = [_nt16(k_ref[:, sl], q_ref[:, sl]) for sl in sls]
            for h, sl in enumerate(sls):
                st = sts[h]
                if masked:
                    st = jnp.where(_causal_keep(st.shape, True), st, NEG)
                m_old = m_s[h][:1]
                m_new = jnp.maximum(m_old, jnp.max(st, axis=0, keepdims=True))
                alpha = jnp.exp2(m_old - m_new)
                pt = jnp.exp2(st - m_new).astype(BF16)
                acc_s[sl, :] = alpha * acc_s[sl, :] + jnp.dot(vt_ref[sl, :], pt, preferred_element_type=F32)
                m_s[h] = jnp.broadcast_to(m_new, (8, tq))

        @pl.when(ki < qi)
        def _():
            step(False)

        @pl.when(ki == qi)
        def _():
            step(True)
            row = lax.broadcasted_iota(jnp.int32, (LANE, tq), 0)
            for h, sl in enumerate(sls):
                acc = acc_s[sl, :]
                lsum = acc[D_V:D_V + 1, :]
                o_ref[:, sl] = jnp.transpose(jnp.where(row < D_V, acc / lsum, 0.0))
                lset_ref[h * 8:(h + 1) * 8, :] = m_s[h] + jnp.log2(lsum)

    qspec = pl.BlockSpec((tq, ATT_W), lambda g, t, qt, kt: (qt[t], g))
    kspec = pl.BlockSpec((tq, ATT_W), lambda g, t, qt, kt: (kt[t], g))
    vspec = pl.BlockSpec((ATT_W, tq), lambda g, t, qt, kt: (g, kt[t]))
    rspec = pl.BlockSpec((8 * ATT_HEADS_PER_STEP, tq), lambda g, t, qt, kt: (g, qt[t]))
    return _att_call(
        body, name, l, tq, tabs, [q, k, v_t], [qspec, kspec, vspec],
        (_sds((l, MLA_PAD)), _sds((8 * MLA_HEADS, l))), (qspec, rspec),
        scratch=[pltpu.VMEM((ATT_HEADS_PER_STEP, 8, tq), F32), pltpu.VMEM((ATT_W, tq), F32)])


def _flash_bwd(name, q, k, v, k_t, o, lse_t, do):
    l = q.shape[0]
    tq = min(ATT_BWD_TILE, l)
    nq = l // tq
    hb = ATT_BWD_HEADS
    wb = hb * LANE

    def delta_fn(i, dov, ov):
        rows = []
        for h in range(MLA_HEADS):
            sl = slice(h * LANE, (h + 1) * LANE)
            rows.append(_row_form(jnp.sum(dov[:, sl] * ov[:, sl], axis=-1, keepdims=True)))
        return jnp.concatenate(rows, axis=0), dov

    delta_t, do16 = _rows(
        delta_fn, name=name + "_delta", n=nq, ins=[do, o], in_specs=[_rt(tq, MLA_PAD)] * 2,
        outs=[_sds((8 * MLA_HEADS, l)), _sds((l, MLA_PAD), BF16)],
        out_specs=[pl.BlockSpec((8 * MLA_HEADS, tq), lambda i: (0, i)), _rt(tq, MLA_PAD)])

    def body(qt, kt, q_ref, k_ref, v_ref, do_ref, kt_ref, lset_ref, dlt_ref, dk_ref, dv_ref, dqt_ref):
        t = pl.program_id(1)
        qi = qt[t]
        ki = kt[t]
        sls = [slice(h * LANE, (h + 1) * LANE) for h in range(hb)]

        @pl.when(ki == 0)
        def _():
            dqt_ref[qi] = jnp.zeros((wb, tq), F32)

        def step(masked):
            sts = [_nt16(k_ref[:, sl], q_ref[:, sl]) for sl in sls]
            dpts = [_nt16(v_ref[:, sl], do_ref[:, sl]) for sl in sls]
            for h, sl in enumerate(sls):
                st = sts[h]
                if masked:
                    st = jnp.where(_causal_keep(st.shape, True), st, NEG)
                pt = jnp.exp2(st - lset_ref[h * 8:(h + 1) * 8, :][:1])
                dst = (pt * (dpts[h] - dlt_ref[h * 8:(h + 1) * 8, :][:1])).astype(BF16)
                dv_ref[:, sl] += jnp.dot(pt.astype(BF16), do_ref[:, sl], preferred_element_type=F32)
                dk_ref[:, sl] += jnp.dot(dst, q_ref[:, sl], preferred_element_type=F32)
                dqt_ref[qi, sl, :] += jnp.dot(kt_ref[sl, :], dst, preferred_element_type=F32)

        @pl.when(qi == ki)
        def _():
            dk_ref[...] = jnp.zeros(dk_ref.shape, F32)
            dv_ref[...] = jnp.zeros(dv_ref.shape, F32)
            step(True)
            dqt_ref[qi] = dqt_ref[qi] * ATT_SCALE

        @pl.when(qi > ki)
        def _():
            step(False)

        @pl.when(qi == nq - 1)
        def _():
            dk_ref[...] = dk_ref[...] * (1.0 / LOG2E)

    tabs_k = _tri_tables(nq, by_k=True)
    qspec = pl.BlockSpec((tq, wb), lambda g, t, qt, kt: (qt[t], g))
    kspec = pl.BlockSpec((tq, wb), lambda g, t, qt, kt: (kt[t], g))
    ktspec = pl.BlockSpec((wb, tq), lambda g, t, qt, kt: (g, kt[t]))
    rspec = pl.BlockSpec((8 * hb, tq), lambda g, t, qt, kt: (g, qt[t]))
    dqspec = pl.BlockSpec((nq, wb, tq), lambda g, t, qt, kt: (0, g, 0))
    dk, dv, dq_t = _att_call(body, name + "_dqkv", l, tq, tabs_k, [q, k, v, do16, k_t, lse_t, delta_t],
                             [qspec, kspec, kspec, qspec, ktspec, rspec, rspec],
                             (_sds((l, MLA_PAD)), _sds((l, MLA_PAD)), _sds((nq, MLA_PAD, tq))), (kspec, kspec, dqspec),
                             groups=MLA_HEADS // hb, vmem=56)
    return dq_t, dk, dv


def _cmul(ar, ai, br, bi):
    return ar * br - ai * bi, ar * bi + ai * br


def _scan(name, x_re, x_im, a_re, a_im, reverse):
    l, lanes = x_re.shape
    ns = SCAN_SEGS
    tl = l // ns
    steps = int(math.log2(tl))
    assert 2 ** steps == tl and tl * ns == l

    def body(xr_ref, xi_ref, ar_ref, ai_ref, sr_ref, si_ref):
        a_r1 = ar_ref[...]
        a_i1 = ai_ref[...]
        a_r = jnp.broadcast_to(a_r1, (ns, LANE))
        a_i = jnp.broadcast_to(a_i1, (ns, LANE))

        def rows(t):
            t = (tl - 1 - t) if reverse else t
            return pl.ds(pl.multiple_of(t * ns, ns), ns)

        def local(t, carry):
            cr, ci = carry
            r = rows(t)
            pr, pi = _cmul(a_r, a_i, cr, ci)
            return pr + xr_ref[r, :], pi + xi_ref[r, :]

        zero = jnp.zeros((ns, LANE), F32)
        e_r, e_i = lax.fori_loop(0, tl, local, (zero, zero), unroll=min(8, tl))
        p_r, p_i = a_r1, a_i1
        for _ in range(steps):
            p_r, p_i = _cmul(p_r, p_i, p_r, p_i)
        rid = lax.broadcasted_iota(jnp.int32, (ns, LANE), 0)
        c_r = jnp.zeros((1, LANE), F32)
        c_i = jnp.zeros((1, LANE), F32)
        in_r, in_i = zero, zero
        order = range(ns - 2, -1, -1) if reverse else range(1, ns)
        for kk in order:
            src = kk + 1 if reverse else kk - 1
            ek_r = jnp.sum(jnp.where(rid == src, e_r, 0.0), axis=0, keepdims=True)
            ek_i = jnp.sum(jnp.where(rid == src, e_i, 0.0), axis=0, keepdims=True)
            q_r, q_i = _cmul(p_r, p_i, c_r, c_i)
            c_r, c_i = q_r + ek_r, q_i + ek_i
            in_r = jnp.where(rid == kk, jnp.broadcast_to(c_r, (ns, LANE)), in_r)
            in_i = jnp.where(rid == kk, jnp.broadcast_to(c_i, (ns, LANE)), in_i)

        def final(t, carry):
            cr, ci = carry
            r = rows(t)
            pr, pi = _cmul(a_r, a_i, cr, ci)
            nr, ni = pr + xr_ref[r, :], pi + xi_ref[r, :]
            sr_ref[r, :] = nr
            si_ref[r, :] = ni
            return nr, ni

        lax.fori_loop(0, tl, final, (in_r, in_i), unroll=min(8, tl))

    xs = pl.BlockSpec((l, LANE), lambda j: (0, j))
    as_ = pl.BlockSpec((1, LANE), lambda j: (0, j))
    return pl.pallas_call(
        body, name=name, grid=(lanes // LANE,), in_specs=[xs, xs, as_, as_], out_specs=(xs, xs),
        out_shape=(_sds((l, lanes)), _sds((l, lanes))),
        compiler_params=_params(("arbitrary",), 48))(x_re, x_im, a_re, a_im)


ANY = pl.BlockSpec(memory_space=pl.ANY)


def _place():
    mx, my, mc = lax.axis_index("x"), lax.axis_index("y"), lax.axis_index("c")
    return mx, my, mc, [(1 - mx, my), (mx, 1 - my), (1 - mx, 1 - my)]


def _run_copies(copies):
    for cp in copies:
        cp.start()
    for cp in copies:
        cp.wait_recv()
    for cp in copies:
        cp.wait_send()


def _remote(src, dst, sems, k, dev):
    return pltpu.make_async_remote_copy(src_ref=src, dst_ref=dst, send_sem=sems[0].at[k], recv_sem=sems[1].at[k],
                                        device_id=dev, device_id_type=MESH)


def _copy_call(body, name, ins, outs, n_copies, aliases=None):
    return pl.pallas_call(
        body, name=name, in_specs=[ANY] * len(ins), out_specs=[ANY] * len(outs), out_shape=list(outs),
        input_output_aliases=aliases or {},
        scratch_shapes=[pltpu.SemaphoreType.DMA((n_copies,)), pltpu.SemaphoreType.DMA((n_copies,))])(*ins)


def _gather_ici(xs):
    n = len(xs)

    def body(*refs):
        x_refs, y_refs, sems = refs[:n], refs[n:2 * n], refs[2 * n:]
        mx, my, mc, peers = _place()
        me = 2 * mx + my
        _run_copies([_remote(x_refs[i].at[mc], y_refs[i].at[me, mc], sems, 3 * i + j, (px, py, mc))
                     for i in range(n) for j, (px, py) in enumerate(peers)])

    return _copy_call(body, "comm_gather_ici", xs, [_sds((4,) + x.shape, x.dtype) for x in xs], 3 * n)


def _gather_d2d(ys):
    n = len(ys)

    def body(*refs):
        y_in, y_out, sems = refs[:n], refs[n:2 * n], refs[2 * n:]
        mx, my, mc, peers = _place()
        _run_copies([_remote(y_in[i].at[2 * px + py, mc], y_out[i].at[2 * px + py, mc], sems, 3 * i + j, (mx, my, 1 - mc))
                     for i in range(n) for j, (px, py) in enumerate(peers)])

    return _copy_call(body, "comm_gather_d2d", ys, [_sds(y.shape, y.dtype) for y in ys], 3 * n,
                      aliases={i: i for i in range(n)})


def _reduce_ici(gs):
    n = len(gs)

    def body(*refs):
        g_refs, y_refs, sems = refs[:n], refs[n:2 * n], refs[2 * n:]
        mx, my, mc, peers = _place()
        _run_copies([_remote(g_refs[i].at[2 * px + py], y_refs[i].at[j], sems, 3 * i + j, (px, py, mc))
                     for i in range(n) for j, (px, py) in enumerate(peers)])

    return _copy_call(body, "comm_reduce_ici", gs, [_sds((3,) + g.shape[1:], g.dtype) for g in gs], 3 * n)


def _swap_d2d(name, ps, other_layer):
    n = len(ps)

    def body(*refs):
        p_refs, o_refs, sems = refs[:n], refs[n:2 * n], refs[2 * n:]
        mx, my, mc, _ = _place()
        _run_copies([_remote(p_refs[i].at[1 - mc] if other_layer else p_refs[i], o_refs[i], sems, i, (mx, my, 1 - mc))
                     for i in range(n)])

    outs = [_sds(p.shape[1:] if other_layer else p.shape, p.dtype) for p in ps]
    return _copy_call(body, name, ps, outs, n)


def _all_exchange(name, src):
    def body(x_ref, y_ref, send_sems, recv_sems, local_sem):
        mx, my, mc = lax.axis_index("x"), lax.axis_index("y"), lax.axis_index("c")
        me = 4 * mx + 2 * my + mc
        own = pltpu.make_async_copy(x_ref, y_ref.at[me], local_sem)
        own.start()
        copies = []
        for j in range(1, 8):
            px = (1 - mx) if (j & 4) else mx
            py = (1 - my) if (j & 2) else my
            pc = (1 - mc) if (j & 1) else mc
            cp = pltpu.make_async_remote_copy(
                src_ref=x_ref, dst_ref=y_ref.at[me], send_sem=send_sems.at[j - 1], recv_sem=recv_sems.at[j - 1],
                device_id=(px, py, pc), device_id_type=MESH)
            cp.start()
            copies.append(cp)
        for cp in copies:
            cp.wait_recv()
        for cp in copies:
            cp.wait_send()
        own.wait()

    return pl.pallas_call(
        body, name=name, in_specs=[ANY], out_specs=ANY, out_shape=_sds((8,) + src.shape, src.dtype),
        scratch_shapes=[pltpu.SemaphoreType.DMA((7,)), pltpu.SemaphoreType.DMA((7,)), pltpu.SemaphoreType.DMA])(src)


PACK_W = 1024


def _pack(arrs, rows_multiple, dtype):
    flat = jnp.concatenate([a.reshape(-1).astype(dtype) for a in arrs])
    n = flat.shape[0]
    unit = PACK_W * rows_multiple
    tot = -(-n // unit) * unit
    flat = jnp.pad(flat, (0, tot - n))
    return flat.reshape(tot // PACK_W, PACK_W)


def _unpack(flat, shapes):
    flat = flat.reshape(-1)
    out = []
    off = 0
    for s in shapes:
        n = int(np.prod(s))
        out.append(flat[off:off + n].reshape(s))
        off += n
    return out


def _rope_tables(pos):
    l = pos.shape[0]
    tm = min(512, l)
    inv = (np.float32(ROPE_THETA) ** (-np.arange(0, D_ROPE, 2, dtype=np.float32) / np.float32(D_ROPE))).astype(np.float32)
    lane_f = np.zeros((1, LANE), np.float32)
    lane_f[0, D_NOPE:D_NOPE + 16] = inv
    lane_f[0, D_NOPE + 16:D_NOPE + 32] = inv

    def fn(i, p, f):
        ang = p * f
        lane = lax.broadcasted_iota(jnp.int32, ang.shape, 1)
        co = jnp.cos(ang)
        si = jnp.sin(ang)
        c = jnp.where(lane < D_NOPE, 1.0, jnp.where(lane < D_QK, co, 0.0))
        s1 = jnp.where((lane >= D_NOPE) & (lane < D_NOPE + 16), -si, 0.0)
        s2 = jnp.where((lane >= D_NOPE + 16) & (lane < D_QK), si, 0.0)
        return c, s1, s2

    return _rows(fn, name="rope_tables", n=l // tm, ins=[pos, jnp.asarray(lane_f)],
                 in_specs=[_rt(tm, 1), _full((1, LANE))], outs=[_sds((l, LANE))] * 3, out_specs=[_rt(tm, LANE)] * 3)


def _ssm_param_fn(lr, li, log_dt, br, bi):
    dt = jnp.exp(log_dt)
    mag = jnp.exp(lr * dt)
    a_re = mag * jnp.cos(li * dt)
    a_im = mag * jnp.sin(li * dt)
    den = lr * lr + li * li
    e_re = a_re - 1.0
    e_im = a_im
    f_re = (e_re * lr + e_im * li) / den
    f_im = (e_im * lr - e_re * li) / den
    bb_re = f_re[None] * br - f_im[None] * bi
    bb_im = f_re[None] * bi + f_im[None] * br
    return a_re, a_im, bb_re, bb_im


def _ssm_params(name, lr, li, log_dt, br, bi):
    g, n = lr.shape
    c = br.shape[0]
    return _rows(lambda i, *v: _ssm_param_fn(*v), name=name, n=1, ins=[lr, li, log_dt, br, bi],
                 in_specs=[_full((g, n)), _full((g, n)), _full((g, 1)), _full((c, g, n)), _full((c, g, n))],
                 outs=[_sds((g, n)), _sds((g, n)), _sds((c, g, n)), _sds((c, g, n))],
                 out_specs=[_full((g, n)), _full((g, n)), _full((c, g, n)), _full((c, g, n))])


def _ssm_params_bwd(name, lr, li, log_dt, br, bi, d_are, d_aim, d_bbre, d_bbim):
    g, n = lr.shape
    c = br.shape[0]

    def fn(i, lr, li, log_dt, br, bi, g0, g1, g2, g3):
        _, vjp = jax.vjp(_ssm_param_fn, lr, li, log_dt, br, bi)
        return vjp((g0, g1, g2, g3))

    sp = [_full((g, n)), _full((g, n)), _full((g, 1)), _full((c, g, n)), _full((c, g, n))]
    return _rows(fn, name=name, n=1, ins=[lr, li, log_dt, br, bi, d_are, d_aim, d_bbre, d_bbim],
                 in_specs=sp + [_full((g, n)), _full((g, n)), _full((c, g, n)), _full((c, g, n))],
                 outs=[_sds((g, n)), _sds((g, n)), _sds((g, 1)), _sds((c, g, n)), _sds((c, g, n))], out_specs=sp)


_EYE8 = np.eye(8, dtype=np.float32)


def _blockdiag(v):
    j, g, p, q = v.shape
    m = v[:, :, :, None, :] * jnp.asarray(_EYE8)[None, :, None, :, None]
    return m.reshape(j, g * p, g * q)


def _blockdiag_t(m, p, q):
    j = m.shape[0]
    m = m.reshape(j, 8, p, 8, q)
    return jnp.sum(m * jnp.asarray(_EYE8)[None, :, None, :, None], axis=3)


def _to_perm(v, l):
    ns = SCAN_SEGS
    return v.reshape(ns, l // ns, v.shape[-1]).transpose(1, 0, 2).reshape(l, v.shape[-1])


def _from_perm(v, l):
    ns = SCAN_SEGS
    return v.reshape(l // ns, ns, v.shape[-1]).transpose(1, 0, 2).reshape(l, v.shape[-1])


def _prep_layer(w, i):
    p = {}
    w_in = w['w_in'][i]
    z = lambda n: jnp.zeros((D_MODEL, n), w_in.dtype)
    o = Q_LORA + KV_LORA
    p['w_s'] = jnp.concatenate([w_in[:, :o], z(D_NOPE), w_in[:, o:o + D_ROPE], z(HEAD_PAD - D_QK)], axis=1)
    o += D_ROPE
    p['w_u'] = w_in[:, o:o + SSM_WIDTH]
    o += SSM_WIDTH
    p['w_xq'] = w_in[:, o:o + X_WIDTH]
    o += X_WIDTH
    p['w_g'] = w_in[:, o:]
    wq = w['w_q_b'][i].reshape(Q_LORA, MLA_HEADS, D_QK)
    p['w_qb'] = jnp.pad(wq, ((0, 0), (0, 0), (0, HEAD_PAD - D_QK))).reshape(Q_LORA, MLA_PAD)
    wkv = w['w_kv_b'][i].reshape(KV_LORA, MLA_HEADS, D_NOPE + D_V)
    p['w_k'] = jnp.pad(wkv[:, :, :D_NOPE], ((0, 0), (0, 0), (0, HEAD_PAD - D_NOPE))).reshape(KV_LORA, MLA_PAD)
    p['w_v'] = jnp.pad(wkv[:, :, D_NOPE:], ((0, 0), (0, 0), (0, HEAD_PAD - D_V))).reshape(KV_LORA, MLA_PAD)
    wo = w['w_o_mla'][i].reshape(MLA_HEADS, D_V, D_MODEL)
    p['w_oa'] = jnp.pad(wo, ((0, 0), (0, HEAD_PAD - D_V), (0, 0))).reshape(MLA_PAD, D_MODEL)
    for n in ('w_glu', 'w_o_ssm', 'w_mem_kv', 'w_o_cross', 'w_out', 'w_up', 'w_down'):
        p[n] = w[n][i]
    p['conv_w'] = w['conv_w'][i]
    for n in ('norm_mix_g', 'q_a_norm_g', 'kv_a_norm_g', 'b_glu', 'mem_norm_g', 'xq_norm_g', 'xk_norm_g', 'b_gate',
              'norm_ffn_g', 'conv_b'):
        p[n] = w[n][i].reshape(1, -1)
    p['q_norm_g'] = jnp.pad(w['q_norm_g'][i], (0, HEAD_PAD - D_QK)).reshape(1, HEAD_PAD)
    p['k_norm_g'] = jnp.pad(w['k_norm_g'][i], (0, HEAD_PAD - D_QK)).reshape(1, HEAD_PAD)
    p['ssm_d'] = w['ssm_d'][i].reshape(1, SSM_WIDTH)
    p['lr'] = w['ssm_lambda_re'][i]
    p['li'] = w['ssm_lambda_im'][i]
    p['log_dt'] = w['ssm_log_dt'][i].reshape(SSM_GROUPS, 1)
    p['br'] = w['ssm_b_re'][i].transpose(2, 0, 1)
    p['bi'] = w['ssm_b_im'][i].transpose(2, 0, 1)
    cr = w['ssm_c_re'][i].reshape(SSM_JB, 8, SSM_GROUP_CH, SSM_STATE).transpose(0, 1, 3, 2)
    ci = w['ssm_c_im'][i].reshape(SSM_JB, 8, SSM_GROUP_CH, SSM_STATE).transpose(0, 1, 3, 2)
    p['c_mat'] = jnp.concatenate([_blockdiag(cr), -_blockdiag(ci)], axis=1).astype(BF16)
    return p


def _b_mat(bb_re, bb_im):
    r = bb_re.transpose(1, 0, 2).reshape(SSM_JB, 8, SSM_GROUP_CH, SSM_STATE)
    i = bb_im.transpose(1, 0, 2).reshape(SSM_JB, 8, SSM_GROUP_CH, SSM_STATE)
    return jnp.concatenate([_blockdiag(r), _blockdiag(i)], axis=2).astype(BF16)


def _qkv_fn(ps, c, s1, s2, qag, wqb, kvag, wk, wv, qng, kng):
    c_q = ps[:, :Q_LORA]
    c_kv = ps[:, Q_LORA:Q_LORA + KV_LORA]
    kr = ps[:, Q_LORA + KV_LORA:]
    cqn = _rms(c_q, qag, Q_LORA)
    ckvn = _rms(c_kv, kvag, KV_LORA)
    q_raw = _mm(cqn, wqb)
    k_raw = _mm(ckvn, wk) + jnp.concatenate([kr] * MLA_HEADS, axis=-1)
    v = _mm(ckvn, wv)
    q = _heads(_rope, _head_rms(q_raw, qng, MLA_HEADS, D_QK), MLA_HEADS, c, s1, s2)
    k = _heads(_rope, _head_rms(k_raw, kng, MLA_HEADS, D_QK), MLA_HEADS, c, s1, s2)
    lane = lax.broadcasted_iota(jnp.int32, v.shape, 1)
    v = jnp.where((lane & (LANE - 1)) == D_V, 1.0, v)
    return q * ATT_QSCALE, k, v


def _layer_fwd(name, x, tabs, mem, p):
    l = x.shape[0]
    tm = min(512, l)
    nt = l // tm
    sv = {'x0': x}
    sv['p_g'] = _matmul(name + "_in_g", [(x, p['w_g'])], l, 3 * D_MODEL, rms_gain=p['norm_mix_g'])
    sv['p_u'] = _matmul(name + "_in_u", [(x, p['w_u'])], l, SSM_WIDTH, rms_gain=p['norm_mix_g'])
    sv['p_xq'] = _matmul(name + "_in_xq", [(x, p['w_xq'])], l, X_WIDTH, rms_gain=p['norm_mix_g'])
    sv['p_s'] = _matmul(name + "_in_s", [(x, p['w_s'])], l, SMALL_W, rms_gain=p['norm_mix_g'])

    qkv_consts = [p['q_a_norm_g'], p['w_qb'], p['kv_a_norm_g'], p['w_k'], p['w_v'], p['q_norm_g'], p['k_norm_g']]
    qkv_cspecs = [_full(a.shape) for a in qkv_consts]
    def qkv_fwd(i, *a):
        qv, kv, vv = _qkv_fn(*a)
        return qv, kv, vv, jnp.transpose(kv), jnp.transpose(vv)

    q, k, v, k_t, v_t = _rows(qkv_fwd, name=name + "_qkv", n=nt, ins=[sv['p_s'], *tabs, *qkv_consts],
                              in_specs=[_rt(tm, SMALL_W)] + [_rt(tm, LANE)] * 3 + qkv_cspecs,
                              outs=[_sds((l, MLA_PAD), BF16)] * 3 + [_sds((MLA_PAD, l), BF16)] * 2,
                              out_specs=[_rt(tm, MLA_PAD)] * 3 + [pl.BlockSpec((MLA_PAD, tm), lambda i: (0, i))] * 2)
    sv['q'], sv['k'], sv['v'], sv['k_t'] = q, k, v, k_t
    sv['o_a'], sv['lse_t'] = _flash_fwd(name + "_attn", q, k, v_t)

    a_re, a_im, bb_re, bb_im = _ssm_params(name + "_ssm_par", p['lr'], p['li'], p['log_dt'], p['br'], p['bi'])
    sv['a_re'], sv['a_im'] = a_re.reshape(1, SSM_LANES), a_im.reshape(1, SSM_LANES)
    sv['b_mat'] = _b_mat(bb_re, bb_im)
    u_p = _to_perm(sv['p_u'], l)
    sv['u_p'] = u_p

    def bu_fn(i, u, bm):
        res = [_mm(u[:, j * LANE:(j + 1) * LANE], bm[j]) for j in range(SSM_JB)]
        return (jnp.concatenate([r[:, :512] for r in res], axis=-1), jnp.concatenate([r[:, 512:] for r in res], axis=-1))

    ts = min(256, l)
    bu_re, bu_im = _rows(bu_fn, name=name + "_ssm_bu", n=l // ts, ins=[u_p, sv['b_mat']],
                         in_specs=[_rt(ts, SSM_WIDTH), _full(sv['b_mat'].shape)],
                         outs=[_sds((l, SSM_LANES))] * 2, out_specs=[_rt(ts, SSM_LANES)] * 2)
    s_re, s_im = _scan(name + "_ssm_scan", bu_re, bu_im, sv['a_re'], sv['a_im'], reverse=False)
    sv['s_re'], sv['s_im'] = s_re, s_im

    def glu_fn(i, sr, si, u, cm, dsk, wg, bg):
        y = jnp.concatenate([_mm(jnp.concatenate([sr[:, j * 512:(j + 1) * 512], si[:, j * 512:(j + 1) * 512]], axis=-1),
                                 cm[j]) for j in range(SSM_JB)], axis=-1) + dsk * u
        zz = _gelu(y)
        return zz * jax.nn.sigmoid(_mm(zz, wg) + bg)

    glu_consts = [p['c_mat'], p['ssm_d'], p['w_glu'], p['b_glu']]
    zo_p = _rows(glu_fn, name=name + "_ssm_glu", n=l // ts, ins=[s_re, s_im, u_p, *glu_consts],
                 in_specs=[_rt(ts, SSM_LANES), _rt(ts, SSM_LANES), _rt(ts, SSM_WIDTH)] + [_full(a.shape) for a in glu_consts],
                 outs=[_sds((l, SSM_WIDTH), BF16)], out_specs=[_rt(ts, SSM_WIDTH)])[0]
    sv['zo'] = _from_perm(zo_p, l)

    m_len = mem.shape[0]

    def memkv_fn(i, mm_, mg, wmk, xkg):
        kv = _mm(_rms(mm_, mg, D_MODEL), wmk)
        return _head_rms(kv[:, :X_WIDTH], xkg, X_HEADS, X_HEAD_DIM), kv[:, X_WIDTH:]

    mem_consts = [p['mem_norm_g'], p['w_mem_kv'], p['xk_norm_g']]
    k_c, v_c = _rows(memkv_fn, name=name + "_memkv", n=1, ins=[mem, *mem_consts],
                     in_specs=[_full(mem.shape)] + [_full(a.shape) for a in mem_consts],
                     outs=[_sds((m_len, X_WIDTH))] * 2, out_specs=[_full((m_len, X_WIDTH))] * 2)
    sv['k_c'], sv['v_c'] = k_c, v_c

    def cross_fn(i, xq, kc, vc, xqg):
        outs = []
        for h in range(X_HEADS):
            sl = slice(h * LANE, (h + 1) * LANE)
            qh = _rms(xq[:, sl], xqg, X_HEAD_DIM)
            s = _mm_nt(qh, kc[:, sl]) * (X_HEAD_DIM ** -0.5)
            s = s - jnp.max(s, axis=-1, keepdims=True)
            e = jnp.exp(s)
            pr = e / jnp.sum(e, axis=-1, keepdims=True)
            outs.append(_mm(pr, vc[:, sl]))
        return jnp.concatenate(outs, axis=-1)

    sv['o_c'] = _rows(cross_fn, name=name + "_cross", n=nt, ins=[sv['p_xq'], k_c, v_c, p['xq_norm_g']],
                      in_specs=[_rt(tm, X_WIDTH), _full(k_c.shape), _full(v_c.shape), _full((1, LANE))],
                      outs=[_sds((l, X_WIDTH), BF16)], out_specs=[_rt(tm, X_WIDTH)])[0]

    def merge_fn(i, oa, zo, oc, pg, x0, woa, wos, woc, bg, wout):
        gates = jax.nn.sigmoid(pg + bg)
        merged = (gates[:, :D_MODEL] * _mm(oa, woa) + gates[:, D_MODEL:2 * D_MODEL] * _mm(zo, wos)
                  + gates[:, 2 * D_MODEL:] * _mm(oc, woc))
        return x0 + _mm(merged, wout), merged

    merge_consts = [p['w_oa'], p['w_o_ssm'], p['w_o_cross'], p['b_gate'], p['w_out']]
    tg = min(256, l)
    x1, merged = _rows(merge_fn, name=name + "_merge", n=l // tg, ins=[sv['o_a'], sv['zo'], sv['o_c'], sv['p_g'], x, *merge_consts],
                       in_specs=[_rt(tg, MLA_PAD), _rt(tg, SSM_WIDTH), _rt(tg, X_WIDTH), _rt(tg, 3 * D_MODEL), _rt(tg, D_MODEL)]
                       + [_full(a.shape) for a in merge_consts],
                       outs=[_sds((l, D_MODEL)), _sds((l, D_MODEL), BF16)], out_specs=[_rt(tg, D_MODEL)] * 2)
    sv['x1'], sv['merged'] = x1, merged

    up = _matmul(name + "_up", [(x1, p['w_up'])], l, 2 * D_FF, rms_gain=p['norm_ffn_g'])
    sv['up'] = up
    tc = min(128, l)

    def conv_fn(i, upt, halo, cw, cb):
        upc = _conv(i, upt, halo, cw) + cb
        return _silu(upc[:, :D_FF]) * upc[:, D_FF:]

    act = _rows(conv_fn, name=name + "_conv", n=l // tc, ins=[up, up, p['conv_w'], p['conv_b']],
                in_specs=[_rt(tc, 2 * D_FF), _halo_prev(tc, 2 * D_FF), _full((3, 2 * D_FF)), _full((1, 2 * D_FF))],
                outs=[_sds((l, D_FF), BF16)], out_specs=[_rt(tc, D_FF)])[0]
    sv['act'] = act
    x2 = _matmul(name + "_down", [(act, p['w_down'])], l, D_MODEL, resid=x1)
    return x2, sv


def _halo_prev(tm, w):
    return pl.BlockSpec((8, w), lambda i: (jnp.maximum(i * (tm // 8) - 1, 0), 0))


def _halo_next(tm, w, n_tiles):
    last = n_tiles * (tm // 8) - 1
    return pl.BlockSpec((8, w), lambda i: (jnp.minimum((i + 1) * (tm // 8), last), 0))


def _conv(i, tile, halo, cw):
    halo = jnp.where(i > 0, halo, 0.0)
    ext = jnp.concatenate([halo, tile], axis=0)
    n = ext.shape[0]
    x1 = pltpu.roll(ext, 1, 0)[8:]
    x2 = pltpu.roll(ext, 2, 0)[8:]
    del n
    return cw[0:1] * x2 + cw[1:2] * x1 + cw[2:3] * tile


def _layer_bwd(name, dx2, sv, tabs, mem, p):
    l = dx2.shape[0]
    tm = min(512, l)
    nt = l // tm
    g = {}
    x1 = sv['x1']
    dact = _matmul(name + "_b_down", [(dx2, p['w_down'])], l, D_FF, nt=True)
    g['w_down'] = _matmul_tn(name + "_gw_down", sv['act'], dx2)
    tc = min(128, l)
    ntc = l // tc

    def conv_b(i, upt, up_prev, up_next, da, da_next, cw, cb):
        up_prev = jnp.where(i > 0, up_prev, 0.0)
        da_next = jnp.where(i < ntc - 1, da_next, 0.0)
        ext = jnp.concatenate([up_prev, upt, up_next], axis=0)
        x0 = ext[8:]
        xm1 = pltpu.roll(ext, 1, 0)[8:]
        xm2 = pltpu.roll(ext, 2, 0)[8:]
        upc = cw[0:1] * xm2 + cw[1:2] * xm1 + cw[2:3] * x0 + cb
        _, vjp = jax.vjp(lambda a, b: _silu(a) * b, upc[:, :D_FF], upc[:, D_FF:])
        dg, dv = vjp(jnp.concatenate([da, da_next], axis=0))
        dupc = jnp.concatenate([dg, dv], axis=-1)
        n = dupc.shape[0]
        dup = cw[2:3] * dupc[:tc] + cw[1:2] * pltpu.roll(dupc, n - 1, 0)[:tc] + cw[0:1] * pltpu.roll(dupc, n - 2, 0)[:tc]
        dt = dupc[:tc]
        dcw = _row_select([_colsum(dt * xm2[:tc]), _colsum(dt * xm1[:tc]), _colsum(dt * upt)], 8)
        return dup, dcw, _colsum(dt)

    dup, g_cw, g_cb = _rows(
        conv_b, name=name + "_b_conv", n=ntc, ins=[sv['up'], sv['up'], sv['up'], dact, dact, p['conv_w'], p['conv_b']],
        in_specs=[_rt(tc, 2 * D_FF), _halo_prev(tc, 2 * D_FF), _halo_next(tc, 2 * D_FF, ntc), _rt(tc, D_FF),
                  _halo_next(tc, D_FF, ntc), _full((3, 2 * D_FF)), _full((1, 2 * D_FF))],
        outs=[_sds((l, 2 * D_FF)), _sds((8, 2 * D_FF)), _sds((1, 2 * D_FF))],
        out_specs=[_rt(tc, 2 * D_FF), _full((8, 2 * D_FF)), _full((1, 2 * D_FF))], n_acc=2, vmem=56)
    g['conv_w'] = g_cw[:3]
    g['conv_b'] = g_cb
    dh2 = _matmul(name + "_b_up", [(dup, p['w_up'])], l, D_MODEL, nt=True, tm=256)
    g['w_up'] = _matmul_tn(name + "_gw_up", x1, dup, rms_gain=p['norm_ffn_g'])

    def norm_b(i, xv, dh, dres, gn):
        _, vjp = jax.vjp(lambda a, b: _rms(a, b, D_MODEL), xv, gn)
        dxv, dgn = vjp(dh)
        return dres + dxv, dgn

    dx1, g['norm_ffn_g'] = _rows(norm_b, name=name + "_b_norm2", n=nt, ins=[x1, dh2, dx2, p['norm_ffn_g']],
                                 in_specs=[_rt(tm, D_MODEL)] * 3 + [_full((1, D_MODEL))],
                                 outs=[_sds((l, D_MODEL)), _sds((1, D_MODEL))], out_specs=[_rt(tm, D_MODEL), _full((1, D_MODEL))],
                                 n_acc=1)

    tg = min(256, l)

    def merge_b(i, dx, oa, zo, oc, pg, woa, wos, woc, bg, wout):
        dm = _mm_nt(dx, wout)
        gates = jax.nn.sigmoid(pg + bg)
        ys = [_mm(oa, woa), _mm(zo, wos), _mm(oc, woc)]
        dys, dpg = [], []
        for b in range(3):
            gb = gates[:, b * D_MODEL:(b + 1) * D_MODEL]
            dys.append(dm * gb)
            dpg.append(dm * ys[b] * gb * (1.0 - gb))
        dpg = jnp.concatenate(dpg, axis=-1)
        return (_mm_nt(dys[0], woa), _mm_nt(dys[1], wos), _mm_nt(dys[2], woc), dpg, dys[0], dys[1], dys[2], _colsum(dpg))

    merge_consts = [p['w_oa'], p['w_o_ssm'], p['w_o_cross'], p['b_gate'], p['w_out']]
    (do_a, dzo, do_c, dp_g, dy_a, dy_b, dy_c, g['b_gate']) = _rows(
        merge_b, name=name + "_b_merge", n=l // tg, ins=[dx1, sv['o_a'], sv['zo'], sv['o_c'], sv['p_g'], *merge_consts],
        in_specs=[_rt(tg, D_MODEL), _rt(tg, MLA_PAD), _rt(tg, SSM_WIDTH), _rt(tg, X_WIDTH), _rt(tg, 3 * D_MODEL)]
        + [_full(a.shape) for a in merge_consts],
        outs=[_sds((l, MLA_PAD)), _sds((l, SSM_WIDTH)), _sds((l, X_WIDTH)), _sds((l, 3 * D_MODEL)),
              _sds((l, D_MODEL), BF16), _sds((l, D_MODEL), BF16), _sds((l, D_MODEL), BF16), _sds((1, 3 * D_MODEL))],
        out_specs=[_rt(tg, MLA_PAD), _rt(tg, SSM_WIDTH), _rt(tg, X_WIDTH), _rt(tg, 3 * D_MODEL),
                   _rt(tg, D_MODEL), _rt(tg, D_MODEL), _rt(tg, D_MODEL), _full((1, 3 * D_MODEL))], n_acc=1, vmem=56)
    g['w_out'] = _matmul_tn(name + "_gw_out", sv['merged'], dx1)
    g['w_oa'] = _matmul_tn(name + "_gw_oa", sv['o_a'], dy_a)
    g['w_o_ssm'] = _matmul_tn(name + "_gw_os", sv['zo'], dy_b)
    g['w_o_cross'] = _matmul_tn(name + "_gw_oc", sv['o_c'], dy_c)

    k_c, v_c = sv['k_c'], sv['v_c']
    m_len = k_c.shape[0]

    def cross_b(i, xq, do, kc, vc, xqg):
        dxq, dk, dv = [], [], []
        dg = jnp.zeros((1, LANE), F32)
        for h in range(X_HEADS):
            sl = slice(h * LANE, (h + 1) * LANE)
            qh, vjp = jax.vjp(lambda a, b: _rms(a, b, X_HEAD_DIM), xq[:, sl], xqg)
            sc = X_HEAD_DIM ** -0.5
            s = _mm_nt(qh, kc[:, sl]) * sc
            s = s - jnp.max(s, axis=-1, keepdims=True)
            e = jnp.exp(s)
            pr = e / jnp.sum(e, axis=-1, keepdims=True)
            doh = do[:, sl]
            dv.append(_mm_tn(pr, doh))
            dp = _mm_nt(doh, vc[:, sl])
            ds = pr * (dp - jnp.sum(dp * pr, axis=-1, keepdims=True)) * sc
            dk.append(_mm_tn(ds, qh))
            dxh, dgh = vjp(_mm(ds, kc[:, sl]))
            dxq.append(dxh)
            dg = dg + dgh
        return jnp.concatenate(dxq, axis=-1), jnp.concatenate(dk, axis=-1), jnp.concatenate(dv, axis=-1), dg

    dp_xq, dk_c, dv_c, g['xq_norm_g'] = _rows(
        cross_b, name=name + "_b_cross", n=nt, ins=[sv['p_xq'], do_c, k_c, v_c, p['xq_norm_g']],
        in_specs=[_rt(tm, X_WIDTH), _rt(tm, X_WIDTH), _full(k_c.shape), _full(v_c.shape), _full((1, LANE))],
        outs=[_sds((l, X_WIDTH)), _sds((m_len, X_WIDTH)), _sds((m_len, X_WIDTH)), _sds((1, LANE))],
        out_specs=[_rt(tm, X_WIDTH), _full((m_len, X_WIDTH)), _full((m_len, X_WIDTH)), _full((1, LANE))], n_acc=3)

    def memkv_b(i, mm_, dk, dv, mg, wmk, xkg):
        memn, vjp_n = jax.vjp(lambda a, b: _rms(a, b, D_MODEL), mm_, mg)
        kv = _mm(memn, wmk)
        _, vjp_k = jax.vjp(lambda a, b: _head_rms(a, b, X_HEADS, X_HEAD_DIM), kv[:, :X_WIDTH], xkg)
        dkr, dxkg = vjp_k(dk)
        dkv = jnp.concatenate([dkr, dv], axis=-1)
        _, dmg = vjp_n(_mm_nt(dkv, wmk))
        return _mm_tn(memn, dkv), dmg, dxkg

    mem_consts = [p['mem_norm_g'], p['w_mem_kv'], p['xk_norm_g']]
    g['w_mem_kv'], g['mem_norm_g'], g['xk_norm_g'] = _rows(
        memkv_b, name=name + "_b_memkv", n=1, ins=[mem, dk_c, dv_c, *mem_consts],
        in_specs=[_full(mem.shape), _full(dk_c.shape), _full(dv_c.shape)] + [_full(a.shape) for a in mem_consts],
        outs=[_sds((D_MODEL, 2 * X_WIDTH)), _sds((1, D_MODEL)), _sds((1, LANE))],
        out_specs=[_full((D_MODEL, 2 * X_WIDTH)), _full((1, D_MODEL)), _full((1, LANE))])

    u_p = sv['u_p']
    dzo_p = _to_perm(dzo, l)
    s_re, s_im = sv['s_re'], sv['s_im']

    def glu_b(i, sr, si, u, dz, cm, dsk, wg, bg):
        cats = [jnp.concatenate([sr[:, j * 512:(j + 1) * 512], si[:, j * 512:(j + 1) * 512]], axis=-1) for j in range(SSM_JB)]
        y = jnp.concatenate([_mm(cats[j], cm[j]) for j in range(SSM_JB)], axis=-1) + dsk * u
        zz, vjp_g = jax.vjp(_gelu, y)
        t = _mm(zz, wg) + bg
        sg = jax.nn.sigmoid(t)
        dt = dz * zz * sg * (1.0 - sg)
        dzz = dz * sg + _mm_nt(dt, wg)
        dy = vjp_g(dzz)[0]
        dss = [_mm_nt(dy[:, j * LANE:(j + 1) * LANE], cm[j]) for j in range(SSM_JB)]
        dsr = jnp.concatenate([d[:, :512] for d in dss], axis=-1)
        dsi = jnp.concatenate([d[:, 512:] for d in dss], axis=-1)
        dcm = jnp.stack([_mm_tn(cats[j], dy[:, j * LANE:(j + 1) * LANE]) for j in range(SSM_JB)], axis=0)
        return dsr, dsi, dy * dsk, dcm, _colsum(dy * u), _mm_tn(zz, dt), _colsum(dt)

    glu_consts = [p['c_mat'], p['ssm_d'], p['w_glu'], p['b_glu']]
    ts = min(256, l)
    nts = l // ts
    ds_re, ds_im, du_dir, g['c_mat'], g['ssm_d'], g['w_glu'], g['b_glu'] = _rows(
        glu_b, name=name + "_b_glu", n=nts, ins=[s_re, s_im, u_p, dzo_p, *glu_consts],
        in_specs=[_rt(ts, SSM_LANES), _rt(ts, SSM_LANES), _rt(ts, SSM_WIDTH), _rt(ts, SSM_WIDTH)] + [_full(a.shape) for a in glu_consts],
        outs=[_sds((l, SSM_LANES)), _sds((l, SSM_LANES)), _sds((l, SSM_WIDTH)), _sds((SSM_JB, 1024, LANE)), _sds((1, SSM_WIDTH)),
              _sds((SSM_WIDTH, SSM_WIDTH)), _sds((1, SSM_WIDTH))],
        out_specs=[_rt(ts, SSM_LANES), _rt(ts, SSM_LANES), _rt(ts, SSM_WIDTH), _full((SSM_JB, 1024, LANE)), _full((1, SSM_WIDTH)),
                   _full((SSM_WIDTH, SSM_WIDTH)), _full((1, SSM_WIDTH))], n_acc=4)
    gb_re, gb_im = _scan(name + "_b_scan", ds_re, ds_im, sv['a_re'], -sv['a_im'], reverse=True)
    ns = SCAN_SEGS
    last_blk = l // ns - 1

    def da_fn(i, gr, gi, sr, si, hr, hi, lr_, li_):
        rid = lax.broadcasted_iota(jnp.int32, lr_.shape, 0)
        fr = jnp.where(rid == 0, 0.0, pltpu.roll(lr_, 1, 0))
        fi = jnp.where(rid == 0, 0.0, pltpu.roll(li_, 1, 0))
        hr = jnp.where(i == 0, fr, hr)
        hi = jnp.where(i == 0, fi, hi)
        if ts > ns:
            pr = jnp.concatenate([hr, sr[:ts - ns]], axis=0)
            pi = jnp.concatenate([hi, si[:ts - ns]], axis=0)
        else:
            pr, pi = hr, hi
        return _colsum(gr * pr + gi * pi), _colsum(gi * pr - gr * pi)

    hprev = pl.BlockSpec((ns, SSM_LANES), lambda i: (jnp.maximum(i * (ts // ns) - 1, 0), 0))
    hlast = pl.BlockSpec((ns, SSM_LANES), lambda i: (last_blk, 0))
    da_re, da_im = _rows(da_fn, name=name + "_b_da", n=nts, ins=[gb_re, gb_im, s_re, s_im, s_re, s_im, s_re, s_im],
                         in_specs=[_rt(ts, SSM_LANES)] * 4 + [hprev, hprev, hlast, hlast],
                         outs=[_sds((1, SSM_LANES))] * 2, out_specs=[_full((1, SSM_LANES))] * 2, n_acc=2)

    def bu_b(i, dbr, dbi, u, dud, bm):
        dus, dbm = [], []
        for j in range(SSM_JB):
            cat = jnp.concatenate([dbr[:, j * 512:(j + 1) * 512], dbi[:, j * 512:(j + 1) * 512]], axis=-1)
            dus.append(_mm_nt(cat, bm[j]))
            dbm.append(_mm_tn(u[:, j * LANE:(j + 1) * LANE], cat))
        return dud + jnp.concatenate(dus, axis=-1), jnp.stack(dbm, axis=0)

    du_p, d_bmat = _rows(bu_b, name=name + "_b_bu", n=nts, ins=[gb_re, gb_im, u_p, du_dir, sv['b_mat']],
                         in_specs=[_rt(ts, SSM_LANES), _rt(ts, SSM_LANES), _rt(ts, SSM_WIDTH), _rt(ts, SSM_WIDTH),
                                   _full(sv['b_mat'].shape)],
                         outs=[_sds((l, SSM_WIDTH)), _sds((SSM_JB, LANE, 1024))],
                         out_specs=[_rt(ts, SSM_WIDTH), _full((SSM_JB, LANE, 1024))], n_acc=1)
    dp_u = _from_perm(du_p, l)
    dbb_re = _blockdiag_t(d_bmat[:, :, :512], SSM_GROUP_CH, SSM_STATE).reshape(SSM_GROUPS, SSM_GROUP_CH, SSM_STATE).transpose(1, 0, 2)
    dbb_im = _blockdiag_t(d_bmat[:, :, 512:], SSM_GROUP_CH, SSM_STATE).reshape(SSM_GROUPS, SSM_GROUP_CH, SSM_STATE).transpose(1, 0, 2)
    g['lr'], g['li'], g['log_dt'], g['br'], g['bi'] = _ssm_params_bwd(
        name + "_b_ssm_par", p['lr'], p['li'], p['log_dt'], p['br'], p['bi'],
        da_re.reshape(SSM_GROUPS, SSM_STATE), da_im.reshape(SSM_GROUPS, SSM_STATE), dbb_re, dbb_im)

    dq_t, dk, dv = _flash_bwd(name + "_b_attn", sv['q'], sv['k'], sv['v'], sv['k_t'], sv['o_a'], sv['lse_t'], do_a)

    def qkv_b(i, ps, c, s1, s2, dq_, dk_, dv_, qag, wqb, kvag, wk, wv, qng, kng):
        c_q = ps[:, :Q_LORA]
        c_kv = ps[:, Q_LORA:Q_LORA + KV_LORA]
        kr = ps[:, Q_LORA + KV_LORA:]
        cqn, vjp_cq = jax.vjp(lambda a, b: _rms(a, b, Q_LORA), c_q, qag)
        ckvn, vjp_ckv = jax.vjp(lambda a, b: _rms(a, b, KV_LORA), c_kv, kvag)
        q_raw = _mm(cqn, wqb)
        k_raw = _mm(ckvn, wk) + jnp.concatenate([kr] * MLA_HEADS, axis=-1)
        _, vjp_qn = jax.vjp(lambda a, b: _head_rms(a, b, MLA_HEADS, D_QK), q_raw, qng)
        _, vjp_kn = jax.vjp(lambda a, b: _head_rms(a, b, MLA_HEADS, D_QK), k_raw, kng)
        dq_raw, dqng = vjp_qn(_heads(_rope_t, jnp.transpose(dq_[0]), MLA_HEADS, c, s1, s2))
        dk_raw, dkng = vjp_kn(_heads(_rope_t, dk_, MLA_HEADS, c, s1, s2))
        dkr = dk_raw[:, :LANE]
        for h in range(1, MLA_HEADS):
            dkr = dkr + dk_raw[:, h * LANE:(h + 1) * LANE]
        dcq, dqag = vjp_cq(_mm_nt(dq_raw, wqb))
        dckv, dkvag = vjp_ckv(_mm_nt(dk_raw, wk) + _mm_nt(dv_, wv))
        dps = jnp.concatenate([dcq, dckv, dkr], axis=-1)
        return (dps, _mm_tn(cqn, dq_raw), _mm_tn(ckvn, dk_raw), _mm_tn(ckvn, dv_), dqag, dkvag, dqng, dkng)

    qkv_consts = [p['q_a_norm_g'], p['w_qb'], p['kv_a_norm_g'], p['w_k'], p['w_v'], p['q_norm_g'], p['k_norm_g']]
    (dp_s, g['w_qb'], g['w_k'], g['w_v'], g['q_a_norm_g'], g['kv_a_norm_g'], g['q_norm_g'], g['k_norm_g']) = _rows(
        qkv_b, name=name + "_b_qkv", n=nt, ins=[sv['p_s'], *tabs, dq_t, dk, dv, *qkv_consts],
        in_specs=[_rt(tm, SMALL_W)] + [_rt(tm, LANE)] * 3
        + [pl.BlockSpec((1, MLA_PAD, tm), lambda i: (i // (dq_t.shape[2] // tm), 0, i % (dq_t.shape[2] // tm)))]
        + [_rt(tm, MLA_PAD)] * 2 + [_full(a.shape) for a in qkv_consts],
        outs=[_sds((l, SMALL_W)), _sds((Q_LORA, MLA_PAD)), _sds((KV_LORA, MLA_PAD)), _sds((KV_LORA, MLA_PAD)),
              _sds((1, Q_LORA)), _sds((1, KV_LORA)), _sds((1, LANE)), _sds((1, LANE))],
        out_specs=[_rt(tm, SMALL_W), _full((Q_LORA, MLA_PAD)), _full((KV_LORA, MLA_PAD)), _full((KV_LORA, MLA_PAD)),
                   _full((1, Q_LORA)), _full((1, KV_LORA)), _full((1, LANE)), _full((1, LANE))], n_acc=7)

    x0 = sv['x0']
    dh = _matmul(name + "_b_in", [(dp_g, p['w_g']), (dp_u, p['w_u']), (dp_xq, p['w_xq']), (dp_s, p['w_s'])], l, D_MODEL, nt=True,
                 tm=256)
    gm = p['norm_mix_g']
    g['w_g'] = _matmul_tn(name + "_gw_g", x0, dp_g, rms_gain=gm)
    g['w_u'] = _matmul_tn(name + "_gw_u", x0, dp_u, rms_gain=gm)
    g['w_xq'] = _matmul_tn(name + "_gw_xq", x0, dp_xq, rms_gain=gm)
    g['w_s'] = _matmul_tn(name + "_gw_s", x0, dp_s, rms_gain=gm)
    dx0, g['norm_mix_g'] = _rows(norm_b, name=name + "_b_norm1", n=nt, ins=[x0, dh, dx1, gm],
                                 in_specs=[_rt(tm, D_MODEL)] * 3 + [_full((1, D_MODEL))],
                                 outs=[_sds((l, D_MODEL)), _sds((1, D_MODEL))], out_specs=[_rt(tm, D_MODEL), _full((1, D_MODEL))],
                                 n_acc=1)
    return dx0, g


def _unprep_grads(g):
    o = {}
    ws = g['w_s']
    o['w_in'] = jnp.concatenate([ws[:, :Q_LORA + KV_LORA], ws[:, Q_LORA + KV_LORA + D_NOPE:Q_LORA + KV_LORA + D_QK],
                                 g['w_u'], g['w_xq'], g['w_g']], axis=1)
    o['w_q_b'] = g['w_qb'].reshape(Q_LORA, MLA_HEADS, HEAD_PAD)[:, :, :D_QK].reshape(Q_LORA, MLA_HEADS * D_QK)
    gk = g['w_k'].reshape(KV_LORA, MLA_HEADS, HEAD_PAD)[:, :, :D_NOPE]
    gv = g['w_v'].reshape(KV_LORA, MLA_HEADS, HEAD_PAD)[:, :, :D_V]
    o['w_kv_b'] = jnp.concatenate([gk, gv], axis=2).reshape(KV_LORA, MLA_HEADS * (D_NOPE + D_V))
    o['w_o_mla'] = g['w_oa'].reshape(MLA_HEADS, HEAD_PAD, D_MODEL)[:, :D_V].reshape(MLA_HEADS * D_V, D_MODEL)
    for n in ('w_glu', 'w_o_ssm', 'w_mem_kv', 'w_o_cross', 'w_out', 'w_up', 'w_down', 'conv_w'):
        o[n] = g[n]
    for n in ('norm_mix_g', 'q_a_norm_g', 'kv_a_norm_g', 'b_glu', 'mem_norm_g', 'xq_norm_g', 'xk_norm_g', 'b_gate',
              'norm_ffn_g', 'conv_b'):
        o[n] = g[n].reshape(-1)
    o['q_norm_g'] = g['q_norm_g'].reshape(-1)[:D_QK]
    o['k_norm_g'] = g['k_norm_g'].reshape(-1)[:D_QK]
    o['ssm_d'] = g['ssm_d'].reshape(SSM_GROUPS, SSM_GROUP_CH)
    o['ssm_lambda_re'] = g['lr']
    o['ssm_lambda_im'] = g['li']
    o['ssm_log_dt'] = g['log_dt'].reshape(SSM_GROUPS)
    o['ssm_b_re'] = g['br'].transpose(1, 2, 0)
    o['ssm_b_im'] = g['bi'].transpose(1, 2, 0)
    dc = g['c_mat']
    o['ssm_c_re'] = _blockdiag_t(dc[:, :512], SSM_STATE, SSM_GROUP_CH).transpose(0, 1, 3, 2).reshape(SSM_GROUPS, SSM_GROUP_CH, SSM_STATE)
    o['ssm_c_im'] = -_blockdiag_t(dc[:, 512:], SSM_STATE, SSM_GROUP_CH).transpose(0, 1, 3, 2).reshape(SSM_GROUPS, SSM_GROUP_CH, SSM_STATE)
    return o


def _local_step(x, mem, pos, target, w):
    l = x.shape[0]
    tm = min(512, l)
    tabs = _rope_tables(pos.astype(F32).reshape(l, 1))
    ps = [_prep_layer(w, i) for i in range(DEPTH)]
    saved = []
    h = x
    for i in range(DEPTH):
        h, sv = _layer_fwd("l%d" % i, h, tabs, mem, ps[i])
        saved.append(sv)

    def loss_fn(i, y, t):
        e = y - t
        per_tok = jnp.sum(e * e, axis=-1, keepdims=True) * (1.0 / D_MODEL)
        tot = 0.5 * jnp.sum(per_tok, axis=0, keepdims=True)
        return e * (1.0 / D_MODEL), jnp.broadcast_to(tot, (1, LANE))

    dy, loss = _rows(loss_fn, name="loss", n=l // tm, ins=[h, target], in_specs=[_rt(tm, D_MODEL)] * 2,
                     outs=[_sds((l, D_MODEL)), _sds((1, LANE))], out_specs=[_rt(tm, D_MODEL), _full((1, LANE))], n_acc=1)
    grads = []
    d = dy
    for i in reversed(range(DEPTH)):
        d, g = _layer_bwd("l%d" % i, d, saved[i], tabs, mem, ps[i])
        grads.append(_unprep_grads(g))
    return loss[0, 0], d, grads[::-1]


def _sum_picked(name, slots, pick, extra, out_dtype):
    _, r, c = slots.shape
    e = extra.shape[0]
    tr = _row_tile(r)

    def body(pk, s_ref, x_ref, o_ref):
        acc = s_ref[...].astype(F32)
        for k in range(e):
            acc = acc + x_ref[k].astype(F32)
        o_ref[...] = acc.astype(o_ref.dtype)

    grid_spec = pltpu.PrefetchScalarGridSpec(
        num_scalar_prefetch=1, grid=(r // tr,),
        in_specs=[pl.BlockSpec((None, tr, c), lambda i, pk: (pk[0], i, 0)), pl.BlockSpec((e, tr, c), lambda i, pk: (0, i, 0))],
        out_specs=pl.BlockSpec((tr, c), lambda i, pk: (i, 0)))
    return pl.pallas_call(body, name=name, grid_spec=grid_spec, out_shape=_sds((r, c), out_dtype),
                          compiler_params=_params(("arbitrary",), 48))(pick, slots, extra)


def _row_tile(r):
    for t in (256, 128, 64, 32, 16, 8):
        if r % t == 0:
            return t
    return r


def _adamw(name, parts, w, m, v):
    r, cw = w.shape
    tr = _row_tile(r)
    np_ = len(parts)

    def fn(i, *vals):
        wv, mv, vv = vals[np_:]
        terms = []
        for pv in vals[:np_]:
            terms += [pv] if pv.ndim == 2 else [pv[k] for k in range(pv.shape[0])]
        g = terms[0]
        for t in terms[1:]:
            g = g + t
        mn = ADAM_B1 * mv + (1.0 - ADAM_B1) * g
        vn = ADAM_B2 * vv + (1.0 - ADAM_B2) * (g * g)
        m_hat = mn / (1.0 - ADAM_B1 ** ADAM_STEP)
        v_hat = vn / (1.0 - ADAM_B2 ** ADAM_STEP)
        delta = -ADAM_LR * (m_hat / (jnp.sqrt(v_hat) + ADAM_EPS) + ADAM_WD * wv)
        return g, delta, mn, vn

    pspecs = [_rt(tr, cw) if p.ndim == 2 else pl.BlockSpec((p.shape[0], tr, cw), lambda i: (0, i, 0)) for p in parts]
    return _rows(fn, name=name, n=r // tr, ins=[*parts, w, m, v], in_specs=pspecs + [_rt(tr, cw)] * 3,
                 outs=[_sds((r, cw))] * 4, out_specs=[_rt(tr, cw)] * 4)


def _shard_of(a, axis, k):
    n = a.shape[axis] // 4
    return lax.slice_in_dim(a, k * n, (k + 1) * n, axis=axis)


def _step(a):
    x = a['x'][0]
    mem = a['mem'][0]
    pos = a['positions'][0]
    target = a['loss_target'][0]

    me = 2 * lax.axis_index("x") + lax.axis_index("y")

    mine = [a[n] if n == 'conv_w' else a[n].astype(BF16) for n in SHARDED]
    got = _gather_d2d(_gather_ici(mine))
    w = {}
    for n, own, y in zip(SHARDED, mine, got):
        ax = SHARD_AXIS[n] - 1
        w[n] = [jnp.concatenate([jnp.where(me == k, own[i], y[k, i]) for k in range(4)], axis=ax) for i in range(DEPTH)]
    for n in SMALL:
        w[n] = a[n]

    loss, grad_x, grads = _local_step(x, mem, pos, target, w)

    mc = lax.axis_index("c")
    mc1 = mc.astype(jnp.int32).reshape(1)
    me1 = me.astype(jnp.int32).reshape(1)
    gsh = []
    for n in SHARDED:
        ax = SHARD_AXIS[n] - 1
        gsh.append(jnp.stack([jnp.stack([_shard_of(grads[i][n], ax, k) for k in range(4)], axis=0)
                              for i in range(DEPTH)], axis=0).astype(BF16))
    sib = _swap_d2d("comm_reduce_pair", gsh, other_layer=True)
    pair = []
    for n, g, s in zip(SHARDED, gsh, sib):
        rows, cols = g.shape[-2:]
        pair.append(_sum_picked("sum2_" + n, g.reshape(DEPTH, 4 * rows, cols), mc1, s.reshape(1, 4 * rows, cols), BF16)
                    .reshape(4, rows, cols))
    got = _reduce_ici(pair)
    parts = [_sum_picked("sum4_" + n, p4, me1, g3, F32) for n, p4, g3 in zip(SHARDED, pair, got)]
    others = _swap_d2d("comm_reduce_d2d", parts, other_layer=False)
    res_sh = []
    for n, part, other in zip(SHARDED, parts, others):
        cols = part.shape[-1]
        full = jnp.where(mc == 0, jnp.stack([part, other], axis=0), jnp.stack([other, part], axis=0))
        res = _adamw("adamw_" + n, [full.reshape(-1, cols)], *[a[pre + n].reshape(-1, cols) for pre in ('', 'm_', 'v_')])
        res_sh.append([r.reshape(a[n].shape) for r in res])
    res_sh = [[res_sh[j][kind] for j in range(len(SHARDED))] for kind in range(4)]

    sm_shapes = [a[n].shape for n in SMALL] + [(1,)]
    gsm = _pack([jnp.stack([grads[i][n] for i in range(DEPTH)], axis=0) for n in SMALL] + [loss.reshape(1)], 8, F32)
    alls = _all_exchange("comm_reduce_small", gsm)
    zero1 = jnp.zeros((1,), F32)
    res_sm = _adamw("adamw_small", [alls], *[_pack([a[pre + n] for n in SMALL] + [zero1], 8, F32) for pre in ('', 'm_', 'v_')])
    res_sm = [_unpack(r, sm_shapes) for r in res_sm]
    loss = res_sm[0][-1][0]

    outs = [loss, grad_x[None]]
    for kind in range(4):
        byname = dict(zip(SHARDED, res_sh[kind]))
        byname.update(zip(SMALL, res_sm[kind]))
        outs += [byname[n] for n in WEIGHTS]
    return tuple(outs)


def kernel(x, mem, positions, norm_mix_g, w_in, q_a_norm_g, w_q_b, kv_a_norm_g, w_kv_b, q_norm_g, k_norm_g, w_o_mla, ssm_lambda_re, ssm_lambda_im, ssm_log_dt, ssm_b_re, ssm_b_im, ssm_c_re, ssm_c_im, ssm_d, w_glu, b_glu, w_o_ssm, mem_norm_g, w_mem_kv, xq_norm_g, xk_norm_g, w_o_cross, b_gate, w_out, norm_ffn_g, w_up, conv_w, conv_b, w_down, loss_target, m_norm_mix_g, m_w_in, m_q_a_norm_g, m_w_q_b, m_kv_a_norm_g, m_w_kv_b, m_q_norm_g, m_k_norm_g, m_w_o_mla, m_ssm_lambda_re, m_ssm_lambda_im, m_ssm_log_dt, m_ssm_b_re, m_ssm_b_im, m_ssm_c_re, m_ssm_c_im, m_ssm_d, m_w_glu, m_b_glu, m_w_o_ssm, m_mem_norm_g, m_w_mem_kv, m_xq_norm_g, m_xk_norm_g, m_w_o_cross, m_b_gate, m_w_out, m_norm_ffn_g, m_w_up, m_conv_w, m_conv_b, m_w_down, v_norm_mix_g, v_w_in, v_q_a_norm_g, v_w_q_b, v_kv_a_norm_g, v_w_kv_b, v_q_norm_g, v_k_norm_g, v_w_o_mla, v_ssm_lambda_re, v_ssm_lambda_im, v_ssm_log_dt, v_ssm_b_re, v_ssm_b_im, v_ssm_c_re, v_ssm_c_im, v_ssm_d, v_w_glu, v_b_glu, v_w_o_ssm, v_mem_norm_g, v_w_mem_kv, v_xq_norm_g, v_xk_norm_g, v_w_o_cross, v_b_gate, v_w_out, v_norm_ffn_g, v_w_up, v_conv_w, v_conv_b, v_w_down):
    return _step(dict(locals()))
```

```python
import functools
import math

import numpy as np
import jax
import jax.numpy as jnp
from jax import lax
from jax.experimental import pallas as pl
from jax.experimental.pallas import tpu as pltpu

F32 = jnp.float32
BF16 = jnp.bfloat16
MESH = pl.DeviceIdType.MESH

DEPTH = 2
D_MODEL = 1024
EPS = 1e-6
MLA_HEADS = 8
Q_LORA = 384
KV_LORA = 256
D_NOPE = 64
D_ROPE = 32
D_QK = D_NOPE + D_ROPE
D_V = 64
HEAD_PAD = 128
MLA_PAD = MLA_HEADS * HEAD_PAD
ROPE_THETA = 10000.0
SSM_GROUPS = 32
SSM_GROUP_CH = 16
SSM_WIDTH = 512
SSM_STATE = 64
SSM_LANES = SSM_GROUPS * SSM_STATE
SSM_JB = 4
X_HEADS = 4
X_HEAD_DIM = 128
X_WIDTH = 512
D_FF = 2816
SMALL_W = Q_LORA + KV_LORA + HEAD_PAD
SCAN_SEGS = 32
LANE = 128
NEG = -1e30

ADAM_LR = 0.001
ADAM_B1 = 0.9
ADAM_B2 = 0.999
ADAM_EPS = 1e-08
ADAM_WD = 0.01
ADAM_STEP = 10

WEIGHTS = ['norm_mix_g', 'w_in', 'q_a_norm_g', 'w_q_b', 'kv_a_norm_g', 'w_kv_b', 'q_norm_g', 'k_norm_g', 'w_o_mla',
           'ssm_lambda_re', 'ssm_lambda_im', 'ssm_log_dt', 'ssm_b_re', 'ssm_b_im', 'ssm_c_re', 'ssm_c_im', 'ssm_d',
           'w_glu', 'b_glu', 'w_o_ssm', 'mem_norm_g', 'w_mem_kv', 'xq_norm_g', 'xk_norm_g', 'w_o_cross', 'b_gate',
           'w_out', 'norm_ffn_g', 'w_up', 'conv_w', 'conv_b', 'w_down']
SHARD_AXIS = {'w_in': 2, 'w_q_b': 2, 'w_kv_b': 2, 'w_o_mla': 2, 'w_glu': 1, 'w_o_ssm': 2, 'w_mem_kv': 1,
              'w_o_cross': 2, 'w_out': 1, 'w_up': 2, 'conv_w': 2, 'w_down': 1}
SHARDED = [n for n in WEIGHTS if n in SHARD_AXIS]
GATHER_BF16 = [n for n in SHARDED if n != 'conv_w']
SMALL = [n for n in WEIGHTS if n not in SHARD_AXIS]


def _bf(v):
    return v.astype(BF16)


def _mm(a, b):
    return jnp.dot(_bf(a), _bf(b), preferred_element_type=F32)


def _mm_nt(a, b):
    return lax.dot_general(_bf(a), _bf(b), (((1,), (1,)), ((), ())), preferred_element_type=F32)


def _mm_tn(a, b):
    return lax.dot_general(_bf(a), _bf(b), (((0,), (0,)), ((), ())), preferred_element_type=F32)


def _rms(v, g, n):
    ms = jnp.sum(v * v, axis=-1, keepdims=True) * (1.0 / n)
    return (v * lax.rsqrt(ms + EPS)) * g


def _head_rms(v, g, heads, n):
    return jnp.concatenate([_rms(v[:, h * LANE:(h + 1) * LANE], g, n) for h in range(heads)], axis=-1)


def _rope(v, c, s1, s2):
    return v * c + pltpu.roll(v, LANE - 16, 1) * s1 + pltpu.roll(v, 16, 1) * s2


def _rope_t(g, c, s1, s2):
    return g * c + pltpu.roll(g * s1, 16, 1) + pltpu.roll(g * s2, LANE - 16, 1)


def _heads(fn, v, heads, *tabs):
    return jnp.concatenate([fn(v[:, h * LANE:(h + 1) * LANE], *tabs) for h in range(heads)], axis=-1)


def _gelu(y):
    return y * (0.5 * (1.0 + jnp.tanh(math.sqrt(2.0 / math.pi) * (y + 0.044715 * (y * y * y)))))


def _silu(g):
    return g * jax.nn.sigmoid(g)


def _colsum(v):
    return jnp.sum(v, axis=0, keepdims=True)


def _row_select(rows, n):
    rid = lax.broadcasted_iota(jnp.int32, (n, rows[0].shape[-1]), 0)
    out = jnp.zeros((n, rows[0].shape[-1]), F32)
    for k, r in enumerate(rows):
        out = jnp.where(rid == k, jnp.broadcast_to(r, out.shape), out)
    return out


def _params(sem, vmem_mb):
    return pltpu.CompilerParams(dimension_semantics=sem, vmem_limit_bytes=vmem_mb * 1024 * 1024)


def _rt(tm, w, cb=0):
    return pl.BlockSpec((tm, w), lambda i: (i, cb))


def _full(shape):
    nd = len(shape)
    return pl.BlockSpec(tuple(shape), lambda i: (0,) * nd)


def _rows(fn, *, name, n, ins, in_specs, outs, out_specs, n_acc=0, vmem=48):
    n_in = len(ins)
    n_out = len(outs)

    def body(*refs):
        i = pl.program_id(0)
        res = fn(i, *[r[...] for r in refs[:n_in]])
        if not isinstance(res, (tuple, list)):
            res = (res,)
        assert len(res) == n_out, (name, len(res), n_out)
        for k, (r, v) in enumerate(zip(refs[n_in:], res)):
            if k < n_out - n_acc:
                r[...] = v.astype(r.dtype)
            else:
                @pl.when(i == 0)
                def _():
                    r[...] = v

                @pl.when(i > 0)
                def _():
                    r[...] += v

    return pl.pallas_call(
        body, name=name, grid=(n,), in_specs=list(in_specs), out_specs=tuple(out_specs), out_shape=tuple(outs),
        compiler_params=_params(("arbitrary",), vmem))(*ins)


def _sds(shape, dtype=F32):
    return jax.ShapeDtypeStruct(tuple(shape), dtype)


def _tile_n(n, cap=1536):
    best = None
    for t in range(LANE, min(n, cap) + 1, LANE):
        if n % t == 0:
            best = t
    if best is None or n <= 1408:
        return n
    return best


def _matmul(name, pairs, m, n, *, nt=False, rms_gain=None, resid=None, out_dtype=F32, tm=1024, vmem=56):
    tm = min(tm, m)
    tn = _tile_n(n)
    ks = [a.shape[1] for a, _ in pairs]
    np_ = len(pairs)

    def body(*refs):
        a_refs = refs[:np_]
        b_refs = refs[np_:2 * np_]
        k = 2 * np_
        g_ref = None
        r_ref = None
        if rms_gain is not None:
            g_ref = refs[k]
            k += 1
        if resid is not None:
            r_ref = refs[k]
            k += 1
        o_ref = refs[k]
        scr = refs[k + 1:]
        j = pl.program_id(1)

        @pl.when(j == 0)
        def _():
            for p in range(np_):
                a = a_refs[p][...]
                if p == 0 and g_ref is not None:
                    a = _rms(a.astype(F32), g_ref[...], ks[0])
                scr[p][...] = a.astype(BF16)

        acc = None
        for p in range(np_):
            b = b_refs[p][...].astype(BF16)
            if nt:
                t = lax.dot_general(scr[p][...], b, (((1,), (1,)), ((), ())), preferred_element_type=F32)
            else:
                t = jnp.dot(scr[p][...], b, preferred_element_type=F32)
            acc = t if acc is None else acc + t
        if r_ref is not None:
            acc = acc + r_ref[...]
        o_ref[...] = acc.astype(o_ref.dtype)

    in_specs = [pl.BlockSpec((tm, kk), lambda i, j: (i, 0)) for kk in ks]
    if nt:
        in_specs += [pl.BlockSpec((tn, kk), lambda i, j: (j, 0)) for kk in ks]
    else:
        in_specs += [pl.BlockSpec((kk, tn), lambda i, j: (0, j)) for kk in ks]
    ins = [a for a, _ in pairs] + [b for _, b in pairs]
    if rms_gain is not None:
        in_specs.append(pl.BlockSpec((1, ks[0]), lambda i, j: (0, 0)))
        ins.append(rms_gain)
    if resid is not None:
        in_specs.append(pl.BlockSpec((tm, tn), lambda i, j: (i, j)))
        ins.append(resid)
    return pl.pallas_call(
        body, name=name, grid=(m // tm, n // tn), in_specs=in_specs,
        out_specs=pl.BlockSpec((tm, tn), lambda i, j: (i, j)), out_shape=_sds((m, n), out_dtype),
        scratch_shapes=[pltpu.VMEM((tm, kk), BF16) for kk in ks],
        compiler_params=_params(("arbitrary", "arbitrary"), vmem))(*ins)


def _matmul_tn(name, a, b, *, rms_gain=None, tl=1024, vmem=56):
    l, ka = a.shape
    n = b.shape[1]
    tl = min(tl, l)
    tn = _tile_n(n, 1536)

    def body(*refs):
        if rms_gain is not None:
            a_ref, b_ref, g_ref, o_ref = refs
        else:
            a_ref, b_ref, o_ref = refs
        t = pl.program_id(1)
        av = a_ref[...]
        if rms_gain is not None:
            av = _rms(av.astype(F32), g_ref[...], ka)
        v = _mm_tn(av, b_ref[...])

        @pl.when(t == 0)
        def _():
            o_ref[...] = v

        @pl.when(t > 0)
        def _():
            o_ref[...] += v

    in_specs = [pl.BlockSpec((tl, ka), lambda j, t: (t, 0)), pl.BlockSpec((tl, tn), lambda j, t: (t, j))]
    ins = [a, b]
    if rms_gain is not None:
        in_specs.append(pl.BlockSpec((1, ka), lambda j, t: (0, 0)))
        ins.append(rms_gain)
    return pl.pallas_call(
        body, name=name, grid=(n // tn, l // tl), in_specs=in_specs,
        out_specs=pl.BlockSpec((ka, tn), lambda j, t: (0, j)), out_shape=_sds((ka, n)),
        compiler_params=_params(("arbitrary", "arbitrary"), vmem))(*ins)


ATT_HEADS_PER_STEP = 2
ATT_W = ATT_HEADS_PER_STEP * LANE
ATT_GROUPS = MLA_HEADS // ATT_HEADS_PER_STEP
LOG2E = math.log2(math.e)
ATT_FWD_TILE = 1024
ATT_BWD_TILE = 1024
ATT_BWD_HEADS = 1
ATT_SCALE = D_QK ** -0.5
ATT_QSCALE = ATT_SCALE * LOG2E


def _tri_tables(nq, by_k):
    qs, ks = [], []
    if by_k:
        for ki in range(nq):
            for qi in range(ki, nq):
                qs.append(qi)
                ks.append(ki)
    else:
        for qi in range(nq):
            for ki in range(qi + 1):
                qs.append(qi)
                ks.append(ki)
    return jnp.asarray(np.array(qs, np.int32)), jnp.asarray(np.array(ks, np.int32))


def _causal_keep(shape, transposed):
    r = lax.broadcasted_iota(jnp.int32, shape, 0)
    c = lax.broadcasted_iota(jnp.int32, shape, 1)
    return (r <= c) if transposed else (c <= r)


def _nt16(a, b):
    return lax.dot_general(a, b, (((1,), (1,)), ((), ())), preferred_element_type=F32)


def _row_form(col):
    return jnp.transpose(jnp.broadcast_to(col, (col.shape[0], LANE)))[:8]


def _att_call(body, name, l, tq, tabs, ins, in_specs, outs, out_specs, scratch=(), groups=ATT_GROUPS, vmem=48):
    grid_spec = pltpu.PrefetchScalarGridSpec(
        num_scalar_prefetch=2, grid=(groups, tabs[0].shape[0]), in_specs=in_specs, out_specs=out_specs,
        scratch_shapes=list(scratch))
    return pl.pallas_call(body, name=name, grid_spec=grid_spec, out_shape=outs,
                          compiler_params=_params(("arbitrary", "arbitrary"), vmem))(*tabs, *ins)


def _flash_fwd(name, q, k, v_t):
    l = q.shape[0]
    tq = min(ATT_FWD_TILE, l)
    nq = l // tq
    tabs = _tri_tables(nq, by_k=False)

    def body(qt, kt, q_ref, k_ref, vt_ref, o_ref, lset_ref, m_s, acc_s):
        t = pl.program_id(1)
        qi = qt[t]
        ki = kt[t]
        sls = [slice(h * LANE, (h + 1) * LANE) for h in range(ATT_HEADS_PER_STEP)]

        @pl.when(ki == 0)
        def _():
            m_s[...] = jnp.full(m_s.shape, NEG, F32)
            acc_s[...] = jnp.zeros(acc_s.shape, F32)

        def step(masked):
            sts = [_nt16(k_ref[:, sl], q_ref[:, sl]) for sl in sls]
            for h, sl in enumerate(sls):
                st = sts[h]
                if masked:
                    st = jnp.where(_causal_keep(st.shape, True), st, NEG)
                m_old = m_s[h][:1]
                m_new = jnp.maximum(m_old, jnp.max(st, axis=0, keepdims=True))
                alpha = jnp.exp2(m_old - m_new)
                pt = jnp.exp2(st - m_new).astype(BF16)
                acc_s[sl, :] = alpha * acc_s[sl, :] + jnp.dot(vt_ref[sl, :], pt, preferred_element_type=F32)
                m_s[h] = jnp.broadcast_to(m_new, (8, tq))

        @pl.when(ki < qi)
        def _():
            step(False)

        @pl.when(ki == qi)
        def _():
            step(True)
            row = lax.broadcasted_iota(jnp.int32, (LANE, tq), 0)
            for h, sl in enumerate(sls):
                acc = acc_s[sl, :]
                lsum = acc[D_V:D_V + 1, :]
                o_ref[:, sl] = jnp.transpose(jnp.where(row < D_V, acc / lsum, 0.0))
                lset_ref[h * 8:(h + 1) * 8, :] = m_s[h] + jnp.log2(lsum)

    qspec = pl.BlockSpec((tq, ATT_W), lambda g, t, qt, kt: (qt[t], g))
    kspec = pl.BlockSpec((tq, ATT_W), lambda g, t, qt, kt: (kt[t], g))
    vspec = pl.BlockSpec((ATT_W, tq), lambda g, t, qt, kt: (g, kt[t]))
    rspec = pl.BlockSpec((8 * ATT_HEADS_PER_STEP, tq), lambda g, t, qt, kt: (g, qt[t]))
    return _att_call(
        body, name, l, tq, tabs, [q, k, v_t], [qspec, kspec, vspec],
        (_sds((l, MLA_PAD)), _sds((8 * MLA_HEADS, l))), (qspec, rspec),
        scratch=[pltpu.VMEM((ATT_HEADS_PER_STEP, 8, tq), F32), pltpu.VMEM((ATT_W, tq), F32)])


def _flash_bwd(name, q, k, v, k_t, o, lse_t, do):
    l = q.shape[0]
    tq = min(ATT_BWD_TILE, l)
    nq = l // tq
    hb = ATT_BWD_HEADS
    wb = hb * LANE

    def delta_fn(i, dov, ov):
        rows = []
        for h in range(MLA_HEADS):
            sl = slice(h * LANE, (h + 1) * LANE)
            rows.append(_row_form(jnp.sum(dov[:, sl] * ov[:, sl], axis=-1, keepdims=True)))
        return jnp.concatenate(rows, axis=0), dov

    delta_t, do16 = _rows(
        delta_fn, name=name + "_delta", n=nq, ins=[do, o], in_specs=[_rt(tq, MLA_PAD)] * 2,
        outs=[_sds((8 * MLA_HEADS, l)), _sds((l, MLA_PAD), BF16)],
        out_specs=[pl.BlockSpec((8 * MLA_HEADS, tq), lambda i: (0, i)), _rt(tq, MLA_PAD)])

    def body(qt, kt, q_ref, k_ref, v_ref, do_ref, kt_ref, lset_ref, dlt_ref, dk_ref, dv_ref, dqt_ref):
        t = pl.program_id(1)
        qi = qt[t]
        ki = kt[t]
        sls = [slice(h * LANE, (h + 1) * LANE) for h in range(hb)]

        @pl.when(ki == 0)
        def _():
            dqt_ref[qi] = jnp.zeros((wb, tq), F32)

        def step(masked):
            sts = [_nt16(k_ref[:, sl], q_ref[:, sl]) for sl in sls]
            dpts = [_nt16(v_ref[:, sl], do_ref[:, sl]) for sl in sls]
            for h, sl in enumerate(sls):
                st = sts[h]
                if masked:
                    st = jnp.where(_causal_keep(st.shape, True), st, NEG)
                pt = jnp.exp2(st - lset_ref[h * 8:(h + 1) * 8, :][:1])
                dst = (pt * (dpts[h] - dlt_ref[h * 8:(h + 1) * 8, :][:1])).astype(BF16)
                dv_ref[:, sl] += jnp.dot(pt.astype(BF16), do_ref[:, sl], preferred_element_type=F32)
                dk_ref[:, sl] += jnp.dot(dst, q_ref[:, sl], preferred_element_type=F32)
                dqt_ref[qi, sl, :] += jnp.dot(kt_ref[sl, :], dst, preferred_element_type=F32)

        @pl.when(qi == ki)
        def _():
            dk_ref[...] = jnp.zeros(dk_ref.shape, F32)
            dv_ref[...] = jnp.zeros(dv_ref.shape, F32)
            step(True)
            dqt_ref[qi] = dqt_ref[qi] * ATT_SCALE

        @pl.when(qi > ki)
        def _():
            step(False)

        @pl.when(qi == nq - 1)
        def _():
            dk_ref[...] = dk_ref[...] * (1.0 / LOG2E)

    tabs_k = _tri_tables(nq, by_k=True)
    qspec = pl.BlockSpec((tq, wb), lambda g, t, qt, kt: (qt[t], g))
    kspec = pl.BlockSpec((tq, wb), lambda g, t, qt, kt: (kt[t], g))
    ktspec = pl.BlockSpec((wb, tq), lambda g, t, qt, kt: (g, kt[t]))
    rspec = pl.BlockSpec((8 * hb, tq), lambda g, t, qt, kt: (g, qt[t]))
    dqspec = pl.BlockSpec((nq, wb, tq), lambda g, t, qt, kt: (0, g, 0))
    dk, dv, dq_t = _att_call(body, name + "_dqkv", l, tq, tabs_k, [q, k, v, do16, k_t, lse_t, delta_t],
                             [qspec, kspec, kspec, qspec, ktspec, rspec, rspec],
                             (_sds((l, MLA_PAD)), _sds((l, MLA_PAD)), _sds((nq, MLA_PAD, tq))), (kspec, kspec, dqspec),
                             groups=MLA_HEADS // hb, vmem=56)
    return dq_t, dk, dv


def _cmul(ar, ai, br, bi):
    return ar * br - ai * bi, ar * bi + ai * br


def _scan(name, x_re, x_im, a_re, a_im, reverse):
    l, lanes = x_re.shape
    ns = SCAN_SEGS
    tl = l // ns
    steps = int(math.log2(tl))
    assert 2 ** steps == tl and tl * ns == l

    def body(xr_ref, xi_ref, ar_ref, ai_ref, sr_ref, si_ref):
        a_r1 = ar_ref[...]
        a_i1 = ai_ref[...]
        a_r = jnp.broadcast_to(a_r1, (ns, LANE))
        a_i = jnp.broadcast_to(a_i1, (ns, LANE))

        def rows(t):
            t = (tl - 1 - t) if reverse else t
            return pl.ds(pl.multiple_of(t * ns, ns), ns)

        def local(t, carry):
            cr, ci = carry
            r = rows(t)
            pr, pi = _cmul(a_r, a_i, cr, ci)
            return pr + xr_ref[r, :], pi + xi_ref[r, :]

        zero = jnp.zeros((ns, LANE), F32)
        e_r, e_i = lax.fori_loop(0, tl, local, (zero, zero), unroll=min(8, tl))
        p_r, p_i = a_r1, a_i1
        for _ in range(steps):
            p_r, p_i = _cmul(p_r, p_i, p_r, p_i)
        rid = lax.broadcasted_iota(jnp.int32, (ns, LANE), 0)
        c_r = jnp.zeros((1, LANE), F32)
        c_i = jnp.zeros((1, LANE), F32)
        in_r, in_i = zero, zero
        order = range(ns - 2, -1, -1) if reverse else range(1, ns)
        for kk in order:
            src = kk + 1 if reverse else kk - 1
            ek_r = jnp.sum(jnp.where(rid == src, e_r, 0.0), axis=0, keepdims=True)
            ek_i = jnp.sum(jnp.where(rid == src, e_i, 0.0), axis=0, keepdims=True)
            q_r, q_i = _cmul(p_r, p_i, c_r, c_i)
            c_r, c_i = q_r + ek_r, q_i + ek_i
            in_r = jnp.where(rid == kk, jnp.broadcast_to(c_r, (ns, LANE)), in_r)
            in_i = jnp.where(rid == kk, jnp.broadcast_to(c_i, (ns, LANE)), in_i)

        def final(t, carry):
            cr, ci = carry
            r = rows(t)
            pr, pi = _cmul(a_r, a_i, cr, ci)
            nr, ni = pr + xr_ref[r, :], pi + xi_ref[r, :]
            sr_ref[r, :] = nr
            si_ref[r, :] = ni
            return nr, ni

        lax.fori_loop(0, tl, final, (in_r, in_i), unroll=min(8, tl))

    xs = pl.BlockSpec((l, LANE), lambda j: (0, j))
    as_ = pl.BlockSpec((1, LANE), lambda j: (0, j))
    return pl.pallas_call(
        body, name=name, grid=(lanes // LANE,), in_specs=[xs, xs, as_, as_], out_specs=(xs, xs),
        out_shape=(_sds((l, lanes)), _sds((l, lanes))),
        compiler_params=_params(("arbitrary",), 48))(x_re, x_im, a_re, a_im)


ANY = pl.BlockSpec(memory_space=pl.ANY)


def _place():
    mx, my, mc = lax.axis_index("x"), lax.axis_index("y"), lax.axis_index("c")
    return mx, my, mc, [(1 - mx, my), (mx, 1 - my), (1 - mx, 1 - my)]


def _run_copies(copies):
    for cp in copies:
        cp.start()
    for cp in copies:
        cp.wait_recv()
    for cp in copies:
        cp.wait_send()


def _remote(src, dst, sems, k, dev):
    return pltpu.make_async_remote_copy(src_ref=src, dst_ref=dst, send_sem=sems[0].at[k], recv_sem=sems[1].at[k],
                                        device_id=dev, device_id_type=MESH)


def _copy_call(body, name, ins, outs, n_copies, aliases=None):
    return pl.pallas_call(
        body, name=name, in_specs=[ANY] * len(ins), out_specs=[ANY] * len(outs), out_shape=list(outs),
        input_output_aliases=aliases or {},
        scratch_shapes=[pltpu.SemaphoreType.DMA((n_copies,)), pltpu.SemaphoreType.DMA((n_copies,))])(*ins)


def _gather_ici(xs):
    n = len(xs)

    def body(*refs):
        x_refs, y_refs, sems = refs[:n], refs[n:2 * n], refs[2 * n:]
        mx, my, mc, peers = _place()
        me = 2 * mx + my
        _run_copies([_remote(x_refs[i].at[mc], y_refs[i].at[me, mc], sems, 3 * i + j, (px, py, mc))
                     for i in range(n) for j, (px, py) in enumerate(peers)])

    return _copy_call(body, "comm_gather_ici", xs, [_sds((4,) + x.shape, x.dtype) for x in xs], 3 * n)


def _gather_d2d(ys):
    n = len(ys)

    def body(*refs):
        y_in, y_out, sems = refs[:n], refs[n:2 * n], refs[2 * n:]
        mx, my, mc, peers = _place()
        _run_copies([_remote(y_in[i].at[2 * px + py, mc], y_out[i].at[2 * px + py, mc], sems, 3 * i + j, (mx, my, 1 - mc))
                     for i in range(n) for j, (px, py) in enumerate(peers)])

    return _copy_call(body, "comm_gather_d2d", ys, [_sds(y.shape, y.dtype) for y in ys], 3 * n,
                      aliases={i: i for i in range(n)})


def _reduce_ici(gs):
    n = len(gs)

    def body(*refs):
        g_refs, y_refs, sems = refs[:n], refs[n:2 * n], refs[2 * n:]
        mx, my, mc, peers = _place()
        _run_copies([_remote(g_refs[i].at[2 * px + py], y_refs[i].at[j], sems, 3 * i + j, (px, py, mc))
                     for i in range(n) for j, (px, py) in enumerate(peers)])

    return _copy_call(body, "comm_reduce_ici", gs, [_sds((3,) + g.shape[1:], g.dtype) for g in gs], 3 * n)


def _swap_d2d(name, ps, other_layer):
    n = len(ps)

    def body(*refs):
        p_refs, o_refs, sems = refs[:n], refs[n:2 * n], refs[2 * n:]
        mx, my, mc, _ = _place()
        _run_copies([_remote(p_refs[i].at[1 - mc] if other_layer else p_refs[i], o_refs[i], sems, i, (mx, my, 1 - mc))
                     for i in range(n)])

    outs = [_sds(p.shape[1:] if other_layer else p.shape, p.dtype) for p in ps]
    return _copy_call(body, name, ps, outs, n)


def _all_exchange(name, src):
    def body(x_ref, y_ref, send_sems, recv_sems, local_sem):
        mx, my, mc = lax.axis_index("x"), lax.axis_index("y"), lax.axis_index("c")
        me = 4 * mx + 2 * my + mc
        own = pltpu.make_async_copy(x_ref, y_ref.at[me], local_sem)
        own.start()
        copies = []
        for j in range(1, 8):
            px = (1 - mx) if (j & 4) else mx
            py = (1 - my) if (j & 2) else my
            pc = (1 - mc) if (j & 1) else mc
            cp = pltpu.make_async_remote_copy(
                src_ref=x_ref, dst_ref=y_ref.at[me], send_sem=send_sems.at[j - 1], recv_sem=recv_sems.at[j - 1],
                device_id=(px, py, pc), device_id_type=MESH)
            cp.start()
            copies.append(cp)
        for cp in copies:
            cp.wait_recv()
        for cp in copies:
            cp.wait_send()
        own.wait()

    return pl.pallas_call(
        body, name=name, in_specs=[ANY], out_specs=ANY, out_shape=_sds((8,) + src.shape, src.dtype),
        scratch_shapes=[pltpu.SemaphoreType.DMA((7,)), pltpu.SemaphoreType.DMA((7,)), pltpu.SemaphoreType.DMA])(src)


PACK_W = 1024


def _pack(arrs, rows_multiple, dtype):
    flat = jnp.concatenate([a.reshape(-1).astype(dtype) for a in arrs])
    n = flat.shape[0]
    unit = PACK_W * rows_multiple
    tot = -(-n // unit) * unit
    flat = jnp.pad(flat, (0, tot - n))
    return flat.reshape(tot // PACK_W, PACK_W)


def _unpack(flat, shapes):
    flat = flat.reshape(-1)
    out = []
    off = 0
    for s in shapes:
        n = int(np.prod(s))
        out.append(flat[off:off + n].reshape(s))
        off += n
    return out


def _rope_tables(pos):
    l = pos.shape[0]
    tm = min(512, l)
    inv = (np.float32(ROPE_THETA) ** (-np.arange(0, D_ROPE, 2, dtype=np.float32) / np.float32(D_ROPE))).astype(np.float32)
    lane_f = np.zeros((1, LANE), np.float32)
    lane_f[0, D_NOPE:D_NOPE + 16] = inv
    lane_f[0, D_NOPE + 16:D_NOPE + 32] = inv

    def fn(i, p, f):
        ang = p * f
        lane = lax.broadcasted_iota(jnp.int32, ang.shape, 1)
        co = jnp.cos(ang)
        si = jnp.sin(ang)
        c = jnp.where(lane < D_NOPE, 1.0, jnp.where(lane < D_QK, co, 0.0))
        s1 = jnp.where((lane >= D_NOPE) & (lane < D_NOPE + 16), -si, 0.0)
        s2 = jnp.where((lane >= D_NOPE + 16) & (lane < D_QK), si, 0.0)
        return c, s1, s2

    return _rows(fn, name="rope_tables", n=l // tm, ins=[pos, jnp.asarray(lane_f)],
                 in_specs=[_rt(tm, 1), _full((1, LANE))], outs=[_sds((l, LANE))] * 3, out_specs=[_rt(tm, LANE)] * 3)


def _ssm_param_fn(lr, li, log_dt, br, bi):
    dt = jnp.exp(log_dt)
    mag = jnp.exp(lr * dt)
    a_re = mag * jnp.cos(li * dt)
    a_im = mag * jnp.sin(li * dt)
    den = lr * lr + li * li
    e_re = a_re - 1.0
    e_im = a_im
    f_re = (e_re * lr + e_im * li) / den
    f_im = (e_im * lr - e_re * li) / den
    bb_re = f_re[None] * br - f_im[None] * bi
    bb_im = f_re[None] * bi + f_im[None] * br
    return a_re, a_im, bb_re, bb_im


def _ssm_params(name, lr, li, log_dt, br, bi):
    g, n = lr.shape
    c = br.shape[0]
    return _rows(lambda i, *v: _ssm_param_fn(*v), name=name, n=1, ins=[lr, li, log_dt, br, bi],
                 in_specs=[_full((g, n)), _full((g, n)), _full((g, 1)), _full((c, g, n)), _full((c, g, n))],
                 outs=[_sds((g, n)), _sds((g, n)), _sds((c, g, n)), _sds((c, g, n))],
                 out_specs=[_full((g, n)), _full((g, n)), _full((c, g, n)), _full((c, g, n))])


def _ssm_params_bwd(name, lr, li, log_dt, br, bi, d_are, d_aim, d_bbre, d_bbim):
    g, n = lr.shape
    c = br.shape[0]

    def fn(i, lr, li, log_dt, br, bi, g0, g1, g2, g3):
        _, vjp = jax.vjp(_ssm_param_fn, lr, li, log_dt, br, bi)
        return vjp((g0, g1, g2, g3))

    sp = [_full((g, n)), _full((g, n)), _full((g, 1)), _full((c, g, n)), _full((c, g, n))]
    return _rows(fn, name=name, n=1, ins=[lr, li, log_dt, br, bi, d_are, d_aim, d_bbre, d_bbim],
                 in_specs=sp + [_full((g, n)), _full((g, n)), _full((c, g, n)), _full((c, g, n))],
                 outs=[_sds((g, n)), _sds((g, n)), _sds((g, 1)), _sds((c, g, n)), _sds((c, g, n))], out_specs=sp)


_EYE8 = np.eye(8, dtype=np.float32)


def _blockdiag(v):
    j, g, p, q = v.shape
    m = v[:, :, :, None, :] * jnp.asarray(_EYE8)[None, :, None, :, None]
    return m.reshape(j, g * p, g * q)


def _blockdiag_t(m, p, q):
    j = m.shape[0]
    m = m.reshape(j, 8, p, 8, q)
    return jnp.sum(m * jnp.asarray(_EYE8)[None, :, None, :, None], axis=3)


def _to_perm(v, l):
    ns = SCAN_SEGS
    return v.reshape(ns, l // ns, v.shape[-1]).transpose(1, 0, 2).reshape(l, v.shape[-1])


def _from_perm(v, l):
    ns = SCAN_SEGS
    return v.reshape(l // ns, ns, v.shape[-1]).transpose(1, 0, 2).reshape(l, v.shape[-1])


def _prep_layer(w, i):
    p = {}
    w_in = w['w_in'][i]
    z = lambda n: jnp.zeros((D_MODEL, n), w_in.dtype)
    o = Q_LORA + KV_LORA
    p['w_s'] = jnp.concatenate([w_in[:, :o], z(D_NOPE), w_in[:, o:o + D_ROPE], z(HEAD_PAD - D_QK)], axis=1)
    o += D_ROPE
    p['w_u'] = w_in[:, o:o + SSM_WIDTH]
    o += SSM_WIDTH
    p['w_xq'] = w_in[:, o:o + X_WIDTH]
    o += X_WIDTH
    p['w_g'] = w_in[:, o:]
    wq = w['w_q_b'][i].reshape(Q_LORA, MLA_HEADS, D_QK)
    p['w_qb'] = jnp.pad(wq, ((0, 0), (0, 0), (0, HEAD_PAD - D_QK))).reshape(Q_LORA, MLA_PAD)
    wkv = w['w_kv_b'][i].reshape(KV_LORA, MLA_HEADS, D_NOPE + D_V)
    p['w_k'] = jnp.pad(wkv[:, :, :D_NOPE], ((0, 0), (0, 0), (0, HEAD_PAD - D_NOPE))).reshape(KV_LORA, MLA_PAD)
    p['w_v'] = jnp.pad(wkv[:, :, D_NOPE:], ((0, 0), (0, 0), (0, HEAD_PAD - D_V))).reshape(KV_LORA, MLA_PAD)
    wo = w['w_o_mla'][i].reshape(MLA_HEADS, D_V, D_MODEL)
    p['w_oa'] = jnp.pad(wo, ((0, 0), (0, HEAD_PAD - D_V), (0, 0))).reshape(MLA_PAD, D_MODEL)
    for n in ('w_glu', 'w_o_ssm', 'w_mem_kv', 'w_o_cross', 'w_out', 'w_up', 'w_down'):
        p[n] = w[n][i]
    p['conv_w'] = w['conv_w'][i]
    for n in ('norm_mix_g', 'q_a_norm_g', 'kv_a_norm_g', 'b_glu', 'mem_norm_g', 'xq_norm_g', 'xk_norm_g', 'b_gate',
              'norm_ffn_g', 'conv_b'):
        p[n] = w[n][i].reshape(1, -1)
    p['q_norm_g'] = jnp.pad(w['q_norm_g'][i], (0, HEAD_PAD - D_QK)).reshape(1, HEAD_PAD)
    p['k_norm_g'] = jnp.pad(w['k_norm_g'][i], (0, HEAD_PAD - D_QK)).reshape(1, HEAD_PAD)
    p['ssm_d'] = w['ssm_d'][i].reshape(1, SSM_WIDTH)
    p['lr'] = w['ssm_lambda_re'][i]
    p['li'] = w['ssm_lambda_im'][i]
    p['log_dt'] = w['ssm_log_dt'][i].reshape(SSM_GROUPS, 1)
    p['br'] = w['ssm_b_re'][i].transpose(2, 0, 1)
    p['bi'] = w['ssm_b_im'][i].transpose(2, 0, 1)
    cr = w['ssm_c_re'][i].reshape(SSM_JB, 8, SSM_GROUP_CH, SSM_STATE).transpose(0, 1, 3, 2)
    ci = w['ssm_c_im'][i].reshape(SSM_JB, 8, SSM_GROUP_CH, SSM_STATE).transpose(0, 1, 3, 2)
    p['c_mat'] = jnp.concatenate([_blockdiag(cr), -_blockdiag(ci)], axis=1).astype(BF16)
    return p


def _b_mat(bb_re, bb_im):
    r = bb_re.transpose(1, 0, 2).reshape(SSM_JB, 8, SSM_GROUP_CH, SSM_STATE)
    i = bb_im.transpose(1, 0, 2).reshape(SSM_JB, 8, SSM_GROUP_CH, SSM_STATE)
    return jnp.concatenate([_blockdiag(r), _blockdiag(i)], axis=2).astype(BF16)


def _qkv_fn(ps, c, s1, s2, qag, wqb, kvag, wk, wv, qng, kng):
    c_q = ps[:, :Q_LORA]
    c_kv = ps[:, Q_LORA:Q_LORA + KV_LORA]
    kr = ps[:, Q_LORA + KV_LORA:]
    cqn = _rms(c_q, qag, Q_LORA)
    ckvn = _rms(c_kv, kvag, KV_LORA)
    q_raw = _mm(cqn, wqb)
    k_raw = _mm(ckvn, wk) + jnp.concatenate([kr] * MLA_HEADS, axis=-1)
    v = _mm(ckvn, wv)
    q = _heads(_rope, _head_rms(q_raw, qng, MLA_HEADS, D_QK), MLA_HEADS, c, s1, s2)
    k = _heads(_rope, _head_rms(k_raw, kng, MLA_HEADS, D_QK), MLA_HEADS, c, s1, s2)
    lane = lax.broadcasted_iota(jnp.int32, v.shape, 1)
    v = jnp.where((lane & (LANE - 1)) == D_V, 1.0, v)
    return q * ATT_QSCALE, k, v


def _layer_fwd(name, x, tabs, mem, p):
    l = x.shape[0]
    tm = min(512, l)
    nt = l // tm
    sv = {'x0': x}
    sv['p_g'] = _matmul(name + "_in_g", [(x, p['w_g'])], l, 3 * D_MODEL, rms_gain=p['norm_mix_g'])
    sv['p_u'] = _matmul(name + "_in_u", [(x, p['w_u'])], l, SSM_WIDTH, rms_gain=p['norm_mix_g'])
    sv['p_xq'] = _matmul(name + "_in_xq", [(x, p['w_xq'])], l, X_WIDTH, rms_gain=p['norm_mix_g'])
    sv['p_s'] = _matmul(name + "_in_s", [(x, p['w_s'])], l, SMALL_W, rms_gain=p['norm_mix_g'])

    qkv_consts = [p['q_a_norm_g'], p['w_qb'], p['kv_a_norm_g'], p['w_k'], p['w_v'], p['q_norm_g'], p['k_norm_g']]
    qkv_cspecs = [_full(a.shape) for a in qkv_consts]
    def qkv_fwd(i, *a):
        qv, kv, vv = _qkv_fn(*a)
        return qv, kv, vv, jnp.transpose(kv), jnp.transpose(vv)

    q, k, v, k_t, v_t = _rows(qkv_fwd, name=name + "_qkv", n=nt, ins=[sv['p_s'], *tabs, *qkv_consts],
                              in_specs=[_rt(tm, SMALL_W)] + [_rt(tm, LANE)] * 3 + qkv_cspecs,
                              outs=[_sds((l, MLA_PAD), BF16)] * 3 + [_sds((MLA_PAD, l), BF16)] * 2,
                              out_specs=[_rt(tm, MLA_PAD)] * 3 + [pl.BlockSpec((MLA_PAD, tm), lambda i: (0, i))] * 2)
    sv['q'], sv['k'], sv['v'], sv['k_t'] = q, k, v, k_t
    sv['o_a'], sv['lse_t'] = _flash_fwd(name + "_attn", q, k, v_t)

    a_re, a_im, bb_re, bb_im = _ssm_params(name + "_ssm_par", p['lr'], p['li'], p['log_dt'], p['br'], p['bi'])
    sv['a_re'], sv['a_im'] = a_re.reshape(1, SSM_LANES), a_im.reshape(1, SSM_LANES)
    sv['b_mat'] = _b_mat(bb_re, bb_im)
    u_p = _to_perm(sv['p_u'], l)
    sv['u_p'] = u_p

    def bu_fn(i, u, bm):
        res = [_mm(u[:, j * LANE:(j + 1) * LANE], bm[j]) for j in range(SSM_JB)]
        return (jnp.concatenate([r[:, :512] for r in res], axis=-1), jnp.concatenate([r[:, 512:] for r in res], axis=-1))

    ts = min(256, l)
    bu_re, bu_im = _rows(bu_fn, name=name + "_ssm_bu", n=l // ts, ins=[u_p, sv['b_mat']],
                         in_specs=[_rt(ts, SSM_WIDTH), _full(sv['b_mat'].shape)],
                         outs=[_sds((l, SSM_LANES))] * 2, out_specs=[_rt(ts, SSM_LANES)] * 2)
    s_re, s_im = _scan(name + "_ssm_scan", bu_re, bu_im, sv['a_re'], sv['a_im'], reverse=False)
    sv['s_re'], sv['s_im'] = s_re, s_im

    def glu_fn(i, sr, si, u, cm, dsk, wg, bg):
        y = jnp.concatenate([_mm(jnp.concatenate([sr[:, j * 512:(j + 1) * 512], si[:, j * 512:(j + 1) * 512]], axis=-1),
                                 cm[j]) for j in range(SSM_JB)], axis=-1) + dsk * u
        zz = _gelu(y)
        return zz * jax.nn.sigmoid(_mm(zz, wg) + bg)

    glu_consts = [p['c_mat'], p['ssm_d'], p['w_glu'], p['b_glu']]
    zo_p = _rows(glu_fn, name=name + "_ssm_glu", n=l // ts, ins=[s_re, s_im, u_p, *glu_consts],
                 in_specs=[_rt(ts, SSM_LANES), _rt(ts, SSM_LANES), _rt(ts, SSM_WIDTH)] + [_full(a.shape) for a in glu_consts],
                 outs=[_sds((l, SSM_WIDTH), BF16)], out_specs=[_rt(ts, SSM_WIDTH)])[0]
    sv['zo'] = _from_perm(zo_p, l)

    m_len = mem.shape[0]

    def memkv_fn(i, mm_, mg, wmk, xkg):
        kv = _mm(_rms(mm_, mg, D_MODEL), wmk)
        return _head_rms(kv[:, :X_WIDTH], xkg, X_HEADS, X_HEAD_DIM), kv[:, X_WIDTH:]

    mem_consts = [p['mem_norm_g'], p['w_mem_kv'], p['xk_norm_g']]
    k_c, v_c = _rows(memkv_fn, name=name + "_memkv", n=1, ins=[mem, *mem_consts],
                     in_specs=[_full(mem.shape)] + [_full(a.shape) for a in mem_consts],
                     outs=[_sds((m_len, X_WIDTH))] * 2, out_specs=[_full((m_len, X_WIDTH))] * 2)
    sv['k_c'], sv['v_c'] = k_c, v_c

    def cross_fn(i, xq, kc, vc, xqg):
        outs = []
        for h in range(X_HEADS):
            sl = slice(h * LANE, (h + 1) * LANE)
            qh = _rms(xq[:, sl], xqg, X_HEAD_DIM)
            s = _mm_nt(qh, kc[:, sl]) * (X_HEAD_DIM ** -0.5)
            s = s - jnp.max(s, axis=-1, keepdims=True)
            e = jnp.exp(s)
            pr = e / jnp.sum(e, axis=-1, keepdims=True)
            outs.append(_mm(pr, vc[:, sl]))
        return jnp.concatenate(outs, axis=-1)

    sv['o_c'] = _rows(cross_fn, name=name + "_cross", n=nt, ins=[sv['p_xq'], k_c, v_c, p['xq_norm_g']],
                      in_specs=[_rt(tm, X_WIDTH), _full(k_c.shape), _full(v_c.shape), _full((1, LANE))],
                      outs=[_sds((l, X_WIDTH), BF16)], out_specs=[_rt(tm, X_WIDTH)])[0]

    def merge_fn(i, oa, zo, oc, pg, x0, woa, wos, woc, bg, wout):
        gates = jax.nn.sigmoid(pg + bg)
        merged = (gates[:, :D_MODEL] * _mm(oa, woa) + gates[:, D_MODEL:2 * D_MODEL] * _mm(zo, wos)
                  + gates[:, 2 * D_MODEL:] * _mm(oc, woc))
        return x0 + _mm(merged, wout), merged

    merge_consts = [p['w_oa'], p['w_o_ssm'], p['w_o_cross'], p['b_gate'], p['w_out']]
    tg = min(256, l)
    x1, merged = _rows(merge_fn, name=name + "_merge", n=l // tg, ins=[sv['o_a'], sv['zo'], sv['o_c'], sv['p_g'], x, *merge_consts],
                       in_specs=[_rt(tg, MLA_PAD), _rt(tg, SSM_WIDTH), _rt(tg, X_WIDTH), _rt(tg, 3 * D_MODEL), _rt(tg, D_MODEL)]
                       + [_full(a.shape) for a in merge_consts],
                       outs=[_sds((l, D_MODEL)), _sds((l, D_MODEL), BF16)], out_specs=[_rt(tg, D_MODEL)] * 2)
    sv['x1'], sv['merged'] = x1, merged

    up = _matmul(name + "_up", [(x1, p['w_up'])], l, 2 * D_FF, rms_gain=p['norm_ffn_g'])
    sv['up'] = up
    tc = min(128, l)

    def conv_fn(i, upt, halo, cw, cb):
        upc = _conv(i, upt, halo, cw) + cb
        return _silu(upc[:, :D_FF]) * upc[:, D_FF:]

    act = _rows(conv_fn, name=name + "_conv", n=l // tc, ins=[up, up, p['conv_w'], p['conv_b']],
                in_specs=[_rt(tc, 2 * D_FF), _halo_prev(tc, 2 * D_FF), _full((3, 2 * D_FF)), _full((1, 2 * D_FF))],
                outs=[_sds((l, D_FF), BF16)], out_specs=[_rt(tc, D_FF)])[0]
    sv['act'] = act
    x2 = _matmul(name + "_down", [(act, p['w_down'])], l, D_MODEL, resid=x1)
    return x2, sv


def _halo_prev(tm, w):
    return pl.BlockSpec((8, w), lambda i: (jnp.maximum(i * (tm // 8) - 1, 0), 0))


def _halo_next(tm, w, n_tiles):
    last = n_tiles * (tm // 8) - 1
    return pl.BlockSpec((8, w), lambda i: (jnp.minimum((i + 1) * (tm // 8), last), 0))


def _conv(i, tile, halo, cw):
    halo = jnp.where(i > 0, halo, 0.0)
    ext = jnp.concatenate([halo, tile], axis=0)
    n = ext.shape[0]
    x1 = pltpu.roll(ext, 1, 0)[8:]
    x2 = pltpu.roll(ext, 2, 0)[8:]
    del n
    return cw[0:1] * x2 + cw[1:2] * x1 + cw[2:3] * tile


def _layer_bwd(name, dx2, sv, tabs, mem, p):
    l = dx2.shape[0]
    tm = min(512, l)
    nt = l // tm
    g = {}
    x1 = sv['x1']
    dact = _matmul(name + "_b_down", [(dx2, p['w_down'])], l, D_FF, nt=True)
    g['w_down'] = _matmul_tn(name + "_gw_down", sv['act'], dx2)
    tc = min(128, l)
    ntc = l // tc

    def conv_b(i, upt, up_prev, up_next, da, da_next, cw, cb):
        up_prev = jnp.where(i > 0, up_prev, 0.0)
        da_next = jnp.where(i < ntc - 1, da_next, 0.0)
        ext = jnp.concatenate([up_prev, upt, up_next], axis=0)
        x0 = ext[8:]
        xm1 = pltpu.roll(ext, 1, 0)[8:]
        xm2 = pltpu.roll(ext, 2, 0)[8:]
        upc = cw[0:1] * xm2 + cw[1:2] * xm1 + cw[2:3] * x0 + cb
        _, vjp = jax.vjp(lambda a, b: _silu(a) * b, upc[:, :D_FF], upc[:, D_FF:])
        dg, dv = vjp(jnp.concatenate([da, da_next], axis=0))
        dupc = jnp.concatenate([dg, dv], axis=-1)
        n = dupc.shape[0]
        dup = cw[2:3] * dupc[:tc] + cw[1:2] * pltpu.roll(dupc, n - 1, 0)[:tc] + cw[0:1] * pltpu.roll(dupc, n - 2, 0)[:tc]
        dt = dupc[:tc]
        dcw = _row_select([_colsum(dt * xm2[:tc]), _colsum(dt * xm1[:tc]), _colsum(dt * upt)], 8)
        return dup, dcw, _colsum(dt)

    dup, g_cw, g_cb = _rows(
        conv_b, name=name + "_b_conv", n=ntc, ins=[sv['up'], sv['up'], sv['up'], dact, dact, p['conv_w'], p['conv_b']],
        in_specs=[_rt(tc, 2 * D_FF), _halo_prev(tc, 2 * D_FF), _halo_next(tc, 2 * D_FF, ntc), _rt(tc, D_FF),
                  _halo_next(tc, D_FF, ntc), _full((3, 2 * D_FF)), _full((1, 2 * D_FF))],
        outs=[_sds((l, 2 * D_FF)), _sds((8, 2 * D_FF)), _sds((1, 2 * D_FF))],
        out_specs=[_rt(tc, 2 * D_FF), _full((8, 2 * D_FF)), _full((1, 2 * D_FF))], n_acc=2, vmem=56)
    g['conv_w'] = g_cw[:3]
    g['conv_b'] = g_cb
    dh2 = _matmul(name + "_b_up", [(dup, p['w_up'])], l, D_MODEL, nt=True, tm=256)
    g['w_up'] = _matmul_tn(name + "_gw_up", x1, dup, rms_gain=p['norm_ffn_g'])

    def norm_b(i, xv, dh, dres, gn):
        _, vjp = jax.vjp(lambda a, b: _rms(a, b, D_MODEL), xv, gn)
        dxv, dgn = vjp(dh)
        return dres + dxv, dgn

    dx1, g['norm_ffn_g'] = _rows(norm_b, name=name + "_b_norm2", n=nt, ins=[x1, dh2, dx2, p['norm_ffn_g']],
                                 in_specs=[_rt(tm, D_MODEL)] * 3 + [_full((1, D_MODEL))],
                                 outs=[_sds((l, D_MODEL)), _sds((1, D_MODEL))], out_specs=[_rt(tm, D_MODEL), _full((1, D_MODEL))],
                                 n_acc=1)

    tg = min(256, l)

    def merge_b(i, dx, oa, zo, oc, pg, woa, wos, woc, bg, wout):
        dm = _mm_nt(dx, wout)
        gates = jax.nn.sigmoid(pg + bg)
        ys = [_mm(oa, woa), _mm(zo, wos), _mm(oc, woc)]
        dys, dpg = [], []
        for b in range(3):
            gb = gates[:, b * D_MODEL:(b + 1) * D_MODEL]
            dys.append(dm * gb)
            dpg.append(dm * ys[b] * gb * (1.0 - gb))
        dpg = jnp.concatenate(dpg, axis=-1)
        return (_mm_nt(dys[0], woa), _mm_nt(dys[1], wos), _mm_nt(dys[2], woc), dpg, dys[0], dys[1], dys[2], _colsum(dpg))

    merge_consts = [p['w_oa'], p['w_o_ssm'], p['w_o_cross'], p['b_gate'], p['w_out']]
    (do_a, dzo, do_c, dp_g, dy_a, dy_b, dy_c, g['b_gate']) = _rows(
        merge_b, name=name + "_b_merge", n=l // tg, ins=[dx1, sv['o_a'], sv['zo'], sv['o_c'], sv['p_g'], *merge_consts],
        in_specs=[_rt(tg, D_MODEL), _rt(tg, MLA_PAD), _rt(tg, SSM_WIDTH), _rt(tg, X_WIDTH), _rt(tg, 3 * D_MODEL)]
        + [_full(a.shape) for a in merge_consts],
        outs=[_sds((l, MLA_PAD)), _sds((l, SSM_WIDTH)), _sds((l, X_WIDTH)), _sds((l, 3 * D_MODEL)),
              _sds((l, D_MODEL), BF16), _sds((l, D_MODEL), BF16), _sds((l, D_MODEL), BF16), _sds((1, 3 * D_MODEL))],
        out_specs=[_rt(tg, MLA_PAD), _rt(tg, SSM_WIDTH), _rt(tg, X_WIDTH), _rt(tg, 3 * D_MODEL),
                   _rt(tg, D_MODEL), _rt(tg, D_MODEL), _rt(tg, D_MODEL), _full((1, 3 * D_MODEL))], n_acc=1, vmem=56)
    g['w_out'] = _matmul_tn(name + "_gw_out", sv['merged'], dx1)
    g['w_oa'] = _matmul_tn(name + "_gw_oa", sv['o_a'], dy_a)
    g['w_o_ssm'] = _matmul_tn(name + "_gw_os", sv['zo'], dy_b)
    g['w_o_cross'] = _matmul_tn(name + "_gw_oc", sv['o_c'], dy_c)

    k_c, v_c = sv['k_c'], sv['v_c']
    m_len = k_c.shape[0]

    def cross_b(i, xq, do, kc, vc, xqg):
        dxq, dk, dv = [], [], []
        dg = jnp.zeros((1, LANE), F32)
        for h in range(X_HEADS):
            sl = slice(h * LANE, (h + 1) * LANE)
            qh, vjp = jax.vjp(lambda a, b: _rms(a, b, X_HEAD_DIM), xq[:, sl], xqg)
            sc = X_HEAD_DIM ** -0.5
            s = _mm_nt(qh, kc[:, sl]) * sc
            s = s - jnp.max(s, axis=-1, keepdims=True)
            e = jnp.exp(s)
            pr = e / jnp.sum(e, axis=-1, keepdims=True)
            doh = do[:, sl]
            dv.append(_mm_tn(pr, doh))
            dp = _mm_nt(doh, vc[:, sl])
            ds = pr * (dp - jnp.sum(dp * pr, axis=-1, keepdims=True)) * sc
            dk.append(_mm_tn(ds, qh))
            dxh, dgh = vjp(_mm(ds, kc[:, sl]))
            dxq.append(dxh)
            dg = dg + dgh
        return jnp.concatenate(dxq, axis=-1), jnp.concatenate(dk, axis=-1), jnp.concatenate(dv, axis=-1), dg

    dp_xq, dk_c, dv_c, g['xq_norm_g'] = _rows(
        cross_b, name=name + "_b_cross", n=nt, ins=[sv['p_xq'], do_c, k_c, v_c, p['xq_norm_g']],
        in_specs=[_rt(tm, X_WIDTH), _rt(tm, X_WIDTH), _full(k_c.shape), _full(v_c.shape), _full((1, LANE))],
        outs=[_sds((l, X_WIDTH)), _sds((m_len, X_WIDTH)), _sds((m_len, X_WIDTH)), _sds((1, LANE))],
        out_specs=[_rt(tm, X_WIDTH), _full((m_len, X_WIDTH)), _full((m_len, X_WIDTH)), _full((1, LANE))], n_acc=3)

    def memkv_b(i, mm_, dk, dv, mg, wmk, xkg):
        memn, vjp_n = jax.vjp(lambda a, b: _rms(a, b, D_MODEL), mm_, mg)
        kv = _mm(memn, wmk)
        _, vjp_k = jax.vjp(lambda a, b: _head_rms(a, b, X_HEADS, X_HEAD_DIM), kv[:, :X_WIDTH], xkg)
        dkr, dxkg = vjp_k(dk)
        dkv = jnp.concatenate([dkr, dv], axis=-1)
        _, dmg = vjp_n(_mm_nt(dkv, wmk))
        return _mm_tn(memn, dkv), dmg, dxkg

    mem_consts = [p['mem_norm_g'], p['w_mem_kv'], p['xk_norm_g']]
    g['w_mem_kv'], g['mem_norm_g'], g['xk_norm_g'] = _rows(
        memkv_b, name=name + "_b_memkv", n=1, ins=[mem, dk_c, dv_c, *mem_consts],
        in_specs=[_full(mem.shape), _full(dk_c.shape), _full(dv_c.shape)] + [_full(a.shape) for a in mem_consts],
        outs=[_sds((D_MODEL, 2 * X_WIDTH)), _sds((1, D_MODEL)), _sds((1, LANE))],
        out_specs=[_full((D_MODEL, 2 * X_WIDTH)), _full((1, D_MODEL)), _full((1, LANE))])

    u_p = sv['u_p']
    dzo_p = _to_perm(dzo, l)
    s_re, s_im = sv['s_re'], sv['s_im']

    def glu_b(i, sr, si, u, dz, cm, dsk, wg, bg):
        cats = [jnp.concatenate([sr[:, j * 512:(j + 1) * 512], si[:, j * 512:(j + 1) * 512]], axis=-1) for j in range(SSM_JB)]
        y = jnp.concatenate([_mm(cats[j], cm[j]) for j in range(SSM_JB)], axis=-1) + dsk * u
        zz, vjp_g = jax.vjp(_gelu, y)
        t = _mm(zz, wg) + bg
        sg = jax.nn.sigmoid(t)
        dt = dz * zz * sg * (1.0 - sg)
        dzz = dz * sg + _mm_nt(dt, wg)
        dy = vjp_g(dzz)[0]
        dss = [_mm_nt(dy[:, j * LANE:(j + 1) * LANE], cm[j]) for j in range(SSM_JB)]
        dsr = jnp.concatenate([d[:, :512] for d in dss], axis=-1)
        dsi = jnp.concatenate([d[:, 512:] for d in dss], axis=-1)
        dcm = jnp.stack([_mm_tn(cats[j], dy[:, j * LANE:(j + 1) * LANE]) for j in range(SSM_JB)], axis=0)
        return dsr, dsi, dy * dsk, dcm, _colsum(dy * u), _mm_tn(zz, dt), _colsum(dt)

    glu_consts = [p['c_mat'], p['ssm_d'], p['w_glu'], p['b_glu']]
    ts = min(256, l)
    nts = l // ts
    ds_re, ds_im, du_dir, g['c_mat'], g['ssm_d'], g['w_glu'], g['b_glu'] = _rows(
        glu_b, name=name + "_b_glu", n=nts, ins=[s_re, s_im, u_p, dzo_p, *glu_consts],
        in_specs=[_rt(ts, SSM_LANES), _rt(ts, SSM_LANES), _rt(ts, SSM_WIDTH), _rt(ts, SSM_WIDTH)] + [_full(a.shape) for a in glu_consts],
        outs=[_sds((l, SSM_LANES)), _sds((l, SSM_LANES)), _sds((l, SSM_WIDTH)), _sds((SSM_JB, 1024, LANE)), _sds((1, SSM_WIDTH)),
              _sds((SSM_WIDTH, SSM_WIDTH)), _sds((1, SSM_WIDTH))],
        out_specs=[_rt(ts, SSM_LANES), _rt(ts, SSM_LANES), _rt(ts, SSM_WIDTH), _full((SSM_JB, 1024, LANE)), _full((1, SSM_WIDTH)),
                   _full((SSM_WIDTH, SSM_WIDTH)), _full((1, SSM_WIDTH))], n_acc=4)
    gb_re, gb_im = _scan(name + "_b_scan", ds_re, ds_im, sv['a_re'], -sv['a_im'], reverse=True)
    ns = SCAN_SEGS
    last_blk = l // ns - 1

    def da_fn(i, gr, gi, sr, si, hr, hi, lr_, li_):
        rid = lax.broadcasted_iota(jnp.int32, lr_.shape, 0)
        fr = jnp.where(rid == 0, 0.0, pltpu.roll(lr_, 1, 0))
        fi = jnp.where(rid == 0, 0.0, pltpu.roll(li_, 1, 0))
        hr = jnp.where(i == 0, fr, hr)
        hi = jnp.where(i == 0, fi, hi)
        if ts > ns:
            pr = jnp.concatenate([hr, sr[:ts - ns]], axis=0)
            pi = jnp.concatenate([hi, si[:ts - ns]], axis=0)
        else:
            pr, pi = hr, hi
        return _colsum(gr * pr + gi * pi), _colsum(gi * pr - gr * pi)

    hprev = pl.BlockSpec((ns, SSM_LANES), lambda i: (jnp.maximum(i * (ts // ns) - 1, 0), 0))
    hlast = pl.BlockSpec((ns, SSM_LANES), lambda i: (last_blk, 0))
    da_re, da_im = _rows(da_fn, name=name + "_b_da", n=nts, ins=[gb_re, gb_im, s_re, s_im, s_re, s_im, s_re, s_im],
                         in_specs=[_rt(ts, SSM_LANES)] * 4 + [hprev, hprev, hlast, hlast],
                         outs=[_sds((1, SSM_LANES))] * 2, out_specs=[_full((1, SSM_LANES))] * 2, n_acc=2)

    def bu_b(i, dbr, dbi, u, dud, bm):
        dus, dbm = [], []
        for j in range(SSM_JB):
            cat = jnp.concatenate([dbr[:, j * 512:(j + 1) * 512], dbi[:, j * 512:(j + 1) * 512]], axis=-1)
            dus.append(_mm_nt(cat, bm[j]))
            dbm.append(_mm_tn(u[:, j * LANE:(j + 1) * LANE], cat))
        return dud + jnp.concatenate(dus, axis=-1), jnp.stack(dbm, axis=0)

    du_p, d_bmat = _rows(bu_b, name=name + "_b_bu", n=nts, ins=[gb_re, gb_im, u_p, du_dir, sv['b_mat']],
                         in_specs=[_rt(ts, SSM_LANES), _rt(ts, SSM_LANES), _rt(ts, SSM_WIDTH), _rt(ts, SSM_WIDTH),
                                   _full(sv['b_mat'].shape)],
                         outs=[_sds((l, SSM_WIDTH)), _sds((SSM_JB, LANE, 1024))],
                         out_specs=[_rt(ts, SSM_WIDTH), _full((SSM_JB, LANE, 1024))], n_acc=1)
    dp_u = _from_perm(du_p, l)
    dbb_re = _blockdiag_t(d_bmat[:, :, :512], SSM_GROUP_CH, SSM_STATE).reshape(SSM_GROUPS, SSM_GROUP_CH, SSM_STATE).transpose(1, 0, 2)
    dbb_im = _blockdiag_t(d_bmat[:, :, 512:], SSM_GROUP_CH, SSM_STATE).reshape(SSM_GROUPS, SSM_GROUP_CH, SSM_STATE).transpose(1, 0, 2)
    g['lr'], g['li'], g['log_dt'], g['br'], g['bi'] = _ssm_params_bwd(
        name + "_b_ssm_par", p['lr'], p['li'], p['log_dt'], p['br'], p['bi'],
        da_re.reshape(SSM_GROUPS, SSM_STATE), da_im.reshape(SSM_GROUPS, SSM_STATE), dbb_re, dbb_im)

    dq_t, dk, dv = _flash_bwd(name + "_b_attn", sv['q'], sv['k'], sv['v'], sv['k_t'], sv['o_a'], sv['lse_t'], do_a)

    def qkv_b(i, ps, c, s1, s2, dq_, dk_, dv_, qag, wqb, kvag, wk, wv, qng, kng):
        c_q = ps[:, :Q_LORA]
        c_kv = ps[:, Q_LORA:Q_LORA + KV_LORA]
        kr = ps[:, Q_LORA + KV_LORA:]
        cqn, vjp_cq = jax.vjp(lambda a, b: _rms(a, b, Q_LORA), c_q, qag)
        ckvn, vjp_ckv = jax.vjp(lambda a, b: _rms(a, b, KV_LORA), c_kv, kvag)
        q_raw = _mm(cqn, wqb)
        k_raw = _mm(ckvn, wk) + jnp.concatenate([kr] * MLA_HEADS, axis=-1)
        _, vjp_qn = jax.vjp(lambda a, b: _head_rms(a, b, MLA_HEADS, D_QK), q_raw, qng)
        _, vjp_kn = jax.vjp(lambda a, b: _head_rms(a, b, MLA_HEADS, D_QK), k_raw, kng)
        dq_raw, dqng = vjp_qn(_heads(_rope_t, jnp.transpose(dq_[0]), MLA_HEADS, c, s1, s2))
        dk_raw, dkng = vjp_kn(_heads(_rope_t, dk_, MLA_HEADS, c, s1, s2))
        dkr = dk_raw[:, :LANE]
        for h in range(1, MLA_HEADS):
            dkr = dkr + dk_raw[:, h * LANE:(h + 1) * LANE]
        dcq, dqag = vjp_cq(_mm_nt(dq_raw, wqb))
        dckv, dkvag = vjp_ckv(_mm_nt(dk_raw, wk) + _mm_nt(dv_, wv))
        dps = jnp.concatenate([dcq, dckv, dkr], axis=-1)
        return (dps, _mm_tn(cqn, dq_raw), _mm_tn(ckvn, dk_raw), _mm_tn(ckvn, dv_), dqag, dkvag, dqng, dkng)

    qkv_consts = [p['q_a_norm_g'], p['w_qb'], p['kv_a_norm_g'], p['w_k'], p['w_v'], p['q_norm_g'], p['k_norm_g']]
    (dp_s, g['w_qb'], g['w_k'], g['w_v'], g['q_a_norm_g'], g['kv_a_norm_g'], g['q_norm_g'], g['k_norm_g']) = _rows(
        qkv_b, name=name + "_b_qkv", n=nt, ins=[sv['p_s'], *tabs, dq_t, dk, dv, *qkv_consts],
        in_specs=[_rt(tm, SMALL_W)] + [_rt(tm, LANE)] * 3
        + [pl.BlockSpec((1, MLA_PAD, tm), lambda i: (i // (dq_t.shape[2] // tm), 0, i % (dq_t.shape[2] // tm)))]
        + [_rt(tm, MLA_PAD)] * 2 + [_full(a.shape) for a in qkv_consts],
        outs=[_sds((l, SMALL_W)), _sds((Q_LORA, MLA_PAD)), _sds((KV_LORA, MLA_PAD)), _sds((KV_LORA, MLA_PAD)),
              _sds((1, Q_LORA)), _sds((1, KV_LORA)), _sds((1, LANE)), _sds((1, LANE))],
        out_specs=[_rt(tm, SMALL_W), _full((Q_LORA, MLA_PAD)), _full((KV_LORA, MLA_PAD)), _full((KV_LORA, MLA_PAD)),
                   _full((1, Q_LORA)), _full((1, KV_LORA)), _full((1, LANE)), _full((1, LANE))], n_acc=7)

    x0 = sv['x0']
    dh = _matmul(name + "_b_in", [(dp_g, p['w_g']), (dp_u, p['w_u']), (dp_xq, p['w_xq']), (dp_s, p['w_s'])], l, D_MODEL, nt=True,
                 tm=256)
    gm = p['norm_mix_g']
    g['w_g'] = _matmul_tn(name + "_gw_g", x0, dp_g, rms_gain=gm)
    g['w_u'] = _matmul_tn(name + "_gw_u", x0, dp_u, rms_gain=gm)
    g['w_xq'] = _matmul_tn(name + "_gw_xq", x0, dp_xq, rms_gain=gm)
    g['w_s'] = _matmul_tn(name + "_gw_s", x0, dp_s, rms_gain=gm)
    dx0, g['norm_mix_g'] = _rows(norm_b, name=name + "_b_norm1", n=nt, ins=[x0, dh, dx1, gm],
                                 in_specs=[_rt(tm, D_MODEL)] * 3 + [_full((1, D_MODEL))],
                                 outs=[_sds((l, D_MODEL)), _sds((1, D_MODEL))], out_specs=[_rt(tm, D_MODEL), _full((1, D_MODEL))],
                                 n_acc=1)
    return dx0, g


def _unprep_grads(g):
    o = {}
    ws = g['w_s']
    o['w_in'] = jnp.concatenate([ws[:, :Q_LORA + KV_LORA], ws[:, Q_LORA + KV_LORA + D_NOPE:Q_LORA + KV_LORA + D_QK],
                                 g['w_u'], g['w_xq'], g['w_g']], axis=1)
    o['w_q_b'] = g['w_qb'].reshape(Q_LORA, MLA_HEADS, HEAD_PAD)[:, :, :D_QK].reshape(Q_LORA, MLA_HEADS * D_QK)
    gk = g['w_k'].reshape(KV_LORA, MLA_HEADS, HEAD_PAD)[:, :, :D_NOPE]
    gv = g['w_v'].reshape(KV_LORA, MLA_HEADS, HEAD_PAD)[:, :, :D_V]
    o['w_kv_b'] = jnp.concatenate([gk, gv], axis=2).reshape(KV_LORA, MLA_HEADS * (D_NOPE + D_V))
    o['w_o_mla'] = g['w_oa'].reshape(MLA_HEADS, HEAD_PAD, D_MODEL)[:, :D_V].reshape(MLA_HEADS * D_V, D_MODEL)
    for n in ('w_glu', 'w_o_ssm', 'w_mem_kv', 'w_o_cross', 'w_out', 'w_up', 'w_down', 'conv_w'):
        o[n] = g[n]
    for n in ('norm_mix_g', 'q_a_norm_g', 'kv_a_norm_g', 'b_glu', 'mem_norm_g', 'xq_norm_g', 'xk_norm_g', 'b_gate',
              'norm_ffn_g', 'conv_b'):
        o[n] = g[n].reshape(-1)
    o['q_norm_g'] = g['q_norm_g'].reshape(-1)[:D_QK]
    o['k_norm_g'] = g['k_norm_g'].reshape(-1)[:D_QK]
    o['ssm_d'] = g['ssm_d'].reshape(SSM_GROUPS, SSM_GROUP_CH)
    o['ssm_lambda_re'] = g['lr']
    o['ssm_lambda_im'] = g['li']
    o['ssm_log_dt'] = g['log_dt'].reshape(SSM_GROUPS)
    o['ssm_b_re'] = g['br'].transpose(1, 2, 0)
    o['ssm_b_im'] = g['bi'].transpose(1, 2, 0)
    dc = g['c_mat']
    o['ssm_c_re'] = _blockdiag_t(dc[:, :512], SSM_STATE, SSM_GROUP_CH).transpose(0, 1, 3, 2).reshape(SSM_GROUPS, SSM_GROUP_CH, SSM_STATE)
    o['ssm_c_im'] = -_blockdiag_t(dc[:, 512:], SSM_STATE, SSM_GROUP_CH).transpose(0, 1, 3, 2).reshape(SSM_GROUPS, SSM_GROUP_CH, SSM_STATE)
    return o


def _local_step(x, mem, pos, target, w):
    l = x.shape[0]
    tm = min(512, l)
    tabs = _rope_tables(pos.astype(F32).reshape(l, 1))
    ps = [_prep_layer(w, i) for i in range(DEPTH)]
    saved = []
    h = x
    for i in range(DEPTH):
        h, sv = _layer_fwd("l%d" % i, h, tabs, mem, ps[i])
        saved.append(sv)

    def loss_fn(i, y, t):
        e = y - t
        per_tok = jnp.sum(e * e, axis=-1, keepdims=True) * (1.0 / D_MODEL)
        tot = 0.5 * jnp.sum(per_tok, axis=0, keepdims=True)
        return e * (1.0 / D_MODEL), jnp.broadcast_to(tot, (1, LANE))

    dy, loss = _rows(loss_fn, name="loss", n=l // tm, ins=[h, target], in_specs=[_rt(tm, D_MODEL)] * 2,
                     outs=[_sds((l, D_MODEL)), _sds((1, LANE))], out_specs=[_rt(tm, D_MODEL), _full((1, LANE))], n_acc=1)
    grads = []
    d = dy
    for i in reversed(range(DEPTH)):
        d, g = _layer_bwd("l%d" % i, d, saved[i], tabs, mem, ps[i])
        grads.append(_unprep_grads(g))
    return loss[0, 0], d, grads[::-1]


def _sum_picked(name, slots, pick, extra, out_dtype):
    _, r, c = slots.shape
    e = extra.shape[0]
    tr = _row_tile(r)

    def body(pk, s_ref, x_ref, o_ref):
        acc = s_ref[...].astype(F32)
        for k in range(e):
            acc = acc + x_ref[k].astype(F32)
        o_ref[...] = acc.astype(o_ref.dtype)

    grid_spec = pltpu.PrefetchScalarGridSpec(
        num_scalar_prefetch=1, grid=(r // tr,),
        in_specs=[pl.BlockSpec((None, tr, c), lambda i, pk: (pk[0], i, 0)), pl.BlockSpec((e, tr, c), lambda i, pk: (0, i, 0))],
        out_specs=pl.BlockSpec((tr, c), lambda i, pk: (i, 0)))
    return pl.pallas_call(body, name=name, grid_spec=grid_spec, out_shape=_sds((r, c), out_dtype),
                          compiler_params=_params(("arbitrary",), 48))(pick, slots, extra)


def _row_tile(r):
    for t in (256, 128, 64, 32, 16, 8):
        if r % t == 0:
            return t
    return r


def _adamw(name, parts, w, m, v):
    r, cw = w.shape
    tr = _row_tile(r)
    np_ = len(parts)

    def fn(i, *vals):
        wv, mv, vv = vals[np_:]
        terms = []
        for pv in vals[:np_]:
            terms += [pv] if pv.ndim == 2 else [pv[k] for k in range(pv.shape[0])]
        g = terms[0]
        for t in terms[1:]:
            g = g + t
        mn = ADAM_B1 * mv + (1.0 - ADAM_B1) * g
        vn = ADAM_B2 * vv + (1.0 - ADAM_B2) * (g * g)
        m_hat = mn / (1.0 - ADAM_B1 ** ADAM_STEP)
        v_hat = vn / (1.0 - ADAM_B2 ** ADAM_STEP)
        delta = -ADAM_LR * (m_hat / (jnp.sqrt(v_hat) + ADAM_EPS) + ADAM_WD * wv)
        return g, delta, mn, vn

    pspecs = [_rt(tr, cw) if p.ndim == 2 else pl.BlockSpec((p.shape[0], tr, cw), lambda i: (0, i, 0)) for p in parts]
    return _rows(fn, name=name, n=r // tr, ins=[*parts, w, m, v], in_specs=pspecs + [_rt(tr, cw)] * 3,
                 outs=[_sds((r, cw))] * 4, out_specs=[_rt(tr, cw)] * 4)


def _shard_of(a, axis, k):
    n = a.shape[axis] // 4
    return lax.slice_in_dim(a, k * n, (k + 1) * n, axis=axis)


def _step(a):
    x = a['x'][0]
    mem = a['mem'][0]
    pos = a['positions'][0]
    target = a['loss_target'][0]

    me = 2 * lax.axis_index("x") + lax.axis_index("y")

    mine = [a[n] if n == 'conv_w' else a[n].astype(BF16) for n in SHARDED]
    got = _gather_d2d(_gather_ici(mine))
    w = {}
    for n, own, y in zip(SHARDED, mine, got):
        ax = SHARD_AXIS[n] - 1
        w[n] = [jnp.concatenate([jnp.where(me == k, own[i], y[k, i]) for k in range(4)], axis=ax) for i in range(DEPTH)]
    for n in SMALL:
        w[n] = a[n]

    loss, grad_x, grads = _local_step(x, mem, pos, target, w)

    mc = lax.axis_index("c")
    mc1 = mc.astype(jnp.int32).reshape(1)
    me1 = me.astype(jnp.int32).reshape(1)
    gsh = []
    for n in SHARDED:
        ax = SHARD_AXIS[n] - 1
        gsh.append(jnp.stack([jnp.stack([_shard_of(grads[i][n], ax, k) for k in range(4)], axis=0)
                              for i in range(DEPTH)], axis=0).astype(BF16))
    sib = _swap_d2d("comm_reduce_pair", gsh, other_layer=True)
    pair = []
    for n, g, s in zip(SHARDED, gsh, sib):
        rows, cols = g.shape[-2:]
        pair.append(_sum_picked("sum2_" + n, g.reshape(DEPTH, 4 * rows, cols), mc1, s.reshape(1, 4 * rows, cols), BF16)
                    .reshape(4, rows, cols))
    got = _reduce_ici(pair)
    parts = [_sum_picked("sum4_" + n, p4, me1, g3, F32) for n, p4, g3 in zip(SHARDED, pair, got)]
    others = _swap_d2d("comm_reduce_d2d", parts, other_layer=False)
    res_sh = []
    for n, part, other in zip(SHARDED, parts, others):
        cols = part.shape[-1]
        full = jnp.where(mc == 0, jnp.stack([part, other], axis=0), jnp.stack([other, part], axis=0))
        res = _adamw("adamw_" + n, [full.reshape(-1, cols)], *[a[pre + n].reshape(-1, cols) for pre in ('', 'm_', 'v_')])
        res_sh.append([r.reshape(a[n].shape) for r in res])
    res_sh = [[res_sh[j][kind] for j in range(len(SHARDED))] for kind in range(4)]

    sm_shapes = [a[n].shape for n in SMALL] + [(1,)]
    gsm = _pack([jnp.stack([grads[i][n] for i in range(DEPTH)], axis=0) for n in SMALL] + [loss.reshape(1)], 8, F32)
    alls = _all_exchange("comm_reduce_small", gsm)
    zero1 = jnp.zeros((1,), F32)
    res_sm = _adamw("adamw_small", [alls], *[_pack([a[pre + n] for n in SMALL] + [zero1], 8, F32) for pre in ('', 'm_', 'v_')])
    res_sm = [_unpack(r, sm_shapes) for r in res_sm]
    loss = res_sm[0][-1][0]

    outs = [loss, grad_x[None]]
    for kind in range(4):
        byname = dict(zip(SHARDED, res_sh[kind]))
        byname.update(zip(SMALL, res_sm[kind]))
        outs += [byname[n] for n in WEIGHTS]
    return tuple(outs)


def kernel(x, mem, positions, norm_mix_g, w_in, q_a_norm_g, w_q_b, kv_a_norm_g, w_kv_b, q_norm_g, k_norm_g, w_o_mla, ssm_lambda_re, ssm_lambda_im, ssm_log_dt, ssm_b_re, ssm_b_im, ssm_c_re, ssm_c_im, ssm_d, w_glu, b_glu, w_o_ssm, mem_norm_g, w_mem_kv, xq_norm_g, xk_norm_g, w_o_cross, b_gate, w_out, norm_ffn_g, w_up, conv_w, conv_b, w_down, loss_target, m_norm_mix_g, m_w_in, m_q_a_norm_g, m_w_q_b, m_kv_a_norm_g, m_w_kv_b, m_q_norm_g, m_k_norm_g, m_w_o_mla, m_ssm_lambda_re, m_ssm_lambda_im, m_ssm_log_dt, m_ssm_b_re, m_ssm_b_im, m_ssm_c_re, m_ssm_c_im, m_ssm_d, m_w_glu, m_b_glu, m_w_o_ssm, m_mem_norm_g, m_w_mem_kv, m_xq_norm_g, m_xk_norm_g, m_w_o_cross, m_b_gate, m_w_out, m_norm_ffn_g, m_w_up, m_conv_w, m_conv_b, m_w_down, v_norm_mix_g, v_w_in, v_q_a_norm_g, v_w_q_b, v_kv_a_norm_g, v_w_kv_b, v_q_norm_g, v_k_norm_g, v_w_o_mla, v_ssm_lambda_re, v_ssm_lambda_im, v_ssm_log_dt, v_ssm_b_re, v_ssm_b_im, v_ssm_c_re, v_ssm_c_im, v_ssm_d, v_w_glu, v_b_glu, v_w_o_ssm, v_mem_norm_g, v_w_mem_kv, v_xq_norm_g, v_xk_norm_g, v_w_o_cross, v_b_gate, v_w_out, v_norm_ffn_g, v_w_up, v_conv_w, v_conv_b, v_w_down):
    return _step(dict(locals()))
```

```python
import functools
import math

import numpy as np
import jax
import jax.numpy as jnp
from jax import lax
from jax.experimental import pallas as pl
from jax.experimental.pallas import tpu as pltpu

F32 = jnp.float32
BF16 = jnp.bfloat16
MESH = pl.DeviceIdType.MESH

DEPTH = 2
D_MODEL = 1024
EPS = 1e-6
MLA_HEADS = 8
Q_LORA = 384
KV_LORA = 256
D_NOPE = 64
D_ROPE = 32
D_QK = D_NOPE + D_ROPE
D_V = 64
HEAD_PAD = 128
MLA_PAD = MLA_HEADS * HEAD_PAD
ROPE_THETA = 10000.0
SSM_GROUPS = 32
SSM_GROUP_CH = 16
SSM_WIDTH = 512
SSM_STATE = 64
SSM_LANES = SSM_GROUPS * SSM_STATE
SSM_JB = 4
X_HEADS = 4
X_HEAD_DIM = 128
X_WIDTH = 512
D_FF = 2816
SMALL_W = Q_LORA + KV_LORA + HEAD_PAD
SCAN_SEGS = 32
SSM_STATE_DTYPE = BF16
LANE = 128
NEG = -1e30

ADAM_LR = 0.001
ADAM_B1 = 0.9
ADAM_B2 = 0.999
ADAM_EPS = 1e-08
ADAM_WD = 0.01
ADAM_STEP = 10

WEIGHTS = ['norm_mix_g', 'w_in', 'q_a_norm_g', 'w_q_b', 'kv_a_norm_g', 'w_kv_b', 'q_norm_g', 'k_norm_g', 'w_o_mla',
           'ssm_lambda_re', 'ssm_lambda_im', 'ssm_log_dt', 'ssm_b_re', 'ssm_b_im', 'ssm_c_re', 'ssm_c_im', 'ssm_d',
           'w_glu', 'b_glu', 'w_o_ssm', 'mem_norm_g', 'w_mem_kv', 'xq_norm_g', 'xk_norm_g', 'w_o_cross', 'b_gate',
           'w_out', 'norm_ffn_g', 'w_up', 'conv_w', 'conv_b', 'w_down']
SHARD_AXIS = {'w_in': 2, 'w_q_b': 2, 'w_kv_b': 2, 'w_o_mla': 2, 'w_glu': 1, 'w_o_ssm': 2, 'w_mem_kv': 1,
              'w_o_cross': 2, 'w_out': 1, 'w_up': 2, 'conv_w': 2, 'w_down': 1}
SHARDED = [n for n in WEIGHTS if n in SHARD_AXIS]
GATHER_BF16 = [n for n in SHARDED if n != 'conv_w']
SMALL = [n for n in WEIGHTS if n not in SHARD_AXIS]


def _bf(v):
    return v.astype(BF16)


def _mm(a, b):
    return jnp.dot(_bf(a), _bf(b), preferred_element_type=F32)


def _mm_nt(a, b):
    return lax.dot_general(_bf(a), _bf(b), (((1,), (1,)), ((), ())), preferred_element_type=F32)


def _mm_tn(a, b):
    return lax.dot_general(_bf(a), _bf(b), (((0,), (0,)), ((), ())), preferred_element_type=F32)


def _rms(v, g, n):
    ms = jnp.sum(v * v, axis=-1, keepdims=True) * (1.0 / n)
    return (v * lax.rsqrt(ms + EPS)) * g


def _head_rms(v, g, heads, n):
    return jnp.concatenate([_rms(v[:, h * LANE:(h + 1) * LANE], g, n) for h in range(heads)], axis=-1)


def _rope(v, c, s1, s2):
    return v * c + pltpu.roll(v, LANE - 16, 1) * s1 + pltpu.roll(v, 16, 1) * s2


def _rope_t(g, c, s1, s2):
    return g * c + pltpu.roll(g * s1, 16, 1) + pltpu.roll(g * s2, LANE - 16, 1)


def _heads(fn, v, heads, *tabs):
    return jnp.concatenate([fn(v[:, h * LANE:(h + 1) * LANE], *tabs) for h in range(heads)], axis=-1)


def _gelu(y):
    return y * (0.5 * (1.0 + jnp.tanh(math.sqrt(2.0 / math.pi) * (y + 0.044715 * (y * y * y)))))


def _silu(g):
    return g * jax.nn.sigmoid(g)


def _colsum(v):
    return jnp.sum(v, axis=0, keepdims=True)


def _row_select(rows, n):
    rid = lax.broadcasted_iota(jnp.int32, (n, rows[0].shape[-1]), 0)
    out = jnp.zeros((n, rows[0].shape[-1]), F32)
    for k, r in enumerate(rows):
        out = jnp.where(rid == k, jnp.broadcast_to(r, out.shape), out)
    return out


def _params(sem, vmem_mb):
    return pltpu.CompilerParams(dimension_semantics=sem, vmem_limit_bytes=vmem_mb * 1024 * 1024)


def _rt(tm, w, cb=0):
    return pl.BlockSpec((tm, w), lambda i: (i, cb))


def _full(shape):
    nd = len(shape)
    return pl.BlockSpec(tuple(shape), lambda i: (0,) * nd)


def _rows(fn, *, name, n, ins, in_specs, outs, out_specs, n_acc=0, vmem=48):
    n_in = len(ins)
    n_out = len(outs)

    def body(*refs):
        i = pl.program_id(0)
        res = fn(i, *[r[...] for r in refs[:n_in]])
        if not isinstance(res, (tuple, list)):
            res = (res,)
        assert len(res) == n_out, (name, len(res), n_out)
        for k, (r, v) in enumerate(zip(refs[n_in:], res)):
            if k < n_out - n_acc:
                r[...] = v.astype(r.dtype)
            else:
                @pl.when(i == 0)
                def _():
                    r[...] = v

                @pl.when(i > 0)
                def _():
                    r[...] += v

    return pl.pallas_call(
        body, name=name, grid=(n,), in_specs=list(in_specs), out_specs=tuple(out_specs), out_shape=tuple(outs),
        compiler_params=_params(("arbitrary",), vmem))(*ins)


def _sds(shape, dtype=F32):
    return jax.ShapeDtypeStruct(tuple(shape), dtype)


def _tile_n(n, cap=1536):
    best = None
    for t in range(LANE, min(n, cap) + 1, LANE):
        if n % t == 0:
            best = t
    if best is None or n <= 1408:
        return n
    return best


def _matmul(name, pairs, m, n, *, nt=False, rms_gain=None, resid=None, out_dtype=F32, tm=1024, vmem=56):
    tm = min(tm, m)
    tn = _tile_n(n)
    ks = [a.shape[1] for a, _ in pairs]
    np_ = len(pairs)

    def body(*refs):
        a_refs = refs[:np_]
        b_refs = refs[np_:2 * np_]
        k = 2 * np_
        g_ref = None
        r_ref = None
        if rms_gain is not None:
            g_ref = refs[k]
            k += 1
        if resid is not None:
            r_ref = refs[k]
            k += 1
        o_ref = refs[k]
        scr = refs[k + 1:]
        j = pl.program_id(1)

        @pl.when(j == 0)
        def _():
            for p in range(np_):
                a = a_refs[p][...]
                if p == 0 and g_ref is not None:
                    a = _rms(a.astype(F32), g_ref[...], ks[0])
                scr[p][...] = a.astype(BF16)

        acc = None
        for p in range(np_):
            b = b_refs[p][...].astype(BF16)
            if nt:
                t = lax.dot_general(scr[p][...], b, (((1,), (1,)), ((), ())), preferred_element_type=F32)
            else:
                t = jnp.dot(scr[p][...], b, preferred_element_type=F32)
            acc = t if acc is None else acc + t
        if r_ref is not None:
            acc = acc + r_ref[...]
        o_ref[...] = acc.astype(o_ref.dtype)

    in_specs = [pl.BlockSpec((tm, kk), lambda i, j: (i, 0)) for kk in ks]
    if nt:
        in_specs += [pl.BlockSpec((tn, kk), lambda i, j: (j, 0)) for kk in ks]
    else:
        in_specs += [pl.BlockSpec((kk, tn), lambda i, j: (0, j)) for kk in ks]
    ins = [a for a, _ in pairs] + [b for _, b in pairs]
    if rms_gain is not None:
        in_specs.append(pl.BlockSpec((1, ks[0]), lambda i, j: (0, 0)))
        ins.append(rms_gain)
    if resid is not None:
        in_specs.append(pl.BlockSpec((tm, tn), lambda i, j: (i, j)))
        ins.append(resid)
    return pl.pallas_call(
        body, name=name, grid=(m // tm, n // tn), in_specs=in_specs,
        out_specs=pl.BlockSpec((tm, tn), lambda i, j: (i, j)), out_shape=_sds((m, n), out_dtype),
        scratch_shapes=[pltpu.VMEM((tm, kk), BF16) for kk in ks],
        compiler_params=_params(("arbitrary", "arbitrary"), vmem))(*ins)


def _matmul_tn(name, a, b, *, rms_gain=None, tl=1024, vmem=56):
    l, ka = a.shape
    n = b.shape[1]
    tl = min(tl, l)
    tn = _tile_n(n, 1536)

    def body(*refs):
        if rms_gain is not None:
            a_ref, b_ref, g_ref, o_ref = refs
        else:
            a_ref, b_ref, o_ref = refs
        t = pl.program_id(1)
        av = a_ref[...]
        if rms_gain is not None:
            av = _rms(av.astype(F32), g_ref[...], ka)
        v = _mm_tn(av, b_ref[...])

        @pl.when(t == 0)
        def _():
            o_ref[...] = v

        @pl.when(t > 0)
        def _():
            o_ref[...] += v

    in_specs = [pl.BlockSpec((tl, ka), lambda j, t: (t, 0)), pl.BlockSpec((tl, tn), lambda j, t: (t, j))]
    ins = [a, b]
    if rms_gain is not None:
        in_specs.append(pl.BlockSpec((1, ka), lambda j, t: (0, 0)))
        ins.append(rms_gain)
    return pl.pallas_call(
        body, name=name, grid=(n // tn, l // tl), in_specs=in_specs,
        out_specs=pl.BlockSpec((ka, tn), lambda j, t: (0, j)), out_shape=_sds((ka, n)),
        compiler_params=_params(("arbitrary", "arbitrary"), vmem))(*ins)


ATT_HEADS_PER_STEP = 2
ATT_W = ATT_HEADS_PER_STEP * LANE
ATT_GROUPS = MLA_HEADS // ATT_HEADS_PER_STEP
LOG2E = math.log2(math.e)
ATT_FWD_TILE = 1024
ATT_BWD_TILE = 1024
ATT_BWD_HEADS = 1
ATT_SCALE = D_QK ** -0.5
ATT_QSCALE = ATT_SCALE * LOG2E


def _tri_tables(nq, by_k):
    qs, ks = [], []
    if by_k:
        for ki in range(nq):
            for qi in range(ki, nq):
                qs.append(qi)
                ks.append(ki)
    else:
        for qi in range(nq):
            for ki in range(qi + 1):
                qs.append(qi)
                ks.append(ki)
    return jnp.asarray(np.array(qs, np.int32)), jnp.asarray(np.array(ks, np.int32))


def _causal_keep(shape, transposed):
    r = lax.broadcasted_iota(jnp.int32, shape, 0)
    c = lax.broadcasted_iota(jnp.int32, shape, 1)
    return (r <= c) if transposed else (c <= r)


def _nt16(a, b):
    return lax.dot_general(a, b, (((1,), (1,)), ((), ())), preferred_element_type=F32)


def _row_form(col):
    return jnp.transpose(jnp.broadcast_to(col, (col.shape[0], LANE)))[:8]


def _att_call(body, name, l, tq, tabs, ins, in_specs, outs, out_specs, scratch=(), groups=ATT_GROUPS, vmem=48):
    grid_spec = pltpu.PrefetchScalarGridSpec(
        num_scalar_prefetch=2, grid=(groups, tabs[0].shape[0]), in_specs=in_specs, out_specs=out_specs,
        scratch_shapes=list(scratch))
    return pl.pallas_call(body, name=name, grid_spec=grid_spec, out_shape=outs,
                          compiler_params=_params(("arbitrary", "arbitrary"), vmem))(*tabs, *ins)


def _flash_fwd(name, q, k, v_t):
    l = q.shape[0]
    tq = min(ATT_FWD_TILE, l)
    nq = l // tq
    tabs = _tri_tables(nq, by_k=False)

    def body(qt, kt, q_ref, k_ref, vt_ref, o_ref, lset_ref, m_s, acc_s):
        t = pl.program_id(1)
        qi = qt[t]
        ki = kt[t]
        sls = [slice(h * LANE, (h + 1) * LANE) for h in range(ATT_HEADS_PER_STEP)]

        @pl.when(ki == 0)
        def _():
            m_s[...] = jnp.full(m_s.shape, NEG, F32)
            acc_s[...] = jnp.zeros(acc_s.shape, F32)

        def step(masked):
            sts = [_nt16(k_ref[:, sl], q_ref[:, sl]) for sl in sls]
            for h, sl in enumerate(sls):
                st = sts[h]
                if masked:
                    st = jnp.where(_causal_keep(st.shape, True), st, NEG)
                m_old = m_s[h][:1]
                m_new = jnp.maximum(m_old, jnp.max(st, axis=0, keepdims=True))
                alpha = jnp.exp2(m_old - m_new)
                pt = jnp.exp2(st - m_new).astype(BF16)
                acc_s[sl, :] = alpha * acc_s[sl, :] + jnp.dot(vt_ref[sl, :], pt, preferred_element_type=F32)
                m_s[h] = jnp.broadcast_to(m_new, (8, tq))

        @pl.when(ki < qi)
        def _():
            step(False)

        @pl.when(ki == qi)
        def _():
            step(True)
            row = lax.broadcasted_iota(jnp.int32, (LANE, tq), 0)
            for h, sl in enumerate(sls):
                acc = acc_s[sl, :]
                lsum = acc[D_V:D_V + 1, :]
                o_ref[:, sl] = jnp.transpose(jnp.where(row < D_V, acc / lsum, 0.0))
                lset_ref[h * 8:(h + 1) * 8, :] = m_s[h] + jnp.log2(lsum)

    qspec = pl.BlockSpec((tq, ATT_W), lambda g, t, qt, kt: (qt[t], g))
    kspec = pl.BlockSpec((tq, ATT_W), lambda g, t, qt, kt: (kt[t], g))
    vspec = pl.BlockSpec((ATT_W, tq), lambda g, t, qt, kt: (g, kt[t]))
    rspec = pl.BlockSpec((8 * ATT_HEADS_PER_STEP, tq), lambda g, t, qt, kt: (g, qt[t]))
    return _att_call(
        body, name, l, tq, tabs, [q, k, v_t], [qspec, kspec, vspec],
        (_sds((l, MLA_PAD)), _sds((8 * MLA_HEADS, l))), (qspec, rspec),
        scratch=[pltpu.VMEM((ATT_HEADS_PER_STEP, 8, tq), F32), pltpu.VMEM((ATT_W, tq), F32)])


def _flash_bwd(name, q, k, v, k_t, o, lse_t, do):
    l = q.shape[0]
    tq = min(ATT_BWD_TILE, l)
    nq = l // tq
    hb = ATT_BWD_HEADS
    wb = hb * LANE

    def delta_fn(i, dov, ov):
        rows = []
        for h in range(MLA_HEADS):
            sl = slice(h * LANE, (h + 1) * LANE)
            rows.append(_row_form(jnp.sum(dov[:, sl] * ov[:, sl], axis=-1, keepdims=True)))
        return jnp.concatenate(rows, axis=0), dov

    delta_t, do16 = _rows(
        delta_fn, name=name + "_delta", n=nq, ins=[do, o], in_specs=[_rt(tq, MLA_PAD)] * 2,
        outs=[_sds((8 * MLA_HEADS, l)), _sds((l, MLA_PAD), BF16)],
        out_specs=[pl.BlockSpec((8 * MLA_HEADS, tq), lambda i: (0, i)), _rt(tq, MLA_PAD)])

    def body(qt, kt, q_ref, k_ref, v_ref, do_ref, kt_ref, lset_ref, dlt_ref, dk_ref, dv_ref, dqt_ref):
        t = pl.program_id(1)
        qi = qt[t]
        ki = kt[t]
        sls = [slice(h * LANE, (h + 1) * LANE) for h in range(hb)]

        @pl.when(ki == 0)
        def _():
            dqt_ref[qi] = jnp.zeros((wb, tq), F32)

        def step(masked):
            sts = [_nt16(k_ref[:, sl], q_ref[:, sl]) for sl in sls]
            dpts = [_nt16(v_ref[:, sl], do_ref[:, sl]) for sl in sls]
            for h, sl in enumerate(sls):
                st = sts[h]
                if masked:
                    st = jnp.where(_causal_keep(st.shape, True), st, NEG)
                pt = jnp.exp2(st - lset_ref[h * 8:(h + 1) * 8, :][:1])
                dst = (pt * (dpts[h] - dlt_ref[h * 8:(h + 1) * 8, :][:1])).astype(BF16)
                dv_ref[:, sl] += jnp.dot(pt.astype(BF16), do_ref[:, sl], preferred_element_type=F32)
                dk_ref[:, sl] += jnp.dot(dst, q_ref[:, sl], preferred_element_type=F32)
                dqt_ref[qi, sl, :] += jnp.dot(kt_ref[sl, :], dst, preferred_element_type=F32)

        @pl.when(qi == ki)
        def _():
            dk_ref[...] = jnp.zeros(dk_ref.shape, F32)
            dv_ref[...] = jnp.zeros(dv_ref.shape, F32)
            step(True)
            dqt_ref[qi] = dqt_ref[qi] * ATT_SCALE

        @pl.when(qi > ki)
        def _():
            step(False)

        @pl.when(qi == nq - 1)
        def _():
            dk_ref[...] = dk_ref[...] * (1.0 / LOG2E)

    tabs_k = _tri_tables(nq, by_k=True)
    qspec = pl.BlockSpec((tq, wb), lambda g, t, qt, kt: (qt[t], g))
    kspec = pl.BlockSpec((tq, wb), lambda g, t, qt, kt: (kt[t], g))
    ktspec = pl.BlockSpec((wb, tq), lambda g, t, qt, kt: (g, kt[t]))
    rspec = pl.BlockSpec((8 * hb, tq), lambda g, t, qt, kt: (g, qt[t]))
    dqspec = pl.BlockSpec((nq, wb, tq), lambda g, t, qt, kt: (0, g, 0))
    dk, dv, dq_t = _att_call(body, name + "_dqkv", l, tq, tabs_k, [q, k, v, do16, k_t, lse_t, delta_t],
                             [qspec, kspec, kspec, qspec, ktspec, rspec, rspec],
                             (_sds((l, MLA_PAD)), _sds((l, MLA_PAD)), _sds((nq, MLA_PAD, tq))), (kspec, kspec, dqspec),
                             groups=MLA_HEADS // hb, vmem=56)
    return dq_t, dk, dv


def _cmul(ar, ai, br, bi):
    return ar * br - ai * bi, ar * bi + ai * br


def _scan(name, x_re, x_im, a_re, a_im, reverse):
    l, lanes = x_re.shape
    ns = SCAN_SEGS
    tl = l // ns
    steps = int(math.log2(tl))
    assert 2 ** steps == tl and tl * ns == l

    def body(xr_ref, xi_ref, ar_ref, ai_ref, sr_ref, si_ref):
        a_r1 = ar_ref[...]
        a_i1 = ai_ref[...]
        a_r = jnp.broadcast_to(a_r1, (ns, LANE))
        a_i = jnp.broadcast_to(a_i1, (ns, LANE))

        def rows(t):
            t = (tl - 1 - t) if reverse else t
            return pl.ds(pl.multiple_of(t * ns, ns), ns)

        def local(t, carry):
            cr, ci = carry
            r = rows(t)
            pr, pi = _cmul(a_r, a_i, cr, ci)
            return pr + xr_ref[r, :].astype(F32), pi + xi_ref[r, :].astype(F32)

        zero = jnp.zeros((ns, LANE), F32)
        e_r, e_i = lax.fori_loop(0, tl, local, (zero, zero), unroll=min(8, tl))
        p_r, p_i = a_r1, a_i1
        for _ in range(steps):
            p_r, p_i = _cmul(p_r, p_i, p_r, p_i)
        rid = lax.broadcasted_iota(jnp.int32, (ns, LANE), 0)
        c_r = jnp.zeros((1, LANE), F32)
        c_i = jnp.zeros((1, LANE), F32)
        in_r, in_i = zero, zero
        order = range(ns - 2, -1, -1) if reverse else range(1, ns)
        for kk in order:
            src = kk + 1 if reverse else kk - 1
            ek_r = jnp.sum(jnp.where(rid == src, e_r, 0.0), axis=0, keepdims=True)
            ek_i = jnp.sum(jnp.where(rid == src, e_i, 0.0), axis=0, keepdims=True)
            q_r, q_i = _cmul(p_r, p_i, c_r, c_i)
            c_r, c_i = q_r + ek_r, q_i + ek_i
            in_r = jnp.where(rid == kk, jnp.broadcast_to(c_r, (ns, LANE)), in_r)
            in_i = jnp.where(rid == kk, jnp.broadcast_to(c_i, (ns, LANE)), in_i)

        def final(t, carry):
            cr, ci = carry
            r = rows(t)
            pr, pi = _cmul(a_r, a_i, cr, ci)
            nr, ni = pr + xr_ref[r, :].astype(F32), pi + xi_ref[r, :].astype(F32)
            sr_ref[r, :] = nr.astype(sr_ref.dtype)
            si_ref[r, :] = ni.astype(si_ref.dtype)
            return nr, ni

        lax.fori_loop(0, tl, final, (in_r, in_i), unroll=min(8, tl))

    xs = pl.BlockSpec((l, LANE), lambda j: (0, j))
    as_ = pl.BlockSpec((1, LANE), lambda j: (0, j))
    return pl.pallas_call(
        body, name=name, grid=(lanes // LANE,), in_specs=[xs, xs, as_, as_], out_specs=(xs, xs),
        out_shape=(_sds((l, lanes), x_re.dtype), _sds((l, lanes), x_re.dtype)),
        compiler_params=_params(("arbitrary",), 48))(x_re, x_im, a_re, a_im)


ANY = pl.BlockSpec(memory_space=pl.ANY)


def _place():
    mx, my, mc = lax.axis_index("x"), lax.axis_index("y"), lax.axis_index("c")
    return mx, my, mc, [(1 - mx, my), (mx, 1 - my), (1 - mx, 1 - my)]


def _run_copies(copies):
    for cp in copies:
        cp.start()
    for cp in copies:
        cp.wait_recv()
    for cp in copies:
        cp.wait_send()


def _remote(src, dst, sems, k, dev):
    return pltpu.make_async_remote_copy(src_ref=src, dst_ref=dst, send_sem=sems[0].at[k], recv_sem=sems[1].at[k],
                                        device_id=dev, device_id_type=MESH)


def _copy_call(body, name, ins, outs, n_copies, aliases=None):
    return pl.pallas_call(
        body, name=name, in_specs=[ANY] * len(ins), out_specs=[ANY] * len(outs), out_shape=list(outs),
        input_output_aliases=aliases or {},
        scratch_shapes=[pltpu.SemaphoreType.DMA((n_copies,)), pltpu.SemaphoreType.DMA((n_copies,))])(*ins)


def _gather_ici(xs):
    n = len(xs)

    def body(*refs):
        x_refs, y_refs, sems = refs[:n], refs[n:2 * n], refs[2 * n:]
        mx, my, mc, peers = _place()
        me = 2 * mx + my
        _run_copies([_remote(x_refs[i].at[mc], y_refs[i].at[me, mc], sems, 3 * i + j, (px, py, mc))
                     for i in range(n) for j, (px, py) in enumerate(peers)])

    return _copy_call(body, "comm_gather_ici", xs, [_sds((4,) + x.shape, x.dtype) for x in xs], 3 * n)


def _gather_d2d(ys):
    n = len(ys)

    def body(*refs):
        y_in, y_out, sems = refs[:n], refs[n:2 * n], refs[2 * n:]
        mx, my, mc, peers = _place()
        _run_copies([_remote(y_in[i].at[2 * px + py, mc], y_out[i].at[2 * px + py, mc], sems, 3 * i + j, (mx, my, 1 - mc))
                     for i in range(n) for j, (px, py) in enumerate(peers)])

    return _copy_call(body, "comm_gather_d2d", ys, [_sds(y.shape, y.dtype) for y in ys], 3 * n,
                      aliases={i: i for i in range(n)})


def _reduce_ici(gs):
    n = len(gs)

    def body(*refs):
        g_refs, y_refs, sems = refs[:n], refs[n:2 * n], refs[2 * n:]
        mx, my, mc, peers = _place()
        _run_copies([_remote(g_refs[i].at[2 * px + py], y_refs[i].at[j], sems, 3 * i + j, (px, py, mc))
                     for i in range(n) for j, (px, py) in enumerate(peers)])

    return _copy_call(body, "comm_reduce_ici", gs, [_sds((3,) + g.shape[1:], g.dtype) for g in gs], 3 * n)


def _swap_d2d(name, ps, other_layer):
    n = len(ps)

    def body(*refs):
        p_refs, o_refs, sems = refs[:n], refs[n:2 * n], refs[2 * n:]
        mx, my, mc, _ = _place()
        _run_copies([_remote(p_refs[i].at[1 - mc] if other_layer else p_refs[i], o_refs[i], sems, i, (mx, my, 1 - mc))
                     for i in range(n)])

    outs = [_sds(p.shape[1:] if other_layer else p.shape, p.dtype) for p in ps]
    return _copy_call(body, name, ps, outs, n)


def _all_exchange(name, src):
    def body(x_ref, y_ref, send_sems, recv_sems, local_sem):
        mx, my, mc = lax.axis_index("x"), lax.axis_index("y"), lax.axis_index("c")
        me = 4 * mx + 2 * my + mc
        own = pltpu.make_async_copy(x_ref, y_ref.at[me], local_sem)
        own.start()
        copies = []
        for j in range(1, 8):
            px = (1 - mx) if (j & 4) else mx
            py = (1 - my) if (j & 2) else my
            pc = (1 - mc) if (j & 1) else mc
            cp = pltpu.make_async_remote_copy(
                src_ref=x_ref, dst_ref=y_ref.at[me], send_sem=send_sems.at[j - 1], recv_sem=recv_sems.at[j - 1],
                device_id=(px, py, pc), device_id_type=MESH)
            cp.start()
            copies.append(cp)
        for cp in copies:
            cp.wait_recv()
        for cp in copies:
            cp.wait_send()
        own.wait()

    return pl.pallas_call(
        body, name=name, in_specs=[ANY], out_specs=ANY, out_shape=_sds((8,) + src.shape, src.dtype),
        scratch_shapes=[pltpu.SemaphoreType.DMA((7,)), pltpu.SemaphoreType.DMA((7,)), pltpu.SemaphoreType.DMA])(src)


PACK_W = 1024


def _pack(arrs, rows_multiple, dtype):
    flat = jnp.concatenate([a.reshape(-1).astype(dtype) for a in arrs])
    n = flat.shape[0]
    unit = PACK_W * rows_multiple
    tot = -(-n // unit) * unit
    flat = jnp.pad(flat, (0, tot - n))
    return flat.reshape(tot // PACK_W, PACK_W)


def _unpack(flat, shapes):
    flat = flat.reshape(-1)
    out = []
    off = 0
    for s in shapes:
        n = int(np.prod(s))
        out.append(flat[off:off + n].reshape(s))
        off += n
    return out


def _rope_tables(pos):
    l = pos.shape[0]
    tm = min(512, l)
    inv = (np.float32(ROPE_THETA) ** (-np.arange(0, D_ROPE, 2, dtype=np.float32) / np.float32(D_ROPE))).astype(np.float32)
    lane_f = np.zeros((1, LANE), np.float32)
    lane_f[0, D_NOPE:D_NOPE + 16] = inv
    lane_f[0, D_NOPE + 16:D_NOPE + 32] = inv

    def fn(i, p, f):
        ang = p * f
        lane = lax.broadcasted_iota(jnp.int32, ang.shape, 1)
        co = jnp.cos(ang)
        si = jnp.sin(ang)
        c = jnp.where(lane < D_NOPE, 1.0, jnp.where(lane < D_QK, co, 0.0))
        s1 = jnp.where((lane >= D_NOPE) & (lane < D_NOPE + 16), -si, 0.0)
        s2 = jnp.where((lane >= D_NOPE + 16) & (lane < D_QK), si, 0.0)
        return c, s1, s2

    return _rows(fn, name="rope_tables", n=l // tm, ins=[pos, jnp.asarray(lane_f)],
                 in_specs=[_rt(tm, 1), _full((1, LANE))], outs=[_sds((l, LANE))] * 3, out_specs=[_rt(tm, LANE)] * 3)


def _ssm_param_fn(lr, li, log_dt, br, bi):
    dt = jnp.exp(log_dt)
    mag = jnp.exp(lr * dt)
    a_re = mag * jnp.cos(li * dt)
    a_im = mag * jnp.sin(li * dt)
    den = lr * lr + li * li
    e_re = a_re - 1.0
    e_im = a_im
    f_re = (e_re * lr + e_im * li) / den
    f_im = (e_im * lr - e_re * li) / den
    bb_re = f_re[None] * br - f_im[None] * bi
    bb_im = f_re[None] * bi + f_im[None] * br
    return a_re, a_im, bb_re, bb_im


def _ssm_params(name, lr, li, log_dt, br, bi):
    g, n = lr.shape
    c = br.shape[0]
    return _rows(lambda i, *v: _ssm_param_fn(*v), name=name, n=1, ins=[lr, li, log_dt, br, bi],
                 in_specs=[_full((g, n)), _full((g, n)), _full((g, 1)), _full((c, g, n)), _full((c, g, n))],
                 outs=[_sds((g, n)), _sds((g, n)), _sds((c, g, n)), _sds((c, g, n))],
                 out_specs=[_full((g, n)), _full((g, n)), _full((c, g, n)), _full((c, g, n))])


def _ssm_params_bwd(name, lr, li, log_dt, br, bi, d_are, d_aim, d_bbre, d_bbim):
    g, n = lr.shape
    c = br.shape[0]

    def fn(i, lr, li, log_dt, br, bi, g0, g1, g2, g3):
        _, vjp = jax.vjp(_ssm_param_fn, lr, li, log_dt, br, bi)
        return vjp((g0, g1, g2, g3))

    sp = [_full((g, n)), _full((g, n)), _full((g, 1)), _full((c, g, n)), _full((c, g, n))]
    return _rows(fn, name=name, n=1, ins=[lr, li, log_dt, br, bi, d_are, d_aim, d_bbre, d_bbim],
                 in_specs=sp + [_full((g, n)), _full((g, n)), _full((c, g, n)), _full((c, g, n))],
                 outs=[_sds((g, n)), _sds((g, n)), _sds((g, 1)), _sds((c, g, n)), _sds((c, g, n))], out_specs=sp)


_EYE8 = np.eye(8, dtype=np.float32)


def _blockdiag(v):
    j, g, p, q = v.shape
    m = v[:, :, :, None, :] * jnp.asarray(_EYE8)[None, :, None, :, None]
    return m.reshape(j, g * p, g * q)


def _blockdiag_t(m, p, q):
    j = m.shape[0]
    m = m.reshape(j, 8, p, 8, q)
    return jnp.sum(m * jnp.asarray(_EYE8)[None, :, None, :, None], axis=3)


def _to_perm(v, l):
    ns = SCAN_SEGS
    return v.reshape(ns, l // ns, v.shape[-1]).transpose(1, 0, 2).reshape(l, v.shape[-1])


def _from_perm(v, l):
    ns = SCAN_SEGS
    return v.reshape(l // ns, ns, v.shape[-1]).transpose(1, 0, 2).reshape(l, v.shape[-1])


def _prep_layer(w, i):
    p = {}
    w_in = w['w_in'][i]
    z = lambda n: jnp.zeros((D_MODEL, n), w_in.dtype)
    o = Q_LORA + KV_LORA
    p['w_s'] = jnp.concatenate([w_in[:, :o], z(D_NOPE), w_in[:, o:o + D_ROPE], z(HEAD_PAD - D_QK)], axis=1)
    o += D_ROPE
    p['w_u'] = w_in[:, o:o + SSM_WIDTH]
    o += SSM_WIDTH
    p['w_xq'] = w_in[:, o:o + X_WIDTH]
    o += X_WIDTH
    p['w_g'] = w_in[:, o:]
    wq = w['w_q_b'][i].reshape(Q_LORA, MLA_HEADS, D_QK)
    p['w_qb'] = jnp.pad(wq, ((0, 0), (0, 0), (0, HEAD_PAD - D_QK))).reshape(Q_LORA, MLA_PAD)
    wkv = w['w_kv_b'][i].reshape(KV_LORA, MLA_HEADS, D_NOPE + D_V)
    p['w_k'] = jnp.pad(wkv[:, :, :D_NOPE], ((0, 0), (0, 0), (0, HEAD_PAD - D_NOPE))).reshape(KV_LORA, MLA_PAD)
    p['w_v'] = jnp.pad(wkv[:, :, D_NOPE:], ((0, 0), (0, 0), (0, HEAD_PAD - D_V))).reshape(KV_LORA, MLA_PAD)
    wo = w['w_o_mla'][i].reshape(MLA_HEADS, D_V, D_MODEL)
    p['w_oa'] = jnp.pad(wo, ((0, 0), (0, HEAD_PAD - D_V), (0, 0))).reshape(MLA_PAD, D_MODEL)
    for n in ('w_glu', 'w_o_ssm', 'w_mem_kv', 'w_o_cross', 'w_out', 'w_up', 'w_down'):
        p[n] = w[n][i]
    p['conv_w'] = w['conv_w'][i]
    for n in ('norm_mix_g', 'q_a_norm_g', 'kv_a_norm_g', 'b_glu', 'mem_norm_g', 'xq_norm_g', 'xk_norm_g', 'b_gate',
              'norm_ffn_g', 'conv_b'):
        p[n] = w[n][i].reshape(1, -1)
    p['q_norm_g'] = jnp.pad(w['q_norm_g'][i], (0, HEAD_PAD - D_QK)).reshape(1, HEAD_PAD)
    p['k_norm_g'] = jnp.pad(w['k_norm_g'][i], (0, HEAD_PAD - D_QK)).reshape(1, HEAD_PAD)
    p['ssm_d'] = w['ssm_d'][i].reshape(1, SSM_WIDTH)
    p['lr'] = w['ssm_lambda_re'][i]
    p['li'] = w['ssm_lambda_im'][i]
    p['log_dt'] = w['ssm_log_dt'][i].reshape(SSM_GROUPS, 1)
    p['br'] = w['ssm_b_re'][i].transpose(2, 0, 1)
    p['bi'] = w['ssm_b_im'][i].transpose(2, 0, 1)
    cr = w['ssm_c_re'][i].reshape(SSM_JB, 8, SSM_GROUP_CH, SSM_STATE).transpose(0, 1, 3, 2)
    ci = w['ssm_c_im'][i].reshape(SSM_JB, 8, SSM_GROUP_CH, SSM_STATE).transpose(0, 1, 3, 2)
    p['c_mat'] = jnp.concatenate([_blockdiag(cr), -_blockdiag(ci)], axis=1).astype(BF16)
    return p


def _b_mat(bb_re, bb_im):
    r = bb_re.transpose(1, 0, 2).reshape(SSM_JB, 8, SSM_GROUP_CH, SSM_STATE)
    i = bb_im.transpose(1, 0, 2).reshape(SSM_JB, 8, SSM_GROUP_CH, SSM_STATE)
    return jnp.concatenate([_blockdiag(r), _blockdiag(i)], axis=2).astype(BF16)


def _qkv_fn(ps, c, s1, s2, qag, wqb, kvag, wk, wv, qng, kng):
    c_q = ps[:, :Q_LORA]
    c_kv = ps[:, Q_LORA:Q_LORA + KV_LORA]
    kr = ps[:, Q_LORA + KV_LORA:]
    cqn = _rms(c_q, qag, Q_LORA)
    ckvn = _rms(c_kv, kvag, KV_LORA)
    q_raw = _mm(cqn, wqb)
    k_raw = _mm(ckvn, wk) + jnp.concatenate([kr] * MLA_HEADS, axis=-1)
    v = _mm(ckvn, wv)
    q = _heads(_rope, _head_rms(q_raw, qng, MLA_HEADS, D_QK), MLA_HEADS, c, s1, s2)
    k = _heads(_rope, _head_rms(k_raw, kng, MLA_HEADS, D_QK), MLA_HEADS, c, s1, s2)
    lane = lax.broadcasted_iota(jnp.int32, v.shape, 1)
    v = jnp.where((lane & (LANE - 1)) == D_V, 1.0, v)
    return q * ATT_QSCALE, k, v


def _layer_fwd(name, x, tabs, mem, p):
    l = x.shape[0]
    tm = min(512, l)
    nt = l // tm
    sv = {'x0': x}
    sv['p_g'] = _matmul(name + "_in_g", [(x, p['w_g'])], l, 3 * D_MODEL, rms_gain=p['norm_mix_g'])
    sv['p_u'] = _matmul(name + "_in_u", [(x, p['w_u'])], l, SSM_WIDTH, rms_gain=p['norm_mix_g'])
    sv['p_xq'] = _matmul(name + "_in_xq", [(x, p['w_xq'])], l, X_WIDTH, rms_gain=p['norm_mix_g'])
    sv['p_s'] = _matmul(name + "_in_s", [(x, p['w_s'])], l, SMALL_W, rms_gain=p['norm_mix_g'])

    qkv_consts = [p['q_a_norm_g'], p['w_qb'], p['kv_a_norm_g'], p['w_k'], p['w_v'], p['q_norm_g'], p['k_norm_g']]
    qkv_cspecs = [_full(a.shape) for a in qkv_consts]
    def qkv_fwd(i, *a):
        qv, kv, vv = _qkv_fn(*a)
        return qv, kv, vv, jnp.transpose(kv), jnp.transpose(vv)

    q, k, v, k_t, v_t = _rows(qkv_fwd, name=name + "_qkv", n=nt, ins=[sv['p_s'], *tabs, *qkv_consts],
                              in_specs=[_rt(tm, SMALL_W)] + [_rt(tm, LANE)] * 3 + qkv_cspecs,
                              outs=[_sds((l, MLA_PAD), BF16)] * 3 + [_sds((MLA_PAD, l), BF16)] * 2,
                              out_specs=[_rt(tm, MLA_PAD)] * 3 + [pl.BlockSpec((MLA_PAD, tm), lambda i: (0, i))] * 2)
    sv['q'], sv['k'], sv['v'], sv['k_t'] = q, k, v, k_t
    sv['o_a'], sv['lse_t'] = _flash_fwd(name + "_attn", q, k, v_t)

    a_re, a_im, bb_re, bb_im = _ssm_params(name + "_ssm_par", p['lr'], p['li'], p['log_dt'], p['br'], p['bi'])
    sv['a_re'], sv['a_im'] = a_re.reshape(1, SSM_LANES), a_im.reshape(1, SSM_LANES)
    sv['b_mat'] = _b_mat(bb_re, bb_im)
    u_p = _to_perm(sv['p_u'], l)
    sv['u_p'] = u_p

    def bu_fn(i, u, bm):
        res = [_mm(u[:, j * LANE:(j + 1) * LANE], bm[j]) for j in range(SSM_JB)]
        return (jnp.concatenate([r[:, :512] for r in res], axis=-1), jnp.concatenate([r[:, 512:] for r in res], axis=-1))

    ts = min(256, l)
    bu_re, bu_im = _rows(bu_fn, name=name + "_ssm_bu", n=l // ts, ins=[u_p, sv['b_mat']],
                         in_specs=[_rt(ts, SSM_WIDTH), _full(sv['b_mat'].shape)],
                         outs=[_sds((l, SSM_LANES), SSM_STATE_DTYPE)] * 2, out_specs=[_rt(ts, SSM_LANES)] * 2)
    s_re, s_im = _scan(name + "_ssm_scan", bu_re, bu_im, sv['a_re'], sv['a_im'], reverse=False)
    sv['s_re'], sv['s_im'] = s_re, s_im

    def glu_fn(i, sr, si, u, cm, dsk, wg, bg):
        y = jnp.concatenate([_mm(jnp.concatenate([sr[:, j * 512:(j + 1) * 512], si[:, j * 512:(j + 1) * 512]], axis=-1),
                                 cm[j]) for j in range(SSM_JB)], axis=-1) + dsk * u
        zz = _gelu(y)
        return zz * jax.nn.sigmoid(_mm(zz, wg) + bg)

    glu_consts = [p['c_mat'], p['ssm_d'], p['w_glu'], p['b_glu']]
    zo_p = _rows(glu_fn, name=name + "_ssm_glu", n=l // ts, ins=[s_re, s_im, u_p, *glu_consts],
                 in_specs=[_rt(ts, SSM_LANES), _rt(ts, SSM_LANES), _rt(ts, SSM_WIDTH)] + [_full(a.shape) for a in glu_consts],
                 outs=[_sds((l, SSM_WIDTH), BF16)], out_specs=[_rt(ts, SSM_WIDTH)])[0]
    sv['zo'] = _from_perm(zo_p, l)

    m_len = mem.shape[0]

    def memkv_fn(i, mm_, mg, wmk, xkg):
        kv = _mm(_rms(mm_, mg, D_MODEL), wmk)
        return _head_rms(kv[:, :X_WIDTH], xkg, X_HEADS, X_HEAD_DIM), kv[:, X_WIDTH:]

    mem_consts = [p['mem_norm_g'], p['w_mem_kv'], p['xk_norm_g']]
    k_c, v_c = _rows(memkv_fn, name=name + "_memkv", n=1, ins=[mem, *mem_consts],
                     in_specs=[_full(mem.shape)] + [_full(a.shape) for a in mem_consts],
                     outs=[_sds((m_len, X_WIDTH))] * 2, out_specs=[_full((m_len, X_WIDTH))] * 2)
    sv['k_c'], sv['v_c'] = k_c, v_c

    def cross_fn(i, xq, kc, vc, xqg):
        outs = []
        for h in range(X_HEADS):
            sl = slice(h * LANE, (h + 1) * LANE)
            qh = _rms(xq[:, sl], xqg, X_HEAD_DIM)
            s = _mm_nt(qh, kc[:, sl]) * (X_HEAD_DIM ** -0.5)
            s = s - jnp.max(s, axis=-1, keepdims=True)
            e = jnp.exp(s)
            pr = e / jnp.sum(e, axis=-1, keepdims=True)
            outs.append(_mm(pr, vc[:, sl]))
        return jnp.concatenate(outs, axis=-1)

    sv['o_c'] = _rows(cross_fn, name=name + "_cross", n=nt, ins=[sv['p_xq'], k_c, v_c, p['xq_norm_g']],
                      in_specs=[_rt(tm, X_WIDTH), _full(k_c.shape), _full(v_c.shape), _full((1, LANE))],
                      outs=[_sds((l, X_WIDTH), BF16)], out_specs=[_rt(tm, X_WIDTH)])[0]

    def merge_fn(i, oa, zo, oc, pg, x0, woa, wos, woc, bg, wout):
        gates = jax.nn.sigmoid(pg + bg)
        merged = (gates[:, :D_MODEL] * _mm(oa, woa) + gates[:, D_MODEL:2 * D_MODEL] * _mm(zo, wos)
                  + gates[:, 2 * D_MODEL:] * _mm(oc, woc))
        return x0 + _mm(merged, wout), merged

    merge_consts = [p['w_oa'], p['w_o_ssm'], p['w_o_cross'], p['b_gate'], p['w_out']]
    tg = min(256, l)
    x1, merged = _rows(merge_fn, name=name + "_merge", n=l // tg, ins=[sv['o_a'], sv['zo'], sv['o_c'], sv['p_g'], x, *merge_consts],
                       in_specs=[_rt(tg, MLA_PAD), _rt(tg, SSM_WIDTH), _rt(tg, X_WIDTH), _rt(tg, 3 * D_MODEL), _rt(tg, D_MODEL)]
                       + [_full(a.shape) for a in merge_consts],
                       outs=[_sds((l, D_MODEL)), _sds((l, D_MODEL), BF16)], out_specs=[_rt(tg, D_MODEL)] * 2)
    sv['x1'], sv['merged'] = x1, merged

    up = _matmul(name + "_up", [(x1, p['w_up'])], l, 2 * D_FF, rms_gain=p['norm_ffn_g'])
    sv['up'] = up
    tc = min(128, l)

    def conv_fn(i, upt, halo, cw, cb):
        upc = _conv(i, upt, halo, cw) + cb
        return _silu(upc[:, :D_FF]) * upc[:, D_FF:]

    act = _rows(conv_fn, name=name + "_conv", n=l // tc, ins=[up, up, p['conv_w'], p['conv_b']],
                in_specs=[_rt(tc, 2 * D_FF), _halo_prev(tc, 2 * D_FF), _full((3, 2 * D_FF)), _full((1, 2 * D_FF))],
                outs=[_sds((l, D_FF), BF16)], out_specs=[_rt(tc, D_FF)])[0]
    sv['act'] = act
    x2 = _matmul(name + "_down", [(act, p['w_down'])], l, D_MODEL, resid=x1)
    return x2, sv


def _halo_prev(tm, w):
    return pl.BlockSpec((8, w), lambda i: (jnp.maximum(i * (tm // 8) - 1, 0), 0))


def _halo_next(tm, w, n_tiles):
    last = n_tiles * (tm // 8) - 1
    return pl.BlockSpec((8, w), lambda i: (jnp.minimum((i + 1) * (tm // 8), last), 0))


def _conv(i, tile, halo, cw):
    halo = jnp.where(i > 0, halo, 0.0)
    ext = jnp.concatenate([halo, tile], axis=0)
    n = ext.shape[0]
    x1 = pltpu.roll(ext, 1, 0)[8:]
    x2 = pltpu.roll(ext, 2, 0)[8:]
    del n
    return cw[0:1] * x2 + cw[1:2] * x1 + cw[2:3] * tile


def _layer_bwd(name, dx2, sv, tabs, mem, p):
    l = dx2.shape[0]
    tm = min(512, l)
    nt = l // tm
    g = {}
    x1 = sv['x1']
    dact = _matmul(name + "_b_down", [(dx2, p['w_down'])], l, D_FF, nt=True)
    g['w_down'] = _matmul_tn(name + "_gw_down", sv['act'], dx2)
    tc = min(128, l)
    ntc = l // tc

    def conv_b(i, upt, up_prev, up_next, da, da_next, cw, cb):
        up_prev = jnp.where(i > 0, up_prev, 0.0)
        da_next = jnp.where(i < ntc - 1, da_next, 0.0)
        ext = jnp.concatenate([up_prev, upt, up_next], axis=0)
        x0 = ext[8:]
        xm1 = pltpu.roll(ext, 1, 0)[8:]
        xm2 = pltpu.roll(ext, 2, 0)[8:]
        upc = cw[0:1] * xm2 + cw[1:2] * xm1 + cw[2:3] * x0 + cb
        _, vjp = jax.vjp(lambda a, b: _silu(a) * b, upc[:, :D_FF], upc[:, D_FF:])
        dg, dv = vjp(jnp.concatenate([da, da_next], axis=0))
        dupc = jnp.concatenate([dg, dv], axis=-1)
        n = dupc.shape[0]
        dup = cw[2:3] * dupc[:tc] + cw[1:2] * pltpu.roll(dupc, n - 1, 0)[:tc] + cw[0:1] * pltpu.roll(dupc, n - 2, 0)[:tc]
        dt = dupc[:tc]
        dcw = _row_select([_colsum(dt * xm2[:tc]), _colsum(dt * xm1[:tc]), _colsum(dt * upt)], 8)
        return dup, dcw, _colsum(dt)

    dup, g_cw, g_cb = _rows(
        conv_b, name=name + "_b_conv", n=ntc, ins=[sv['up'], sv['up'], sv['up'], dact, dact, p['conv_w'], p['conv_b']],
        in_specs=[_rt(tc, 2 * D_FF), _halo_prev(tc, 2 * D_FF), _halo_next(tc, 2 * D_FF, ntc), _rt(tc, D_FF),
                  _halo_next(tc, D_FF, ntc), _full((3, 2 * D_FF)), _full((1, 2 * D_FF))],
        outs=[_sds((l, 2 * D_FF)), _sds((8, 2 * D_FF)), _sds((1, 2 * D_FF))],
        out_specs=[_rt(tc, 2 * D_FF), _full((8, 2 * D_FF)), _full((1, 2 * D_FF))], n_acc=2, vmem=56)
    g['conv_w'] = g_cw[:3]
    g['conv_b'] = g_cb
    dh2 = _matmul(name + "_b_up", [(dup, p['w_up'])], l, D_MODEL, nt=True, tm=256)
    g['w_up'] = _matmul_tn(name + "_gw_up", x1, dup, rms_gain=p['norm_ffn_g'])

    def norm_b(i, xv, dh, dres, gn):
        _, vjp = jax.vjp(lambda a, b: _rms(a, b, D_MODEL), xv, gn)
        dxv, dgn = vjp(dh)
        return dres + dxv, dgn

    dx1, g['norm_ffn_g'] = _rows(norm_b, name=name + "_b_norm2", n=nt, ins=[x1, dh2, dx2, p['norm_ffn_g']],
                                 in_specs=[_rt(tm, D_MODEL)] * 3 + [_full((1, D_MODEL))],
                                 outs=[_sds((l, D_MODEL)), _sds((1, D_MODEL))], out_specs=[_rt(tm, D_MODEL), _full((1, D_MODEL))],
                                 n_acc=1)

    tg = min(256, l)

    def merge_b(i, dx, oa, zo, oc, pg, woa, wos, woc, bg, wout):
        dm = _mm_nt(dx, wout)
        gates = jax.nn.sigmoid(pg + bg)
        ys = [_mm(oa, woa), _mm(zo, wos), _mm(oc, woc)]
        dys, dpg = [], []
        for b in range(3):
            gb = gates[:, b * D_MODEL:(b + 1) * D_MODEL]
            dys.append(dm * gb)
            dpg.append(dm * ys[b] * gb * (1.0 - gb))
        dpg = jnp.concatenate(dpg, axis=-1)
        return (_mm_nt(dys[0], woa), _mm_nt(dys[1], wos), _mm_nt(dys[2], woc), dpg, dys[0], dys[1], dys[2], _colsum(dpg))

    merge_consts = [p['w_oa'], p['w_o_ssm'], p['w_o_cross'], p['b_gate'], p['w_out']]
    (do_a, dzo, do_c, dp_g, dy_a, dy_b, dy_c, g['b_gate']) = _rows(
        merge_b, name=name + "_b_merge", n=l // tg, ins=[dx1, sv['o_a'], sv['zo'], sv['o_c'], sv['p_g'], *merge_consts],
        in_specs=[_rt(tg, D_MODEL), _rt(tg, MLA_PAD), _rt(tg, SSM_WIDTH), _rt(tg, X_WIDTH), _rt(tg, 3 * D_MODEL)]
        + [_full(a.shape) for a in merge_consts],
        outs=[_sds((l, MLA_PAD)), _sds((l, SSM_WIDTH)), _sds((l, X_WIDTH)), _sds((l, 3 * D_MODEL)),
              _sds((l, D_MODEL), BF16), _sds((l, D_MODEL), BF16), _sds((l, D_MODEL), BF16), _sds((1, 3 * D_MODEL))],
        out_specs=[_rt(tg, MLA_PAD), _rt(tg, SSM_WIDTH), _rt(tg, X_WIDTH), _rt(tg, 3 * D_MODEL),
                   _rt(tg, D_MODEL), _rt(tg, D_MODEL), _rt(tg, D_MODEL), _full((1, 3 * D_MODEL))], n_acc=1, vmem=56)
    g['w_out'] = _matmul_tn(name + "_gw_out", sv['merged'], dx1)
    g['w_oa'] = _matmul_tn(name + "_gw_oa", sv['o_a'], dy_a)
    g['w_o_ssm'] = _matmul_tn(name + "_gw_os", sv['zo'], dy_b)
    g['w_o_cross'] = _matmul_tn(name + "_gw_oc", sv['o_c'], dy_c)

    k_c, v_c = sv['k_c'], sv['v_c']
    m_len = k_c.shape[0]

    def cross_b(i, xq, do, kc, vc, xqg):
        dxq, dk, dv = [], [], []
        dg = jnp.zeros((1, LANE), F32)
        for h in range(X_HEADS):
            sl = slice(h * LANE, (h + 1) * LANE)
            qh, vjp = jax.vjp(lambda a, b: _rms(a, b, X_HEAD_DIM), xq[:, sl], xqg)
            sc = X_HEAD_DIM ** -0.5
            s = _mm_nt(qh, kc[:, sl]) * sc
            s = s - jnp.max(s, axis=-1, keepdims=True)
            e = jnp.exp(s)
            pr = e / jnp.sum(e, axis=-1, keepdims=True)
            doh = do[:, sl]
            dv.append(_mm_tn(pr, doh))
            dp = _mm_nt(doh, vc[:, sl])
            ds = pr * (dp - jnp.sum(dp * pr, axis=-1, keepdims=True)) * sc
            dk.append(_mm_tn(ds, qh))
            dxh, dgh = vjp(_mm(ds, kc[:, sl]))
            dxq.append(dxh)
            dg = dg + dgh
        return jnp.concatenate(dxq, axis=-1), jnp.concatenate(dk, axis=-1), jnp.concatenate(dv, axis=-1), dg

    dp_xq, dk_c, dv_c, g['xq_norm_g'] = _rows(
        cross_b, name=name + "_b_cross", n=nt, ins=[sv['p_xq'], do_c, k_c, v_c, p['xq_norm_g']],
        in_specs=[_rt(tm, X_WIDTH), _rt(tm, X_WIDTH), _full(k_c.shape), _full(v_c.shape), _full((1, LANE))],
        outs=[_sds((l, X_WIDTH)), _sds((m_len, X_WIDTH)), _sds((m_len, X_WIDTH)), _sds((1, LANE))],
        out_specs=[_rt(tm, X_WIDTH), _full((m_len, X_WIDTH)), _full((m_len, X_WIDTH)), _full((1, LANE))], n_acc=3)

    def memkv_b(i, mm_, dk, dv, mg, wmk, xkg):
        memn, vjp_n = jax.vjp(lambda a, b: _rms(a, b, D_MODEL), mm_, mg)
        kv = _mm(memn, wmk)
        _, vjp_k = jax.vjp(lambda a, b: _head_rms(a, b, X_HEADS, X_HEAD_DIM), kv[:, :X_WIDTH], xkg)
        dkr, dxkg = vjp_k(dk)
        dkv = jnp.concatenate([dkr, dv], axis=-1)
        _, dmg = vjp_n(_mm_nt(dkv, wmk))
        return _mm_tn(memn, dkv), dmg, dxkg

    mem_consts = [p['mem_norm_g'], p['w_mem_kv'], p['xk_norm_g']]
    g['w_mem_kv'], g['mem_norm_g'], g['xk_norm_g'] = _rows(
        memkv_b, name=name + "_b_memkv", n=1, ins=[mem, dk_c, dv_c, *mem_consts],
        in_specs=[_full(mem.shape), _full(dk_c.shape), _full(dv_c.shape)] + [_full(a.shape) for a in mem_consts],
        outs=[_sds((D_MODEL, 2 * X_WIDTH)), _sds((1, D_MODEL)), _sds((1, LANE))],
        out_specs=[_full((D_MODEL, 2 * X_WIDTH)), _full((1, D_MODEL)), _full((1, LANE))])

    u_p = sv['u_p']
    dzo_p = _to_perm(dzo, l)
    s_re, s_im = sv['s_re'], sv['s_im']

    def glu_b(i, sr, si, u, dz, cm, dsk, wg, bg):
        cats = [jnp.concatenate([sr[:, j * 512:(j + 1) * 512], si[:, j * 512:(j + 1) * 512]], axis=-1) for j in range(SSM_JB)]
        y = jnp.concatenate([_mm(cats[j], cm[j]) for j in range(SSM_JB)], axis=-1) + dsk * u
        zz, vjp_g = jax.vjp(_gelu, y)
        t = _mm(zz, wg) + bg
        sg = jax.nn.sigmoid(t)
        dt = dz * zz * sg * (1.0 - sg)
        dzz = dz * sg + _mm_nt(dt, wg)
        dy = vjp_g(dzz)[0]
        dss = [_mm_nt(dy[:, j * LANE:(j + 1) * LANE], cm[j]) for j in range(SSM_JB)]
        dsr = jnp.concatenate([d[:, :512] for d in dss], axis=-1)
        dsi = jnp.concatenate([d[:, 512:] for d in dss], axis=-1)
        dcm = jnp.stack([_mm_tn(cats[j], dy[:, j * LANE:(j + 1) * LANE]) for j in range(SSM_JB)], axis=0)
        return dsr, dsi, dy * dsk, dcm, _colsum(dy * u), _mm_tn(zz, dt), _colsum(dt)

    glu_consts = [p['c_mat'], p['ssm_d'], p['w_glu'], p['b_glu']]
    ts = min(256, l)
    nts = l // ts
    ds_re, ds_im, du_dir, g['c_mat'], g['ssm_d'], g['w_glu'], g['b_glu'] = _rows(
        glu_b, name=name + "_b_glu", n=nts, ins=[s_re, s_im, u_p, dzo_p, *glu_consts],
        in_specs=[_rt(ts, SSM_LANES), _rt(ts, SSM_LANES), _rt(ts, SSM_WIDTH), _rt(ts, SSM_WIDTH)] + [_full(a.shape) for a in glu_consts],
        outs=[_sds((l, SSM_LANES), SSM_STATE_DTYPE), _sds((l, SSM_LANES), SSM_STATE_DTYPE), _sds((l, SSM_WIDTH)),
              _sds((SSM_JB, 1024, LANE)), _sds((1, SSM_WIDTH)),
              _sds((SSM_WIDTH, SSM_WIDTH)), _sds((1, SSM_WIDTH))],
        out_specs=[_rt(ts, SSM_LANES), _rt(ts, SSM_LANES), _rt(ts, SSM_WIDTH), _full((SSM_JB, 1024, LANE)), _full((1, SSM_WIDTH)),
                   _full((SSM_WIDTH, SSM_WIDTH)), _full((1, SSM_WIDTH))], n_acc=4)
    gb_re, gb_im = _scan(name + "_b_scan", ds_re, ds_im, sv['a_re'], -sv['a_im'], reverse=True)
    ns = SCAN_SEGS
    last_blk = l // ns - 1

    def da_fn(i, *vals):
        gr, gi, sr, si, hr, hi, lr_, li_ = [v.astype(F32) for v in vals]
        rid = lax.broadcasted_iota(jnp.int32, lr_.shape, 0)
        fr = jnp.where(rid == 0, 0.0, pltpu.roll(lr_, 1, 0))
        fi = jnp.where(rid == 0, 0.0, pltpu.roll(li_, 1, 0))
        hr = jnp.where(i == 0, fr, hr)
        hi = jnp.where(i == 0, fi, hi)
        if ts > ns:
            pr = jnp.concatenate([hr, sr[:ts - ns]], axis=0)
            pi = jnp.concatenate([hi, si[:ts - ns]], axis=0)
        else:
            pr, pi = hr, hi
        return _colsum(gr * pr + gi * pi), _colsum(gi * pr - gr * pi)

    hprev = pl.BlockSpec((ns, SSM_LANES), lambda i: (jnp.maximum(i * (ts // ns) - 1, 0), 0))
    hlast = pl.BlockSpec((ns, SSM_LANES), lambda i: (last_blk, 0))
    da_re, da_im = _rows(da_fn, name=name + "_b_da", n=nts, ins=[gb_re, gb_im, s_re, s_im, s_re, s_im, s_re, s_im],
                         in_specs=[_rt(ts, SSM_LANES)] * 4 + [hprev, hprev, hlast, hlast],
                         outs=[_sds((1, SSM_LANES))] * 2, out_specs=[_full((1, SSM_LANES))] * 2, n_acc=2)

    def bu_b(i, dbr, dbi, u, dud, bm):
        dus, dbm = [], []
        for j in range(SSM_JB):
            cat = jnp.concatenate([dbr[:, j * 512:(j + 1) * 512], dbi[:, j * 512:(j + 1) * 512]], axis=-1)
            dus.append(_mm_nt(cat, bm[j]))
            dbm.append(_mm_tn(u[:, j * LANE:(j + 1) * LANE], cat))
        return dud + jnp.concatenate(dus, axis=-1), jnp.stack(dbm, axis=0)

    du_p, d_bmat = _rows(bu_b, name=name + "_b_bu", n=nts, ins=[gb_re, gb_im, u_p, du_dir, sv['b_mat']],
                         in_specs=[_rt(ts, SSM_LANES), _rt(ts, SSM_LANES), _rt(ts, SSM_WIDTH), _rt(ts, SSM_WIDTH),
                                   _full(sv['b_mat'].shape)],
                         outs=[_sds((l, SSM_WIDTH)), _sds((SSM_JB, LANE, 1024))],
                         out_specs=[_rt(ts, SSM_WIDTH), _full((SSM_JB, LANE, 1024))], n_acc=1)
    dp_u = _from_perm(du_p, l)
    dbb_re = _blockdiag_t(d_bmat[:, :, :512], SSM_GROUP_CH, SSM_STATE).reshape(SSM_GROUPS, SSM_GROUP_CH, SSM_STATE).transpose(1, 0, 2)
    dbb_im = _blockdiag_t(d_bmat[:, :, 512:], SSM_GROUP_CH, SSM_STATE).reshape(SSM_GROUPS, SSM_GROUP_CH, SSM_STATE).transpose(1, 0, 2)
    g['lr'], g['li'], g['log_dt'], g['br'], g['bi'] = _ssm_params_bwd(
        name + "_b_ssm_par", p['lr'], p['li'], p['log_dt'], p['br'], p['bi'],
        da_re.reshape(SSM_GROUPS, SSM_STATE), da_im.reshape(SSM_GROUPS, SSM_STATE), dbb_re, dbb_im)

    dq_t, dk, dv = _flash_bwd(name + "_b_attn", sv['q'], sv['k'], sv['v'], sv['k_t'], sv['o_a'], sv['lse_t'], do_a)

    def qkv_b(i, ps, c, s1, s2, dq_, dk_, dv_, qag, wqb, kvag, wk, wv, qng, kng):
        c_q = ps[:, :Q_LORA]
        c_kv = ps[:, Q_LORA:Q_LORA + KV_LORA]
        kr = ps[:, Q_LORA + KV_LORA:]
        cqn, vjp_cq = jax.vjp(lambda a, b: _rms(a, b, Q_LORA), c_q, qag)
        ckvn, vjp_ckv = jax.vjp(lambda a, b: _rms(a, b, KV_LORA), c_kv, kvag)
        q_raw = _mm(cqn, wqb)
        k_raw = _mm(ckvn, wk) + jnp.concatenate([kr] * MLA_HEADS, axis=-1)
        _, vjp_qn = jax.vjp(lambda a, b: _head_rms(a, b, MLA_HEADS, D_QK), q_raw, qng)
        _, vjp_kn = jax.vjp(lambda a, b: _head_rms(a, b, MLA_HEADS, D_QK), k_raw, kng)
        dq_raw, dqng = vjp_qn(_heads(_rope_t, jnp.transpose(dq_[0]), MLA_HEADS, c, s1, s2))
        dk_raw, dkng = vjp_kn(_heads(_rope_t, dk_, MLA_HEADS, c, s1, s2))
        dkr = dk_raw[:, :LANE]
        for h in range(1, MLA_HEADS):
            dkr = dkr + dk_raw[:, h * LANE:(h + 1) * LANE]
        dcq, dqag = vjp_cq(_mm_nt(dq_raw, wqb))
        dckv, dkvag = vjp_ckv(_mm_nt(dk_raw, wk) + _mm_nt(dv_, wv))
        dps = jnp.concatenate([dcq, dckv, dkr], axis=-1)
        return (dps, _mm_tn(cqn, dq_raw), _mm_tn(ckvn, dk_raw), _mm_tn(ckvn, dv_), dqag, dkvag, dqng, dkng)

    qkv_consts = [p['q_a_norm_g'], p['w_qb'], p['kv_a_norm_g'], p['w_k'], p['w_v'], p['q_norm_g'], p['k_norm_g']]
    (dp_s, g['w_qb'], g['w_k'], g['w_v'], g['q_a_norm_g'], g['kv_a_norm_g'], g['q_norm_g'], g['k_norm_g']) = _rows(
        qkv_b, name=name + "_b_qkv", n=nt, ins=[sv['p_s'], *tabs, dq_t, dk, dv, *qkv_consts],
        in_specs=[_rt(tm, SMALL_W)] + [_rt(tm, LANE)] * 3
        + [pl.BlockSpec((1, MLA_PAD, tm), lambda i: (i // (dq_t.shape[2] // tm), 0, i % (dq_t.shape[2] // tm)))]
        + [_rt(tm, MLA_PAD)] * 2 + [_full(a.shape) for a in qkv_consts],
        outs=[_sds((l, SMALL_W)), _sds((Q_LORA, MLA_PAD)), _sds((KV_LORA, MLA_PAD)), _sds((KV_LORA, MLA_PAD)),
              _sds((1, Q_LORA)), _sds((1, KV_LORA)), _sds((1, LANE)), _sds((1, LANE))],
        out_specs=[_rt(tm, SMALL_W), _full((Q_LORA, MLA_PAD)), _full((KV_LORA, MLA_PAD)), _full((KV_LORA, MLA_PAD)),
                   _full((1, Q_LORA)), _full((1, KV_LORA)), _full((1, LANE)), _full((1, LANE))], n_acc=7)

    x0 = sv['x0']
    dh = _matmul(name + "_b_in", [(dp_g, p['w_g']), (dp_u, p['w_u']), (dp_xq, p['w_xq']), (dp_s, p['w_s'])], l, D_MODEL, nt=True,
                 tm=256)
    gm = p['norm_mix_g']
    g['w_g'] = _matmul_tn(name + "_gw_g", x0, dp_g, rms_gain=gm)
    g['w_u'] = _matmul_tn(name + "_gw_u", x0, dp_u, rms_gain=gm)
    g['w_xq'] = _matmul_tn(name + "_gw_xq", x0, dp_xq, rms_gain=gm)
    g['w_s'] = _matmul_tn(name + "_gw_s", x0, dp_s, rms_gain=gm)
    dx0, g['norm_mix_g'] = _rows(norm_b, name=name + "_b_norm1", n=nt, ins=[x0, dh, dx1, gm],
                                 in_specs=[_rt(tm, D_MODEL)] * 3 + [_full((1, D_MODEL))],
                                 outs=[_sds((l, D_MODEL)), _sds((1, D_MODEL))], out_specs=[_rt(tm, D_MODEL), _full((1, D_MODEL))],
                                 n_acc=1)
    return dx0, g


def _unprep_grads(g):
    o = {}
    ws = g['w_s']
    o['w_in'] = jnp.concatenate([ws[:, :Q_LORA + KV_LORA], ws[:, Q_LORA + KV_LORA + D_NOPE:Q_LORA + KV_LORA + D_QK],
                                 g['w_u'], g['w_xq'], g['w_g']], axis=1)
    o['w_q_b'] = g['w_qb'].reshape(Q_LORA, MLA_HEADS, HEAD_PAD)[:, :, :D_QK].reshape(Q_LORA, MLA_HEADS * D_QK)
    gk = g['w_k'].reshape(KV_LORA, MLA_HEADS, HEAD_PAD)[:, :, :D_NOPE]
    gv = g['w_v'].reshape(KV_LORA, MLA_HEADS, HEAD_PAD)[:, :, :D_V]
    o['w_kv_b'] = jnp.concatenate([gk, gv], axis=2).reshape(KV_LORA, MLA_HEADS * (D_NOPE + D_V))
    o['w_o_mla'] = g['w_oa'].reshape(MLA_HEADS, HEAD_PAD, D_MODEL)[:, :D_V].reshape(MLA_HEADS * D_V, D_MODEL)
    for n in ('w_glu', 'w_o_ssm', 'w_mem_kv', 'w_o_cross', 'w_out', 'w_up', 'w_down', 'conv_w'):
        o[n] = g[n]
    for n in ('norm_mix_g', 'q_a_norm_g', 'kv_a_norm_g', 'b_glu', 'mem_norm_g', 'xq_norm_g', 'xk_norm_g', 'b_gate',
              'norm_ffn_g', 'conv_b'):
        o[n] = g[n].reshape(-1)
    o['q_norm_g'] = g['q_norm_g'].reshape(-1)[:D_QK]
    o['k_norm_g'] = g['k_norm_g'].reshape(-1)[:D_QK]
    o['ssm_d'] = g['ssm_d'].reshape(SSM_GROUPS, SSM_GROUP_CH)
    o['ssm_lambda_re'] = g['lr']
    o['ssm_lambda_im'] = g['li']
    o['ssm_log_dt'] = g['log_dt'].reshape(SSM_GROUPS)
    o['ssm_b_re'] = g['br'].transpose(1, 2, 0)
    o['ssm_b_im'] = g['bi'].transpose(1, 2, 0)
    dc = g['c_mat']
    o['ssm_c_re'] = _blockdiag_t(dc[:, :512], SSM_STATE, SSM_GROUP_CH).transpose(0, 1, 3, 2).reshape(SSM_GROUPS, SSM_GROUP_CH, SSM_STATE)
    o['ssm_c_im'] = -_blockdiag_t(dc[:, 512:], SSM_STATE, SSM_GROUP_CH).transpose(0, 1, 3, 2).reshape(SSM_GROUPS, SSM_GROUP_CH, SSM_STATE)
    return o


def _local_step(x, mem, pos, target, w):
    l = x.shape[0]
    tm = min(512, l)
    tabs = _rope_tables(pos.astype(F32).reshape(l, 1))
    ps = [_prep_layer(w, i) for i in range(DEPTH)]
    saved = []
    h = x
    for i in range(DEPTH):
        h, sv = _layer_fwd("l%d" % i, h, tabs, mem, ps[i])
        saved.append(sv)

    def loss_fn(i, y, t):
        e = y - t
        per_tok = jnp.sum(e * e, axis=-1, keepdims=True) * (1.0 / D_MODEL)
        tot = 0.5 * jnp.sum(per_tok, axis=0, keepdims=True)
        return e * (1.0 / D_MODEL), jnp.broadcast_to(tot, (1, LANE))

    dy, loss = _rows(loss_fn, name="loss", n=l // tm, ins=[h, target], in_specs=[_rt(tm, D_MODEL)] * 2,
                     outs=[_sds((l, D_MODEL)), _sds((1, LANE))], out_specs=[_rt(tm, D_MODEL), _full((1, LANE))], n_acc=1)
    grads = []
    d = dy
    for i in reversed(range(DEPTH)):
        d, g = _layer_bwd("l%d" % i, d, saved[i], tabs, mem, ps[i])
        grads.append(_unprep_grads(g))
    return loss[0, 0], d, grads[::-1]


def _sum_picked(name, slots, pick, extra, out_dtype):
    _, r, c = slots.shape
    e = extra.shape[0]
    tr = _row_tile(r)

    def body(pk, s_ref, x_ref, o_ref):
        acc = s_ref[...].astype(F32)
        for k in range(e):
            acc = acc + x_ref[k].astype(F32)
        o_ref[...] = acc.astype(o_ref.dtype)

    grid_spec = pltpu.PrefetchScalarGridSpec(
        num_scalar_prefetch=1, grid=(r // tr,),
        in_specs=[pl.BlockSpec((None, tr, c), lambda i, pk: (pk[0], i, 0)), pl.BlockSpec((e, tr, c), lambda i, pk: (0, i, 0))],
        out_specs=pl.BlockSpec((tr, c), lambda i, pk: (i, 0)))
    return pl.pallas_call(body, name=name, grid_spec=grid_spec, out_shape=_sds((r, c), out_dtype),
                          compiler_params=_params(("arbitrary",), 48))(pick, slots, extra)


def _row_tile(r):
    for t in (256, 128, 64, 32, 16, 8):
        if r % t == 0:
            return t
    return r


def _adamw(name, parts, w, m, v):
    r, cw = w.shape
    tr = _row_tile(r)
    np_ = len(parts)

    def fn(i, *vals):
        wv, mv, vv = vals[np_:]
        terms = []
        for pv in vals[:np_]:
            terms += [pv] if pv.ndim == 2 else [pv[k] for k in range(pv.shape[0])]
        g = terms[0]
        for t in terms[1:]:
            g = g + t
        mn = ADAM_B1 * mv + (1.0 - ADAM_B1) * g
        vn = ADAM_B2 * vv + (1.0 - ADAM_B2) * (g * g)
        m_hat = mn / (1.0 - ADAM_B1 ** ADAM_STEP)
        v_hat = vn / (1.0 - ADAM_B2 ** ADAM_STEP)
        delta = -ADAM_LR * (m_hat / (jnp.sqrt(v_hat) + ADAM_EPS) + ADAM_WD * wv)
        return g, delta, mn, vn

    pspecs = [_rt(tr, cw) if p.ndim == 2 else pl.BlockSpec((p.shape[0], tr, cw), lambda i: (0, i, 0)) for p in parts]
    return _rows(fn, name=name, n=r // tr, ins=[*parts, w, m, v], in_specs=pspecs + [_rt(tr, cw)] * 3,
                 outs=[_sds((r, cw))] * 4, out_specs=[_rt(tr, cw)] * 4)


def _shard_of(a, axis, k):
    n = a.shape[axis] // 4
    return lax.slice_in_dim(a, k * n, (k + 1) * n, axis=axis)


def _step(a):
    x = a['x'][0]
    mem = a['mem'][0]
    pos = a['positions'][0]
    target = a['loss_target'][0]

    me = 2 * lax.axis_index("x") + lax.axis_index("y")

    mine = [a[n] if n == 'conv_w' else a[n].astype(BF16) for n in SHARDED]
    got = _gather_d2d(_gather_ici(mine))
    w = {}
    for n, own, y in zip(SHARDED, mine, got):
        ax = SHARD_AXIS[n] - 1
        w[n] = [jnp.concatenate([jnp.where(me == k, own[i], y[k, i]) for k in range(4)], axis=ax) for i in range(DEPTH)]
    for n in SMALL:
        w[n] = a[n]

    loss, grad_x, grads = _local_step(x, mem, pos, target, w)

    mc = lax.axis_index("c")
    mc1 = mc.astype(jnp.int32).reshape(1)
    me1 = me.astype(jnp.int32).reshape(1)
    gsh = []
    for n in SHARDED:
        ax = SHARD_AXIS[n] - 1
        gsh.append(jnp.stack([jnp.stack([_shard_of(grads[i][n], ax, k) for k in range(4)], axis=0)
                              for i in range(DEPTH)], axis=0).astype(BF16))
    sib = _swap_d2d("comm_reduce_pair", gsh, other_layer=True)
    pair = []
    for n, g, s in zip(SHARDED, gsh, sib):
        rows, cols = g.shape[-2:]
        pair.append(_sum_picked("sum2_" + n, g.reshape(DEPTH, 4 * rows, cols), mc1, s.reshape(1, 4 * rows, cols), BF16)
                    .reshape(4, rows, cols))
    got = _reduce_ici(pair)
    parts = [_sum_picked("sum4_" + n, p4, me1, g3, F32) for n, p4, g3 in zip(SHARDED, pair, got)]
    others = _swap_d2d("comm_reduce_d2d", parts, other_layer=False)
    res_sh = []
    for n, part, other in zip(SHARDED, parts, others):
        cols = part.shape[-1]
        full = jnp.where(mc == 0, jnp.stack([part, other], axis=0), jnp.stack([other, part], axis=0))
        res = _adamw("adamw_" + n, [full.reshape(-1, cols)], *[a[pre + n].reshape(-1, cols) for pre in ('', 'm_', 'v_')])
        res_sh.append([r.reshape(a[n].shape) for r in res])
    res_sh = [[res_sh[j][kind] for j in range(len(SHARDED))] for kind in range(4)]

    sm_shapes = [a[n].shape for n in SMALL] + [(1,)]
    gsm = _pack([jnp.stack([grads[i][n] for i in range(DEPTH)], axis=0) for n in SMALL] + [loss.reshape(1)], 8, F32)
    alls = _all_exchange("comm_reduce_small", gsm)
    zero1 = jnp.zeros((1,), F32)
    res_sm = _adamw("adamw_small", [alls], *[_pack([a[pre + n] for n in SMALL] + [zero1], 8, F32) for pre in ('', 'm_', 'v_')])
    res_sm = [_unpack(r, sm_shapes) for r in res_sm]
    loss = res_sm[0][-1][0]

    outs = [loss, grad_x[None]]
    for kind in range(4):
        byname = dict(zip(SHARDED, res_sh[kind]))
        byname.update(zip(SMALL, res_sm[kind]))
        outs += [byname[n] for n in WEIGHTS]
    return tuple(outs)


def kernel(x, mem, positions, norm_mix_g, w_in, q_a_norm_g, w_q_b, kv_a_norm_g, w_kv_b, q_norm_g, k_norm_g, w_o_mla, ssm_lambda_re, ssm_lambda_im, ssm_log_dt, ssm_b_re, ssm_b_im, ssm_c_re, ssm_c_im, ssm_d, w_glu, b_glu, w_o_ssm, mem_norm_g, w_mem_kv, xq_norm_g, xk_norm_g, w_o_cross, b_gate, w_out, norm_ffn_g, w_up, conv_w, conv_b, w_down, loss_target, m_norm_mix_g, m_w_in, m_q_a_norm_g, m_w_q_b, m_kv_a_norm_g, m_w_kv_b, m_q_norm_g, m_k_norm_g, m_w_o_mla, m_ssm_lambda_re, m_ssm_lambda_im, m_ssm_log_dt, m_ssm_b_re, m_ssm_b_im, m_ssm_c_re, m_ssm_c_im, m_ssm_d, m_w_glu, m_b_glu, m_w_o_ssm, m_mem_norm_g, m_w_mem_kv, m_xq_norm_g, m_xk_norm_g, m_w_o_cross, m_b_gate, m_w_out, m_norm_ffn_g, m_w_up, m_conv_w, m_conv_b, m_w_down, v_norm_mix_g, v_w_in, v_q_a_norm_g, v_w_q_b, v_kv_a_norm_g, v_w_kv_b, v_q_norm_g, v_k_norm_g, v_w_o_mla, v_ssm_lambda_re, v_ssm_lambda_im, v_ssm_log_dt, v_ssm_b_re, v_ssm_b_im, v_ssm_c_re, v_ssm_c_im, v_ssm_d, v_w_glu, v_b_glu, v_w_o_ssm, v_mem_norm_g, v_w_mem_kv, v_xq_norm_g, v_xk_norm_g, v_w_o_cross, v_b_gate, v_w_out, v_norm_ffn_g, v_w_up, v_conv_w, v_conv_b, v_w_down):
    return _step(dict(locals()))
```

```python
import functools
import math

import numpy as np
import jax
import jax.numpy as jnp
from jax import lax
from jax.experimental import pallas as pl
from jax.experimental.pallas import tpu as pltpu

F32 = jnp.float32
BF16 = jnp.bfloat16
MESH = pl.DeviceIdType.MESH

DEPTH = 2
D_MODEL = 1024
EPS = 1e-6
MLA_HEADS = 8
Q_LORA = 384
KV_LORA = 256
D_NOPE = 64
D_ROPE = 32
D_QK = D_NOPE + D_ROPE
D_V = 64
HEAD_PAD = 128
MLA_PAD = MLA_HEADS * HEAD_PAD
ROPE_THETA = 10000.0
SSM_GROUPS = 32
SSM_GROUP_CH = 16
SSM_WIDTH = 512
SSM_STATE = 64
SSM_LANES = SSM_GROUPS * SSM_STATE
SSM_JB = 4
X_HEADS = 4
X_HEAD_DIM = 128
X_WIDTH = 512
D_FF = 2816
SMALL_W = Q_LORA + KV_LORA + HEAD_PAD
SCAN_SEGS = 32
SSM_STATE_DTYPE = BF16
LANE = 128
NEG = -1e30

ADAM_LR = 0.001
ADAM_B1 = 0.9
ADAM_B2 = 0.999
ADAM_EPS = 1e-08
ADAM_WD = 0.01
ADAM_STEP = 10

WEIGHTS = ['norm_mix_g', 'w_in', 'q_a_norm_g', 'w_q_b', 'kv_a_norm_g', 'w_kv_b', 'q_norm_g', 'k_norm_g', 'w_o_mla',
           'ssm_lambda_re', 'ssm_lambda_im', 'ssm_log_dt', 'ssm_b_re', 'ssm_b_im', 'ssm_c_re', 'ssm_c_im', 'ssm_d',
           'w_glu', 'b_glu', 'w_o_ssm', 'mem_norm_g', 'w_mem_kv', 'xq_norm_g', 'xk_norm_g', 'w_o_cross', 'b_gate',
           'w_out', 'norm_ffn_g', 'w_up', 'conv_w', 'conv_b', 'w_down']
SHARD_AXIS = {'w_in': 2, 'w_q_b': 2, 'w_kv_b': 2, 'w_o_mla': 2, 'w_glu': 1, 'w_o_ssm': 2, 'w_mem_kv': 1,
              'w_o_cross': 2, 'w_out': 1, 'w_up': 2, 'conv_w': 2, 'w_down': 1}
SHARDED = [n for n in WEIGHTS if n in SHARD_AXIS]
GATHER_BF16 = [n for n in SHARDED if n != 'conv_w']
SMALL = [n for n in WEIGHTS if n not in SHARD_AXIS]


def _bf(v):
    return v.astype(BF16)


def _mm(a, b):
    return jnp.dot(_bf(a), _bf(b), preferred_element_type=F32)


def _mm_nt(a, b):
    return lax.dot_general(_bf(a), _bf(b), (((1,), (1,)), ((), ())), preferred_element_type=F32)


def _mm_tn(a, b):
    return lax.dot_general(_bf(a), _bf(b), (((0,), (0,)), ((), ())), preferred_element_type=F32)


def _rms(v, g, n):
    ms = jnp.sum(v * v, axis=-1, keepdims=True) * (1.0 / n)
    return (v * lax.rsqrt(ms + EPS)) * g


def _head_rms(v, g, heads, n):
    return jnp.concatenate([_rms(v[:, h * LANE:(h + 1) * LANE], g, n) for h in range(heads)], axis=-1)


def _rope(v, c, s1, s2):
    return v * c + pltpu.roll(v, LANE - 16, 1) * s1 + pltpu.roll(v, 16, 1) * s2


def _rope_t(g, c, s1, s2):
    return g * c + pltpu.roll(g * s1, 16, 1) + pltpu.roll(g * s2, LANE - 16, 1)


def _heads(fn, v, heads, *tabs):
    return jnp.concatenate([fn(v[:, h * LANE:(h + 1) * LANE], *tabs) for h in range(heads)], axis=-1)


def _gelu(y):
    return y * (0.5 * (1.0 + jnp.tanh(math.sqrt(2.0 / math.pi) * (y + 0.044715 * (y * y * y)))))


def _silu(g):
    return g * jax.nn.sigmoid(g)


def _colsum(v):
    return jnp.sum(v, axis=0, keepdims=True)


def _row_select(rows, n):
    rid = lax.broadcasted_iota(jnp.int32, (n, rows[0].shape[-1]), 0)
    out = jnp.zeros((n, rows[0].shape[-1]), F32)
    for k, r in enumerate(rows):
        out = jnp.where(rid == k, jnp.broadcast_to(r, out.shape), out)
    return out


def _params(sem, vmem_mb):
    return pltpu.CompilerParams(dimension_semantics=sem, vmem_limit_bytes=vmem_mb * 1024 * 1024)


def _rt(tm, w, cb=0):
    return pl.BlockSpec((tm, w), lambda i: (i, cb))


def _full(shape):
    nd = len(shape)
    return pl.BlockSpec(tuple(shape), lambda i: (0,) * nd)


def _rows(fn, *, name, n, ins, in_specs, outs, out_specs, n_acc=0, vmem=48):
    n_in = len(ins)
    n_out = len(outs)

    def body(*refs):
        i = pl.program_id(0)
        res = fn(i, *[r[...] for r in refs[:n_in]])
        if not isinstance(res, (tuple, list)):
            res = (res,)
        assert len(res) == n_out, (name, len(res), n_out)
        for k, (r, v) in enumerate(zip(refs[n_in:], res)):
            if k < n_out - n_acc:
                r[...] = v.astype(r.dtype)
            else:
                @pl.when(i == 0)
                def _():
                    r[...] = v

                @pl.when(i > 0)
                def _():
                    r[...] += v

    return pl.pallas_call(
        body, name=name, grid=(n,), in_specs=list(in_specs), out_specs=tuple(out_specs), out_shape=tuple(outs),
        compiler_params=_params(("arbitrary",), vmem))(*ins)


def _sds(shape, dtype=F32):
    return jax.ShapeDtypeStruct(tuple(shape), dtype)


def _tile_n(n, cap=1536):
    best = None
    for t in range(LANE, min(n, cap) + 1, LANE):
        if n % t == 0:
            best = t
    if best is None or n <= 1408:
        return n
    return best


def _matmul(name, pairs, m, n, *, nt=False, rms_gain=None, resid=None, out_dtype=F32, tm=1024, vmem=56):
    tm = min(tm, m)
    tn = _tile_n(n)
    ks = [a.shape[1] for a, _ in pairs]
    np_ = len(pairs)

    def body(*refs):
        a_refs = refs[:np_]
        b_refs = refs[np_:2 * np_]
        k = 2 * np_
        g_ref = None
        r_ref = None
        if rms_gain is not None:
            g_ref = refs[k]
            k += 1
        if resid is not None:
            r_ref = refs[k]
            k += 1
        o_ref = refs[k]
        scr = refs[k + 1:]
        j = pl.program_id(1)

        @pl.when(j == 0)
        def _():
            for p in range(np_):
                a = a_refs[p][...]
                if p == 0 and g_ref is not None:
                    a = _rms(a.astype(F32), g_ref[...], ks[0])
                scr[p][...] = a.astype(BF16)

        acc = None
        for p in range(np_):
            b = b_refs[p][...].astype(BF16)
            if nt:
                t = lax.dot_general(scr[p][...], b, (((1,), (1,)), ((), ())), preferred_element_type=F32)
            else:
                t = jnp.dot(scr[p][...], b, preferred_element_type=F32)
            acc = t if acc is None else acc + t
        if r_ref is not None:
            acc = acc + r_ref[...]
        o_ref[...] = acc.astype(o_ref.dtype)

    in_specs = [pl.BlockSpec((tm, kk), lambda i, j: (i, 0)) for kk in ks]
    if nt:
        in_specs += [pl.BlockSpec((tn, kk), lambda i, j: (j, 0)) for kk in ks]
    else:
        in_specs += [pl.BlockSpec((kk, tn), lambda i, j: (0, j)) for kk in ks]
    ins = [a for a, _ in pairs] + [b for _, b in pairs]
    if rms_gain is not None:
        in_specs.append(pl.BlockSpec((1, ks[0]), lambda i, j: (0, 0)))
        ins.append(rms_gain)
    if resid is not None:
        in_specs.append(pl.BlockSpec((tm, tn), lambda i, j: (i, j)))
        ins.append(resid)
    return pl.pallas_call(
        body, name=name, grid=(m // tm, n // tn), in_specs=in_specs,
        out_specs=pl.BlockSpec((tm, tn), lambda i, j: (i, j)), out_shape=_sds((m, n), out_dtype),
        scratch_shapes=[pltpu.VMEM((tm, kk), BF16) for kk in ks],
        compiler_params=_params(("arbitrary", "arbitrary"), vmem))(*ins)


def _matmul_tn(name, a, b, *, rms_gain=None, tl=1024, vmem=56):
    l, ka = a.shape
    n = b.shape[1]
    tl = min(tl, l)
    tn = _tile_n(n, 1536)

    def body(*refs):
        if rms_gain is not None:
            a_ref, b_ref, g_ref, o_ref = refs
        else:
            a_ref, b_ref, o_ref = refs
        t = pl.program_id(1)
        av = a_ref[...]
        if rms_gain is not None:
            av = _rms(av.astype(F32), g_ref[...], ka)
        v = _mm_tn(av, b_ref[...])

        @pl.when(t == 0)
        def _():
            o_ref[...] = v

        @pl.when(t > 0)
        def _():
            o_ref[...] += v

    in_specs = [pl.BlockSpec((tl, ka), lambda j, t: (t, 0)), pl.BlockSpec((tl, tn), lambda j, t: (t, j))]
    ins = [a, b]
    if rms_gain is not None:
        in_specs.append(pl.BlockSpec((1, ka), lambda j, t: (0, 0)))
        ins.append(rms_gain)
    return pl.pallas_call(
        body, name=name, grid=(n // tn, l // tl), in_specs=in_specs,
        out_specs=pl.BlockSpec((ka, tn), lambda j, t: (0, j)), out_shape=_sds((ka, n)),
        compiler_params=_params(("arbitrary", "arbitrary"), vmem))(*ins)


ATT_HEADS_PER_STEP = 2
ATT_W = ATT_HEADS_PER_STEP * LANE
ATT_GROUPS = MLA_HEADS // ATT_HEADS_PER_STEP
LOG2E = math.log2(math.e)
ATT_FWD_TILE = 1024
ATT_BWD_TILE = 1024
ATT_BWD_HEADS = 1
ATT_SCALE = D_QK ** -0.5
ATT_QSCALE = ATT_SCALE * LOG2E


def _tri_tables(nq, by_k):
    qs, ks = [], []
    if by_k:
        for ki in range(nq):
            for qi in range(ki, nq):
                qs.append(qi)
                ks.append(ki)
    else:
        for qi in range(nq):
            for ki in range(qi + 1):
                qs.append(qi)
                ks.append(ki)
    return jnp.asarray(np.array(qs, np.int32)), jnp.asarray(np.array(ks, np.int32))


def _causal_keep(shape, transposed):
    r = lax.broadcasted_iota(jnp.int32, shape, 0)
    c = lax.broadcasted_iota(jnp.int32, shape, 1)
    return (r <= c) if transposed else (c <= r)


def _nt16(a, b):
    return lax.dot_general(a, b, (((1,), (1,)), ((), ())), preferred_element_type=F32)


def _row_form(col):
    return jnp.transpose(jnp.broadcast_to(col, (col.shape[0], LANE)))[:8]


def _att_call(body, name, l, tq, tabs, ins, in_specs, outs, out_specs, scratch=(), groups=ATT_GROUPS, vmem=48):
    grid_spec = pltpu.PrefetchScalarGridSpec(
        num_scalar_prefetch=2, grid=(groups, tabs[0].shape[0]), in_specs=in_specs, out_specs=out_specs,
        scratch_shapes=list(scratch))
    return pl.pallas_call(body, name=name, grid_spec=grid_spec, out_shape=outs,
                          compiler_params=_params(("arbitrary", "arbitrary"), vmem))(*tabs, *ins)


def _flash_fwd(name, q, k, v_t):
    l = q.shape[0]
    tq = min(ATT_FWD_TILE, l)
    nq = l // tq
    tabs = _tri_tables(nq, by_k=False)

    def body(qt, kt, q_ref, k_ref, vt_ref, o_ref, lset_ref, m_s, acc_s):
        t = pl.program_id(1)
        qi = qt[t]
        ki = kt[t]
        sls = [slice(h * LANE, (h + 1) * LANE) for h in range(ATT_HEADS_PER_STEP)]

        @pl.when(ki == 0)
        def _():
            m_s[...] = jnp.full(m_s.shape, NEG, F32)
            acc_s[...] = jnp.zeros(acc_s.shape, F32)

        def step(masked):
            sts = [_nt16(k_ref[:, sl], q_ref[:, sl]) for sl in sls]
            for h, sl in enumerate(sls):
                st = sts[h]
                if masked:
                    st = jnp.where(_causal_keep(st.shape, True), st, NEG)
                m_old = m_s[h][:1]
                m_new = jnp.maximum(m_old, jnp.max(st, axis=0, keepdims=True))
                alpha = jnp.exp2(m_old - m_new)
                pt = jnp.exp2(st - m_new).astype(BF16)
                acc_s[sl, :] = alpha * acc_s[sl, :] + jnp.dot(vt_ref[sl, :], pt, preferred_element_type=F32)
                m_s[h] = jnp.broadcast_to(m_new, (8, tq))

        @pl.when(ki < qi)
        def _():
            step(False)

        @pl.when(ki == qi)
        def _():
            step(True)
            row = lax.broadcasted_iota(jnp.int32, (LANE, tq), 0)
            for h, sl in enumerate(sls):
                acc = acc_s[sl, :]
                lsum = acc[D_V:D_V + 1, :]
                o_ref[:, sl] = jnp.transpose(jnp.where(row < D_V, acc / lsum, 0.0))
                lset_ref[h * 8:(h + 1) * 8, :] = m_s[h] + jnp.log2(lsum)

    qspec = pl.BlockSpec((tq, ATT_W), lambda g, t, qt, kt: (qt[t], g))
    kspec = pl.BlockSpec((tq, ATT_W), lambda g, t, qt, kt: (kt[t], g))
    vspec = pl.BlockSpec((ATT_W, tq), lambda g, t, qt, kt: (g, kt[t]))
    rspec = pl.BlockSpec((8 * ATT_HEADS_PER_STEP, tq), lambda g, t, qt, kt: (g, qt[t]))
    return _att_call(
        body, name, l, tq, tabs, [q, k, v_t], [qspec, kspec, vspec],
        (_sds((l, MLA_PAD)), _sds((8 * MLA_HEADS, l))), (qspec, rspec),
        scratch=[pltpu.VMEM((ATT_HEADS_PER_STEP, 8, tq), F32), pltpu.VMEM((ATT_W, tq), F32)])


def _flash_bwd(name, q, k, v, k_t, o, lse_t, do):
    l = q.shape[0]
    tq = min(ATT_BWD_TILE, l)
    nq = l // tq
    hb = ATT_BWD_HEADS
    wb = hb * LANE

    def delta_fn(i, dov, ov):
        rows = []
        for h in range(MLA_HEADS):
            sl = slice(h * LANE, (h + 1) * LANE)
            rows.append(_row_form(jnp.sum(dov[:, sl] * ov[:, sl], axis=-1, keepdims=True)))
        return jnp.concatenate(rows, axis=0), dov

    delta_t, do16 = _rows(
        delta_fn, name=name + "_delta", n=nq, ins=[do, o], in_specs=[_rt(tq, MLA_PAD)] * 2,
        outs=[_sds((8 * MLA_HEADS, l)), _sds((l, MLA_PAD), BF16)],
        out_specs=[pl.BlockSpec((8 * MLA_HEADS, tq), lambda i: (0, i)), _rt(tq, MLA_PAD)])

    def body(qt, kt, q_ref, k_ref, v_ref, do_ref, kt_ref, lset_ref, dlt_ref, dk_ref, dv_ref, dqt_ref):
        t = pl.program_id(1)
        qi = qt[t]
        ki = kt[t]
        sls = [slice(h * LANE, (h + 1) * LANE) for h in range(hb)]

        @pl.when(ki == 0)
        def _():
            dqt_ref[qi] = jnp.zeros((wb, tq), F32)

        def step(masked):
            sts = [_nt16(k_ref[:, sl], q_ref[:, sl]) for sl in sls]
            dpts = [_nt16(v_ref[:, sl], do_ref[:, sl]) for sl in sls]
            for h, sl in enumerate(sls):
                st = sts[h]
                if masked:
                    st = jnp.where(_causal_keep(st.shape, True), st, NEG)
                pt = jnp.exp2(st - lset_ref[h * 8:(h + 1) * 8, :][:1])
                dst = (pt * (dpts[h] - dlt_ref[h * 8:(h + 1) * 8, :][:1])).astype(BF16)
                dv_ref[:, sl] += jnp.dot(pt.astype(BF16), do_ref[:, sl], preferred_element_type=F32)
                dk_ref[:, sl] += jnp.dot(dst, q_ref[:, sl], preferred_element_type=F32)
                dqt_ref[qi, sl, :] += jnp.dot(kt_ref[sl, :], dst, preferred_element_type=F32)

        @pl.when(qi == ki)
        def _():
            dk_ref[...] = jnp.zeros(dk_ref.shape, F32)
            dv_ref[...] = jnp.zeros(dv_ref.shape, F32)
            step(True)
            dqt_ref[qi] = dqt_ref[qi] * ATT_SCALE

        @pl.when(qi > ki)
        def _():
            step(False)

        @pl.when(qi == nq - 1)
        def _():
            dk_ref[...] = dk_ref[...] * (1.0 / LOG2E)

    tabs_k = _tri_tables(nq, by_k=True)
    qspec = pl.BlockSpec((tq, wb), lambda g, t, qt, kt: (qt[t], g))
    kspec = pl.BlockSpec((tq, wb), lambda g, t, qt, kt: (kt[t], g))
    ktspec = pl.BlockSpec((wb, tq), lambda g, t, qt, kt: (g, kt[t]))
    rspec = pl.BlockSpec((8 * hb, tq), lambda g, t, qt, kt: (g, qt[t]))
    dqspec = pl.BlockSpec((nq, wb, tq), lambda g, t, qt, kt: (0, g, 0))
    dk, dv, dq_t = _att_call(body, name + "_dqkv", l, tq, tabs_k, [q, k, v, do16, k_t, lse_t, delta_t],
                             [qspec, kspec, kspec, qspec, ktspec, rspec, rspec],
                             (_sds((l, MLA_PAD)), _sds((l, MLA_PAD)), _sds((nq, MLA_PAD, tq))), (kspec, kspec, dqspec),
                             groups=MLA_HEADS // hb, vmem=56)
    return dq_t, dk, dv


def _cmul(ar, ai, br, bi):
    return ar * br - ai * bi, ar * bi + ai * br


def _scan(name, x_re, x_im, a_re, a_im, reverse):
    l, lanes = x_re.shape
    ns = SCAN_SEGS
    tl = l // ns
    steps = int(math.log2(tl))
    assert 2 ** steps == tl and tl * ns == l

    def body(xr_ref, xi_ref, ar_ref, ai_ref, sr_ref, si_ref):
        a_r1 = ar_ref[...]
        a_i1 = ai_ref[...]
        a_r = jnp.broadcast_to(a_r1, (ns, LANE))
        a_i = jnp.broadcast_to(a_i1, (ns, LANE))

        def rows(t):
            t = (tl - 1 - t) if reverse else t
            return pl.ds(pl.multiple_of(t * ns, ns), ns)

        def local(t, carry):
            cr, ci = carry
            r = rows(t)
            pr, pi = _cmul(a_r, a_i, cr, ci)
            return pr + xr_ref[r, :].astype(F32), pi + xi_ref[r, :].astype(F32)

        zero = jnp.zeros((ns, LANE), F32)
        e_r, e_i = lax.fori_loop(0, tl, local, (zero, zero), unroll=min(8, tl))
        p_r, p_i = a_r1, a_i1
        for _ in range(steps):
            p_r, p_i = _cmul(p_r, p_i, p_r, p_i)
        rid = lax.broadcasted_iota(jnp.int32, (ns, LANE), 0)
        c_r = jnp.zeros((1, LANE), F32)
        c_i = jnp.zeros((1, LANE), F32)
        in_r, in_i = zero, zero
        order = range(ns - 2, -1, -1) if reverse else range(1, ns)
        for kk in order:
            src = kk + 1 if reverse else kk - 1
            ek_r = jnp.sum(jnp.where(rid == src, e_r, 0.0), axis=0, keepdims=True)
            ek_i = jnp.sum(jnp.where(rid == src, e_i, 0.0), axis=0, keepdims=True)
            q_r, q_i = _cmul(p_r, p_i, c_r, c_i)
            c_r, c_i = q_r + ek_r, q_i + ek_i
            in_r = jnp.where(rid == kk, jnp.broadcast_to(c_r, (ns, LANE)), in_r)
            in_i = jnp.where(rid == kk, jnp.broadcast_to(c_i, (ns, LANE)), in_i)

        def final(t, carry):
            cr, ci = carry
            r = rows(t)
            pr, pi = _cmul(a_r, a_i, cr, ci)
            nr, ni = pr + xr_ref[r, :].astype(F32), pi + xi_ref[r, :].astype(F32)
            sr_ref[r, :] = nr.astype(sr_ref.dtype)
            si_ref[r, :] = ni.astype(si_ref.dtype)
            return nr, ni

        lax.fori_loop(0, tl, final, (in_r, in_i), unroll=min(8, tl))

    xs = pl.BlockSpec((l, LANE), lambda j: (0, j))
    as_ = pl.BlockSpec((1, LANE), lambda j: (0, j))
    return pl.pallas_call(
        body, name=name, grid=(lanes // LANE,), in_specs=[xs, xs, as_, as_], out_specs=(xs, xs),
        out_shape=(_sds((l, lanes), x_re.dtype), _sds((l, lanes), x_re.dtype)),
        compiler_params=_params(("arbitrary",), 48))(x_re, x_im, a_re, a_im)


ANY = pl.BlockSpec(memory_space=pl.ANY)


def _place():
    mx, my, mc = lax.axis_index("x"), lax.axis_index("y"), lax.axis_index("c")
    return mx, my, mc, [(1 - mx, my), (mx, 1 - my), (1 - mx, 1 - my)]


def _run_copies(copies):
    for cp in copies:
        cp.start()
    for cp in copies:
        cp.wait_recv()
    for cp in copies:
        cp.wait_send()


def _remote(src, dst, sems, k, dev):
    return pltpu.make_async_remote_copy(src_ref=src, dst_ref=dst, send_sem=sems[0].at[k], recv_sem=sems[1].at[k],
                                        device_id=dev, device_id_type=MESH)


def _copy_call(body, name, ins, outs, n_copies, aliases=None):
    return pl.pallas_call(
        body, name=name, in_specs=[ANY] * len(ins), out_specs=[ANY] * len(outs), out_shape=list(outs),
        input_output_aliases=aliases or {},
        scratch_shapes=[pltpu.SemaphoreType.DMA((n_copies,)), pltpu.SemaphoreType.DMA((n_copies,))])(*ins)


def _gather_ici(xs):
    n = len(xs)

    def body(*refs):
        x_refs, y_refs, sems = refs[:n], refs[n:2 * n], refs[2 * n:]
        mx, my, mc, peers = _place()
        me = 2 * mx + my
        ici = [_remote(x_refs[i].at[mc], y_refs[i].at[me, mc], sems, 3 * i + j, (px, py, mc))
               for i in range(n) for j, (px, py) in enumerate(peers)]
        for cp in ici:
            cp.start()
        fwd = []
        for idx, cp in enumerate(ici):
            i, j = divmod(idx, 3)
            pk = 2 * peers[j][0] + peers[j][1]
            cp.wait_recv()
            f = _remote(y_refs[i].at[pk, mc], y_refs[i].at[pk, mc], sems, 3 * n + idx, (mx, my, 1 - mc))
            f.start()
            fwd.append(f)
        for f in fwd:
            f.wait_recv()
        for cp in ici + fwd:
            cp.wait_send()

    return _copy_call(body, "comm_gather", xs, [_sds((4,) + x.shape, x.dtype) for x in xs], 6 * n)


def _reduce_ici(gs, small):
    n = len(gs)

    def body(*refs):
        g_refs, s_ref, y_refs, a_ref, sems, local_sem = refs[:n], refs[n], refs[n + 1:2 * n + 1], refs[2 * n + 1], refs[2 * n + 2:2 * n + 4], refs[2 * n + 4]
        mx, my, mc, peers = _place()
        me8 = 4 * mx + 2 * my + mc
        own = pltpu.make_async_copy(s_ref, a_ref.at[me8], local_sem)
        own.start()
        copies = [_remote(g_refs[i].at[2 * px + py], y_refs[i].at[j], sems, 3 * i + j, (px, py, mc))
                  for i in range(n) for j, (px, py) in enumerate(peers)]
        for j in range(1, 8):
            dev = ((1 - mx) if (j & 4) else mx, (1 - my) if (j & 2) else my, (1 - mc) if (j & 1) else mc)
            copies.append(_remote(s_ref, a_ref.at[me8], sems, 3 * n + j - 1, dev))
        _run_copies(copies)
        own.wait()

    outs = [_sds((3,) + g.shape[1:], g.dtype) for g in gs] + [_sds((8,) + small.shape, small.dtype)]
    res = pl.pallas_call(
        body, name="comm_reduce_ici", in_specs=[ANY] * (n + 1), out_specs=[ANY] * (n + 1), out_shape=outs,
        scratch_shapes=[pltpu.SemaphoreType.DMA((3 * n + 7,)), pltpu.SemaphoreType.DMA((3 * n + 7,)),
                        pltpu.SemaphoreType.DMA])(*gs, small)
    return res[:n], res[n]


def _swap_d2d(name, ps, other_layer):
    n = len(ps)

    def body(*refs):
        p_refs, o_refs, sems = refs[:n], refs[n:2 * n], refs[2 * n:]
        mx, my, mc, _ = _place()
        _run_copies([_remote(p_refs[i].at[1 - mc] if other_layer else p_refs[i], o_refs[i], sems, i, (mx, my, 1 - mc))
                     for i in range(n)])

    outs = [_sds(p.shape[1:] if other_layer else p.shape, p.dtype) for p in ps]
    return _copy_call(body, name, ps, outs, n)


PACK_W = 1024


def _pack(arrs, rows_multiple, dtype):
    flat = jnp.concatenate([a.reshape(-1).astype(dtype) for a in arrs])
    n = flat.shape[0]
    unit = PACK_W * rows_multiple
    tot = -(-n // unit) * unit
    flat = jnp.pad(flat, (0, tot - n))
    return flat.reshape(tot // PACK_W, PACK_W)


def _unpack(flat, shapes):
    flat = flat.reshape(-1)
    out = []
    off = 0
    for s in shapes:
        n = int(np.prod(s))
        out.append(flat[off:off + n].reshape(s))
        off += n
    return out


def _rope_tables(pos):
    l = pos.shape[0]
    tm = min(512, l)
    inv = (np.float32(ROPE_THETA) ** (-np.arange(0, D_ROPE, 2, dtype=np.float32) / np.float32(D_ROPE))).astype(np.float32)
    lane_f = np.zeros((1, LANE), np.float32)
    lane_f[0, D_NOPE:D_NOPE + 16] = inv
    lane_f[0, D_NOPE + 16:D_NOPE + 32] = inv

    def fn(i, p, f):
        ang = p * f
        lane = lax.broadcasted_iota(jnp.int32, ang.shape, 1)
        co = jnp.cos(ang)
        si = jnp.sin(ang)
        c = jnp.where(lane < D_NOPE, 1.0, jnp.where(lane < D_QK, co, 0.0))
        s1 = jnp.where((lane >= D_NOPE) & (lane < D_NOPE + 16), -si, 0.0)
        s2 = jnp.where((lane >= D_NOPE + 16) & (lane < D_QK), si, 0.0)
        return c, s1, s2

    return _rows(fn, name="rope_tables", n=l // tm, ins=[pos, jnp.asarray(lane_f)],
                 in_specs=[_rt(tm, 1), _full((1, LANE))], outs=[_sds((l, LANE))] * 3, out_specs=[_rt(tm, LANE)] * 3)


def _ssm_param_fn(lr, li, log_dt, br, bi):
    dt = jnp.exp(log_dt)
    mag = jnp.exp(lr * dt)
    a_re = mag * jnp.cos(li * dt)
    a_im = mag * jnp.sin(li * dt)
    den = lr * lr + li * li
    e_re = a_re - 1.0
    e_im = a_im
    f_re = (e_re * lr + e_im * li) / den
    f_im = (e_im * lr - e_re * li) / den
    bb_re = f_re[None] * br - f_im[None] * bi
    bb_im = f_re[None] * bi + f_im[None] * br
    return a_re, a_im, bb_re, bb_im


def _ssm_params(name, lr, li, log_dt, br, bi):
    g, n = lr.shape
    c = br.shape[0]
    return _rows(lambda i, *v: _ssm_param_fn(*v), name=name, n=1, ins=[lr, li, log_dt, br, bi],
                 in_specs=[_full((g, n)), _full((g, n)), _full((g, 1)), _full((c, g, n)), _full((c, g, n))],
                 outs=[_sds((g, n)), _sds((g, n)), _sds((c, g, n)), _sds((c, g, n))],
                 out_specs=[_full((g, n)), _full((g, n)), _full((c, g, n)), _full((c, g, n))])


def _ssm_params_bwd(name, lr, li, log_dt, br, bi, d_are, d_aim, d_bbre, d_bbim):
    g, n = lr.shape
    c = br.shape[0]

    def fn(i, lr, li, log_dt, br, bi, g0, g1, g2, g3):
        _, vjp = jax.vjp(_ssm_param_fn, lr, li, log_dt, br, bi)
        return vjp((g0, g1, g2, g3))

    sp = [_full((g, n)), _full((g, n)), _full((g, 1)), _full((c, g, n)), _full((c, g, n))]
    return _rows(fn, name=name, n=1, ins=[lr, li, log_dt, br, bi, d_are, d_aim, d_bbre, d_bbim],
                 in_specs=sp + [_full((g, n)), _full((g, n)), _full((c, g, n)), _full((c, g, n))],
                 outs=[_sds((g, n)), _sds((g, n)), _sds((g, 1)), _sds((c, g, n)), _sds((c, g, n))], out_specs=sp)


_EYE8 = np.eye(8, dtype=np.float32)


def _blockdiag(v):
    j, g, p, q = v.shape
    m = v[:, :, :, None, :] * jnp.asarray(_EYE8)[None, :, None, :, None]
    return m.reshape(j, g * p, g * q)


def _blockdiag_t(m, p, q):
    j = m.shape[0]
    m = m.reshape(j, 8, p, 8, q)
    return jnp.sum(m * jnp.asarray(_EYE8)[None, :, None, :, None], axis=3)


def _to_perm(v, l):
    ns = SCAN_SEGS
    return v.reshape(ns, l // ns, v.shape[-1]).transpose(1, 0, 2).reshape(l, v.shape[-1])


def _from_perm(v, l):
    ns = SCAN_SEGS
    return v.reshape(l // ns, ns, v.shape[-1]).transpose(1, 0, 2).reshape(l, v.shape[-1])


def _prep_layer(w, i):
    p = {}
    w_in = w['w_in'][i]
    z = lambda n: jnp.zeros((D_MODEL, n), w_in.dtype)
    o = Q_LORA + KV_LORA
    p['w_s'] = jnp.concatenate([w_in[:, :o], z(D_NOPE), w_in[:, o:o + D_ROPE], z(HEAD_PAD - D_QK)], axis=1)
    o += D_ROPE
    p['w_u'] = w_in[:, o:o + SSM_WIDTH]
    o += SSM_WIDTH
    p['w_xq'] = w_in[:, o:o + X_WIDTH]
    o += X_WIDTH
    p['w_g'] = w_in[:, o:]
    wq = w['w_q_b'][i].reshape(Q_LORA, MLA_HEADS, D_QK)
    p['w_qb'] = jnp.pad(wq, ((0, 0), (0, 0), (0, HEAD_PAD - D_QK))).reshape(Q_LORA, MLA_PAD)
    wkv = w['w_kv_b'][i].reshape(KV_LORA, MLA_HEADS, D_NOPE + D_V)
    p['w_k'] = jnp.pad(wkv[:, :, :D_NOPE], ((0, 0), (0, 0), (0, HEAD_PAD - D_NOPE))).reshape(KV_LORA, MLA_PAD)
    p['w_v'] = jnp.pad(wkv[:, :, D_NOPE:], ((0, 0), (0, 0), (0, HEAD_PAD - D_V))).reshape(KV_LORA, MLA_PAD)
    wo = w['w_o_mla'][i].reshape(MLA_HEADS, D_V, D_MODEL)
    p['w_oa'] = jnp.pad(wo, ((0, 0), (0, HEAD_PAD - D_V), (0, 0))).reshape(MLA_PAD, D_MODEL)
    for n in ('w_glu', 'w_o_ssm', 'w_mem_kv', 'w_o_cross', 'w_out', 'w_up', 'w_down'):
        p[n] = w[n][i]
    p['conv_w'] = w['conv_w'][i]
    for n in ('norm_mix_g', 'q_a_norm_g', 'kv_a_norm_g', 'b_glu', 'mem_norm_g', 'xq_norm_g', 'xk_norm_g', 'b_gate',
              'norm_ffn_g', 'conv_b'):
        p[n] = w[n][i].reshape(1, -1)
    p['q_norm_g'] = jnp.pad(w['q_norm_g'][i], (0, HEAD_PAD - D_QK)).reshape(1, HEAD_PAD)
    p['k_norm_g'] = jnp.pad(w['k_norm_g'][i], (0, HEAD_PAD - D_QK)).reshape(1, HEAD_PAD)
    p['ssm_d'] = w['ssm_d'][i].reshape(1, SSM_WIDTH)
    p['lr'] = w['ssm_lambda_re'][i]
    p['li'] = w['ssm_lambda_im'][i]
    p['log_dt'] = w['ssm_log_dt'][i].reshape(SSM_GROUPS, 1)
    p['br'] = w['ssm_b_re'][i].transpose(2, 0, 1)
    p['bi'] = w['ssm_b_im'][i].transpose(2, 0, 1)
    cr = w['ssm_c_re'][i].reshape(SSM_JB, 8, SSM_GROUP_CH, SSM_STATE).transpose(0, 1, 3, 2)
    ci = w['ssm_c_im'][i].reshape(SSM_JB, 8, SSM_GROUP_CH, SSM_STATE).transpose(0, 1, 3, 2)
    p['c_mat'] = jnp.concatenate([_blockdiag(cr), -_blockdiag(ci)], axis=1).astype(BF16)
    return p


def _b_mat(bb_re, bb_im):
    r = bb_re.transpose(1, 0, 2).reshape(SSM_JB, 8, SSM_GROUP_CH, SSM_STATE)
    i = bb_im.transpose(1, 0, 2).reshape(SSM_JB, 8, SSM_GROUP_CH, SSM_STATE)
    return jnp.concatenate([_blockdiag(r), _blockdiag(i)], axis=2).astype(BF16)


def _qkv_fn(ps, c, s1, s2, qag, wqb, kvag, wk, wv, qng, kng):
    c_q = ps[:, :Q_LORA]
    c_kv = ps[:, Q_LORA:Q_LORA + KV_LORA]
    kr = ps[:, Q_LORA + KV_LORA:]
    cqn = _rms(c_q, qag, Q_LORA)
    ckvn = _rms(c_kv, kvag, KV_LORA)
    q_raw = _mm(cqn, wqb)
    k_raw = _mm(ckvn, wk) + jnp.concatenate([kr] * MLA_HEADS, axis=-1)
    v = _mm(ckvn, wv)
    q = _heads(_rope, _head_rms(q_raw, qng, MLA_HEADS, D_QK), MLA_HEADS, c, s1, s2)
    k = _heads(_rope, _head_rms(k_raw, kng, MLA_HEADS, D_QK), MLA_HEADS, c, s1, s2)
    lane = lax.broadcasted_iota(jnp.int32, v.shape, 1)
    v = jnp.where((lane & (LANE - 1)) == D_V, 1.0, v)
    return q * ATT_QSCALE, k, v


def _layer_fwd(name, x, tabs, mem, p):
    l = x.shape[0]
    tm = min(512, l)
    nt = l // tm
    sv = {'x0': x}
    sv['p_g'] = _matmul(name + "_in_g", [(x, p['w_g'])], l, 3 * D_MODEL, rms_gain=p['norm_mix_g'])
    sv['p_u'] = _matmul(name + "_in_u", [(x, p['w_u'])], l, SSM_WIDTH, rms_gain=p['norm_mix_g'])
    sv['p_xq'] = _matmul(name + "_in_xq", [(x, p['w_xq'])], l, X_WIDTH, rms_gain=p['norm_mix_g'])
    sv['p_s'] = _matmul(name + "_in_s", [(x, p['w_s'])], l, SMALL_W, rms_gain=p['norm_mix_g'])

    qkv_consts = [p['q_a_norm_g'], p['w_qb'], p['kv_a_norm_g'], p['w_k'], p['w_v'], p['q_norm_g'], p['k_norm_g']]
    qkv_cspecs = [_full(a.shape) for a in qkv_consts]
    def qkv_fwd(i, *a):
        qv, kv, vv = _qkv_fn(*a)
        return qv, kv, vv, jnp.transpose(kv), jnp.transpose(vv)

    q, k, v, k_t, v_t = _rows(qkv_fwd, name=name + "_qkv", n=nt, ins=[sv['p_s'], *tabs, *qkv_consts],
                              in_specs=[_rt(tm, SMALL_W)] + [_rt(tm, LANE)] * 3 + qkv_cspecs,
                              outs=[_sds((l, MLA_PAD), BF16)] * 3 + [_sds((MLA_PAD, l), BF16)] * 2,
                              out_specs=[_rt(tm, MLA_PAD)] * 3 + [pl.BlockSpec((MLA_PAD, tm), lambda i: (0, i))] * 2)
    sv['q'], sv['k'], sv['v'], sv['k_t'] = q, k, v, k_t
    sv['o_a'], sv['lse_t'] = _flash_fwd(name + "_attn", q, k, v_t)

    a_re, a_im, bb_re, bb_im = _ssm_params(name + "_ssm_par", p['lr'], p['li'], p['log_dt'], p['br'], p['bi'])
    sv['a_re'], sv['a_im'] = a_re.reshape(1, SSM_LANES), a_im.reshape(1, SSM_LANES)
    sv['b_mat'] = _b_mat(bb_re, bb_im)
    u_p = _to_perm(sv['p_u'], l)
    sv['u_p'] = u_p

    def bu_fn(i, u, bm):
        res = [_mm(u[:, j * LANE:(j + 1) * LANE], bm[j]) for j in range(SSM_JB)]
        return (jnp.concatenate([r[:, :512] for r in res], axis=-1), jnp.concatenate([r[:, 512:] for r in res], axis=-1))

    ts = min(256, l)
    bu_re, bu_im = _rows(bu_fn, name=name + "_ssm_bu", n=l // ts, ins=[u_p, sv['b_mat']],
                         in_specs=[_rt(ts, SSM_WIDTH), _full(sv['b_mat'].shape)],
                         outs=[_sds((l, SSM_LANES), SSM_STATE_DTYPE)] * 2, out_specs=[_rt(ts, SSM_LANES)] * 2)
    s_re, s_im = _scan(name + "_ssm_scan", bu_re, bu_im, sv['a_re'], sv['a_im'], reverse=False)
    sv['s_re'], sv['s_im'] = s_re, s_im

    def glu_fn(i, sr, si, u, cm, dsk, wg, bg):
        y = jnp.concatenate([_mm(jnp.concatenate([sr[:, j * 512:(j + 1) * 512], si[:, j * 512:(j + 1) * 512]], axis=-1),
                                 cm[j]) for j in range(SSM_JB)], axis=-1) + dsk * u
        zz = _gelu(y)
        return zz * jax.nn.sigmoid(_mm(zz, wg) + bg)

    glu_consts = [p['c_mat'], p['ssm_d'], p['w_glu'], p['b_glu']]
    zo_p = _rows(glu_fn, name=name + "_ssm_glu", n=l // ts, ins=[s_re, s_im, u_p, *glu_consts],
                 in_specs=[_rt(ts, SSM_LANES), _rt(ts, SSM_LANES), _rt(ts, SSM_WIDTH)] + [_full(a.shape) for a in glu_consts],
                 outs=[_sds((l, SSM_WIDTH), BF16)], out_specs=[_rt(ts, SSM_WIDTH)])[0]
    sv['zo'] = _from_perm(zo_p, l)

    m_len = mem.shape[0]

    def memkv_fn(i, mm_, mg, wmk, xkg):
        kv = _mm(_rms(mm_, mg, D_MODEL), wmk)
        return _head_rms(kv[:, :X_WIDTH], xkg, X_HEADS, X_HEAD_DIM), kv[:, X_WIDTH:]

    mem_consts = [p['mem_norm_g'], p['w_mem_kv'], p['xk_norm_g']]
    k_c, v_c = _rows(memkv_fn, name=name + "_memkv", n=1, ins=[mem, *mem_consts],
                     in_specs=[_full(mem.shape)] + [_full(a.shape) for a in mem_consts],
                     outs=[_sds((m_len, X_WIDTH))] * 2, out_specs=[_full((m_len, X_WIDTH))] * 2)
    sv['k_c'], sv['v_c'] = k_c, v_c

    def cross_fn(i, xq, kc, vc, xqg):
        outs = []
        for h in range(X_HEADS):
            sl = slice(h * LANE, (h + 1) * LANE)
            qh = _rms(xq[:, sl], xqg, X_HEAD_DIM)
            s = _mm_nt(qh, kc[:, sl]) * (X_HEAD_DIM ** -0.5)
            s = s - jnp.max(s, axis=-1, keepdims=True)
            e = jnp.exp(s)
            pr = e / jnp.sum(e, axis=-1, keepdims=True)
            outs.append(_mm(pr, vc[:, sl]))
        return jnp.concatenate(outs, axis=-1)

    sv['o_c'] = _rows(cross_fn, name=name + "_cross", n=nt, ins=[sv['p_xq'], k_c, v_c, p['xq_norm_g']],
                      in_specs=[_rt(tm, X_WIDTH), _full(k_c.shape), _full(v_c.shape), _full((1, LANE))],
                      outs=[_sds((l, X_WIDTH), BF16)], out_specs=[_rt(tm, X_WIDTH)])[0]

    def merge_fn(i, oa, zo, oc, pg, x0, woa, wos, woc, bg, wout):
        gates = jax.nn.sigmoid(pg + bg)
        merged = (gates[:, :D_MODEL] * _mm(oa, woa) + gates[:, D_MODEL:2 * D_MODEL] * _mm(zo, wos)
                  + gates[:, 2 * D_MODEL:] * _mm(oc, woc))
        return x0 + _mm(merged, wout), merged

    merge_consts = [p['w_oa'], p['w_o_ssm'], p['w_o_cross'], p['b_gate'], p['w_out']]
    tg = min(256, l)
    x1, merged = _rows(merge_fn, name=name + "_merge", n=l // tg, ins=[sv['o_a'], sv['zo'], sv['o_c'], sv['p_g'], x, *merge_consts],
                       in_specs=[_rt(tg, MLA_PAD), _rt(tg, SSM_WIDTH), _rt(tg, X_WIDTH), _rt(tg, 3 * D_MODEL), _rt(tg, D_MODEL)]
                       + [_full(a.shape) for a in merge_consts],
                       outs=[_sds((l, D_MODEL)), _sds((l, D_MODEL), BF16)], out_specs=[_rt(tg, D_MODEL)] * 2)
    sv['x1'], sv['merged'] = x1, merged

    up = _matmul(name + "_up", [(x1, p['w_up'])], l, 2 * D_FF, rms_gain=p['norm_ffn_g'])
    sv['up'] = up
    tc = min(128, l)

    def conv_fn(i, upt, halo, cw, cb):
        upc = _conv(i, upt, halo, cw) + cb
        return _silu(upc[:, :D_FF]) * upc[:, D_FF:]

    act = _rows(conv_fn, name=name + "_conv", n=l // tc, ins=[up, up, p['conv_w'], p['conv_b']],
                in_specs=[_rt(tc, 2 * D_FF), _halo_prev(tc, 2 * D_FF), _full((3, 2 * D_FF)), _full((1, 2 * D_FF))],
                outs=[_sds((l, D_FF), BF16)], out_specs=[_rt(tc, D_FF)])[0]
    sv['act'] = act
    x2 = _matmul(name + "_down", [(act, p['w_down'])], l, D_MODEL, resid=x1)
    return x2, sv


def _halo_prev(tm, w):
    return pl.BlockSpec((8, w), lambda i: (jnp.maximum(i * (tm // 8) - 1, 0), 0))


def _halo_next(tm, w, n_tiles):
    last = n_tiles * (tm // 8) - 1
    return pl.BlockSpec((8, w), lambda i: (jnp.minimum((i + 1) * (tm // 8), last), 0))


def _conv(i, tile, halo, cw):
    halo = jnp.where(i > 0, halo, 0.0)
    ext = jnp.concatenate([halo, tile], axis=0)
    n = ext.shape[0]
    x1 = pltpu.roll(ext, 1, 0)[8:]
    x2 = pltpu.roll(ext, 2, 0)[8:]
    del n
    return cw[0:1] * x2 + cw[1:2] * x1 + cw[2:3] * tile


def _layer_bwd(name, dx2, sv, tabs, mem, p):
    l = dx2.shape[0]
    tm = min(512, l)
    nt = l // tm
    g = {}
    x1 = sv['x1']
    dact = _matmul(name + "_b_down", [(dx2, p['w_down'])], l, D_FF, nt=True)
    g['w_down'] = _matmul_tn(name + "_gw_down", sv['act'], dx2)
    tc = min(128, l)
    ntc = l // tc

    def conv_b(i, upt, up_prev, up_next, da, da_next, cw, cb):
        up_prev = jnp.where(i > 0, up_prev, 0.0)
        da_next = jnp.where(i < ntc - 1, da_next, 0.0)
        ext = jnp.concatenate([up_prev, upt, up_next], axis=0)
        x0 = ext[8:]
        xm1 = pltpu.roll(ext, 1, 0)[8:]
        xm2 = pltpu.roll(ext, 2, 0)[8:]
        upc = cw[0:1] * xm2 + cw[1:2] * xm1 + cw[2:3] * x0 + cb
        _, vjp = jax.vjp(lambda a, b: _silu(a) * b, upc[:, :D_FF], upc[:, D_FF:])
        dg, dv = vjp(jnp.concatenate([da, da_next], axis=0))
        dupc = jnp.concatenate([dg, dv], axis=-1)
        n = dupc.shape[0]
        dup = cw[2:3] * dupc[:tc] + cw[1:2] * pltpu.roll(dupc, n - 1, 0)[:tc] + cw[0:1] * pltpu.roll(dupc, n - 2, 0)[:tc]
        dt = dupc[:tc]
        dcw = _row_select([_colsum(dt * xm2[:tc]), _colsum(dt * xm1[:tc]), _colsum(dt * upt)], 8)
        return dup, dcw, _colsum(dt)

    dup, g_cw, g_cb = _rows(
        conv_b, name=name + "_b_conv", n=ntc, ins=[sv['up'], sv['up'], sv['up'], dact, dact, p['conv_w'], p['conv_b']],
        in_specs=[_rt(tc, 2 * D_FF), _halo_prev(tc, 2 * D_FF), _halo_next(tc, 2 * D_FF, ntc), _rt(tc, D_FF),
                  _halo_next(tc, D_FF, ntc), _full((3, 2 * D_FF)), _full((1, 2 * D_FF))],
        outs=[_sds((l, 2 * D_FF)), _sds((8, 2 * D_FF)), _sds((1, 2 * D_FF))],
        out_specs=[_rt(tc, 2 * D_FF), _full((8, 2 * D_FF)), _full((1, 2 * D_FF))], n_acc=2, vmem=56)
    g['conv_w'] = g_cw[:3]
    g['conv_b'] = g_cb
    dh2 = _matmul(name + "_b_up", [(dup, p['w_up'])], l, D_MODEL, nt=True, tm=256)
    g['w_up'] = _matmul_tn(name + "_gw_up", x1, dup, rms_gain=p['norm_ffn_g'])

    def norm_b(i, xv, dh, dres, gn):
        _, vjp = jax.vjp(lambda a, b: _rms(a, b, D_MODEL), xv, gn)
        dxv, dgn = vjp(dh)
        return dres + dxv, dgn

    dx1, g['norm_ffn_g'] = _rows(norm_b, name=name + "_b_norm2", n=nt, ins=[x1, dh2, dx2, p['norm_ffn_g']],
                                 in_specs=[_rt(tm, D_MODEL)] * 3 + [_full((1, D_MODEL))],
                                 outs=[_sds((l, D_MODEL)), _sds((1, D_MODEL))], out_specs=[_rt(tm, D_MODEL), _full((1, D_MODEL))],
                                 n_acc=1)

    tg = min(256, l)

    def merge_b(i, dx, oa, zo, oc, pg, woa, wos, woc, bg, wout):
        dm = _mm_nt(dx, wout)
        gates = jax.nn.sigmoid(pg + bg)
        ys = [_mm(oa, woa), _mm(zo, wos), _mm(oc, woc)]
        dys, dpg = [], []
        for b in range(3):
            gb = gates[:, b * D_MODEL:(b + 1) * D_MODEL]
            dys.append(dm * gb)
            dpg.append(dm * ys[b] * gb * (1.0 - gb))
        dpg = jnp.concatenate(dpg, axis=-1)
        return (_mm_nt(dys[0], woa), _mm_nt(dys[1], wos), _mm_nt(dys[2], woc), dpg, dys[0], dys[1], dys[2], _colsum(dpg))

    merge_consts = [p['w_oa'], p['w_o_ssm'], p['w_o_cross'], p['b_gate'], p['w_out']]
    (do_a, dzo, do_c, dp_g, dy_a, dy_b, dy_c, g['b_gate']) = _rows(
        merge_b, name=name + "_b_merge", n=l // tg, ins=[dx1, sv['o_a'], sv['zo'], sv['o_c'], sv['p_g'], *merge_consts],
        in_specs=[_rt(tg, D_MODEL), _rt(tg, MLA_PAD), _rt(tg, SSM_WIDTH), _rt(tg, X_WIDTH), _rt(tg, 3 * D_MODEL)]
        + [_full(a.shape) for a in merge_consts],
        outs=[_sds((l, MLA_PAD)), _sds((l, SSM_WIDTH)), _sds((l, X_WIDTH)), _sds((l, 3 * D_MODEL)),
              _sds((l, D_MODEL), BF16), _sds((l, D_MODEL), BF16), _sds((l, D_MODEL), BF16), _sds((1, 3 * D_MODEL))],
        out_specs=[_rt(tg, MLA_PAD), _rt(tg, SSM_WIDTH), _rt(tg, X_WIDTH), _rt(tg, 3 * D_MODEL),
                   _rt(tg, D_MODEL), _rt(tg, D_MODEL), _rt(tg, D_MODEL), _full((1, 3 * D_MODEL))], n_acc=1, vmem=56)
    g['w_out'] = _matmul_tn(name + "_gw_out", sv['merged'], dx1)
    g['w_oa'] = _matmul_tn(name + "_gw_oa", sv['o_a'], dy_a)
    g['w_o_ssm'] = _matmul_tn(name + "_gw_os", sv['zo'], dy_b)
    g['w_o_cross'] = _matmul_tn(name + "_gw_oc", sv['o_c'], dy_c)

    k_c, v_c = sv['k_c'], sv['v_c']
    m_len = k_c.shape[0]

    def cross_b(i, xq, do, kc, vc, xqg):
        dxq, dk, dv = [], [], []
        dg = jnp.zeros((1, LANE), F32)
        for h in range(X_HEADS):
            sl = slice(h * LANE, (h + 1) * LANE)
            qh, vjp = jax.vjp(lambda a, b: _rms(a, b, X_HEAD_DIM), xq[:, sl], xqg)
            sc = X_HEAD_DIM ** -0.5
            s = _mm_nt(qh, kc[:, sl]) * sc
            s = s - jnp.max(s, axis=-1, keepdims=True)
            e = jnp.exp(s)
            pr = e / jnp.sum(e, axis=-1, keepdims=True)
            doh = do[:, sl]
            dv.append(_mm_tn(pr, doh))
            dp = _mm_nt(doh, vc[:, sl])
            ds = pr * (dp - jnp.sum(dp * pr, axis=-1, keepdims=True)) * sc
            dk.append(_mm_tn(ds, qh))
            dxh, dgh = vjp(_mm(ds, kc[:, sl]))
            dxq.append(dxh)
            dg = dg + dgh
        return jnp.concatenate(dxq, axis=-1), jnp.concatenate(dk, axis=-1), jnp.concatenate(dv, axis=-1), dg

    dp_xq, dk_c, dv_c, g['xq_norm_g'] = _rows(
        cross_b, name=name + "_b_cross", n=nt, ins=[sv['p_xq'], do_c, k_c, v_c, p['xq_norm_g']],
        in_specs=[_rt(tm, X_WIDTH), _rt(tm, X_WIDTH), _full(k_c.shape), _full(v_c.shape), _full((1, LANE))],
        outs=[_sds((l, X_WIDTH)), _sds((m_len, X_WIDTH)), _sds((m_len, X_WIDTH)), _sds((1, LANE))],
        out_specs=[_rt(tm, X_WIDTH), _full((m_len, X_WIDTH)), _full((m_len, X_WIDTH)), _full((1, LANE))], n_acc=3)

    def memkv_b(i, mm_, dk, dv, mg, wmk, xkg):
        memn, vjp_n = jax.vjp(lambda a, b: _rms(a, b, D_MODEL), mm_, mg)
        kv = _mm(memn, wmk)
        _, vjp_k = jax.vjp(lambda a, b: _head_rms(a, b, X_HEADS, X_HEAD_DIM), kv[:, :X_WIDTH], xkg)
        dkr, dxkg = vjp_k(dk)
        dkv = jnp.concatenate([dkr, dv], axis=-1)
        _, dmg = vjp_n(_mm_nt(dkv, wmk))
        return _mm_tn(memn, dkv), dmg, dxkg

    mem_consts = [p['mem_norm_g'], p['w_mem_kv'], p['xk_norm_g']]
    g['w_mem_kv'], g['mem_norm_g'], g['xk_norm_g'] = _rows(
        memkv_b, name=name + "_b_memkv", n=1, ins=[mem, dk_c, dv_c, *mem_consts],
        in_specs=[_full(mem.shape), _full(dk_c.shape), _full(dv_c.shape)] + [_full(a.shape) for a in mem_consts],
        outs=[_sds((D_MODEL, 2 * X_WIDTH)), _sds((1, D_MODEL)), _sds((1, LANE))],
        out_specs=[_full((D_MODEL, 2 * X_WIDTH)), _full((1, D_MODEL)), _full((1, LANE))])

    u_p = sv['u_p']
    dzo_p = _to_perm(dzo, l)
    s_re, s_im = sv['s_re'], sv['s_im']

    def glu_b(i, sr, si, u, dz, cm, dsk, wg, bg):
        cats = [jnp.concatenate([sr[:, j * 512:(j + 1) * 512], si[:, j * 512:(j + 1) * 512]], axis=-1) for j in range(SSM_JB)]
        y = jnp.concatenate([_mm(cats[j], cm[j]) for j in range(SSM_JB)], axis=-1) + dsk * u
        zz, vjp_g = jax.vjp(_gelu, y)
        t = _mm(zz, wg) + bg
        sg = jax.nn.sigmoid(t)
        dt = dz * zz * sg * (1.0 - sg)
        dzz = dz * sg + _mm_nt(dt, wg)
        dy = vjp_g(dzz)[0]
        dss = [_mm_nt(dy[:, j * LANE:(j + 1) * LANE], cm[j]) for j in range(SSM_JB)]
        dsr = jnp.concatenate([d[:, :512] for d in dss], axis=-1)
        dsi = jnp.concatenate([d[:, 512:] for d in dss], axis=-1)
        dcm = jnp.stack([_mm_tn(cats[j], dy[:, j * LANE:(j + 1) * LANE]) for j in range(SSM_JB)], axis=0)
        return dsr, dsi, dy * dsk, dcm, _colsum(dy * u), _mm_tn(zz, dt), _colsum(dt)

    glu_consts = [p['c_mat'], p['ssm_d'], p['w_glu'], p['b_glu']]
    ts = min(256, l)
    nts = l // ts
    ds_re, ds_im, du_dir, g['c_mat'], g['ssm_d'], g['w_glu'], g['b_glu'] = _rows(
        glu_b, name=name + "_b_glu", n=nts, ins=[s_re, s_im, u_p, dzo_p, *glu_consts],
        in_specs=[_rt(ts, SSM_LANES), _rt(ts, SSM_LANES), _rt(ts, SSM_WIDTH), _rt(ts, SSM_WIDTH)] + [_full(a.shape) for a in glu_consts],
        outs=[_sds((l, SSM_LANES), SSM_STATE_DTYPE), _sds((l, SSM_LANES), SSM_STATE_DTYPE), _sds((l, SSM_WIDTH)),
              _sds((SSM_JB, 1024, LANE)), _sds((1, SSM_WIDTH)),
              _sds((SSM_WIDTH, SSM_WIDTH)), _sds((1, SSM_WIDTH))],
        out_specs=[_rt(ts, SSM_LANES), _rt(ts, SSM_LANES), _rt(ts, SSM_WIDTH), _full((SSM_JB, 1024, LANE)), _full((1, SSM_WIDTH)),
                   _full((SSM_WIDTH, SSM_WIDTH)), _full((1, SSM_WIDTH))], n_acc=4)
    gb_re, gb_im = _scan(name + "_b_scan", ds_re, ds_im, sv['a_re'], -sv['a_im'], reverse=True)
    ns = SCAN_SEGS
    last_blk = l // ns - 1

    def da_fn(i, *vals):
        gr, gi, sr, si, hr, hi, lr_, li_ = [v.astype(F32) for v in vals]
        rid = lax.broadcasted_iota(jnp.int32, lr_.shape, 0)
        fr = jnp.where(rid == 0, 0.0, pltpu.roll(lr_, 1, 0))
        fi = jnp.where(rid == 0, 0.0, pltpu.roll(li_, 1, 0))
        hr = jnp.where(i == 0, fr, hr)
        hi = jnp.where(i == 0, fi, hi)
        if ts > ns:
            pr = jnp.concatenate([hr, sr[:ts - ns]], axis=0)
            pi = jnp.concatenate([hi, si[:ts - ns]], axis=0)
        else:
            pr, pi = hr, hi
        return _colsum(gr * pr + gi * pi), _colsum(gi * pr - gr * pi)

    hprev = pl.BlockSpec((ns, SSM_LANES), lambda i: (jnp.maximum(i * (ts // ns) - 1, 0), 0))
    hlast = pl.BlockSpec((ns, SSM_LANES), lambda i: (last_blk, 0))
    da_re, da_im = _rows(da_fn, name=name + "_b_da", n=nts, ins=[gb_re, gb_im, s_re, s_im, s_re, s_im, s_re, s_im],
                         in_specs=[_rt(ts, SSM_LANES)] * 4 + [hprev, hprev, hlast, hlast],
                         outs=[_sds((1, SSM_LANES))] * 2, out_specs=[_full((1, SSM_LANES))] * 2, n_acc=2)

    def bu_b(i, dbr, dbi, u, dud, bm):
        dus, dbm = [], []
        for j in range(SSM_JB):
            cat = jnp.concatenate([dbr[:, j * 512:(j + 1) * 512], dbi[:, j * 512:(j + 1) * 512]], axis=-1)
            dus.append(_mm_nt(cat, bm[j]))
            dbm.append(_mm_tn(u[:, j * LANE:(j + 1) * LANE], cat))
        return dud + jnp.concatenate(dus, axis=-1), jnp.stack(dbm, axis=0)

    du_p, d_bmat = _rows(bu_b, name=name + "_b_bu", n=nts, ins=[gb_re, gb_im, u_p, du_dir, sv['b_mat']],
                         in_specs=[_rt(ts, SSM_LANES), _rt(ts, SSM_LANES), _rt(ts, SSM_WIDTH), _rt(ts, SSM_WIDTH),
                                   _full(sv['b_mat'].shape)],
                         outs=[_sds((l, SSM_WIDTH)), _sds((SSM_JB, LANE, 1024))],
                         out_specs=[_rt(ts, SSM_WIDTH), _full((SSM_JB, LANE, 1024))], n_acc=1)
    dp_u = _from_perm(du_p, l)
    dbb_re = _blockdiag_t(d_bmat[:, :, :512], SSM_GROUP_CH, SSM_STATE).reshape(SSM_GROUPS, SSM_GROUP_CH, SSM_STATE).transpose(1, 0, 2)
    dbb_im = _blockdiag_t(d_bmat[:, :, 512:], SSM_GROUP_CH, SSM_STATE).reshape(SSM_GROUPS, SSM_GROUP_CH, SSM_STATE).transpose(1, 0, 2)
    g['lr'], g['li'], g['log_dt'], g['br'], g['bi'] = _ssm_params_bwd(
        name + "_b_ssm_par", p['lr'], p['li'], p['log_dt'], p['br'], p['bi'],
        da_re.reshape(SSM_GROUPS, SSM_STATE), da_im.reshape(SSM_GROUPS, SSM_STATE), dbb_re, dbb_im)

    dq_t, dk, dv = _flash_bwd(name + "_b_attn", sv['q'], sv['k'], sv['v'], sv['k_t'], sv['o_a'], sv['lse_t'], do_a)

    def qkv_b(i, ps, c, s1, s2, dq_, dk_, dv_, qag, wqb, kvag, wk, wv, qng, kng):
        c_q = ps[:, :Q_LORA]
        c_kv = ps[:, Q_LORA:Q_LORA + KV_LORA]
        kr = ps[:, Q_LORA + KV_LORA:]
        cqn, vjp_cq = jax.vjp(lambda a, b: _rms(a, b, Q_LORA), c_q, qag)
        ckvn, vjp_ckv = jax.vjp(lambda a, b: _rms(a, b, KV_LORA), c_kv, kvag)
        q_raw = _mm(cqn, wqb)
        k_raw = _mm(ckvn, wk) + jnp.concatenate([kr] * MLA_HEADS, axis=-1)
        _, vjp_qn = jax.vjp(lambda a, b: _head_rms(a, b, MLA_HEADS, D_QK), q_raw, qng)
        _, vjp_kn = jax.vjp(lambda a, b: _head_rms(a, b, MLA_HEADS, D_QK), k_raw, kng)
        dq_raw, dqng = vjp_qn(_heads(_rope_t, jnp.transpose(dq_[0]), MLA_HEADS, c, s1, s2))
        dk_raw, dkng = vjp_kn(_heads(_rope_t, dk_, MLA_HEADS, c, s1, s2))
        dkr = dk_raw[:, :LANE]
        for h in range(1, MLA_HEADS):
            dkr = dkr + dk_raw[:, h * LANE:(h + 1) * LANE]
        dcq, dqag = vjp_cq(_mm_nt(dq_raw, wqb))
        dckv, dkvag = vjp_ckv(_mm_nt(dk_raw, wk) + _mm_nt(dv_, wv))
        dps = jnp.concatenate([dcq, dckv, dkr], axis=-1)
        return (dps, _mm_tn(cqn, dq_raw), _mm_tn(ckvn, dk_raw), _mm_tn(ckvn, dv_), dqag, dkvag, dqng, dkng)

    qkv_consts = [p['q_a_norm_g'], p['w_qb'], p['kv_a_norm_g'], p['w_k'], p['w_v'], p['q_norm_g'], p['k_norm_g']]
    (dp_s, g['w_qb'], g['w_k'], g['w_v'], g['q_a_norm_g'], g['kv_a_norm_g'], g['q_norm_g'], g['k_norm_g']) = _rows(
        qkv_b, name=name + "_b_qkv", n=nt, ins=[sv['p_s'], *tabs, dq_t, dk, dv, *qkv_consts],
        in_specs=[_rt(tm, SMALL_W)] + [_rt(tm, LANE)] * 3
        + [pl.BlockSpec((1, MLA_PAD, tm), lambda i: (i // (dq_t.shape[2] // tm), 0, i % (dq_t.shape[2] // tm)))]
        + [_rt(tm, MLA_PAD)] * 2 + [_full(a.shape) for a in qkv_consts],
        outs=[_sds((l, SMALL_W)), _sds((Q_LORA, MLA_PAD)), _sds((KV_LORA, MLA_PAD)), _sds((KV_LORA, MLA_PAD)),
              _sds((1, Q_LORA)), _sds((1, KV_LORA)), _sds((1, LANE)), _sds((1, LANE))],
        out_specs=[_rt(tm, SMALL_W), _full((Q_LORA, MLA_PAD)), _full((KV_LORA, MLA_PAD)), _full((KV_LORA, MLA_PAD)),
                   _full((1, Q_LORA)), _full((1, KV_LORA)), _full((1, LANE)), _full((1, LANE))], n_acc=7)

    x0 = sv['x0']
    dh = _matmul(name + "_b_in", [(dp_g, p['w_g']), (dp_u, p['w_u']), (dp_xq, p['w_xq']), (dp_s, p['w_s'])], l, D_MODEL, nt=True,
                 tm=256)
    gm = p['norm_mix_g']
    g['w_g'] = _matmul_tn(name + "_gw_g", x0, dp_g, rms_gain=gm)
    g['w_u'] = _matmul_tn(name + "_gw_u", x0, dp_u, rms_gain=gm)
    g['w_xq'] = _matmul_tn(name + "_gw_xq", x0, dp_xq, rms_gain=gm)
    g['w_s'] = _matmul_tn(name + "_gw_s", x0, dp_s, rms_gain=gm)
    dx0, g['norm_mix_g'] = _rows(norm_b, name=name + "_b_norm1", n=nt, ins=[x0, dh, dx1, gm],
                                 in_specs=[_rt(tm, D_MODEL)] * 3 + [_full((1, D_MODEL))],
                                 outs=[_sds((l, D_MODEL)), _sds((1, D_MODEL))], out_specs=[_rt(tm, D_MODEL), _full((1, D_MODEL))],
                                 n_acc=1)
    return dx0, g


def _unprep_grads(g):
    o = {}
    ws = g['w_s']
    o['w_in'] = jnp.concatenate([ws[:, :Q_LORA + KV_LORA], ws[:, Q_LORA + KV_LORA + D_NOPE:Q_LORA + KV_LORA + D_QK],
                                 g['w_u'], g['w_xq'], g['w_g']], axis=1)
    o['w_q_b'] = g['w_qb'].reshape(Q_LORA, MLA_HEADS, HEAD_PAD)[:, :, :D_QK].reshape(Q_LORA, MLA_HEADS * D_QK)
    gk = g['w_k'].reshape(KV_LORA, MLA_HEADS, HEAD_PAD)[:, :, :D_NOPE]
    gv = g['w_v'].reshape(KV_LORA, MLA_HEADS, HEAD_PAD)[:, :, :D_V]
    o['w_kv_b'] = jnp.concatenate([gk, gv], axis=2).reshape(KV_LORA, MLA_HEADS * (D_NOPE + D_V))
    o['w_o_mla'] = g['w_oa'].reshape(MLA_HEADS, HEAD_PAD, D_MODEL)[:, :D_V].reshape(MLA_HEADS * D_V, D_MODEL)
    for n in ('w_glu', 'w_o_ssm', 'w_mem_kv', 'w_o_cross', 'w_out', 'w_up', 'w_down', 'conv_w'):
        o[n] = g[n]
    for n in ('norm_mix_g', 'q_a_norm_g', 'kv_a_norm_g', 'b_glu', 'mem_norm_g', 'xq_norm_g', 'xk_norm_g', 'b_gate',
              'norm_ffn_g', 'conv_b'):
        o[n] = g[n].reshape(-1)
    o['q_norm_g'] = g['q_norm_g'].reshape(-1)[:D_QK]
    o['k_norm_g'] = g['k_norm_g'].reshape(-1)[:D_QK]
    o['ssm_d'] = g['ssm_d'].reshape(SSM_GROUPS, SSM_GROUP_CH)
    o['ssm_lambda_re'] = g['lr']
    o['ssm_lambda_im'] = g['li']
    o['ssm_log_dt'] = g['log_dt'].reshape(SSM_GROUPS)
    o['ssm_b_re'] = g['br'].transpose(1, 2, 0)
    o['ssm_b_im'] = g['bi'].transpose(1, 2, 0)
    dc = g['c_mat']
    o['ssm_c_re'] = _blockdiag_t(dc[:, :512], SSM_STATE, SSM_GROUP_CH).transpose(0, 1, 3, 2).reshape(SSM_GROUPS, SSM_GROUP_CH, SSM_STATE)
    o['ssm_c_im'] = -_blockdiag_t(dc[:, 512:], SSM_STATE, SSM_GROUP_CH).transpose(0, 1, 3, 2).reshape(SSM_GROUPS, SSM_GROUP_CH, SSM_STATE)
    return o


def _local_step(x, mem, pos, target, w):
    l = x.shape[0]
    tm = min(512, l)
    tabs = _rope_tables(pos.astype(F32).reshape(l, 1))
    ps = [_prep_layer(w, i) for i in range(DEPTH)]
    saved = []
    h = x
    for i in range(DEPTH):
        h, sv = _layer_fwd("l%d" % i, h, tabs, mem, ps[i])
        saved.append(sv)

    def loss_fn(i, y, t):
        e = y - t
        per_tok = jnp.sum(e * e, axis=-1, keepdims=True) * (1.0 / D_MODEL)
        tot = 0.5 * jnp.sum(per_tok, axis=0, keepdims=True)
        return e * (1.0 / D_MODEL), jnp.broadcast_to(tot, (1, LANE))

    dy, loss = _rows(loss_fn, name="loss", n=l // tm, ins=[h, target], in_specs=[_rt(tm, D_MODEL)] * 2,
                     outs=[_sds((l, D_MODEL)), _sds((1, LANE))], out_specs=[_rt(tm, D_MODEL), _full((1, LANE))], n_acc=1)
    grads = []
    d = dy
    for i in reversed(range(DEPTH)):
        d, g = _layer_bwd("l%d" % i, d, saved[i], tabs, mem, ps[i])
        grads.append(_unprep_grads(g))
    return loss[0, 0], d, grads[::-1]


def _sum_picked(name, slots, pick, extra, out_dtype):
    _, r, c = slots.shape
    e = extra.shape[0]
    tr = _row_tile(r)

    def body(pk, s_ref, x_ref, o_ref):
        acc = s_ref[...].astype(F32)
        for k in range(e):
            acc = acc + x_ref[k].astype(F32)
        o_ref[...] = acc.astype(o_ref.dtype)

    grid_spec = pltpu.PrefetchScalarGridSpec(
        num_scalar_prefetch=1, grid=(r // tr,),
        in_specs=[pl.BlockSpec((None, tr, c), lambda i, pk: (pk[0], i, 0)), pl.BlockSpec((e, tr, c), lambda i, pk: (0, i, 0))],
        out_specs=pl.BlockSpec((tr, c), lambda i, pk: (i, 0)))
    return pl.pallas_call(body, name=name, grid_spec=grid_spec, out_shape=_sds((r, c), out_dtype),
                          compiler_params=_params(("arbitrary",), 48))(pick, slots, extra)


def _row_tile(r):
    for t in (256, 128, 64, 32, 16, 8):
        if r % t == 0:
            return t
    return r


def _adamw(name, parts, w, m, v):
    r, cw = w.shape
    tr = _row_tile(r)
    np_ = len(parts)

    def fn(i, *vals):
        wv, mv, vv = vals[np_:]
        terms = []
        for pv in vals[:np_]:
            terms += [pv] if pv.ndim == 2 else [pv[k] for k in range(pv.shape[0])]
        g = terms[0]
        for t in terms[1:]:
            g = g + t
        mn = ADAM_B1 * mv + (1.0 - ADAM_B1) * g
        vn = ADAM_B2 * vv + (1.0 - ADAM_B2) * (g * g)
        m_hat = mn / (1.0 - ADAM_B1 ** ADAM_STEP)
        v_hat = vn / (1.0 - ADAM_B2 ** ADAM_STEP)
        delta = -ADAM_LR * (m_hat / (jnp.sqrt(v_hat) + ADAM_EPS) + ADAM_WD * wv)
        return g, delta, mn, vn

    pspecs = [_rt(tr, cw) if p.ndim == 2 else pl.BlockSpec((p.shape[0], tr, cw), lambda i: (0, i, 0)) for p in parts]
    return _rows(fn, name=name, n=r // tr, ins=[*parts, w, m, v], in_specs=pspecs + [_rt(tr, cw)] * 3,
                 outs=[_sds((r, cw))] * 4, out_specs=[_rt(tr, cw)] * 4)


def _shard_of(a, axis, k):
    n = a.shape[axis] // 4
    return lax.slice_in_dim(a, k * n, (k + 1) * n, axis=axis)


def _step(a):
    x = a['x'][0]
    mem = a['mem'][0]
    pos = a['positions'][0]
    target = a['loss_target'][0]

    me = 2 * lax.axis_index("x") + lax.axis_index("y")

    mine = [a[n] if n == 'conv_w' else a[n].astype(BF16) for n in SHARDED]
    got = _gather_ici(mine)
    w = {}
    for n, own, y in zip(SHARDED, mine, got):
        ax = SHARD_AXIS[n] - 1
        w[n] = [jnp.concatenate([jnp.where(me == k, own[i], y[k, i]) for k in range(4)], axis=ax) for i in range(DEPTH)]
    for n in SMALL:
        w[n] = a[n]

    loss, grad_x, grads = _local_step(x, mem, pos, target, w)

    mc = lax.axis_index("c")
    mc1 = mc.astype(jnp.int32).reshape(1)
    me1 = me.astype(jnp.int32).reshape(1)
    gsh = []
    for n in SHARDED:
        ax = SHARD_AXIS[n] - 1
        gsh.append(jnp.stack([jnp.stack([_shard_of(grads[i][n], ax, k) for k in range(4)], axis=0)
                              for i in range(DEPTH)], axis=0).astype(BF16))
    sib = _swap_d2d("comm_reduce_pair", gsh, other_layer=True)
    pair = []
    for n, g, s in zip(SHARDED, gsh, sib):
        rows, cols = g.shape[-2:]
        pair.append(_sum_picked("sum2_" + n, g.reshape(DEPTH, 4 * rows, cols), mc1, s.reshape(1, 4 * rows, cols), BF16)
                    .reshape(4, rows, cols))
    gsm = _pack([jnp.stack([grads[i][n] for i in range(DEPTH)], axis=0) for n in SMALL] + [loss.reshape(1)], 8, F32)
    got, alls = _reduce_ici(pair, gsm)
    parts = [_sum_picked("sum4_" + n, p4, me1, g3, F32) for n, p4, g3 in zip(SHARDED, pair, got)]
    others = _swap_d2d("comm_reduce_d2d", parts, other_layer=False)
    res_sh = []
    for n, part, other in zip(SHARDED, parts, others):
        cols = part.shape[-1]
        full = jnp.where(mc == 0, jnp.stack([part, other], axis=0), jnp.stack([other, part], axis=0))
        res = _adamw("adamw_" + n, [full.reshape(-1, cols)], *[a[pre + n].reshape(-1, cols) for pre in ('', 'm_', 'v_')])
        res_sh.append([r.reshape(a[n].shape) for r in res])
    res_sh = [[res_sh[j][kind] for j in range(len(SHARDED))] for kind in range(4)]

    sm_shapes = [a[n].shape for n in SMALL] + [(1,)]
    zero1 = jnp.zeros((1,), F32)
    res_sm = _adamw("adamw_small", [alls], *[_pack([a[pre + n] for n in SMALL] + [zero1], 8, F32) for pre in ('', 'm_', 'v_')])
    res_sm = [_unpack(r, sm_shapes) for r in res_sm]
    loss = res_sm[0][-1][0]

    outs = [loss, grad_x[None]]
    for kind in range(4):
        byname = dict(zip(SHARDED, res_sh[kind]))
        byname.update(zip(SMALL, res_sm[kind]))
        outs += [byname[n] for n in WEIGHTS]
    return tuple(outs)


def kernel(x, mem, positions, norm_mix_g, w_in, q_a_norm_g, w_q_b, kv_a_norm_g, w_kv_b, q_norm_g, k_norm_g, w_o_mla, ssm_lambda_re, ssm_lambda_im, ssm_log_dt, ssm_b_re, ssm_b_im, ssm_c_re, ssm_c_im, ssm_d, w_glu, b_glu, w_o_ssm, mem_norm_g, w_mem_kv, xq_norm_g, xk_norm_g, w_o_cross, b_gate, w_out, norm_ffn_g, w_up, conv_w, conv_b, w_down, loss_target, m_norm_mix_g, m_w_in, m_q_a_norm_g, m_w_q_b, m_kv_a_norm_g, m_w_kv_b, m_q_norm_g, m_k_norm_g, m_w_o_mla, m_ssm_lambda_re, m_ssm_lambda_im, m_ssm_log_dt, m_ssm_b_re, m_ssm_b_im, m_ssm_c_re, m_ssm_c_im, m_ssm_d, m_w_glu, m_b_glu, m_w_o_ssm, m_mem_norm_g, m_w_mem_kv, m_xq_norm_g, m_xk_norm_g, m_w_o_cross, m_b_gate, m_w_out, m_norm_ffn_g, m_w_up, m_conv_w, m_conv_b, m_w_down, v_norm_mix_g, v_w_in, v_q_a_norm_g, v_w_q_b, v_kv_a_norm_g, v_w_kv_b, v_q_norm_g, v_k_norm_g, v_w_o_mla, v_ssm_lambda_re, v_ssm_lambda_im, v_ssm_log_dt, v_ssm_b_re, v_ssm_b_im, v_ssm_c_re, v_ssm_c_im, v_ssm_d, v_w_glu, v_b_glu, v_w_o_ssm, v_mem_norm_g, v_w_mem_kv, v_xq_norm_g, v_xk_norm_g, v_w_o_cross, v_b_gate, v_w_out, v_norm_ffn_g, v_w_up, v_conv_w, v_conv_b, v_w_down):
    return _step(dict(locals()))
```

```python
import functools
import math

import numpy as np
import jax
import jax.numpy as jnp
from jax import lax
from jax.experimental import pallas as pl
from jax.experimental.pallas import tpu as pltpu

F32 = jnp.float32
BF16 = jnp.bfloat16
MESH = pl.DeviceIdType.MESH

DEPTH = 2
D_MODEL = 1024
EPS = 1e-6
MLA_HEADS = 8
Q_LORA = 384
KV_LORA = 256
D_NOPE = 64
D_ROPE = 32
D_QK = D_NOPE + D_ROPE
D_V = 64
HEAD_PAD = 128
MLA_PAD = MLA_HEADS * HEAD_PAD
ROPE_THETA = 10000.0
SSM_GROUPS = 32
SSM_GROUP_CH = 16
SSM_WIDTH = 512
SSM_STATE = 64
SSM_LANES = SSM_GROUPS * SSM_STATE
SSM_JB = 4
X_HEADS = 4
X_HEAD_DIM = 128
X_WIDTH = 512
D_FF = 2816
SMALL_W = Q_LORA + KV_LORA + HEAD_PAD
SCAN_SEGS = 32
SSM_STATE_DTYPE = BF16
LANE = 128
NEG = -1e30

ADAM_LR = 0.001
ADAM_B1 = 0.9
ADAM_B2 = 0.999
ADAM_EPS = 1e-08
ADAM_WD = 0.01
ADAM_STEP = 10

WEIGHTS = ['norm_mix_g', 'w_in', 'q_a_norm_g', 'w_q_b', 'kv_a_norm_g', 'w_kv_b', 'q_norm_g', 'k_norm_g', 'w_o_mla',
           'ssm_lambda_re', 'ssm_lambda_im', 'ssm_log_dt', 'ssm_b_re', 'ssm_b_im', 'ssm_c_re', 'ssm_c_im', 'ssm_d',
           'w_glu', 'b_glu', 'w_o_ssm', 'mem_norm_g', 'w_mem_kv', 'xq_norm_g', 'xk_norm_g', 'w_o_cross', 'b_gate',
           'w_out', 'norm_ffn_g', 'w_up', 'conv_w', 'conv_b', 'w_down']
SHARD_AXIS = {'w_in': 2, 'w_q_b': 2, 'w_kv_b': 2, 'w_o_mla': 2, 'w_glu': 1, 'w_o_ssm': 2, 'w_mem_kv': 1,
              'w_o_cross': 2, 'w_out': 1, 'w_up': 2, 'conv_w': 2, 'w_down': 1}
SHARDED = [n for n in WEIGHTS if n in SHARD_AXIS]
GATHER_BF16 = [n for n in SHARDED if n != 'conv_w']
SMALL = [n for n in WEIGHTS if n not in SHARD_AXIS]


def _bf(v):
    return v.astype(BF16)


def _mm(a, b):
    return jnp.dot(_bf(a), _bf(b), preferred_element_type=F32)


def _mm_nt(a, b):
    return lax.dot_general(_bf(a), _bf(b), (((1,), (1,)), ((), ())), preferred_element_type=F32)


def _mm_tn(a, b):
    return lax.dot_general(_bf(a), _bf(b), (((0,), (0,)), ((), ())), preferred_element_type=F32)


def _rms(v, g, n):
    ms = jnp.sum(v * v, axis=-1, keepdims=True) * (1.0 / n)
    return (v * lax.rsqrt(ms + EPS)) * g


def _head_rms(v, g, heads, n):
    return jnp.concatenate([_rms(v[:, h * LANE:(h + 1) * LANE], g, n) for h in range(heads)], axis=-1)


def _rope(v, c, s1, s2):
    return v * c + pltpu.roll(v, LANE - 16, 1) * s1 + pltpu.roll(v, 16, 1) * s2


def _rope_t(g, c, s1, s2):
    return g * c + pltpu.roll(g * s1, 16, 1) + pltpu.roll(g * s2, LANE - 16, 1)


def _heads(fn, v, heads, *tabs):
    return jnp.concatenate([fn(v[:, h * LANE:(h + 1) * LANE], *tabs) for h in range(heads)], axis=-1)


def _gelu(y):
    return y * (0.5 * (1.0 + jnp.tanh(math.sqrt(2.0 / math.pi) * (y + 0.044715 * (y * y * y)))))


def _silu(g):
    return g * jax.nn.sigmoid(g)


def _colsum(v):
    return jnp.sum(v, axis=0, keepdims=True)


def _row_select(rows, n):
    rid = lax.broadcasted_iota(jnp.int32, (n, rows[0].shape[-1]), 0)
    out = jnp.zeros((n, rows[0].shape[-1]), F32)
    for k, r in enumerate(rows):
        out = jnp.where(rid == k, jnp.broadcast_to(r, out.shape), out)
    return out


def _params(sem, vmem_mb):
    return pltpu.CompilerParams(dimension_semantics=sem, vmem_limit_bytes=vmem_mb * 1024 * 1024)


def _rt(tm, w, cb=0):
    return pl.BlockSpec((tm, w), lambda i: (i, cb))


def _full(shape):
    nd = len(shape)
    return pl.BlockSpec(tuple(shape), lambda i: (0,) * nd)


def _rows(fn, *, name, n, ins, in_specs, outs, out_specs, n_acc=0, vmem=48):
    n_in = len(ins)
    n_out = len(outs)

    def body(*refs):
        i = pl.program_id(0)
        res = fn(i, *[r[...] for r in refs[:n_in]])
        if not isinstance(res, (tuple, list)):
            res = (res,)
        assert len(res) == n_out, (name, len(res), n_out)
        for k, (r, v) in enumerate(zip(refs[n_in:], res)):
            if k < n_out - n_acc:
                r[...] = v.astype(r.dtype)
            else:
                @pl.when(i == 0)
                def _():
                    r[...] = v

                @pl.when(i > 0)
                def _():
                    r[...] += v

    return pl.pallas_call(
        body, name=name, grid=(n,), in_specs=list(in_specs), out_specs=tuple(out_specs), out_shape=tuple(outs),
        compiler_params=_params(("arbitrary",), vmem))(*ins)


def _sds(shape, dtype=F32):
    return jax.ShapeDtypeStruct(tuple(shape), dtype)


def _tile_n(n, cap=1536):
    best = None
    for t in range(LANE, min(n, cap) + 1, LANE):
        if n % t == 0:
            best = t
    if best is None or n <= 1408:
        return n
    return best


def _matmul(name, pairs, m, n, *, nt=False, rms_gain=None, resid=None, out_dtype=F32, tm=1024, vmem=56):
    tm = min(tm, m)
    tn = _tile_n(n)
    ks = [a.shape[1] for a, _ in pairs]
    np_ = len(pairs)

    def body(*refs):
        a_refs = refs[:np_]
        b_refs = refs[np_:2 * np_]
        k = 2 * np_
        g_ref = None
        r_ref = None
        if rms_gain is not None:
            g_ref = refs[k]
            k += 1
        if resid is not None:
            r_ref = refs[k]
            k += 1
        o_ref = refs[k]
        scr = refs[k + 1:]
        j = pl.program_id(1)

        @pl.when(j == 0)
        def _():
            for p in range(np_):
                a = a_refs[p][...]
                if p == 0 and g_ref is not None:
                    a = _rms(a.astype(F32), g_ref[...], ks[0])
                scr[p][...] = a.astype(BF16)

        acc = None
        for p in range(np_):
            b = b_refs[p][...].astype(BF16)
            if nt:
                t = lax.dot_general(scr[p][...], b, (((1,), (1,)), ((), ())), preferred_element_type=F32)
            else:
                t = jnp.dot(scr[p][...], b, preferred_element_type=F32)
            acc = t if acc is None else acc + t
        if r_ref is not None:
            acc = acc + r_ref[...]
        o_ref[...] = acc.astype(o_ref.dtype)

    in_specs = [pl.BlockSpec((tm, kk), lambda i, j: (i, 0)) for kk in ks]
    if nt:
        in_specs += [pl.BlockSpec((tn, kk), lambda i, j: (j, 0)) for kk in ks]
    else:
        in_specs += [pl.BlockSpec((kk, tn), lambda i, j: (0, j)) for kk in ks]
    ins = [a for a, _ in pairs] + [b for _, b in pairs]
    if rms_gain is not None:
        in_specs.append(pl.BlockSpec((1, ks[0]), lambda i, j: (0, 0)))
        ins.append(rms_gain)
    if resid is not None:
        in_specs.append(pl.BlockSpec((tm, tn), lambda i, j: (i, j)))
        ins.append(resid)
    return pl.pallas_call(
        body, name=name, grid=(m // tm, n // tn), in_specs=in_specs,
        out_specs=pl.BlockSpec((tm, tn), lambda i, j: (i, j)), out_shape=_sds((m, n), out_dtype),
        scratch_shapes=[pltpu.VMEM((tm, kk), BF16) for kk in ks],
        compiler_params=_params(("arbitrary", "arbitrary"), vmem))(*ins)


def _matmul_tn(name, a, b, *, rms_gain=None, tl=1024, vmem=56):
    l, ka = a.shape
    n = b.shape[1]
    tl = min(tl, l)
    tn = _tile_n(n, 1536)

    def body(*refs):
        if rms_gain is not None:
            a_ref, b_ref, g_ref, o_ref = refs
        else:
            a_ref, b_ref, o_ref = refs
        t = pl.program_id(1)
        av = a_ref[...]
        if rms_gain is not None:
            av = _rms(av.astype(F32), g_ref[...], ka)
        v = _mm_tn(av, b_ref[...])

        @pl.when(t == 0)
        def _():
            o_ref[...] = v

        @pl.when(t > 0)
        def _():
            o_ref[...] += v

    in_specs = [pl.BlockSpec((tl, ka), lambda j, t: (t, 0)), pl.BlockSpec((tl, tn), lambda j, t: (t, j))]
    ins = [a, b]
    if rms_gain is not None:
        in_specs.append(pl.BlockSpec((1, ka), lambda j, t: (0, 0)))
        ins.append(rms_gain)
    return pl.pallas_call(
        body, name=name, grid=(n // tn, l // tl), in_specs=in_specs,
        out_specs=pl.BlockSpec((ka, tn), lambda j, t: (0, j)), out_shape=_sds((ka, n)),
        compiler_params=_params(("arbitrary", "arbitrary"), vmem))(*ins)


ATT_HEADS_PER_STEP = 2
ATT_W = ATT_HEADS_PER_STEP * LANE
ATT_GROUPS = MLA_HEADS // ATT_HEADS_PER_STEP
LOG2E = math.log2(math.e)
ATT_FWD_TILE = 1024
ATT_BWD_TILE = 1024
ATT_BWD_HEADS = 1
ATT_SCALE = D_QK ** -0.5
ATT_QSCALE = ATT_SCALE * LOG2E


def _tri_tables(nq, by_k):
    qs, ks = [], []
    if by_k:
        for ki in range(nq):
            for qi in range(ki, nq):
                qs.append(qi)
                ks.append(ki)
    else:
        for qi in range(nq):
            for ki in range(qi + 1):
                qs.append(qi)
                ks.append(ki)
    return jnp.asarray(np.array(qs, np.int32)), jnp.asarray(np.array(ks, np.int32))


def _causal_keep(shape, transposed):
    r = lax.broadcasted_iota(jnp.int32, shape, 0)
    c = lax.broadcasted_iota(jnp.int32, shape, 1)
    return (r <= c) if transposed else (c <= r)


def _nt16(a, b):
    return lax.dot_general(a, b, (((1,), (1,)), ((), ())), preferred_element_type=F32)


def _row_form(col):
    return jnp.transpose(jnp.broadcast_to(col, (col.shape[0], LANE)))[:8]


def _att_call(body, name, l, tq, tabs, ins, in_specs, outs, out_specs, scratch=(), groups=ATT_GROUPS, vmem=48):
    grid_spec = pltpu.PrefetchScalarGridSpec(
        num_scalar_prefetch=2, grid=(groups, tabs[0].shape[0]), in_specs=in_specs, out_specs=out_specs,
        scratch_shapes=list(scratch))
    return pl.pallas_call(body, name=name, grid_spec=grid_spec, out_shape=outs,
                          compiler_params=_params(("arbitrary", "arbitrary"), vmem))(*tabs, *ins)


def _flash_fwd(name, q, k, v_t, gather=None):
    l = q.shape[0]
    tq = min(ATT_FWD_TILE, l)
    nq = l // tq
    tabs = _tri_tables(nq, by_k=False)
    gx, glayer = gather if gather is not None else ([], 0)
    ng = len(gx)
    n_steps = int(tabs[0].shape[0])

    def body(qt, kt, *refs):
        q_ref, k_ref, vt_ref = refs[:3]
        gx_refs = refs[3:3 + ng]
        o_ref, lset_ref = refs[3 + ng:5 + ng]
        gy_refs = refs[5 + ng:5 + 2 * ng]
        m_s, acc_s = refs[5 + 2 * ng:7 + 2 * ng]
        t = pl.program_id(1)
        qi = qt[t]
        ki = kt[t]
        sls = [slice(h * LANE, (h + 1) * LANE) for h in range(ATT_HEADS_PER_STEP)]
        if ng:
            g_start, g_finish = _layer_gather_steps(gx_refs, gy_refs, refs[7 + 2 * ng:9 + 2 * ng], glayer)

            @pl.when((pl.program_id(0) == 0) & (t == 0))
            def _():
                g_start()

        @pl.when(ki == 0)
        def _():
            m_s[...] = jnp.full(m_s.shape, NEG, F32)
            acc_s[...] = jnp.zeros(acc_s.shape, F32)

        def step(masked):
            sts = [_nt16(k_ref[:, sl], q_ref[:, sl]) for sl in sls]
            for h, sl in enumerate(sls):
                st = sts[h]
                if masked:
                    st = jnp.where(_causal_keep(st.shape, True), st, NEG)
                m_old = m_s[h][:1]
                m_new = jnp.maximum(m_old, jnp.max(st, axis=0, keepdims=True))
                alpha = jnp.exp2(m_old - m_new)
                pt = jnp.exp2(st - m_new).astype(BF16)
                acc_s[sl, :] = alpha * acc_s[sl, :] + jnp.dot(vt_ref[sl, :], pt, preferred_element_type=F32)
                m_s[h] = jnp.broadcast_to(m_new, (8, tq))

        @pl.when(ki < qi)
        def _():
            step(False)

        @pl.when(ki == qi)
        def _():
            step(True)
            row = lax.broadcasted_iota(jnp.int32, (LANE, tq), 0)
            for h, sl in enumerate(sls):
                acc = acc_s[sl, :]
                lsum = acc[D_V:D_V + 1, :]
                o_ref[:, sl] = jnp.transpose(jnp.where(row < D_V, acc / lsum, 0.0))
                lset_ref[h * 8:(h + 1) * 8, :] = m_s[h] + jnp.log2(lsum)

        if ng:
            @pl.when((pl.program_id(0) == ATT_GROUPS - 1) & (t == n_steps - 1))
            def _():
                g_finish()

    qspec = pl.BlockSpec((tq, ATT_W), lambda g, t, qt, kt: (qt[t], g))
    kspec = pl.BlockSpec((tq, ATT_W), lambda g, t, qt, kt: (kt[t], g))
    vspec = pl.BlockSpec((ATT_W, tq), lambda g, t, qt, kt: (g, kt[t]))
    rspec = pl.BlockSpec((8 * ATT_HEADS_PER_STEP, tq), lambda g, t, qt, kt: (g, qt[t]))
    res = _att_call(
        body, name, l, tq, tabs, [q, k, v_t, *gx], [qspec, kspec, vspec] + [ANY] * ng,
        (_sds((l, MLA_PAD)), _sds((8 * MLA_HEADS, l)), *[_sds((4,) + x.shape[1:], x.dtype) for x in gx]),
        (qspec, rspec, *([ANY] * ng)),
        scratch=[pltpu.VMEM((ATT_HEADS_PER_STEP, 8, tq), F32), pltpu.VMEM((ATT_W, tq), F32)]
        + ([pltpu.SemaphoreType.DMA((6 * ng,)), pltpu.SemaphoreType.DMA((6 * ng,))] if ng else []))
    return res[0], res[1], list(res[2:])


def _flash_bwd(name, q, k, v, k_t, o, lse_t, do):
    l = q.shape[0]
    tq = min(ATT_BWD_TILE, l)
    nq = l // tq
    hb = ATT_BWD_HEADS
    wb = hb * LANE

    def delta_fn(i, dov, ov):
        rows = []
        for h in range(MLA_HEADS):
            sl = slice(h * LANE, (h + 1) * LANE)
            rows.append(_row_form(jnp.sum(dov[:, sl] * ov[:, sl], axis=-1, keepdims=True)))
        return jnp.concatenate(rows, axis=0), dov

    delta_t, do16 = _rows(
        delta_fn, name=name + "_delta", n=nq, ins=[do, o], in_specs=[_rt(tq, MLA_PAD)] * 2,
        outs=[_sds((8 * MLA_HEADS, l)), _sds((l, MLA_PAD), BF16)],
        out_specs=[pl.BlockSpec((8 * MLA_HEADS, tq), lambda i: (0, i)), _rt(tq, MLA_PAD)])

    def body(qt, kt, q_ref, k_ref, v_ref, do_ref, kt_ref, lset_ref, dlt_ref, dk_ref, dv_ref, dqt_ref):
        t = pl.program_id(1)
        qi = qt[t]
        ki = kt[t]
        sls = [slice(h * LANE, (h + 1) * LANE) for h in range(hb)]

        @pl.when(ki == 0)
        def _():
            dqt_ref[qi] = jnp.zeros((wb, tq), F32)

        def step(masked):
            sts = [_nt16(k_ref[:, sl], q_ref[:, sl]) for sl in sls]
            dpts = [_nt16(v_ref[:, sl], do_ref[:, sl]) for sl in sls]
            for h, sl in enumerate(sls):
                st = sts[h]
                if masked:
                    st = jnp.where(_causal_keep(st.shape, True), st, NEG)
                pt = jnp.exp2(st - lset_ref[h * 8:(h + 1) * 8, :][:1])
                dst = (pt * (dpts[h] - dlt_ref[h * 8:(h + 1) * 8, :][:1])).astype(BF16)
                dv_ref[:, sl] += jnp.dot(pt.astype(BF16), do_ref[:, sl], preferred_element_type=F32)
                dk_ref[:, sl] += jnp.dot(dst, q_ref[:, sl], preferred_element_type=F32)
                dqt_ref[qi, sl, :] += jnp.dot(kt_ref[sl, :], dst, preferred_element_type=F32)

        @pl.when(qi == ki)
        def _():
            dk_ref[...] = jnp.zeros(dk_ref.shape, F32)
            dv_ref[...] = jnp.zeros(dv_ref.shape, F32)
            step(True)
            dqt_ref[qi] = dqt_ref[qi] * ATT_SCALE

        @pl.when(qi > ki)
        def _():
            step(False)

        @pl.when(qi == nq - 1)
        def _():
            dk_ref[...] = dk_ref[...] * (1.0 / LOG2E)

    tabs_k = _tri_tables(nq, by_k=True)
    qspec = pl.BlockSpec((tq, wb), lambda g, t, qt, kt: (qt[t], g))
    kspec = pl.BlockSpec((tq, wb), lambda g, t, qt, kt: (kt[t], g))
    ktspec = pl.BlockSpec((wb, tq), lambda g, t, qt, kt: (g, kt[t]))
    rspec = pl.BlockSpec((8 * hb, tq), lambda g, t, qt, kt: (g, qt[t]))
    dqspec = pl.BlockSpec((nq, wb, tq), lambda g, t, qt, kt: (0, g, 0))
    dk, dv, dq_t = _att_call(body, name + "_dqkv", l, tq, tabs_k, [q, k, v, do16, k_t, lse_t, delta_t],
                             [qspec, kspec, kspec, qspec, ktspec, rspec, rspec],
                             (_sds((l, MLA_PAD)), _sds((l, MLA_PAD)), _sds((nq, MLA_PAD, tq))), (kspec, kspec, dqspec),
                             groups=MLA_HEADS // hb, vmem=56)
    return dq_t, dk, dv


def _cmul(ar, ai, br, bi):
    return ar * br - ai * bi, ar * bi + ai * br


def _scan(name, x_re, x_im, a_re, a_im, reverse):
    l, lanes = x_re.shape
    ns = SCAN_SEGS
    tl = l // ns
    steps = int(math.log2(tl))
    assert 2 ** steps == tl and tl * ns == l

    def body(xr_ref, xi_ref, ar_ref, ai_ref, sr_ref, si_ref):
        a_r1 = ar_ref[...]
        a_i1 = ai_ref[...]
        a_r = jnp.broadcast_to(a_r1, (ns, LANE))
        a_i = jnp.broadcast_to(a_i1, (ns, LANE))

        def rows(t):
            t = (tl - 1 - t) if reverse else t
            return pl.ds(pl.multiple_of(t * ns, ns), ns)

        def local(t, carry):
            cr, ci = carry
            r = rows(t)
            pr, pi = _cmul(a_r, a_i, cr, ci)
            return pr + xr_ref[r, :].astype(F32), pi + xi_ref[r, :].astype(F32)

        zero = jnp.zeros((ns, LANE), F32)
        e_r, e_i = lax.fori_loop(0, tl, local, (zero, zero), unroll=min(8, tl))
        p_r, p_i = a_r1, a_i1
        for _ in range(steps):
            p_r, p_i = _cmul(p_r, p_i, p_r, p_i)
        rid = lax.broadcasted_iota(jnp.int32, (ns, LANE), 0)
        c_r = jnp.zeros((1, LANE), F32)
        c_i = jnp.zeros((1, LANE), F32)
        in_r, in_i = zero, zero
        order = range(ns - 2, -1, -1) if reverse else range(1, ns)
        for kk in order:
            src = kk + 1 if reverse else kk - 1
            ek_r = jnp.sum(jnp.where(rid == src, e_r, 0.0), axis=0, keepdims=True)
            ek_i = jnp.sum(jnp.where(rid == src, e_i, 0.0), axis=0, keepdims=True)
            q_r, q_i = _cmul(p_r, p_i, c_r, c_i)
            c_r, c_i = q_r + ek_r, q_i + ek_i
            in_r = jnp.where(rid == kk, jnp.broadcast_to(c_r, (ns, LANE)), in_r)
            in_i = jnp.where(rid == kk, jnp.broadcast_to(c_i, (ns, LANE)), in_i)

        def final(t, carry):
            cr, ci = carry
            r = rows(t)
            pr, pi = _cmul(a_r, a_i, cr, ci)
            nr, ni = pr + xr_ref[r, :].astype(F32), pi + xi_ref[r, :].astype(F32)
            sr_ref[r, :] = nr.astype(sr_ref.dtype)
            si_ref[r, :] = ni.astype(si_ref.dtype)
            return nr, ni

        lax.fori_loop(0, tl, final, (in_r, in_i), unroll=min(8, tl))

    xs = pl.BlockSpec((l, LANE), lambda j: (0, j))
    as_ = pl.BlockSpec((1, LANE), lambda j: (0, j))
    return pl.pallas_call(
        body, name=name, grid=(lanes // LANE,), in_specs=[xs, xs, as_, as_], out_specs=(xs, xs),
        out_shape=(_sds((l, lanes), x_re.dtype), _sds((l, lanes), x_re.dtype)),
        compiler_params=_params(("arbitrary",), 48))(x_re, x_im, a_re, a_im)


ANY = pl.BlockSpec(memory_space=pl.ANY)


def _place():
    mx, my, mc = lax.axis_index("x"), lax.axis_index("y"), lax.axis_index("c")
    return mx, my, mc, [(1 - mx, my), (mx, 1 - my), (1 - mx, 1 - my)]


def _run_copies(copies):
    for cp in copies:
        cp.start()
    for cp in copies:
        cp.wait_recv()
    for cp in copies:
        cp.wait_send()


def _remote(src, dst, sems, k, dev):
    return pltpu.make_async_remote_copy(src_ref=src, dst_ref=dst, send_sem=sems[0].at[k], recv_sem=sems[1].at[k],
                                        device_id=dev, device_id_type=MESH)


def _copy_call(body, name, ins, outs, n_copies, aliases=None):
    return pl.pallas_call(
        body, name=name, in_specs=[ANY] * len(ins), out_specs=[ANY] * len(outs), out_shape=list(outs),
        input_output_aliases=aliases or {},
        scratch_shapes=[pltpu.SemaphoreType.DMA((n_copies,)), pltpu.SemaphoreType.DMA((n_copies,))])(*ins)


def _layer_gather_steps(x_refs, y_refs, sems, layer):
    n = len(x_refs)
    mx, my, mc, peers = _place()
    me = 2 * mx + my

    def ici(idx):
        i, j = divmod(idx, 3)
        return _remote(x_refs[i].at[layer], y_refs[i].at[me], sems, idx, (peers[j][0], peers[j][1], layer))

    def fwd(idx):
        i, j = divmod(idx, 3)
        pk = 2 * peers[j][0] + peers[j][1]
        return _remote(y_refs[i].at[pk], y_refs[i].at[pk], sems, 3 * n + idx, (mx, my, 1 - layer))

    def start():
        @pl.when(mc == layer)
        def _():
            for idx in range(3 * n):
                ici(idx).start()

    def finish():
        @pl.when(mc == layer)
        def _():
            for idx in range(3 * n):
                ici(idx).wait_recv()
                fwd(idx).start()
            for idx in range(3 * n):
                ici(idx).wait_send()
                fwd(idx).wait_send()

        @pl.when(mc != layer)
        def _():
            for idx in range(3 * n):
                fwd(idx).wait_recv()

    return start, finish


def _layer_gather(name, xs, layer):
    n = len(xs)

    def body(*refs):
        start, finish = _layer_gather_steps(refs[:n], refs[n:2 * n], refs[2 * n:], layer)
        start()
        finish()

    return _copy_call(body, name, xs, [_sds((4,) + x.shape[1:], x.dtype) for x in xs], 6 * n)


def _reduce_ici(gs, small):
    n = len(gs)

    def body(*refs):
        g_refs, s_ref, y_refs, a_ref, sems, local_sem = refs[:n], refs[n], refs[n + 1:2 * n + 1], refs[2 * n + 1], refs[2 * n + 2:2 * n + 4], refs[2 * n + 4]
        mx, my, mc, peers = _place()
        me8 = 4 * mx + 2 * my + mc
        own = pltpu.make_async_copy(s_ref, a_ref.at[me8], local_sem)
        own.start()
        copies = [_remote(g_refs[i].at[2 * px + py], y_refs[i].at[j], sems, 3 * i + j, (px, py, mc))
                  for i in range(n) for j, (px, py) in enumerate(peers)]
        for j in range(1, 8):
            dev = ((1 - mx) if (j & 4) else mx, (1 - my) if (j & 2) else my, (1 - mc) if (j & 1) else mc)
            copies.append(_remote(s_ref, a_ref.at[me8], sems, 3 * n + j - 1, dev))
        _run_copies(copies)
        own.wait()

    outs = [_sds((3,) + g.shape[1:], g.dtype) for g in gs] + [_sds((8,) + small.shape, small.dtype)]
    res = pl.pallas_call(
        body, name="comm_reduce_ici", in_specs=[ANY] * (n + 1), out_specs=[ANY] * (n + 1), out_shape=outs,
        scratch_shapes=[pltpu.SemaphoreType.DMA((3 * n + 7,)), pltpu.SemaphoreType.DMA((3 * n + 7,)),
                        pltpu.SemaphoreType.DMA])(*gs, small)
    return res[:n], res[n]


def _swap_d2d(name, ps, other_layer):
    n = len(ps)

    def body(*refs):
        p_refs, o_refs, sems = refs[:n], refs[n:2 * n], refs[2 * n:]
        mx, my, mc, _ = _place()
        _run_copies([_remote(p_refs[i].at[1 - mc] if other_layer else p_refs[i], o_refs[i], sems, i, (mx, my, 1 - mc))
                     for i in range(n)])

    outs = [_sds(p.shape[1:] if other_layer else p.shape, p.dtype) for p in ps]
    return _copy_call(body, name, ps, outs, n)


PACK_W = 1024


def _pack(arrs, rows_multiple, dtype):
    flat = jnp.concatenate([a.reshape(-1).astype(dtype) for a in arrs])
    n = flat.shape[0]
    unit = PACK_W * rows_multiple
    tot = -(-n // unit) * unit
    flat = jnp.pad(flat, (0, tot - n))
    return flat.reshape(tot // PACK_W, PACK_W)


def _unpack(flat, shapes):
    flat = flat.reshape(-1)
    out = []
    off = 0
    for s in shapes:
        n = int(np.prod(s))
        out.append(flat[off:off + n].reshape(s))
        off += n
    return out


def _rope_tables(pos):
    l = pos.shape[0]
    tm = min(512, l)
    inv = (np.float32(ROPE_THETA) ** (-np.arange(0, D_ROPE, 2, dtype=np.float32) / np.float32(D_ROPE))).astype(np.float32)
    lane_f = np.zeros((1, LANE), np.float32)
    lane_f[0, D_NOPE:D_NOPE + 16] = inv
    lane_f[0, D_NOPE + 16:D_NOPE + 32] = inv

    def fn(i, p, f):
        ang = p * f
        lane = lax.broadcasted_iota(jnp.int32, ang.shape, 1)
        co = jnp.cos(ang)
        si = jnp.sin(ang)
        c = jnp.where(lane < D_NOPE, 1.0, jnp.where(lane < D_QK, co, 0.0))
        s1 = jnp.where((lane >= D_NOPE) & (lane < D_NOPE + 16), -si, 0.0)
        s2 = jnp.where((lane >= D_NOPE + 16) & (lane < D_QK), si, 0.0)
        return c, s1, s2

    return _rows(fn, name="rope_tables", n=l // tm, ins=[pos, jnp.asarray(lane_f)],
                 in_specs=[_rt(tm, 1), _full((1, LANE))], outs=[_sds((l, LANE))] * 3, out_specs=[_rt(tm, LANE)] * 3)


def _ssm_param_fn(lr, li, log_dt, br, bi):
    dt = jnp.exp(log_dt)
    mag = jnp.exp(lr * dt)
    a_re = mag * jnp.cos(li * dt)
    a_im = mag * jnp.sin(li * dt)
    den = lr * lr + li * li
    e_re = a_re - 1.0
    e_im = a_im
    f_re = (e_re * lr + e_im * li) / den
    f_im = (e_im * lr - e_re * li) / den
    bb_re = f_re[None] * br - f_im[None] * bi
    bb_im = f_re[None] * bi + f_im[None] * br
    return a_re, a_im, bb_re, bb_im


def _ssm_params(name, lr, li, log_dt, br, bi):
    g, n = lr.shape
    c = br.shape[0]
    return _rows(lambda i, *v: _ssm_param_fn(*v), name=name, n=1, ins=[lr, li, log_dt, br, bi],
                 in_specs=[_full((g, n)), _full((g, n)), _full((g, 1)), _full((c, g, n)), _full((c, g, n))],
                 outs=[_sds((g, n)), _sds((g, n)), _sds((c, g, n)), _sds((c, g, n))],
                 out_specs=[_full((g, n)), _full((g, n)), _full((c, g, n)), _full((c, g, n))])


def _ssm_params_bwd(name, lr, li, log_dt, br, bi, d_are, d_aim, d_bbre, d_bbim):
    g, n = lr.shape
    c = br.shape[0]

    def fn(i, lr, li, log_dt, br, bi, g0, g1, g2, g3):
        _, vjp = jax.vjp(_ssm_param_fn, lr, li, log_dt, br, bi)
        return vjp((g0, g1, g2, g3))

    sp = [_full((g, n)), _full((g, n)), _full((g, 1)), _full((c, g, n)), _full((c, g, n))]
    return _rows(fn, name=name, n=1, ins=[lr, li, log_dt, br, bi, d_are, d_aim, d_bbre, d_bbim],
                 in_specs=sp + [_full((g, n)), _full((g, n)), _full((c, g, n)), _full((c, g, n))],
                 outs=[_sds((g, n)), _sds((g, n)), _sds((g, 1)), _sds((c, g, n)), _sds((c, g, n))], out_specs=sp)


_EYE8 = np.eye(8, dtype=np.float32)


def _blockdiag(v):
    j, g, p, q = v.shape
    m = v[:, :, :, None, :] * jnp.asarray(_EYE8)[None, :, None, :, None]
    return m.reshape(j, g * p, g * q)


def _blockdiag_t(m, p, q):
    j = m.shape[0]
    m = m.reshape(j, 8, p, 8, q)
    return jnp.sum(m * jnp.asarray(_EYE8)[None, :, None, :, None], axis=3)


def _to_perm(v, l):
    ns = SCAN_SEGS
    return v.reshape(ns, l // ns, v.shape[-1]).transpose(1, 0, 2).reshape(l, v.shape[-1])


def _from_perm(v, l):
    ns = SCAN_SEGS
    return v.reshape(l // ns, ns, v.shape[-1]).transpose(1, 0, 2).reshape(l, v.shape[-1])


def _prep_layer(w, i):
    p = {}
    w_in = w['w_in'][i]
    z = lambda n: jnp.zeros((D_MODEL, n), w_in.dtype)
    o = Q_LORA + KV_LORA
    p['w_s'] = jnp.concatenate([w_in[:, :o], z(D_NOPE), w_in[:, o:o + D_ROPE], z(HEAD_PAD - D_QK)], axis=1)
    o += D_ROPE
    p['w_u'] = w_in[:, o:o + SSM_WIDTH]
    o += SSM_WIDTH
    p['w_xq'] = w_in[:, o:o + X_WIDTH]
    o += X_WIDTH
    p['w_g'] = w_in[:, o:]
    wq = w['w_q_b'][i].reshape(Q_LORA, MLA_HEADS, D_QK)
    p['w_qb'] = jnp.pad(wq, ((0, 0), (0, 0), (0, HEAD_PAD - D_QK))).reshape(Q_LORA, MLA_PAD)
    wkv = w['w_kv_b'][i].reshape(KV_LORA, MLA_HEADS, D_NOPE + D_V)
    p['w_k'] = jnp.pad(wkv[:, :, :D_NOPE], ((0, 0), (0, 0), (0, HEAD_PAD - D_NOPE))).reshape(KV_LORA, MLA_PAD)
    p['w_v'] = jnp.pad(wkv[:, :, D_NOPE:], ((0, 0), (0, 0), (0, HEAD_PAD - D_V))).reshape(KV_LORA, MLA_PAD)
    wo = w['w_o_mla'][i].reshape(MLA_HEADS, D_V, D_MODEL)
    p['w_oa'] = jnp.pad(wo, ((0, 0), (0, HEAD_PAD - D_V), (0, 0))).reshape(MLA_PAD, D_MODEL)
    for n in ('w_glu', 'w_o_ssm', 'w_mem_kv', 'w_o_cross', 'w_out', 'w_up', 'w_down'):
        p[n] = w[n][i]
    p['conv_w'] = w['conv_w'][i]
    for n in ('norm_mix_g', 'q_a_norm_g', 'kv_a_norm_g', 'b_glu', 'mem_norm_g', 'xq_norm_g', 'xk_norm_g', 'b_gate',
              'norm_ffn_g', 'conv_b'):
        p[n] = w[n][i].reshape(1, -1)
    p['q_norm_g'] = jnp.pad(w['q_norm_g'][i], (0, HEAD_PAD - D_QK)).reshape(1, HEAD_PAD)
    p['k_norm_g'] = jnp.pad(w['k_norm_g'][i], (0, HEAD_PAD - D_QK)).reshape(1, HEAD_PAD)
    p['ssm_d'] = w['ssm_d'][i].reshape(1, SSM_WIDTH)
    p['lr'] = w['ssm_lambda_re'][i]
    p['li'] = w['ssm_lambda_im'][i]
    p['log_dt'] = w['ssm_log_dt'][i].reshape(SSM_GROUPS, 1)
    p['br'] = w['ssm_b_re'][i].transpose(2, 0, 1)
    p['bi'] = w['ssm_b_im'][i].transpose(2, 0, 1)
    cr = w['ssm_c_re'][i].reshape(SSM_JB, 8, SSM_GROUP_CH, SSM_STATE).transpose(0, 1, 3, 2)
    ci = w['ssm_c_im'][i].reshape(SSM_JB, 8, SSM_GROUP_CH, SSM_STATE).transpose(0, 1, 3, 2)
    p['c_mat'] = jnp.concatenate([_blockdiag(cr), -_blockdiag(ci)], axis=1).astype(BF16)
    return p


def _b_mat(bb_re, bb_im):
    r = bb_re.transpose(1, 0, 2).reshape(SSM_JB, 8, SSM_GROUP_CH, SSM_STATE)
    i = bb_im.transpose(1, 0, 2).reshape(SSM_JB, 8, SSM_GROUP_CH, SSM_STATE)
    return jnp.concatenate([_blockdiag(r), _blockdiag(i)], axis=2).astype(BF16)


def _qkv_fn(ps, c, s1, s2, qag, wqb, kvag, wk, wv, qng, kng):
    c_q = ps[:, :Q_LORA]
    c_kv = ps[:, Q_LORA:Q_LORA + KV_LORA]
    kr = ps[:, Q_LORA + KV_LORA:]
    cqn = _rms(c_q, qag, Q_LORA)
    ckvn = _rms(c_kv, kvag, KV_LORA)
    q_raw = _mm(cqn, wqb)
    k_raw = _mm(ckvn, wk) + jnp.concatenate([kr] * MLA_HEADS, axis=-1)
    v = _mm(ckvn, wv)
    q = _heads(_rope, _head_rms(q_raw, qng, MLA_HEADS, D_QK), MLA_HEADS, c, s1, s2)
    k = _heads(_rope, _head_rms(k_raw, kng, MLA_HEADS, D_QK), MLA_HEADS, c, s1, s2)
    lane = lax.broadcasted_iota(jnp.int32, v.shape, 1)
    v = jnp.where((lane & (LANE - 1)) == D_V, 1.0, v)
    return q * ATT_QSCALE, k, v


def _layer_fwd(name, x, tabs, mem, p, gather=None):
    l = x.shape[0]
    tm = min(512, l)
    nt = l // tm
    sv = {'x0': x}
    sv['p_g'] = _matmul(name + "_in_g", [(x, p['w_g'])], l, 3 * D_MODEL, rms_gain=p['norm_mix_g'])
    sv['p_u'] = _matmul(name + "_in_u", [(x, p['w_u'])], l, SSM_WIDTH, rms_gain=p['norm_mix_g'])
    sv['p_xq'] = _matmul(name + "_in_xq", [(x, p['w_xq'])], l, X_WIDTH, rms_gain=p['norm_mix_g'])
    sv['p_s'] = _matmul(name + "_in_s", [(x, p['w_s'])], l, SMALL_W, rms_gain=p['norm_mix_g'])

    qkv_consts = [p['q_a_norm_g'], p['w_qb'], p['kv_a_norm_g'], p['w_k'], p['w_v'], p['q_norm_g'], p['k_norm_g']]
    qkv_cspecs = [_full(a.shape) for a in qkv_consts]
    def qkv_fwd(i, *a):
        qv, kv, vv = _qkv_fn(*a)
        return qv, kv, vv, jnp.transpose(kv), jnp.transpose(vv)

    q, k, v, k_t, v_t = _rows(qkv_fwd, name=name + "_qkv", n=nt, ins=[sv['p_s'], *tabs, *qkv_consts],
                              in_specs=[_rt(tm, SMALL_W)] + [_rt(tm, LANE)] * 3 + qkv_cspecs,
                              outs=[_sds((l, MLA_PAD), BF16)] * 3 + [_sds((MLA_PAD, l), BF16)] * 2,
                              out_specs=[_rt(tm, MLA_PAD)] * 3 + [pl.BlockSpec((MLA_PAD, tm), lambda i: (0, i))] * 2)
    sv['q'], sv['k'], sv['v'], sv['k_t'] = q, k, v, k_t
    sv['o_a'], sv['lse_t'], sv['gathered'] = _flash_fwd(name + "_attn", q, k, v_t, gather=gather)

    a_re, a_im, bb_re, bb_im = _ssm_params(name + "_ssm_par", p['lr'], p['li'], p['log_dt'], p['br'], p['bi'])
    sv['a_re'], sv['a_im'] = a_re.reshape(1, SSM_LANES), a_im.reshape(1, SSM_LANES)
    sv['b_mat'] = _b_mat(bb_re, bb_im)
    u_p = _to_perm(sv['p_u'], l)
    sv['u_p'] = u_p

    def bu_fn(i, u, bm):
        res = [_mm(u[:, j * LANE:(j + 1) * LANE], bm[j]) for j in range(SSM_JB)]
        return (jnp.concatenate([r[:, :512] for r in res], axis=-1), jnp.concatenate([r[:, 512:] for r in res], axis=-1))

    ts = min(256, l)
    bu_re, bu_im = _rows(bu_fn, name=name + "_ssm_bu", n=l // ts, ins=[u_p, sv['b_mat']],
                         in_specs=[_rt(ts, SSM_WIDTH), _full(sv['b_mat'].shape)],
                         outs=[_sds((l, SSM_LANES), SSM_STATE_DTYPE)] * 2, out_specs=[_rt(ts, SSM_LANES)] * 2)
    s_re, s_im = _scan(name + "_ssm_scan", bu_re, bu_im, sv['a_re'], sv['a_im'], reverse=False)
    sv['s_re'], sv['s_im'] = s_re, s_im

    def glu_fn(i, sr, si, u, cm, dsk, wg, bg):
        y = jnp.concatenate([_mm(jnp.concatenate([sr[:, j * 512:(j + 1) * 512], si[:, j * 512:(j + 1) * 512]], axis=-1),
                                 cm[j]) for j in range(SSM_JB)], axis=-1) + dsk * u
        zz = _gelu(y)
        return zz * jax.nn.sigmoid(_mm(zz, wg) + bg)

    glu_consts = [p['c_mat'], p['ssm_d'], p['w_glu'], p['b_glu']]
    zo_p = _rows(glu_fn, name=name + "_ssm_glu", n=l // ts, ins=[s_re, s_im, u_p, *glu_consts],
                 in_specs=[_rt(ts, SSM_LANES), _rt(ts, SSM_LANES), _rt(ts, SSM_WIDTH)] + [_full(a.shape) for a in glu_consts],
                 outs=[_sds((l, SSM_WIDTH), BF16)], out_specs=[_rt(ts, SSM_WIDTH)])[0]
    sv['zo'] = _from_perm(zo_p, l)

    m_len = mem.shape[0]

    def memkv_fn(i, mm_, mg, wmk, xkg):
        kv = _mm(_rms(mm_, mg, D_MODEL), wmk)
        return _head_rms(kv[:, :X_WIDTH], xkg, X_HEADS, X_HEAD_DIM), kv[:, X_WIDTH:]

    mem_consts = [p['mem_norm_g'], p['w_mem_kv'], p['xk_norm_g']]
    k_c, v_c = _rows(memkv_fn, name=name + "_memkv", n=1, ins=[mem, *mem_consts],
                     in_specs=[_full(mem.shape)] + [_full(a.shape) for a in mem_consts],
                     outs=[_sds((m_len, X_WIDTH))] * 2, out_specs=[_full((m_len, X_WIDTH))] * 2)
    sv['k_c'], sv['v_c'] = k_c, v_c

    def cross_fn(i, xq, kc, vc, xqg):
        outs = []
        for h in range(X_HEADS):
            sl = slice(h * LANE, (h + 1) * LANE)
            qh = _rms(xq[:, sl], xqg, X_HEAD_DIM)
            s = _mm_nt(qh, kc[:, sl]) * (X_HEAD_DIM ** -0.5)
            s = s - jnp.max(s, axis=-1, keepdims=True)
            e = jnp.exp(s)
            pr = e / jnp.sum(e, axis=-1, keepdims=True)
            outs.append(_mm(pr, vc[:, sl]))
        return jnp.concatenate(outs, axis=-1)

    sv['o_c'] = _rows(cross_fn, name=name + "_cross", n=nt, ins=[sv['p_xq'], k_c, v_c, p['xq_norm_g']],
                      in_specs=[_rt(tm, X_WIDTH), _full(k_c.shape), _full(v_c.shape), _full((1, LANE))],
                      outs=[_sds((l, X_WIDTH), BF16)], out_specs=[_rt(tm, X_WIDTH)])[0]

    def merge_fn(i, oa, zo, oc, pg, x0, woa, wos, woc, bg, wout):
        gates = jax.nn.sigmoid(pg + bg)
        merged = (gates[:, :D_MODEL] * _mm(oa, woa) + gates[:, D_MODEL:2 * D_MODEL] * _mm(zo, wos)
                  + gates[:, 2 * D_MODEL:] * _mm(oc, woc))
        return x0 + _mm(merged, wout), merged

    merge_consts = [p['w_oa'], p['w_o_ssm'], p['w_o_cross'], p['b_gate'], p['w_out']]
    tg = min(256, l)
    x1, merged = _rows(merge_fn, name=name + "_merge", n=l // tg, ins=[sv['o_a'], sv['zo'], sv['o_c'], sv['p_g'], x, *merge_consts],
                       in_specs=[_rt(tg, MLA_PAD), _rt(tg, SSM_WIDTH), _rt(tg, X_WIDTH), _rt(tg, 3 * D_MODEL), _rt(tg, D_MODEL)]
                       + [_full(a.shape) for a in merge_consts],
                       outs=[_sds((l, D_MODEL)), _sds((l, D_MODEL), BF16)], out_specs=[_rt(tg, D_MODEL)] * 2)
    sv['x1'], sv['merged'] = x1, merged

    up = _matmul(name + "_up", [(x1, p['w_up'])], l, 2 * D_FF, rms_gain=p['norm_ffn_g'])
    sv['up'] = up
    tc = min(128, l)

    def conv_fn(i, upt, halo, cw, cb):
        upc = _conv(i, upt, halo, cw) + cb
        return _silu(upc[:, :D_FF]) * upc[:, D_FF:]

    act = _rows(conv_fn, name=name + "_conv", n=l // tc, ins=[up, up, p['conv_w'], p['conv_b']],
                in_specs=[_rt(tc, 2 * D_FF), _halo_prev(tc, 2 * D_FF), _full((3, 2 * D_FF)), _full((1, 2 * D_FF))],
                outs=[_sds((l, D_FF), BF16)], out_specs=[_rt(tc, D_FF)])[0]
    sv['act'] = act
    x2 = _matmul(name + "_down", [(act, p['w_down'])], l, D_MODEL, resid=x1)
    return x2, sv


def _halo_prev(tm, w):
    return pl.BlockSpec((8, w), lambda i: (jnp.maximum(i * (tm // 8) - 1, 0), 0))


def _halo_next(tm, w, n_tiles):
    last = n_tiles * (tm // 8) - 1
    return pl.BlockSpec((8, w), lambda i: (jnp.minimum((i + 1) * (tm // 8), last), 0))


def _conv(i, tile, halo, cw):
    halo = jnp.where(i > 0, halo, 0.0)
    ext = jnp.concatenate([halo, tile], axis=0)
    n = ext.shape[0]
    x1 = pltpu.roll(ext, 1, 0)[8:]
    x2 = pltpu.roll(ext, 2, 0)[8:]
    del n
    return cw[0:1] * x2 + cw[1:2] * x1 + cw[2:3] * tile


def _layer_bwd(name, dx2, sv, tabs, mem, p):
    l = dx2.shape[0]
    tm = min(512, l)
    nt = l // tm
    g = {}
    x1 = sv['x1']
    dact = _matmul(name + "_b_down", [(dx2, p['w_down'])], l, D_FF, nt=True)
    g['w_down'] = _matmul_tn(name + "_gw_down", sv['act'], dx2)
    tc = min(128, l)
    ntc = l // tc

    def conv_b(i, upt, up_prev, up_next, da, da_next, cw, cb):
        up_prev = jnp.where(i > 0, up_prev, 0.0)
        da_next = jnp.where(i < ntc - 1, da_next, 0.0)
        ext = jnp.concatenate([up_prev, upt, up_next], axis=0)
        x0 = ext[8:]
        xm1 = pltpu.roll(ext, 1, 0)[8:]
        xm2 = pltpu.roll(ext, 2, 0)[8:]
        upc = cw[0:1] * xm2 + cw[1:2] * xm1 + cw[2:3] * x0 + cb
        _, vjp = jax.vjp(lambda a, b: _silu(a) * b, upc[:, :D_FF], upc[:, D_FF:])
        dg, dv = vjp(jnp.concatenate([da, da_next], axis=0))
        dupc = jnp.concatenate([dg, dv], axis=-1)
        n = dupc.shape[0]
        dup = cw[2:3] * dupc[:tc] + cw[1:2] * pltpu.roll(dupc, n - 1, 0)[:tc] + cw[0:1] * pltpu.roll(dupc, n - 2, 0)[:tc]
        dt = dupc[:tc]
        dcw = _row_select([_colsum(dt * xm2[:tc]), _colsum(dt * xm1[:tc]), _colsum(dt * upt)], 8)
        return dup, dcw, _colsum(dt)

    dup, g_cw, g_cb = _rows(
        conv_b, name=name + "_b_conv", n=ntc, ins=[sv['up'], sv['up'], sv['up'], dact, dact, p['conv_w'], p['conv_b']],
        in_specs=[_rt(tc, 2 * D_FF), _halo_prev(tc, 2 * D_FF), _halo_next(tc, 2 * D_FF, ntc), _rt(tc, D_FF),
                  _halo_next(tc, D_FF, ntc), _full((3, 2 * D_FF)), _full((1, 2 * D_FF))],
        outs=[_sds((l, 2 * D_FF)), _sds((8, 2 * D_FF)), _sds((1, 2 * D_FF))],
        out_specs=[_rt(tc, 2 * D_FF), _full((8, 2 * D_FF)), _full((1, 2 * D_FF))], n_acc=2, vmem=56)
    g['conv_w'] = g_cw[:3]
    g['conv_b'] = g_cb
    dh2 = _matmul(name + "_b_up", [(dup, p['w_up'])], l, D_MODEL, nt=True, tm=256)
    g['w_up'] = _matmul_tn(name + "_gw_up", x1, dup, rms_gain=p['norm_ffn_g'])

    def norm_b(i, xv, dh, dres, gn):
        _, vjp = jax.vjp(lambda a, b: _rms(a, b, D_MODEL), xv, gn)
        dxv, dgn = vjp(dh)
        return dres + dxv, dgn

    dx1, g['norm_ffn_g'] = _rows(norm_b, name=name + "_b_norm2", n=nt, ins=[x1, dh2, dx2, p['norm_ffn_g']],
                                 in_specs=[_rt(tm, D_MODEL)] * 3 + [_full((1, D_MODEL))],
                                 outs=[_sds((l, D_MODEL)), _sds((1, D_MODEL))], out_specs=[_rt(tm, D_MODEL), _full((1, D_MODEL))],
                                 n_acc=1)

    tg = min(256, l)

    def merge_b(i, dx, oa, zo, oc, pg, woa, wos, woc, bg, wout):
        dm = _mm_nt(dx, wout)
        gates = jax.nn.sigmoid(pg + bg)
        ys = [_mm(oa, woa), _mm(zo, wos), _mm(oc, woc)]
        dys, dpg = [], []
        for b in range(3):
            gb = gates[:, b * D_MODEL:(b + 1) * D_MODEL]
            dys.append(dm * gb)
            dpg.append(dm * ys[b] * gb * (1.0 - gb))
        dpg = jnp.concatenate(dpg, axis=-1)
        return (_mm_nt(dys[0], woa), _mm_nt(dys[1], wos), _mm_nt(dys[2], woc), dpg, dys[0], dys[1], dys[2], _colsum(dpg))

    merge_consts = [p['w_oa'], p['w_o_ssm'], p['w_o_cross'], p['b_gate'], p['w_out']]
    (do_a, dzo, do_c, dp_g, dy_a, dy_b, dy_c, g['b_gate']) = _rows(
        merge_b, name=name + "_b_merge", n=l // tg, ins=[dx1, sv['o_a'], sv['zo'], sv['o_c'], sv['p_g'], *merge_consts],
        in_specs=[_rt(tg, D_MODEL), _rt(tg, MLA_PAD), _rt(tg, SSM_WIDTH), _rt(tg, X_WIDTH), _rt(tg, 3 * D_MODEL)]
        + [_full(a.shape) for a in merge_consts],
        outs=[_sds((l, MLA_PAD)), _sds((l, SSM_WIDTH)), _sds((l, X_WIDTH)), _sds((l, 3 * D_MODEL)),
              _sds((l, D_MODEL), BF16), _sds((l, D_MODEL), BF16), _sds((l, D_MODEL), BF16), _sds((1, 3 * D_MODEL))],
        out_specs=[_rt(tg, MLA_PAD), _rt(tg, SSM_WIDTH), _rt(tg, X_WIDTH), _rt(tg, 3 * D_MODEL),
                   _rt(tg, D_MODEL), _rt(tg, D_MODEL), _rt(tg, D_MODEL), _full((1, 3 * D_MODEL))], n_acc=1, vmem=56)
    g['w_out'] = _matmul_tn(name + "_gw_out", sv['merged'], dx1)
    g['w_oa'] = _matmul_tn(name + "_gw_oa", sv['o_a'], dy_a)
    g['w_o_ssm'] = _matmul_tn(name + "_gw_os", sv['zo'], dy_b)
    g['w_o_cross'] = _matmul_tn(name + "_gw_oc", sv['o_c'], dy_c)

    k_c, v_c = sv['k_c'], sv['v_c']
    m_len = k_c.shape[0]

    def cross_b(i, xq, do, kc, vc, xqg):
        dxq, dk, dv = [], [], []
        dg = jnp.zeros((1, LANE), F32)
        for h in range(X_HEADS):
            sl = slice(h * LANE, (h + 1) * LANE)
            qh, vjp = jax.vjp(lambda a, b: _rms(a, b, X_HEAD_DIM), xq[:, sl], xqg)
            sc = X_HEAD_DIM ** -0.5
            s = _mm_nt(qh, kc[:, sl]) * sc
            s = s - jnp.max(s, axis=-1, keepdims=True)
            e = jnp.exp(s)
            pr = e / jnp.sum(e, axis=-1, keepdims=True)
            doh = do[:, sl]
            dv.append(_mm_tn(pr, doh))
            dp = _mm_nt(doh, vc[:, sl])
            ds = pr * (dp - jnp.sum(dp * pr, axis=-1, keepdims=True)) * sc
            dk.append(_mm_tn(ds, qh))
            dxh, dgh = vjp(_mm(ds, kc[:, sl]))
            dxq.append(dxh)
            dg = dg + dgh
        return jnp.concatenate(dxq, axis=-1), jnp.concatenate(dk, axis=-1), jnp.concatenate(dv, axis=-1), dg

    dp_xq, dk_c, dv_c, g['xq_norm_g'] = _rows(
        cross_b, name=name + "_b_cross", n=nt, ins=[sv['p_xq'], do_c, k_c, v_c, p['xq_norm_g']],
        in_specs=[_rt(tm, X_WIDTH), _rt(tm, X_WIDTH), _full(k_c.shape), _full(v_c.shape), _full((1, LANE))],
        outs=[_sds((l, X_WIDTH)), _sds((m_len, X_WIDTH)), _sds((m_len, X_WIDTH)), _sds((1, LANE))],
        out_specs=[_rt(tm, X_WIDTH), _full((m_len, X_WIDTH)), _full((m_len, X_WIDTH)), _full((1, LANE))], n_acc=3)

    def memkv_b(i, mm_, dk, dv, mg, wmk, xkg):
        memn, vjp_n = jax.vjp(lambda a, b: _rms(a, b, D_MODEL), mm_, mg)
        kv = _mm(memn, wmk)
        _, vjp_k = jax.vjp(lambda a, b: _head_rms(a, b, X_HEADS, X_HEAD_DIM), kv[:, :X_WIDTH], xkg)
        dkr, dxkg = vjp_k(dk)
        dkv = jnp.concatenate([dkr, dv], axis=-1)
        _, dmg = vjp_n(_mm_nt(dkv, wmk))
        return _mm_tn(memn, dkv), dmg, dxkg

    mem_consts = [p['mem_norm_g'], p['w_mem_kv'], p['xk_norm_g']]
    g['w_mem_kv'], g['mem_norm_g'], g['xk_norm_g'] = _rows(
        memkv_b, name=name + "_b_memkv", n=1, ins=[mem, dk_c, dv_c, *mem_consts],
        in_specs=[_full(mem.shape), _full(dk_c.shape), _full(dv_c.shape)] + [_full(a.shape) for a in mem_consts],
        outs=[_sds((D_MODEL, 2 * X_WIDTH)), _sds((1, D_MODEL)), _sds((1, LANE))],
        out_specs=[_full((D_MODEL, 2 * X_WIDTH)), _full((1, D_MODEL)), _full((1, LANE))])

    u_p = sv['u_p']
    dzo_p = _to_perm(dzo, l)
    s_re, s_im = sv['s_re'], sv['s_im']

    def glu_b(i, sr, si, u, dz, cm, dsk, wg, bg):
        cats = [jnp.concatenate([sr[:, j * 512:(j + 1) * 512], si[:, j * 512:(j + 1) * 512]], axis=-1) for j in range(SSM_JB)]
        y = jnp.concatenate([_mm(cats[j], cm[j]) for j in range(SSM_JB)], axis=-1) + dsk * u
        zz, vjp_g = jax.vjp(_gelu, y)
        t = _mm(zz, wg) + bg
        sg = jax.nn.sigmoid(t)
        dt = dz * zz * sg * (1.0 - sg)
        dzz = dz * sg + _mm_nt(dt, wg)
        dy = vjp_g(dzz)[0]
        dss = [_mm_nt(dy[:, j * LANE:(j + 1) * LANE], cm[j]) for j in range(SSM_JB)]
        dsr = jnp.concatenate([d[:, :512] for d in dss], axis=-1)
        dsi = jnp.concatenate([d[:, 512:] for d in dss], axis=-1)
        dcm = jnp.stack([_mm_tn(cats[j], dy[:, j * LANE:(j + 1) * LANE]) for j in range(SSM_JB)], axis=0)
        return dsr, dsi, dy * dsk, dcm, _colsum(dy * u), _mm_tn(zz, dt), _colsum(dt)

    glu_consts = [p['c_mat'], p['ssm_d'], p['w_glu'], p['b_glu']]
    ts = min(256, l)
    nts = l // ts
    ds_re, ds_im, du_dir, g['c_mat'], g['ssm_d'], g['w_glu'], g['b_glu'] = _rows(
        glu_b, name=name + "_b_glu", n=nts, ins=[s_re, s_im, u_p, dzo_p, *glu_consts],
        in_specs=[_rt(ts, SSM_LANES), _rt(ts, SSM_LANES), _rt(ts, SSM_WIDTH), _rt(ts, SSM_WIDTH)] + [_full(a.shape) for a in glu_consts],
        outs=[_sds((l, SSM_LANES), SSM_STATE_DTYPE), _sds((l, SSM_LANES), SSM_STATE_DTYPE), _sds((l, SSM_WIDTH)),
              _sds((SSM_JB, 1024, LANE)), _sds((1, SSM_WIDTH)),
              _sds((SSM_WIDTH, SSM_WIDTH)), _sds((1, SSM_WIDTH))],
        out_specs=[_rt(ts, SSM_LANES), _rt(ts, SSM_LANES), _rt(ts, SSM_WIDTH), _full((SSM_JB, 1024, LANE)), _full((1, SSM_WIDTH)),
                   _full((SSM_WIDTH, SSM_WIDTH)), _full((1, SSM_WIDTH))], n_acc=4)
    gb_re, gb_im = _scan(name + "_b_scan", ds_re, ds_im, sv['a_re'], -sv['a_im'], reverse=True)
    ns = SCAN_SEGS
    last_blk = l // ns - 1

    def da_fn(i, *vals):
        gr, gi, sr, si, hr, hi, lr_, li_ = [v.astype(F32) for v in vals]
        rid = lax.broadcasted_iota(jnp.int32, lr_.shape, 0)
        fr = jnp.where(rid == 0, 0.0, pltpu.roll(lr_, 1, 0))
        fi = jnp.where(rid == 0, 0.0, pltpu.roll(li_, 1, 0))
        hr = jnp.where(i == 0, fr, hr)
        hi = jnp.where(i == 0, fi, hi)
        if ts > ns:
            pr = jnp.concatenate([hr, sr[:ts - ns]], axis=0)
            pi = jnp.concatenate([hi, si[:ts - ns]], axis=0)
        else:
            pr, pi = hr, hi
        return _colsum(gr * pr + gi * pi), _colsum(gi * pr - gr * pi)

    hprev = pl.BlockSpec((ns, SSM_LANES), lambda i: (jnp.maximum(i * (ts // ns) - 1, 0), 0))
    hlast = pl.BlockSpec((ns, SSM_LANES), lambda i: (last_blk, 0))
    da_re, da_im = _rows(da_fn, name=name + "_b_da", n=nts, ins=[gb_re, gb_im, s_re, s_im, s_re, s_im, s_re, s_im],
                         in_specs=[_rt(ts, SSM_LANES)] * 4 + [hprev, hprev, hlast, hlast],
                         outs=[_sds((1, SSM_LANES))] * 2, out_specs=[_full((1, SSM_LANES))] * 2, n_acc=2)

    def bu_b(i, dbr, dbi, u, dud, bm):
        dus, dbm = [], []
        for j in range(SSM_JB):
            cat = jnp.concatenate([dbr[:, j * 512:(j + 1) * 512], dbi[:, j * 512:(j + 1) * 512]], axis=-1)
            dus.append(_mm_nt(cat, bm[j]))
            dbm.append(_mm_tn(u[:, j * LANE:(j + 1) * LANE], cat))
        return dud + jnp.concatenate(dus, axis=-1), jnp.stack(dbm, axis=0)

    du_p, d_bmat = _rows(bu_b, name=name + "_b_bu", n=nts, ins=[gb_re, gb_im, u_p, du_dir, sv['b_mat']],
                         in_specs=[_rt(ts, SSM_LANES), _rt(ts, SSM_LANES), _rt(ts, SSM_WIDTH), _rt(ts, SSM_WIDTH),
                                   _full(sv['b_mat'].shape)],
                         outs=[_sds((l, SSM_WIDTH)), _sds((SSM_JB, LANE, 1024))],
                         out_specs=[_rt(ts, SSM_WIDTH), _full((SSM_JB, LANE, 1024))], n_acc=1)
    dp_u = _from_perm(du_p, l)
    dbb_re = _blockdiag_t(d_bmat[:, :, :512], SSM_GROUP_CH, SSM_STATE).reshape(SSM_GROUPS, SSM_GROUP_CH, SSM_STATE).transpose(1, 0, 2)
    dbb_im = _blockdiag_t(d_bmat[:, :, 512:], SSM_GROUP_CH, SSM_STATE).reshape(SSM_GROUPS, SSM_GROUP_CH, SSM_STATE).transpose(1, 0, 2)
    g['lr'], g['li'], g['log_dt'], g['br'], g['bi'] = _ssm_params_bwd(
        name + "_b_ssm_par", p['lr'], p['li'], p['log_dt'], p['br'], p['bi'],
        da_re.reshape(SSM_GROUPS, SSM_STATE), da_im.reshape(SSM_GROUPS, SSM_STATE), dbb_re, dbb_im)

    dq_t, dk, dv = _flash_bwd(name + "_b_attn", sv['q'], sv['k'], sv['v'], sv['k_t'], sv['o_a'], sv['lse_t'], do_a)

    def qkv_b(i, ps, c, s1, s2, dq_, dk_, dv_, qag, wqb, kvag, wk, wv, qng, kng):
        c_q = ps[:, :Q_LORA]
        c_kv = ps[:, Q_LORA:Q_LORA + KV_LORA]
        kr = ps[:, Q_LORA + KV_LORA:]
        cqn, vjp_cq = jax.vjp(lambda a, b: _rms(a, b, Q_LORA), c_q, qag)
        ckvn, vjp_ckv = jax.vjp(lambda a, b: _rms(a, b, KV_LORA), c_kv, kvag)
        q_raw = _mm(cqn, wqb)
        k_raw = _mm(ckvn, wk) + jnp.concatenate([kr] * MLA_HEADS, axis=-1)
        _, vjp_qn = jax.vjp(lambda a, b: _head_rms(a, b, MLA_HEADS, D_QK), q_raw, qng)
        _, vjp_kn = jax.vjp(lambda a, b: _head_rms(a, b, MLA_HEADS, D_QK), k_raw, kng)
        dq_raw, dqng = vjp_qn(_heads(_rope_t, jnp.transpose(dq_[0]), MLA_HEADS, c, s1, s2))
        dk_raw, dkng = vjp_kn(_heads(_rope_t, dk_, MLA_HEADS, c, s1, s2))
        dkr = dk_raw[:, :LANE]
        for h in range(1, MLA_HEADS):
            dkr = dkr + dk_raw[:, h * LANE:(h + 1) * LANE]
        dcq, dqag = vjp_cq(_mm_nt(dq_raw, wqb))
        dckv, dkvag = vjp_ckv(_mm_nt(dk_raw, wk) + _mm_nt(dv_, wv))
        dps = jnp.concatenate([dcq, dckv, dkr], axis=-1)
        return (dps, _mm_tn(cqn, dq_raw), _mm_tn(ckvn, dk_raw), _mm_tn(ckvn, dv_), dqag, dkvag, dqng, dkng)

    qkv_consts = [p['q_a_norm_g'], p['w_qb'], p['kv_a_norm_g'], p['w_k'], p['w_v'], p['q_norm_g'], p['k_norm_g']]
    (dp_s, g['w_qb'], g['w_k'], g['w_v'], g['q_a_norm_g'], g['kv_a_norm_g'], g['q_norm_g'], g['k_norm_g']) = _rows(
        qkv_b, name=name + "_b_qkv", n=nt, ins=[sv['p_s'], *tabs, dq_t, dk, dv, *qkv_consts],
        in_specs=[_rt(tm, SMALL_W)] + [_rt(tm, LANE)] * 3
        + [pl.BlockSpec((1, MLA_PAD, tm), lambda i: (i // (dq_t.shape[2] // tm), 0, i % (dq_t.shape[2] // tm)))]
        + [_rt(tm, MLA_PAD)] * 2 + [_full(a.shape) for a in qkv_consts],
        outs=[_sds((l, SMALL_W)), _sds((Q_LORA, MLA_PAD)), _sds((KV_LORA, MLA_PAD)), _sds((KV_LORA, MLA_PAD)),
              _sds((1, Q_LORA)), _sds((1, KV_LORA)), _sds((1, LANE)), _sds((1, LANE))],
        out_specs=[_rt(tm, SMALL_W), _full((Q_LORA, MLA_PAD)), _full((KV_LORA, MLA_PAD)), _full((KV_LORA, MLA_PAD)),
                   _full((1, Q_LORA)), _full((1, KV_LORA)), _full((1, LANE)), _full((1, LANE))], n_acc=7)

    x0 = sv['x0']
    dh = _matmul(name + "_b_in", [(dp_g, p['w_g']), (dp_u, p['w_u']), (dp_xq, p['w_xq']), (dp_s, p['w_s'])], l, D_MODEL, nt=True,
                 tm=256)
    gm = p['norm_mix_g']
    g['w_g'] = _matmul_tn(name + "_gw_g", x0, dp_g, rms_gain=gm)
    g['w_u'] = _matmul_tn(name + "_gw_u", x0, dp_u, rms_gain=gm)
    g['w_xq'] = _matmul_tn(name + "_gw_xq", x0, dp_xq, rms_gain=gm)
    g['w_s'] = _matmul_tn(name + "_gw_s", x0, dp_s, rms_gain=gm)
    dx0, g['norm_mix_g'] = _rows(norm_b, name=name + "_b_norm1", n=nt, ins=[x0, dh, dx1, gm],
                                 in_specs=[_rt(tm, D_MODEL)] * 3 + [_full((1, D_MODEL))],
                                 outs=[_sds((l, D_MODEL)), _sds((1, D_MODEL))], out_specs=[_rt(tm, D_MODEL), _full((1, D_MODEL))],
                                 n_acc=1)
    return dx0, g


def _unprep_grads(g):
    o = {}
    ws = g['w_s']
    o['w_in'] = jnp.concatenate([ws[:, :Q_LORA + KV_LORA], ws[:, Q_LORA + KV_LORA + D_NOPE:Q_LORA + KV_LORA + D_QK],
                                 g['w_u'], g['w_xq'], g['w_g']], axis=1)
    o['w_q_b'] = g['w_qb'].reshape(Q_LORA, MLA_HEADS, HEAD_PAD)[:, :, :D_QK].reshape(Q_LORA, MLA_HEADS * D_QK)
    gk = g['w_k'].reshape(KV_LORA, MLA_HEADS, HEAD_PAD)[:, :, :D_NOPE]
    gv = g['w_v'].reshape(KV_LORA, MLA_HEADS, HEAD_PAD)[:, :, :D_V]
    o['w_kv_b'] = jnp.concatenate([gk, gv], axis=2).reshape(KV_LORA, MLA_HEADS * (D_NOPE + D_V))
    o['w_o_mla'] = g['w_oa'].reshape(MLA_HEADS, HEAD_PAD, D_MODEL)[:, :D_V].reshape(MLA_HEADS * D_V, D_MODEL)
    for n in ('w_glu', 'w_o_ssm', 'w_mem_kv', 'w_o_cross', 'w_out', 'w_up', 'w_down', 'conv_w'):
        o[n] = g[n]
    for n in ('norm_mix_g', 'q_a_norm_g', 'kv_a_norm_g', 'b_glu', 'mem_norm_g', 'xq_norm_g', 'xk_norm_g', 'b_gate',
              'norm_ffn_g', 'conv_b'):
        o[n] = g[n].reshape(-1)
    o['q_norm_g'] = g['q_norm_g'].reshape(-1)[:D_QK]
    o['k_norm_g'] = g['k_norm_g'].reshape(-1)[:D_QK]
    o['ssm_d'] = g['ssm_d'].reshape(SSM_GROUPS, SSM_GROUP_CH)
    o['ssm_lambda_re'] = g['lr']
    o['ssm_lambda_im'] = g['li']
    o['ssm_log_dt'] = g['log_dt'].reshape(SSM_GROUPS)
    o['ssm_b_re'] = g['br'].transpose(1, 2, 0)
    o['ssm_b_im'] = g['bi'].transpose(1, 2, 0)
    dc = g['c_mat']
    o['ssm_c_re'] = _blockdiag_t(dc[:, :512], SSM_STATE, SSM_GROUP_CH).transpose(0, 1, 3, 2).reshape(SSM_GROUPS, SSM_GROUP_CH, SSM_STATE)
    o['ssm_c_im'] = -_blockdiag_t(dc[:, 512:], SSM_STATE, SSM_GROUP_CH).transpose(0, 1, 3, 2).reshape(SSM_GROUPS, SSM_GROUP_CH, SSM_STATE)
    return o


def _local_step(x, mem, pos, target, w, late_gather=None):
    l = x.shape[0]
    tm = min(512, l)
    tabs = _rope_tables(pos.astype(F32).reshape(l, 1))
    saved = []
    ps = []
    h = x
    for i in range(DEPTH):
        ps.append(_prep_layer(w, i))
        riding = late_gather[:2] if (late_gather is not None and i == 0) else None
        h, sv = _layer_fwd("l%d" % i, h, tabs, mem, ps[i], gather=riding)
        if riding is not None:
            for n, v in late_gather[2](sv.pop('gathered')).items():
                w[n][late_gather[1]] = v
        saved.append(sv)

    def loss_fn(i, y, t):
        e = y - t
        per_tok = jnp.sum(e * e, axis=-1, keepdims=True) * (1.0 / D_MODEL)
        tot = 0.5 * jnp.sum(per_tok, axis=0, keepdims=True)
        return e * (1.0 / D_MODEL), jnp.broadcast_to(tot, (1, LANE))

    dy, loss = _rows(loss_fn, name="loss", n=l // tm, ins=[h, target], in_specs=[_rt(tm, D_MODEL)] * 2,
                     outs=[_sds((l, D_MODEL)), _sds((1, LANE))], out_specs=[_rt(tm, D_MODEL), _full((1, LANE))], n_acc=1)
    grads = []
    d = dy
    for i in reversed(range(DEPTH)):
        d, g = _layer_bwd("l%d" % i, d, saved[i], tabs, mem, ps[i])
        grads.append(_unprep_grads(g))
    return loss[0, 0], d, grads[::-1]


def _sum_picked(name, slots, pick, extra, out_dtype):
    _, r, c = slots.shape
    e = extra.shape[0]
    tr = _row_tile(r)

    def body(pk, s_ref, x_ref, o_ref):
        acc = s_ref[...].astype(F32)
        for k in range(e):
            acc = acc + x_ref[k].astype(F32)
        o_ref[...] = acc.astype(o_ref.dtype)

    grid_spec = pltpu.PrefetchScalarGridSpec(
        num_scalar_prefetch=1, grid=(r // tr,),
        in_specs=[pl.BlockSpec((None, tr, c), lambda i, pk: (pk[0], i, 0)), pl.BlockSpec((e, tr, c), lambda i, pk: (0, i, 0))],
        out_specs=pl.BlockSpec((tr, c), lambda i, pk: (i, 0)))
    return pl.pallas_call(body, name=name, grid_spec=grid_spec, out_shape=_sds((r, c), out_dtype),
                          compiler_params=_params(("arbitrary",), 48))(pick, slots, extra)


def _row_tile(r):
    for t in (256, 128, 64, 32, 16, 8):
        if r % t == 0:
            return t
    return r


def _adamw(name, parts, w, m, v):
    r, cw = w.shape
    tr = _row_tile(r)
    np_ = len(parts)

    def fn(i, *vals):
        wv, mv, vv = vals[np_:]
        terms = []
        for pv in vals[:np_]:
            terms += [pv] if pv.ndim == 2 else [pv[k] for k in range(pv.shape[0])]
        g = terms[0]
        for t in terms[1:]:
            g = g + t
        mn = ADAM_B1 * mv + (1.0 - ADAM_B1) * g
        vn = ADAM_B2 * vv + (1.0 - ADAM_B2) * (g * g)
        m_hat = mn / (1.0 - ADAM_B1 ** ADAM_STEP)
        v_hat = vn / (1.0 - ADAM_B2 ** ADAM_STEP)
        delta = -ADAM_LR * (m_hat / (jnp.sqrt(v_hat) + ADAM_EPS) + ADAM_WD * wv)
        return g, delta, mn, vn

    pspecs = [_rt(tr, cw) if p.ndim == 2 else pl.BlockSpec((p.shape[0], tr, cw), lambda i: (0, i, 0)) for p in parts]
    return _rows(fn, name=name, n=r // tr, ins=[*parts, w, m, v], in_specs=pspecs + [_rt(tr, cw)] * 3,
                 outs=[_sds((r, cw))] * 4, out_specs=[_rt(tr, cw)] * 4)


def _shard_of(a, axis, k):
    n = a.shape[axis] // 4
    return lax.slice_in_dim(a, k * n, (k + 1) * n, axis=axis)


def _step(a):
    x = a['x'][0]
    mem = a['mem'][0]
    pos = a['positions'][0]
    target = a['loss_target'][0]

    me = 2 * lax.axis_index("x") + lax.axis_index("y")

    mine = [a[n] if n == 'conv_w' else a[n].astype(BF16) for n in SHARDED]

    def assemble(bufs, layer):
        return {n: jnp.concatenate([jnp.where(me == k, own[layer], y[k]) for k in range(4)], axis=SHARD_AXIS[n] - 1)
                for n, own, y in zip(SHARDED, mine, bufs)}

    w = {n: [v, None] for n, v in assemble(_layer_gather("comm_gather_l0", mine, 0), 0).items()}
    for n in SMALL:
        w[n] = a[n]

    loss, grad_x, grads = _local_step(x, mem, pos, target, w, late_gather=(mine, 1, lambda bufs: assemble(bufs, 1)))

    mc = lax.axis_index("c")
    mc1 = mc.astype(jnp.int32).reshape(1)
    me1 = me.astype(jnp.int32).reshape(1)
    gsh = []
    for n in SHARDED:
        ax = SHARD_AXIS[n] - 1
        gsh.append(jnp.stack([jnp.stack([_shard_of(grads[i][n], ax, k) for k in range(4)], axis=0)
                              for i in range(DEPTH)], axis=0).astype(BF16))
    sib = _swap_d2d("comm_reduce_pair", gsh, other_layer=True)
    pair = []
    for n, g, s in zip(SHARDED, gsh, sib):
        rows, cols = g.shape[-2:]
        pair.append(_sum_picked("sum2_" + n, g.reshape(DEPTH, 4 * rows, cols), mc1, s.reshape(1, 4 * rows, cols), BF16)
                    .reshape(4, rows, cols))
    gsm = _pack([jnp.stack([grads[i][n] for i in range(DEPTH)], axis=0) for n in SMALL] + [loss.reshape(1)], 8, F32)
    got, alls = _reduce_ici(pair, gsm)
    parts = [_sum_picked("sum4_" + n, p4, me1, g3, F32) for n, p4, g3 in zip(SHARDED, pair, got)]
    others = _swap_d2d("comm_reduce_d2d", parts, other_layer=False)
    res_sh = []
    for n, part, other in zip(SHARDED, parts, others):
        cols = part.shape[-1]
        full = jnp.where(mc == 0, jnp.stack([part, other], axis=0), jnp.stack([other, part], axis=0))
        res = _adamw("adamw_" + n, [full.reshape(-1, cols)], *[a[pre + n].reshape(-1, cols) for pre in ('', 'm_', 'v_')])
        res_sh.append([r.reshape(a[n].shape) for r in res])
    res_sh = [[res_sh[j][kind] for j in range(len(SHARDED))] for kind in range(4)]

    sm_shapes = [a[n].shape for n in SMALL] + [(1,)]
    zero1 = jnp.zeros((1,), F32)
    res_sm = _adamw("adamw_small", [alls], *[_pack([a[pre + n] for n in SMALL] + [zero1], 8, F32) for pre in ('', 'm_', 'v_')])
    res_sm = [_unpack(r, sm_shapes) for r in res_sm]
    loss = res_sm[0][-1][0]

    outs = [loss, grad_x[None]]
    for kind in range(4):
        byname = dict(zip(SHARDED, res_sh[kind]))
        byname.update(zip(SMALL, res_sm[kind]))
        outs += [byname[n] for n in WEIGHTS]
    return tuple(outs)


def kernel(x, mem, positions, norm_mix_g, w_in, q_a_norm_g, w_q_b, kv_a_norm_g, w_kv_b, q_norm_g, k_norm_g, w_o_mla, ssm_lambda_re, ssm_lambda_im, ssm_log_dt, ssm_b_re, ssm_b_im, ssm_c_re, ssm_c_im, ssm_d, w_glu, b_glu, w_o_ssm, mem_norm_g, w_mem_kv, xq_norm_g, xk_norm_g, w_o_cross, b_gate, w_out, norm_ffn_g, w_up, conv_w, conv_b, w_down, loss_target, m_norm_mix_g, m_w_in, m_q_a_norm_g, m_w_q_b, m_kv_a_norm_g, m_w_kv_b, m_q_norm_g, m_k_norm_g, m_w_o_mla, m_ssm_lambda_re, m_ssm_lambda_im, m_ssm_log_dt, m_ssm_b_re, m_ssm_b_im, m_ssm_c_re, m_ssm_c_im, m_ssm_d, m_w_glu, m_b_glu, m_w_o_ssm, m_mem_norm_g, m_w_mem_kv, m_xq_norm_g, m_xk_norm_g, m_w_o_cross, m_b_gate, m_w_out, m_norm_ffn_g, m_w_up, m_conv_w, m_conv_b, m_w_down, v_norm_mix_g, v_w_in, v_q_a_norm_g, v_w_q_b, v_kv_a_norm_g, v_w_kv_b, v_q_norm_g, v_k_norm_g, v_w_o_mla, v_ssm_lambda_re, v_ssm_lambda_im, v_ssm_log_dt, v_ssm_b_re, v_ssm_b_im, v_ssm_c_re, v_ssm_c_im, v_ssm_d, v_w_glu, v_b_glu, v_w_o_ssm, v_mem_norm_g, v_w_mem_kv, v_xq_norm_g, v_xk_norm_g, v_w_o_cross, v_b_gate, v_w_out, v_norm_ffn_g, v_w_up, v_conv_w, v_conv_b, v_w_down):
    return _step(dict(locals()))
```

```python
import functools
import math

import numpy as np
import jax
import jax.numpy as jnp
from jax import lax
from jax.experimental import pallas as pl
from jax.experimental.pallas import tpu as pltpu

F32 = jnp.float32
BF16 = jnp.bfloat16
MESH = pl.DeviceIdType.MESH

DEPTH = 2
D_MODEL = 1024
EPS = 1e-6
MLA_HEADS = 8
Q_LORA = 384
KV_LORA = 256
D_NOPE = 64
D_ROPE = 32
D_QK = D_NOPE + D_ROPE
D_V = 64
HEAD_PAD = 128
MLA_PAD = MLA_HEADS * HEAD_PAD
ROPE_THETA = 10000.0
SSM_GROUPS = 32
SSM_GROUP_CH = 16
SSM_WIDTH = 512
SSM_STATE = 64
SSM_LANES = SSM_GROUPS * SSM_STATE
SSM_JB = 4
X_HEADS = 4
X_HEAD_DIM = 128
X_WIDTH = 512
D_FF = 2816
SMALL_W = Q_LORA + KV_LORA + HEAD_PAD
SCAN_SEGS = 32
SSM_STATE_DTYPE = BF16
LANE = 128
NEG = -1e30

ADAM_LR = 0.001
ADAM_B1 = 0.9
ADAM_B2 = 0.999
ADAM_EPS = 1e-08
ADAM_WD = 0.01
ADAM_STEP = 10

WEIGHTS = ['norm_mix_g', 'w_in', 'q_a_norm_g', 'w_q_b', 'kv_a_norm_g', 'w_kv_b', 'q_norm_g', 'k_norm_g', 'w_o_mla',
           'ssm_lambda_re', 'ssm_lambda_im', 'ssm_log_dt', 'ssm_b_re', 'ssm_b_im', 'ssm_c_re', 'ssm_c_im', 'ssm_d',
           'w_glu', 'b_glu', 'w_o_ssm', 'mem_norm_g', 'w_mem_kv', 'xq_norm_g', 'xk_norm_g', 'w_o_cross', 'b_gate',
           'w_out', 'norm_ffn_g', 'w_up', 'conv_w', 'conv_b', 'w_down']
SHARD_AXIS = {'w_in': 2, 'w_q_b': 2, 'w_kv_b': 2, 'w_o_mla': 2, 'w_glu': 1, 'w_o_ssm': 2, 'w_mem_kv': 1,
              'w_o_cross': 2, 'w_out': 1, 'w_up': 2, 'conv_w': 2, 'w_down': 1}
SHARDED = [n for n in WEIGHTS if n in SHARD_AXIS]
GATHER_BF16 = [n for n in SHARDED if n != 'conv_w']
SMALL = [n for n in WEIGHTS if n not in SHARD_AXIS]


def _bf(v):
    return v.astype(BF16)


def _mm(a, b):
    return jnp.dot(_bf(a), _bf(b), preferred_element_type=F32)


def _mm_nt(a, b):
    return lax.dot_general(_bf(a), _bf(b), (((1,), (1,)), ((), ())), preferred_element_type=F32)


def _mm_tn(a, b):
    return lax.dot_general(_bf(a), _bf(b), (((0,), (0,)), ((), ())), preferred_element_type=F32)


def _rms(v, g, n):
    ms = jnp.sum(v * v, axis=-1, keepdims=True) * (1.0 / n)
    return (v * lax.rsqrt(ms + EPS)) * g


def _head_rms(v, g, heads, n):
    return jnp.concatenate([_rms(v[:, h * LANE:(h + 1) * LANE], g, n) for h in range(heads)], axis=-1)


def _rope(v, c, s1, s2):
    return v * c + pltpu.roll(v, LANE - 16, 1) * s1 + pltpu.roll(v, 16, 1) * s2


def _rope_t(g, c, s1, s2):
    return g * c + pltpu.roll(g * s1, 16, 1) + pltpu.roll(g * s2, LANE - 16, 1)


def _heads(fn, v, heads, *tabs):
    return jnp.concatenate([fn(v[:, h * LANE:(h + 1) * LANE], *tabs) for h in range(heads)], axis=-1)


def _gelu(y):
    return y * (0.5 * (1.0 + jnp.tanh(math.sqrt(2.0 / math.pi) * (y + 0.044715 * (y * y * y)))))


def _silu(g):
    return g * jax.nn.sigmoid(g)


def _colsum(v):
    return jnp.sum(v, axis=0, keepdims=True)


def _row_select(rows, n):
    rid = lax.broadcasted_iota(jnp.int32, (n, rows[0].shape[-1]), 0)
    out = jnp.zeros((n, rows[0].shape[-1]), F32)
    for k, r in enumerate(rows):
        out = jnp.where(rid == k, jnp.broadcast_to(r, out.shape), out)
    return out


def _params(sem, vmem_mb):
    return pltpu.CompilerParams(dimension_semantics=sem, vmem_limit_bytes=vmem_mb * 1024 * 1024)


def _rt(tm, w, cb=0):
    return pl.BlockSpec((tm, w), lambda i: (i, cb))


def _full(shape):
    nd = len(shape)
    return pl.BlockSpec(tuple(shape), lambda i: (0,) * nd)


def _rows(fn, *, name, n, ins, in_specs, outs, out_specs, n_acc=0, vmem=48):
    n_in = len(ins)
    n_out = len(outs)

    def body(*refs):
        i = pl.program_id(0)
        res = fn(i, *[r[...] for r in refs[:n_in]])
        if not isinstance(res, (tuple, list)):
            res = (res,)
        assert len(res) == n_out, (name, len(res), n_out)
        for k, (r, v) in enumerate(zip(refs[n_in:], res)):
            if k < n_out - n_acc:
                r[...] = v.astype(r.dtype)
            else:
                @pl.when(i == 0)
                def _():
                    r[...] = v

                @pl.when(i > 0)
                def _():
                    r[...] += v

    return pl.pallas_call(
        body, name=name, grid=(n,), in_specs=list(in_specs), out_specs=tuple(out_specs), out_shape=tuple(outs),
        compiler_params=_params(("arbitrary",), vmem))(*ins)


def _sds(shape, dtype=F32):
    return jax.ShapeDtypeStruct(tuple(shape), dtype)


def _tile_n(n, cap=1536):
    best = None
    for t in range(LANE, min(n, cap) + 1, LANE):
        if n % t == 0:
            best = t
    if best is None or n <= 1408:
        return n
    return best


def _matmul(name, pairs, m, n, *, nt=False, rms_gain=None, resid=None, out_dtype=F32, tm=1024, vmem=56):
    tm = min(tm, m)
    tn = _tile_n(n)
    ks = [a.shape[1] for a, _ in pairs]
    np_ = len(pairs)

    def body(*refs):
        a_refs = refs[:np_]
        b_refs = refs[np_:2 * np_]
        k = 2 * np_
        g_ref = None
        r_ref = None
        if rms_gain is not None:
            g_ref = refs[k]
            k += 1
        if resid is not None:
            r_ref = refs[k]
            k += 1
        o_ref = refs[k]
        scr = refs[k + 1:]
        j = pl.program_id(1)

        @pl.when(j == 0)
        def _():
            for p in range(np_):
                a = a_refs[p][...]
                if p == 0 and g_ref is not None:
                    a = _rms(a.astype(F32), g_ref[...], ks[0])
                scr[p][...] = a.astype(BF16)

        acc = None
        for p in range(np_):
            b = b_refs[p][...].astype(BF16)
            if nt:
                t = lax.dot_general(scr[p][...], b, (((1,), (1,)), ((), ())), preferred_element_type=F32)
            else:
                t = jnp.dot(scr[p][...], b, preferred_element_type=F32)
            acc = t if acc is None else acc + t
        if r_ref is not None:
            acc = acc + r_ref[...]
        o_ref[...] = acc.astype(o_ref.dtype)

    in_specs = [pl.BlockSpec((tm, kk), lambda i, j: (i, 0)) for kk in ks]
    if nt:
        in_specs += [pl.BlockSpec((tn, kk), lambda i, j: (j, 0)) for kk in ks]
    else:
        in_specs += [pl.BlockSpec((kk, tn), lambda i, j: (0, j)) for kk in ks]
    ins = [a for a, _ in pairs] + [b for _, b in pairs]
    if rms_gain is not None:
        in_specs.append(pl.BlockSpec((1, ks[0]), lambda i, j: (0, 0)))
        ins.append(rms_gain)
    if resid is not None:
        in_specs.append(pl.BlockSpec((tm, tn), lambda i, j: (i, j)))
        ins.append(resid)
    return pl.pallas_call(
        body, name=name, grid=(m // tm, n // tn), in_specs=in_specs,
        out_specs=pl.BlockSpec((tm, tn), lambda i, j: (i, j)), out_shape=_sds((m, n), out_dtype),
        scratch_shapes=[pltpu.VMEM((tm, kk), BF16) for kk in ks],
        compiler_params=_params(("arbitrary", "arbitrary"), vmem))(*ins)


def _matmul_tn(name, a, b, *, rms_gain=None, tl=1024, vmem=56):
    l, ka = a.shape
    n = b.shape[1]
    tl = min(tl, l)
    tn = _tile_n(n, 1536)

    def body(*refs):
        if rms_gain is not None:
            a_ref, b_ref, g_ref, o_ref = refs
        else:
            a_ref, b_ref, o_ref = refs
        t = pl.program_id(1)
        av = a_ref[...]
        if rms_gain is not None:
            av = _rms(av.astype(F32), g_ref[...], ka)
        v = _mm_tn(av, b_ref[...])

        @pl.when(t == 0)
        def _():
            o_ref[...] = v

        @pl.when(t > 0)
        def _():
            o_ref[...] += v

    in_specs = [pl.BlockSpec((tl, ka), lambda j, t: (t, 0)), pl.BlockSpec((tl, tn), lambda j, t: (t, j))]
    ins = [a, b]
    if rms_gain is not None:
        in_specs.append(pl.BlockSpec((1, ka), lambda j, t: (0, 0)))
        ins.append(rms_gain)
    return pl.pallas_call(
        body, name=name, grid=(n // tn, l // tl), in_specs=in_specs,
        out_specs=pl.BlockSpec((ka, tn), lambda j, t: (0, j)), out_shape=_sds((ka, n)),
        compiler_params=_params(("arbitrary", "arbitrary"), vmem))(*ins)


ATT_HEADS_PER_STEP = 2
ATT_W = ATT_HEADS_PER_STEP * LANE
ATT_GROUPS = MLA_HEADS // ATT_HEADS_PER_STEP
LOG2E = math.log2(math.e)
ATT_FWD_TILE = 1024
ATT_BWD_TILE = 1024
ATT_BWD_HEADS = 1
ATT_SCALE = D_QK ** -0.5
ATT_QSCALE = ATT_SCALE * LOG2E


def _tri_tables(nq, by_k):
    qs, ks = [], []
    if by_k:
        for ki in range(nq):
            for qi in range(ki, nq):
                qs.append(qi)
                ks.append(ki)
    else:
        for qi in range(nq):
            for ki in range(qi + 1):
                qs.append(qi)
                ks.append(ki)
    return jnp.asarray(np.array(qs, np.int32)), jnp.asarray(np.array(ks, np.int32))


def _causal_keep(shape, transposed):
    r = lax.broadcasted_iota(jnp.int32, shape, 0)
    c = lax.broadcasted_iota(jnp.int32, shape, 1)
    return (r <= c) if transposed else (c <= r)


def _nt16(a, b):
    return lax.dot_general(a, b, (((1,), (1,)), ((), ())), preferred_element_type=F32)


def _row_form(col):
    return jnp.transpose(jnp.broadcast_to(col, (col.shape[0], LANE)))[:8]


def _att_call(body, name, l, tq, tabs, ins, in_specs, outs, out_specs, scratch=(), groups=ATT_GROUPS, vmem=48):
    grid_spec = pltpu.PrefetchScalarGridSpec(
        num_scalar_prefetch=2, grid=(groups, tabs[0].shape[0]), in_specs=in_specs, out_specs=out_specs,
        scratch_shapes=list(scratch))
    return pl.pallas_call(body, name=name, grid_spec=grid_spec, out_shape=outs,
                          compiler_params=_params(("arbitrary", "arbitrary"), vmem))(*tabs, *ins)


def _flash_fwd(name, q, k, v_t, gather=None):
    l = q.shape[0]
    tq = min(ATT_FWD_TILE, l)
    nq = l // tq
    tabs = _tri_tables(nq, by_k=False)
    gx, glayer = gather if gather is not None else ([], 0)
    ng = len(gx)
    n_steps = int(tabs[0].shape[0])

    def body(qt, kt, *refs):
        q_ref, k_ref, vt_ref = refs[:3]
        gx_refs = refs[3:3 + ng]
        o_ref, lset_ref = refs[3 + ng:5 + ng]
        gy_refs = refs[5 + ng:5 + 2 * ng]
        m_s, acc_s = refs[5 + 2 * ng:7 + 2 * ng]
        t = pl.program_id(1)
        qi = qt[t]
        ki = kt[t]
        sls = [slice(h * LANE, (h + 1) * LANE) for h in range(ATT_HEADS_PER_STEP)]
        if ng:
            g_start, g_finish = _layer_gather_steps(gx_refs, gy_refs, refs[7 + 2 * ng:9 + 2 * ng], glayer)

            @pl.when((pl.program_id(0) == 0) & (t == 0))
            def _():
                g_start()

        @pl.when(ki == 0)
        def _():
            m_s[...] = jnp.full(m_s.shape, NEG, F32)
            acc_s[...] = jnp.zeros(acc_s.shape, F32)

        def step(masked):
            sts = [_nt16(k_ref[:, sl], q_ref[:, sl]) for sl in sls]
            for h, sl in enumerate(sls):
                st = sts[h]
                if masked:
                    st = jnp.where(_causal_keep(st.shape, True), st, NEG)
                m_old = m_s[h][:1]
                m_new = jnp.maximum(m_old, jnp.max(st, axis=0, keepdims=True))
                alpha = jnp.exp2(m_old - m_new)
                pt = jnp.exp2(st - m_new).astype(BF16)
                acc_s[sl, :] = alpha * acc_s[sl, :] + jnp.dot(vt_ref[sl, :], pt, preferred_element_type=F32)
                m_s[h] = jnp.broadcast_to(m_new, (8, tq))

        @pl.when(ki < qi)
        def _():
            step(False)

        @pl.when(ki == qi)
        def _():
            step(True)
            row = lax.broadcasted_iota(jnp.int32, (LANE, tq), 0)
            for h, sl in enumerate(sls):
                acc = acc_s[sl, :]
                lsum = acc[D_V:D_V + 1, :]
                o_ref[:, sl] = jnp.transpose(jnp.where(row < D_V, acc / lsum, 0.0))
                lset_ref[h * 8:(h + 1) * 8, :] = m_s[h] + jnp.log2(lsum)

        if ng:
            @pl.when((pl.program_id(0) == ATT_GROUPS - 1) & (t == n_steps - 1))
            def _():
                g_finish()

    qspec = pl.BlockSpec((tq, ATT_W), lambda g, t, qt, kt: (qt[t], g))
    kspec = pl.BlockSpec((tq, ATT_W), lambda g, t, qt, kt: (kt[t], g))
    vspec = pl.BlockSpec((ATT_W, tq), lambda g, t, qt, kt: (g, kt[t]))
    rspec = pl.BlockSpec((8 * ATT_HEADS_PER_STEP, tq), lambda g, t, qt, kt: (g, qt[t]))
    res = _att_call(
        body, name, l, tq, tabs, [q, k, v_t, *gx], [qspec, kspec, vspec] + [ANY] * ng,
        (_sds((l, MLA_PAD)), _sds((8 * MLA_HEADS, l)), *[_sds((4,) + x.shape[1:], x.dtype) for x in gx]),
        (qspec, rspec, *([ANY] * ng)),
        scratch=[pltpu.VMEM((ATT_HEADS_PER_STEP, 8, tq), F32), pltpu.VMEM((ATT_W, tq), F32)]
        + ([pltpu.SemaphoreType.DMA((6 * ng,)), pltpu.SemaphoreType.DMA((6 * ng,))] if ng else []))
    return res[0], res[1], list(res[2:])


def _flash_bwd(name, q, k, v, k_t, o, lse_t, do, reduce=None):
    l = q.shape[0]
    tq = min(ATT_BWD_TILE, l)
    nq = l // tq
    hb = ATT_BWD_HEADS
    wb = hb * LANE

    def delta_fn(i, dov, ov):
        rows = []
        for h in range(MLA_HEADS):
            sl = slice(h * LANE, (h + 1) * LANE)
            rows.append(_row_form(jnp.sum(dov[:, sl] * ov[:, sl], axis=-1, keepdims=True)))
        return jnp.concatenate(rows, axis=0), dov

    delta_t, do16 = _rows(
        delta_fn, name=name + "_delta", n=nq, ins=[do, o], in_specs=[_rt(tq, MLA_PAD)] * 2,
        outs=[_sds((8 * MLA_HEADS, l)), _sds((l, MLA_PAD), BF16)],
        out_specs=[pl.BlockSpec((8 * MLA_HEADS, tq), lambda i: (0, i)), _rt(tq, MLA_PAD)])

    rg, rlayer = reduce if reduce is not None else ([], 0)
    nr = len(rg)
    groups = MLA_HEADS // hb
    tabs_k = _tri_tables(nq, by_k=True)
    n_steps = int(tabs_k[0].shape[0])

    def body(qt, kt, *refs):
        q_ref, k_ref, v_ref, do_ref, kt_ref, lset_ref, dlt_ref = refs[:7]
        rg_refs = refs[7:7 + nr]
        dk_ref, dv_ref, dqt_ref = refs[7 + nr:10 + nr]
        ry_refs = refs[10 + nr:10 + 2 * nr]
        t = pl.program_id(1)
        qi = qt[t]
        ki = kt[t]
        sls = [slice(h * LANE, (h + 1) * LANE) for h in range(hb)]
        if nr:
            r_start, r_finish = _layer_reduce_steps(rg_refs, ry_refs, refs[10 + 2 * nr:12 + 2 * nr], rlayer)

            @pl.when((pl.program_id(0) == 0) & (t == 0))
            def _():
                r_start()

        @pl.when(ki == 0)
        def _():
            dqt_ref[qi] = jnp.zeros((wb, tq), F32)

        def step(masked):
            sts = [_nt16(k_ref[:, sl], q_ref[:, sl]) for sl in sls]
            dpts = [_nt16(v_ref[:, sl], do_ref[:, sl]) for sl in sls]
            for h, sl in enumerate(sls):
                st = sts[h]
                if masked:
                    st = jnp.where(_causal_keep(st.shape, True), st, NEG)
                pt = jnp.exp2(st - lset_ref[h * 8:(h + 1) * 8, :][:1])
                dst = (pt * (dpts[h] - dlt_ref[h * 8:(h + 1) * 8, :][:1])).astype(BF16)
                dv_ref[:, sl] += jnp.dot(pt.astype(BF16), do_ref[:, sl], preferred_element_type=F32)
                dk_ref[:, sl] += jnp.dot(dst, q_ref[:, sl], preferred_element_type=F32)
                dqt_ref[qi, sl, :] += jnp.dot(kt_ref[sl, :], dst, preferred_element_type=F32)

        @pl.when(qi == ki)
        def _():
            dk_ref[...] = jnp.zeros(dk_ref.shape, F32)
            dv_ref[...] = jnp.zeros(dv_ref.shape, F32)
            step(True)
            dqt_ref[qi] = dqt_ref[qi] * ATT_SCALE

        @pl.when(qi > ki)
        def _():
            step(False)

        @pl.when(qi == nq - 1)
        def _():
            dk_ref[...] = dk_ref[...] * (1.0 / LOG2E)

        if nr:
            @pl.when((pl.program_id(0) == groups - 1) & (t == n_steps - 1))
            def _():
                r_finish()

    qspec = pl.BlockSpec((tq, wb), lambda g, t, qt, kt: (qt[t], g))
    kspec = pl.BlockSpec((tq, wb), lambda g, t, qt, kt: (kt[t], g))
    ktspec = pl.BlockSpec((wb, tq), lambda g, t, qt, kt: (g, kt[t]))
    rspec = pl.BlockSpec((8 * hb, tq), lambda g, t, qt, kt: (g, qt[t]))
    dqspec = pl.BlockSpec((nq, wb, tq), lambda g, t, qt, kt: (0, g, 0))
    res = _att_call(body, name + "_dqkv", l, tq, tabs_k, [q, k, v, do16, k_t, lse_t, delta_t, *rg],
                    [qspec, kspec, kspec, qspec, ktspec, rspec, rspec] + [ANY] * nr,
                    (_sds((l, MLA_PAD)), _sds((l, MLA_PAD)), _sds((nq, MLA_PAD, tq)),
                     *[_sds((3,) + g.shape[1:], g.dtype) for g in rg]),
                    (kspec, kspec, dqspec, *([ANY] * nr)),
                    scratch=([pltpu.SemaphoreType.DMA((3 * nr,)), pltpu.SemaphoreType.DMA((3 * nr,))] if nr else []),
                    groups=groups, vmem=56)
    dk, dv, dq_t = res[:3]
    return dq_t, dk, dv, list(res[3:])


def _cmul(ar, ai, br, bi):
    return ar * br - ai * bi, ar * bi + ai * br


def _scan(name, x_re, x_im, a_re, a_im, reverse):
    l, lanes = x_re.shape
    ns = SCAN_SEGS
    tl = l // ns
    steps = int(math.log2(tl))
    assert 2 ** steps == tl and tl * ns == l

    def body(xr_ref, xi_ref, ar_ref, ai_ref, sr_ref, si_ref):
        a_r1 = ar_ref[...]
        a_i1 = ai_ref[...]
        a_r = jnp.broadcast_to(a_r1, (ns, LANE))
        a_i = jnp.broadcast_to(a_i1, (ns, LANE))

        def rows(t):
            t = (tl - 1 - t) if reverse else t
            return pl.ds(pl.multiple_of(t * ns, ns), ns)

        def local(t, carry):
            cr, ci = carry
            r = rows(t)
            pr, pi = _cmul(a_r, a_i, cr, ci)
            return pr + xr_ref[r, :].astype(F32), pi + xi_ref[r, :].astype(F32)

        zero = jnp.zeros((ns, LANE), F32)
        e_r, e_i = lax.fori_loop(0, tl, local, (zero, zero), unroll=min(8, tl))
        p_r, p_i = a_r1, a_i1
        for _ in range(steps):
            p_r, p_i = _cmul(p_r, p_i, p_r, p_i)
        rid = lax.broadcasted_iota(jnp.int32, (ns, LANE), 0)
        c_r = jnp.zeros((1, LANE), F32)
        c_i = jnp.zeros((1, LANE), F32)
        in_r, in_i = zero, zero
        order = range(ns - 2, -1, -1) if reverse else range(1, ns)
        for kk in order:
            src = kk + 1 if reverse else kk - 1
            ek_r = jnp.sum(jnp.where(rid == src, e_r, 0.0), axis=0, keepdims=True)
            ek_i = jnp.sum(jnp.where(rid == src, e_i, 0.0), axis=0, keepdims=True)
            q_r, q_i = _cmul(p_r, p_i, c_r, c_i)
            c_r, c_i = q_r + ek_r, q_i + ek_i
            in_r = jnp.where(rid == kk, jnp.broadcast_to(c_r, (ns, LANE)), in_r)
            in_i = jnp.where(rid == kk, jnp.broadcast_to(c_i, (ns, LANE)), in_i)

        def final(t, carry):
            cr, ci = carry
            r = rows(t)
            pr, pi = _cmul(a_r, a_i, cr, ci)
            nr, ni = pr + xr_ref[r, :].astype(F32), pi + xi_ref[r, :].astype(F32)
            sr_ref[r, :] = nr.astype(sr_ref.dtype)
            si_ref[r, :] = ni.astype(si_ref.dtype)
            return nr, ni

        lax.fori_loop(0, tl, final, (in_r, in_i), unroll=min(8, tl))

    xs = pl.BlockSpec((l, LANE), lambda j: (0, j))
    as_ = pl.BlockSpec((1, LANE), lambda j: (0, j))
    return pl.pallas_call(
        body, name=name, grid=(lanes // LANE,), in_specs=[xs, xs, as_, as_], out_specs=(xs, xs),
        out_shape=(_sds((l, lanes), x_re.dtype), _sds((l, lanes), x_re.dtype)),
        compiler_params=_params(("arbitrary",), 48))(x_re, x_im, a_re, a_im)


ANY = pl.BlockSpec(memory_space=pl.ANY)


def _place():
    mx, my, mc = lax.axis_index("x"), lax.axis_index("y"), lax.axis_index("c")
    return mx, my, mc, [(1 - mx, my), (mx, 1 - my), (1 - mx, 1 - my)]


def _run_copies(copies):
    for cp in copies:
        cp.start()
    for cp in copies:
        cp.wait_recv()
    for cp in copies:
        cp.wait_send()


def _remote(src, dst, sems, k, dev):
    return pltpu.make_async_remote_copy(src_ref=src, dst_ref=dst, send_sem=sems[0].at[k], recv_sem=sems[1].at[k],
                                        device_id=dev, device_id_type=MESH)


def _copy_call(body, name, ins, outs, n_copies, aliases=None):
    return pl.pallas_call(
        body, name=name, in_specs=[ANY] * len(ins), out_specs=[ANY] * len(outs), out_shape=list(outs),
        input_output_aliases=aliases or {},
        scratch_shapes=[pltpu.SemaphoreType.DMA((n_copies,)), pltpu.SemaphoreType.DMA((n_copies,))])(*ins)


def _layer_gather_steps(x_refs, y_refs, sems, layer):
    n = len(x_refs)
    mx, my, mc, peers = _place()
    me = 2 * mx + my

    def ici(idx):
        i, j = divmod(idx, 3)
        return _remote(x_refs[i].at[layer], y_refs[i].at[me], sems, idx, (peers[j][0], peers[j][1], layer))

    def fwd(idx):
        i, j = divmod(idx, 3)
        pk = 2 * peers[j][0] + peers[j][1]
        return _remote(y_refs[i].at[pk], y_refs[i].at[pk], sems, 3 * n + idx, (mx, my, 1 - layer))

    def start():
        @pl.when(mc == layer)
        def _():
            for idx in range(3 * n):
                ici(idx).start()

    def finish():
        @pl.when(mc == layer)
        def _():
            for idx in range(3 * n):
                ici(idx).wait_recv()
                fwd(idx).start()
            for idx in range(3 * n):
                ici(idx).wait_send()
                fwd(idx).wait_send()

        @pl.when(mc != layer)
        def _():
            for idx in range(3 * n):
                fwd(idx).wait_recv()

    return start, finish


def _layer_gather(name, xs, layer):
    n = len(xs)

    def body(*refs):
        start, finish = _layer_gather_steps(refs[:n], refs[n:2 * n], refs[2 * n:], layer)
        start()
        finish()

    return _copy_call(body, name, xs, [_sds((4,) + x.shape[1:], x.dtype) for x in xs], 6 * n)


def _layer_reduce_steps(g_refs, y_refs, sems, layer):
    n = len(g_refs)
    mx, my, mc, peers = _place()

    def cp(idx):
        i, j = divmod(idx, 3)
        px, py = peers[j]
        return _remote(g_refs[i].at[2 * px + py], y_refs[i].at[j], sems, idx, (px, py, layer))

    def start():
        @pl.when(mc == layer)
        def _():
            for idx in range(3 * n):
                cp(idx).start()

    def finish():
        @pl.when(mc == layer)
        def _():
            for idx in range(3 * n):
                cp(idx).wait_recv()
            for idx in range(3 * n):
                cp(idx).wait_send()

    return start, finish


def _layer_reduce(name, gs, layer, small):
    n = len(gs)

    def body(*refs):
        g_refs, s_ref, y_refs, a_ref = refs[:n], refs[n], refs[n + 1:2 * n + 1], refs[2 * n + 1]
        sems, small_sems, local_sem = refs[2 * n + 2:2 * n + 4], refs[2 * n + 4:2 * n + 6], refs[2 * n + 6]
        mx, my, mc, _ = _place()
        me8 = 4 * mx + 2 * my + mc
        start, finish = _layer_reduce_steps(g_refs, y_refs, sems, layer)
        start()
        own = pltpu.make_async_copy(s_ref, a_ref.at[me8], local_sem)
        own.start()
        copies = []
        for j in range(1, 8):
            dev = ((1 - mx) if (j & 4) else mx, (1 - my) if (j & 2) else my, (1 - mc) if (j & 1) else mc)
            copies.append(_remote(s_ref, a_ref.at[me8], small_sems, j - 1, dev))
        _run_copies(copies)
        own.wait()
        finish()

    outs = [_sds((3,) + g.shape[1:], g.dtype) for g in gs] + [_sds((8,) + small.shape, small.dtype)]
    res = pl.pallas_call(
        body, name=name, in_specs=[ANY] * (n + 1), out_specs=[ANY] * (n + 1), out_shape=outs,
        scratch_shapes=[pltpu.SemaphoreType.DMA((3 * n,)), pltpu.SemaphoreType.DMA((3 * n,)),
                        pltpu.SemaphoreType.DMA((7,)), pltpu.SemaphoreType.DMA((7,)), pltpu.SemaphoreType.DMA])(*gs, small)
    return res[:n], res[n]


def _send_d2d(name, per_core):
    shapes = next(p for p in per_core if p is not None)
    n = len(shapes)
    ins = [x for p in per_core if p is not None for x in p]

    def body(*refs):
        o_refs, sems = refs[len(ins):len(ins) + n], refs[len(ins) + n:]
        mx, my, mc, _ = _place()
        off = 0
        for c in range(2):
            if per_core[c] is None:
                continue
            src = refs[off:off + n]
            off += n

            @pl.when(mc == c)
            def _():
                cps = [_remote(src[i], o_refs[i], sems, i, (mx, my, 1 - c)) for i in range(n)]
                for d in cps:
                    d.start()
                if per_core[1 - c] is not None:
                    for d in cps:
                        d.wait_recv()
                for d in cps:
                    d.wait_send()

            if per_core[1 - c] is None:
                @pl.when(mc != c)
                def _():
                    for i in range(n):
                        _remote(src[i], o_refs[i], sems, i, (mx, my, c)).wait_recv()

    return _copy_call(body, name, ins, [_sds(p.shape, p.dtype) for p in shapes], n)


PACK_W = 1024


def _pack(arrs, rows_multiple, dtype):
    flat = jnp.concatenate([a.reshape(-1).astype(dtype) for a in arrs])
    n = flat.shape[0]
    unit = PACK_W * rows_multiple
    tot = -(-n // unit) * unit
    flat = jnp.pad(flat, (0, tot - n))
    return flat.reshape(tot // PACK_W, PACK_W)


def _unpack(flat, shapes):
    flat = flat.reshape(-1)
    out = []
    off = 0
    for s in shapes:
        n = int(np.prod(s))
        out.append(flat[off:off + n].reshape(s))
        off += n
    return out


def _rope_tables(pos):
    l = pos.shape[0]
    tm = min(512, l)
    inv = (np.float32(ROPE_THETA) ** (-np.arange(0, D_ROPE, 2, dtype=np.float32) / np.float32(D_ROPE))).astype(np.float32)
    lane_f = np.zeros((1, LANE), np.float32)
    lane_f[0, D_NOPE:D_NOPE + 16] = inv
    lane_f[0, D_NOPE + 16:D_NOPE + 32] = inv

    def fn(i, p, f):
        ang = p * f
        lane = lax.broadcasted_iota(jnp.int32, ang.shape, 1)
        co = jnp.cos(ang)
        si = jnp.sin(ang)
        c = jnp.where(lane < D_NOPE, 1.0, jnp.where(lane < D_QK, co, 0.0))
        s1 = jnp.where((lane >= D_NOPE) & (lane < D_NOPE + 16), -si, 0.0)
        s2 = jnp.where((lane >= D_NOPE + 16) & (lane < D_QK), si, 0.0)
        return c, s1, s2

    return _rows(fn, name="rope_tables", n=l // tm, ins=[pos, jnp.asarray(lane_f)],
                 in_specs=[_rt(tm, 1), _full((1, LANE))], outs=[_sds((l, LANE))] * 3, out_specs=[_rt(tm, LANE)] * 3)


def _ssm_param_fn(lr, li, log_dt, br, bi):
    dt = jnp.exp(log_dt)
    mag = jnp.exp(lr * dt)
    a_re = mag * jnp.cos(li * dt)
    a_im = mag * jnp.sin(li * dt)
    den = lr * lr + li * li
    e_re = a_re - 1.0
    e_im = a_im
    f_re = (e_re * lr + e_im * li) / den
    f_im = (e_im * lr - e_re * li) / den
    bb_re = f_re[None] * br - f_im[None] * bi
    bb_im = f_re[None] * bi + f_im[None] * br
    return a_re, a_im, bb_re, bb_im


def _ssm_params(name, lr, li, log_dt, br, bi):
    g, n = lr.shape
    c = br.shape[0]
    return _rows(lambda i, *v: _ssm_param_fn(*v), name=name, n=1, ins=[lr, li, log_dt, br, bi],
                 in_specs=[_full((g, n)), _full((g, n)), _full((g, 1)), _full((c, g, n)), _full((c, g, n))],
                 outs=[_sds((g, n)), _sds((g, n)), _sds((c, g, n)), _sds((c, g, n))],
                 out_specs=[_full((g, n)), _full((g, n)), _full((c, g, n)), _full((c, g, n))])


def _ssm_params_bwd(name, lr, li, log_dt, br, bi, d_are, d_aim, d_bbre, d_bbim):
    g, n = lr.shape
    c = br.shape[0]

    def fn(i, lr, li, log_dt, br, bi, g0, g1, g2, g3):
        _, vjp = jax.vjp(_ssm_param_fn, lr, li, log_dt, br, bi)
        return vjp((g0, g1, g2, g3))

    sp = [_full((g, n)), _full((g, n)), _full((g, 1)), _full((c, g, n)), _full((c, g, n))]
    return _rows(fn, name=name, n=1, ins=[lr, li, log_dt, br, bi, d_are, d_aim, d_bbre, d_bbim],
                 in_specs=sp + [_full((g, n)), _full((g, n)), _full((c, g, n)), _full((c, g, n))],
                 outs=[_sds((g, n)), _sds((g, n)), _sds((g, 1)), _sds((c, g, n)), _sds((c, g, n))], out_specs=sp)


_EYE8 = np.eye(8, dtype=np.float32)


def _blockdiag(v):
    j, g, p, q = v.shape
    m = v[:, :, :, None, :] * jnp.asarray(_EYE8)[None, :, None, :, None]
    return m.reshape(j, g * p, g * q)


def _blockdiag_t(m, p, q):
    j = m.shape[0]
    m = m.reshape(j, 8, p, 8, q)
    return jnp.sum(m * jnp.asarray(_EYE8)[None, :, None, :, None], axis=3)


def _to_perm(v, l):
    ns = SCAN_SEGS
    return v.reshape(ns, l // ns, v.shape[-1]).transpose(1, 0, 2).reshape(l, v.shape[-1])


def _from_perm(v, l):
    ns = SCAN_SEGS
    return v.reshape(l // ns, ns, v.shape[-1]).transpose(1, 0, 2).reshape(l, v.shape[-1])


def _prep_layer(w, i):
    p = {}
    w_in = w['w_in'][i]
    z = lambda n: jnp.zeros((D_MODEL, n), w_in.dtype)
    o = Q_LORA + KV_LORA
    p['w_s'] = jnp.concatenate([w_in[:, :o], z(D_NOPE), w_in[:, o:o + D_ROPE], z(HEAD_PAD - D_QK)], axis=1)
    o += D_ROPE
    p['w_u'] = w_in[:, o:o + SSM_WIDTH]
    o += SSM_WIDTH
    p['w_xq'] = w_in[:, o:o + X_WIDTH]
    o += X_WIDTH
    p['w_g'] = w_in[:, o:]
    wq = w['w_q_b'][i].reshape(Q_LORA, MLA_HEADS, D_QK)
    p['w_qb'] = jnp.pad(wq, ((0, 0), (0, 0), (0, HEAD_PAD - D_QK))).reshape(Q_LORA, MLA_PAD)
    wkv = w['w_kv_b'][i].reshape(KV_LORA, MLA_HEADS, D_NOPE + D_V)
    p['w_k'] = jnp.pad(wkv[:, :, :D_NOPE], ((0, 0), (0, 0), (0, HEAD_PAD - D_NOPE))).reshape(KV_LORA, MLA_PAD)
    p['w_v'] = jnp.pad(wkv[:, :, D_NOPE:], ((0, 0), (0, 0), (0, HEAD_PAD - D_V))).reshape(KV_LORA, MLA_PAD)
    wo = w['w_o_mla'][i].reshape(MLA_HEADS, D_V, D_MODEL)
    p['w_oa'] = jnp.pad(wo, ((0, 0), (0, HEAD_PAD - D_V), (0, 0))).reshape(MLA_PAD, D_MODEL)
    for n in ('w_glu', 'w_o_ssm', 'w_mem_kv', 'w_o_cross', 'w_out', 'w_up', 'w_down'):
        p[n] = w[n][i]
    p['conv_w'] = w['conv_w'][i]
    for n in ('norm_mix_g', 'q_a_norm_g', 'kv_a_norm_g', 'b_glu', 'mem_norm_g', 'xq_norm_g', 'xk_norm_g', 'b_gate',
              'norm_ffn_g', 'conv_b'):
        p[n] = w[n][i].reshape(1, -1)
    p['q_norm_g'] = jnp.pad(w['q_norm_g'][i], (0, HEAD_PAD - D_QK)).reshape(1, HEAD_PAD)
    p['k_norm_g'] = jnp.pad(w['k_norm_g'][i], (0, HEAD_PAD - D_QK)).reshape(1, HEAD_PAD)
    p['ssm_d'] = w['ssm_d'][i].reshape(1, SSM_WIDTH)
    p['lr'] = w['ssm_lambda_re'][i]
    p['li'] = w['ssm_lambda_im'][i]
    p['log_dt'] = w['ssm_log_dt'][i].reshape(SSM_GROUPS, 1)
    p['br'] = w['ssm_b_re'][i].transpose(2, 0, 1)
    p['bi'] = w['ssm_b_im'][i].transpose(2, 0, 1)
    cr = w['ssm_c_re'][i].reshape(SSM_JB, 8, SSM_GROUP_CH, SSM_STATE).transpose(0, 1, 3, 2)
    ci = w['ssm_c_im'][i].reshape(SSM_JB, 8, SSM_GROUP_CH, SSM_STATE).transpose(0, 1, 3, 2)
    p['c_mat'] = jnp.concatenate([_blockdiag(cr), -_blockdiag(ci)], axis=1).astype(BF16)
    return p


def _b_mat(bb_re, bb_im):
    r = bb_re.transpose(1, 0, 2).reshape(SSM_JB, 8, SSM_GROUP_CH, SSM_STATE)
    i = bb_im.transpose(1, 0, 2).reshape(SSM_JB, 8, SSM_GROUP_CH, SSM_STATE)
    return jnp.concatenate([_blockdiag(r), _blockdiag(i)], axis=2).astype(BF16)


def _qkv_fn(ps, c, s1, s2, qag, wqb, kvag, wk, wv, qng, kng):
    c_q = ps[:, :Q_LORA]
    c_kv = ps[:, Q_LORA:Q_LORA + KV_LORA]
    kr = ps[:, Q_LORA + KV_LORA:]
    cqn = _rms(c_q, qag, Q_LORA)
    ckvn = _rms(c_kv, kvag, KV_LORA)
    q_raw = _mm(cqn, wqb)
    k_raw = _mm(ckvn, wk) + jnp.concatenate([kr] * MLA_HEADS, axis=-1)
    v = _mm(ckvn, wv)
    q = _heads(_rope, _head_rms(q_raw, qng, MLA_HEADS, D_QK), MLA_HEADS, c, s1, s2)
    k = _heads(_rope, _head_rms(k_raw, kng, MLA_HEADS, D_QK), MLA_HEADS, c, s1, s2)
    lane = lax.broadcasted_iota(jnp.int32, v.shape, 1)
    v = jnp.where((lane & (LANE - 1)) == D_V, 1.0, v)
    return q * ATT_QSCALE, k, v


def _layer_fwd(name, x, tabs, mem, p, gather=None):
    l = x.shape[0]
    tm = min(512, l)
    nt = l // tm
    sv = {'x0': x}
    sv['p_g'] = _matmul(name + "_in_g", [(x, p['w_g'])], l, 3 * D_MODEL, rms_gain=p['norm_mix_g'])
    sv['p_u'] = _matmul(name + "_in_u", [(x, p['w_u'])], l, SSM_WIDTH, rms_gain=p['norm_mix_g'])
    sv['p_xq'] = _matmul(name + "_in_xq", [(x, p['w_xq'])], l, X_WIDTH, rms_gain=p['norm_mix_g'])
    sv['p_s'] = _matmul(name + "_in_s", [(x, p['w_s'])], l, SMALL_W, rms_gain=p['norm_mix_g'])

    qkv_consts = [p['q_a_norm_g'], p['w_qb'], p['kv_a_norm_g'], p['w_k'], p['w_v'], p['q_norm_g'], p['k_norm_g']]
    qkv_cspecs = [_full(a.shape) for a in qkv_consts]
    def qkv_fwd(i, *a):
        qv, kv, vv = _qkv_fn(*a)
        return qv, kv, vv, jnp.transpose(kv), jnp.transpose(vv)

    q, k, v, k_t, v_t = _rows(qkv_fwd, name=name + "_qkv", n=nt, ins=[sv['p_s'], *tabs, *qkv_consts],
                              in_specs=[_rt(tm, SMALL_W)] + [_rt(tm, LANE)] * 3 + qkv_cspecs,
                              outs=[_sds((l, MLA_PAD), BF16)] * 3 + [_sds((MLA_PAD, l), BF16)] * 2,
                              out_specs=[_rt(tm, MLA_PAD)] * 3 + [pl.BlockSpec((MLA_PAD, tm), lambda i: (0, i))] * 2)
    sv['q'], sv['k'], sv['v'], sv['k_t'] = q, k, v, k_t
    sv['o_a'], sv['lse_t'], sv['gathered'] = _flash_fwd(name + "_attn", q, k, v_t, gather=gather)

    a_re, a_im, bb_re, bb_im = _ssm_params(name + "_ssm_par", p['lr'], p['li'], p['log_dt'], p['br'], p['bi'])
    sv['a_re'], sv['a_im'] = a_re.reshape(1, SSM_LANES), a_im.reshape(1, SSM_LANES)
    sv['b_mat'] = _b_mat(bb_re, bb_im)
    u_p = _to_perm(sv['p_u'], l)
    sv['u_p'] = u_p

    def bu_fn(i, u, bm):
        res = [_mm(u[:, j * LANE:(j + 1) * LANE], bm[j]) for j in range(SSM_JB)]
        return (jnp.concatenate([r[:, :512] for r in res], axis=-1), jnp.concatenate([r[:, 512:] for r in res], axis=-1))

    ts = min(256, l)
    bu_re, bu_im = _rows(bu_fn, name=name + "_ssm_bu", n=l // ts, ins=[u_p, sv['b_mat']],
                         in_specs=[_rt(ts, SSM_WIDTH), _full(sv['b_mat'].shape)],
                         outs=[_sds((l, SSM_LANES), SSM_STATE_DTYPE)] * 2, out_specs=[_rt(ts, SSM_LANES)] * 2)
    s_re, s_im = _scan(name + "_ssm_scan", bu_re, bu_im, sv['a_re'], sv['a_im'], reverse=False)
    sv['s_re'], sv['s_im'] = s_re, s_im

    def glu_fn(i, sr, si, u, cm, dsk, wg, bg):
        y = jnp.concatenate([_mm(jnp.concatenate([sr[:, j * 512:(j + 1) * 512], si[:, j * 512:(j + 1) * 512]], axis=-1),
                                 cm[j]) for j in range(SSM_JB)], axis=-1) + dsk * u
        zz = _gelu(y)
        return zz * jax.nn.sigmoid(_mm(zz, wg) + bg)

    glu_consts = [p['c_mat'], p['ssm_d'], p['w_glu'], p['b_glu']]
    zo_p = _rows(glu_fn, name=name + "_ssm_glu", n=l // ts, ins=[s_re, s_im, u_p, *glu_consts],
                 in_specs=[_rt(ts, SSM_LANES), _rt(ts, SSM_LANES), _rt(ts, SSM_WIDTH)] + [_full(a.shape) for a in glu_consts],
                 outs=[_sds((l, SSM_WIDTH), BF16)], out_specs=[_rt(ts, SSM_WIDTH)])[0]
    sv['zo'] = _from_perm(zo_p, l)

    m_len = mem.shape[0]

    def memkv_fn(i, mm_, mg, wmk, xkg):
        kv = _mm(_rms(mm_, mg, D_MODEL), wmk)
        return _head_rms(kv[:, :X_WIDTH], xkg, X_HEADS, X_HEAD_DIM), kv[:, X_WIDTH:]

    mem_consts = [p['mem_norm_g'], p['w_mem_kv'], p['xk_norm_g']]
    k_c, v_c = _rows(memkv_fn, name=name + "_memkv", n=1, ins=[mem, *mem_consts],
                     in_specs=[_full(mem.shape)] + [_full(a.shape) for a in mem_consts],
                     outs=[_sds((m_len, X_WIDTH))] * 2, out_specs=[_full((m_len, X_WIDTH))] * 2)
    sv['k_c'], sv['v_c'] = k_c, v_c

    def cross_fn(i, xq, kc, vc, xqg):
        outs = []
        for h in range(X_HEADS):
            sl = slice(h * LANE, (h + 1) * LANE)
            qh = _rms(xq[:, sl], xqg, X_HEAD_DIM)
            s = _mm_nt(qh, kc[:, sl]) * (X_HEAD_DIM ** -0.5)
            s = s - jnp.max(s, axis=-1, keepdims=True)
            e = jnp.exp(s)
            pr = e / jnp.sum(e, axis=-1, keepdims=True)
            outs.append(_mm(pr, vc[:, sl]))
        return jnp.concatenate(outs, axis=-1)

    sv['o_c'] = _rows(cross_fn, name=name + "_cross", n=nt, ins=[sv['p_xq'], k_c, v_c, p['xq_norm_g']],
                      in_specs=[_rt(tm, X_WIDTH), _full(k_c.shape), _full(v_c.shape), _full((1, LANE))],
                      outs=[_sds((l, X_WIDTH), BF16)], out_specs=[_rt(tm, X_WIDTH)])[0]

    def merge_fn(i, oa, zo, oc, pg, x0, woa, wos, woc, bg, wout):
        gates = jax.nn.sigmoid(pg + bg)
        merged = (gates[:, :D_MODEL] * _mm(oa, woa) + gates[:, D_MODEL:2 * D_MODEL] * _mm(zo, wos)
                  + gates[:, 2 * D_MODEL:] * _mm(oc, woc))
        return x0 + _mm(merged, wout), merged

    merge_consts = [p['w_oa'], p['w_o_ssm'], p['w_o_cross'], p['b_gate'], p['w_out']]
    tg = min(256, l)
    x1, merged = _rows(merge_fn, name=name + "_merge", n=l // tg, ins=[sv['o_a'], sv['zo'], sv['o_c'], sv['p_g'], x, *merge_consts],
                       in_specs=[_rt(tg, MLA_PAD), _rt(tg, SSM_WIDTH), _rt(tg, X_WIDTH), _rt(tg, 3 * D_MODEL), _rt(tg, D_MODEL)]
                       + [_full(a.shape) for a in merge_consts],
                       outs=[_sds((l, D_MODEL)), _sds((l, D_MODEL), BF16)], out_specs=[_rt(tg, D_MODEL)] * 2)
    sv['x1'], sv['merged'] = x1, merged

    up = _matmul(name + "_up", [(x1, p['w_up'])], l, 2 * D_FF, rms_gain=p['norm_ffn_g'])
    sv['up'] = up
    tc = min(128, l)

    def conv_fn(i, upt, halo, cw, cb):
        upc = _conv(i, upt, halo, cw) + cb
        return _silu(upc[:, :D_FF]) * upc[:, D_FF:]

    act = _rows(conv_fn, name=name + "_conv", n=l // tc, ins=[up, up, p['conv_w'], p['conv_b']],
                in_specs=[_rt(tc, 2 * D_FF), _halo_prev(tc, 2 * D_FF), _full((3, 2 * D_FF)), _full((1, 2 * D_FF))],
                outs=[_sds((l, D_FF), BF16)], out_specs=[_rt(tc, D_FF)])[0]
    sv['act'] = act
    x2 = _matmul(name + "_down", [(act, p['w_down'])], l, D_MODEL, resid=x1)
    return x2, sv


def _halo_prev(tm, w):
    return pl.BlockSpec((8, w), lambda i: (jnp.maximum(i * (tm // 8) - 1, 0), 0))


def _halo_next(tm, w, n_tiles):
    last = n_tiles * (tm // 8) - 1
    return pl.BlockSpec((8, w), lambda i: (jnp.minimum((i + 1) * (tm // 8), last), 0))


def _conv(i, tile, halo, cw):
    halo = jnp.where(i > 0, halo, 0.0)
    ext = jnp.concatenate([halo, tile], axis=0)
    n = ext.shape[0]
    x1 = pltpu.roll(ext, 1, 0)[8:]
    x2 = pltpu.roll(ext, 2, 0)[8:]
    del n
    return cw[0:1] * x2 + cw[1:2] * x1 + cw[2:3] * tile


def _layer_bwd(name, dx2, sv, tabs, mem, p, reduce=None):
    l = dx2.shape[0]
    tm = min(512, l)
    nt = l // tm
    g = {}
    x1 = sv['x1']
    dact = _matmul(name + "_b_down", [(dx2, p['w_down'])], l, D_FF, nt=True)
    g['w_down'] = _matmul_tn(name + "_gw_down", sv['act'], dx2)
    tc = min(128, l)
    ntc = l // tc

    def conv_b(i, upt, up_prev, up_next, da, da_next, cw, cb):
        up_prev = jnp.where(i > 0, up_prev, 0.0)
        da_next = jnp.where(i < ntc - 1, da_next, 0.0)
        ext = jnp.concatenate([up_prev, upt, up_next], axis=0)
        x0 = ext[8:]
        xm1 = pltpu.roll(ext, 1, 0)[8:]
        xm2 = pltpu.roll(ext, 2, 0)[8:]
        upc = cw[0:1] * xm2 + cw[1:2] * xm1 + cw[2:3] * x0 + cb
        _, vjp = jax.vjp(lambda a, b: _silu(a) * b, upc[:, :D_FF], upc[:, D_FF:])
        dg, dv = vjp(jnp.concatenate([da, da_next], axis=0))
        dupc = jnp.concatenate([dg, dv], axis=-1)
        n = dupc.shape[0]
        dup = cw[2:3] * dupc[:tc] + cw[1:2] * pltpu.roll(dupc, n - 1, 0)[:tc] + cw[0:1] * pltpu.roll(dupc, n - 2, 0)[:tc]
        dt = dupc[:tc]
        dcw = _row_select([_colsum(dt * xm2[:tc]), _colsum(dt * xm1[:tc]), _colsum(dt * upt)], 8)
        return dup, dcw, _colsum(dt)

    dup, g_cw, g_cb = _rows(
        conv_b, name=name + "_b_conv", n=ntc, ins=[sv['up'], sv['up'], sv['up'], dact, dact, p['conv_w'], p['conv_b']],
        in_specs=[_rt(tc, 2 * D_FF), _halo_prev(tc, 2 * D_FF), _halo_next(tc, 2 * D_FF, ntc), _rt(tc, D_FF),
                  _halo_next(tc, D_FF, ntc), _full((3, 2 * D_FF)), _full((1, 2 * D_FF))],
        outs=[_sds((l, 2 * D_FF)), _sds((8, 2 * D_FF)), _sds((1, 2 * D_FF))],
        out_specs=[_rt(tc, 2 * D_FF), _full((8, 2 * D_FF)), _full((1, 2 * D_FF))], n_acc=2, vmem=56)
    g['conv_w'] = g_cw[:3]
    g['conv_b'] = g_cb
    dh2 = _matmul(name + "_b_up", [(dup, p['w_up'])], l, D_MODEL, nt=True, tm=256)
    g['w_up'] = _matmul_tn(name + "_gw_up", x1, dup, rms_gain=p['norm_ffn_g'])

    def norm_b(i, xv, dh, dres, gn):
        _, vjp = jax.vjp(lambda a, b: _rms(a, b, D_MODEL), xv, gn)
        dxv, dgn = vjp(dh)
        return dres + dxv, dgn

    dx1, g['norm_ffn_g'] = _rows(norm_b, name=name + "_b_norm2", n=nt, ins=[x1, dh2, dx2, p['norm_ffn_g']],
                                 in_specs=[_rt(tm, D_MODEL)] * 3 + [_full((1, D_MODEL))],
                                 outs=[_sds((l, D_MODEL)), _sds((1, D_MODEL))], out_specs=[_rt(tm, D_MODEL), _full((1, D_MODEL))],
                                 n_acc=1)

    tg = min(256, l)

    def merge_b(i, dx, oa, zo, oc, pg, woa, wos, woc, bg, wout):
        dm = _mm_nt(dx, wout)
        gates = jax.nn.sigmoid(pg + bg)
        ys = [_mm(oa, woa), _mm(zo, wos), _mm(oc, woc)]
        dys, dpg = [], []
        for b in range(3):
            gb = gates[:, b * D_MODEL:(b + 1) * D_MODEL]
            dys.append(dm * gb)
            dpg.append(dm * ys[b] * gb * (1.0 - gb))
        dpg = jnp.concatenate(dpg, axis=-1)
        return (_mm_nt(dys[0], woa), _mm_nt(dys[1], wos), _mm_nt(dys[2], woc), dpg, dys[0], dys[1], dys[2], _colsum(dpg))

    merge_consts = [p['w_oa'], p['w_o_ssm'], p['w_o_cross'], p['b_gate'], p['w_out']]
    (do_a, dzo, do_c, dp_g, dy_a, dy_b, dy_c, g['b_gate']) = _rows(
        merge_b, name=name + "_b_merge", n=l // tg, ins=[dx1, sv['o_a'], sv['zo'], sv['o_c'], sv['p_g'], *merge_consts],
        in_specs=[_rt(tg, D_MODEL), _rt(tg, MLA_PAD), _rt(tg, SSM_WIDTH), _rt(tg, X_WIDTH), _rt(tg, 3 * D_MODEL)]
        + [_full(a.shape) for a in merge_consts],
        outs=[_sds((l, MLA_PAD)), _sds((l, SSM_WIDTH)), _sds((l, X_WIDTH)), _sds((l, 3 * D_MODEL)),
              _sds((l, D_MODEL), BF16), _sds((l, D_MODEL), BF16), _sds((l, D_MODEL), BF16), _sds((1, 3 * D_MODEL))],
        out_specs=[_rt(tg, MLA_PAD), _rt(tg, SSM_WIDTH), _rt(tg, X_WIDTH), _rt(tg, 3 * D_MODEL),
                   _rt(tg, D_MODEL), _rt(tg, D_MODEL), _rt(tg, D_MODEL), _full((1, 3 * D_MODEL))], n_acc=1, vmem=56)
    g['w_out'] = _matmul_tn(name + "_gw_out", sv['merged'], dx1)
    g['w_oa'] = _matmul_tn(name + "_gw_oa", sv['o_a'], dy_a)
    g['w_o_ssm'] = _matmul_tn(name + "_gw_os", sv['zo'], dy_b)
    g['w_o_cross'] = _matmul_tn(name + "_gw_oc", sv['o_c'], dy_c)

    k_c, v_c = sv['k_c'], sv['v_c']
    m_len = k_c.shape[0]

    def cross_b(i, xq, do, kc, vc, xqg):
        dxq, dk, dv = [], [], []
        dg = jnp.zeros((1, LANE), F32)
        for h in range(X_HEADS):
            sl = slice(h * LANE, (h + 1) * LANE)
            qh, vjp = jax.vjp(lambda a, b: _rms(a, b, X_HEAD_DIM), xq[:, sl], xqg)
            sc = X_HEAD_DIM ** -0.5
            s = _mm_nt(qh, kc[:, sl]) * sc
            s = s - jnp.max(s, axis=-1, keepdims=True)
            e = jnp.exp(s)
            pr = e / jnp.sum(e, axis=-1, keepdims=True)
            doh = do[:, sl]
            dv.append(_mm_tn(pr, doh))
            dp = _mm_nt(doh, vc[:, sl])
            ds = pr * (dp - jnp.sum(dp * pr, axis=-1, keepdims=True)) * sc
            dk.append(_mm_tn(ds, qh))
            dxh, dgh = vjp(_mm(ds, kc[:, sl]))
            dxq.append(dxh)
            dg = dg + dgh
        return jnp.concatenate(dxq, axis=-1), jnp.concatenate(dk, axis=-1), jnp.concatenate(dv, axis=-1), dg

    dp_xq, dk_c, dv_c, g['xq_norm_g'] = _rows(
        cross_b, name=name + "_b_cross", n=nt, ins=[sv['p_xq'], do_c, k_c, v_c, p['xq_norm_g']],
        in_specs=[_rt(tm, X_WIDTH), _rt(tm, X_WIDTH), _full(k_c.shape), _full(v_c.shape), _full((1, LANE))],
        outs=[_sds((l, X_WIDTH)), _sds((m_len, X_WIDTH)), _sds((m_len, X_WIDTH)), _sds((1, LANE))],
        out_specs=[_rt(tm, X_WIDTH), _full((m_len, X_WIDTH)), _full((m_len, X_WIDTH)), _full((1, LANE))], n_acc=3)

    def memkv_b(i, mm_, dk, dv, mg, wmk, xkg):
        memn, vjp_n = jax.vjp(lambda a, b: _rms(a, b, D_MODEL), mm_, mg)
        kv = _mm(memn, wmk)
        _, vjp_k = jax.vjp(lambda a, b: _head_rms(a, b, X_HEADS, X_HEAD_DIM), kv[:, :X_WIDTH], xkg)
        dkr, dxkg = vjp_k(dk)
        dkv = jnp.concatenate([dkr, dv], axis=-1)
        _, dmg = vjp_n(_mm_nt(dkv, wmk))
        return _mm_tn(memn, dkv), dmg, dxkg

    mem_consts = [p['mem_norm_g'], p['w_mem_kv'], p['xk_norm_g']]
    g['w_mem_kv'], g['mem_norm_g'], g['xk_norm_g'] = _rows(
        memkv_b, name=name + "_b_memkv", n=1, ins=[mem, dk_c, dv_c, *mem_consts],
        in_specs=[_full(mem.shape), _full(dk_c.shape), _full(dv_c.shape)] + [_full(a.shape) for a in mem_consts],
        outs=[_sds((D_MODEL, 2 * X_WIDTH)), _sds((1, D_MODEL)), _sds((1, LANE))],
        out_specs=[_full((D_MODEL, 2 * X_WIDTH)), _full((1, D_MODEL)), _full((1, LANE))])

    u_p = sv['u_p']
    dzo_p = _to_perm(dzo, l)
    s_re, s_im = sv['s_re'], sv['s_im']

    def glu_b(i, sr, si, u, dz, cm, dsk, wg, bg):
        cats = [jnp.concatenate([sr[:, j * 512:(j + 1) * 512], si[:, j * 512:(j + 1) * 512]], axis=-1) for j in range(SSM_JB)]
        y = jnp.concatenate([_mm(cats[j], cm[j]) for j in range(SSM_JB)], axis=-1) + dsk * u
        zz, vjp_g = jax.vjp(_gelu, y)
        t = _mm(zz, wg) + bg
        sg = jax.nn.sigmoid(t)
        dt = dz * zz * sg * (1.0 - sg)
        dzz = dz * sg + _mm_nt(dt, wg)
        dy = vjp_g(dzz)[0]
        dss = [_mm_nt(dy[:, j * LANE:(j + 1) * LANE], cm[j]) for j in range(SSM_JB)]
        dsr = jnp.concatenate([d[:, :512] for d in dss], axis=-1)
        dsi = jnp.concatenate([d[:, 512:] for d in dss], axis=-1)
        dcm = jnp.stack([_mm_tn(cats[j], dy[:, j * LANE:(j + 1) * LANE]) for j in range(SSM_JB)], axis=0)
        return dsr, dsi, dy * dsk, dcm, _colsum(dy * u), _mm_tn(zz, dt), _colsum(dt)

    glu_consts = [p['c_mat'], p['ssm_d'], p['w_glu'], p['b_glu']]
    ts = min(256, l)
    nts = l // ts
    ds_re, ds_im, du_dir, g['c_mat'], g['ssm_d'], g['w_glu'], g['b_glu'] = _rows(
        glu_b, name=name + "_b_glu", n=nts, ins=[s_re, s_im, u_p, dzo_p, *glu_consts],
        in_specs=[_rt(ts, SSM_LANES), _rt(ts, SSM_LANES), _rt(ts, SSM_WIDTH), _rt(ts, SSM_WIDTH)] + [_full(a.shape) for a in glu_consts],
        outs=[_sds((l, SSM_LANES), SSM_STATE_DTYPE), _sds((l, SSM_LANES), SSM_STATE_DTYPE), _sds((l, SSM_WIDTH)),
              _sds((SSM_JB, 1024, LANE)), _sds((1, SSM_WIDTH)),
              _sds((SSM_WIDTH, SSM_WIDTH)), _sds((1, SSM_WIDTH))],
        out_specs=[_rt(ts, SSM_LANES), _rt(ts, SSM_LANES), _rt(ts, SSM_WIDTH), _full((SSM_JB, 1024, LANE)), _full((1, SSM_WIDTH)),
                   _full((SSM_WIDTH, SSM_WIDTH)), _full((1, SSM_WIDTH))], n_acc=4)
    gb_re, gb_im = _scan(name + "_b_scan", ds_re, ds_im, sv['a_re'], -sv['a_im'], reverse=True)
    ns = SCAN_SEGS
    last_blk = l // ns - 1

    def da_fn(i, *vals):
        gr, gi, sr, si, hr, hi, lr_, li_ = [v.astype(F32) for v in vals]
        rid = lax.broadcasted_iota(jnp.int32, lr_.shape, 0)
        fr = jnp.where(rid == 0, 0.0, pltpu.roll(lr_, 1, 0))
        fi = jnp.where(rid == 0, 0.0, pltpu.roll(li_, 1, 0))
        hr = jnp.where(i == 0, fr, hr)
        hi = jnp.where(i == 0, fi, hi)
        if ts > ns:
            pr = jnp.concatenate([hr, sr[:ts - ns]], axis=0)
            pi = jnp.concatenate([hi, si[:ts - ns]], axis=0)
        else:
            pr, pi = hr, hi
        return _colsum(gr * pr + gi * pi), _colsum(gi * pr - gr * pi)

    hprev = pl.BlockSpec((ns, SSM_LANES), lambda i: (jnp.maximum(i * (ts // ns) - 1, 0), 0))
    hlast = pl.BlockSpec((ns, SSM_LANES), lambda i: (last_blk, 0))
    da_re, da_im = _rows(da_fn, name=name + "_b_da", n=nts, ins=[gb_re, gb_im, s_re, s_im, s_re, s_im, s_re, s_im],
                         in_specs=[_rt(ts, SSM_LANES)] * 4 + [hprev, hprev, hlast, hlast],
                         outs=[_sds((1, SSM_LANES))] * 2, out_specs=[_full((1, SSM_LANES))] * 2, n_acc=2)

    def bu_b(i, dbr, dbi, u, dud, bm):
        dus, dbm = [], []
        for j in range(SSM_JB):
            cat = jnp.concatenate([dbr[:, j * 512:(j + 1) * 512], dbi[:, j * 512:(j + 1) * 512]], axis=-1)
            dus.append(_mm_nt(cat, bm[j]))
            dbm.append(_mm_tn(u[:, j * LANE:(j + 1) * LANE], cat))
        return dud + jnp.concatenate(dus, axis=-1), jnp.stack(dbm, axis=0)

    du_p, d_bmat = _rows(bu_b, name=name + "_b_bu", n=nts, ins=[gb_re, gb_im, u_p, du_dir, sv['b_mat']],
                         in_specs=[_rt(ts, SSM_LANES), _rt(ts, SSM_LANES), _rt(ts, SSM_WIDTH), _rt(ts, SSM_WIDTH),
                                   _full(sv['b_mat'].shape)],
                         outs=[_sds((l, SSM_WIDTH)), _sds((SSM_JB, LANE, 1024))],
                         out_specs=[_rt(ts, SSM_WIDTH), _full((SSM_JB, LANE, 1024))], n_acc=1)
    dp_u = _from_perm(du_p, l)
    dbb_re = _blockdiag_t(d_bmat[:, :, :512], SSM_GROUP_CH, SSM_STATE).reshape(SSM_GROUPS, SSM_GROUP_CH, SSM_STATE).transpose(1, 0, 2)
    dbb_im = _blockdiag_t(d_bmat[:, :, 512:], SSM_GROUP_CH, SSM_STATE).reshape(SSM_GROUPS, SSM_GROUP_CH, SSM_STATE).transpose(1, 0, 2)
    g['lr'], g['li'], g['log_dt'], g['br'], g['bi'] = _ssm_params_bwd(
        name + "_b_ssm_par", p['lr'], p['li'], p['log_dt'], p['br'], p['bi'],
        da_re.reshape(SSM_GROUPS, SSM_STATE), da_im.reshape(SSM_GROUPS, SSM_STATE), dbb_re, dbb_im)

    dq_t, dk, dv, rode = _flash_bwd(name + "_b_attn", sv['q'], sv['k'], sv['v'], sv['k_t'], sv['o_a'], sv['lse_t'], do_a,
                                    reduce=reduce)

    def qkv_b(i, ps, c, s1, s2, dq_, dk_, dv_, qag, wqb, kvag, wk, wv, qng, kng):
        c_q = ps[:, :Q_LORA]
        c_kv = ps[:, Q_LORA:Q_LORA + KV_LORA]
        kr = ps[:, Q_LORA + KV_LORA:]
        cqn, vjp_cq = jax.vjp(lambda a, b: _rms(a, b, Q_LORA), c_q, qag)
        ckvn, vjp_ckv = jax.vjp(lambda a, b: _rms(a, b, KV_LORA), c_kv, kvag)
        q_raw = _mm(cqn, wqb)
        k_raw = _mm(ckvn, wk) + jnp.concatenate([kr] * MLA_HEADS, axis=-1)
        _, vjp_qn = jax.vjp(lambda a, b: _head_rms(a, b, MLA_HEADS, D_QK), q_raw, qng)
        _, vjp_kn = jax.vjp(lambda a, b: _head_rms(a, b, MLA_HEADS, D_QK), k_raw, kng)
        dq_raw, dqng = vjp_qn(_heads(_rope_t, jnp.transpose(dq_[0]), MLA_HEADS, c, s1, s2))
        dk_raw, dkng = vjp_kn(_heads(_rope_t, dk_, MLA_HEADS, c, s1, s2))
        dkr = dk_raw[:, :LANE]
        for h in range(1, MLA_HEADS):
            dkr = dkr + dk_raw[:, h * LANE:(h + 1) * LANE]
        dcq, dqag = vjp_cq(_mm_nt(dq_raw, wqb))
        dckv, dkvag = vjp_ckv(_mm_nt(dk_raw, wk) + _mm_nt(dv_, wv))
        dps = jnp.concatenate([dcq, dckv, dkr], axis=-1)
        return (dps, _mm_tn(cqn, dq_raw), _mm_tn(ckvn, dk_raw), _mm_tn(ckvn, dv_), dqag, dkvag, dqng, dkng)

    qkv_consts = [p['q_a_norm_g'], p['w_qb'], p['kv_a_norm_g'], p['w_k'], p['w_v'], p['q_norm_g'], p['k_norm_g']]
    (dp_s, g['w_qb'], g['w_k'], g['w_v'], g['q_a_norm_g'], g['kv_a_norm_g'], g['q_norm_g'], g['k_norm_g']) = _rows(
        qkv_b, name=name + "_b_qkv", n=nt, ins=[sv['p_s'], *tabs, dq_t, dk, dv, *qkv_consts],
        in_specs=[_rt(tm, SMALL_W)] + [_rt(tm, LANE)] * 3
        + [pl.BlockSpec((1, MLA_PAD, tm), lambda i: (i // (dq_t.shape[2] // tm), 0, i % (dq_t.shape[2] // tm)))]
        + [_rt(tm, MLA_PAD)] * 2 + [_full(a.shape) for a in qkv_consts],
        outs=[_sds((l, SMALL_W)), _sds((Q_LORA, MLA_PAD)), _sds((KV_LORA, MLA_PAD)), _sds((KV_LORA, MLA_PAD)),
              _sds((1, Q_LORA)), _sds((1, KV_LORA)), _sds((1, LANE)), _sds((1, LANE))],
        out_specs=[_rt(tm, SMALL_W), _full((Q_LORA, MLA_PAD)), _full((KV_LORA, MLA_PAD)), _full((KV_LORA, MLA_PAD)),
                   _full((1, Q_LORA)), _full((1, KV_LORA)), _full((1, LANE)), _full((1, LANE))], n_acc=7)

    x0 = sv['x0']
    dh = _matmul(name + "_b_in", [(dp_g, p['w_g']), (dp_u, p['w_u']), (dp_xq, p['w_xq']), (dp_s, p['w_s'])], l, D_MODEL, nt=True,
                 tm=256)
    gm = p['norm_mix_g']
    g['w_g'] = _matmul_tn(name + "_gw_g", x0, dp_g, rms_gain=gm)
    g['w_u'] = _matmul_tn(name + "_gw_u", x0, dp_u, rms_gain=gm)
    g['w_xq'] = _matmul_tn(name + "_gw_xq", x0, dp_xq, rms_gain=gm)
    g['w_s'] = _matmul_tn(name + "_gw_s", x0, dp_s, rms_gain=gm)
    dx0, g['norm_mix_g'] = _rows(norm_b, name=name + "_b_norm1", n=nt, ins=[x0, dh, dx1, gm],
                                 in_specs=[_rt(tm, D_MODEL)] * 3 + [_full((1, D_MODEL))],
                                 outs=[_sds((l, D_MODEL)), _sds((1, D_MODEL))], out_specs=[_rt(tm, D_MODEL), _full((1, D_MODEL))],
                                 n_acc=1)
    return dx0, g, rode


def _unprep_grads(g):
    o = {}
    ws = g['w_s']
    o['w_in'] = jnp.concatenate([ws[:, :Q_LORA + KV_LORA], ws[:, Q_LORA + KV_LORA + D_NOPE:Q_LORA + KV_LORA + D_QK],
                                 g['w_u'], g['w_xq'], g['w_g']], axis=1)
    o['w_q_b'] = g['w_qb'].reshape(Q_LORA, MLA_HEADS, HEAD_PAD)[:, :, :D_QK].reshape(Q_LORA, MLA_HEADS * D_QK)
    gk = g['w_k'].reshape(KV_LORA, MLA_HEADS, HEAD_PAD)[:, :, :D_NOPE]
    gv = g['w_v'].reshape(KV_LORA, MLA_HEADS, HEAD_PAD)[:, :, :D_V]
    o['w_kv_b'] = jnp.concatenate([gk, gv], axis=2).reshape(KV_LORA, MLA_HEADS * (D_NOPE + D_V))
    o['w_o_mla'] = g['w_oa'].reshape(MLA_HEADS, HEAD_PAD, D_MODEL)[:, :D_V].reshape(MLA_HEADS * D_V, D_MODEL)
    for n in ('w_glu', 'w_o_ssm', 'w_mem_kv', 'w_o_cross', 'w_out', 'w_up', 'w_down', 'conv_w'):
        o[n] = g[n]
    for n in ('norm_mix_g', 'q_a_norm_g', 'kv_a_norm_g', 'b_glu', 'mem_norm_g', 'xq_norm_g', 'xk_norm_g', 'b_gate',
              'norm_ffn_g', 'conv_b'):
        o[n] = g[n].reshape(-1)
    o['q_norm_g'] = g['q_norm_g'].reshape(-1)[:D_QK]
    o['k_norm_g'] = g['k_norm_g'].reshape(-1)[:D_QK]
    o['ssm_d'] = g['ssm_d'].reshape(SSM_GROUPS, SSM_GROUP_CH)
    o['ssm_lambda_re'] = g['lr']
    o['ssm_lambda_im'] = g['li']
    o['ssm_log_dt'] = g['log_dt'].reshape(SSM_GROUPS)
    o['ssm_b_re'] = g['br'].transpose(1, 2, 0)
    o['ssm_b_im'] = g['bi'].transpose(1, 2, 0)
    dc = g['c_mat']
    o['ssm_c_re'] = _blockdiag_t(dc[:, :512], SSM_STATE, SSM_GROUP_CH).transpose(0, 1, 3, 2).reshape(SSM_GROUPS, SSM_GROUP_CH, SSM_STATE)
    o['ssm_c_im'] = -_blockdiag_t(dc[:, 512:], SSM_STATE, SSM_GROUP_CH).transpose(0, 1, 3, 2).reshape(SSM_GROUPS, SSM_GROUP_CH, SSM_STATE)
    return o


def _local_step(x, mem, pos, target, w, late_gather=None, early_reduce=None):
    l = x.shape[0]
    tm = min(512, l)
    tabs = _rope_tables(pos.astype(F32).reshape(l, 1))
    saved = []
    ps = []
    h = x
    for i in range(DEPTH):
        ps.append(_prep_layer(w, i))
        riding = late_gather[:2] if (late_gather is not None and i == 0) else None
        h, sv = _layer_fwd("l%d" % i, h, tabs, mem, ps[i], gather=riding)
        if riding is not None:
            for n, v in late_gather[2](sv.pop('gathered')).items():
                w[n][late_gather[1]] = v
        saved.append(sv)

    def loss_fn(i, y, t):
        e = y - t
        per_tok = jnp.sum(e * e, axis=-1, keepdims=True) * (1.0 / D_MODEL)
        tot = 0.5 * jnp.sum(per_tok, axis=0, keepdims=True)
        return e * (1.0 / D_MODEL), jnp.broadcast_to(tot, (1, LANE))

    dy, loss = _rows(loss_fn, name="loss", n=l // tm, ins=[h, target], in_specs=[_rt(tm, D_MODEL)] * 2,
                     outs=[_sds((l, D_MODEL)), _sds((1, LANE))], out_specs=[_rt(tm, D_MODEL), _full((1, LANE))], n_acc=1)
    grads = []
    d = dy
    riding, rode = None, []
    for i in reversed(range(DEPTH)):
        d, g, got = _layer_bwd("l%d" % i, d, saved[i], tabs, mem, ps[i], reduce=riding)
        rode = got or rode
        grads.append(_unprep_grads(g))
        riding = (early_reduce(grads[-1]), DEPTH - 1) if (early_reduce is not None and i == DEPTH - 1) else None
    return loss[0, 0], d, grads[::-1], rode


def _sum_picked(name, slots, pick, extra, out_dtype):
    _, r, c = slots.shape
    e = extra.shape[0]
    tr = _row_tile(r)

    def body(pk, s_ref, x_ref, o_ref):
        acc = s_ref[...].astype(F32)
        for k in range(e):
            acc = acc + x_ref[k].astype(F32)
        o_ref[...] = acc.astype(o_ref.dtype)

    grid_spec = pltpu.PrefetchScalarGridSpec(
        num_scalar_prefetch=1, grid=(r // tr,),
        in_specs=[pl.BlockSpec((None, tr, c), lambda i, pk: (pk[0], i, 0)), pl.BlockSpec((e, tr, c), lambda i, pk: (0, i, 0))],
        out_specs=pl.BlockSpec((tr, c), lambda i, pk: (i, 0)))
    return pl.pallas_call(body, name=name, grid_spec=grid_spec, out_shape=_sds((r, c), out_dtype),
                          compiler_params=_params(("arbitrary",), 48))(pick, slots, extra)


def _row_tile(r):
    for t in (256, 128, 64, 32, 16, 8):
        if r % t == 0:
            return t
    return r


def _adamw(name, parts, w, m, v):
    r, cw = w.shape
    tr = _row_tile(r)
    np_ = len(parts)

    def fn(i, *vals):
        wv, mv, vv = vals[np_:]
        terms = []
        for pv in vals[:np_]:
            terms += [pv] if pv.ndim == 2 else [pv[k] for k in range(pv.shape[0])]
        g = terms[0]
        for t in terms[1:]:
            g = g + t
        mn = ADAM_B1 * mv + (1.0 - ADAM_B1) * g
        vn = ADAM_B2 * vv + (1.0 - ADAM_B2) * (g * g)
        m_hat = mn / (1.0 - ADAM_B1 ** ADAM_STEP)
        v_hat = vn / (1.0 - ADAM_B2 ** ADAM_STEP)
        delta = -ADAM_LR * (m_hat / (jnp.sqrt(v_hat) + ADAM_EPS) + ADAM_WD * wv)
        return g, delta, mn, vn

    pspecs = [_rt(tr, cw) if p.ndim == 2 else pl.BlockSpec((p.shape[0], tr, cw), lambda i: (0, i, 0)) for p in parts]
    return _rows(fn, name=name, n=r // tr, ins=[*parts, w, m, v], in_specs=pspecs + [_rt(tr, cw)] * 3,
                 outs=[_sds((r, cw))] * 4, out_specs=[_rt(tr, cw)] * 4)


def _shard_of(a, axis, k):
    n = a.shape[axis] // 4
    return lax.slice_in_dim(a, k * n, (k + 1) * n, axis=axis)


def _step(a):
    x = a['x'][0]
    mem = a['mem'][0]
    pos = a['positions'][0]
    target = a['loss_target'][0]

    me = 2 * lax.axis_index("x") + lax.axis_index("y")

    mine = [a[n] if n == 'conv_w' else a[n].astype(BF16) for n in SHARDED]

    def assemble(bufs, layer):
        return {n: jnp.concatenate([jnp.where(me == k, own[layer], y[k]) for k in range(4)], axis=SHARD_AXIS[n] - 1)
                for n, own, y in zip(SHARDED, mine, bufs)}

    w = {n: [v, None] for n, v in assemble(_layer_gather("comm_gather_l0", mine, 0), 0).items()}
    for n in SMALL:
        w[n] = a[n]

    mc = lax.axis_index("c")
    me1 = me.astype(jnp.int32).reshape(1)
    zero1i = jnp.zeros((1,), jnp.int32)

    def pair_sums(layer, gl):
        mine_l = [jnp.stack([_shard_of(gl[n], SHARD_AXIS[n] - 1, k) for k in range(4)], axis=0).astype(BF16) for n in SHARDED]
        per_core = [None, None]
        per_core[1 - layer] = mine_l
        theirs = _send_d2d("comm_pair_l%d" % layer, per_core)
        out = []
        for n, g, s in zip(SHARDED, mine_l, theirs):
            cols = g.shape[-1]
            out.append(_sum_picked("sum2_l%d_%s" % (layer, n), g.reshape(1, -1, cols), zero1i, s.reshape(1, -1, cols), BF16)
                       .reshape(g.shape))
        return out

    pairs = [None, None]

    def early_reduce(gl):
        pairs[1] = pair_sums(1, gl)
        return pairs[1]

    loss, grad_x, grads, got1 = _local_step(x, mem, pos, target, w, late_gather=(mine, 1, lambda bufs: assemble(bufs, 1)),
                                            early_reduce=early_reduce)
    pairs[0] = pair_sums(0, grads[0])
    gsm = _pack([jnp.stack([grads[i][n] for i in range(DEPTH)], axis=0) for n in SMALL] + [loss.reshape(1)], 8, F32)
    got0, alls = _layer_reduce("comm_reduce_l0", pairs[0], 0, gsm)
    done = [[_sum_picked("sum4_l%d_%s" % (layer, n), p4, me1, g3, F32) for n, p4, g3 in zip(SHARDED, pairs[layer], got)]
            for layer, got in ((0, got0), (1, got1))]
    others = _send_d2d("comm_reduce_d2d", done)
    res_sh = []
    for n, d0, d1, other in zip(SHARDED, done[0], done[1], others):
        cols = d0.shape[-1]
        full = jnp.where(mc == 0, jnp.stack([d0, other], axis=0), jnp.stack([other, d1], axis=0))
        res = _adamw("adamw_" + n, [full.reshape(-1, cols)], *[a[pre + n].reshape(-1, cols) for pre in ('', 'm_', 'v_')])
        res_sh.append([r.reshape(a[n].shape) for r in res])
    res_sh = [[res_sh[j][kind] for j in range(len(SHARDED))] for kind in range(4)]

    sm_shapes = [a[n].shape for n in SMALL] + [(1,)]
    zero1 = jnp.zeros((1,), F32)
    res_sm = _adamw("adamw_small", [alls], *[_pack([a[pre + n] for n in SMALL] + [zero1], 8, F32) for pre in ('', 'm_', 'v_')])
    res_sm = [_unpack(r, sm_shapes) for r in res_sm]
    loss = res_sm[0][-1][0]

    outs = [loss, grad_x[None]]
    for kind in range(4):
        byname = dict(zip(SHARDED, res_sh[kind]))
        byname.update(zip(SMALL, res_sm[kind]))
        outs += [byname[n] for n in WEIGHTS]
    return tuple(outs)


def kernel(x, mem, positions, norm_mix_g, w_in, q_a_norm_g, w_q_b, kv_a_norm_g, w_kv_b, q_norm_g, k_norm_g, w_o_mla, ssm_lambda_re, ssm_lambda_im, ssm_log_dt, ssm_b_re, ssm_b_im, ssm_c_re, ssm_c_im, ssm_d, w_glu, b_glu, w_o_ssm, mem_norm_g, w_mem_kv, xq_norm_g, xk_norm_g, w_o_cross, b_gate, w_out, norm_ffn_g, w_up, conv_w, conv_b, w_down, loss_target, m_norm_mix_g, m_w_in, m_q_a_norm_g, m_w_q_b, m_kv_a_norm_g, m_w_kv_b, m_q_norm_g, m_k_norm_g, m_w_o_mla, m_ssm_lambda_re, m_ssm_lambda_im, m_ssm_log_dt, m_ssm_b_re, m_ssm_b_im, m_ssm_c_re, m_ssm_c_im, m_ssm_d, m_w_glu, m_b_glu, m_w_o_ssm, m_mem_norm_g, m_w_mem_kv, m_xq_norm_g, m_xk_norm_g, m_w_o_cross, m_b_gate, m_w_out, m_norm_ffn_g, m_w_up, m_conv_w, m_conv_b, m_w_down, v_norm_mix_g, v_w_in, v_q_a_norm_g, v_w_q_b, v_kv_a_norm_g, v_w_kv_b, v_q_norm_g, v_k_norm_g, v_w_o_mla, v_ssm_lambda_re, v_ssm_lambda_im, v_ssm_log_dt, v_ssm_b_re, v_ssm_b_im, v_ssm_c_re, v_ssm_c_im, v_ssm_d, v_w_glu, v_b_glu, v_w_o_ssm, v_mem_norm_g, v_w_mem_kv, v_xq_norm_g, v_xk_norm_g, v_w_o_cross, v_b_gate, v_w_out, v_norm_ffn_g, v_w_up, v_conv_w, v_conv_b, v_w_down):
    return _step(dict(locals()))
```

```python
import math

import numpy as np
import jax
import jax.numpy as jnp
from jax import lax
from jax.experimental import pallas as pl
from jax.experimental.pallas import tpu as pltpu

F32 = jnp.float32
BF16 = jnp.bfloat16
MESH = pl.DeviceIdType.MESH

DEPTH = 2
D_MODEL = 1024
EPS = 1e-6
MLA_HEADS = 8
Q_LORA = 384
KV_LORA = 256
D_NOPE = 64
D_ROPE = 32
D_QK = D_NOPE + D_ROPE
D_V = 64
HEAD_PAD = 128
MLA_PAD = MLA_HEADS * HEAD_PAD
ROPE_THETA = 10000.0
SSM_GROUPS = 32
SSM_GROUP_CH = 16
SSM_WIDTH = 512
SSM_STATE = 64
SSM_LANES = SSM_GROUPS * SSM_STATE
SSM_JB = 4
X_HEADS = 4
X_HEAD_DIM = 128
X_WIDTH = 512
D_FF = 2816
SMALL_W = Q_LORA + KV_LORA + HEAD_PAD
SCAN_SEGS = 64
SSM_STATE_DTYPE = BF16
LANE = 128
NEG = -1e30

ADAM_LR = 0.001
ADAM_B1 = 0.9
ADAM_B2 = 0.999
ADAM_EPS = 1e-08
ADAM_WD = 0.01
ADAM_STEP = 10

WEIGHTS = ['norm_mix_g', 'w_in', 'q_a_norm_g', 'w_q_b', 'kv_a_norm_g', 'w_kv_b', 'q_norm_g', 'k_norm_g', 'w_o_mla',
           'ssm_lambda_re', 'ssm_lambda_im', 'ssm_log_dt', 'ssm_b_re', 'ssm_b_im', 'ssm_c_re', 'ssm_c_im', 'ssm_d',
           'w_glu', 'b_glu', 'w_o_ssm', 'mem_norm_g', 'w_mem_kv', 'xq_norm_g', 'xk_norm_g', 'w_o_cross', 'b_gate',
           'w_out', 'norm_ffn_g', 'w_up', 'conv_w', 'conv_b', 'w_down']
SHARD_AXIS = {'w_in': 2, 'w_q_b': 2, 'w_kv_b': 2, 'w_o_mla': 2, 'w_glu': 1, 'w_o_ssm': 2, 'w_mem_kv': 1,
              'w_o_cross': 2, 'w_out': 1, 'w_up': 2, 'conv_w': 2, 'w_down': 1}
SHARDED = [n for n in WEIGHTS if n in SHARD_AXIS]
SMALL = [n for n in WEIGHTS if n not in SHARD_AXIS]


def _bf(v):
    return v.astype(BF16)


def _mm(a, b):
    return jnp.dot(_bf(a), _bf(b), preferred_element_type=F32)


def _mm_nt(a, b):
    return lax.dot_general(_bf(a), _bf(b), (((1,), (1,)), ((), ())), preferred_element_type=F32)


def _mm_tn(a, b):
    return lax.dot_general(_bf(a), _bf(b), (((0,), (0,)), ((), ())), preferred_element_type=F32)


def _rms(v, g, n):
    ms = jnp.sum(v * v, axis=-1, keepdims=True) * (1.0 / n)
    return (v * lax.rsqrt(ms + EPS)) * g


def _head_rms(v, g, heads, n):
    return jnp.concatenate([_rms(v[:, h * LANE:(h + 1) * LANE], g, n) for h in range(heads)], axis=-1)


def _rope(v, c, s1, s2):
    return v * c + pltpu.roll(v, LANE - 16, 1) * s1 + pltpu.roll(v, 16, 1) * s2


def _rope_t(g, c, s1, s2):
    return g * c + pltpu.roll(g * s1, 16, 1) + pltpu.roll(g * s2, LANE - 16, 1)


def _heads(fn, v, heads, *tabs):
    return jnp.concatenate([fn(v[:, h * LANE:(h + 1) * LANE], *tabs) for h in range(heads)], axis=-1)


def _gelu(y):
    return y * (0.5 * (1.0 + jnp.tanh(math.sqrt(2.0 / math.pi) * (y + 0.044715 * (y * y * y)))))


def _silu(g):
    return g * jax.nn.sigmoid(g)


def _colsum(v):
    return jnp.sum(v, axis=0, keepdims=True)


def _row_select(rows, n):
    rid = lax.broadcasted_iota(jnp.int32, (n, rows[0].shape[-1]), 0)
    out = jnp.zeros((n, rows[0].shape[-1]), F32)
    for k, r in enumerate(rows):
        out = jnp.where(rid == k, jnp.broadcast_to(r, out.shape), out)
    return out


def _params(sem, vmem_mb):
    return pltpu.CompilerParams(dimension_semantics=sem, vmem_limit_bytes=vmem_mb * 1024 * 1024)


def _rt(tm, w, cb=0):
    return pl.BlockSpec((tm, w), lambda i: (i, cb))


def _full(shape):
    nd = len(shape)
    return pl.BlockSpec(tuple(shape), lambda i: (0,) * nd)


def _rows(fn, *, name, n, ins, in_specs, outs, out_specs, n_acc=0, vmem=48):
    n_in = len(ins)
    n_out = len(outs)

    def body(*refs):
        i = pl.program_id(0)
        res = fn(i, *[r[...] for r in refs[:n_in]])
        if not isinstance(res, (tuple, list)):
            res = (res,)
        assert len(res) == n_out, (name, len(res), n_out)
        for k, (r, v) in enumerate(zip(refs[n_in:], res)):
            if k < n_out - n_acc:
                r[...] = v.astype(r.dtype)
            else:
                @pl.when(i == 0)
                def _():
                    r[...] = v

                @pl.when(i > 0)
                def _():
                    r[...] += v

    return pl.pallas_call(
        body, name=name, grid=(n,), in_specs=list(in_specs), out_specs=tuple(out_specs), out_shape=tuple(outs),
        compiler_params=_params(("arbitrary",), vmem))(*ins)


def _sds(shape, dtype=F32):
    return jax.ShapeDtypeStruct(tuple(shape), dtype)


def _tile_n(n, cap=1536):
    best = None
    for t in range(LANE, min(n, cap) + 1, LANE):
        if n % t == 0:
            best = t
    if best is None or n <= 1408:
        return n
    return best


def _matmul(name, pairs, m, n, *, nt=False, rms_gain=None, resid=None, out_dtype=F32, tm=1024, vmem=56):
    tm = min(tm, m)
    tn = _tile_n(n)
    ks = [a.shape[1] for a, _ in pairs]
    np_ = len(pairs)

    def body(*refs):
        a_refs = refs[:np_]
        b_refs = refs[np_:2 * np_]
        k = 2 * np_
        g_ref = None
        r_ref = None
        if rms_gain is not None:
            g_ref = refs[k]
            k += 1
        if resid is not None:
            r_ref = refs[k]
            k += 1
        o_ref = refs[k]
        scr = refs[k + 1:]
        j = pl.program_id(1)

        @pl.when(j == 0)
        def _():
            for p in range(np_):
                a = a_refs[p][...]
                if p == 0 and g_ref is not None:
                    a = _rms(a.astype(F32), g_ref[...], ks[0])
                scr[p][...] = a.astype(BF16)

        acc = None
        for p in range(np_):
            b = b_refs[p][...].astype(BF16)
            if nt:
                t = lax.dot_general(scr[p][...], b, (((1,), (1,)), ((), ())), preferred_element_type=F32)
            else:
                t = jnp.dot(scr[p][...], b, preferred_element_type=F32)
            acc = t if acc is None else acc + t
        if r_ref is not None:
            acc = acc + r_ref[...]
        o_ref[...] = acc.astype(o_ref.dtype)

    in_specs = [pl.BlockSpec((tm, kk), lambda i, j: (i, 0)) for kk in ks]
    if nt:
        in_specs += [pl.BlockSpec((tn, kk), lambda i, j: (j, 0)) for kk in ks]
    else:
        in_specs += [pl.BlockSpec((kk, tn), lambda i, j: (0, j)) for kk in ks]
    ins = [a for a, _ in pairs] + [b for _, b in pairs]
    if rms_gain is not None:
        in_specs.append(pl.BlockSpec((1, ks[0]), lambda i, j: (0, 0)))
        ins.append(rms_gain)
    if resid is not None:
        in_specs.append(pl.BlockSpec((tm, tn), lambda i, j: (i, j)))
        ins.append(resid)
    return pl.pallas_call(
        body, name=name, grid=(m // tm, n // tn), in_specs=in_specs,
        out_specs=pl.BlockSpec((tm, tn), lambda i, j: (i, j)), out_shape=_sds((m, n), out_dtype),
        scratch_shapes=[pltpu.VMEM((tm, kk), BF16) for kk in ks],
        compiler_params=_params(("arbitrary", "arbitrary"), vmem))(*ins)


def _matmul_tn(name, a, b, *, rms_gain=None, tl=1024, vmem=56):
    l, ka = a.shape
    n = b.shape[1]
    tl = min(tl, l)
    tn = _tile_n(n, 1536)

    def body(*refs):
        if rms_gain is not None:
            a_ref, b_ref, g_ref, o_ref = refs
        else:
            a_ref, b_ref, o_ref = refs
        t = pl.program_id(1)
        av = a_ref[...]
        if rms_gain is not None:
            av = _rms(av.astype(F32), g_ref[...], ka)
        v = _mm_tn(av, b_ref[...])

        @pl.when(t == 0)
        def _():
            o_ref[...] = v

        @pl.when(t > 0)
        def _():
            o_ref[...] += v

    in_specs = [pl.BlockSpec((tl, ka), lambda j, t: (t, 0)), pl.BlockSpec((tl, tn), lambda j, t: (t, j))]
    ins = [a, b]
    if rms_gain is not None:
        in_specs.append(pl.BlockSpec((1, ka), lambda j, t: (0, 0)))
        ins.append(rms_gain)
    return pl.pallas_call(
        body, name=name, grid=(n // tn, l // tl), in_specs=in_specs,
        out_specs=pl.BlockSpec((ka, tn), lambda j, t: (0, j)), out_shape=_sds((ka, n)),
        compiler_params=_params(("arbitrary", "arbitrary"), vmem))(*ins)


ATT_HEADS_PER_STEP = 2
ATT_W = ATT_HEADS_PER_STEP * LANE
ATT_GROUPS = MLA_HEADS // ATT_HEADS_PER_STEP
LOG2E = math.log2(math.e)
ATT_FWD_TILE = 1024
ATT_BWD_TILE = 1024
ATT_BWD_HEADS = 1
ATT_SCALE = D_QK ** -0.5
ATT_QSCALE = ATT_SCALE * LOG2E


def _tri_tables(nq, by_k):
    qs, ks = [], []
    if by_k:
        for ki in range(nq):
            for qi in range(ki, nq):
                qs.append(qi)
                ks.append(ki)
    else:
        for qi in range(nq):
            for ki in range(qi + 1):
                qs.append(qi)
                ks.append(ki)
    return jnp.asarray(np.array(qs, np.int32)), jnp.asarray(np.array(ks, np.int32))


def _causal_keep(shape, transposed):
    r = lax.broadcasted_iota(jnp.int32, shape, 0)
    c = lax.broadcasted_iota(jnp.int32, shape, 1)
    return (r <= c) if transposed else (c <= r)


def _nt16(a, b):
    return lax.dot_general(a, b, (((1,), (1,)), ((), ())), preferred_element_type=F32)


def _row_form(col):
    return jnp.transpose(jnp.broadcast_to(col, (col.shape[0], LANE)))[:8]


def _att_call(body, name, l, tq, tabs, ins, in_specs, outs, out_specs, scratch=(), groups=ATT_GROUPS, vmem=48):
    grid_spec = pltpu.PrefetchScalarGridSpec(
        num_scalar_prefetch=2, grid=(groups, tabs[0].shape[0]), in_specs=in_specs, out_specs=out_specs,
        scratch_shapes=list(scratch))
    return pl.pallas_call(body, name=name, grid_spec=grid_spec, out_shape=outs,
                          compiler_params=_params(("arbitrary", "arbitrary"), vmem))(*tabs, *ins)


def _flash_fwd(name, q, k, v_t, gather=None):
    l = q.shape[0]
    tq = min(ATT_FWD_TILE, l)
    nq = l // tq
    tabs = _tri_tables(nq, by_k=False)
    gx, glayer = gather if gather is not None else ([], 0)
    ng = len(gx)
    n_steps = int(tabs[0].shape[0])

    def body(qt, kt, *refs):
        q_ref, k_ref, vt_ref = refs[:3]
        gx_refs = refs[3:3 + ng]
        o_ref, lset_ref = refs[3 + ng:5 + ng]
        gy_refs = refs[5 + ng:5 + 2 * ng]
        m_s, acc_s = refs[5 + 2 * ng:7 + 2 * ng]
        t = pl.program_id(1)
        qi = qt[t]
        ki = kt[t]
        sls = [slice(h * LANE, (h + 1) * LANE) for h in range(ATT_HEADS_PER_STEP)]
        if ng:
            g_start, g_finish = _layer_gather_steps(gx_refs, gy_refs, refs[7 + 2 * ng:9 + 2 * ng], glayer)

            @pl.when((pl.program_id(0) == 0) & (t == 0))
            def _():
                g_start()

        @pl.when(ki == 0)
        def _():
            m_s[...] = jnp.full(m_s.shape, NEG, F32)
            acc_s[...] = jnp.zeros(acc_s.shape, F32)

        def step(masked):
            sts = [_nt16(k_ref[:, sl], q_ref[:, sl]) for sl in sls]
            for h, sl in enumerate(sls):
                st = sts[h]
                if masked:
                    st = jnp.where(_causal_keep(st.shape, True), st, NEG)
                m_old = m_s[h][:1]
                m_new = jnp.maximum(m_old, jnp.max(st, axis=0, keepdims=True))
                alpha = jnp.exp2(m_old - m_new)
                pt = jnp.exp2(st - m_new).astype(BF16)
                acc_s[sl, :] = alpha * acc_s[sl, :] + jnp.dot(vt_ref[sl, :], pt, preferred_element_type=F32)
                m_s[h] = jnp.broadcast_to(m_new, (8, tq))

        @pl.when(ki < qi)
        def _():
            step(False)

        @pl.when(ki == qi)
        def _():
            step(True)
            row = lax.broadcasted_iota(jnp.int32, (LANE, tq), 0)
            for h, sl in enumerate(sls):
                acc = acc_s[sl, :]
                lsum = acc[D_V:D_V + 1, :]
                o_ref[:, sl] = jnp.transpose(jnp.where(row < D_V, acc / lsum, 0.0))
                lset_ref[h * 8:(h + 1) * 8, :] = m_s[h] + jnp.log2(lsum)

        if ng:
            @pl.when((pl.program_id(0) == ATT_GROUPS - 1) & (t == n_steps - 1))
            def _():
                g_finish()

    qspec = pl.BlockSpec((tq, ATT_W), lambda g, t, qt, kt: (qt[t], g))
    kspec = pl.BlockSpec((tq, ATT_W), lambda g, t, qt, kt: (kt[t], g))
    vspec = pl.BlockSpec((ATT_W, tq), lambda g, t, qt, kt: (g, kt[t]))
    rspec = pl.BlockSpec((8 * ATT_HEADS_PER_STEP, tq), lambda g, t, qt, kt: (g, qt[t]))
    res = _att_call(
        body, name, l, tq, tabs, [q, k, v_t, *gx], [qspec, kspec, vspec] + [ANY] * ng,
        (_sds((l, MLA_PAD)), _sds((8 * MLA_HEADS, l)), *[_sds((4,) + x.shape[1:], x.dtype) for x in gx]),
        (qspec, rspec, *([ANY] * ng)),
        scratch=[pltpu.VMEM((ATT_HEADS_PER_STEP, 8, tq), F32), pltpu.VMEM((ATT_W, tq), F32)]
        + ([pltpu.SemaphoreType.DMA((6 * ng,)), pltpu.SemaphoreType.DMA((6 * ng,))] if ng else []))
    return res[0], res[1], list(res[2:])


def _flash_bwd(name, q, k, v, k_t, o, lse_t, do, reduce=None):
    l = q.shape[0]
    tq = min(ATT_BWD_TILE, l)
    nq = l // tq
    hb = ATT_BWD_HEADS
    wb = hb * LANE

    def delta_fn(i, dov, ov):
        rows = []
        for h in range(MLA_HEADS):
            sl = slice(h * LANE, (h + 1) * LANE)
            rows.append(_row_form(jnp.sum(dov[:, sl] * ov[:, sl], axis=-1, keepdims=True)))
        return jnp.concatenate(rows, axis=0), dov

    delta_t, do16 = _rows(
        delta_fn, name=name + "_delta", n=nq, ins=[do, o], in_specs=[_rt(tq, MLA_PAD)] * 2,
        outs=[_sds((8 * MLA_HEADS, l)), _sds((l, MLA_PAD), BF16)],
        out_specs=[pl.BlockSpec((8 * MLA_HEADS, tq), lambda i: (0, i)), _rt(tq, MLA_PAD)])

    rg, rlayer = reduce if reduce is not None else ([], 0)
    nr = len(rg)
    groups = MLA_HEADS // hb
    tabs_k = _tri_tables(nq, by_k=True)
    n_steps = int(tabs_k[0].shape[0])

    def body(qt, kt, *refs):
        q_ref, k_ref, v_ref, do_ref, kt_ref, lset_ref, dlt_ref = refs[:7]
        rg_refs = refs[7:7 + nr]
        dk_ref, dv_ref, dqt_ref = refs[7 + nr:10 + nr]
        ry_refs = refs[10 + nr:10 + 2 * nr]
        t = pl.program_id(1)
        qi = qt[t]
        ki = kt[t]
        sls = [slice(h * LANE, (h + 1) * LANE) for h in range(hb)]
        if nr:
            r_start, r_finish = _layer_reduce_steps(rg_refs, ry_refs, refs[10 + 2 * nr:12 + 2 * nr], rlayer)

            @pl.when((pl.program_id(0) == 0) & (t == 0))
            def _():
                r_start()

        @pl.when(ki == 0)
        def _():
            dqt_ref[qi] = jnp.zeros((wb, tq), F32)

        def step(masked):
            sts = [_nt16(k_ref[:, sl], q_ref[:, sl]) for sl in sls]
            dpts = [_nt16(v_ref[:, sl], do_ref[:, sl]) for sl in sls]
            for h, sl in enumerate(sls):
                st = sts[h]
                if masked:
                    st = jnp.where(_causal_keep(st.shape, True), st, NEG)
                pt = jnp.exp2(st - lset_ref[h * 8:(h + 1) * 8, :][:1])
                dst = (pt * (dpts[h] - dlt_ref[h * 8:(h + 1) * 8, :][:1])).astype(BF16)
                dv_ref[:, sl] += jnp.dot(pt.astype(BF16), do_ref[:, sl], preferred_element_type=F32)
                dk_ref[:, sl] += jnp.dot(dst, q_ref[:, sl], preferred_element_type=F32)
                dqt_ref[qi, sl, :] += jnp.dot(kt_ref[sl, :], dst, preferred_element_type=F32)

        @pl.when(qi == ki)
        def _():
            dk_ref[...] = jnp.zeros(dk_ref.shape, F32)
            dv_ref[...] = jnp.zeros(dv_ref.shape, F32)
            step(True)
            dqt_ref[qi] = dqt_ref[qi] * ATT_SCALE

        @pl.when(qi > ki)
        def _():
            step(False)

        @pl.when(qi == nq - 1)
        def _():
            dk_ref[...] = dk_ref[...] * (1.0 / LOG2E)

        if nr:
            @pl.when((pl.program_id(0) == groups - 1) & (t == n_steps - 1))
            def _():
                r_finish()

    qspec = pl.BlockSpec((tq, wb), lambda g, t, qt, kt: (qt[t], g))
    kspec = pl.BlockSpec((tq, wb), lambda g, t, qt, kt: (kt[t], g))
    ktspec = pl.BlockSpec((wb, tq), lambda g, t, qt, kt: (g, kt[t]))
    rspec = pl.BlockSpec((8 * hb, tq), lambda g, t, qt, kt: (g, qt[t]))
    dqspec = pl.BlockSpec((nq, wb, tq), lambda g, t, qt, kt: (0, g, 0))
    res = _att_call(body, name + "_dqkv", l, tq, tabs_k, [q, k, v, do16, k_t, lse_t, delta_t, *rg],
                    [qspec, kspec, kspec, qspec, ktspec, rspec, rspec] + [ANY] * nr,
                    (_sds((l, MLA_PAD)), _sds((l, MLA_PAD)), _sds((nq, MLA_PAD, tq)),
                     *[_sds((3,) + g.shape[1:], g.dtype) for g in rg]),
                    (kspec, kspec, dqspec, *([ANY] * nr)),
                    scratch=([pltpu.SemaphoreType.DMA((3 * nr,)), pltpu.SemaphoreType.DMA((3 * nr,))] if nr else []),
                    groups=groups, vmem=56)
    dk, dv, dq_t = res[:3]
    return dq_t, dk, dv, list(res[3:])


def _cmul(ar, ai, br, bi):
    return ar * br - ai * bi, ar * bi + ai * br


def _scan(name, x_re, x_im, a_re, a_im, reverse):
    l, lanes = x_re.shape
    ns = SCAN_SEGS
    tl = l // ns
    steps = int(math.log2(tl))
    assert 2 ** steps == tl and tl * ns == l

    def body(xr_ref, xi_ref, ar_ref, ai_ref, sr_ref, si_ref):
        a_r1 = ar_ref[...]
        a_i1 = ai_ref[...]
        a_r = jnp.broadcast_to(a_r1, (ns, LANE))
        a_i = jnp.broadcast_to(a_i1, (ns, LANE))

        def rows(t):
            t = (tl - 1 - t) if reverse else t
            return pl.ds(pl.multiple_of(t * ns, ns), ns)

        def local(t, carry):
            cr, ci = carry
            r = rows(t)
            pr, pi = _cmul(a_r, a_i, cr, ci)
            return pr + xr_ref[r, :].astype(F32), pi + xi_ref[r, :].astype(F32)

        zero = jnp.zeros((ns, LANE), F32)
        e_r, e_i = lax.fori_loop(0, tl, local, (zero, zero), unroll=min(8, tl))
        p_r, p_i = a_r1, a_i1
        for _ in range(steps):
            p_r, p_i = _cmul(p_r, p_i, p_r, p_i)
        rid = lax.broadcasted_iota(jnp.int32, (ns, LANE), 0)
        c_r = jnp.zeros((1, LANE), F32)
        c_i = jnp.zeros((1, LANE), F32)
        in_r, in_i = zero, zero
        order = range(ns - 2, -1, -1) if reverse else range(1, ns)
        for kk in order:
            src = kk + 1 if reverse else kk - 1
            ek_r = jnp.sum(jnp.where(rid == src, e_r, 0.0), axis=0, keepdims=True)
            ek_i = jnp.sum(jnp.where(rid == src, e_i, 0.0), axis=0, keepdims=True)
            q_r, q_i = _cmul(p_r, p_i, c_r, c_i)
            c_r, c_i = q_r + ek_r, q_i + ek_i
            in_r = jnp.where(rid == kk, jnp.broadcast_to(c_r, (ns, LANE)), in_r)
            in_i = jnp.where(rid == kk, jnp.broadcast_to(c_i, (ns, LANE)), in_i)

        def final(t, carry):
            cr, ci = carry
            r = rows(t)
            pr, pi = _cmul(a_r, a_i, cr, ci)
            nr, ni = pr + xr_ref[r, :].astype(F32), pi + xi_ref[r, :].astype(F32)
            sr_ref[r, :] = nr.astype(sr_ref.dtype)
            si_ref[r, :] = ni.astype(si_ref.dtype)
            return nr, ni

        lax.fori_loop(0, tl, final, (in_r, in_i), unroll=min(8, tl))

    xs = pl.BlockSpec((l, LANE), lambda j: (0, j))
    as_ = pl.BlockSpec((1, LANE), lambda j: (0, j))
    return pl.pallas_call(
        body, name=name, grid=(lanes // LANE,), in_specs=[xs, xs, as_, as_], out_specs=(xs, xs),
        out_shape=(_sds((l, lanes), x_re.dtype), _sds((l, lanes), x_re.dtype)),
        compiler_params=_params(("arbitrary",), 48))(x_re, x_im, a_re, a_im)


ANY = pl.BlockSpec(memory_space=pl.ANY)


def _place():
    mx, my, mc = lax.axis_index("x"), lax.axis_index("y"), lax.axis_index("c")
    return mx, my, mc, [(1 - mx, my), (mx, 1 - my), (1 - mx, 1 - my)]


def _run_copies(copies):
    for cp in copies:
        cp.start()
    for cp in copies:
        cp.wait_recv()
    for cp in copies:
        cp.wait_send()


def _remote(src, dst, sems, k, dev):
    return pltpu.make_async_remote_copy(src_ref=src, dst_ref=dst, send_sem=sems[0].at[k], recv_sem=sems[1].at[k],
                                        device_id=dev, device_id_type=MESH)


def _copy_call(body, name, ins, outs, n_copies, aliases=None):
    return pl.pallas_call(
        body, name=name, in_specs=[ANY] * len(ins), out_specs=[ANY] * len(outs), out_shape=list(outs),
        input_output_aliases=aliases or {},
        scratch_shapes=[pltpu.SemaphoreType.DMA((n_copies,)), pltpu.SemaphoreType.DMA((n_copies,))])(*ins)


def _layer_gather_steps(x_refs, y_refs, sems, layer):
    n = len(x_refs)
    mx, my, mc, peers = _place()
    me = 2 * mx + my

    def ici(idx):
        i, j = divmod(idx, 3)
        return _remote(x_refs[i].at[layer], y_refs[i].at[me], sems, idx, (peers[j][0], peers[j][1], layer))

    def fwd(idx):
        i, j = divmod(idx, 3)
        pk = 2 * peers[j][0] + peers[j][1]
        return _remote(y_refs[i].at[pk], y_refs[i].at[pk], sems, 3 * n + idx, (mx, my, 1 - layer))

    def start():
        @pl.when(mc == layer)
        def _():
            for idx in range(3 * n):
                ici(idx).start()

    def finish():
        @pl.when(mc == layer)
        def _():
            for idx in range(3 * n):
                ici(idx).wait_recv()
                fwd(idx).start()
            for idx in range(3 * n):
                ici(idx).wait_send()
                fwd(idx).wait_send()

        @pl.when(mc != layer)
        def _():
            for idx in range(3 * n):
                fwd(idx).wait_recv()

    return start, finish


def _layer_gather(name, xs, layer):
    n = len(xs)

    def body(*refs):
        start, finish = _layer_gather_steps(refs[:n], refs[n:2 * n], refs[2 * n:], layer)
        start()
        finish()

    return _copy_call(body, name, xs, [_sds((4,) + x.shape[1:], x.dtype) for x in xs], 6 * n)


def _layer_reduce_steps(g_refs, y_refs, sems, layer):
    n = len(g_refs)
    mx, my, mc, peers = _place()

    def cp(idx):
        i, j = divmod(idx, 3)
        px, py = peers[j]
        return _remote(g_refs[i].at[2 * px + py], y_refs[i].at[j], sems, idx, (px, py, layer))

    def start():
        @pl.when(mc == layer)
        def _():
            for idx in range(3 * n):
                cp(idx).start()

    def finish():
        @pl.when(mc == layer)
        def _():
            for idx in range(3 * n):
                cp(idx).wait_recv()
            for idx in range(3 * n):
                cp(idx).wait_send()

    return start, finish


def _layer_reduce(name, gs, layer, small):
    n = len(gs)

    def body(*refs):
        g_refs, s_ref, y_refs, a_ref = refs[:n], refs[n], refs[n + 1:2 * n + 1], refs[2 * n + 1]
        sems, small_sems, local_sem = refs[2 * n + 2:2 * n + 4], refs[2 * n + 4:2 * n + 6], refs[2 * n + 6]
        mx, my, mc, _ = _place()
        me8 = 4 * mx + 2 * my + mc
        start, finish = _layer_reduce_steps(g_refs, y_refs, sems, layer)
        start()
        own = pltpu.make_async_copy(s_ref, a_ref.at[me8], local_sem)
        own.start()
        copies = []
        for j in range(1, 8):
            dev = ((1 - mx) if (j & 4) else mx, (1 - my) if (j & 2) else my, (1 - mc) if (j & 1) else mc)
            copies.append(_remote(s_ref, a_ref.at[me8], small_sems, j - 1, dev))
        _run_copies(copies)
        own.wait()
        finish()

    outs = [_sds((3,) + g.shape[1:], g.dtype) for g in gs] + [_sds((8,) + small.shape, small.dtype)]
    res = pl.pallas_call(
        body, name=name, in_specs=[ANY] * (n + 1), out_specs=[ANY] * (n + 1), out_shape=outs,
        scratch_shapes=[pltpu.SemaphoreType.DMA((3 * n,)), pltpu.SemaphoreType.DMA((3 * n,)),
                        pltpu.SemaphoreType.DMA((7,)), pltpu.SemaphoreType.DMA((7,)), pltpu.SemaphoreType.DMA])(*gs, small)
    return res[:n], res[n]


def _send_d2d(name, per_core):
    shapes = next(p for p in per_core if p is not None)
    n = len(shapes)
    ins = [x for p in per_core if p is not None for x in p]

    def body(*refs):
        o_refs, sems = refs[len(ins):len(ins) + n], refs[len(ins) + n:]
        mx, my, mc, _ = _place()
        off = 0
        for c in range(2):
            if per_core[c] is None:
                continue
            src = refs[off:off + n]
            off += n

            @pl.when(mc == c)
            def _():
                cps = [_remote(src[i], o_refs[i], sems, i, (mx, my, 1 - c)) for i in range(n)]
                for d in cps:
                    d.start()
                if per_core[1 - c] is not None:
                    for d in cps:
                        d.wait_recv()
                for d in cps:
                    d.wait_send()

            if per_core[1 - c] is None:
                @pl.when(mc != c)
                def _():
                    for i in range(n):
                        _remote(src[i], o_refs[i], sems, i, (mx, my, c)).wait_recv()

    return _copy_call(body, name, ins, [_sds(p.shape, p.dtype) for p in shapes], n)


PACK_W = 1024


def _pack(arrs, rows_multiple, dtype):
    flat = jnp.concatenate([a.reshape(-1).astype(dtype) for a in arrs])
    n = flat.shape[0]
    unit = PACK_W * rows_multiple
    tot = -(-n // unit) * unit
    flat = jnp.pad(flat, (0, tot - n))
    return flat.reshape(tot // PACK_W, PACK_W)


def _unpack(flat, shapes):
    flat = flat.reshape(-1)
    out = []
    off = 0
    for s in shapes:
        n = int(np.prod(s))
        out.append(flat[off:off + n].reshape(s))
        off += n
    return out


def _rope_tables(pos):
    l = pos.shape[0]
    tm = min(512, l)
    inv = (np.float32(ROPE_THETA) ** (-np.arange(0, D_ROPE, 2, dtype=np.float32) / np.float32(D_ROPE))).astype(np.float32)
    lane_f = np.zeros((1, LANE), np.float32)
    lane_f[0, D_NOPE:D_NOPE + 16] = inv
    lane_f[0, D_NOPE + 16:D_NOPE + 32] = inv

    def fn(i, p, f):
        ang = p * f
        lane = lax.broadcasted_iota(jnp.int32, ang.shape, 1)
        co = jnp.cos(ang)
        si = jnp.sin(ang)
        c = jnp.where(lane < D_NOPE, 1.0, jnp.where(lane < D_QK, co, 0.0))
        s1 = jnp.where((lane >= D_NOPE) & (lane < D_NOPE + 16), -si, 0.0)
        s2 = jnp.where((lane >= D_NOPE + 16) & (lane < D_QK), si, 0.0)
        return c, s1, s2

    return _rows(fn, name="rope_tables", n=l // tm, ins=[pos, jnp.asarray(lane_f)],
                 in_specs=[_rt(tm, 1), _full((1, LANE))], outs=[_sds((l, LANE))] * 3, out_specs=[_rt(tm, LANE)] * 3)


def _ssm_param_fn(lr, li, log_dt, br, bi):
    dt = jnp.exp(log_dt)
    mag = jnp.exp(lr * dt)
    a_re = mag * jnp.cos(li * dt)
    a_im = mag * jnp.sin(li * dt)
    den = lr * lr + li * li
    e_re = a_re - 1.0
    e_im = a_im
    f_re = (e_re * lr + e_im * li) / den
    f_im = (e_im * lr - e_re * li) / den
    bb_re = f_re[None] * br - f_im[None] * bi
    bb_im = f_re[None] * bi + f_im[None] * br
    return a_re, a_im, bb_re, bb_im


def _ssm_params(name, lr, li, log_dt, br, bi):
    g, n = lr.shape
    c = br.shape[0]
    return _rows(lambda i, *v: _ssm_param_fn(*v), name=name, n=1, ins=[lr, li, log_dt, br, bi],
                 in_specs=[_full((g, n)), _full((g, n)), _full((g, 1)), _full((c, g, n)), _full((c, g, n))],
                 outs=[_sds((g, n)), _sds((g, n)), _sds((c, g, n)), _sds((c, g, n))],
                 out_specs=[_full((g, n)), _full((g, n)), _full((c, g, n)), _full((c, g, n))])


def _ssm_params_bwd(name, lr, li, log_dt, br, bi, d_are, d_aim, d_bbre, d_bbim):
    g, n = lr.shape
    c = br.shape[0]

    def fn(i, lr, li, log_dt, br, bi, g0, g1, g2, g3):
        _, vjp = jax.vjp(_ssm_param_fn, lr, li, log_dt, br, bi)
        return vjp((g0, g1, g2, g3))

    sp = [_full((g, n)), _full((g, n)), _full((g, 1)), _full((c, g, n)), _full((c, g, n))]
    return _rows(fn, name=name, n=1, ins=[lr, li, log_dt, br, bi, d_are, d_aim, d_bbre, d_bbim],
                 in_specs=sp + [_full((g, n)), _full((g, n)), _full((c, g, n)), _full((c, g, n))],
                 outs=[_sds((g, n)), _sds((g, n)), _sds((g, 1)), _sds((c, g, n)), _sds((c, g, n))], out_specs=sp)


_EYE8 = np.eye(8, dtype=np.float32)


def _blockdiag(v):
    j, g, p, q = v.shape
    m = v[:, :, :, None, :] * jnp.asarray(_EYE8)[None, :, None, :, None]
    return m.reshape(j, g * p, g * q)


def _blockdiag_t(m, p, q):
    j = m.shape[0]
    m = m.reshape(j, 8, p, 8, q)
    return jnp.sum(m * jnp.asarray(_EYE8)[None, :, None, :, None], axis=3)


def _to_perm(v, l):
    ns = SCAN_SEGS
    return v.reshape(ns, l // ns, v.shape[-1]).transpose(1, 0, 2).reshape(l, v.shape[-1])


def _from_perm(v, l):
    ns = SCAN_SEGS
    return v.reshape(l // ns, ns, v.shape[-1]).transpose(1, 0, 2).reshape(l, v.shape[-1])


def _prep_layer(w, i):
    p = {}
    w_in = w['w_in'][i]
    z = lambda n: jnp.zeros((D_MODEL, n), w_in.dtype)
    o = Q_LORA + KV_LORA
    p['w_s'] = jnp.concatenate([w_in[:, :o], z(D_NOPE), w_in[:, o:o + D_ROPE], z(HEAD_PAD - D_QK)], axis=1)
    o += D_ROPE
    p['w_u'] = w_in[:, o:o + SSM_WIDTH]
    o += SSM_WIDTH
    p['w_xq'] = w_in[:, o:o + X_WIDTH]
    o += X_WIDTH
    p['w_g'] = w_in[:, o:]
    wq = w['w_q_b'][i].reshape(Q_LORA, MLA_HEADS, D_QK)
    p['w_qb'] = jnp.pad(wq, ((0, 0), (0, 0), (0, HEAD_PAD - D_QK))).reshape(Q_LORA, MLA_PAD)
    wkv = w['w_kv_b'][i].reshape(KV_LORA, MLA_HEADS, D_NOPE + D_V)
    p['w_k'] = jnp.pad(wkv[:, :, :D_NOPE], ((0, 0), (0, 0), (0, HEAD_PAD - D_NOPE))).reshape(KV_LORA, MLA_PAD)
    p['w_v'] = jnp.pad(wkv[:, :, D_NOPE:], ((0, 0), (0, 0), (0, HEAD_PAD - D_V))).reshape(KV_LORA, MLA_PAD)
    wo = w['w_o_mla'][i].reshape(MLA_HEADS, D_V, D_MODEL)
    p['w_oa'] = jnp.pad(wo, ((0, 0), (0, HEAD_PAD - D_V), (0, 0))).reshape(MLA_PAD, D_MODEL)
    for n in ('w_glu', 'w_o_ssm', 'w_mem_kv', 'w_o_cross', 'w_out', 'w_up', 'w_down'):
        p[n] = w[n][i]
    p['conv_w'] = w['conv_w'][i]
    for n in ('norm_mix_g', 'q_a_norm_g', 'kv_a_norm_g', 'b_glu', 'mem_norm_g', 'xq_norm_g', 'xk_norm_g', 'b_gate',
              'norm_ffn_g', 'conv_b'):
        p[n] = w[n][i].reshape(1, -1)
    p['q_norm_g'] = jnp.pad(w['q_norm_g'][i], (0, HEAD_PAD - D_QK)).reshape(1, HEAD_PAD)
    p['k_norm_g'] = jnp.pad(w['k_norm_g'][i], (0, HEAD_PAD - D_QK)).reshape(1, HEAD_PAD)
    p['ssm_d'] = w['ssm_d'][i].reshape(1, SSM_WIDTH)
    p['lr'] = w['ssm_lambda_re'][i]
    p['li'] = w['ssm_lambda_im'][i]
    p['log_dt'] = w['ssm_log_dt'][i].reshape(SSM_GROUPS, 1)
    p['br'] = w['ssm_b_re'][i].transpose(2, 0, 1)
    p['bi'] = w['ssm_b_im'][i].transpose(2, 0, 1)
    cr = w['ssm_c_re'][i].reshape(SSM_JB, 8, SSM_GROUP_CH, SSM_STATE).transpose(0, 1, 3, 2)
    ci = w['ssm_c_im'][i].reshape(SSM_JB, 8, SSM_GROUP_CH, SSM_STATE).transpose(0, 1, 3, 2)
    p['c_mat'] = jnp.concatenate([_blockdiag(cr), -_blockdiag(ci)], axis=1).astype(BF16)
    return p


def _b_mat(bb_re, bb_im):
    r = bb_re.transpose(1, 0, 2).reshape(SSM_JB, 8, SSM_GROUP_CH, SSM_STATE)
    i = bb_im.transpose(1, 0, 2).reshape(SSM_JB, 8, SSM_GROUP_CH, SSM_STATE)
    return jnp.concatenate([_blockdiag(r), _blockdiag(i)], axis=2).astype(BF16)


def _qkv_fn(ps, c, s1, s2, qag, wqb, kvag, wk, wv, qng, kng):
    c_q = ps[:, :Q_LORA]
    c_kv = ps[:, Q_LORA:Q_LORA + KV_LORA]
    kr = ps[:, Q_LORA + KV_LORA:]
    cqn = _rms(c_q, qag, Q_LORA)
    ckvn = _rms(c_kv, kvag, KV_LORA)
    q_raw = _mm(cqn, wqb)
    k_raw = _mm(ckvn, wk) + jnp.concatenate([kr] * MLA_HEADS, axis=-1)
    v = _mm(ckvn, wv)
    q = _heads(_rope, _head_rms(q_raw, qng, MLA_HEADS, D_QK), MLA_HEADS, c, s1, s2)
    k = _heads(_rope, _head_rms(k_raw, kng, MLA_HEADS, D_QK), MLA_HEADS, c, s1, s2)
    lane = lax.broadcasted_iota(jnp.int32, v.shape, 1)
    v = jnp.where((lane & (LANE - 1)) == D_V, 1.0, v)
    return q * ATT_QSCALE, k, v


def _layer_fwd(name, x, tabs, mem, p, gather=None):
    l = x.shape[0]
    tm = min(512, l)
    nt = l // tm
    sv = {'x0': x}
    sv['p_g'] = _matmul(name + "_in_g", [(x, p['w_g'])], l, 3 * D_MODEL, rms_gain=p['norm_mix_g'])
    sv['p_u'] = _matmul(name + "_in_u", [(x, p['w_u'])], l, SSM_WIDTH, rms_gain=p['norm_mix_g'])
    sv['p_xq'] = _matmul(name + "_in_xq", [(x, p['w_xq'])], l, X_WIDTH, rms_gain=p['norm_mix_g'])
    sv['p_s'] = _matmul(name + "_in_s", [(x, p['w_s'])], l, SMALL_W, rms_gain=p['norm_mix_g'])

    qkv_consts = [p['q_a_norm_g'], p['w_qb'], p['kv_a_norm_g'], p['w_k'], p['w_v'], p['q_norm_g'], p['k_norm_g']]
    qkv_cspecs = [_full(a.shape) for a in qkv_consts]
    def qkv_fwd(i, *a):
        qv, kv, vv = _qkv_fn(*a)
        return qv, kv, vv, jnp.transpose(kv), jnp.transpose(vv)

    q, k, v, k_t, v_t = _rows(qkv_fwd, name=name + "_qkv", n=nt, ins=[sv['p_s'], *tabs, *qkv_consts],
                              in_specs=[_rt(tm, SMALL_W)] + [_rt(tm, LANE)] * 3 + qkv_cspecs,
                              outs=[_sds((l, MLA_PAD), BF16)] * 3 + [_sds((MLA_PAD, l), BF16)] * 2,
                              out_specs=[_rt(tm, MLA_PAD)] * 3 + [pl.BlockSpec((MLA_PAD, tm), lambda i: (0, i))] * 2)
    sv['q'], sv['k'], sv['v'], sv['k_t'] = q, k, v, k_t
    sv['o_a'], sv['lse_t'], sv['gathered'] = _flash_fwd(name + "_attn", q, k, v_t, gather=gather)

    a_re, a_im, bb_re, bb_im = _ssm_params(name + "_ssm_par", p['lr'], p['li'], p['log_dt'], p['br'], p['bi'])
    sv['a_re'], sv['a_im'] = a_re.reshape(1, SSM_LANES), a_im.reshape(1, SSM_LANES)
    sv['b_mat'] = _b_mat(bb_re, bb_im)
    u_p = _to_perm(sv['p_u'], l)
    sv['u_p'] = u_p

    def bu_fn(i, u, bm):
        res = [_mm(u[:, j * LANE:(j + 1) * LANE], bm[j]) for j in range(SSM_JB)]
        return (jnp.concatenate([r[:, :512] for r in res], axis=-1), jnp.concatenate([r[:, 512:] for r in res], axis=-1))

    ts = min(256, l)
    bu_re, bu_im = _rows(bu_fn, name=name + "_ssm_bu", n=l // ts, ins=[u_p, sv['b_mat']],
                         in_specs=[_rt(ts, SSM_WIDTH), _full(sv['b_mat'].shape)],
                         outs=[_sds((l, SSM_LANES), SSM_STATE_DTYPE)] * 2, out_specs=[_rt(ts, SSM_LANES)] * 2)
    s_re, s_im = _scan(name + "_ssm_scan", bu_re, bu_im, sv['a_re'], sv['a_im'], reverse=False)
    sv['s_re'], sv['s_im'] = s_re, s_im

    def glu_fn(i, sr, si, u, cm, dsk, wg, bg):
        y = jnp.concatenate([_mm(jnp.concatenate([sr[:, j * 512:(j + 1) * 512], si[:, j * 512:(j + 1) * 512]], axis=-1),
                                 cm[j]) for j in range(SSM_JB)], axis=-1) + dsk * u
        zz = _gelu(y)
        return zz * jax.nn.sigmoid(_mm(zz, wg) + bg)

    glu_consts = [p['c_mat'], p['ssm_d'], p['w_glu'], p['b_glu']]
    zo_p = _rows(glu_fn, name=name + "_ssm_glu", n=l // ts, ins=[s_re, s_im, u_p, *glu_consts],
                 in_specs=[_rt(ts, SSM_LANES), _rt(ts, SSM_LANES), _rt(ts, SSM_WIDTH)] + [_full(a.shape) for a in glu_consts],
                 outs=[_sds((l, SSM_WIDTH), BF16)], out_specs=[_rt(ts, SSM_WIDTH)])[0]
    sv['zo'] = _from_perm(zo_p, l)

    m_len = mem.shape[0]

    def memkv_fn(i, mm_, mg, wmk, xkg):
        kv = _mm(_rms(mm_, mg, D_MODEL), wmk)
        return _head_rms(kv[:, :X_WIDTH], xkg, X_HEADS, X_HEAD_DIM), kv[:, X_WIDTH:]

    mem_consts = [p['mem_norm_g'], p['w_mem_kv'], p['xk_norm_g']]
    k_c, v_c = _rows(memkv_fn, name=name + "_memkv", n=1, ins=[mem, *mem_consts],
                     in_specs=[_full(mem.shape)] + [_full(a.shape) for a in mem_consts],
                     outs=[_sds((m_len, X_WIDTH))] * 2, out_specs=[_full((m_len, X_WIDTH))] * 2)
    sv['k_c'], sv['v_c'] = k_c, v_c

    def cross_fn(i, xq, kc, vc, xqg):
        outs = []
        for h in range(X_HEADS):
            sl = slice(h * LANE, (h + 1) * LANE)
            qh = _rms(xq[:, sl], xqg, X_HEAD_DIM)
            s = _mm_nt(qh, kc[:, sl]) * (X_HEAD_DIM ** -0.5)
            s = s - jnp.max(s, axis=-1, keepdims=True)
            e = jnp.exp(s)
            pr = e / jnp.sum(e, axis=-1, keepdims=True)
            outs.append(_mm(pr, vc[:, sl]))
        return jnp.concatenate(outs, axis=-1)

    sv['o_c'] = _rows(cross_fn, name=name + "_cross", n=nt, ins=[sv['p_xq'], k_c, v_c, p['xq_norm_g']],
                      in_specs=[_rt(tm, X_WIDTH), _full(k_c.shape), _full(v_c.shape), _full((1, LANE))],
                      outs=[_sds((l, X_WIDTH), BF16)], out_specs=[_rt(tm, X_WIDTH)])[0]

    def merge_fn(i, oa, zo, oc, pg, x0, woa, wos, woc, bg, wout):
        gates = jax.nn.sigmoid(pg + bg)
        merged = (gates[:, :D_MODEL] * _mm(oa, woa) + gates[:, D_MODEL:2 * D_MODEL] * _mm(zo, wos)
                  + gates[:, 2 * D_MODEL:] * _mm(oc, woc))
        return x0 + _mm(merged, wout), merged

    merge_consts = [p['w_oa'], p['w_o_ssm'], p['w_o_cross'], p['b_gate'], p['w_out']]
    tg = min(256, l)
    x1, merged = _rows(merge_fn, name=name + "_merge", n=l // tg, ins=[sv['o_a'], sv['zo'], sv['o_c'], sv['p_g'], x, *merge_consts],
                       in_specs=[_rt(tg, MLA_PAD), _rt(tg, SSM_WIDTH), _rt(tg, X_WIDTH), _rt(tg, 3 * D_MODEL), _rt(tg, D_MODEL)]
                       + [_full(a.shape) for a in merge_consts],
                       outs=[_sds((l, D_MODEL)), _sds((l, D_MODEL), BF16)], out_specs=[_rt(tg, D_MODEL)] * 2)
    sv['x1'], sv['merged'] = x1, merged

    up = _matmul(name + "_up", [(x1, p['w_up'])], l, 2 * D_FF, rms_gain=p['norm_ffn_g'])
    sv['up'] = up
    tc = min(128, l)

    def conv_fn(i, upt, halo, cw, cb):
        upc = _conv(i, upt, halo, cw) + cb
        return _silu(upc[:, :D_FF]) * upc[:, D_FF:]

    act = _rows(conv_fn, name=name + "_conv", n=l // tc, ins=[up, up, p['conv_w'], p['conv_b']],
                in_specs=[_rt(tc, 2 * D_FF), _halo_prev(tc, 2 * D_FF), _full((3, 2 * D_FF)), _full((1, 2 * D_FF))],
                outs=[_sds((l, D_FF), BF16)], out_specs=[_rt(tc, D_FF)])[0]
    sv['act'] = act
    x2 = _matmul(name + "_down", [(act, p['w_down'])], l, D_MODEL, resid=x1)
    return x2, sv


def _halo_prev(tm, w):
    return pl.BlockSpec((8, w), lambda i: (jnp.maximum(i * (tm // 8) - 1, 0), 0))


def _halo_next(tm, w, n_tiles):
    last = n_tiles * (tm // 8) - 1
    return pl.BlockSpec((8, w), lambda i: (jnp.minimum((i + 1) * (tm // 8), last), 0))


def _conv(i, tile, halo, cw):
    halo = jnp.where(i > 0, halo, 0.0)
    ext = jnp.concatenate([halo, tile], axis=0)
    n = ext.shape[0]
    x1 = pltpu.roll(ext, 1, 0)[8:]
    x2 = pltpu.roll(ext, 2, 0)[8:]
    del n
    return cw[0:1] * x2 + cw[1:2] * x1 + cw[2:3] * tile


def _layer_bwd(name, dx2, sv, tabs, mem, p, reduce=None):
    l = dx2.shape[0]
    tm = min(512, l)
    nt = l // tm
    g = {}
    x1 = sv['x1']
    dact = _matmul(name + "_b_down", [(dx2, p['w_down'])], l, D_FF, nt=True)
    g['w_down'] = _matmul_tn(name + "_gw_down", sv['act'], dx2)
    tc = min(128, l)
    ntc = l // tc

    def conv_b(i, upt, up_prev, up_next, da, da_next, cw, cb):
        up_prev = jnp.where(i > 0, up_prev, 0.0)
        da_next = jnp.where(i < ntc - 1, da_next, 0.0)
        ext = jnp.concatenate([up_prev, upt, up_next], axis=0)
        x0 = ext[8:]
        xm1 = pltpu.roll(ext, 1, 0)[8:]
        xm2 = pltpu.roll(ext, 2, 0)[8:]
        upc = cw[0:1] * xm2 + cw[1:2] * xm1 + cw[2:3] * x0 + cb
        _, vjp = jax.vjp(lambda a, b: _silu(a) * b, upc[:, :D_FF], upc[:, D_FF:])
        dg, dv = vjp(jnp.concatenate([da, da_next], axis=0))
        dupc = jnp.concatenate([dg, dv], axis=-1)
        n = dupc.shape[0]
        dup = cw[2:3] * dupc[:tc] + cw[1:2] * pltpu.roll(dupc, n - 1, 0)[:tc] + cw[0:1] * pltpu.roll(dupc, n - 2, 0)[:tc]
        dt = dupc[:tc]
        dcw = _row_select([_colsum(dt * xm2[:tc]), _colsum(dt * xm1[:tc]), _colsum(dt * upt)], 8)
        return dup, dcw, _colsum(dt)

    dup, g_cw, g_cb = _rows(
        conv_b, name=name + "_b_conv", n=ntc, ins=[sv['up'], sv['up'], sv['up'], dact, dact, p['conv_w'], p['conv_b']],
        in_specs=[_rt(tc, 2 * D_FF), _halo_prev(tc, 2 * D_FF), _halo_next(tc, 2 * D_FF, ntc), _rt(tc, D_FF),
                  _halo_next(tc, D_FF, ntc), _full((3, 2 * D_FF)), _full((1, 2 * D_FF))],
        outs=[_sds((l, 2 * D_FF)), _sds((8, 2 * D_FF)), _sds((1, 2 * D_FF))],
        out_specs=[_rt(tc, 2 * D_FF), _full((8, 2 * D_FF)), _full((1, 2 * D_FF))], n_acc=2, vmem=56)
    g['conv_w'] = g_cw[:3]
    g['conv_b'] = g_cb
    dh2 = _matmul(name + "_b_up", [(dup, p['w_up'])], l, D_MODEL, nt=True, tm=256)
    g['w_up'] = _matmul_tn(name + "_gw_up", x1, dup, rms_gain=p['norm_ffn_g'])

    def norm_b(i, xv, dh, dres, gn):
        _, vjp = jax.vjp(lambda a, b: _rms(a, b, D_MODEL), xv, gn)
        dxv, dgn = vjp(dh)
        return dres + dxv, dgn

    dx1, g['norm_ffn_g'] = _rows(norm_b, name=name + "_b_norm2", n=nt, ins=[x1, dh2, dx2, p['norm_ffn_g']],
                                 in_specs=[_rt(tm, D_MODEL)] * 3 + [_full((1, D_MODEL))],
                                 outs=[_sds((l, D_MODEL)), _sds((1, D_MODEL))], out_specs=[_rt(tm, D_MODEL), _full((1, D_MODEL))],
                                 n_acc=1)

    tg = min(256, l)

    def merge_b(i, dx, oa, zo, oc, pg, woa, wos, woc, bg, wout):
        dm = _mm_nt(dx, wout)
        gates = jax.nn.sigmoid(pg + bg)
        ys = [_mm(oa, woa), _mm(zo, wos), _mm(oc, woc)]
        dys, dpg = [], []
        for b in range(3):
            gb = gates[:, b * D_MODEL:(b + 1) * D_MODEL]
            dys.append(dm * gb)
            dpg.append(dm * ys[b] * gb * (1.0 - gb))
        dpg = jnp.concatenate(dpg, axis=-1)
        return (_mm_nt(dys[0], woa), _mm_nt(dys[1], wos), _mm_nt(dys[2], woc), dpg, dys[0], dys[1], dys[2], _colsum(dpg))

    merge_consts = [p['w_oa'], p['w_o_ssm'], p['w_o_cross'], p['b_gate'], p['w_out']]
    (do_a, dzo, do_c, dp_g, dy_a, dy_b, dy_c, g['b_gate']) = _rows(
        merge_b, name=name + "_b_merge", n=l // tg, ins=[dx1, sv['o_a'], sv['zo'], sv['o_c'], sv['p_g'], *merge_consts],
        in_specs=[_rt(tg, D_MODEL), _rt(tg, MLA_PAD), _rt(tg, SSM_WIDTH), _rt(tg, X_WIDTH), _rt(tg, 3 * D_MODEL)]
        + [_full(a.shape) for a in merge_consts],
        outs=[_sds((l, MLA_PAD)), _sds((l, SSM_WIDTH)), _sds((l, X_WIDTH)), _sds((l, 3 * D_MODEL)),
              _sds((l, D_MODEL), BF16), _sds((l, D_MODEL), BF16), _sds((l, D_MODEL), BF16), _sds((1, 3 * D_MODEL))],
        out_specs=[_rt(tg, MLA_PAD), _rt(tg, SSM_WIDTH), _rt(tg, X_WIDTH), _rt(tg, 3 * D_MODEL),
                   _rt(tg, D_MODEL), _rt(tg, D_MODEL), _rt(tg, D_MODEL), _full((1, 3 * D_MODEL))], n_acc=1, vmem=56)
    g['w_out'] = _matmul_tn(name + "_gw_out", sv['merged'], dx1)
    g['w_oa'] = _matmul_tn(name + "_gw_oa", sv['o_a'], dy_a)
    g['w_o_ssm'] = _matmul_tn(name + "_gw_os", sv['zo'], dy_b)
    g['w_o_cross'] = _matmul_tn(name + "_gw_oc", sv['o_c'], dy_c)

    k_c, v_c = sv['k_c'], sv['v_c']
    m_len = k_c.shape[0]

    def cross_b(i, xq, do, kc, vc, xqg):
        dxq, dk, dv = [], [], []
        dg = jnp.zeros((1, LANE), F32)
        for h in range(X_HEADS):
            sl = slice(h * LANE, (h + 1) * LANE)
            qh, vjp = jax.vjp(lambda a, b: _rms(a, b, X_HEAD_DIM), xq[:, sl], xqg)
            sc = X_HEAD_DIM ** -0.5
            s = _mm_nt(qh, kc[:, sl]) * sc
            s = s - jnp.max(s, axis=-1, keepdims=True)
            e = jnp.exp(s)
            pr = e / jnp.sum(e, axis=-1, keepdims=True)
            doh = do[:, sl]
            dv.append(_mm_tn(pr, doh))
            dp = _mm_nt(doh, vc[:, sl])
            ds = pr * (dp - jnp.sum(dp * pr, axis=-1, keepdims=True)) * sc
            dk.append(_mm_tn(ds, qh))
            dxh, dgh = vjp(_mm(ds, kc[:, sl]))
            dxq.append(dxh)
            dg = dg + dgh
        return jnp.concatenate(dxq, axis=-1), jnp.concatenate(dk, axis=-1), jnp.concatenate(dv, axis=-1), dg

    dp_xq, dk_c, dv_c, g['xq_norm_g'] = _rows(
        cross_b, name=name + "_b_cross", n=nt, ins=[sv['p_xq'], do_c, k_c, v_c, p['xq_norm_g']],
        in_specs=[_rt(tm, X_WIDTH), _rt(tm, X_WIDTH), _full(k_c.shape), _full(v_c.shape), _full((1, LANE))],
        outs=[_sds((l, X_WIDTH)), _sds((m_len, X_WIDTH)), _sds((m_len, X_WIDTH)), _sds((1, LANE))],
        out_specs=[_rt(tm, X_WIDTH), _full((m_len, X_WIDTH)), _full((m_len, X_WIDTH)), _full((1, LANE))], n_acc=3)

    def memkv_b(i, mm_, dk, dv, mg, wmk, xkg):
        memn, vjp_n = jax.vjp(lambda a, b: _rms(a, b, D_MODEL), mm_, mg)
        kv = _mm(memn, wmk)
        _, vjp_k = jax.vjp(lambda a, b: _head_rms(a, b, X_HEADS, X_HEAD_DIM), kv[:, :X_WIDTH], xkg)
        dkr, dxkg = vjp_k(dk)
        dkv = jnp.concatenate([dkr, dv], axis=-1)
        _, dmg = vjp_n(_mm_nt(dkv, wmk))
        return _mm_tn(memn, dkv), dmg, dxkg

    mem_consts = [p['mem_norm_g'], p['w_mem_kv'], p['xk_norm_g']]
    g['w_mem_kv'], g['mem_norm_g'], g['xk_norm_g'] = _rows(
        memkv_b, name=name + "_b_memkv", n=1, ins=[mem, dk_c, dv_c, *mem_consts],
        in_specs=[_full(mem.shape), _full(dk_c.shape), _full(dv_c.shape)] + [_full(a.shape) for a in mem_consts],
        outs=[_sds((D_MODEL, 2 * X_WIDTH)), _sds((1, D_MODEL)), _sds((1, LANE))],
        out_specs=[_full((D_MODEL, 2 * X_WIDTH)), _full((1, D_MODEL)), _full((1, LANE))])

    u_p = sv['u_p']
    dzo_p = _to_perm(dzo, l)
    s_re, s_im = sv['s_re'], sv['s_im']

    def glu_b(i, sr, si, u, dz, cm, dsk, wg, bg):
        cats = [jnp.concatenate([sr[:, j * 512:(j + 1) * 512], si[:, j * 512:(j + 1) * 512]], axis=-1) for j in range(SSM_JB)]
        y = jnp.concatenate([_mm(cats[j], cm[j]) for j in range(SSM_JB)], axis=-1) + dsk * u
        zz, vjp_g = jax.vjp(_gelu, y)
        t = _mm(zz, wg) + bg
        sg = jax.nn.sigmoid(t)
        dt = dz * zz * sg * (1.0 - sg)
        dzz = dz * sg + _mm_nt(dt, wg)
        dy = vjp_g(dzz)[0]
        dss = [_mm_nt(dy[:, j * LANE:(j + 1) * LANE], cm[j]) for j in range(SSM_JB)]
        dsr = jnp.concatenate([d[:, :512] for d in dss], axis=-1)
        dsi = jnp.concatenate([d[:, 512:] for d in dss], axis=-1)
        dcm = jnp.stack([_mm_tn(cats[j], dy[:, j * LANE:(j + 1) * LANE]) for j in range(SSM_JB)], axis=0)
        return dsr, dsi, dy * dsk, dcm, _colsum(dy * u), _mm_tn(zz, dt), _colsum(dt)

    glu_consts = [p['c_mat'], p['ssm_d'], p['w_glu'], p['b_glu']]
    ts = min(256, l)
    nts = l // ts
    ds_re, ds_im, du_dir, g['c_mat'], g['ssm_d'], g['w_glu'], g['b_glu'] = _rows(
        glu_b, name=name + "_b_glu", n=nts, ins=[s_re, s_im, u_p, dzo_p, *glu_consts],
        in_specs=[_rt(ts, SSM_LANES), _rt(ts, SSM_LANES), _rt(ts, SSM_WIDTH), _rt(ts, SSM_WIDTH)] + [_full(a.shape) for a in glu_consts],
        outs=[_sds((l, SSM_LANES), SSM_STATE_DTYPE), _sds((l, SSM_LANES), SSM_STATE_DTYPE), _sds((l, SSM_WIDTH)),
              _sds((SSM_JB, 1024, LANE)), _sds((1, SSM_WIDTH)),
              _sds((SSM_WIDTH, SSM_WIDTH)), _sds((1, SSM_WIDTH))],
        out_specs=[_rt(ts, SSM_LANES), _rt(ts, SSM_LANES), _rt(ts, SSM_WIDTH), _full((SSM_JB, 1024, LANE)), _full((1, SSM_WIDTH)),
                   _full((SSM_WIDTH, SSM_WIDTH)), _full((1, SSM_WIDTH))], n_acc=4)
    gb_re, gb_im = _scan(name + "_b_scan", ds_re, ds_im, sv['a_re'], -sv['a_im'], reverse=True)
    ns = SCAN_SEGS
    last_blk = l // ns - 1

    def da_fn(i, *vals):
        gr, gi, sr, si, hr, hi, lr_, li_ = [v.astype(F32) for v in vals]
        rid = lax.broadcasted_iota(jnp.int32, lr_.shape, 0)
        fr = jnp.where(rid == 0, 0.0, pltpu.roll(lr_, 1, 0))
        fi = jnp.where(rid == 0, 0.0, pltpu.roll(li_, 1, 0))
        hr = jnp.where(i == 0, fr, hr)
        hi = jnp.where(i == 0, fi, hi)
        if ts > ns:
            pr = jnp.concatenate([hr, sr[:ts - ns]], axis=0)
            pi = jnp.concatenate([hi, si[:ts - ns]], axis=0)
        else:
            pr, pi = hr, hi
        return _colsum(gr * pr + gi * pi), _colsum(gi * pr - gr * pi)

    hprev = pl.BlockSpec((ns, SSM_LANES), lambda i: (jnp.maximum(i * (ts // ns) - 1, 0), 0))
    hlast = pl.BlockSpec((ns, SSM_LANES), lambda i: (last_blk, 0))
    da_re, da_im = _rows(da_fn, name=name + "_b_da", n=nts, ins=[gb_re, gb_im, s_re, s_im, s_re, s_im, s_re, s_im],
                         in_specs=[_rt(ts, SSM_LANES)] * 4 + [hprev, hprev, hlast, hlast],
                         outs=[_sds((1, SSM_LANES))] * 2, out_specs=[_full((1, SSM_LANES))] * 2, n_acc=2)

    def bu_b(i, dbr, dbi, u, dud, bm):
        dus, dbm = [], []
        for j in range(SSM_JB):
            cat = jnp.concatenate([dbr[:, j * 512:(j + 1) * 512], dbi[:, j * 512:(j + 1) * 512]], axis=-1)
            dus.append(_mm_nt(cat, bm[j]))
            dbm.append(_mm_tn(u[:, j * LANE:(j + 1) * LANE], cat))
        return dud + jnp.concatenate(dus, axis=-1), jnp.stack(dbm, axis=0)

    du_p, d_bmat = _rows(bu_b, name=name + "_b_bu", n=nts, ins=[gb_re, gb_im, u_p, du_dir, sv['b_mat']],
                         in_specs=[_rt(ts, SSM_LANES), _rt(ts, SSM_LANES), _rt(ts, SSM_WIDTH), _rt(ts, SSM_WIDTH),
                                   _full(sv['b_mat'].shape)],
                         outs=[_sds((l, SSM_WIDTH)), _sds((SSM_JB, LANE, 1024))],
                         out_specs=[_rt(ts, SSM_WIDTH), _full((SSM_JB, LANE, 1024))], n_acc=1)
    dp_u = _from_perm(du_p, l)
    dbb_re = _blockdiag_t(d_bmat[:, :, :512], SSM_GROUP_CH, SSM_STATE).reshape(SSM_GROUPS, SSM_GROUP_CH, SSM_STATE).transpose(1, 0, 2)
    dbb_im = _blockdiag_t(d_bmat[:, :, 512:], SSM_GROUP_CH, SSM_STATE).reshape(SSM_GROUPS, SSM_GROUP_CH, SSM_STATE).transpose(1, 0, 2)
    g['lr'], g['li'], g['log_dt'], g['br'], g['bi'] = _ssm_params_bwd(
        name + "_b_ssm_par", p['lr'], p['li'], p['log_dt'], p['br'], p['bi'],
        da_re.reshape(SSM_GROUPS, SSM_STATE), da_im.reshape(SSM_GROUPS, SSM_STATE), dbb_re, dbb_im)

    dq_t, dk, dv, rode = _flash_bwd(name + "_b_attn", sv['q'], sv['k'], sv['v'], sv['k_t'], sv['o_a'], sv['lse_t'], do_a,
                                    reduce=reduce)

    def qkv_b(i, ps, c, s1, s2, dq_, dk_, dv_, qag, wqb, kvag, wk, wv, qng, kng):
        c_q = ps[:, :Q_LORA]
        c_kv = ps[:, Q_LORA:Q_LORA + KV_LORA]
        kr = ps[:, Q_LORA + KV_LORA:]
        cqn, vjp_cq = jax.vjp(lambda a, b: _rms(a, b, Q_LORA), c_q, qag)
        ckvn, vjp_ckv = jax.vjp(lambda a, b: _rms(a, b, KV_LORA), c_kv, kvag)
        q_raw = _mm(cqn, wqb)
        k_raw = _mm(ckvn, wk) + jnp.concatenate([kr] * MLA_HEADS, axis=-1)
        _, vjp_qn = jax.vjp(lambda a, b: _head_rms(a, b, MLA_HEADS, D_QK), q_raw, qng)
        _, vjp_kn = jax.vjp(lambda a, b: _head_rms(a, b, MLA_HEADS, D_QK), k_raw, kng)
        dq_raw, dqng = vjp_qn(_heads(_rope_t, jnp.transpose(dq_[0]), MLA_HEADS, c, s1, s2))
        dk_raw, dkng = vjp_kn(_heads(_rope_t, dk_, MLA_HEADS, c, s1, s2))
        dkr = dk_raw[:, :LANE]
        for h in range(1, MLA_HEADS):
            dkr = dkr + dk_raw[:, h * LANE:(h + 1) * LANE]
        dcq, dqag = vjp_cq(_mm_nt(dq_raw, wqb))
        dckv, dkvag = vjp_ckv(_mm_nt(dk_raw, wk) + _mm_nt(dv_, wv))
        dps = jnp.concatenate([dcq, dckv, dkr], axis=-1)
        return (dps, _mm_tn(cqn, dq_raw), _mm_tn(ckvn, dk_raw), _mm_tn(ckvn, dv_), dqag, dkvag, dqng, dkng)

    qkv_consts = [p['q_a_norm_g'], p['w_qb'], p['kv_a_norm_g'], p['w_k'], p['w_v'], p['q_norm_g'], p['k_norm_g']]
    (dp_s, g['w_qb'], g['w_k'], g['w_v'], g['q_a_norm_g'], g['kv_a_norm_g'], g['q_norm_g'], g['k_norm_g']) = _rows(
        qkv_b, name=name + "_b_qkv", n=nt, ins=[sv['p_s'], *tabs, dq_t, dk, dv, *qkv_consts],
        in_specs=[_rt(tm, SMALL_W)] + [_rt(tm, LANE)] * 3
        + [pl.BlockSpec((1, MLA_PAD, tm), lambda i: (i // (dq_t.shape[2] // tm), 0, i % (dq_t.shape[2] // tm)))]
        + [_rt(tm, MLA_PAD)] * 2 + [_full(a.shape) for a in qkv_consts],
        outs=[_sds((l, SMALL_W)), _sds((Q_LORA, MLA_PAD)), _sds((KV_LORA, MLA_PAD)), _sds((KV_LORA, MLA_PAD)),
              _sds((1, Q_LORA)), _sds((1, KV_LORA)), _sds((1, LANE)), _sds((1, LANE))],
        out_specs=[_rt(tm, SMALL_W), _full((Q_LORA, MLA_PAD)), _full((KV_LORA, MLA_PAD)), _full((KV_LORA, MLA_PAD)),
                   _full((1, Q_LORA)), _full((1, KV_LORA)), _full((1, LANE)), _full((1, LANE))], n_acc=7)

    x0 = sv['x0']
    dh = _matmul(name + "_b_in", [(dp_g, p['w_g']), (dp_u, p['w_u']), (dp_xq, p['w_xq']), (dp_s, p['w_s'])], l, D_MODEL, nt=True,
                 tm=256)
    gm = p['norm_mix_g']
    g['w_g'] = _matmul_tn(name + "_gw_g", x0, dp_g, rms_gain=gm)
    g['w_u'] = _matmul_tn(name + "_gw_u", x0, dp_u, rms_gain=gm)
    g['w_xq'] = _matmul_tn(name + "_gw_xq", x0, dp_xq, rms_gain=gm)
    g['w_s'] = _matmul_tn(name + "_gw_s", x0, dp_s, rms_gain=gm)
    dx0, g['norm_mix_g'] = _rows(norm_b, name=name + "_b_norm1", n=nt, ins=[x0, dh, dx1, gm],
                                 in_specs=[_rt(tm, D_MODEL)] * 3 + [_full((1, D_MODEL))],
                                 outs=[_sds((l, D_MODEL)), _sds((1, D_MODEL))], out_specs=[_rt(tm, D_MODEL), _full((1, D_MODEL))],
                                 n_acc=1)
    return dx0, g, rode


def _unprep_grads(g):
    o = {}
    ws = g['w_s']
    o['w_in'] = jnp.concatenate([ws[:, :Q_LORA + KV_LORA], ws[:, Q_LORA + KV_LORA + D_NOPE:Q_LORA + KV_LORA + D_QK],
                                 g['w_u'], g['w_xq'], g['w_g']], axis=1)
    o['w_q_b'] = g['w_qb'].reshape(Q_LORA, MLA_HEADS, HEAD_PAD)[:, :, :D_QK].reshape(Q_LORA, MLA_HEADS * D_QK)
    gk = g['w_k'].reshape(KV_LORA, MLA_HEADS, HEAD_PAD)[:, :, :D_NOPE]
    gv = g['w_v'].reshape(KV_LORA, MLA_HEADS, HEAD_PAD)[:, :, :D_V]
    o['w_kv_b'] = jnp.concatenate([gk, gv], axis=2).reshape(KV_LORA, MLA_HEADS * (D_NOPE + D_V))
    o['w_o_mla'] = g['w_oa'].reshape(MLA_HEADS, HEAD_PAD, D_MODEL)[:, :D_V].reshape(MLA_HEADS * D_V, D_MODEL)
    for n in ('w_glu', 'w_o_ssm', 'w_mem_kv', 'w_o_cross', 'w_out', 'w_up', 'w_down', 'conv_w'):
        o[n] = g[n]
    for n in ('norm_mix_g', 'q_a_norm_g', 'kv_a_norm_g', 'b_glu', 'mem_norm_g', 'xq_norm_g', 'xk_norm_g', 'b_gate',
              'norm_ffn_g', 'conv_b'):
        o[n] = g[n].reshape(-1)
    o['q_norm_g'] = g['q_norm_g'].reshape(-1)[:D_QK]
    o['k_norm_g'] = g['k_norm_g'].reshape(-1)[:D_QK]
    o['ssm_d'] = g['ssm_d'].reshape(SSM_GROUPS, SSM_GROUP_CH)
    o['ssm_lambda_re'] = g['lr']
    o['ssm_lambda_im'] = g['li']
    o['ssm_log_dt'] = g['log_dt'].reshape(SSM_GROUPS)
    o['ssm_b_re'] = g['br'].transpose(1, 2, 0)
    o['ssm_b_im'] = g['bi'].transpose(1, 2, 0)
    dc = g['c_mat']
    o['ssm_c_re'] = _blockdiag_t(dc[:, :512], SSM_STATE, SSM_GROUP_CH).transpose(0, 1, 3, 2).reshape(SSM_GROUPS, SSM_GROUP_CH, SSM_STATE)
    o['ssm_c_im'] = -_blockdiag_t(dc[:, 512:], SSM_STATE, SSM_GROUP_CH).transpose(0, 1, 3, 2).reshape(SSM_GROUPS, SSM_GROUP_CH, SSM_STATE)
    return o


def _local_step(x, mem, pos, target, w, late_gather=None, early_reduce=None):
    l = x.shape[0]
    tm = min(512, l)
    tabs = _rope_tables(pos.astype(F32).reshape(l, 1))
    saved = []
    ps = []
    h = x
    for i in range(DEPTH):
        ps.append(_prep_layer(w, i))
        riding = late_gather[:2] if (late_gather is not None and i == 0) else None
        h, sv = _layer_fwd("l%d" % i, h, tabs, mem, ps[i], gather=riding)
        if riding is not None:
            for n, v in late_gather[2](sv.pop('gathered')).items():
                w[n][late_gather[1]] = v
        saved.append(sv)

    def loss_fn(i, y, t):
        e = y - t
        per_tok = jnp.sum(e * e, axis=-1, keepdims=True) * (1.0 / D_MODEL)
        tot = 0.5 * jnp.sum(per_tok, axis=0, keepdims=True)
        return e * (1.0 / D_MODEL), jnp.broadcast_to(tot, (1, LANE))

    dy, loss = _rows(loss_fn, name="loss", n=l // tm, ins=[h, target], in_specs=[_rt(tm, D_MODEL)] * 2,
                     outs=[_sds((l, D_MODEL)), _sds((1, LANE))], out_specs=[_rt(tm, D_MODEL), _full((1, LANE))], n_acc=1)
    grads = []
    d = dy
    riding, rode = None, []
    for i in reversed(range(DEPTH)):
        d, g, got = _layer_bwd("l%d" % i, d, saved[i], tabs, mem, ps[i], reduce=riding)
        rode = got or rode
        grads.append(_unprep_grads(g))
        riding = (early_reduce(grads[-1]), DEPTH - 1) if (early_reduce is not None and i == DEPTH - 1) else None
    return loss[0, 0], d, grads[::-1], rode


def _sum_picked(name, slots, pick, extra, out_dtype):
    _, r, c = slots.shape
    e = extra.shape[0]
    tr = _row_tile(r)

    def body(pk, s_ref, x_ref, o_ref):
        acc = s_ref[...].astype(F32)
        for k in range(e):
            acc = acc + x_ref[k].astype(F32)
        o_ref[...] = acc.astype(o_ref.dtype)

    grid_spec = pltpu.PrefetchScalarGridSpec(
        num_scalar_prefetch=1, grid=(r // tr,),
        in_specs=[pl.BlockSpec((None, tr, c), lambda i, pk: (pk[0], i, 0)), pl.BlockSpec((e, tr, c), lambda i, pk: (0, i, 0))],
        out_specs=pl.BlockSpec((tr, c), lambda i, pk: (i, 0)))
    return pl.pallas_call(body, name=name, grid_spec=grid_spec, out_shape=_sds((r, c), out_dtype),
                          compiler_params=_params(("arbitrary",), 48))(pick, slots, extra)


def _row_tile(r):
    for t in (256, 128, 64, 32, 16, 8):
        if r % t == 0:
            return t
    return r


def _adamw(name, parts, w, m, v):
    r, cw = w.shape
    tr = _row_tile(r)
    np_ = len(parts)

    def fn(i, *vals):
        wv, mv, vv = vals[np_:]
        terms = []
        for pv in vals[:np_]:
            terms += [pv] if pv.ndim == 2 else [pv[k] for k in range(pv.shape[0])]
        g = terms[0]
        for t in terms[1:]:
            g = g + t
        mn = ADAM_B1 * mv + (1.0 - ADAM_B1) * g
        vn = ADAM_B2 * vv + (1.0 - ADAM_B2) * (g * g)
        m_hat = mn / (1.0 - ADAM_B1 ** ADAM_STEP)
        v_hat = vn / (1.0 - ADAM_B2 ** ADAM_STEP)
        delta = -ADAM_LR * (m_hat / (jnp.sqrt(v_hat) + ADAM_EPS) + ADAM_WD * wv)
        return g, delta, mn, vn

    pspecs = [_rt(tr, cw) if p.ndim == 2 else pl.BlockSpec((p.shape[0], tr, cw), lambda i: (0, i, 0)) for p in parts]
    return _rows(fn, name=name, n=r // tr, ins=[*parts, w, m, v], in_specs=pspecs + [_rt(tr, cw)] * 3,
                 outs=[_sds((r, cw))] * 4, out_specs=[_rt(tr, cw)] * 4)


def _shard_of(a, axis, k):
    n = a.shape[axis] // 4
    return lax.slice_in_dim(a, k * n, (k + 1) * n, axis=axis)


def _step(a):
    x = a['x'][0]
    mem = a['mem'][0]
    pos = a['positions'][0]
    target = a['loss_target'][0]

    me = 2 * lax.axis_index("x") + lax.axis_index("y")

    mine = [a[n] if n == 'conv_w' else a[n].astype(BF16) for n in SHARDED]

    def assemble(bufs, layer):
        return {n: jnp.concatenate([jnp.where(me == k, own[layer], y[k]) for k in range(4)], axis=SHARD_AXIS[n] - 1)
                for n, own, y in zip(SHARDED, mine, bufs)}

    w = {n: [v, None] for n, v in assemble(_layer_gather("comm_gather_l0", mine, 0), 0).items()}
    for n in SMALL:
        w[n] = a[n]

    mc = lax.axis_index("c")
    me1 = me.astype(jnp.int32).reshape(1)
    zero1i = jnp.zeros((1,), jnp.int32)

    def pair_sums(layer, gl):
        mine_l = [jnp.stack([_shard_of(gl[n], SHARD_AXIS[n] - 1, k) for k in range(4)], axis=0).astype(BF16) for n in SHARDED]
        per_core = [None, None]
        per_core[1 - layer] = mine_l
        theirs = _send_d2d("comm_pair_l%d" % layer, per_core)
        out = []
        for n, g, s in zip(SHARDED, mine_l, theirs):
            cols = g.shape[-1]
            out.append(_sum_picked("sum2_l%d_%s" % (layer, n), g.reshape(1, -1, cols), zero1i, s.reshape(1, -1, cols), BF16)
                       .reshape(g.shape))
        return out

    pairs = [None, None]

    def early_reduce(gl):
        pairs[1] = pair_sums(1, gl)
        return pairs[1]

    loss, grad_x, grads, got1 = _local_step(x, mem, pos, target, w, late_gather=(mine, 1, lambda bufs: assemble(bufs, 1)),
                                            early_reduce=early_reduce)
    pairs[0] = pair_sums(0, grads[0])
    gsm = _pack([jnp.stack([grads[i][n] for i in range(DEPTH)], axis=0) for n in SMALL] + [loss.reshape(1)], 8, F32)
    got0, alls = _layer_reduce("comm_reduce_l0", pairs[0], 0, gsm)
    done = [[_sum_picked("sum4_l%d_%s" % (layer, n), p4, me1, g3, F32) for n, p4, g3 in zip(SHARDED, pairs[layer], got)]
            for layer, got in ((0, got0), (1, got1))]
    others = _send_d2d("comm_reduce_d2d", done)
    res_sh = []
    for n, d0, d1, other in zip(SHARDED, done[0], done[1], others):
        cols = d0.shape[-1]
        full = jnp.where(mc == 0, jnp.stack([d0, other], axis=0), jnp.stack([other, d1], axis=0))
        res = _adamw("adamw_" + n, [full.reshape(-1, cols)], *[a[pre + n].reshape(-1, cols) for pre in ('', 'm_', 'v_')])
        res_sh.append([r.reshape(a[n].shape) for r in res])
    res_sh = [[res_sh[j][kind] for j in range(len(SHARDED))] for kind in range(4)]

    sm_shapes = [a[n].shape for n in SMALL] + [(1,)]
    zero1 = jnp.zeros((1,), F32)
    res_sm = _adamw("adamw_small", [alls], *[_pack([a[pre + n] for n in SMALL] + [zero1], 8, F32) for pre in ('', 'm_', 'v_')])
    res_sm = [_unpack(r, sm_shapes) for r in res_sm]
    loss = res_sm[0][-1][0]

    outs = [loss, grad_x[None]]
    for kind in range(4):
        byname = dict(zip(SHARDED, res_sh[kind]))
        byname.update(zip(SMALL, res_sm[kind]))
        outs += [byname[n] for n in WEIGHTS]
    return tuple(outs)


def kernel(x, mem, positions, norm_mix_g, w_in, q_a_norm_g, w_q_b, kv_a_norm_g, w_kv_b, q_norm_g, k_norm_g, w_o_mla, ssm_lambda_re, ssm_lambda_im, ssm_log_dt, ssm_b_re, ssm_b_im, ssm_c_re, ssm_c_im, ssm_d, w_glu, b_glu, w_o_ssm, mem_norm_g, w_mem_kv, xq_norm_g, xk_norm_g, w_o_cross, b_gate, w_out, norm_ffn_g, w_up, conv_w, conv_b, w_down, loss_target, m_norm_mix_g, m_w_in, m_q_a_norm_g, m_w_q_b, m_kv_a_norm_g, m_w_kv_b, m_q_norm_g, m_k_norm_g, m_w_o_mla, m_ssm_lambda_re, m_ssm_lambda_im, m_ssm_log_dt, m_ssm_b_re, m_ssm_b_im, m_ssm_c_re, m_ssm_c_im, m_ssm_d, m_w_glu, m_b_glu, m_w_o_ssm, m_mem_norm_g, m_w_mem_kv, m_xq_norm_g, m_xk_norm_g, m_w_o_cross, m_b_gate, m_w_out, m_norm_ffn_g, m_w_up, m_conv_w, m_conv_b, m_w_down, v_norm_mix_g, v_w_in, v_q_a_norm_g, v_w_q_b, v_kv_a_norm_g, v_w_kv_b, v_q_norm_g, v_k_norm_g, v_w_o_mla, v_ssm_lambda_re, v_ssm_lambda_im, v_ssm_log_dt, v_ssm_b_re, v_ssm_b_im, v_ssm_c_re, v_ssm_c_im, v_ssm_d, v_w_glu, v_b_glu, v_w_o_ssm, v_mem_norm_g, v_w_mem_kv, v_xq_norm_g, v_xk_norm_g, v_w_o_cross, v_b_gate, v_w_out, v_norm_ffn_g, v_w_up, v_conv_w, v_conv_b, v_w_down):
    return _step(dict(locals()))
```

```python
import math

import numpy as np
import jax
import jax.numpy as jnp
from jax import lax
from jax.experimental import pallas as pl
from jax.experimental.pallas import tpu as pltpu

F32 = jnp.float32
BF16 = jnp.bfloat16
MESH = pl.DeviceIdType.MESH

DEPTH = 2
D_MODEL = 1024
EPS = 1e-6
MLA_HEADS = 8
Q_LORA = 384
KV_LORA = 256
D_NOPE = 64
D_ROPE = 32
D_QK = D_NOPE + D_ROPE
D_V = 64
HEAD_PAD = 128
MLA_PAD = MLA_HEADS * HEAD_PAD
ROPE_THETA = 10000.0
SSM_GROUPS = 32
SSM_GROUP_CH = 16
SSM_WIDTH = 512
SSM_STATE = 64
SSM_LANES = SSM_GROUPS * SSM_STATE
SSM_JB = 4
X_HEADS = 4
X_HEAD_DIM = 128
X_WIDTH = 512
D_FF = 2816
SMALL_W = Q_LORA + KV_LORA + HEAD_PAD
SCAN_SEGS = 64
SSM_STATE_DTYPE = BF16
LANE = 128
NEG = -1e30

ADAM_LR = 0.001
ADAM_B1 = 0.9
ADAM_B2 = 0.999
ADAM_EPS = 1e-08
ADAM_WD = 0.01
ADAM_STEP = 10

WEIGHTS = ['norm_mix_g', 'w_in', 'q_a_norm_g', 'w_q_b', 'kv_a_norm_g', 'w_kv_b', 'q_norm_g', 'k_norm_g', 'w_o_mla',
           'ssm_lambda_re', 'ssm_lambda_im', 'ssm_log_dt', 'ssm_b_re', 'ssm_b_im', 'ssm_c_re', 'ssm_c_im', 'ssm_d',
           'w_glu', 'b_glu', 'w_o_ssm', 'mem_norm_g', 'w_mem_kv', 'xq_norm_g', 'xk_norm_g', 'w_o_cross', 'b_gate',
           'w_out', 'norm_ffn_g', 'w_up', 'conv_w', 'conv_b', 'w_down']
SHARD_AXIS = {'w_in': 2, 'w_q_b': 2, 'w_kv_b': 2, 'w_o_mla': 2, 'w_glu': 1, 'w_o_ssm': 2, 'w_mem_kv': 1,
              'w_o_cross': 2, 'w_out': 1, 'w_up': 2, 'conv_w': 2, 'w_down': 1}
SHARDED = [n for n in WEIGHTS if n in SHARD_AXIS]
SMALL = [n for n in WEIGHTS if n not in SHARD_AXIS]


def _bf(v):
    return v.astype(BF16)


def _mm(a, b):
    return jnp.dot(_bf(a), _bf(b), preferred_element_type=F32)


def _mm_nt(a, b):
    return lax.dot_general(_bf(a), _bf(b), (((1,), (1,)), ((), ())), preferred_element_type=F32)


def _mm_tn(a, b):
    return lax.dot_general(_bf(a), _bf(b), (((0,), (0,)), ((), ())), preferred_element_type=F32)


def _rms(v, g, n):
    ms = jnp.sum(v * v, axis=-1, keepdims=True) * (1.0 / n)
    return (v * lax.rsqrt(ms + EPS)) * g


def _head_rms(v, g, heads, n):
    return jnp.concatenate([_rms(v[:, h * LANE:(h + 1) * LANE], g, n) for h in range(heads)], axis=-1)


def _rope(v, c, s1, s2):
    return v * c + pltpu.roll(v, LANE - 16, 1) * s1 + pltpu.roll(v, 16, 1) * s2


def _rope_t(g, c, s1, s2):
    return g * c + pltpu.roll(g * s1, 16, 1) + pltpu.roll(g * s2, LANE - 16, 1)


def _heads(fn, v, heads, *tabs):
    return jnp.concatenate([fn(v[:, h * LANE:(h + 1) * LANE], *tabs) for h in range(heads)], axis=-1)


def _gelu(y):
    return y * (0.5 * (1.0 + jnp.tanh(math.sqrt(2.0 / math.pi) * (y + 0.044715 * (y * y * y)))))


def _silu(g):
    return g * jax.nn.sigmoid(g)


def _colsum(v):
    return jnp.sum(v, axis=0, keepdims=True)


def _row_select(rows, n):
    rid = lax.broadcasted_iota(jnp.int32, (n, rows[0].shape[-1]), 0)
    out = jnp.zeros((n, rows[0].shape[-1]), F32)
    for k, r in enumerate(rows):
        out = jnp.where(rid == k, jnp.broadcast_to(r, out.shape), out)
    return out


def _params(sem, vmem_mb):
    return pltpu.CompilerParams(dimension_semantics=sem, vmem_limit_bytes=vmem_mb * 1024 * 1024)


def _rt(tm, w, cb=0):
    return pl.BlockSpec((tm, w), lambda i: (i, cb))


def _full(shape):
    nd = len(shape)
    return pl.BlockSpec(tuple(shape), lambda i: (0,) * nd)


def _rows(fn, *, name, n, ins, in_specs, outs, out_specs, n_acc=0, vmem=48):
    n_in = len(ins)
    n_out = len(outs)

    def body(*refs):
        i = pl.program_id(0)
        res = fn(i, *[r[...] for r in refs[:n_in]])
        if not isinstance(res, (tuple, list)):
            res = (res,)
        assert len(res) == n_out, (name, len(res), n_out)
        for k, (r, v) in enumerate(zip(refs[n_in:], res)):
            if k < n_out - n_acc:
                r[...] = v.astype(r.dtype)
            else:
                @pl.when(i == 0)
                def _():
                    r[...] = v

                @pl.when(i > 0)
                def _():
                    r[...] += v

    return pl.pallas_call(
        body, name=name, grid=(n,), in_specs=list(in_specs), out_specs=tuple(out_specs), out_shape=tuple(outs),
        compiler_params=_params(("arbitrary",), vmem))(*ins)


def _sds(shape, dtype=F32):
    return jax.ShapeDtypeStruct(tuple(shape), dtype)


def _tile_n(n, cap=1536):
    best = None
    for t in range(LANE, min(n, cap) + 1, LANE):
        if n % t == 0:
            best = t
    if best is None or n <= 1408:
        return n
    return best


def _matmul(name, pairs, m, n, *, nt=False, rms_gain=None, resid=None, out_dtype=F32, tm=1024, vmem=56):
    tm = min(tm, m)
    tn = _tile_n(n)
    ks = [a.shape[1] for a, _ in pairs]
    np_ = len(pairs)

    def body(*refs):
        a_refs = refs[:np_]
        b_refs = refs[np_:2 * np_]
        k = 2 * np_
        g_ref = None
        r_ref = None
        if rms_gain is not None:
            g_ref = refs[k]
            k += 1
        if resid is not None:
            r_ref = refs[k]
            k += 1
        o_ref = refs[k]
        scr = refs[k + 1:]
        j = pl.program_id(1)

        @pl.when(j == 0)
        def _():
            for p in range(np_):
                a = a_refs[p][...]
                if p == 0 and g_ref is not None:
                    a = _rms(a.astype(F32), g_ref[...], ks[0])
                scr[p][...] = a.astype(BF16)

        acc = None
        for p in range(np_):
            b = b_refs[p][...].astype(BF16)
            if nt:
                t = lax.dot_general(scr[p][...], b, (((1,), (1,)), ((), ())), preferred_element_type=F32)
            else:
                t = jnp.dot(scr[p][...], b, preferred_element_type=F32)
            acc = t if acc is None else acc + t
        if r_ref is not None:
            acc = acc + r_ref[...]
        o_ref[...] = acc.astype(o_ref.dtype)

    in_specs = [pl.BlockSpec((tm, kk), lambda i, j: (i, 0)) for kk in ks]
    if nt:
        in_specs += [pl.BlockSpec((tn, kk), lambda i, j: (j, 0)) for kk in ks]
    else:
        in_specs += [pl.BlockSpec((kk, tn), lambda i, j: (0, j)) for kk in ks]
    ins = [a for a, _ in pairs] + [b for _, b in pairs]
    if rms_gain is not None:
        in_specs.append(pl.BlockSpec((1, ks[0]), lambda i, j: (0, 0)))
        ins.append(rms_gain)
    if resid is not None:
        in_specs.append(pl.BlockSpec((tm, tn), lambda i, j: (i, j)))
        ins.append(resid)
    return pl.pallas_call(
        body, name=name, grid=(m // tm, n // tn), in_specs=in_specs,
        out_specs=pl.BlockSpec((tm, tn), lambda i, j: (i, j)), out_shape=_sds((m, n), out_dtype),
        scratch_shapes=[pltpu.VMEM((tm, kk), BF16) for kk in ks],
        compiler_params=_params(("arbitrary", "arbitrary"), vmem))(*ins)


def _matmul_tn(name, a, b, *, rms_gain=None, tl=1024, vmem=56):
    l, ka = a.shape
    n = b.shape[1]
    tl = min(tl, l)
    tn = _tile_n(n, 1536)

    def body(*refs):
        if rms_gain is not None:
            a_ref, b_ref, g_ref, o_ref = refs
        else:
            a_ref, b_ref, o_ref = refs
        t = pl.program_id(1)
        av = a_ref[...]
        if rms_gain is not None:
            av = _rms(av.astype(F32), g_ref[...], ka)
        v = _mm_tn(av, b_ref[...])

        @pl.when(t == 0)
        def _():
            o_ref[...] = v

        @pl.when(t > 0)
        def _():
            o_ref[...] += v

    in_specs = [pl.BlockSpec((tl, ka), lambda j, t: (t, 0)), pl.BlockSpec((tl, tn), lambda j, t: (t, j))]
    ins = [a, b]
    if rms_gain is not None:
        in_specs.append(pl.BlockSpec((1, ka), lambda j, t: (0, 0)))
        ins.append(rms_gain)
    return pl.pallas_call(
        body, name=name, grid=(n // tn, l // tl), in_specs=in_specs,
        out_specs=pl.BlockSpec((ka, tn), lambda j, t: (0, j)), out_shape=_sds((ka, n)),
        compiler_params=_params(("arbitrary", "arbitrary"), vmem))(*ins)


ATT_HEADS_PER_STEP = 1
ATT_W = ATT_HEADS_PER_STEP * LANE
ATT_GROUPS = MLA_HEADS // ATT_HEADS_PER_STEP
LOG2E = math.log2(math.e)
ATT_FWD_TILE = 1024
ATT_BWD_TILE = 1024
ATT_BWD_HEADS = 1
ATT_SCALE = D_QK ** -0.5
ATT_QSCALE = ATT_SCALE * LOG2E


def _tri_tables(nq, by_k):
    qs, ks = [], []
    if by_k:
        for ki in range(nq):
            for qi in range(ki, nq):
                qs.append(qi)
                ks.append(ki)
    else:
        for qi in range(nq):
            for ki in range(qi + 1):
                qs.append(qi)
                ks.append(ki)
    return jnp.asarray(np.array(qs, np.int32)), jnp.asarray(np.array(ks, np.int32))


def _causal_keep(shape, transposed):
    r = lax.broadcasted_iota(jnp.int32, shape, 0)
    c = lax.broadcasted_iota(jnp.int32, shape, 1)
    return (r <= c) if transposed else (c <= r)


def _nt16(a, b):
    return lax.dot_general(a, b, (((1,), (1,)), ((), ())), preferred_element_type=F32)


def _row_form(col):
    return jnp.transpose(jnp.broadcast_to(col, (col.shape[0], LANE)))[:8]


def _att_call(body, name, l, tq, tabs, ins, in_specs, outs, out_specs, scratch=(), groups=ATT_GROUPS, vmem=48):
    grid_spec = pltpu.PrefetchScalarGridSpec(
        num_scalar_prefetch=2, grid=(groups, tabs[0].shape[0]), in_specs=in_specs, out_specs=out_specs,
        scratch_shapes=list(scratch))
    return pl.pallas_call(body, name=name, grid_spec=grid_spec, out_shape=outs,
                          compiler_params=_params(("arbitrary", "arbitrary"), vmem))(*tabs, *ins)


def _flash_fwd(name, q, k, v_t, gather=None):
    l = q.shape[0]
    tq = min(ATT_FWD_TILE, l)
    nq = l // tq
    tabs = _tri_tables(nq, by_k=False)
    gx, glayer = gather if gather is not None else ([], 0)
    ng = len(gx)
    n_steps = int(tabs[0].shape[0])

    def body(qt, kt, *refs):
        q_ref, k_ref, vt_ref = refs[:3]
        gx_refs = refs[3:3 + ng]
        o_ref, lset_ref = refs[3 + ng:5 + ng]
        gy_refs = refs[5 + ng:5 + 2 * ng]
        m_s, acc_s = refs[5 + 2 * ng:7 + 2 * ng]
        t = pl.program_id(1)
        qi = qt[t]
        ki = kt[t]
        sls = [slice(h * LANE, (h + 1) * LANE) for h in range(ATT_HEADS_PER_STEP)]
        if ng:
            g_start, g_finish = _layer_gather_steps(gx_refs, gy_refs, refs[7 + 2 * ng:9 + 2 * ng], glayer)

            @pl.when((pl.program_id(0) == 0) & (t == 0))
            def _():
                g_start()

        @pl.when(ki == 0)
        def _():
            m_s[...] = jnp.full(m_s.shape, NEG, F32)
            acc_s[...] = jnp.zeros(acc_s.shape, F32)

        def step(masked):
            sts = [_nt16(k_ref[:, sl], q_ref[:, sl]) for sl in sls]
            for h, sl in enumerate(sls):
                st = sts[h]
                if masked:
                    st = jnp.where(_causal_keep(st.shape, True), st, NEG)
                m_old = m_s[h][:1]
                m_new = jnp.maximum(m_old, jnp.max(st, axis=0, keepdims=True))
                alpha = jnp.exp2(m_old - m_new)
                pt = jnp.exp2(st - m_new).astype(BF16)
                acc_s[sl, :] = alpha * acc_s[sl, :] + jnp.dot(vt_ref[sl, :], pt, preferred_element_type=F32)
                m_s[h] = jnp.broadcast_to(m_new, (8, tq))

        @pl.when(ki < qi)
        def _():
            step(False)

        @pl.when(ki == qi)
        def _():
            step(True)
            row = lax.broadcasted_iota(jnp.int32, (LANE, tq), 0)
            for h, sl in enumerate(sls):
                acc = acc_s[sl, :]
                lsum = acc[D_V:D_V + 1, :]
                o_ref[:, sl] = jnp.transpose(jnp.where(row < D_V, acc / lsum, 0.0))
                lset_ref[h * 8:(h + 1) * 8, :] = m_s[h] + jnp.log2(lsum)

        if ng:
            @pl.when((pl.program_id(0) == ATT_GROUPS - 1) & (t == n_steps - 1))
            def _():
                g_finish()

    qspec = pl.BlockSpec((tq, ATT_W), lambda g, t, qt, kt: (qt[t], g))
    kspec = pl.BlockSpec((tq, ATT_W), lambda g, t, qt, kt: (kt[t], g))
    vspec = pl.BlockSpec((ATT_W, tq), lambda g, t, qt, kt: (g, kt[t]))
    rspec = pl.BlockSpec((8 * ATT_HEADS_PER_STEP, tq), lambda g, t, qt, kt: (g, qt[t]))
    res = _att_call(
        body, name, l, tq, tabs, [q, k, v_t, *gx], [qspec, kspec, vspec] + [ANY] * ng,
        (_sds((l, MLA_PAD)), _sds((8 * MLA_HEADS, l)), *[_sds((4,) + x.shape[1:], x.dtype) for x in gx]),
        (qspec, rspec, *([ANY] * ng)),
        scratch=[pltpu.VMEM((ATT_HEADS_PER_STEP, 8, tq), F32), pltpu.VMEM((ATT_W, tq), F32)]
        + ([pltpu.SemaphoreType.DMA((6 * ng,)), pltpu.SemaphoreType.DMA((6 * ng,))] if ng else []))
    return res[0], res[1], list(res[2:])


def _flash_bwd(name, q, k, v, k_t, o, lse_t, do, reduce=None):
    l = q.shape[0]
    tq = min(ATT_BWD_TILE, l)
    nq = l // tq
    hb = ATT_BWD_HEADS
    wb = hb * LANE

    def delta_fn(i, dov, ov):
        rows = []
        for h in range(MLA_HEADS):
            sl = slice(h * LANE, (h + 1) * LANE)
            rows.append(_row_form(jnp.sum(dov[:, sl] * ov[:, sl], axis=-1, keepdims=True)))
        return jnp.concatenate(rows, axis=0), dov

    delta_t, do16 = _rows(
        delta_fn, name=name + "_delta", n=nq, ins=[do, o], in_specs=[_rt(tq, MLA_PAD)] * 2,
        outs=[_sds((8 * MLA_HEADS, l)), _sds((l, MLA_PAD), BF16)],
        out_specs=[pl.BlockSpec((8 * MLA_HEADS, tq), lambda i: (0, i)), _rt(tq, MLA_PAD)])

    rg, rlayer = reduce if reduce is not None else ([], 0)
    nr = len(rg)
    groups = MLA_HEADS // hb
    tabs_k = _tri_tables(nq, by_k=True)
    n_steps = int(tabs_k[0].shape[0])

    def body(qt, kt, *refs):
        q_ref, k_ref, v_ref, do_ref, kt_ref, lset_ref, dlt_ref = refs[:7]
        rg_refs = refs[7:7 + nr]
        dk_ref, dv_ref, dqt_ref = refs[7 + nr:10 + nr]
        ry_refs = refs[10 + nr:10 + 2 * nr]
        t = pl.program_id(1)
        qi = qt[t]
        ki = kt[t]
        sls = [slice(h * LANE, (h + 1) * LANE) for h in range(hb)]
        if nr:
            r_start, r_finish = _layer_reduce_steps(rg_refs, ry_refs, refs[10 + 2 * nr:12 + 2 * nr], rlayer)

            @pl.when((pl.program_id(0) == 0) & (t == 0))
            def _():
                r_start()

        @pl.when(ki == 0)
        def _():
            dqt_ref[qi] = jnp.zeros((wb, tq), F32)

        def step(masked):
            sts = [_nt16(k_ref[:, sl], q_ref[:, sl]) for sl in sls]
            dpts = [_nt16(v_ref[:, sl], do_ref[:, sl]) for sl in sls]
            for h, sl in enumerate(sls):
                st = sts[h]
                if masked:
                    st = jnp.where(_causal_keep(st.shape, True), st, NEG)
                pt = jnp.exp2(st - lset_ref[h * 8:(h + 1) * 8, :][:1])
                dst = (pt * (dpts[h] - dlt_ref[h * 8:(h + 1) * 8, :][:1])).astype(BF16)
                dv_ref[:, sl] += jnp.dot(pt.astype(BF16), do_ref[:, sl], preferred_element_type=F32)
                dk_ref[:, sl] += jnp.dot(dst, q_ref[:, sl], preferred_element_type=F32)
                dqt_ref[qi, sl, :] += jnp.dot(kt_ref[sl, :], dst, preferred_element_type=F32)

        @pl.when(qi == ki)
        def _():
            dk_ref[...] = jnp.zeros(dk_ref.shape, F32)
            dv_ref[...] = jnp.zeros(dv_ref.shape, F32)
            step(True)
            dqt_ref[qi] = dqt_ref[qi] * ATT_SCALE

        @pl.when(qi > ki)
        def _():
            step(False)

        @pl.when(qi == nq - 1)
        def _():
            dk_ref[...] = dk_ref[...] * (1.0 / LOG2E)

        if nr:
            @pl.when((pl.program_id(0) == groups - 1) & (t == n_steps - 1))
            def _():
                r_finish()

    qspec = pl.BlockSpec((tq, wb), lambda g, t, qt, kt: (qt[t], g))
    kspec = pl.BlockSpec((tq, wb), lambda g, t, qt, kt: (kt[t], g))
    ktspec = pl.BlockSpec((wb, tq), lambda g, t, qt, kt: (g, kt[t]))
    rspec = pl.BlockSpec((8 * hb, tq), lambda g, t, qt, kt: (g, qt[t]))
    dqspec = pl.BlockSpec((nq, wb, tq), lambda g, t, qt, kt: (0, g, 0))
    res = _att_call(body, name + "_dqkv", l, tq, tabs_k, [q, k, v, do16, k_t, lse_t, delta_t, *rg],
                    [qspec, kspec, kspec, qspec, ktspec, rspec, rspec] + [ANY] * nr,
                    (_sds((l, MLA_PAD)), _sds((l, MLA_PAD)), _sds((nq, MLA_PAD, tq)),
                     *[_sds((3,) + g.shape[1:], g.dtype) for g in rg]),
                    (kspec, kspec, dqspec, *([ANY] * nr)),
                    scratch=([pltpu.SemaphoreType.DMA((3 * nr,)), pltpu.SemaphoreType.DMA((3 * nr,))] if nr else []),
                    groups=groups, vmem=56)
    dk, dv, dq_t = res[:3]
    return dq_t, dk, dv, list(res[3:])


def _cmul(ar, ai, br, bi):
    return ar * br - ai * bi, ar * bi + ai * br


def _scan(name, x_re, x_im, a_re, a_im, reverse):
    l, lanes = x_re.shape
    ns = SCAN_SEGS
    tl = l // ns
    steps = int(math.log2(tl))
    assert 2 ** steps == tl and tl * ns == l

    def body(xr_ref, xi_ref, ar_ref, ai_ref, sr_ref, si_ref):
        a_r1 = ar_ref[...]
        a_i1 = ai_ref[...]
        a_r = jnp.broadcast_to(a_r1, (ns, LANE))
        a_i = jnp.broadcast_to(a_i1, (ns, LANE))

        def rows(t):
            t = (tl - 1 - t) if reverse else t
            return pl.ds(pl.multiple_of(t * ns, ns), ns)

        def local(t, carry):
            cr, ci = carry
            r = rows(t)
            pr, pi = _cmul(a_r, a_i, cr, ci)
            return pr + xr_ref[r, :].astype(F32), pi + xi_ref[r, :].astype(F32)

        zero = jnp.zeros((ns, LANE), F32)
        e_r, e_i = lax.fori_loop(0, tl, local, (zero, zero), unroll=min(8, tl))
        p_r, p_i = a_r1, a_i1
        for _ in range(steps):
            p_r, p_i = _cmul(p_r, p_i, p_r, p_i)
        rid = lax.broadcasted_iota(jnp.int32, (ns, LANE), 0)
        c_r = jnp.zeros((1, LANE), F32)
        c_i = jnp.zeros((1, LANE), F32)
        in_r, in_i = zero, zero
        order = range(ns - 2, -1, -1) if reverse else range(1, ns)
        for kk in order:
            src = kk + 1 if reverse else kk - 1
            ek_r = jnp.sum(jnp.where(rid == src, e_r, 0.0), axis=0, keepdims=True)
            ek_i = jnp.sum(jnp.where(rid == src, e_i, 0.0), axis=0, keepdims=True)
            q_r, q_i = _cmul(p_r, p_i, c_r, c_i)
            c_r, c_i = q_r + ek_r, q_i + ek_i
            in_r = jnp.where(rid == kk, jnp.broadcast_to(c_r, (ns, LANE)), in_r)
            in_i = jnp.where(rid == kk, jnp.broadcast_to(c_i, (ns, LANE)), in_i)

        def final(t, carry):
            cr, ci = carry
            r = rows(t)
            pr, pi = _cmul(a_r, a_i, cr, ci)
            nr, ni = pr + xr_ref[r, :].astype(F32), pi + xi_ref[r, :].astype(F32)
            sr_ref[r, :] = nr.astype(sr_ref.dtype)
            si_ref[r, :] = ni.astype(si_ref.dtype)
            return nr, ni

        lax.fori_loop(0, tl, final, (in_r, in_i), unroll=min(8, tl))

    xs = pl.BlockSpec((l, LANE), lambda j: (0, j))
    as_ = pl.BlockSpec((1, LANE), lambda j: (0, j))
    return pl.pallas_call(
        body, name=name, grid=(lanes // LANE,), in_specs=[xs, xs, as_, as_], out_specs=(xs, xs),
        out_shape=(_sds((l, lanes), x_re.dtype), _sds((l, lanes), x_re.dtype)),
        compiler_params=_params(("arbitrary",), 48))(x_re, x_im, a_re, a_im)


ANY = pl.BlockSpec(memory_space=pl.ANY)


def _place():
    mx, my, mc = lax.axis_index("x"), lax.axis_index("y"), lax.axis_index("c")
    return mx, my, mc, [(1 - mx, my), (mx, 1 - my), (1 - mx, 1 - my)]


def _run_copies(copies):
    for cp in copies:
        cp.start()
    for cp in copies:
        cp.wait_recv()
    for cp in copies:
        cp.wait_send()


def _remote(src, dst, sems, k, dev):
    return pltpu.make_async_remote_copy(src_ref=src, dst_ref=dst, send_sem=sems[0].at[k], recv_sem=sems[1].at[k],
                                        device_id=dev, device_id_type=MESH)


def _copy_call(body, name, ins, outs, n_copies, aliases=None):
    return pl.pallas_call(
        body, name=name, in_specs=[ANY] * len(ins), out_specs=[ANY] * len(outs), out_shape=list(outs),
        input_output_aliases=aliases or {},
        scratch_shapes=[pltpu.SemaphoreType.DMA((n_copies,)), pltpu.SemaphoreType.DMA((n_copies,))])(*ins)


def _layer_gather_steps(x_refs, y_refs, sems, layer):
    n = len(x_refs)
    mx, my, mc, peers = _place()
    me = 2 * mx + my

    def ici(idx):
        i, j = divmod(idx, 3)
        return _remote(x_refs[i].at[layer], y_refs[i].at[me], sems, idx, (peers[j][0], peers[j][1], layer))

    def fwd(idx):
        i, j = divmod(idx, 3)
        pk = 2 * peers[j][0] + peers[j][1]
        return _remote(y_refs[i].at[pk], y_refs[i].at[pk], sems, 3 * n + idx, (mx, my, 1 - layer))

    def start():
        @pl.when(mc == layer)
        def _():
            for idx in range(3 * n):
                ici(idx).start()

    def finish():
        @pl.when(mc == layer)
        def _():
            for idx in range(3 * n):
                ici(idx).wait_recv()
                fwd(idx).start()
            for idx in range(3 * n):
                ici(idx).wait_send()
                fwd(idx).wait_send()

        @pl.when(mc != layer)
        def _():
            for idx in range(3 * n):
                fwd(idx).wait_recv()

    return start, finish


def _layer_gather(name, xs, layer):
    n = len(xs)

    def body(*refs):
        start, finish = _layer_gather_steps(refs[:n], refs[n:2 * n], refs[2 * n:], layer)
        start()
        finish()

    return _copy_call(body, name, xs, [_sds((4,) + x.shape[1:], x.dtype) for x in xs], 6 * n)


def _layer_reduce_steps(g_refs, y_refs, sems, layer):
    n = len(g_refs)
    mx, my, mc, peers = _place()

    def cp(idx):
        i, j = divmod(idx, 3)
        px, py = peers[j]
        return _remote(g_refs[i].at[2 * px + py], y_refs[i].at[j], sems, idx, (px, py, layer))

    def start():
        @pl.when(mc == layer)
        def _():
            for idx in range(3 * n):
                cp(idx).start()

    def finish():
        @pl.when(mc == layer)
        def _():
            for idx in range(3 * n):
                cp(idx).wait_recv()
            for idx in range(3 * n):
                cp(idx).wait_send()

    return start, finish


def _layer_reduce(name, gs, layer, small):
    n = len(gs)

    def body(*refs):
        g_refs, s_ref, y_refs, a_ref = refs[:n], refs[n], refs[n + 1:2 * n + 1], refs[2 * n + 1]
        sems, small_sems, local_sem = refs[2 * n + 2:2 * n + 4], refs[2 * n + 4:2 * n + 6], refs[2 * n + 6]
        mx, my, mc, _ = _place()
        me8 = 4 * mx + 2 * my + mc
        start, finish = _layer_reduce_steps(g_refs, y_refs, sems, layer)
        start()
        own = pltpu.make_async_copy(s_ref, a_ref.at[me8], local_sem)
        own.start()
        copies = []
        for j in range(1, 8):
            dev = ((1 - mx) if (j & 4) else mx, (1 - my) if (j & 2) else my, (1 - mc) if (j & 1) else mc)
            copies.append(_remote(s_ref, a_ref.at[me8], small_sems, j - 1, dev))
        _run_copies(copies)
        own.wait()
        finish()

    outs = [_sds((3,) + g.shape[1:], g.dtype) for g in gs] + [_sds((8,) + small.shape, small.dtype)]
    res = pl.pallas_call(
        body, name=name, in_specs=[ANY] * (n + 1), out_specs=[ANY] * (n + 1), out_shape=outs,
        scratch_shapes=[pltpu.SemaphoreType.DMA((3 * n,)), pltpu.SemaphoreType.DMA((3 * n,)),
                        pltpu.SemaphoreType.DMA((7,)), pltpu.SemaphoreType.DMA((7,)), pltpu.SemaphoreType.DMA])(*gs, small)
    return res[:n], res[n]


def _send_d2d(name, per_core):
    shapes = next(p for p in per_core if p is not None)
    n = len(shapes)
    ins = [x for p in per_core if p is not None for x in p]

    def body(*refs):
        o_refs, sems = refs[len(ins):len(ins) + n], refs[len(ins) + n:]
        mx, my, mc, _ = _place()
        off = 0
        for c in range(2):
            if per_core[c] is None:
                continue
            src = refs[off:off + n]
            off += n

            @pl.when(mc == c)
            def _():
                cps = [_remote(src[i], o_refs[i], sems, i, (mx, my, 1 - c)) for i in range(n)]
                for d in cps:
                    d.start()
                if per_core[1 - c] is not None:
                    for d in cps:
                        d.wait_recv()
                for d in cps:
                    d.wait_send()

            if per_core[1 - c] is None:
                @pl.when(mc != c)
                def _():
                    for i in range(n):
                        _remote(src[i], o_refs[i], sems, i, (mx, my, c)).wait_recv()

    return _copy_call(body, name, ins, [_sds(p.shape, p.dtype) for p in shapes], n)


PACK_W = 1024


def _pack(arrs, rows_multiple, dtype):
    flat = jnp.concatenate([a.reshape(-1).astype(dtype) for a in arrs])
    n = flat.shape[0]
    unit = PACK_W * rows_multiple
    tot = -(-n // unit) * unit
    flat = jnp.pad(flat, (0, tot - n))
    return flat.reshape(tot // PACK_W, PACK_W)


def _unpack(flat, shapes):
    flat = flat.reshape(-1)
    out = []
    off = 0
    for s in shapes:
        n = int(np.prod(s))
        out.append(flat[off:off + n].reshape(s))
        off += n
    return out


def _rope_tables(pos):
    l = pos.shape[0]
    tm = min(512, l)
    inv = (np.float32(ROPE_THETA) ** (-np.arange(0, D_ROPE, 2, dtype=np.float32) / np.float32(D_ROPE))).astype(np.float32)
    lane_f = np.zeros((1, LANE), np.float32)
    lane_f[0, D_NOPE:D_NOPE + 16] = inv
    lane_f[0, D_NOPE + 16:D_NOPE + 32] = inv

    def fn(i, p, f):
        ang = p * f
        lane = lax.broadcasted_iota(jnp.int32, ang.shape, 1)
        co = jnp.cos(ang)
        si = jnp.sin(ang)
        c = jnp.where(lane < D_NOPE, 1.0, jnp.where(lane < D_QK, co, 0.0))
        s1 = jnp.where((lane >= D_NOPE) & (lane < D_NOPE + 16), -si, 0.0)
        s2 = jnp.where((lane >= D_NOPE + 16) & (lane < D_QK), si, 0.0)
        return c, s1, s2

    return _rows(fn, name="rope_tables", n=l // tm, ins=[pos, jnp.asarray(lane_f)],
                 in_specs=[_rt(tm, 1), _full((1, LANE))], outs=[_sds((l, LANE))] * 3, out_specs=[_rt(tm, LANE)] * 3)


def _ssm_param_fn(lr, li, log_dt, br, bi):
    dt = jnp.exp(log_dt)
    mag = jnp.exp(lr * dt)
    a_re = mag * jnp.cos(li * dt)
    a_im = mag * jnp.sin(li * dt)
    den = lr * lr + li * li
    e_re = a_re - 1.0
    e_im = a_im
    f_re = (e_re * lr + e_im * li) / den
    f_im = (e_im * lr - e_re * li) / den
    bb_re = f_re[None] * br - f_im[None] * bi
    bb_im = f_re[None] * bi + f_im[None] * br
    return a_re, a_im, bb_re, bb_im


def _ssm_params(name, lr, li, log_dt, br, bi):
    g, n = lr.shape
    c = br.shape[0]
    return _rows(lambda i, *v: _ssm_param_fn(*v), name=name, n=1, ins=[lr, li, log_dt, br, bi],
                 in_specs=[_full((g, n)), _full((g, n)), _full((g, 1)), _full((c, g, n)), _full((c, g, n))],
                 outs=[_sds((g, n)), _sds((g, n)), _sds((c, g, n)), _sds((c, g, n))],
                 out_specs=[_full((g, n)), _full((g, n)), _full((c, g, n)), _full((c, g, n))])


def _ssm_params_bwd(name, lr, li, log_dt, br, bi, d_are, d_aim, d_bbre, d_bbim):
    g, n = lr.shape
    c = br.shape[0]

    def fn(i, lr, li, log_dt, br, bi, g0, g1, g2, g3):
        _, vjp = jax.vjp(_ssm_param_fn, lr, li, log_dt, br, bi)
        return vjp((g0, g1, g2, g3))

    sp = [_full((g, n)), _full((g, n)), _full((g, 1)), _full((c, g, n)), _full((c, g, n))]
    return _rows(fn, name=name, n=1, ins=[lr, li, log_dt, br, bi, d_are, d_aim, d_bbre, d_bbim],
                 in_specs=sp + [_full((g, n)), _full((g, n)), _full((c, g, n)), _full((c, g, n))],
                 outs=[_sds((g, n)), _sds((g, n)), _sds((g, 1)), _sds((c, g, n)), _sds((c, g, n))], out_specs=sp)


_EYE8 = np.eye(8, dtype=np.float32)


def _blockdiag(v):
    j, g, p, q = v.shape
    m = v[:, :, :, None, :] * jnp.asarray(_EYE8)[None, :, None, :, None]
    return m.reshape(j, g * p, g * q)


def _blockdiag_t(m, p, q):
    j = m.shape[0]
    m = m.reshape(j, 8, p, 8, q)
    return jnp.sum(m * jnp.asarray(_EYE8)[None, :, None, :, None], axis=3)


def _to_perm(v, l):
    ns = SCAN_SEGS
    return v.reshape(ns, l // ns, v.shape[-1]).transpose(1, 0, 2).reshape(l, v.shape[-1])


def _from_perm(v, l):
    ns = SCAN_SEGS
    return v.reshape(l // ns, ns, v.shape[-1]).transpose(1, 0, 2).reshape(l, v.shape[-1])


def _prep_layer(w, i):
    p = {}
    w_in = w['w_in'][i]
    z = lambda n: jnp.zeros((D_MODEL, n), w_in.dtype)
    o = Q_LORA + KV_LORA
    p['w_s'] = jnp.concatenate([w_in[:, :o], z(D_NOPE), w_in[:, o:o + D_ROPE], z(HEAD_PAD - D_QK)], axis=1)
    o += D_ROPE
    p['w_u'] = w_in[:, o:o + SSM_WIDTH]
    o += SSM_WIDTH
    p['w_xq'] = w_in[:, o:o + X_WIDTH]
    o += X_WIDTH
    p['w_g'] = w_in[:, o:]
    wq = w['w_q_b'][i].reshape(Q_LORA, MLA_HEADS, D_QK)
    p['w_qb'] = jnp.pad(wq, ((0, 0), (0, 0), (0, HEAD_PAD - D_QK))).reshape(Q_LORA, MLA_PAD)
    wkv = w['w_kv_b'][i].reshape(KV_LORA, MLA_HEADS, D_NOPE + D_V)
    p['w_k'] = jnp.pad(wkv[:, :, :D_NOPE], ((0, 0), (0, 0), (0, HEAD_PAD - D_NOPE))).reshape(KV_LORA, MLA_PAD)
    p['w_v'] = jnp.pad(wkv[:, :, D_NOPE:], ((0, 0), (0, 0), (0, HEAD_PAD - D_V))).reshape(KV_LORA, MLA_PAD)
    wo = w['w_o_mla'][i].reshape(MLA_HEADS, D_V, D_MODEL)
    p['w_oa'] = jnp.pad(wo, ((0, 0), (0, HEAD_PAD - D_V), (0, 0))).reshape(MLA_PAD, D_MODEL)
    for n in ('w_glu', 'w_o_ssm', 'w_mem_kv', 'w_o_cross', 'w_out', 'w_up', 'w_down'):
        p[n] = w[n][i]
    p['conv_w'] = w['conv_w'][i]
    for n in ('norm_mix_g', 'q_a_norm_g', 'kv_a_norm_g', 'b_glu', 'mem_norm_g', 'xq_norm_g', 'xk_norm_g', 'b_gate',
              'norm_ffn_g', 'conv_b'):
        p[n] = w[n][i].reshape(1, -1)
    p['q_norm_g'] = jnp.pad(w['q_norm_g'][i], (0, HEAD_PAD - D_QK)).reshape(1, HEAD_PAD)
    p['k_norm_g'] = jnp.pad(w['k_norm_g'][i], (0, HEAD_PAD - D_QK)).reshape(1, HEAD_PAD)
    p['ssm_d'] = w['ssm_d'][i].reshape(1, SSM_WIDTH)
    p['lr'] = w['ssm_lambda_re'][i]
    p['li'] = w['ssm_lambda_im'][i]
    p['log_dt'] = w['ssm_log_dt'][i].reshape(SSM_GROUPS, 1)
    p['br'] = w['ssm_b_re'][i].transpose(2, 0, 1)
    p['bi'] = w['ssm_b_im'][i].transpose(2, 0, 1)
    cr = w['ssm_c_re'][i].reshape(SSM_JB, 8, SSM_GROUP_CH, SSM_STATE).transpose(0, 1, 3, 2)
    ci = w['ssm_c_im'][i].reshape(SSM_JB, 8, SSM_GROUP_CH, SSM_STATE).transpose(0, 1, 3, 2)
    p['c_mat'] = jnp.concatenate([_blockdiag(cr), -_blockdiag(ci)], axis=1).astype(BF16)
    return p


def _b_mat(bb_re, bb_im):
    r = bb_re.transpose(1, 0, 2).reshape(SSM_JB, 8, SSM_GROUP_CH, SSM_STATE)
    i = bb_im.transpose(1, 0, 2).reshape(SSM_JB, 8, SSM_GROUP_CH, SSM_STATE)
    return jnp.concatenate([_blockdiag(r), _blockdiag(i)], axis=2).astype(BF16)


def _qkv_fn(ps, c, s1, s2, qag, wqb, kvag, wk, wv, qng, kng):
    c_q = ps[:, :Q_LORA]
    c_kv = ps[:, Q_LORA:Q_LORA + KV_LORA]
    kr = ps[:, Q_LORA + KV_LORA:]
    cqn = _rms(c_q, qag, Q_LORA)
    ckvn = _rms(c_kv, kvag, KV_LORA)
    q_raw = _mm(cqn, wqb)
    k_raw = _mm(ckvn, wk) + jnp.concatenate([kr] * MLA_HEADS, axis=-1)
    v = _mm(ckvn, wv)
    q = _heads(_rope, _head_rms(q_raw, qng, MLA_HEADS, D_QK), MLA_HEADS, c, s1, s2)
    k = _heads(_rope, _head_rms(k_raw, kng, MLA_HEADS, D_QK), MLA_HEADS, c, s1, s2)
    lane = lax.broadcasted_iota(jnp.int32, v.shape, 1)
    v = jnp.where((lane & (LANE - 1)) == D_V, 1.0, v)
    return q * ATT_QSCALE, k, v


def _layer_fwd(name, x, tabs, mem, p, gather=None):
    l = x.shape[0]
    tm = min(512, l)
    nt = l // tm
    sv = {'x0': x}
    sv['p_g'] = _matmul(name + "_in_g", [(x, p['w_g'])], l, 3 * D_MODEL, rms_gain=p['norm_mix_g'])
    sv['p_u'] = _matmul(name + "_in_u", [(x, p['w_u'])], l, SSM_WIDTH, rms_gain=p['norm_mix_g'])
    sv['p_xq'] = _matmul(name + "_in_xq", [(x, p['w_xq'])], l, X_WIDTH, rms_gain=p['norm_mix_g'])
    sv['p_s'] = _matmul(name + "_in_s", [(x, p['w_s'])], l, SMALL_W, rms_gain=p['norm_mix_g'])

    qkv_consts = [p['q_a_norm_g'], p['w_qb'], p['kv_a_norm_g'], p['w_k'], p['w_v'], p['q_norm_g'], p['k_norm_g']]
    qkv_cspecs = [_full(a.shape) for a in qkv_consts]
    def qkv_fwd(i, *a):
        qv, kv, vv = _qkv_fn(*a)
        return qv, kv, vv, jnp.transpose(kv), jnp.transpose(vv)

    q, k, v, k_t, v_t = _rows(qkv_fwd, name=name + "_qkv", n=nt, ins=[sv['p_s'], *tabs, *qkv_consts],
                              in_specs=[_rt(tm, SMALL_W)] + [_rt(tm, LANE)] * 3 + qkv_cspecs,
                              outs=[_sds((l, MLA_PAD), BF16)] * 3 + [_sds((MLA_PAD, l), BF16)] * 2,
                              out_specs=[_rt(tm, MLA_PAD)] * 3 + [pl.BlockSpec((MLA_PAD, tm), lambda i: (0, i))] * 2)
    sv['q'], sv['k'], sv['v'], sv['k_t'] = q, k, v, k_t
    sv['o_a'], sv['lse_t'], sv['gathered'] = _flash_fwd(name + "_attn", q, k, v_t, gather=gather)

    a_re, a_im, bb_re, bb_im = _ssm_params(name + "_ssm_par", p['lr'], p['li'], p['log_dt'], p['br'], p['bi'])
    sv['a_re'], sv['a_im'] = a_re.reshape(1, SSM_LANES), a_im.reshape(1, SSM_LANES)
    sv['b_mat'] = _b_mat(bb_re, bb_im)
    u_p = _to_perm(sv['p_u'], l)
    sv['u_p'] = u_p

    def bu_fn(i, u, bm):
        res = [_mm(u[:, j * LANE:(j + 1) * LANE], bm[j]) for j in range(SSM_JB)]
        return (jnp.concatenate([r[:, :512] for r in res], axis=-1), jnp.concatenate([r[:, 512:] for r in res], axis=-1))

    ts = min(256, l)
    bu_re, bu_im = _rows(bu_fn, name=name + "_ssm_bu", n=l // ts, ins=[u_p, sv['b_mat']],
                         in_specs=[_rt(ts, SSM_WIDTH), _full(sv['b_mat'].shape)],
                         outs=[_sds((l, SSM_LANES), SSM_STATE_DTYPE)] * 2, out_specs=[_rt(ts, SSM_LANES)] * 2)
    s_re, s_im = _scan(name + "_ssm_scan", bu_re, bu_im, sv['a_re'], sv['a_im'], reverse=False)
    sv['s_re'], sv['s_im'] = s_re, s_im

    def glu_fn(i, sr, si, u, cm, dsk, wg, bg):
        y = jnp.concatenate([_mm(jnp.concatenate([sr[:, j * 512:(j + 1) * 512], si[:, j * 512:(j + 1) * 512]], axis=-1),
                                 cm[j]) for j in range(SSM_JB)], axis=-1) + dsk * u
        zz = _gelu(y)
        return zz * jax.nn.sigmoid(_mm(zz, wg) + bg)

    glu_consts = [p['c_mat'], p['ssm_d'], p['w_glu'], p['b_glu']]
    zo_p = _rows(glu_fn, name=name + "_ssm_glu", n=l // ts, ins=[s_re, s_im, u_p, *glu_consts],
                 in_specs=[_rt(ts, SSM_LANES), _rt(ts, SSM_LANES), _rt(ts, SSM_WIDTH)] + [_full(a.shape) for a in glu_consts],
                 outs=[_sds((l, SSM_WIDTH), BF16)], out_specs=[_rt(ts, SSM_WIDTH)])[0]
    sv['zo'] = _from_perm(zo_p, l)

    m_len = mem.shape[0]

    def memkv_fn(i, mm_, mg, wmk, xkg):
        kv = _mm(_rms(mm_, mg, D_MODEL), wmk)
        return _head_rms(kv[:, :X_WIDTH], xkg, X_HEADS, X_HEAD_DIM), kv[:, X_WIDTH:]

    mem_consts = [p['mem_norm_g'], p['w_mem_kv'], p['xk_norm_g']]
    k_c, v_c = _rows(memkv_fn, name=name + "_memkv", n=1, ins=[mem, *mem_consts],
                     in_specs=[_full(mem.shape)] + [_full(a.shape) for a in mem_consts],
                     outs=[_sds((m_len, X_WIDTH))] * 2, out_specs=[_full((m_len, X_WIDTH))] * 2)
    sv['k_c'], sv['v_c'] = k_c, v_c

    def cross_fn(i, xq, kc, vc, xqg):
        outs = []
        for h in range(X_HEADS):
            sl = slice(h * LANE, (h + 1) * LANE)
            qh = _rms(xq[:, sl], xqg, X_HEAD_DIM)
            s = _mm_nt(qh, kc[:, sl]) * (X_HEAD_DIM ** -0.5)
            s = s - jnp.max(s, axis=-1, keepdims=True)
            e = jnp.exp(s)
            pr = e / jnp.sum(e, axis=-1, keepdims=True)
            outs.append(_mm(pr, vc[:, sl]))
        return jnp.concatenate(outs, axis=-1)

    sv['o_c'] = _rows(cross_fn, name=name + "_cross", n=nt, ins=[sv['p_xq'], k_c, v_c, p['xq_norm_g']],
                      in_specs=[_rt(tm, X_WIDTH), _full(k_c.shape), _full(v_c.shape), _full((1, LANE))],
                      outs=[_sds((l, X_WIDTH), BF16)], out_specs=[_rt(tm, X_WIDTH)])[0]

    def merge_fn(i, oa, zo, oc, pg, x0, woa, wos, woc, bg, wout):
        gates = jax.nn.sigmoid(pg + bg)
        merged = (gates[:, :D_MODEL] * _mm(oa, woa) + gates[:, D_MODEL:2 * D_MODEL] * _mm(zo, wos)
                  + gates[:, 2 * D_MODEL:] * _mm(oc, woc))
        return x0 + _mm(merged, wout), merged

    merge_consts = [p['w_oa'], p['w_o_ssm'], p['w_o_cross'], p['b_gate'], p['w_out']]
    tg = min(256, l)
    x1, merged = _rows(merge_fn, name=name + "_merge", n=l // tg, ins=[sv['o_a'], sv['zo'], sv['o_c'], sv['p_g'], x, *merge_consts],
                       in_specs=[_rt(tg, MLA_PAD), _rt(tg, SSM_WIDTH), _rt(tg, X_WIDTH), _rt(tg, 3 * D_MODEL), _rt(tg, D_MODEL)]
                       + [_full(a.shape) for a in merge_consts],
                       outs=[_sds((l, D_MODEL)), _sds((l, D_MODEL), BF16)], out_specs=[_rt(tg, D_MODEL)] * 2)
    sv['x1'], sv['merged'] = x1, merged

    up = _matmul(name + "_up", [(x1, p['w_up'])], l, 2 * D_FF, rms_gain=p['norm_ffn_g'])
    sv['up'] = up
    tc = min(128, l)

    def conv_fn(i, upt, halo, cw, cb):
        upc = _conv(i, upt, halo, cw) + cb
        return _silu(upc[:, :D_FF]) * upc[:, D_FF:]

    act = _rows(conv_fn, name=name + "_conv", n=l // tc, ins=[up, up, p['conv_w'], p['conv_b']],
                in_specs=[_rt(tc, 2 * D_FF), _halo_prev(tc, 2 * D_FF), _full((3, 2 * D_FF)), _full((1, 2 * D_FF))],
                outs=[_sds((l, D_FF), BF16)], out_specs=[_rt(tc, D_FF)])[0]
    sv['act'] = act
    x2 = _matmul(name + "_down", [(act, p['w_down'])], l, D_MODEL, resid=x1)
    return x2, sv


def _halo_prev(tm, w):
    return pl.BlockSpec((8, w), lambda i: (jnp.maximum(i * (tm // 8) - 1, 0), 0))


def _halo_next(tm, w, n_tiles):
    last = n_tiles * (tm // 8) - 1
    return pl.BlockSpec((8, w), lambda i: (jnp.minimum((i + 1) * (tm // 8), last), 0))


def _conv(i, tile, halo, cw):
    halo = jnp.where(i > 0, halo, 0.0)
    ext = jnp.concatenate([halo, tile], axis=0)
    n = ext.shape[0]
    x1 = pltpu.roll(ext, 1, 0)[8:]
    x2 = pltpu.roll(ext, 2, 0)[8:]
    del n
    return cw[0:1] * x2 + cw[1:2] * x1 + cw[2:3] * tile


def _layer_bwd(name, dx2, sv, tabs, mem, p, reduce=None):
    l = dx2.shape[0]
    tm = min(512, l)
    nt = l // tm
    g = {}
    x1 = sv['x1']
    dact = _matmul(name + "_b_down", [(dx2, p['w_down'])], l, D_FF, nt=True)
    g['w_down'] = _matmul_tn(name + "_gw_down", sv['act'], dx2)
    tc = min(128, l)
    ntc = l // tc

    def conv_b(i, upt, up_prev, up_next, da, da_next, cw, cb):
        up_prev = jnp.where(i > 0, up_prev, 0.0)
        da_next = jnp.where(i < ntc - 1, da_next, 0.0)
        ext = jnp.concatenate([up_prev, upt, up_next], axis=0)
        x0 = ext[8:]
        xm1 = pltpu.roll(ext, 1, 0)[8:]
        xm2 = pltpu.roll(ext, 2, 0)[8:]
        upc = cw[0:1] * xm2 + cw[1:2] * xm1 + cw[2:3] * x0 + cb
        _, vjp = jax.vjp(lambda a, b: _silu(a) * b, upc[:, :D_FF], upc[:, D_FF:])
        dg, dv = vjp(jnp.concatenate([da, da_next], axis=0))
        dupc = jnp.concatenate([dg, dv], axis=-1)
        n = dupc.shape[0]
        dup = cw[2:3] * dupc[:tc] + cw[1:2] * pltpu.roll(dupc, n - 1, 0)[:tc] + cw[0:1] * pltpu.roll(dupc, n - 2, 0)[:tc]
        dt = dupc[:tc]
        dcw = _row_select([_colsum(dt * xm2[:tc]), _colsum(dt * xm1[:tc]), _colsum(dt * upt)], 8)
        return dup, dcw, _colsum(dt)

    dup, g_cw, g_cb = _rows(
        conv_b, name=name + "_b_conv", n=ntc, ins=[sv['up'], sv['up'], sv['up'], dact, dact, p['conv_w'], p['conv_b']],
        in_specs=[_rt(tc, 2 * D_FF), _halo_prev(tc, 2 * D_FF), _halo_next(tc, 2 * D_FF, ntc), _rt(tc, D_FF),
                  _halo_next(tc, D_FF, ntc), _full((3, 2 * D_FF)), _full((1, 2 * D_FF))],
        outs=[_sds((l, 2 * D_FF)), _sds((8, 2 * D_FF)), _sds((1, 2 * D_FF))],
        out_specs=[_rt(tc, 2 * D_FF), _full((8, 2 * D_FF)), _full((1, 2 * D_FF))], n_acc=2, vmem=56)
    g['conv_w'] = g_cw[:3]
    g['conv_b'] = g_cb
    dh2 = _matmul(name + "_b_up", [(dup, p['w_up'])], l, D_MODEL, nt=True, tm=256)
    g['w_up'] = _matmul_tn(name + "_gw_up", x1, dup, rms_gain=p['norm_ffn_g'])

    def norm_b(i, xv, dh, dres, gn):
        _, vjp = jax.vjp(lambda a, b: _rms(a, b, D_MODEL), xv, gn)
        dxv, dgn = vjp(dh)
        return dres + dxv, dgn

    dx1, g['norm_ffn_g'] = _rows(norm_b, name=name + "_b_norm2", n=nt, ins=[x1, dh2, dx2, p['norm_ffn_g']],
                                 in_specs=[_rt(tm, D_MODEL)] * 3 + [_full((1, D_MODEL))],
                                 outs=[_sds((l, D_MODEL)), _sds((1, D_MODEL))], out_specs=[_rt(tm, D_MODEL), _full((1, D_MODEL))],
                                 n_acc=1)

    tg = min(256, l)

    def merge_b(i, dx, oa, zo, oc, pg, woa, wos, woc, bg, wout):
        dm = _mm_nt(dx, wout)
        gates = jax.nn.sigmoid(pg + bg)
        ys = [_mm(oa, woa), _mm(zo, wos), _mm(oc, woc)]
        dys, dpg = [], []
        for b in range(3):
            gb = gates[:, b * D_MODEL:(b + 1) * D_MODEL]
            dys.append(dm * gb)
            dpg.append(dm * ys[b] * gb * (1.0 - gb))
        dpg = jnp.concatenate(dpg, axis=-1)
        return (_mm_nt(dys[0], woa), _mm_nt(dys[1], wos), _mm_nt(dys[2], woc), dpg, dys[0], dys[1], dys[2], _colsum(dpg))

    merge_consts = [p['w_oa'], p['w_o_ssm'], p['w_o_cross'], p['b_gate'], p['w_out']]
    (do_a, dzo, do_c, dp_g, dy_a, dy_b, dy_c, g['b_gate']) = _rows(
        merge_b, name=name + "_b_merge", n=l // tg, ins=[dx1, sv['o_a'], sv['zo'], sv['o_c'], sv['p_g'], *merge_consts],
        in_specs=[_rt(tg, D_MODEL), _rt(tg, MLA_PAD), _rt(tg, SSM_WIDTH), _rt(tg, X_WIDTH), _rt(tg, 3 * D_MODEL)]
        + [_full(a.shape) for a in merge_consts],
        outs=[_sds((l, MLA_PAD)), _sds((l, SSM_WIDTH)), _sds((l, X_WIDTH)), _sds((l, 3 * D_MODEL)),
              _sds((l, D_MODEL), BF16), _sds((l, D_MODEL), BF16), _sds((l, D_MODEL), BF16), _sds((1, 3 * D_MODEL))],
        out_specs=[_rt(tg, MLA_PAD), _rt(tg, SSM_WIDTH), _rt(tg, X_WIDTH), _rt(tg, 3 * D_MODEL),
                   _rt(tg, D_MODEL), _rt(tg, D_MODEL), _rt(tg, D_MODEL), _full((1, 3 * D_MODEL))], n_acc=1, vmem=56)
    g['w_out'] = _matmul_tn(name + "_gw_out", sv['merged'], dx1)
    g['w_oa'] = _matmul_tn(name + "_gw_oa", sv['o_a'], dy_a)
    g['w_o_ssm'] = _matmul_tn(name + "_gw_os", sv['zo'], dy_b)
    g['w_o_cross'] = _matmul_tn(name + "_gw_oc", sv['o_c'], dy_c)

    k_c, v_c = sv['k_c'], sv['v_c']
    m_len = k_c.shape[0]

    def cross_b(i, xq, do, kc, vc, xqg):
        dxq, dk, dv = [], [], []
        dg = jnp.zeros((1, LANE), F32)
        for h in range(X_HEADS):
            sl = slice(h * LANE, (h + 1) * LANE)
            qh, vjp = jax.vjp(lambda a, b: _rms(a, b, X_HEAD_DIM), xq[:, sl], xqg)
            sc = X_HEAD_DIM ** -0.5
            s = _mm_nt(qh, kc[:, sl]) * sc
            s = s - jnp.max(s, axis=-1, keepdims=True)
            e = jnp.exp(s)
            pr = e / jnp.sum(e, axis=-1, keepdims=True)
            doh = do[:, sl]
            dv.append(_mm_tn(pr, doh))
            dp = _mm_nt(doh, vc[:, sl])
            ds = pr * (dp - jnp.sum(dp * pr, axis=-1, keepdims=True)) * sc
            dk.append(_mm_tn(ds, qh))
            dxh, dgh = vjp(_mm(ds, kc[:, sl]))
            dxq.append(dxh)
            dg = dg + dgh
        return jnp.concatenate(dxq, axis=-1), jnp.concatenate(dk, axis=-1), jnp.concatenate(dv, axis=-1), dg

    dp_xq, dk_c, dv_c, g['xq_norm_g'] = _rows(
        cross_b, name=name + "_b_cross", n=nt, ins=[sv['p_xq'], do_c, k_c, v_c, p['xq_norm_g']],
        in_specs=[_rt(tm, X_WIDTH), _rt(tm, X_WIDTH), _full(k_c.shape), _full(v_c.shape), _full((1, LANE))],
        outs=[_sds((l, X_WIDTH)), _sds((m_len, X_WIDTH)), _sds((m_len, X_WIDTH)), _sds((1, LANE))],
        out_specs=[_rt(tm, X_WIDTH), _full((m_len, X_WIDTH)), _full((m_len, X_WIDTH)), _full((1, LANE))], n_acc=3)

    def memkv_b(i, mm_, dk, dv, mg, wmk, xkg):
        memn, vjp_n = jax.vjp(lambda a, b: _rms(a, b, D_MODEL), mm_, mg)
        kv = _mm(memn, wmk)
        _, vjp_k = jax.vjp(lambda a, b: _head_rms(a, b, X_HEADS, X_HEAD_DIM), kv[:, :X_WIDTH], xkg)
        dkr, dxkg = vjp_k(dk)
        dkv = jnp.concatenate([dkr, dv], axis=-1)
        _, dmg = vjp_n(_mm_nt(dkv, wmk))
        return _mm_tn(memn, dkv), dmg, dxkg

    mem_consts = [p['mem_norm_g'], p['w_mem_kv'], p['xk_norm_g']]
    g['w_mem_kv'], g['mem_norm_g'], g['xk_norm_g'] = _rows(
        memkv_b, name=name + "_b_memkv", n=1, ins=[mem, dk_c, dv_c, *mem_consts],
        in_specs=[_full(mem.shape), _full(dk_c.shape), _full(dv_c.shape)] + [_full(a.shape) for a in mem_consts],
        outs=[_sds((D_MODEL, 2 * X_WIDTH)), _sds((1, D_MODEL)), _sds((1, LANE))],
        out_specs=[_full((D_MODEL, 2 * X_WIDTH)), _full((1, D_MODEL)), _full((1, LANE))])

    u_p = sv['u_p']
    dzo_p = _to_perm(dzo, l)
    s_re, s_im = sv['s_re'], sv['s_im']

    def glu_b(i, sr, si, u, dz, cm, dsk, wg, bg):
        cats = [jnp.concatenate([sr[:, j * 512:(j + 1) * 512], si[:, j * 512:(j + 1) * 512]], axis=-1) for j in range(SSM_JB)]
        y = jnp.concatenate([_mm(cats[j], cm[j]) for j in range(SSM_JB)], axis=-1) + dsk * u
        zz, vjp_g = jax.vjp(_gelu, y)
        t = _mm(zz, wg) + bg
        sg = jax.nn.sigmoid(t)
        dt = dz * zz * sg * (1.0 - sg)
        dzz = dz * sg + _mm_nt(dt, wg)
        dy = vjp_g(dzz)[0]
        dss = [_mm_nt(dy[:, j * LANE:(j + 1) * LANE], cm[j]) for j in range(SSM_JB)]
        dsr = jnp.concatenate([d[:, :512] for d in dss], axis=-1)
        dsi = jnp.concatenate([d[:, 512:] for d in dss], axis=-1)
        dcm = jnp.stack([_mm_tn(cats[j], dy[:, j * LANE:(j + 1) * LANE]) for j in range(SSM_JB)], axis=0)
        return dsr, dsi, dy * dsk, dcm, _colsum(dy * u), _mm_tn(zz, dt), _colsum(dt)

    glu_consts = [p['c_mat'], p['ssm_d'], p['w_glu'], p['b_glu']]
    ts = min(256, l)
    nts = l // ts
    ds_re, ds_im, du_dir, g['c_mat'], g['ssm_d'], g['w_glu'], g['b_glu'] = _rows(
        glu_b, name=name + "_b_glu", n=nts, ins=[s_re, s_im, u_p, dzo_p, *glu_consts],
        in_specs=[_rt(ts, SSM_LANES), _rt(ts, SSM_LANES), _rt(ts, SSM_WIDTH), _rt(ts, SSM_WIDTH)] + [_full(a.shape) for a in glu_consts],
        outs=[_sds((l, SSM_LANES), SSM_STATE_DTYPE), _sds((l, SSM_LANES), SSM_STATE_DTYPE), _sds((l, SSM_WIDTH)),
              _sds((SSM_JB, 1024, LANE)), _sds((1, SSM_WIDTH)),
              _sds((SSM_WIDTH, SSM_WIDTH)), _sds((1, SSM_WIDTH))],
        out_specs=[_rt(ts, SSM_LANES), _rt(ts, SSM_LANES), _rt(ts, SSM_WIDTH), _full((SSM_JB, 1024, LANE)), _full((1, SSM_WIDTH)),
                   _full((SSM_WIDTH, SSM_WIDTH)), _full((1, SSM_WIDTH))], n_acc=4)
    gb_re, gb_im = _scan(name + "_b_scan", ds_re, ds_im, sv['a_re'], -sv['a_im'], reverse=True)
    ns = SCAN_SEGS
    last_blk = l // ns - 1

    def da_fn(i, *vals):
        gr, gi, sr, si, hr, hi, lr_, li_ = [v.astype(F32) for v in vals]
        rid = lax.broadcasted_iota(jnp.int32, lr_.shape, 0)
        fr = jnp.where(rid == 0, 0.0, pltpu.roll(lr_, 1, 0))
        fi = jnp.where(rid == 0, 0.0, pltpu.roll(li_, 1, 0))
        hr = jnp.where(i == 0, fr, hr)
        hi = jnp.where(i == 0, fi, hi)
        if ts > ns:
            pr = jnp.concatenate([hr, sr[:ts - ns]], axis=0)
            pi = jnp.concatenate([hi, si[:ts - ns]], axis=0)
        else:
            pr, pi = hr, hi
        return _colsum(gr * pr + gi * pi), _colsum(gi * pr - gr * pi)

    hprev = pl.BlockSpec((ns, SSM_LANES), lambda i: (jnp.maximum(i * (ts // ns) - 1, 0), 0))
    hlast = pl.BlockSpec((ns, SSM_LANES), lambda i: (last_blk, 0))
    da_re, da_im = _rows(da_fn, name=name + "_b_da", n=nts, ins=[gb_re, gb_im, s_re, s_im, s_re, s_im, s_re, s_im],
                         in_specs=[_rt(ts, SSM_LANES)] * 4 + [hprev, hprev, hlast, hlast],
                         outs=[_sds((1, SSM_LANES))] * 2, out_specs=[_full((1, SSM_LANES))] * 2, n_acc=2)

    def bu_b(i, dbr, dbi, u, dud, bm):
        dus, dbm = [], []
        for j in range(SSM_JB):
            cat = jnp.concatenate([dbr[:, j * 512:(j + 1) * 512], dbi[:, j * 512:(j + 1) * 512]], axis=-1)
            dus.append(_mm_nt(cat, bm[j]))
            dbm.append(_mm_tn(u[:, j * LANE:(j + 1) * LANE], cat))
        return dud + jnp.concatenate(dus, axis=-1), jnp.stack(dbm, axis=0)

    du_p, d_bmat = _rows(bu_b, name=name + "_b_bu", n=nts, ins=[gb_re, gb_im, u_p, du_dir, sv['b_mat']],
                         in_specs=[_rt(ts, SSM_LANES), _rt(ts, SSM_LANES), _rt(ts, SSM_WIDTH), _rt(ts, SSM_WIDTH),
                                   _full(sv['b_mat'].shape)],
                         outs=[_sds((l, SSM_WIDTH)), _sds((SSM_JB, LANE, 1024))],
                         out_specs=[_rt(ts, SSM_WIDTH), _full((SSM_JB, LANE, 1024))], n_acc=1)
    dp_u = _from_perm(du_p, l)
    dbb_re = _blockdiag_t(d_bmat[:, :, :512], SSM_GROUP_CH, SSM_STATE).reshape(SSM_GROUPS, SSM_GROUP_CH, SSM_STATE).transpose(1, 0, 2)
    dbb_im = _blockdiag_t(d_bmat[:, :, 512:], SSM_GROUP_CH, SSM_STATE).reshape(SSM_GROUPS, SSM_GROUP_CH, SSM_STATE).transpose(1, 0, 2)
    g['lr'], g['li'], g['log_dt'], g['br'], g['bi'] = _ssm_params_bwd(
        name + "_b_ssm_par", p['lr'], p['li'], p['log_dt'], p['br'], p['bi'],
        da_re.reshape(SSM_GROUPS, SSM_STATE), da_im.reshape(SSM_GROUPS, SSM_STATE), dbb_re, dbb_im)

    dq_t, dk, dv, rode = _flash_bwd(name + "_b_attn", sv['q'], sv['k'], sv['v'], sv['k_t'], sv['o_a'], sv['lse_t'], do_a,
                                    reduce=reduce)

    def qkv_b(i, ps, c, s1, s2, dq_, dk_, dv_, qag, wqb, kvag, wk, wv, qng, kng):
        c_q = ps[:, :Q_LORA]
        c_kv = ps[:, Q_LORA:Q_LORA + KV_LORA]
        kr = ps[:, Q_LORA + KV_LORA:]
        cqn, vjp_cq = jax.vjp(lambda a, b: _rms(a, b, Q_LORA), c_q, qag)
        ckvn, vjp_ckv = jax.vjp(lambda a, b: _rms(a, b, KV_LORA), c_kv, kvag)
        q_raw = _mm(cqn, wqb)
        k_raw = _mm(ckvn, wk) + jnp.concatenate([kr] * MLA_HEADS, axis=-1)
        _, vjp_qn = jax.vjp(lambda a, b: _head_rms(a, b, MLA_HEADS, D_QK), q_raw, qng)
        _, vjp_kn = jax.vjp(lambda a, b: _head_rms(a, b, MLA_HEADS, D_QK), k_raw, kng)
        dq_raw, dqng = vjp_qn(_heads(_rope_t, jnp.transpose(dq_[0]), MLA_HEADS, c, s1, s2))
        dk_raw, dkng = vjp_kn(_heads(_rope_t, dk_, MLA_HEADS, c, s1, s2))
        dkr = dk_raw[:, :LANE]
        for h in range(1, MLA_HEADS):
            dkr = dkr + dk_raw[:, h * LANE:(h + 1) * LANE]
        dcq, dqag = vjp_cq(_mm_nt(dq_raw, wqb))
        dckv, dkvag = vjp_ckv(_mm_nt(dk_raw, wk) + _mm_nt(dv_, wv))
        dps = jnp.concatenate([dcq, dckv, dkr], axis=-1)
        return (dps, _mm_tn(cqn, dq_raw), _mm_tn(ckvn, dk_raw), _mm_tn(ckvn, dv_), dqag, dkvag, dqng, dkng)

    qkv_consts = [p['q_a_norm_g'], p['w_qb'], p['kv_a_norm_g'], p['w_k'], p['w_v'], p['q_norm_g'], p['k_norm_g']]
    (dp_s, g['w_qb'], g['w_k'], g['w_v'], g['q_a_norm_g'], g['kv_a_norm_g'], g['q_norm_g'], g['k_norm_g']) = _rows(
        qkv_b, name=name + "_b_qkv", n=nt, ins=[sv['p_s'], *tabs, dq_t, dk, dv, *qkv_consts],
        in_specs=[_rt(tm, SMALL_W)] + [_rt(tm, LANE)] * 3
        + [pl.BlockSpec((1, MLA_PAD, tm), lambda i: (i // (dq_t.shape[2] // tm), 0, i % (dq_t.shape[2] // tm)))]
        + [_rt(tm, MLA_PAD)] * 2 + [_full(a.shape) for a in qkv_consts],
        outs=[_sds((l, SMALL_W)), _sds((Q_LORA, MLA_PAD)), _sds((KV_LORA, MLA_PAD)), _sds((KV_LORA, MLA_PAD)),
              _sds((1, Q_LORA)), _sds((1, KV_LORA)), _sds((1, LANE)), _sds((1, LANE))],
        out_specs=[_rt(tm, SMALL_W), _full((Q_LORA, MLA_PAD)), _full((KV_LORA, MLA_PAD)), _full((KV_LORA, MLA_PAD)),
                   _full((1, Q_LORA)), _full((1, KV_LORA)), _full((1, LANE)), _full((1, LANE))], n_acc=7)

    x0 = sv['x0']
    dh = _matmul(name + "_b_in", [(dp_g, p['w_g']), (dp_u, p['w_u']), (dp_xq, p['w_xq']), (dp_s, p['w_s'])], l, D_MODEL, nt=True,
                 tm=256)
    gm = p['norm_mix_g']
    g['w_g'] = _matmul_tn(name + "_gw_g", x0, dp_g, rms_gain=gm)
    g['w_u'] = _matmul_tn(name + "_gw_u", x0, dp_u, rms_gain=gm)
    g['w_xq'] = _matmul_tn(name + "_gw_xq", x0, dp_xq, rms_gain=gm)
    g['w_s'] = _matmul_tn(name + "_gw_s", x0, dp_s, rms_gain=gm)
    dx0, g['norm_mix_g'] = _rows(norm_b, name=name + "_b_norm1", n=nt, ins=[x0, dh, dx1, gm],
                                 in_specs=[_rt(tm, D_MODEL)] * 3 + [_full((1, D_MODEL))],
                                 outs=[_sds((l, D_MODEL)), _sds((1, D_MODEL))], out_specs=[_rt(tm, D_MODEL), _full((1, D_MODEL))],
                                 n_acc=1)
    return dx0, g, rode


def _unprep_grads(g):
    o = {}
    ws = g['w_s']
    o['w_in'] = jnp.concatenate([ws[:, :Q_LORA + KV_LORA], ws[:, Q_LORA + KV_LORA + D_NOPE:Q_LORA + KV_LORA + D_QK],
                                 g['w_u'], g['w_xq'], g['w_g']], axis=1)
    o['w_q_b'] = g['w_qb'].reshape(Q_LORA, MLA_HEADS, HEAD_PAD)[:, :, :D_QK].reshape(Q_LORA, MLA_HEADS * D_QK)
    gk = g['w_k'].reshape(KV_LORA, MLA_HEADS, HEAD_PAD)[:, :, :D_NOPE]
    gv = g['w_v'].reshape(KV_LORA, MLA_HEADS, HEAD_PAD)[:, :, :D_V]
    o['w_kv_b'] = jnp.concatenate([gk, gv], axis=2).reshape(KV_LORA, MLA_HEADS * (D_NOPE + D_V))
    o['w_o_mla'] = g['w_oa'].reshape(MLA_HEADS, HEAD_PAD, D_MODEL)[:, :D_V].reshape(MLA_HEADS * D_V, D_MODEL)
    for n in ('w_glu', 'w_o_ssm', 'w_mem_kv', 'w_o_cross', 'w_out', 'w_up', 'w_down', 'conv_w'):
        o[n] = g[n]
    for n in ('norm_mix_g', 'q_a_norm_g', 'kv_a_norm_g', 'b_glu', 'mem_norm_g', 'xq_norm_g', 'xk_norm_g', 'b_gate',
              'norm_ffn_g', 'conv_b'):
        o[n] = g[n].reshape(-1)
    o['q_norm_g'] = g['q_norm_g'].reshape(-1)[:D_QK]
    o['k_norm_g'] = g['k_norm_g'].reshape(-1)[:D_QK]
    o['ssm_d'] = g['ssm_d'].reshape(SSM_GROUPS, SSM_GROUP_CH)
    o['ssm_lambda_re'] = g['lr']
    o['ssm_lambda_im'] = g['li']
    o['ssm_log_dt'] = g['log_dt'].reshape(SSM_GROUPS)
    o['ssm_b_re'] = g['br'].transpose(1, 2, 0)
    o['ssm_b_im'] = g['bi'].transpose(1, 2, 0)
    dc = g['c_mat']
    o['ssm_c_re'] = _blockdiag_t(dc[:, :512], SSM_STATE, SSM_GROUP_CH).transpose(0, 1, 3, 2).reshape(SSM_GROUPS, SSM_GROUP_CH, SSM_STATE)
    o['ssm_c_im'] = -_blockdiag_t(dc[:, 512:], SSM_STATE, SSM_GROUP_CH).transpose(0, 1, 3, 2).reshape(SSM_GROUPS, SSM_GROUP_CH, SSM_STATE)
    return o


def _local_step(x, mem, pos, target, w, late_gather=None, early_reduce=None):
    l = x.shape[0]
    tm = min(512, l)
    tabs = _rope_tables(pos.astype(F32).reshape(l, 1))
    saved = []
    ps = []
    h = x
    for i in range(DEPTH):
        ps.append(_prep_layer(w, i))
        riding = late_gather[:2] if (late_gather is not None and i == 0) else None
        h, sv = _layer_fwd("l%d" % i, h, tabs, mem, ps[i], gather=riding)
        if riding is not None:
            for n, v in late_gather[2](sv.pop('gathered')).items():
                w[n][late_gather[1]] = v
        saved.append(sv)

    def loss_fn(i, y, t):
        e = y - t
        per_tok = jnp.sum(e * e, axis=-1, keepdims=True) * (1.0 / D_MODEL)
        tot = 0.5 * jnp.sum(per_tok, axis=0, keepdims=True)
        return e * (1.0 / D_MODEL), jnp.broadcast_to(tot, (1, LANE))

    dy, loss = _rows(loss_fn, name="loss", n=l // tm, ins=[h, target], in_specs=[_rt(tm, D_MODEL)] * 2,
                     outs=[_sds((l, D_MODEL)), _sds((1, LANE))], out_specs=[_rt(tm, D_MODEL), _full((1, LANE))], n_acc=1)
    grads = []
    d = dy
    riding, rode = None, []
    for i in reversed(range(DEPTH)):
        d, g, got = _layer_bwd("l%d" % i, d, saved[i], tabs, mem, ps[i], reduce=riding)
        rode = got or rode
        grads.append(_unprep_grads(g))
        riding = (early_reduce(grads[-1]), DEPTH - 1) if (early_reduce is not None and i == DEPTH - 1) else None
    return loss[0, 0], d, grads[::-1], rode


def _sum_picked(name, slots, pick, extra, out_dtype):
    _, r, c = slots.shape
    e = extra.shape[0]
    tr = _row_tile(r)

    def body(pk, s_ref, x_ref, o_ref):
        acc = s_ref[...].astype(F32)
        for k in range(e):
            acc = acc + x_ref[k].astype(F32)
        o_ref[...] = acc.astype(o_ref.dtype)

    grid_spec = pltpu.PrefetchScalarGridSpec(
        num_scalar_prefetch=1, grid=(r // tr,),
        in_specs=[pl.BlockSpec((None, tr, c), lambda i, pk: (pk[0], i, 0)), pl.BlockSpec((e, tr, c), lambda i, pk: (0, i, 0))],
        out_specs=pl.BlockSpec((tr, c), lambda i, pk: (i, 0)))
    return pl.pallas_call(body, name=name, grid_spec=grid_spec, out_shape=_sds((r, c), out_dtype),
                          compiler_params=_params(("arbitrary",), 48))(pick, slots, extra)


def _row_tile(r):
    for t in (256, 128, 64, 32, 16, 8):
        if r % t == 0:
            return t
    return r


def _adamw(name, parts, w, m, v):
    r, cw = w.shape
    tr = _row_tile(r)
    np_ = len(parts)

    def fn(i, *vals):
        wv, mv, vv = vals[np_:]
        terms = []
        for pv in vals[:np_]:
            terms += [pv] if pv.ndim == 2 else [pv[k] for k in range(pv.shape[0])]
        g = terms[0]
        for t in terms[1:]:
            g = g + t
        mn = ADAM_B1 * mv + (1.0 - ADAM_B1) * g
        vn = ADAM_B2 * vv + (1.0 - ADAM_B2) * (g * g)
        m_hat = mn / (1.0 - ADAM_B1 ** ADAM_STEP)
        v_hat = vn / (1.0 - ADAM_B2 ** ADAM_STEP)
        delta = -ADAM_LR * (m_hat / (jnp.sqrt(v_hat) + ADAM_EPS) + ADAM_WD * wv)
        return g, delta, mn, vn

    pspecs = [_rt(tr, cw) if p.ndim == 2 else pl.BlockSpec((p.shape[0], tr, cw), lambda i: (0, i, 0)) for p in parts]
    return _rows(fn, name=name, n=r // tr, ins=[*parts, w, m, v], in_specs=pspecs + [_rt(tr, cw)] * 3,
                 outs=[_sds((r, cw))] * 4, out_specs=[_rt(tr, cw)] * 4)


def _shard_of(a, axis, k):
    n = a.shape[axis] // 4
    return lax.slice_in_dim(a, k * n, (k + 1) * n, axis=axis)


def _step(a):
    x = a['x'][0]
    mem = a['mem'][0]
    pos = a['positions'][0]
    target = a['loss_target'][0]

    me = 2 * lax.axis_index("x") + lax.axis_index("y")

    mine = [a[n] if n == 'conv_w' else a[n].astype(BF16) for n in SHARDED]

    def assemble(bufs, layer):
        return {n: jnp.concatenate([jnp.where(me == k, own[layer], y[k]) for k in range(4)], axis=SHARD_AXIS[n] - 1)
                for n, own, y in zip(SHARDED, mine, bufs)}

    w = {n: [v, None] for n, v in assemble(_layer_gather("comm_gather_l0", mine, 0), 0).items()}
    for n in SMALL:
        w[n] = a[n]

    mc = lax.axis_index("c")
    me1 = me.astype(jnp.int32).reshape(1)
    zero1i = jnp.zeros((1,), jnp.int32)

    def pair_sums(layer, gl):
        mine_l = [jnp.stack([_shard_of(gl[n], SHARD_AXIS[n] - 1, k) for k in range(4)], axis=0).astype(BF16) for n in SHARDED]
        per_core = [None, None]
        per_core[1 - layer] = mine_l
        theirs = _send_d2d("comm_pair_l%d" % layer, per_core)
        out = []
        for n, g, s in zip(SHARDED, mine_l, theirs):
            cols = g.shape[-1]
            out.append(_sum_picked("sum2_l%d_%s" % (layer, n), g.reshape(1, -1, cols), zero1i, s.reshape(1, -1, cols), BF16)
                       .reshape(g.shape))
        return out

    pairs = [None, None]

    def early_reduce(gl):
        pairs[1] = pair_sums(1, gl)
        return pairs[1]

    loss, grad_x, grads, got1 = _local_step(x, mem, pos, target, w, late_gather=(mine, 1, lambda bufs: assemble(bufs, 1)),
                                            early_reduce=early_reduce)
    pairs[0] = pair_sums(0, grads[0])
    gsm = _pack([jnp.stack([grads[i][n] for i in range(DEPTH)], axis=0) for n in SMALL] + [loss.reshape(1)], 8, F32)
    got0, alls = _layer_reduce("comm_reduce_l0", pairs[0], 0, gsm)
    done = [[_sum_picked("sum4_l%d_%s" % (layer, n), p4, me1, g3, F32) for n, p4, g3 in zip(SHARDED, pairs[layer], got)]
            for layer, got in ((0, got0), (1, got1))]
    others = _send_d2d("comm_reduce_d2d", done)
    res_sh = []
    for n, d0, d1, other in zip(SHARDED, done[0], done[1], others):
        cols = d0.shape[-1]
        full = jnp.where(mc == 0, jnp.stack([d0, other], axis=0), jnp.stack([other, d1], axis=0))
        res = _adamw("adamw_" + n, [full.reshape(-1, cols)], *[a[pre + n].reshape(-1, cols) for pre in ('', 'm_', 'v_')])
        res_sh.append([r.reshape(a[n].shape) for r in res])
    res_sh = [[res_sh[j][kind] for j in range(len(SHARDED))] for kind in range(4)]

    sm_shapes = [a[n].shape for n in SMALL] + [(1,)]
    zero1 = jnp.zeros((1,), F32)
    res_sm = _adamw("adamw_small", [alls], *[_pack([a[pre + n] for n in SMALL] + [zero1], 8, F32) for pre in ('', 'm_', 'v_')])
    res_sm = [_unpack(r, sm_shapes) for r in res_sm]
    loss = res_sm[0][-1][0]

    outs = [loss, grad_x[None]]
    for kind in range(4):
        byname = dict(zip(SHARDED, res_sh[kind]))
        byname.update(zip(SMALL, res_sm[kind]))
        outs += [byname[n] for n in WEIGHTS]
    return tuple(outs)


def kernel(x, mem, positions, norm_mix_g, w_in, q_a_norm_g, w_q_b, kv_a_norm_g, w_kv_b, q_norm_g, k_norm_g, w_o_mla, ssm_lambda_re, ssm_lambda_im, ssm_log_dt, ssm_b_re, ssm_b_im, ssm_c_re, ssm_c_im, ssm_d, w_glu, b_glu, w_o_ssm, mem_norm_g, w_mem_kv, xq_norm_g, xk_norm_g, w_o_cross, b_gate, w_out, norm_ffn_g, w_up, conv_w, conv_b, w_down, loss_target, m_norm_mix_g, m_w_in, m_q_a_norm_g, m_w_q_b, m_kv_a_norm_g, m_w_kv_b, m_q_norm_g, m_k_norm_g, m_w_o_mla, m_ssm_lambda_re, m_ssm_lambda_im, m_ssm_log_dt, m_ssm_b_re, m_ssm_b_im, m_ssm_c_re, m_ssm_c_im, m_ssm_d, m_w_glu, m_b_glu, m_w_o_ssm, m_mem_norm_g, m_w_mem_kv, m_xq_norm_g, m_xk_norm_g, m_w_o_cross, m_b_gate, m_w_out, m_norm_ffn_g, m_w_up, m_conv_w, m_conv_b, m_w_down, v_norm_mix_g, v_w_in, v_q_a_norm_g, v_w_q_b, v_kv_a_norm_g, v_w_kv_b, v_q_norm_g, v_k_norm_g, v_w_o_mla, v_ssm_lambda_re, v_ssm_lambda_im, v_ssm_log_dt, v_ssm_b_re, v_ssm_b_im, v_ssm_c_re, v_ssm_c_im, v_ssm_d, v_w_glu, v_b_glu, v_w_o_ssm, v_mem_norm_g, v_w_mem_kv, v_xq_norm_g, v_xk_norm_g, v_w_o_cross, v_b_gate, v_w_out, v_norm_ffn_g, v_w_up, v_conv_w, v_conv_b, v_w_down):
    return _step(dict(locals()))
```

```python
import math

import numpy as np
import jax
import jax.numpy as jnp
from jax import lax
from jax.experimental import pallas as pl
from jax.experimental.pallas import tpu as pltpu

F32 = jnp.float32
BF16 = jnp.bfloat16
MESH = pl.DeviceIdType.MESH

DEPTH = 2
D_MODEL = 1024
EPS = 1e-6
MLA_HEADS = 8
Q_LORA = 384
KV_LORA = 256
D_NOPE = 64
D_ROPE = 32
D_QK = D_NOPE + D_ROPE
D_V = 64
HEAD_PAD = 128
MLA_PAD = MLA_HEADS * HEAD_PAD
ROPE_THETA = 10000.0
SSM_GROUPS = 32
SSM_GROUP_CH = 16
SSM_WIDTH = 512
SSM_STATE = 64
SSM_LANES = SSM_GROUPS * SSM_STATE
SSM_JB = 4
X_HEADS = 4
X_HEAD_DIM = 128
X_WIDTH = 512
D_FF = 2816
SMALL_W = Q_LORA + KV_LORA + HEAD_PAD
SCAN_SEGS = 64
SSM_STATE_DTYPE = BF16
LANE = 128
NEG = -1e30

ADAM_LR = 0.001
ADAM_B1 = 0.9
ADAM_B2 = 0.999
ADAM_EPS = 1e-08
ADAM_WD = 0.01
ADAM_STEP = 10

WEIGHTS = ['norm_mix_g', 'w_in', 'q_a_norm_g', 'w_q_b', 'kv_a_norm_g', 'w_kv_b', 'q_norm_g', 'k_norm_g', 'w_o_mla',
           'ssm_lambda_re', 'ssm_lambda_im', 'ssm_log_dt', 'ssm_b_re', 'ssm_b_im', 'ssm_c_re', 'ssm_c_im', 'ssm_d',
           'w_glu', 'b_glu', 'w_o_ssm', 'mem_norm_g', 'w_mem_kv', 'xq_norm_g', 'xk_norm_g', 'w_o_cross', 'b_gate',
           'w_out', 'norm_ffn_g', 'w_up', 'conv_w', 'conv_b', 'w_down']
SHARD_AXIS = {'w_in': 2, 'w_q_b': 2, 'w_kv_b': 2, 'w_o_mla': 2, 'w_glu': 1, 'w_o_ssm': 2, 'w_mem_kv': 1,
              'w_o_cross': 2, 'w_out': 1, 'w_up': 2, 'conv_w': 2, 'w_down': 1}
SHARDED = [n for n in WEIGHTS if n in SHARD_AXIS]
SMALL = [n for n in WEIGHTS if n not in SHARD_AXIS]


def _bf(v):
    return v.astype(BF16)


def _mm(a, b):
    return jnp.dot(_bf(a), _bf(b), preferred_element_type=F32)


def _mm_nt(a, b):
    return lax.dot_general(_bf(a), _bf(b), (((1,), (1,)), ((), ())), preferred_element_type=F32)


def _mm_tn(a, b):
    return lax.dot_general(_bf(a), _bf(b), (((0,), (0,)), ((), ())), preferred_element_type=F32)


def _rms(v, g, n):
    ms = jnp.sum(v * v, axis=-1, keepdims=True) * (1.0 / n)
    return (v * lax.rsqrt(ms + EPS)) * g


def _head_rms(v, g, heads, n):
    return jnp.concatenate([_rms(v[:, h * LANE:(h + 1) * LANE], g, n) for h in range(heads)], axis=-1)


def _rope(v, c, s1, s2):
    return v * c + pltpu.roll(v, LANE - 16, 1) * s1 + pltpu.roll(v, 16, 1) * s2


def _rope_t(g, c, s1, s2):
    return g * c + pltpu.roll(g * s1, 16, 1) + pltpu.roll(g * s2, LANE - 16, 1)


def _heads(fn, v, heads, *tabs):
    return jnp.concatenate([fn(v[:, h * LANE:(h + 1) * LANE], *tabs) for h in range(heads)], axis=-1)


def _gelu(y):
    return y * (0.5 * (1.0 + jnp.tanh(math.sqrt(2.0 / math.pi) * (y + 0.044715 * (y * y * y)))))


def _silu(g):
    return g * jax.nn.sigmoid(g)


def _colsum(v):
    return jnp.sum(v, axis=0, keepdims=True)


def _row_select(rows, n):
    rid = lax.broadcasted_iota(jnp.int32, (n, rows[0].shape[-1]), 0)
    out = jnp.zeros((n, rows[0].shape[-1]), F32)
    for k, r in enumerate(rows):
        out = jnp.where(rid == k, jnp.broadcast_to(r, out.shape), out)
    return out


def _params(sem, vmem_mb):
    return pltpu.CompilerParams(dimension_semantics=sem, vmem_limit_bytes=vmem_mb * 1024 * 1024)


def _rt(tm, w, cb=0):
    return pl.BlockSpec((tm, w), lambda i: (i, cb))


def _full(shape):
    nd = len(shape)
    return pl.BlockSpec(tuple(shape), lambda i: (0,) * nd)


def _rows(fn, *, name, n, ins, in_specs, outs, out_specs, n_acc=0, vmem=48):
    n_in = len(ins)
    n_out = len(outs)

    def body(*refs):
        i = pl.program_id(0)
        res = fn(i, *[r[...] for r in refs[:n_in]])
        if not isinstance(res, (tuple, list)):
            res = (res,)
        assert len(res) == n_out, (name, len(res), n_out)
        for k, (r, v) in enumerate(zip(refs[n_in:], res)):
            if k < n_out - n_acc:
                r[...] = v.astype(r.dtype)
            else:
                @pl.when(i == 0)
                def _():
                    r[...] = v

                @pl.when(i > 0)
                def _():
                    r[...] += v

    return pl.pallas_call(
        body, name=name, grid=(n,), in_specs=list(in_specs), out_specs=tuple(out_specs), out_shape=tuple(outs),
        compiler_params=_params(("arbitrary",), vmem))(*ins)


def _sds(shape, dtype=F32):
    return jax.ShapeDtypeStruct(tuple(shape), dtype)


def _tile_n(n, cap=2816):
    best = None
    for t in range(LANE, min(n, cap) + 1, LANE):
        if n % t == 0:
            best = t
    if best is None or n <= 1408:
        return n
    return best


def _matmul(name, pairs, m, n, *, nt=False, rms_gain=None, resid=None, out_dtype=F32, tm=1024, vmem=56):
    tm = min(tm, m)
    tn = _tile_n(n)
    ks = [a.shape[1] for a, _ in pairs]
    np_ = len(pairs)

    def body(*refs):
        a_refs = refs[:np_]
        b_refs = refs[np_:2 * np_]
        k = 2 * np_
        g_ref = None
        r_ref = None
        if rms_gain is not None:
            g_ref = refs[k]
            k += 1
        if resid is not None:
            r_ref = refs[k]
            k += 1
        o_ref = refs[k]
        scr = refs[k + 1:]
        j = pl.program_id(1)

        @pl.when(j == 0)
        def _():
            for p in range(np_):
                a = a_refs[p][...]
                if p == 0 and g_ref is not None:
                    a = _rms(a.astype(F32), g_ref[...], ks[0])
                scr[p][...] = a.astype(BF16)

        acc = None
        for p in range(np_):
            b = b_refs[p][...].astype(BF16)
            if nt:
                t = lax.dot_general(scr[p][...], b, (((1,), (1,)), ((), ())), preferred_element_type=F32)
            else:
                t = jnp.dot(scr[p][...], b, preferred_element_type=F32)
            acc = t if acc is None else acc + t
        if r_ref is not None:
            acc = acc + r_ref[...]
        o_ref[...] = acc.astype(o_ref.dtype)

    in_specs = [pl.BlockSpec((tm, kk), lambda i, j: (i, 0)) for kk in ks]
    if nt:
        in_specs += [pl.BlockSpec((tn, kk), lambda i, j: (j, 0)) for kk in ks]
    else:
        in_specs += [pl.BlockSpec((kk, tn), lambda i, j: (0, j)) for kk in ks]
    ins = [a for a, _ in pairs] + [b for _, b in pairs]
    if rms_gain is not None:
        in_specs.append(pl.BlockSpec((1, ks[0]), lambda i, j: (0, 0)))
        ins.append(rms_gain)
    if resid is not None:
        in_specs.append(pl.BlockSpec((tm, tn), lambda i, j: (i, j)))
        ins.append(resid)
    return pl.pallas_call(
        body, name=name, grid=(m // tm, n // tn), in_specs=in_specs,
        out_specs=pl.BlockSpec((tm, tn), lambda i, j: (i, j)), out_shape=_sds((m, n), out_dtype),
        scratch_shapes=[pltpu.VMEM((tm, kk), BF16) for kk in ks],
        compiler_params=_params(("arbitrary", "arbitrary"), vmem))(*ins)


def _matmul_tn(name, a, b, *, rms_gain=None, tl=1024, vmem=56):
    l, ka = a.shape
    n = b.shape[1]
    tl = min(tl, l)
    tn = _tile_n(n, 1536)

    def body(*refs):
        if rms_gain is not None:
            a_ref, b_ref, g_ref, o_ref = refs
        else:
            a_ref, b_ref, o_ref = refs
        t = pl.program_id(1)
        av = a_ref[...]
        if rms_gain is not None:
            av = _rms(av.astype(F32), g_ref[...], ka)
        v = _mm_tn(av, b_ref[...])

        @pl.when(t == 0)
        def _():
            o_ref[...] = v

        @pl.when(t > 0)
        def _():
            o_ref[...] += v

    in_specs = [pl.BlockSpec((tl, ka), lambda j, t: (t, 0)), pl.BlockSpec((tl, tn), lambda j, t: (t, j))]
    ins = [a, b]
    if rms_gain is not None:
        in_specs.append(pl.BlockSpec((1, ka), lambda j, t: (0, 0)))
        ins.append(rms_gain)
    return pl.pallas_call(
        body, name=name, grid=(n // tn, l // tl), in_specs=in_specs,
        out_specs=pl.BlockSpec((ka, tn), lambda j, t: (0, j)), out_shape=_sds((ka, n)),
        compiler_params=_params(("arbitrary", "arbitrary"), vmem))(*ins)


ATT_HEADS_PER_STEP = 2
ATT_W = ATT_HEADS_PER_STEP * LANE
ATT_GROUPS = MLA_HEADS // ATT_HEADS_PER_STEP
LOG2E = math.log2(math.e)
ATT_FWD_TILE = 1024
ATT_BWD_TILE = 1024
ATT_BWD_HEADS = 1
ATT_SCALE = D_QK ** -0.5
ATT_QSCALE = ATT_SCALE * LOG2E


def _tri_tables(nq, by_k):
    qs, ks = [], []
    if by_k:
        for ki in range(nq):
            for qi in range(ki, nq):
                qs.append(qi)
                ks.append(ki)
    else:
        for qi in range(nq):
            for ki in range(qi + 1):
                qs.append(qi)
                ks.append(ki)
    return jnp.asarray(np.array(qs, np.int32)), jnp.asarray(np.array(ks, np.int32))


def _causal_keep(shape, transposed):
    r = lax.broadcasted_iota(jnp.int32, shape, 0)
    c = lax.broadcasted_iota(jnp.int32, shape, 1)
    return (r <= c) if transposed else (c <= r)


def _nt16(a, b):
    return lax.dot_general(a, b, (((1,), (1,)), ((), ())), preferred_element_type=F32)


def _row_form(col):
    return jnp.transpose(jnp.broadcast_to(col, (col.shape[0], LANE)))[:8]


def _att_call(body, name, l, tq, tabs, ins, in_specs, outs, out_specs, scratch=(), groups=ATT_GROUPS, vmem=48):
    grid_spec = pltpu.PrefetchScalarGridSpec(
        num_scalar_prefetch=2, grid=(groups, tabs[0].shape[0]), in_specs=in_specs, out_specs=out_specs,
        scratch_shapes=list(scratch))
    return pl.pallas_call(body, name=name, grid_spec=grid_spec, out_shape=outs,
                          compiler_params=_params(("arbitrary", "arbitrary"), vmem))(*tabs, *ins)


def _flash_fwd(name, q, k, v_t, gather=None):
    l = q.shape[0]
    tq = min(ATT_FWD_TILE, l)
    nq = l // tq
    tabs = _tri_tables(nq, by_k=False)
    gx, glayer = gather if gather is not None else ([], 0)
    ng = len(gx)
    n_steps = int(tabs[0].shape[0])

    def body(qt, kt, *refs):
        q_ref, k_ref, vt_ref = refs[:3]
        gx_refs = refs[3:3 + ng]
        o_ref, lset_ref = refs[3 + ng:5 + ng]
        gy_refs = refs[5 + ng:5 + 2 * ng]
        m_s, acc_s = refs[5 + 2 * ng:7 + 2 * ng]
        t = pl.program_id(1)
        qi = qt[t]
        ki = kt[t]
        sls = [slice(h * LANE, (h + 1) * LANE) for h in range(ATT_HEADS_PER_STEP)]
        if ng:
            g_start, g_finish = _layer_gather_steps(gx_refs, gy_refs, refs[7 + 2 * ng:9 + 2 * ng], glayer)

            @pl.when((pl.program_id(0) == 0) & (t == 0))
            def _():
                g_start()

        @pl.when(ki == 0)
        def _():
            m_s[...] = jnp.full(m_s.shape, NEG, F32)
            acc_s[...] = jnp.zeros(acc_s.shape, F32)

        def step(masked):
            sts = [_nt16(k_ref[:, sl], q_ref[:, sl]) for sl in sls]
            for h, sl in enumerate(sls):
                st = sts[h]
                if masked:
                    st = jnp.where(_causal_keep(st.shape, True), st, NEG)
                m_old = m_s[h][:1]
                m_new = jnp.maximum(m_old, jnp.max(st, axis=0, keepdims=True))
                alpha = jnp.exp2(m_old - m_new)
                pt = jnp.exp2(st - m_new).astype(BF16)
                acc_s[sl, :] = alpha * acc_s[sl, :] + jnp.dot(vt_ref[sl, :], pt, preferred_element_type=F32)
                m_s[h] = jnp.broadcast_to(m_new, (8, tq))

        @pl.when(ki < qi)
        def _():
            step(False)

        @pl.when(ki == qi)
        def _():
            step(True)
            row = lax.broadcasted_iota(jnp.int32, (LANE, tq), 0)
            for h, sl in enumerate(sls):
                acc = acc_s[sl, :]
                lsum = acc[D_V:D_V + 1, :]
                o_ref[:, sl] = jnp.transpose(jnp.where(row < D_V, acc / lsum, 0.0))
                lset_ref[h * 8:(h + 1) * 8, :] = m_s[h] + jnp.log2(lsum)

        if ng:
            @pl.when((pl.program_id(0) == ATT_GROUPS - 1) & (t == n_steps - 1))
            def _():
                g_finish()

    qspec = pl.BlockSpec((tq, ATT_W), lambda g, t, qt, kt: (qt[t], g))
    kspec = pl.BlockSpec((tq, ATT_W), lambda g, t, qt, kt: (kt[t], g))
    vspec = pl.BlockSpec((ATT_W, tq), lambda g, t, qt, kt: (g, kt[t]))
    rspec = pl.BlockSpec((8 * ATT_HEADS_PER_STEP, tq), lambda g, t, qt, kt: (g, qt[t]))
    res = _att_call(
        body, name, l, tq, tabs, [q, k, v_t, *gx], [qspec, kspec, vspec] + [ANY] * ng,
        (_sds((l, MLA_PAD)), _sds((8 * MLA_HEADS, l)), *[_sds((4,) + x.shape[1:], x.dtype) for x in gx]),
        (qspec, rspec, *([ANY] * ng)),
        scratch=[pltpu.VMEM((ATT_HEADS_PER_STEP, 8, tq), F32), pltpu.VMEM((ATT_W, tq), F32)]
        + ([pltpu.SemaphoreType.DMA((6 * ng,)), pltpu.SemaphoreType.DMA((6 * ng,))] if ng else []))
    return res[0], res[1], list(res[2:])


def _flash_bwd(name, q, k, v, k_t, o, lse_t, do, reduce=None):
    l = q.shape[0]
    tq = min(ATT_BWD_TILE, l)
    nq = l // tq
    hb = ATT_BWD_HEADS
    wb = hb * LANE

    def delta_fn(i, dov, ov):
        rows = []
        for h in range(MLA_HEADS):
            sl = slice(h * LANE, (h + 1) * LANE)
            rows.append(_row_form(jnp.sum(dov[:, sl] * ov[:, sl], axis=-1, keepdims=True)))
        return jnp.concatenate(rows, axis=0), dov

    delta_t, do16 = _rows(
        delta_fn, name=name + "_delta", n=nq, ins=[do, o], in_specs=[_rt(tq, MLA_PAD)] * 2,
        outs=[_sds((8 * MLA_HEADS, l)), _sds((l, MLA_PAD), BF16)],
        out_specs=[pl.BlockSpec((8 * MLA_HEADS, tq), lambda i: (0, i)), _rt(tq, MLA_PAD)])

    rg, rlayer = reduce if reduce is not None else ([], 0)
    nr = len(rg)
    groups = MLA_HEADS // hb
    tabs_k = _tri_tables(nq, by_k=True)
    n_steps = int(tabs_k[0].shape[0])

    def body(qt, kt, *refs):
        q_ref, k_ref, v_ref, do_ref, kt_ref, lset_ref, dlt_ref = refs[:7]
        rg_refs = refs[7:7 + nr]
        dk_ref, dv_ref, dqt_ref = refs[7 + nr:10 + nr]
        ry_refs = refs[10 + nr:10 + 2 * nr]
        t = pl.program_id(1)
        qi = qt[t]
        ki = kt[t]
        sls = [slice(h * LANE, (h + 1) * LANE) for h in range(hb)]
        if nr:
            r_start, r_finish = _layer_reduce_steps(rg_refs, ry_refs, refs[10 + 2 * nr:12 + 2 * nr], rlayer)

            @pl.when((pl.program_id(0) == 0) & (t == 0))
            def _():
                r_start()

        @pl.when(ki == 0)
        def _():
            dqt_ref[qi] = jnp.zeros((wb, tq), F32)

        def step(masked):
            sts = [_nt16(k_ref[:, sl], q_ref[:, sl]) for sl in sls]
            dpts = [_nt16(v_ref[:, sl], do_ref[:, sl]) for sl in sls]
            for h, sl in enumerate(sls):
                st = sts[h]
                if masked:
                    st = jnp.where(_causal_keep(st.shape, True), st, NEG)
                pt = jnp.exp2(st - lset_ref[h * 8:(h + 1) * 8, :][:1])
                dst = (pt * (dpts[h] - dlt_ref[h * 8:(h + 1) * 8, :][:1])).astype(BF16)
                dv_ref[:, sl] += jnp.dot(pt.astype(BF16), do_ref[:, sl], preferred_element_type=F32)
                dk_ref[:, sl] += jnp.dot(dst, q_ref[:, sl], preferred_element_type=F32)
                dqt_ref[qi, sl, :] += jnp.dot(kt_ref[sl, :], dst, preferred_element_type=F32)

        @pl.when(qi == ki)
        def _():
            dk_ref[...] = jnp.zeros(dk_ref.shape, F32)
            dv_ref[...] = jnp.zeros(dv_ref.shape, F32)
            step(True)
            dqt_ref[qi] = dqt_ref[qi] * ATT_SCALE

        @pl.when(qi > ki)
        def _():
            step(False)

        @pl.when(qi == nq - 1)
        def _():
            dk_ref[...] = dk_ref[...] * (1.0 / LOG2E)

        if nr:
            @pl.when((pl.program_id(0) == groups - 1) & (t == n_steps - 1))
            def _():
                r_finish()

    qspec = pl.BlockSpec((tq, wb), lambda g, t, qt, kt: (qt[t], g))
    kspec = pl.BlockSpec((tq, wb), lambda g, t, qt, kt: (kt[t], g))
    ktspec = pl.BlockSpec((wb, tq), lambda g, t, qt, kt: (g, kt[t]))
    rspec = pl.BlockSpec((8 * hb, tq), lambda g, t, qt, kt: (g, qt[t]))
    dqspec = pl.BlockSpec((nq, wb, tq), lambda g, t, qt, kt: (0, g, 0))
    res = _att_call(body, name + "_dqkv", l, tq, tabs_k, [q, k, v, do16, k_t, lse_t, delta_t, *rg],
                    [qspec, kspec, kspec, qspec, ktspec, rspec, rspec] + [ANY] * nr,
                    (_sds((l, MLA_PAD)), _sds((l, MLA_PAD)), _sds((nq, MLA_PAD, tq)),
                     *[_sds((3,) + g.shape[1:], g.dtype) for g in rg]),
                    (kspec, kspec, dqspec, *([ANY] * nr)),
                    scratch=([pltpu.SemaphoreType.DMA((3 * nr,)), pltpu.SemaphoreType.DMA((3 * nr,))] if nr else []),
                    groups=groups, vmem=56)
    dk, dv, dq_t = res[:3]
    return dq_t, dk, dv, list(res[3:])


def _cmul(ar, ai, br, bi):
    return ar * br - ai * bi, ar * bi + ai * br


def _scan(name, x_re, x_im, a_re, a_im, reverse):
    l, lanes = x_re.shape
    ns = SCAN_SEGS
    tl = l // ns
    steps = int(math.log2(tl))
    assert 2 ** steps == tl and tl * ns == l

    def body(xr_ref, xi_ref, ar_ref, ai_ref, sr_ref, si_ref):
        a_r1 = ar_ref[...]
        a_i1 = ai_ref[...]
        a_r = jnp.broadcast_to(a_r1, (ns, LANE))
        a_i = jnp.broadcast_to(a_i1, (ns, LANE))

        def rows(t):
            t = (tl - 1 - t) if reverse else t
            return pl.ds(pl.multiple_of(t * ns, ns), ns)

        def local(t, carry):
            cr, ci = carry
            r = rows(t)
            pr, pi = _cmul(a_r, a_i, cr, ci)
            return pr + xr_ref[r, :].astype(F32), pi + xi_ref[r, :].astype(F32)

        zero = jnp.zeros((ns, LANE), F32)
        e_r, e_i = lax.fori_loop(0, tl, local, (zero, zero), unroll=min(8, tl))
        p_r, p_i = a_r1, a_i1
        for _ in range(steps):
            p_r, p_i = _cmul(p_r, p_i, p_r, p_i)
        rid = lax.broadcasted_iota(jnp.int32, (ns, LANE), 0)
        c_r = jnp.zeros((1, LANE), F32)
        c_i = jnp.zeros((1, LANE), F32)
        in_r, in_i = zero, zero
        order = range(ns - 2, -1, -1) if reverse else range(1, ns)
        for kk in order:
            src = kk + 1 if reverse else kk - 1
            ek_r = jnp.sum(jnp.where(rid == src, e_r, 0.0), axis=0, keepdims=True)
            ek_i = jnp.sum(jnp.where(rid == src, e_i, 0.0), axis=0, keepdims=True)
            q_r, q_i = _cmul(p_r, p_i, c_r, c_i)
            c_r, c_i = q_r + ek_r, q_i + ek_i
            in_r = jnp.where(rid == kk, jnp.broadcast_to(c_r, (ns, LANE)), in_r)
            in_i = jnp.where(rid == kk, jnp.broadcast_to(c_i, (ns, LANE)), in_i)

        def final(t, carry):
            cr, ci = carry
            r = rows(t)
            pr, pi = _cmul(a_r, a_i, cr, ci)
            nr, ni = pr + xr_ref[r, :].astype(F32), pi + xi_ref[r, :].astype(F32)
            sr_ref[r, :] = nr.astype(sr_ref.dtype)
            si_ref[r, :] = ni.astype(si_ref.dtype)
            return nr, ni

        lax.fori_loop(0, tl, final, (in_r, in_i), unroll=min(8, tl))

    xs = pl.BlockSpec((l, LANE), lambda j: (0, j))
    as_ = pl.BlockSpec((1, LANE), lambda j: (0, j))
    return pl.pallas_call(
        body, name=name, grid=(lanes // LANE,), in_specs=[xs, xs, as_, as_], out_specs=(xs, xs),
        out_shape=(_sds((l, lanes), x_re.dtype), _sds((l, lanes), x_re.dtype)),
        compiler_params=_params(("arbitrary",), 48))(x_re, x_im, a_re, a_im)


ANY = pl.BlockSpec(memory_space=pl.ANY)


def _place():
    mx, my, mc = lax.axis_index("x"), lax.axis_index("y"), lax.axis_index("c")
    return mx, my, mc, [(1 - mx, my), (mx, 1 - my), (1 - mx, 1 - my)]


def _run_copies(copies):
    for cp in copies:
        cp.start()
    for cp in copies:
        cp.wait_recv()
    for cp in copies:
        cp.wait_send()


def _remote(src, dst, sems, k, dev):
    return pltpu.make_async_remote_copy(src_ref=src, dst_ref=dst, send_sem=sems[0].at[k], recv_sem=sems[1].at[k],
                                        device_id=dev, device_id_type=MESH)


def _copy_call(body, name, ins, outs, n_copies, aliases=None):
    return pl.pallas_call(
        body, name=name, in_specs=[ANY] * len(ins), out_specs=[ANY] * len(outs), out_shape=list(outs),
        input_output_aliases=aliases or {},
        scratch_shapes=[pltpu.SemaphoreType.DMA((n_copies,)), pltpu.SemaphoreType.DMA((n_copies,))])(*ins)


def _layer_gather_steps(x_refs, y_refs, sems, layer):
    n = len(x_refs)
    mx, my, mc, peers = _place()
    me = 2 * mx + my

    def ici(idx):
        i, j = divmod(idx, 3)
        return _remote(x_refs[i].at[layer], y_refs[i].at[me], sems, idx, (peers[j][0], peers[j][1], layer))

    def fwd(idx):
        i, j = divmod(idx, 3)
        pk = 2 * peers[j][0] + peers[j][1]
        return _remote(y_refs[i].at[pk], y_refs[i].at[pk], sems, 3 * n + idx, (mx, my, 1 - layer))

    def start():
        @pl.when(mc == layer)
        def _():
            for idx in range(3 * n):
                ici(idx).start()

    def finish():
        @pl.when(mc == layer)
        def _():
            for idx in range(3 * n):
                ici(idx).wait_recv()
                fwd(idx).start()
            for idx in range(3 * n):
                ici(idx).wait_send()
                fwd(idx).wait_send()

        @pl.when(mc != layer)
        def _():
            for idx in range(3 * n):
                fwd(idx).wait_recv()

    return start, finish


def _layer_gather(name, xs, layer):
    n = len(xs)

    def body(*refs):
        start, finish = _layer_gather_steps(refs[:n], refs[n:2 * n], refs[2 * n:], layer)
        start()
        finish()

    return _copy_call(body, name, xs, [_sds((4,) + x.shape[1:], x.dtype) for x in xs], 6 * n)


def _layer_reduce_steps(g_refs, y_refs, sems, layer):
    n = len(g_refs)
    mx, my, mc, peers = _place()

    def cp(idx):
        i, j = divmod(idx, 3)
        px, py = peers[j]
        return _remote(g_refs[i].at[2 * px + py], y_refs[i].at[j], sems, idx, (px, py, layer))

    def start():
        @pl.when(mc == layer)
        def _():
            for idx in range(3 * n):
                cp(idx).start()

    def finish():
        @pl.when(mc == layer)
        def _():
            for idx in range(3 * n):
                cp(idx).wait_recv()
            for idx in range(3 * n):
                cp(idx).wait_send()

    return start, finish


def _layer_reduce(name, gs, layer, small):
    n = len(gs)

    def body(*refs):
        g_refs, s_ref, y_refs, a_ref = refs[:n], refs[n], refs[n + 1:2 * n + 1], refs[2 * n + 1]
        sems, small_sems, local_sem = refs[2 * n + 2:2 * n + 4], refs[2 * n + 4:2 * n + 6], refs[2 * n + 6]
        mx, my, mc, _ = _place()
        me8 = 4 * mx + 2 * my + mc
        start, finish = _layer_reduce_steps(g_refs, y_refs, sems, layer)
        start()
        own = pltpu.make_async_copy(s_ref, a_ref.at[me8], local_sem)
        own.start()
        copies = []
        for j in range(1, 8):
            dev = ((1 - mx) if (j & 4) else mx, (1 - my) if (j & 2) else my, (1 - mc) if (j & 1) else mc)
            copies.append(_remote(s_ref, a_ref.at[me8], small_sems, j - 1, dev))
        _run_copies(copies)
        own.wait()
        finish()

    outs = [_sds((3,) + g.shape[1:], g.dtype) for g in gs] + [_sds((8,) + small.shape, small.dtype)]
    res = pl.pallas_call(
        body, name=name, in_specs=[ANY] * (n + 1), out_specs=[ANY] * (n + 1), out_shape=outs,
        scratch_shapes=[pltpu.SemaphoreType.DMA((3 * n,)), pltpu.SemaphoreType.DMA((3 * n,)),
                        pltpu.SemaphoreType.DMA((7,)), pltpu.SemaphoreType.DMA((7,)), pltpu.SemaphoreType.DMA])(*gs, small)
    return res[:n], res[n]


def _send_d2d(name, per_core):
    shapes = next(p for p in per_core if p is not None)
    n = len(shapes)
    ins = [x for p in per_core if p is not None for x in p]

    def body(*refs):
        o_refs, sems = refs[len(ins):len(ins) + n], refs[len(ins) + n:]
        mx, my, mc, _ = _place()
        off = 0
        for c in range(2):
            if per_core[c] is None:
                continue
            src = refs[off:off + n]
            off += n

            @pl.when(mc == c)
            def _():
                cps = [_remote(src[i], o_refs[i], sems, i, (mx, my, 1 - c)) for i in range(n)]
                for d in cps:
                    d.start()
                if per_core[1 - c] is not None:
                    for d in cps:
                        d.wait_recv()
                for d in cps:
                    d.wait_send()

            if per_core[1 - c] is None:
                @pl.when(mc != c)
                def _():
                    for i in range(n):
                        _remote(src[i], o_refs[i], sems, i, (mx, my, c)).wait_recv()

    return _copy_call(body, name, ins, [_sds(p.shape, p.dtype) for p in shapes], n)


PACK_W = 1024


def _pack(arrs, rows_multiple, dtype):
    flat = jnp.concatenate([a.reshape(-1).astype(dtype) for a in arrs])
    n = flat.shape[0]
    unit = PACK_W * rows_multiple
    tot = -(-n // unit) * unit
    flat = jnp.pad(flat, (0, tot - n))
    return flat.reshape(tot // PACK_W, PACK_W)


def _unpack(flat, shapes):
    flat = flat.reshape(-1)
    out = []
    off = 0
    for s in shapes:
        n = int(np.prod(s))
        out.append(flat[off:off + n].reshape(s))
        off += n
    return out


def _rope_tables(pos):
    l = pos.shape[0]
    tm = min(512, l)
    inv = (np.float32(ROPE_THETA) ** (-np.arange(0, D_ROPE, 2, dtype=np.float32) / np.float32(D_ROPE))).astype(np.float32)
    lane_f = np.zeros((1, LANE), np.float32)
    lane_f[0, D_NOPE:D_NOPE + 16] = inv
    lane_f[0, D_NOPE + 16:D_NOPE + 32] = inv

    def fn(i, p, f):
        ang = p * f
        lane = lax.broadcasted_iota(jnp.int32, ang.shape, 1)
        co = jnp.cos(ang)
        si = jnp.sin(ang)
        c = jnp.where(lane < D_NOPE, 1.0, jnp.where(lane < D_QK, co, 0.0))
        s1 = jnp.where((lane >= D_NOPE) & (lane < D_NOPE + 16), -si, 0.0)
        s2 = jnp.where((lane >= D_NOPE + 16) & (lane < D_QK), si, 0.0)
        return c, s1, s2

    return _rows(fn, name="rope_tables", n=l // tm, ins=[pos, jnp.asarray(lane_f)],
                 in_specs=[_rt(tm, 1), _full((1, LANE))], outs=[_sds((l, LANE))] * 3, out_specs=[_rt(tm, LANE)] * 3)


def _ssm_param_fn(lr, li, log_dt, br, bi):
    dt = jnp.exp(log_dt)
    mag = jnp.exp(lr * dt)
    a_re = mag * jnp.cos(li * dt)
    a_im = mag * jnp.sin(li * dt)
    den = lr * lr + li * li
    e_re = a_re - 1.0
    e_im = a_im
    f_re = (e_re * lr + e_im * li) / den
    f_im = (e_im * lr - e_re * li) / den
    bb_re = f_re[None] * br - f_im[None] * bi
    bb_im = f_re[None] * bi + f_im[None] * br
    return a_re, a_im, bb_re, bb_im


def _ssm_params(name, lr, li, log_dt, br, bi):
    g, n = lr.shape
    c = br.shape[0]
    return _rows(lambda i, *v: _ssm_param_fn(*v), name=name, n=1, ins=[lr, li, log_dt, br, bi],
                 in_specs=[_full((g, n)), _full((g, n)), _full((g, 1)), _full((c, g, n)), _full((c, g, n))],
                 outs=[_sds((g, n)), _sds((g, n)), _sds((c, g, n)), _sds((c, g, n))],
                 out_specs=[_full((g, n)), _full((g, n)), _full((c, g, n)), _full((c, g, n))])


def _ssm_params_bwd(name, lr, li, log_dt, br, bi, d_are, d_aim, d_bbre, d_bbim):
    g, n = lr.shape
    c = br.shape[0]

    def fn(i, lr, li, log_dt, br, bi, g0, g1, g2, g3):
        _, vjp = jax.vjp(_ssm_param_fn, lr, li, log_dt, br, bi)
        return vjp((g0, g1, g2, g3))

    sp = [_full((g, n)), _full((g, n)), _full((g, 1)), _full((c, g, n)), _full((c, g, n))]
    return _rows(fn, name=name, n=1, ins=[lr, li, log_dt, br, bi, d_are, d_aim, d_bbre, d_bbim],
                 in_specs=sp + [_full((g, n)), _full((g, n)), _full((c, g, n)), _full((c, g, n))],
                 outs=[_sds((g, n)), _sds((g, n)), _sds((g, 1)), _sds((c, g, n)), _sds((c, g, n))], out_specs=sp)


_EYE8 = np.eye(8, dtype=np.float32)


def _blockdiag(v):
    j, g, p, q = v.shape
    m = v[:, :, :, None, :] * jnp.asarray(_EYE8)[None, :, None, :, None]
    return m.reshape(j, g * p, g * q)


def _blockdiag_t(m, p, q):
    j = m.shape[0]
    m = m.reshape(j, 8, p, 8, q)
    return jnp.sum(m * jnp.asarray(_EYE8)[None, :, None, :, None], axis=3)


def _to_perm(v, l):
    ns = SCAN_SEGS
    return v.reshape(ns, l // ns, v.shape[-1]).transpose(1, 0, 2).reshape(l, v.shape[-1])


def _from_perm(v, l):
    ns = SCAN_SEGS
    return v.reshape(l // ns, ns, v.shape[-1]).transpose(1, 0, 2).reshape(l, v.shape[-1])


def _prep_layer(w, i):
    p = {}
    w_in = w['w_in'][i]
    z = lambda n: jnp.zeros((D_MODEL, n), w_in.dtype)
    o = Q_LORA + KV_LORA
    p['w_s'] = jnp.concatenate([w_in[:, :o], z(D_NOPE), w_in[:, o:o + D_ROPE], z(HEAD_PAD - D_QK)], axis=1)
    o += D_ROPE
    p['w_u'] = w_in[:, o:o + SSM_WIDTH]
    o += SSM_WIDTH
    p['w_xq'] = w_in[:, o:o + X_WIDTH]
    o += X_WIDTH
    p['w_g'] = w_in[:, o:]
    wq = w['w_q_b'][i].reshape(Q_LORA, MLA_HEADS, D_QK)
    p['w_qb'] = jnp.pad(wq, ((0, 0), (0, 0), (0, HEAD_PAD - D_QK))).reshape(Q_LORA, MLA_PAD)
    wkv = w['w_kv_b'][i].reshape(KV_LORA, MLA_HEADS, D_NOPE + D_V)
    p['w_k'] = jnp.pad(wkv[:, :, :D_NOPE], ((0, 0), (0, 0), (0, HEAD_PAD - D_NOPE))).reshape(KV_LORA, MLA_PAD)
    p['w_v'] = jnp.pad(wkv[:, :, D_NOPE:], ((0, 0), (0, 0), (0, HEAD_PAD - D_V))).reshape(KV_LORA, MLA_PAD)
    wo = w['w_o_mla'][i].reshape(MLA_HEADS, D_V, D_MODEL)
    p['w_oa'] = jnp.pad(wo, ((0, 0), (0, HEAD_PAD - D_V), (0, 0))).reshape(MLA_PAD, D_MODEL)
    for n in ('w_glu', 'w_o_ssm', 'w_mem_kv', 'w_o_cross', 'w_out', 'w_up', 'w_down'):
        p[n] = w[n][i]
    p['conv_w'] = w['conv_w'][i]
    for n in ('norm_mix_g', 'q_a_norm_g', 'kv_a_norm_g', 'b_glu', 'mem_norm_g', 'xq_norm_g', 'xk_norm_g', 'b_gate',
              'norm_ffn_g', 'conv_b'):
        p[n] = w[n][i].reshape(1, -1)
    p['q_norm_g'] = jnp.pad(w['q_norm_g'][i], (0, HEAD_PAD - D_QK)).reshape(1, HEAD_PAD)
    p['k_norm_g'] = jnp.pad(w['k_norm_g'][i], (0, HEAD_PAD - D_QK)).reshape(1, HEAD_PAD)
    p['ssm_d'] = w['ssm_d'][i].reshape(1, SSM_WIDTH)
    p['lr'] = w['ssm_lambda_re'][i]
    p['li'] = w['ssm_lambda_im'][i]
    p['log_dt'] = w['ssm_log_dt'][i].reshape(SSM_GROUPS, 1)
    p['br'] = w['ssm_b_re'][i].transpose(2, 0, 1)
    p['bi'] = w['ssm_b_im'][i].transpose(2, 0, 1)
    cr = w['ssm_c_re'][i].reshape(SSM_JB, 8, SSM_GROUP_CH, SSM_STATE).transpose(0, 1, 3, 2)
    ci = w['ssm_c_im'][i].reshape(SSM_JB, 8, SSM_GROUP_CH, SSM_STATE).transpose(0, 1, 3, 2)
    p['c_mat'] = jnp.concatenate([_blockdiag(cr), -_blockdiag(ci)], axis=1).astype(BF16)
    return p


def _b_mat(bb_re, bb_im):
    r = bb_re.transpose(1, 0, 2).reshape(SSM_JB, 8, SSM_GROUP_CH, SSM_STATE)
    i = bb_im.transpose(1, 0, 2).reshape(SSM_JB, 8, SSM_GROUP_CH, SSM_STATE)
    return jnp.concatenate([_blockdiag(r), _blockdiag(i)], axis=2).astype(BF16)


def _qkv_fn(ps, c, s1, s2, qag, wqb, kvag, wk, wv, qng, kng):
    c_q = ps[:, :Q_LORA]
    c_kv = ps[:, Q_LORA:Q_LORA + KV_LORA]
    kr = ps[:, Q_LORA + KV_LORA:]
    cqn = _rms(c_q, qag, Q_LORA)
    ckvn = _rms(c_kv, kvag, KV_LORA)
    q_raw = _mm(cqn, wqb)
    k_raw = _mm(ckvn, wk) + jnp.concatenate([kr] * MLA_HEADS, axis=-1)
    v = _mm(ckvn, wv)
    q = _heads(_rope, _head_rms(q_raw, qng, MLA_HEADS, D_QK), MLA_HEADS, c, s1, s2)
    k = _heads(_rope, _head_rms(k_raw, kng, MLA_HEADS, D_QK), MLA_HEADS, c, s1, s2)
    lane = lax.broadcasted_iota(jnp.int32, v.shape, 1)
    v = jnp.where((lane & (LANE - 1)) == D_V, 1.0, v)
    return q * ATT_QSCALE, k, v


def _layer_fwd(name, x, tabs, mem, p, gather=None):
    l = x.shape[0]
    tm = min(512, l)
    nt = l // tm
    sv = {'x0': x}
    sv['p_g'] = _matmul(name + "_in_g", [(x, p['w_g'])], l, 3 * D_MODEL, rms_gain=p['norm_mix_g'])
    sv['p_u'] = _matmul(name + "_in_u", [(x, p['w_u'])], l, SSM_WIDTH, rms_gain=p['norm_mix_g'])
    sv['p_xq'] = _matmul(name + "_in_xq", [(x, p['w_xq'])], l, X_WIDTH, rms_gain=p['norm_mix_g'])
    sv['p_s'] = _matmul(name + "_in_s", [(x, p['w_s'])], l, SMALL_W, rms_gain=p['norm_mix_g'])

    qkv_consts = [p['q_a_norm_g'], p['w_qb'], p['kv_a_norm_g'], p['w_k'], p['w_v'], p['q_norm_g'], p['k_norm_g']]
    qkv_cspecs = [_full(a.shape) for a in qkv_consts]
    def qkv_fwd(i, *a):
        qv, kv, vv = _qkv_fn(*a)
        return qv, kv, vv, jnp.transpose(kv), jnp.transpose(vv)

    q, k, v, k_t, v_t = _rows(qkv_fwd, name=name + "_qkv", n=nt, ins=[sv['p_s'], *tabs, *qkv_consts],
                              in_specs=[_rt(tm, SMALL_W)] + [_rt(tm, LANE)] * 3 + qkv_cspecs,
                              outs=[_sds((l, MLA_PAD), BF16)] * 3 + [_sds((MLA_PAD, l), BF16)] * 2,
                              out_specs=[_rt(tm, MLA_PAD)] * 3 + [pl.BlockSpec((MLA_PAD, tm), lambda i: (0, i))] * 2)
    sv['q'], sv['k'], sv['v'], sv['k_t'] = q, k, v, k_t
    sv['o_a'], sv['lse_t'], sv['gathered'] = _flash_fwd(name + "_attn", q, k, v_t, gather=gather)

    a_re, a_im, bb_re, bb_im = _ssm_params(name + "_ssm_par", p['lr'], p['li'], p['log_dt'], p['br'], p['bi'])
    sv['a_re'], sv['a_im'] = a_re.reshape(1, SSM_LANES), a_im.reshape(1, SSM_LANES)
    sv['b_mat'] = _b_mat(bb_re, bb_im)
    u_p = _to_perm(sv['p_u'], l)
    sv['u_p'] = u_p

    def bu_fn(i, u, bm):
        res = [_mm(u[:, j * LANE:(j + 1) * LANE], bm[j]) for j in range(SSM_JB)]
        return (jnp.concatenate([r[:, :512] for r in res], axis=-1), jnp.concatenate([r[:, 512:] for r in res], axis=-1))

    ts = min(256, l)
    bu_re, bu_im = _rows(bu_fn, name=name + "_ssm_bu", n=l // ts, ins=[u_p, sv['b_mat']],
                         in_specs=[_rt(ts, SSM_WIDTH), _full(sv['b_mat'].shape)],
                         outs=[_sds((l, SSM_LANES), SSM_STATE_DTYPE)] * 2, out_specs=[_rt(ts, SSM_LANES)] * 2)
    s_re, s_im = _scan(name + "_ssm_scan", bu_re, bu_im, sv['a_re'], sv['a_im'], reverse=False)
    sv['s_re'], sv['s_im'] = s_re, s_im

    def glu_fn(i, sr, si, u, cm, dsk, wg, bg):
        y = jnp.concatenate([_mm(jnp.concatenate([sr[:, j * 512:(j + 1) * 512], si[:, j * 512:(j + 1) * 512]], axis=-1),
                                 cm[j]) for j in range(SSM_JB)], axis=-1) + dsk * u
        zz = _gelu(y)
        return zz * jax.nn.sigmoid(_mm(zz, wg) + bg)

    glu_consts = [p['c_mat'], p['ssm_d'], p['w_glu'], p['b_glu']]
    zo_p = _rows(glu_fn, name=name + "_ssm_glu", n=l // ts, ins=[s_re, s_im, u_p, *glu_consts],
                 in_specs=[_rt(ts, SSM_LANES), _rt(ts, SSM_LANES), _rt(ts, SSM_WIDTH)] + [_full(a.shape) for a in glu_consts],
                 outs=[_sds((l, SSM_WIDTH), BF16)], out_specs=[_rt(ts, SSM_WIDTH)])[0]
    sv['zo'] = _from_perm(zo_p, l)

    m_len = mem.shape[0]

    def memkv_fn(i, mm_, mg, wmk, xkg):
        kv = _mm(_rms(mm_, mg, D_MODEL), wmk)
        return _head_rms(kv[:, :X_WIDTH], xkg, X_HEADS, X_HEAD_DIM), kv[:, X_WIDTH:]

    mem_consts = [p['mem_norm_g'], p['w_mem_kv'], p['xk_norm_g']]
    k_c, v_c = _rows(memkv_fn, name=name + "_memkv", n=1, ins=[mem, *mem_consts],
                     in_specs=[_full(mem.shape)] + [_full(a.shape) for a in mem_consts],
                     outs=[_sds((m_len, X_WIDTH))] * 2, out_specs=[_full((m_len, X_WIDTH))] * 2)
    sv['k_c'], sv['v_c'] = k_c, v_c

    def cross_fn(i, xq, kc, vc, xqg):
        outs = []
        for h in range(X_HEADS):
            sl = slice(h * LANE, (h + 1) * LANE)
            qh = _rms(xq[:, sl], xqg, X_HEAD_DIM)
            s = _mm_nt(qh, kc[:, sl]) * (X_HEAD_DIM ** -0.5)
            s = s - jnp.max(s, axis=-1, keepdims=True)
            e = jnp.exp(s)
            pr = e / jnp.sum(e, axis=-1, keepdims=True)
            outs.append(_mm(pr, vc[:, sl]))
        return jnp.concatenate(outs, axis=-1)

    sv['o_c'] = _rows(cross_fn, name=name + "_cross", n=nt, ins=[sv['p_xq'], k_c, v_c, p['xq_norm_g']],
                      in_specs=[_rt(tm, X_WIDTH), _full(k_c.shape), _full(v_c.shape), _full((1, LANE))],
                      outs=[_sds((l, X_WIDTH), BF16)], out_specs=[_rt(tm, X_WIDTH)])[0]

    def merge_fn(i, oa, zo, oc, pg, x0, woa, wos, woc, bg, wout):
        gates = jax.nn.sigmoid(pg + bg)
        merged = (gates[:, :D_MODEL] * _mm(oa, woa) + gates[:, D_MODEL:2 * D_MODEL] * _mm(zo, wos)
                  + gates[:, 2 * D_MODEL:] * _mm(oc, woc))
        return x0 + _mm(merged, wout), merged

    merge_consts = [p['w_oa'], p['w_o_ssm'], p['w_o_cross'], p['b_gate'], p['w_out']]
    tg = min(256, l)
    x1, merged = _rows(merge_fn, name=name + "_merge", n=l // tg, ins=[sv['o_a'], sv['zo'], sv['o_c'], sv['p_g'], x, *merge_consts],
                       in_specs=[_rt(tg, MLA_PAD), _rt(tg, SSM_WIDTH), _rt(tg, X_WIDTH), _rt(tg, 3 * D_MODEL), _rt(tg, D_MODEL)]
                       + [_full(a.shape) for a in merge_consts],
                       outs=[_sds((l, D_MODEL)), _sds((l, D_MODEL), BF16)], out_specs=[_rt(tg, D_MODEL)] * 2)
    sv['x1'], sv['merged'] = x1, merged

    up = _matmul(name + "_up", [(x1, p['w_up'])], l, 2 * D_FF, rms_gain=p['norm_ffn_g'])
    sv['up'] = up
    tc = min(128, l)

    def conv_fn(i, upt, halo, cw, cb):
        upc = _conv(i, upt, halo, cw) + cb
        return _silu(upc[:, :D_FF]) * upc[:, D_FF:]

    act = _rows(conv_fn, name=name + "_conv", n=l // tc, ins=[up, up, p['conv_w'], p['conv_b']],
                in_specs=[_rt(tc, 2 * D_FF), _halo_prev(tc, 2 * D_FF), _full((3, 2 * D_FF)), _full((1, 2 * D_FF))],
                outs=[_sds((l, D_FF), BF16)], out_specs=[_rt(tc, D_FF)])[0]
    sv['act'] = act
    x2 = _matmul(name + "_down", [(act, p['w_down'])], l, D_MODEL, resid=x1)
    return x2, sv


def _halo_prev(tm, w):
    return pl.BlockSpec((8, w), lambda i: (jnp.maximum(i * (tm // 8) - 1, 0), 0))


def _halo_next(tm, w, n_tiles):
    last = n_tiles * (tm // 8) - 1
    return pl.BlockSpec((8, w), lambda i: (jnp.minimum((i + 1) * (tm // 8), last), 0))


def _conv(i, tile, halo, cw):
    halo = jnp.where(i > 0, halo, 0.0)
    ext = jnp.concatenate([halo, tile], axis=0)
    n = ext.shape[0]
    x1 = pltpu.roll(ext, 1, 0)[8:]
    x2 = pltpu.roll(ext, 2, 0)[8:]
    del n
    return cw[0:1] * x2 + cw[1:2] * x1 + cw[2:3] * tile


def _layer_bwd(name, dx2, sv, tabs, mem, p, reduce=None):
    l = dx2.shape[0]
    tm = min(512, l)
    nt = l // tm
    g = {}
    x1 = sv['x1']
    dact = _matmul(name + "_b_down", [(dx2, p['w_down'])], l, D_FF, nt=True)
    g['w_down'] = _matmul_tn(name + "_gw_down", sv['act'], dx2)
    tc = min(128, l)
    ntc = l // tc

    def conv_b(i, upt, up_prev, up_next, da, da_next, cw, cb):
        up_prev = jnp.where(i > 0, up_prev, 0.0)
        da_next = jnp.where(i < ntc - 1, da_next, 0.0)
        ext = jnp.concatenate([up_prev, upt, up_next], axis=0)
        x0 = ext[8:]
        xm1 = pltpu.roll(ext, 1, 0)[8:]
        xm2 = pltpu.roll(ext, 2, 0)[8:]
        upc = cw[0:1] * xm2 + cw[1:2] * xm1 + cw[2:3] * x0 + cb
        _, vjp = jax.vjp(lambda a, b: _silu(a) * b, upc[:, :D_FF], upc[:, D_FF:])
        dg, dv = vjp(jnp.concatenate([da, da_next], axis=0))
        dupc = jnp.concatenate([dg, dv], axis=-1)
        n = dupc.shape[0]
        dup = cw[2:3] * dupc[:tc] + cw[1:2] * pltpu.roll(dupc, n - 1, 0)[:tc] + cw[0:1] * pltpu.roll(dupc, n - 2, 0)[:tc]
        dt = dupc[:tc]
        dcw = _row_select([_colsum(dt * xm2[:tc]), _colsum(dt * xm1[:tc]), _colsum(dt * upt)], 8)
        return dup, dcw, _colsum(dt)

    dup, g_cw, g_cb = _rows(
        conv_b, name=name + "_b_conv", n=ntc, ins=[sv['up'], sv['up'], sv['up'], dact, dact, p['conv_w'], p['conv_b']],
        in_specs=[_rt(tc, 2 * D_FF), _halo_prev(tc, 2 * D_FF), _halo_next(tc, 2 * D_FF, ntc), _rt(tc, D_FF),
                  _halo_next(tc, D_FF, ntc), _full((3, 2 * D_FF)), _full((1, 2 * D_FF))],
        outs=[_sds((l, 2 * D_FF)), _sds((8, 2 * D_FF)), _sds((1, 2 * D_FF))],
        out_specs=[_rt(tc, 2 * D_FF), _full((8, 2 * D_FF)), _full((1, 2 * D_FF))], n_acc=2, vmem=56)
    g['conv_w'] = g_cw[:3]
    g['conv_b'] = g_cb
    dh2 = _matmul(name + "_b_up", [(dup, p['w_up'])], l, D_MODEL, nt=True, tm=256)
    g['w_up'] = _matmul_tn(name + "_gw_up", x1, dup, rms_gain=p['norm_ffn_g'])

    def norm_b(i, xv, dh, dres, gn):
        _, vjp = jax.vjp(lambda a, b: _rms(a, b, D_MODEL), xv, gn)
        dxv, dgn = vjp(dh)
        return dres + dxv, dgn

    dx1, g['norm_ffn_g'] = _rows(norm_b, name=name + "_b_norm2", n=nt, ins=[x1, dh2, dx2, p['norm_ffn_g']],
                                 in_specs=[_rt(tm, D_MODEL)] * 3 + [_full((1, D_MODEL))],
                                 outs=[_sds((l, D_MODEL)), _sds((1, D_MODEL))], out_specs=[_rt(tm, D_MODEL), _full((1, D_MODEL))],
                                 n_acc=1)

    tg = min(256, l)

    def merge_b(i, dx, oa, zo, oc, pg, woa, wos, woc, bg, wout):
        dm = _mm_nt(dx, wout)
        gates = jax.nn.sigmoid(pg + bg)
        ys = [_mm(oa, woa), _mm(zo, wos), _mm(oc, woc)]
        dys, dpg = [], []
        for b in range(3):
            gb = gates[:, b * D_MODEL:(b + 1) * D_MODEL]
            dys.append(dm * gb)
            dpg.append(dm * ys[b] * gb * (1.0 - gb))
        dpg = jnp.concatenate(dpg, axis=-1)
        return (_mm_nt(dys[0], woa), _mm_nt(dys[1], wos), _mm_nt(dys[2], woc), dpg, dys[0], dys[1], dys[2], _colsum(dpg))

    merge_consts = [p['w_oa'], p['w_o_ssm'], p['w_o_cross'], p['b_gate'], p['w_out']]
    (do_a, dzo, do_c, dp_g, dy_a, dy_b, dy_c, g['b_gate']) = _rows(
        merge_b, name=name + "_b_merge", n=l // tg, ins=[dx1, sv['o_a'], sv['zo'], sv['o_c'], sv['p_g'], *merge_consts],
        in_specs=[_rt(tg, D_MODEL), _rt(tg, MLA_PAD), _rt(tg, SSM_WIDTH), _rt(tg, X_WIDTH), _rt(tg, 3 * D_MODEL)]
        + [_full(a.shape) for a in merge_consts],
        outs=[_sds((l, MLA_PAD)), _sds((l, SSM_WIDTH)), _sds((l, X_WIDTH)), _sds((l, 3 * D_MODEL)),
              _sds((l, D_MODEL), BF16), _sds((l, D_MODEL), BF16), _sds((l, D_MODEL), BF16), _sds((1, 3 * D_MODEL))],
        out_specs=[_rt(tg, MLA_PAD), _rt(tg, SSM_WIDTH), _rt(tg, X_WIDTH), _rt(tg, 3 * D_MODEL),
                   _rt(tg, D_MODEL), _rt(tg, D_MODEL), _rt(tg, D_MODEL), _full((1, 3 * D_MODEL))], n_acc=1, vmem=56)
    g['w_out'] = _matmul_tn(name + "_gw_out", sv['merged'], dx1)
    g['w_oa'] = _matmul_tn(name + "_gw_oa", sv['o_a'], dy_a)
    g['w_o_ssm'] = _matmul_tn(name + "_gw_os", sv['zo'], dy_b)
    g['w_o_cross'] = _matmul_tn(name + "_gw_oc", sv['o_c'], dy_c)

    k_c, v_c = sv['k_c'], sv['v_c']
    m_len = k_c.shape[0]

    def cross_b(i, xq, do, kc, vc, xqg):
        dxq, dk, dv = [], [], []
        dg = jnp.zeros((1, LANE), F32)
        for h in range(X_HEADS):
            sl = slice(h * LANE, (h + 1) * LANE)
            qh, vjp = jax.vjp(lambda a, b: _rms(a, b, X_HEAD_DIM), xq[:, sl], xqg)
            sc = X_HEAD_DIM ** -0.5
            s = _mm_nt(qh, kc[:, sl]) * sc
            s = s - jnp.max(s, axis=-1, keepdims=True)
            e = jnp.exp(s)
            pr = e / jnp.sum(e, axis=-1, keepdims=True)
            doh = do[:, sl]
            dv.append(_mm_tn(pr, doh))
            dp = _mm_nt(doh, vc[:, sl])
            ds = pr * (dp - jnp.sum(dp * pr, axis=-1, keepdims=True)) * sc
            dk.append(_mm_tn(ds, qh))
            dxh, dgh = vjp(_mm(ds, kc[:, sl]))
            dxq.append(dxh)
            dg = dg + dgh
        return jnp.concatenate(dxq, axis=-1), jnp.concatenate(dk, axis=-1), jnp.concatenate(dv, axis=-1), dg

    dp_xq, dk_c, dv_c, g['xq_norm_g'] = _rows(
        cross_b, name=name + "_b_cross", n=nt, ins=[sv['p_xq'], do_c, k_c, v_c, p['xq_norm_g']],
        in_specs=[_rt(tm, X_WIDTH), _rt(tm, X_WIDTH), _full(k_c.shape), _full(v_c.shape), _full((1, LANE))],
        outs=[_sds((l, X_WIDTH)), _sds((m_len, X_WIDTH)), _sds((m_len, X_WIDTH)), _sds((1, LANE))],
        out_specs=[_rt(tm, X_WIDTH), _full((m_len, X_WIDTH)), _full((m_len, X_WIDTH)), _full((1, LANE))], n_acc=3)

    def memkv_b(i, mm_, dk, dv, mg, wmk, xkg):
        memn, vjp_n = jax.vjp(lambda a, b: _rms(a, b, D_MODEL), mm_, mg)
        kv = _mm(memn, wmk)
        _, vjp_k = jax.vjp(lambda a, b: _head_rms(a, b, X_HEADS, X_HEAD_DIM), kv[:, :X_WIDTH], xkg)
        dkr, dxkg = vjp_k(dk)
        dkv = jnp.concatenate([dkr, dv], axis=-1)
        _, dmg = vjp_n(_mm_nt(dkv, wmk))
        return _mm_tn(memn, dkv), dmg, dxkg

    mem_consts = [p['mem_norm_g'], p['w_mem_kv'], p['xk_norm_g']]
    g['w_mem_kv'], g['mem_norm_g'], g['xk_norm_g'] = _rows(
        memkv_b, name=name + "_b_memkv", n=1, ins=[mem, dk_c, dv_c, *mem_consts],
        in_specs=[_full(mem.shape), _full(dk_c.shape), _full(dv_c.shape)] + [_full(a.shape) for a in mem_consts],
        outs=[_sds((D_MODEL, 2 * X_WIDTH)), _sds((1, D_MODEL)), _sds((1, LANE))],
        out_specs=[_full((D_MODEL, 2 * X_WIDTH)), _full((1, D_MODEL)), _full((1, LANE))])

    u_p = sv['u_p']
    dzo_p = _to_perm(dzo, l)
    s_re, s_im = sv['s_re'], sv['s_im']

    def glu_b(i, sr, si, u, dz, cm, dsk, wg, bg):
        cats = [jnp.concatenate([sr[:, j * 512:(j + 1) * 512], si[:, j * 512:(j + 1) * 512]], axis=-1) for j in range(SSM_JB)]
        y = jnp.concatenate([_mm(cats[j], cm[j]) for j in range(SSM_JB)], axis=-1) + dsk * u
        zz, vjp_g = jax.vjp(_gelu, y)
        t = _mm(zz, wg) + bg
        sg = jax.nn.sigmoid(t)
        dt = dz * zz * sg * (1.0 - sg)
        dzz = dz * sg + _mm_nt(dt, wg)
        dy = vjp_g(dzz)[0]
        dss = [_mm_nt(dy[:, j * LANE:(j + 1) * LANE], cm[j]) for j in range(SSM_JB)]
        dsr = jnp.concatenate([d[:, :512] for d in dss], axis=-1)
        dsi = jnp.concatenate([d[:, 512:] for d in dss], axis=-1)
        dcm = jnp.stack([_mm_tn(cats[j], dy[:, j * LANE:(j + 1) * LANE]) for j in range(SSM_JB)], axis=0)
        return dsr, dsi, dy * dsk, dcm, _colsum(dy * u), _mm_tn(zz, dt), _colsum(dt)

    glu_consts = [p['c_mat'], p['ssm_d'], p['w_glu'], p['b_glu']]
    ts = min(256, l)
    nts = l // ts
    ds_re, ds_im, du_dir, g['c_mat'], g['ssm_d'], g['w_glu'], g['b_glu'] = _rows(
        glu_b, name=name + "_b_glu", n=nts, ins=[s_re, s_im, u_p, dzo_p, *glu_consts],
        in_specs=[_rt(ts, SSM_LANES), _rt(ts, SSM_LANES), _rt(ts, SSM_WIDTH), _rt(ts, SSM_WIDTH)] + [_full(a.shape) for a in glu_consts],
        outs=[_sds((l, SSM_LANES), SSM_STATE_DTYPE), _sds((l, SSM_LANES), SSM_STATE_DTYPE), _sds((l, SSM_WIDTH)),
              _sds((SSM_JB, 1024, LANE)), _sds((1, SSM_WIDTH)),
              _sds((SSM_WIDTH, SSM_WIDTH)), _sds((1, SSM_WIDTH))],
        out_specs=[_rt(ts, SSM_LANES), _rt(ts, SSM_LANES), _rt(ts, SSM_WIDTH), _full((SSM_JB, 1024, LANE)), _full((1, SSM_WIDTH)),
                   _full((SSM_WIDTH, SSM_WIDTH)), _full((1, SSM_WIDTH))], n_acc=4)
    gb_re, gb_im = _scan(name + "_b_scan", ds_re, ds_im, sv['a_re'], -sv['a_im'], reverse=True)
    ns = SCAN_SEGS
    last_blk = l // ns - 1

    def da_fn(i, *vals):
        gr, gi, sr, si, hr, hi, lr_, li_ = [v.astype(F32) for v in vals]
        rid = lax.broadcasted_iota(jnp.int32, lr_.shape, 0)
        fr = jnp.where(rid == 0, 0.0, pltpu.roll(lr_, 1, 0))
        fi = jnp.where(rid == 0, 0.0, pltpu.roll(li_, 1, 0))
        hr = jnp.where(i == 0, fr, hr)
        hi = jnp.where(i == 0, fi, hi)
        if ts > ns:
            pr = jnp.concatenate([hr, sr[:ts - ns]], axis=0)
            pi = jnp.concatenate([hi, si[:ts - ns]], axis=0)
        else:
            pr, pi = hr, hi
        return _colsum(gr * pr + gi * pi), _colsum(gi * pr - gr * pi)

    hprev = pl.BlockSpec((ns, SSM_LANES), lambda i: (jnp.maximum(i * (ts // ns) - 1, 0), 0))
    hlast = pl.BlockSpec((ns, SSM_LANES), lambda i: (last_blk, 0))
    da_re, da_im = _rows(da_fn, name=name + "_b_da", n=nts, ins=[gb_re, gb_im, s_re, s_im, s_re, s_im, s_re, s_im],
                         in_specs=[_rt(ts, SSM_LANES)] * 4 + [hprev, hprev, hlast, hlast],
                         outs=[_sds((1, SSM_LANES))] * 2, out_specs=[_full((1, SSM_LANES))] * 2, n_acc=2)

    def bu_b(i, dbr, dbi, u, dud, bm):
        dus, dbm = [], []
        for j in range(SSM_JB):
            cat = jnp.concatenate([dbr[:, j * 512:(j + 1) * 512], dbi[:, j * 512:(j + 1) * 512]], axis=-1)
            dus.append(_mm_nt(cat, bm[j]))
            dbm.append(_mm_tn(u[:, j * LANE:(j + 1) * LANE], cat))
        return dud + jnp.concatenate(dus, axis=-1), jnp.stack(dbm, axis=0)

    du_p, d_bmat = _rows(bu_b, name=name + "_b_bu", n=nts, ins=[gb_re, gb_im, u_p, du_dir, sv['b_mat']],
                         in_specs=[_rt(ts, SSM_LANES), _rt(ts, SSM_LANES), _rt(ts, SSM_WIDTH), _rt(ts, SSM_WIDTH),
                                   _full(sv['b_mat'].shape)],
                         outs=[_sds((l, SSM_WIDTH)), _sds((SSM_JB, LANE, 1024))],
                         out_specs=[_rt(ts, SSM_WIDTH), _full((SSM_JB, LANE, 1024))], n_acc=1)
    dp_u = _from_perm(du_p, l)
    dbb_re = _blockdiag_t(d_bmat[:, :, :512], SSM_GROUP_CH, SSM_STATE).reshape(SSM_GROUPS, SSM_GROUP_CH, SSM_STATE).transpose(1, 0, 2)
    dbb_im = _blockdiag_t(d_bmat[:, :, 512:], SSM_GROUP_CH, SSM_STATE).reshape(SSM_GROUPS, SSM_GROUP_CH, SSM_STATE).transpose(1, 0, 2)
    g['lr'], g['li'], g['log_dt'], g['br'], g['bi'] = _ssm_params_bwd(
        name + "_b_ssm_par", p['lr'], p['li'], p['log_dt'], p['br'], p['bi'],
        da_re.reshape(SSM_GROUPS, SSM_STATE), da_im.reshape(SSM_GROUPS, SSM_STATE), dbb_re, dbb_im)

    dq_t, dk, dv, rode = _flash_bwd(name + "_b_attn", sv['q'], sv['k'], sv['v'], sv['k_t'], sv['o_a'], sv['lse_t'], do_a,
                                    reduce=reduce)

    def qkv_b(i, ps, c, s1, s2, dq_, dk_, dv_, qag, wqb, kvag, wk, wv, qng, kng):
        c_q = ps[:, :Q_LORA]
        c_kv = ps[:, Q_LORA:Q_LORA + KV_LORA]
        kr = ps[:, Q_LORA + KV_LORA:]
        cqn, vjp_cq = jax.vjp(lambda a, b: _rms(a, b, Q_LORA), c_q, qag)
        ckvn, vjp_ckv = jax.vjp(lambda a, b: _rms(a, b, KV_LORA), c_kv, kvag)
        q_raw = _mm(cqn, wqb)
        k_raw = _mm(ckvn, wk) + jnp.concatenate([kr] * MLA_HEADS, axis=-1)
        _, vjp_qn = jax.vjp(lambda a, b: _head_rms(a, b, MLA_HEADS, D_QK), q_raw, qng)
        _, vjp_kn = jax.vjp(lambda a, b: _head_rms(a, b, MLA_HEADS, D_QK), k_raw, kng)
        dq_raw, dqng = vjp_qn(_heads(_rope_t, jnp.transpose(dq_[0]), MLA_HEADS, c, s1, s2))
        dk_raw, dkng = vjp_kn(_heads(_rope_t, dk_, MLA_HEADS, c, s1, s2))
        dkr = dk_raw[:, :LANE]
        for h in range(1, MLA_HEADS):
            dkr = dkr + dk_raw[:, h * LANE:(h + 1) * LANE]
        dcq, dqag = vjp_cq(_mm_nt(dq_raw, wqb))
        dckv, dkvag = vjp_ckv(_mm_nt(dk_raw, wk) + _mm_nt(dv_, wv))
        dps = jnp.concatenate([dcq, dckv, dkr], axis=-1)
        return (dps, _mm_tn(cqn, dq_raw), _mm_tn(ckvn, dk_raw), _mm_tn(ckvn, dv_), dqag, dkvag, dqng, dkng)

    qkv_consts = [p['q_a_norm_g'], p['w_qb'], p['kv_a_norm_g'], p['w_k'], p['w_v'], p['q_norm_g'], p['k_norm_g']]
    (dp_s, g['w_qb'], g['w_k'], g['w_v'], g['q_a_norm_g'], g['kv_a_norm_g'], g['q_norm_g'], g['k_norm_g']) = _rows(
        qkv_b, name=name + "_b_qkv", n=nt, ins=[sv['p_s'], *tabs, dq_t, dk, dv, *qkv_consts],
        in_specs=[_rt(tm, SMALL_W)] + [_rt(tm, LANE)] * 3
        + [pl.BlockSpec((1, MLA_PAD, tm), lambda i: (i // (dq_t.shape[2] // tm), 0, i % (dq_t.shape[2] // tm)))]
        + [_rt(tm, MLA_PAD)] * 2 + [_full(a.shape) for a in qkv_consts],
        outs=[_sds((l, SMALL_W)), _sds((Q_LORA, MLA_PAD)), _sds((KV_LORA, MLA_PAD)), _sds((KV_LORA, MLA_PAD)),
              _sds((1, Q_LORA)), _sds((1, KV_LORA)), _sds((1, LANE)), _sds((1, LANE))],
        out_specs=[_rt(tm, SMALL_W), _full((Q_LORA, MLA_PAD)), _full((KV_LORA, MLA_PAD)), _full((KV_LORA, MLA_PAD)),
                   _full((1, Q_LORA)), _full((1, KV_LORA)), _full((1, LANE)), _full((1, LANE))], n_acc=7)

    x0 = sv['x0']
    dh = _matmul(name + "_b_in", [(dp_g, p['w_g']), (dp_u, p['w_u']), (dp_xq, p['w_xq']), (dp_s, p['w_s'])], l, D_MODEL, nt=True,
                 tm=256)
    gm = p['norm_mix_g']
    g['w_g'] = _matmul_tn(name + "_gw_g", x0, dp_g, rms_gain=gm)
    g['w_u'] = _matmul_tn(name + "_gw_u", x0, dp_u, rms_gain=gm)
    g['w_xq'] = _matmul_tn(name + "_gw_xq", x0, dp_xq, rms_gain=gm)
    g['w_s'] = _matmul_tn(name + "_gw_s", x0, dp_s, rms_gain=gm)
    dx0, g['norm_mix_g'] = _rows(norm_b, name=name + "_b_norm1", n=nt, ins=[x0, dh, dx1, gm],
                                 in_specs=[_rt(tm, D_MODEL)] * 3 + [_full((1, D_MODEL))],
                                 outs=[_sds((l, D_MODEL)), _sds((1, D_MODEL))], out_specs=[_rt(tm, D_MODEL), _full((1, D_MODEL))],
                                 n_acc=1)
    return dx0, g, rode


def _unprep_grads(g):
    o = {}
    ws = g['w_s']
    o['w_in'] = jnp.concatenate([ws[:, :Q_LORA + KV_LORA], ws[:, Q_LORA + KV_LORA + D_NOPE:Q_LORA + KV_LORA + D_QK],
                                 g['w_u'], g['w_xq'], g['w_g']], axis=1)
    o['w_q_b'] = g['w_qb'].reshape(Q_LORA, MLA_HEADS, HEAD_PAD)[:, :, :D_QK].reshape(Q_LORA, MLA_HEADS * D_QK)
    gk = g['w_k'].reshape(KV_LORA, MLA_HEADS, HEAD_PAD)[:, :, :D_NOPE]
    gv = g['w_v'].reshape(KV_LORA, MLA_HEADS, HEAD_PAD)[:, :, :D_V]
    o['w_kv_b'] = jnp.concatenate([gk, gv], axis=2).reshape(KV_LORA, MLA_HEADS * (D_NOPE + D_V))
    o['w_o_mla'] = g['w_oa'].reshape(MLA_HEADS, HEAD_PAD, D_MODEL)[:, :D_V].reshape(MLA_HEADS * D_V, D_MODEL)
    for n in ('w_glu', 'w_o_ssm', 'w_mem_kv', 'w_o_cross', 'w_out', 'w_up', 'w_down', 'conv_w'):
        o[n] = g[n]
    for n in ('norm_mix_g', 'q_a_norm_g', 'kv_a_norm_g', 'b_glu', 'mem_norm_g', 'xq_norm_g', 'xk_norm_g', 'b_gate',
              'norm_ffn_g', 'conv_b'):
        o[n] = g[n].reshape(-1)
    o['q_norm_g'] = g['q_norm_g'].reshape(-1)[:D_QK]
    o['k_norm_g'] = g['k_norm_g'].reshape(-1)[:D_QK]
    o['ssm_d'] = g['ssm_d'].reshape(SSM_GROUPS, SSM_GROUP_CH)
    o['ssm_lambda_re'] = g['lr']
    o['ssm_lambda_im'] = g['li']
    o['ssm_log_dt'] = g['log_dt'].reshape(SSM_GROUPS)
    o['ssm_b_re'] = g['br'].transpose(1, 2, 0)
    o['ssm_b_im'] = g['bi'].transpose(1, 2, 0)
    dc = g['c_mat']
    o['ssm_c_re'] = _blockdiag_t(dc[:, :512], SSM_STATE, SSM_GROUP_CH).transpose(0, 1, 3, 2).reshape(SSM_GROUPS, SSM_GROUP_CH, SSM_STATE)
    o['ssm_c_im'] = -_blockdiag_t(dc[:, 512:], SSM_STATE, SSM_GROUP_CH).transpose(0, 1, 3, 2).reshape(SSM_GROUPS, SSM_GROUP_CH, SSM_STATE)
    return o


def _local_step(x, mem, pos, target, w, late_gather=None, early_reduce=None):
    l = x.shape[0]
    tm = min(512, l)
    tabs = _rope_tables(pos.astype(F32).reshape(l, 1))
    saved = []
    ps = []
    h = x
    for i in range(DEPTH):
        ps.append(_prep_layer(w, i))
        riding = late_gather[:2] if (late_gather is not None and i == 0) else None
        h, sv = _layer_fwd("l%d" % i, h, tabs, mem, ps[i], gather=riding)
        if riding is not None:
            for n, v in late_gather[2](sv.pop('gathered')).items():
                w[n][late_gather[1]] = v
        saved.append(sv)

    def loss_fn(i, y, t):
        e = y - t
        per_tok = jnp.sum(e * e, axis=-1, keepdims=True) * (1.0 / D_MODEL)
        tot = 0.5 * jnp.sum(per_tok, axis=0, keepdims=True)
        return e * (1.0 / D_MODEL), jnp.broadcast_to(tot, (1, LANE))

    dy, loss = _rows(loss_fn, name="loss", n=l // tm, ins=[h, target], in_specs=[_rt(tm, D_MODEL)] * 2,
                     outs=[_sds((l, D_MODEL)), _sds((1, LANE))], out_specs=[_rt(tm, D_MODEL), _full((1, LANE))], n_acc=1)
    grads = []
    d = dy
    riding, rode = None, []
    for i in reversed(range(DEPTH)):
        d, g, got = _layer_bwd("l%d" % i, d, saved[i], tabs, mem, ps[i], reduce=riding)
        rode = got or rode
        grads.append(_unprep_grads(g))
        riding = (early_reduce(grads[-1]), DEPTH - 1) if (early_reduce is not None and i == DEPTH - 1) else None
    return loss[0, 0], d, grads[::-1], rode


def _sum_picked(name, slots, pick, extra, out_dtype):
    _, r, c = slots.shape
    e = extra.shape[0]
    tr = _row_tile(r)

    def body(pk, s_ref, x_ref, o_ref):
        acc = s_ref[...].astype(F32)
        for k in range(e):
            acc = acc + x_ref[k].astype(F32)
        o_ref[...] = acc.astype(o_ref.dtype)

    grid_spec = pltpu.PrefetchScalarGridSpec(
        num_scalar_prefetch=1, grid=(r // tr,),
        in_specs=[pl.BlockSpec((None, tr, c), lambda i, pk: (pk[0], i, 0)), pl.BlockSpec((e, tr, c), lambda i, pk: (0, i, 0))],
        out_specs=pl.BlockSpec((tr, c), lambda i, pk: (i, 0)))
    return pl.pallas_call(body, name=name, grid_spec=grid_spec, out_shape=_sds((r, c), out_dtype),
                          compiler_params=_params(("arbitrary",), 48))(pick, slots, extra)


def _row_tile(r):
    for t in (256, 128, 64, 32, 16, 8):
        if r % t == 0:
            return t
    return r


def _adamw(name, parts, w, m, v):
    r, cw = w.shape
    tr = _row_tile(r)
    np_ = len(parts)

    def fn(i, *vals):
        wv, mv, vv = vals[np_:]
        terms = []
        for pv in vals[:np_]:
            terms += [pv] if pv.ndim == 2 else [pv[k] for k in range(pv.shape[0])]
        g = terms[0]
        for t in terms[1:]:
            g = g + t
        mn = ADAM_B1 * mv + (1.0 - ADAM_B1) * g
        vn = ADAM_B2 * vv + (1.0 - ADAM_B2) * (g * g)
        m_hat = mn / (1.0 - ADAM_B1 ** ADAM_STEP)
        v_hat = vn / (1.0 - ADAM_B2 ** ADAM_STEP)
        delta = -ADAM_LR * (m_hat / (jnp.sqrt(v_hat) + ADAM_EPS) + ADAM_WD * wv)
        return g, delta, mn, vn

    pspecs = [_rt(tr, cw) if p.ndim == 2 else pl.BlockSpec((p.shape[0], tr, cw), lambda i: (0, i, 0)) for p in parts]
    return _rows(fn, name=name, n=r // tr, ins=[*parts, w, m, v], in_specs=pspecs + [_rt(tr, cw)] * 3,
                 outs=[_sds((r, cw))] * 4, out_specs=[_rt(tr, cw)] * 4)


def _shard_of(a, axis, k):
    n = a.shape[axis] // 4
    return lax.slice_in_dim(a, k * n, (k + 1) * n, axis=axis)


def _step(a):
    x = a['x'][0]
    mem = a['mem'][0]
    pos = a['positions'][0]
    target = a['loss_target'][0]

    me = 2 * lax.axis_index("x") + lax.axis_index("y")

    mine = [a[n] if n == 'conv_w' else a[n].astype(BF16) for n in SHARDED]

    def assemble(bufs, layer):
        return {n: jnp.concatenate([jnp.where(me == k, own[layer], y[k]) for k in range(4)], axis=SHARD_AXIS[n] - 1)
                for n, own, y in zip(SHARDED, mine, bufs)}

    w = {n: [v, None] for n, v in assemble(_layer_gather("comm_gather_l0", mine, 0), 0).items()}
    for n in SMALL:
        w[n] = a[n]

    mc = lax.axis_index("c")
    me1 = me.astype(jnp.int32).reshape(1)
    zero1i = jnp.zeros((1,), jnp.int32)

    def pair_sums(layer, gl):
        mine_l = [jnp.stack([_shard_of(gl[n], SHARD_AXIS[n] - 1, k) for k in range(4)], axis=0).astype(BF16) for n in SHARDED]
        per_core = [None, None]
        per_core[1 - layer] = mine_l
        theirs = _send_d2d("comm_pair_l%d" % layer, per_core)
        out = []
        for n, g, s in zip(SHARDED, mine_l, theirs):
            cols = g.shape[-1]
            out.append(_sum_picked("sum2_l%d_%s" % (layer, n), g.reshape(1, -1, cols), zero1i, s.reshape(1, -1, cols), BF16)
                       .reshape(g.shape))
        return out

    pairs = [None, None]

    def early_reduce(gl):
        pairs[1] = pair_sums(1, gl)
        return pairs[1]

    loss, grad_x, grads, got1 = _local_step(x, mem, pos, target, w, late_gather=(mine, 1, lambda bufs: assemble(bufs, 1)),
                                            early_reduce=early_reduce)
    pairs[0] = pair_sums(0, grads[0])
    gsm = _pack([jnp.stack([grads[i][n] for i in range(DEPTH)], axis=0) for n in SMALL] + [loss.reshape(1)], 8, F32)
    got0, alls = _layer_reduce("comm_reduce_l0", pairs[0], 0, gsm)
    done = [[_sum_picked("sum4_l%d_%s" % (layer, n), p4, me1, g3, F32) for n, p4, g3 in zip(SHARDED, pairs[layer], got)]
            for layer, got in ((0, got0), (1, got1))]
    others = _send_d2d("comm_reduce_d2d", done)
    res_sh = []
    for n, d0, d1, other in zip(SHARDED, done[0], done[1], others):
        cols = d0.shape[-1]
        full = jnp.where(mc == 0, jnp.stack([d0, other], axis=0), jnp.stack([other, d1], axis=0))
        res = _adamw("adamw_" + n, [full.reshape(-1, cols)], *[a[pre + n].reshape(-1, cols) for pre in ('', 'm_', 'v_')])
        res_sh.append([r.reshape(a[n].shape) for r in res])
    res_sh = [[res_sh[j][kind] for j in range(len(SHARDED))] for kind in range(4)]

    sm_shapes = [a[n].shape for n in SMALL] + [(1,)]
    zero1 = jnp.zeros((1,), F32)
    res_sm = _adamw("adamw_small", [alls], *[_pack([a[pre + n] for n in SMALL] + [zero1], 8, F32) for pre in ('', 'm_', 'v_')])
    res_sm = [_unpack(r, sm_shapes) for r in res_sm]
    loss = res_sm[0][-1][0]

    outs = [loss, grad_x[None]]
    for kind in range(4):
        byname = dict(zip(SHARDED, res_sh[kind]))
        byname.update(zip(SMALL, res_sm[kind]))
        outs += [byname[n] for n in WEIGHTS]
    return tuple(outs)


def kernel(x, mem, positions, norm_mix_g, w_in, q_a_norm_g, w_q_b, kv_a_norm_g, w_kv_b, q_norm_g, k_norm_g, w_o_mla, ssm_lambda_re, ssm_lambda_im, ssm_log_dt, ssm_b_re, ssm_b_im, ssm_c_re, ssm_c_im, ssm_d, w_glu, b_glu, w_o_ssm, mem_norm_g, w_mem_kv, xq_norm_g, xk_norm_g, w_o_cross, b_gate, w_out, norm_ffn_g, w_up, conv_w, conv_b, w_down, loss_target, m_norm_mix_g, m_w_in, m_q_a_norm_g, m_w_q_b, m_kv_a_norm_g, m_w_kv_b, m_q_norm_g, m_k_norm_g, m_w_o_mla, m_ssm_lambda_re, m_ssm_lambda_im, m_ssm_log_dt, m_ssm_b_re, m_ssm_b_im, m_ssm_c_re, m_ssm_c_im, m_ssm_d, m_w_glu, m_b_glu, m_w_o_ssm, m_mem_norm_g, m_w_mem_kv, m_xq_norm_g, m_xk_norm_g, m_w_o_cross, m_b_gate, m_w_out, m_norm_ffn_g, m_w_up, m_conv_w, m_conv_b, m_w_down, v_norm_mix_g, v_w_in, v_q_a_norm_g, v_w_q_b, v_kv_a_norm_g, v_w_kv_b, v_q_norm_g, v_k_norm_g, v_w_o_mla, v_ssm_lambda_re, v_ssm_lambda_im, v_ssm_log_dt, v_ssm_b_re, v_ssm_b_im, v_ssm_c_re, v_ssm_c_im, v_ssm_d, v_w_glu, v_b_glu, v_w_o_ssm, v_mem_norm_g, v_w_mem_kv, v_xq_norm_g, v_xk_norm_g, v_w_o_cross, v_b_gate, v_w_out, v_norm_ffn_g, v_w_up, v_conv_w, v_conv_b, v_w_down):
    return _step(dict(locals()))
```
